```python
import jax, jax.numpy as jnp
from jax import lax
import numpy as np

D_MODEL = 1024
BATCH = 8
SEQ = 4096
DEPTH = 2

N_META = 16
N_MIXERS = 2
RMS_EPS = 1e-6

MLA_HEADS = 8
QK_NOPE = 128
QK_ROPE = 64
V_HEAD = 128
Q_LORA = 384
KV_LORA = 256
MLA_WIDTH = MLA_HEADS * V_HEAD
ROPE_BASE = 10000.0
Q_BLOCK = 128
MASK_VALUE = -1e30

LRU_WIDTH = 1024
LRU_BLOCKS = 4
LRU_BLOCK = LRU_WIDTH // LRU_BLOCKS
CONV_WIDTH = 4
LRU_C = 8.0

kernel_name = 'mla_rglru_interleaved_hybrid'


def rmsnorm(x, g):
    xf = x.astype(jnp.float32)
    y = xf * lax.rsqrt(jnp.mean(xf * xf, axis=-1, keepdims=True) + RMS_EPS)
    return (y * g.astype(jnp.float32)).astype(x.dtype)


def rotate_half_split(x, cos, sin):
    x1, x2 = jnp.split(x, 2, axis=-1)
    return jnp.concatenate([x1 * cos - x2 * sin, x1 * sin + x2 * cos], axis=-1).astype(x.dtype)


def block_causal_attention(q_nope, q_rope, k_nope, k_rope, v):
    B, T, H, _ = q_nope.shape
    pad = (-T) % Q_BLOCK
    Tp = T + pad
    nb = Tp // Q_BLOCK

    def padt(a):
        return jnp.pad(a, [(0, 0), (pad, 0)] + [(0, 0)] * (a.ndim - 2))

    q_nope, q_rope, k_nope, k_rope, v = (padt(a) for a in (q_nope, q_rope, k_nope, k_rope, v))
    scale = (QK_NOPE + QK_ROPE) ** -0.5
    key_idx = jnp.arange(Tp)

    def to_blocks(a):
        return jnp.moveaxis(a.reshape(B, nb, Q_BLOCK, *a.shape[2:]), 1, 0)

    def one_block(args):
        blk, qn, qr = args
        s = (jnp.einsum('bqhd,bkhd->bhqk', qn, k_nope, preferred_element_type=jnp.float32)
             + jnp.einsum('bqhr,bkr->bhqk', qr, k_rope, preferred_element_type=jnp.float32)) * scale
        q_idx = blk * Q_BLOCK + jnp.arange(Q_BLOCK)
        mask = (key_idx[None, :] <= q_idx[:, None]) & (key_idx[None, :] >= pad)
        s = jnp.where(mask[None, None], s, MASK_VALUE)
        p = jax.nn.softmax(s, axis=-1)
        return jnp.einsum('bhqk,bkhd->bqhd', p.astype(v.dtype), v)

    out = lax.map(one_block, (jnp.arange(nb), to_blocks(q_nope), to_blocks(q_rope)))
    out = jnp.moveaxis(out, 0, 1).reshape(B, Tp, H, V_HEAD)
    return out[:, pad:]


def mla_mixer(h, w_in, q_norm_g, kv_norm_g, w_uq, w_ukv, w_out):
    B, T, _ = h.shape
    proj = h @ w_in
    q_lat, kv_lat, k_rope, gate = jnp.split(
        proj, [Q_LORA, Q_LORA + KV_LORA, Q_LORA + KV_LORA + QK_ROPE], axis=-1)
    q = (rmsnorm(q_lat, q_norm_g) @ w_uq).reshape(B, T, MLA_HEADS, QK_NOPE + QK_ROPE)
    q_nope, q_rope = q[..., :QK_NOPE], q[..., QK_NOPE:]
    kv = (rmsnorm(kv_lat, kv_norm_g) @ w_ukv).reshape(B, T, MLA_HEADS, QK_NOPE + V_HEAD)
    k_nope, v = kv[..., :QK_NOPE], kv[..., QK_NOPE:]
    pos = jnp.arange(T, dtype=jnp.float32)
    inv_freq = ROPE_BASE ** (-jnp.arange(0, QK_ROPE, 2, dtype=jnp.float32) / QK_ROPE)
    ang = pos[:, None] * inv_freq[None, :]
    cos, sin = jnp.cos(ang), jnp.sin(ang)
    q_rope = rotate_half_split(q_rope, cos[:, None, :], sin[:, None, :])
    k_rope = rotate_half_split(k_rope, cos, sin)
    attn = block_causal_attention(q_nope, q_rope, k_nope, k_rope, v)
    y = attn.reshape(B, T, MLA_WIDTH) * jax.nn.silu(gate)
    return y @ w_out


def rglru_mixer(h, w_in, conv_w, conv_b, w_rg, b_rg, w_ig, b_ig, lam, w_out):
    B, T, _ = h.shape
    proj = h @ w_in
    u, gate = jnp.split(proj, [LRU_WIDTH], axis=-1)
    up = jnp.pad(u, ((0, 0), (CONV_WIDTH - 1, 0), (0, 0)))
    uc = conv_b + up[:, 0:T] * conv_w[0]
    for j in range(1, CONV_WIDTH):
        uc = uc + up[:, j:j + T] * conv_w[j]
    ub = uc.reshape(B, T, LRU_BLOCKS, LRU_BLOCK)
    r = jax.nn.sigmoid(jnp.einsum('btgi,gij->btgj', ub, w_rg).reshape(B, T, LRU_WIDTH) + b_rg)
    i = jax.nn.sigmoid(jnp.einsum('btgi,gij->btgj', ub, w_ig).reshape(B, T, LRU_WIDTH) + b_ig)
    log_a = -LRU_C * r.astype(jnp.float32) * jax.nn.softplus(-lam.astype(jnp.float32))
    a = jnp.exp(log_a)
    mult = jnp.sqrt(-jnp.expm1(2.0 * log_a))
    mult = jnp.where(jnp.arange(T)[None, :, None] == 0, 1.0, mult)
    b = mult * (i * uc).astype(jnp.float32)

    def combine(left, right):
        a1, b1 = left
        a2, b2 = right
        return a1 * a2, a2 * b1 + b2

    _, hs = lax.associative_scan(combine, (a, b), axis=1)
    y = hs.astype(h.dtype) * jax.nn.silu(gate)
    return y @ w_out


def _fwd_setup_inputs(seed: int = 0) -> dict:
    key = jax.random.key(seed)
    ks = jax.random.split(key, 24)
    n_a = (DEPTH + 1) // 2
    n_b = DEPTH // 2
    d = D_MODEL
    f32 = jnp.float32

    def nrm(k, shape, fan_in):
        return jax.random.normal(k, shape, f32) * (fan_in ** -0.5)

    def gain(k, shape):
        return 1.0 + 0.01 * jax.random.normal(k, shape, f32)

    a_in_cols = Q_LORA + KV_LORA + QK_ROPE + MLA_WIDTH
    u0 = jax.random.uniform(ks[17], (n_b, LRU_WIDTH), f32, minval=0.9, maxval=0.999)
    s0 = u0 ** (1.0 / LRU_C)
    lam = jnp.log(s0) - jnp.log1p(-s0)
    return {
        'x': jax.random.normal(ks[0], (BATCH, SEQ, d), f32),
        'meta_tokens': jax.random.normal(ks[1], (N_META, d), f32),
        'a_norm_g': gain(ks[2], (n_a, d)),
        'a_w_in': nrm(ks[3], (n_a, d, a_in_cols), d),
        'a_q_norm_g': gain(ks[4], (n_a, Q_LORA)),
        'a_kv_norm_g': gain(ks[5], (n_a, KV_LORA)),
        'a_w_uq': nrm(ks[6], (n_a, Q_LORA, MLA_HEADS * (QK_NOPE + QK_ROPE)), Q_LORA),
        'a_w_ukv': nrm(ks[7], (n_a, KV_LORA, MLA_HEADS * (QK_NOPE + V_HEAD)), KV_LORA),
        'a_w_out': nrm(ks[8], (n_a, MLA_WIDTH, d), MLA_WIDTH),
        'b_norm_g': gain(ks[9], (n_b, d)),
        'b_w_in': nrm(ks[10], (n_b, d, 2 * LRU_WIDTH), d),
        'b_conv_w': nrm(ks[11], (n_b, CONV_WIDTH, LRU_WIDTH), CONV_WIDTH),
        'b_conv_b': 0.01 * jax.random.normal(ks[12], (n_b, LRU_WIDTH), f32),
        'b_w_rg': nrm(ks[13], (n_b, LRU_BLOCKS, LRU_BLOCK, LRU_BLOCK), LRU_BLOCK),
        'b_b_rg': 0.01 * jax.random.normal(ks[14], (n_b, LRU_WIDTH), f32),
        'b_w_ig': nrm(ks[15], (n_b, LRU_BLOCKS, LRU_BLOCK, LRU_BLOCK), LRU_BLOCK),
        'b_b_ig': 0.01 * jax.random.normal(ks[16], (n_b, LRU_WIDTH), f32),
        'b_lam': lam,
        'b_w_out': nrm(ks[18], (n_b, LRU_WIDTH, d), LRU_WIDTH),
        'final_norm_g': gain(ks[19], (d,)),
    }


def _fwd_reference(x, meta_tokens, a_norm_g, a_w_in, a_q_norm_g, a_kv_norm_g, a_w_uq, a_w_ukv,
              a_w_out, b_norm_g, b_w_in, b_conv_w, b_conv_b, b_w_rg, b_b_rg, b_w_ig, b_b_ig,
              b_lam, b_w_out, final_norm_g):
    B = x.shape[0]
    meta = jnp.broadcast_to(meta_tokens[None].astype(x.dtype), (B, N_META, x.shape[-1]))
    h = jnp.concatenate([meta, x], axis=1)
    for layer in range(DEPTH):
        j = layer // N_MIXERS
        if layer % N_MIXERS == 0:
            h = h + mla_mixer(rmsnorm(h, a_norm_g[j]), a_w_in[j], a_q_norm_g[j], a_kv_norm_g[j],
                              a_w_uq[j], a_w_ukv[j], a_w_out[j])
        else:
            h = h + rglru_mixer(rmsnorm(h, b_norm_g[j]), b_w_in[j], b_conv_w[j], b_conv_b[j],
                                b_w_rg[j], b_b_rg[j], b_w_ig[j], b_b_ig[j], b_lam[j], b_w_out[j])
    h = rmsnorm(h, final_norm_g)
    return h[:, N_META:]


import jax as _jax
import jax.numpy as _jnp

TWIN_FORMAT = 'train_step'
FWD_PARAMS = ['x', 'meta_tokens', 'a_norm_g', 'a_w_in', 'a_q_norm_g', 'a_kv_norm_g', 'a_w_uq', 'a_w_ukv', 'a_w_out', 'b_norm_g', 'b_w_in', 'b_conv_w', 'b_conv_b', 'b_w_rg', 'b_b_rg', 'b_w_ig', 'b_b_ig', 'b_lam', 'b_w_out', 'final_norm_g']
TWIN_WEIGHTS = ['meta_tokens', 'a_norm_g', 'a_w_in', 'a_q_norm_g', 'a_kv_norm_g', 'a_w_uq', 'a_w_ukv', 'a_w_out', 'b_norm_g', 'b_w_in', 'b_conv_w', 'b_conv_b', 'b_w_rg', 'b_b_rg', 'b_w_ig', 'b_b_ig', 'b_lam', 'b_w_out', 'final_norm_g']
TWIN_DIFF_INPUT = 'x'
TWIN_INPUTS = ['x', 'meta_tokens', 'a_norm_g', 'a_w_in', 'a_q_norm_g', 'a_kv_norm_g', 'a_w_uq', 'a_w_ukv', 'a_w_out', 'b_norm_g', 'b_w_in', 'b_conv_w', 'b_conv_b', 'b_w_rg', 'b_b_rg', 'b_w_ig', 'b_b_ig', 'b_lam', 'b_w_out', 'final_norm_g', 'loss_target', 'm_meta_tokens', 'm_a_norm_g', 'm_a_w_in', 'm_a_q_norm_g', 'm_a_kv_norm_g', 'm_a_w_uq', 'm_a_w_ukv', 'm_a_w_out', 'm_b_norm_g', 'm_b_w_in', 'm_b_conv_w', 'm_b_conv_b', 'm_b_w_rg', 'm_b_b_rg', 'm_b_w_ig', 'm_b_b_ig', 'm_b_lam', 'm_b_w_out', 'm_final_norm_g', 'v_meta_tokens', 'v_a_norm_g', 'v_a_w_in', 'v_a_q_norm_g', 'v_a_kv_norm_g', 'v_a_w_uq', 'v_a_w_ukv', 'v_a_w_out', 'v_b_norm_g', 'v_b_w_in', 'v_b_conv_w', 'v_b_conv_b', 'v_b_w_rg', 'v_b_b_rg', 'v_b_w_ig', 'v_b_b_ig', 'v_b_lam', 'v_b_w_out', 'v_final_norm_g']
TWIN_OUTPUTS = ['loss', 'grad_x', 'grad_meta_tokens', 'grad_a_norm_g', 'grad_a_w_in', 'grad_a_q_norm_g', 'grad_a_kv_norm_g', 'grad_a_w_uq', 'grad_a_w_ukv', 'grad_a_w_out', 'grad_b_norm_g', 'grad_b_w_in', 'grad_b_conv_w', 'grad_b_conv_b', 'grad_b_w_rg', 'grad_b_b_rg', 'grad_b_w_ig', 'grad_b_b_ig', 'grad_b_lam', 'grad_b_w_out', 'grad_final_norm_g', 'delta_meta_tokens', 'delta_a_norm_g', 'delta_a_w_in', 'delta_a_q_norm_g', 'delta_a_kv_norm_g', 'delta_a_w_uq', 'delta_a_w_ukv', 'delta_a_w_out', 'delta_b_norm_g', 'delta_b_w_in', 'delta_b_conv_w', 'delta_b_conv_b', 'delta_b_w_rg', 'delta_b_b_rg', 'delta_b_w_ig', 'delta_b_b_ig', 'delta_b_lam', 'delta_b_w_out', 'delta_final_norm_g', 'new_m_meta_tokens', 'new_m_a_norm_g', 'new_m_a_w_in', 'new_m_a_q_norm_g', 'new_m_a_kv_norm_g', 'new_m_a_w_uq', 'new_m_a_w_ukv', 'new_m_a_w_out', 'new_m_b_norm_g', 'new_m_b_w_in', 'new_m_b_conv_w', 'new_m_b_conv_b', 'new_m_b_w_rg', 'new_m_b_b_rg', 'new_m_b_w_ig', 'new_m_b_b_ig', 'new_m_b_lam', 'new_m_b_w_out', 'new_m_final_norm_g', 'new_v_meta_tokens', 'new_v_a_norm_g', 'new_v_a_w_in', 'new_v_a_q_norm_g', 'new_v_a_kv_norm_g', 'new_v_a_w_uq', 'new_v_a_w_ukv', 'new_v_a_w_out', 'new_v_b_norm_g', 'new_v_b_w_in', 'new_v_b_conv_w', 'new_v_b_conv_b', 'new_v_b_w_rg', 'new_v_b_b_rg', 'new_v_b_w_ig', 'new_v_b_b_ig', 'new_v_b_lam', 'new_v_b_w_out', 'new_v_final_norm_g']
TWIN_LEAF_KINDS = {'loss': 'loss', 'grad_x': 'grad_x', 'grad_meta_tokens': 'grad_w', 'grad_a_norm_g': 'grad_w', 'grad_a_w_in': 'grad_w', 'grad_a_q_norm_g': 'grad_w', 'grad_a_kv_norm_g': 'grad_w', 'grad_a_w_uq': 'grad_w', 'grad_a_w_ukv': 'grad_w', 'grad_a_w_out': 'grad_w', 'grad_b_norm_g': 'grad_w', 'grad_b_w_in': 'grad_w', 'grad_b_conv_w': 'grad_w', 'grad_b_conv_b': 'grad_w', 'grad_b_w_rg': 'grad_w', 'grad_b_b_rg': 'grad_w', 'grad_b_w_ig': 'grad_w', 'grad_b_b_ig': 'grad_w', 'grad_b_lam': 'grad_w', 'grad_b_w_out': 'grad_w', 'grad_final_norm_g': 'grad_w', 'delta_meta_tokens': 'delta_w', 'delta_a_norm_g': 'delta_w', 'delta_a_w_in': 'delta_w', 'delta_a_q_norm_g': 'delta_w', 'delta_a_kv_norm_g': 'delta_w', 'delta_a_w_uq': 'delta_w', 'delta_a_w_ukv': 'delta_w', 'delta_a_w_out': 'delta_w', 'delta_b_norm_g': 'delta_w', 'delta_b_w_in': 'delta_w', 'delta_b_conv_w': 'delta_w', 'delta_b_conv_b': 'delta_w', 'delta_b_w_rg': 'delta_w', 'delta_b_b_rg': 'delta_w', 'delta_b_w_ig': 'delta_w', 'delta_b_b_ig': 'delta_w', 'delta_b_lam': 'delta_w', 'delta_b_w_out': 'delta_w', 'delta_final_norm_g': 'delta_w', 'new_m_meta_tokens': 'new_m', 'new_m_a_norm_g': 'new_m', 'new_m_a_w_in': 'new_m', 'new_m_a_q_norm_g': 'new_m', 'new_m_a_kv_norm_g': 'new_m', 'new_m_a_w_uq': 'new_m', 'new_m_a_w_ukv': 'new_m', 'new_m_a_w_out': 'new_m', 'new_m_b_norm_g': 'new_m', 'new_m_b_w_in': 'new_m', 'new_m_b_conv_w': 'new_m', 'new_m_b_conv_b': 'new_m', 'new_m_b_w_rg': 'new_m', 'new_m_b_b_rg': 'new_m', 'new_m_b_w_ig': 'new_m', 'new_m_b_b_ig': 'new_m', 'new_m_b_lam': 'new_m', 'new_m_b_w_out': 'new_m', 'new_m_final_norm_g': 'new_m', 'new_v_meta_tokens': 'new_v', 'new_v_a_norm_g': 'new_v', 'new_v_a_w_in': 'new_v', 'new_v_a_q_norm_g': 'new_v', 'new_v_a_kv_norm_g': 'new_v', 'new_v_a_w_uq': 'new_v', 'new_v_a_w_ukv': 'new_v', 'new_v_a_w_out': 'new_v', 'new_v_b_norm_g': 'new_v', 'new_v_b_w_in': 'new_v', 'new_v_b_conv_w': 'new_v', 'new_v_b_conv_b': 'new_v', 'new_v_b_w_rg': 'new_v', 'new_v_b_b_rg': 'new_v', 'new_v_b_w_ig': 'new_v', 'new_v_b_b_ig': 'new_v', 'new_v_b_lam': 'new_v', 'new_v_b_w_out': 'new_v', 'new_v_final_norm_g': 'new_v'}


def _forward(args):
    return _fwd_reference(*[args[k] for k in FWD_PARAMS])


def _output_shape():
    out = _jax.eval_shape(lambda: _forward(_fwd_setup_inputs(0)))
    return out.shape, out.dtype

N_MICROBATCH = 1
ADAM_LR = 0.001
ADAM_B1 = 0.9
ADAM_B2 = 0.999
ADAM_EPS = 1e-08
ADAM_WD = 0.01
ADAM_STEP = 10
PER_EXAMPLE_BATCH_AXIS = {'x': 0, 'loss_target': 0}
SHARED_INPUTS = []
_WEIGHT_DTYPES = {'meta_tokens': _jnp.float32, 'a_norm_g': _jnp.float32, 'a_w_in': _jnp.float32, 'a_q_norm_g': _jnp.float32, 'a_kv_norm_g': _jnp.float32, 'a_w_uq': _jnp.float32, 'a_w_ukv': _jnp.float32, 'a_w_out': _jnp.float32, 'b_norm_g': _jnp.float32, 'b_w_in': _jnp.float32, 'b_conv_w': _jnp.float32, 'b_conv_b': _jnp.float32, 'b_w_rg': _jnp.float32, 'b_b_rg': _jnp.float32, 'b_w_ig': _jnp.float32, 'b_b_ig': _jnp.float32, 'b_lam': _jnp.float32, 'b_w_out': _jnp.float32, 'final_norm_g': _jnp.float32}
MOMENT_SCALE = {'meta_tokens': 5.462429e-03, 'a_norm_g': 3.915450e-02, 'a_w_in': 2.957309e-02, 'a_q_norm_g': 2.770455e-02, 'a_kv_norm_g': 5.777311e-02, 'a_w_uq': 1.469080e-02, 'a_w_ukv': 1.864952e-02, 'a_w_out': 2.158283e-02, 'b_norm_g': 9.751167e-02, 'b_w_in': 6.310103e-02, 'b_conv_w': 6.551053e-02, 'b_conv_b': 7.813880e-01, 'b_w_rg': 1.883115e-02, 'b_b_rg': 1.488639e-02, 'b_w_ig': 3.320426e-02, 'b_b_ig': 2.458590e-02, 'b_lam': 2.910924e-02, 'b_w_out': 6.177931e-02, 'final_norm_g': 3.196749e+01}


def _to_microbatches(a, axis):
    t = _jnp.moveaxis(a, axis, 0)
    t = t.reshape((N_MICROBATCH, t.shape[0] // N_MICROBATCH) + t.shape[1:])
    return _jnp.moveaxis(t, 1, axis + 1)


def setup_inputs(seed: int = 0) -> dict:
    inp = _fwd_setup_inputs(seed)
    key = _jax.random.fold_in(_jax.random.key(seed), 7919)
    shape, _ = _output_shape()
    out = dict(inp)
    out["loss_target"] = _jax.random.normal(_jax.random.fold_in(key, 0), shape, _jnp.float32)
    for i, name in enumerate(TWIN_WEIGHTS):
        w = inp[name].astype(_jnp.float32)
        if MOMENT_SCALE is None:
            s = _jnp.sqrt(_jnp.mean(_jnp.square(w)) + 1e-30)
        else:
            s = MOMENT_SCALE[name]
        km, kv = _jax.random.split(_jax.random.fold_in(key, i + 1))
        out[name] = w
        out["m_" + name] = s * _jax.random.normal(km, w.shape, _jnp.float32)
        out["v_" + name] = (s * s) * _jax.random.uniform(kv, w.shape, _jnp.float32, 0.5, 1.5)
    if N_MICROBATCH > 1:
        for name, axis in PER_EXAMPLE_BATCH_AXIS.items():
            out[name] = _to_microbatches(out[name], axis)
    return {'x': out['x'], 'meta_tokens': out['meta_tokens'], 'a_norm_g': out['a_norm_g'], 'a_w_in': out['a_w_in'], 'a_q_norm_g': out['a_q_norm_g'], 'a_kv_norm_g': out['a_kv_norm_g'], 'a_w_uq': out['a_w_uq'], 'a_w_ukv': out['a_w_ukv'], 'a_w_out': out['a_w_out'], 'b_norm_g': out['b_norm_g'], 'b_w_in': out['b_w_in'], 'b_conv_w': out['b_conv_w'], 'b_conv_b': out['b_conv_b'], 'b_w_rg': out['b_w_rg'], 'b_b_rg': out['b_b_rg'], 'b_w_ig': out['b_w_ig'], 'b_b_ig': out['b_b_ig'], 'b_lam': out['b_lam'], 'b_w_out': out['b_w_out'], 'final_norm_g': out['final_norm_g'], 'loss_target': out['loss_target'], 'm_meta_tokens': out['m_meta_tokens'], 'm_a_norm_g': out['m_a_norm_g'], 'm_a_w_in': out['m_a_w_in'], 'm_a_q_norm_g': out['m_a_q_norm_g'], 'm_a_kv_norm_g': out['m_a_kv_norm_g'], 'm_a_w_uq': out['m_a_w_uq'], 'm_a_w_ukv': out['m_a_w_ukv'], 'm_a_w_out': out['m_a_w_out'], 'm_b_norm_g': out['m_b_norm_g'], 'm_b_w_in': out['m_b_w_in'], 'm_b_conv_w': out['m_b_conv_w'], 'm_b_conv_b': out['m_b_conv_b'], 'm_b_w_rg': out['m_b_w_rg'], 'm_b_b_rg': out['m_b_b_rg'], 'm_b_w_ig': out['m_b_w_ig'], 'm_b_b_ig': out['m_b_b_ig'], 'm_b_lam': out['m_b_lam'], 'm_b_w_out': out['m_b_w_out'], 'm_final_norm_g': out['m_final_norm_g'], 'v_meta_tokens': out['v_meta_tokens'], 'v_a_norm_g': out['v_a_norm_g'], 'v_a_w_in': out['v_a_w_in'], 'v_a_q_norm_g': out['v_a_q_norm_g'], 'v_a_kv_norm_g': out['v_a_kv_norm_g'], 'v_a_w_uq': out['v_a_w_uq'], 'v_a_w_ukv': out['v_a_w_ukv'], 'v_a_w_out': out['v_a_w_out'], 'v_b_norm_g': out['v_b_norm_g'], 'v_b_w_in': out['v_b_w_in'], 'v_b_conv_w': out['v_b_conv_w'], 'v_b_conv_b': out['v_b_conv_b'], 'v_b_w_rg': out['v_b_w_rg'], 'v_b_b_rg': out['v_b_b_rg'], 'v_b_w_ig': out['v_b_w_ig'], 'v_b_b_ig': out['v_b_b_ig'], 'v_b_lam': out['v_b_lam'], 'v_b_w_out': out['v_b_w_out'], 'v_final_norm_g': out['v_final_norm_g']}


def _loss(weights, diff, rest, loss_target):
    with _jax.named_scope("forward"):
        args = {**rest, TWIN_DIFF_INPUT: diff, **{k: w.astype(_WEIGHT_DTYPES[k]) for k, w in weights.items()}}
        y = _forward(args)
    with _jax.named_scope("loss_head"):
        err = _jnp.square(y.astype(_jnp.float32) - loss_target)
        return 0.5 * _jnp.sum(_jnp.mean(err, axis=-1)) if err.ndim else 0.5 * err


def _adamw(w, g, m, v):
    m = ADAM_B1 * m + (1.0 - ADAM_B1) * g
    v = ADAM_B2 * v + (1.0 - ADAM_B2) * _jnp.square(g)
    m_hat = m / (1.0 - ADAM_B1 ** ADAM_STEP)
    v_hat = v / (1.0 - ADAM_B2 ** ADAM_STEP)
    delta = -ADAM_LR * (m_hat / (_jnp.sqrt(v_hat) + ADAM_EPS) + ADAM_WD * w)
    return delta, m, v


def reference(x, meta_tokens, a_norm_g, a_w_in, a_q_norm_g, a_kv_norm_g, a_w_uq, a_w_ukv, a_w_out, b_norm_g, b_w_in, b_conv_w, b_conv_b, b_w_rg, b_b_rg, b_w_ig, b_b_ig, b_lam, b_w_out, final_norm_g, loss_target, m_meta_tokens, m_a_norm_g, m_a_w_in, m_a_q_norm_g, m_a_kv_norm_g, m_a_w_uq, m_a_w_ukv, m_a_w_out, m_b_norm_g, m_b_w_in, m_b_conv_w, m_b_conv_b, m_b_w_rg, m_b_b_rg, m_b_w_ig, m_b_b_ig, m_b_lam, m_b_w_out, m_final_norm_g, v_meta_tokens, v_a_norm_g, v_a_w_in, v_a_q_norm_g, v_a_kv_norm_g, v_a_w_uq, v_a_w_ukv, v_a_w_out, v_b_norm_g, v_b_w_in, v_b_conv_w, v_b_conv_b, v_b_w_rg, v_b_b_rg, v_b_w_ig, v_b_b_ig, v_b_lam, v_b_w_out, v_final_norm_g):
    given = dict(x=x, meta_tokens=meta_tokens, a_norm_g=a_norm_g, a_w_in=a_w_in, a_q_norm_g=a_q_norm_g, a_kv_norm_g=a_kv_norm_g, a_w_uq=a_w_uq, a_w_ukv=a_w_ukv, a_w_out=a_w_out, b_norm_g=b_norm_g, b_w_in=b_w_in, b_conv_w=b_conv_w, b_conv_b=b_conv_b, b_w_rg=b_w_rg, b_b_rg=b_b_rg, b_w_ig=b_w_ig, b_b_ig=b_b_ig, b_lam=b_lam, b_w_out=b_w_out, final_norm_g=final_norm_g, loss_target=loss_target, m_meta_tokens=m_meta_tokens, m_a_norm_g=m_a_norm_g, m_a_w_in=m_a_w_in, m_a_q_norm_g=m_a_q_norm_g, m_a_kv_norm_g=m_a_kv_norm_g, m_a_w_uq=m_a_w_uq, m_a_w_ukv=m_a_w_ukv, m_a_w_out=m_a_w_out, m_b_norm_g=m_b_norm_g, m_b_w_in=m_b_w_in, m_b_conv_w=m_b_conv_w, m_b_conv_b=m_b_conv_b, m_b_w_rg=m_b_w_rg, m_b_b_rg=m_b_b_rg, m_b_w_ig=m_b_w_ig, m_b_b_ig=m_b_b_ig, m_b_lam=m_b_lam, m_b_w_out=m_b_w_out, m_final_norm_g=m_final_norm_g, v_meta_tokens=v_meta_tokens, v_a_norm_g=v_a_norm_g, v_a_w_in=v_a_w_in, v_a_q_norm_g=v_a_q_norm_g, v_a_kv_norm_g=v_a_kv_norm_g, v_a_w_uq=v_a_w_uq, v_a_w_ukv=v_a_w_ukv, v_a_w_out=v_a_w_out, v_b_norm_g=v_b_norm_g, v_b_w_in=v_b_w_in, v_b_conv_w=v_b_conv_w, v_b_conv_b=v_b_conv_b, v_b_w_rg=v_b_w_rg, v_b_b_rg=v_b_b_rg, v_b_w_ig=v_b_w_ig, v_b_b_ig=v_b_b_ig, v_b_lam=v_b_lam, v_b_w_out=v_b_w_out, v_final_norm_g=v_final_norm_g)
    weights = {n: given[n] for n in TWIN_WEIGHTS}
    shared = {n: given[n] for n in SHARED_INPUTS}
    per_example = {n: given[n] for n in ['x']}
    grad_fn = _jax.value_and_grad(_loss, argnums=(0, 1))

    def one_microbatch(ex, loss_target):
        ex = dict(ex)
        diff = ex.pop(TWIN_DIFF_INPUT)
        return grad_fn(weights, diff, {**shared, **ex}, loss_target)

    if N_MICROBATCH == 1:
        loss, (grad_w, grad_x) = one_microbatch(per_example, given["loss_target"])
    else:
        def body(carry, xs):
            loss_sum, grad_sum = carry
            l_k, (gw_k, gx_k) = one_microbatch(xs[0], xs[1])
            with _jax.named_scope("update"):
                return (loss_sum + l_k, _jax.tree.map(_jnp.add, grad_sum, gw_k)), gx_k

        init = (_jnp.zeros((), _jnp.float32), _jax.tree.map(_jnp.zeros_like, weights))
        (loss, grad_w), grad_x = _jax.lax.scan(body, init, (per_example, given["loss_target"]))
    with _jax.named_scope("update"):
        delta_w, new_m, new_v = {}, {}, {}
        for n in TWIN_WEIGHTS:
            delta_w[n], new_m[n], new_v[n] = _adamw(weights[n], grad_w[n], given["m_" + n], given["v_" + n])
    return (loss, grad_x, *[grad_w[n] for n in TWIN_WEIGHTS], *[delta_w[n] for n in TWIN_WEIGHTS],
            *[new_m[n] for n in TWIN_WEIGHTS], *[new_v[n] for n in TWIN_WEIGHTS])
```

```python
import functools

import jax
import jax.numpy as jnp
from jax import lax
from jax.experimental import pallas as pl
from jax.experimental.pallas import tpu as pltpu

F32 = jnp.float32
BF16 = jnp.bfloat16

D_MODEL = 1024
N_META = 16
RMS_EPS = 1e-6
HEADS = 8
QK_NOPE = 128
QK_ROPE = 64
V_HEAD = 128
Q_LORA = 384
KV_LORA = 256
HEAD_PAD = 256
LAT = Q_LORA + KV_LORA
LAT_PAD = LAT + 128
ROPE_BASE = 10000.0
MASK_VALUE = -1e30
LRU_WIDTH = 1024
LRU_BLOCKS = 4
LRU_BLOCK = 256
CONV_WIDTH = 4
LRU_C = 8.0
N_DEV = 8
ADAM_LR, ADAM_B1, ADAM_B2, ADAM_EPS, ADAM_WD, ADAM_STEP = 0.001, 0.9, 0.999, 1e-08, 0.01, 10

LANES = 128
SUBLANES = 8
VMEM_LIMIT = 56 * 1024 * 1024
MESH = pl.DeviceIdType.MESH

NT = (((1,), (1,)), ((), ()))
TN = (((0,), (0,)), ((), ()))


def _row_block(rows):
    return 384 if rows % 384 == 0 else 128


def _cparams(sem):
    return pltpu.CompilerParams(dimension_semantics=sem, vmem_limit_bytes=VMEM_LIMIT)


def _silu(x):
    return x * jax.nn.sigmoid(x)


def _dsilu(x):
    s = jax.nn.sigmoid(x)
    return s * (1.0 + x * (1.0 - s))


def _rms_fwd(x):
    r = lax.rsqrt(jnp.mean(x * x, axis=-1, keepdims=True) + RMS_EPS)
    return x * r, r


def _rms_bwd(dy, xn, r, g):
    t = dy * g
    dx = r * (t - xn * jnp.mean(t * xn, axis=-1, keepdims=True))
    return dx, jnp.sum(dy * xn, axis=0, keepdims=True)


def _expm1_neg(x):
    small = x * (1.0 + x * (1 / 2) * (1.0 + x * (1 / 3) * (1.0 + x * (1 / 4) * (1.0 + x * (1 / 5) * (
        1.0 + x * (1 / 6) * (1.0 + x * (1 / 7)))))))
    return jnp.where(x > -0.25, small, jnp.exp(x) - 1.0)


def _softplus_neg(lam):
    z = jnp.exp(-jnp.abs(lam))
    w = z / (2.0 + z)
    w2 = w * w
    series = 2.0 * w * (1.0 + w2 * (1 / 3) + w2 * w2 * (1 / 5))
    return jnp.maximum(-lam, 0.0) + jnp.where(z < 0.1, series, jnp.log(1.0 + z))


def _norm_proj_fwd(h, g, w, n1, name):
    rows, n = h.shape[0], w.shape[1]
    tr = _row_block(rows)

    def body(h_ref, g_ref, w_ref, p1_ref, p2_ref):
        xn, _ = _rms_fwd(h_ref[...])
        hn = (xn * g_ref[...]).astype(BF16)
        p = jnp.dot(hn, w_ref[...], preferred_element_type=F32)
        p1_ref[...] = p[:, :n1]
        p2_ref[...] = p[:, n1:]

    return pl.pallas_call(
        body, name=name, grid=(rows // tr,),
        in_specs=[pl.BlockSpec((tr, D_MODEL), lambda i: (i, 0)),
                  pl.BlockSpec((1, D_MODEL), lambda i: (0, 0)),
                  pl.BlockSpec((D_MODEL, n), lambda i: (0, 0))],
        out_specs=[pl.BlockSpec((tr, n1), lambda i: (i, 0)),
                   pl.BlockSpec((tr, n - n1), lambda i: (i, 0))],
        out_shape=[jax.ShapeDtypeStruct((rows, n1), F32), jax.ShapeDtypeStruct((rows, n - n1), F32)],
        compiler_params=_cparams(("parallel",)),
    )(h, g, w)


def _norm_proj_bwd(h, g, w, dp1, dp2, dh_in, name):
    rows, n = h.shape[0], w.shape[1]
    n1 = dp1.shape[1]
    tr = _row_block(rows)

    def body(h_ref, g_ref, w_ref, dp1_ref, dp2_ref, dhin_ref, dh_ref, dw_ref, dg_ref):
        @pl.when(pl.program_id(0) == 0)
        def _():
            dw_ref[...] = jnp.zeros_like(dw_ref)
            dg_ref[...] = jnp.zeros_like(dg_ref)

        gv = g_ref[...]
        xn, r = _rms_fwd(h_ref[...])
        hn = (xn * gv).astype(BF16)
        dp = jnp.concatenate([dp1_ref[...].astype(BF16), dp2_ref[...].astype(BF16)], axis=1)
        dw_ref[...] += lax.dot_general(hn, dp, TN, preferred_element_type=F32)
        dhn = lax.dot_general(dp, w_ref[...], NT, preferred_element_type=F32)
        dx, dg = _rms_bwd(dhn, xn, r, gv)
        dg_ref[...] += dg
        dh_ref[...] = dhin_ref[...] + dx

    return pl.pallas_call(
        body, name=name, grid=(rows // tr,),
        in_specs=[pl.BlockSpec((tr, D_MODEL), lambda i: (i, 0)),
                  pl.BlockSpec((1, D_MODEL), lambda i: (0, 0)),
                  pl.BlockSpec((D_MODEL, n), lambda i: (0, 0)),
                  pl.BlockSpec((tr, n1), lambda i: (i, 0)),
                  pl.BlockSpec((tr, n - n1), lambda i: (i, 0)),
                  pl.BlockSpec((tr, D_MODEL), lambda i: (i, 0))],
        out_specs=[pl.BlockSpec((tr, D_MODEL), lambda i: (i, 0)),
                   pl.BlockSpec((D_MODEL, n), lambda i: (0, 0)),
                   pl.BlockSpec((1, D_MODEL), lambda i: (0, 0))],
        out_shape=[jax.ShapeDtypeStruct((rows, D_MODEL), F32),
                   jax.ShapeDtypeStruct((D_MODEL, n), F32),
                   jax.ShapeDtypeStruct((1, D_MODEL), F32)],
        compiler_params=_cparams(("arbitrary",)),
    )(h, g, w, dp1, dp2, dh_in)


def _gated_out_fwd(a, gate, h, w, name):
    rows = a.shape[0]
    tr = _row_block(rows)

    def body(a_ref, gate_ref, h_ref, w_ref, o_ref):
        y = (a_ref[...] * _silu(gate_ref[...])).astype(BF16)
        o_ref[...] = h_ref[...] + jnp.dot(y, w_ref[...], preferred_element_type=F32)

    blk = pl.BlockSpec((tr, D_MODEL), lambda i: (i, 0))
    return pl.pallas_call(
        body, name=name, grid=(rows // tr,),
        in_specs=[blk, blk, blk, pl.BlockSpec((D_MODEL, D_MODEL), lambda i: (0, 0))],
        out_specs=blk,
        out_shape=jax.ShapeDtypeStruct((rows, D_MODEL), F32),
        compiler_params=_cparams(("parallel",)),
    )(a, gate, h, w)


def _gated_out_bwd(a, gate, dh, w, da_dtype, name):
    rows = a.shape[0]
    tr = _row_block(rows)

    def body(a_ref, gate_ref, dh_ref, w_ref, da_ref, dgate_ref, dw_ref):
        @pl.when(pl.program_id(0) == 0)
        def _():
            dw_ref[...] = jnp.zeros_like(dw_ref)

        av, gv = a_ref[...], gate_ref[...]
        sg = _silu(gv)
        dhb = dh_ref[...].astype(BF16)
        dw_ref[...] += lax.dot_general((av * sg).astype(BF16), dhb, TN, preferred_element_type=F32)
        dy = lax.dot_general(dhb, w_ref[...], NT, preferred_element_type=F32)
        da_ref[...] = (dy * sg).astype(da_dtype)
        dgate_ref[...] = dy * av * _dsilu(gv)

    blk = pl.BlockSpec((tr, D_MODEL), lambda i: (i, 0))
    wblk = pl.BlockSpec((D_MODEL, D_MODEL), lambda i: (0, 0))
    return pl.pallas_call(
        body, name=name, grid=(rows // tr,),
        in_specs=[blk, blk, blk, wblk],
        out_specs=[blk, blk, wblk],
        out_shape=[jax.ShapeDtypeStruct((rows, D_MODEL), da_dtype),
                   jax.ShapeDtypeStruct((rows, D_MODEL), F32),
                   jax.ShapeDtypeStruct((D_MODEL, D_MODEL), F32)],
        compiler_params=_cparams(("arbitrary",)),
    )(a, gate, dh, w)


def _rope(v, cos, sin, lane):
    swapped = jnp.where(lane < QK_ROPE // 2, pltpu.roll(v, LANES - QK_ROPE // 2, 1), pltpu.roll(v, QK_ROPE // 2, 1))
    return v * cos + swapped * sin


def _unrope(dv, cos, sin, lane):
    t = dv * sin
    swapped = jnp.where(lane < QK_ROPE // 2, pltpu.roll(t, LANES - QK_ROPE // 2, 1), pltpu.roll(t, QK_ROPE // 2, 1))
    return dv * cos + swapped


def _mla_qkv_fwd(lat, gq, gkv, wuq, wukv, cos, sin, scale):
    rows = lat.shape[0]
    tr = _row_block(rows)

    def body(lat_ref, gq_ref, gkv_ref, wuq_ref, wukv_ref, cos_ref, sin_ref, qc_ref, kc_ref, v_ref):
        qn, _ = _rms_fwd(lat_ref[:, :Q_LORA])
        kvn, _ = _rms_fwd(lat_ref[:, Q_LORA:LAT])
        q = jnp.dot((qn * gq_ref[...]).astype(BF16), wuq_ref[...], preferred_element_type=F32)
        kv = jnp.dot((kvn * gkv_ref[...]).astype(BF16), wukv_ref[...], preferred_element_type=F32)
        c, s = cos_ref[...], sin_ref[...]
        lane = lax.broadcasted_iota(jnp.int32, (tr, LANES), 1)
        kr = _rope(lat_ref[:, LAT:LAT_PAD], c, s, lane).astype(BF16)
        for hd in range(HEADS):
            o = hd * HEAD_PAD
            qc_ref[:, o:o + QK_NOPE] = (q[:, o:o + QK_NOPE] * scale).astype(BF16)
            qc_ref[:, o + QK_NOPE:o + HEAD_PAD] = (_rope(q[:, o + QK_NOPE:o + HEAD_PAD], c, s, lane) * scale).astype(BF16)
            kc_ref[:, o:o + QK_NOPE] = kv[:, o:o + QK_NOPE].astype(BF16)
            kc_ref[:, o + QK_NOPE:o + HEAD_PAD] = kr
            v_ref[:, hd * V_HEAD:(hd + 1) * V_HEAD] = kv[:, o + QK_NOPE:o + HEAD_PAD].astype(BF16)

    full = lambda shape: pl.BlockSpec(shape, lambda i: (0, 0))
    rowb = lambda n: pl.BlockSpec((tr, n), lambda i: (i, 0))
    return pl.pallas_call(
        body, name="mla_qkv_fwd", grid=(rows // tr,),
        in_specs=[rowb(LAT_PAD), full((1, Q_LORA)), full((1, KV_LORA)), full((Q_LORA, HEADS * HEAD_PAD)),
                  full((KV_LORA, HEADS * HEAD_PAD)), rowb(LANES), rowb(LANES)],
        out_specs=[rowb(HEADS * HEAD_PAD), rowb(HEADS * HEAD_PAD), rowb(HEADS * V_HEAD)],
        out_shape=[jax.ShapeDtypeStruct((rows, HEADS * HEAD_PAD), BF16),
                   jax.ShapeDtypeStruct((rows, HEADS * HEAD_PAD), BF16),
                   jax.ShapeDtypeStruct((rows, HEADS * V_HEAD), BF16)],
        compiler_params=_cparams(("parallel",)),
    )(lat, gq, gkv, wuq, wukv, cos, sin)


def _mla_qkv_bwd(lat, gq, gkv, wuq, wukv, cos, sin, dqc, dkc, dv, scale):
    rows = lat.shape[0]
    tr = _row_block(rows)

    def body(lat_ref, gq_ref, gkv_ref, wuq_ref, wukv_ref, cos_ref, sin_ref, dqc_ref, dkc_ref, dv_ref,
             dlat_ref, dwuq_ref, dwukv_ref, dgq_ref, dgkv_ref):
        @pl.when(pl.program_id(0) == 0)
        def _():
            dwuq_ref[...] = jnp.zeros_like(dwuq_ref)
            dwukv_ref[...] = jnp.zeros_like(dwukv_ref)
            dgq_ref[...] = jnp.zeros_like(dgq_ref)
            dgkv_ref[...] = jnp.zeros_like(dgkv_ref)

        c, s = cos_ref[...], sin_ref[...]
        lane = lax.broadcasted_iota(jnp.int32, (tr, LANES), 1)
        gqv, gkvv = gq_ref[...], gkv_ref[...]
        qn, rq = _rms_fwd(lat_ref[:, :Q_LORA])
        kvn, rkv = _rms_fwd(lat_ref[:, Q_LORA:LAT])
        dq_parts, dkv_parts = [], []
        dkr = jnp.zeros((tr, LANES), F32)
        for hd in range(HEADS):
            o = hd * HEAD_PAD
            dq_parts.append(dqc_ref[:, o:o + QK_NOPE])
            dq_parts.append(_unrope(dqc_ref[:, o + QK_NOPE:o + HEAD_PAD].astype(F32), c, s, lane).astype(BF16))
            dkv_parts.append(dkc_ref[:, o:o + QK_NOPE])
            dkv_parts.append(dv_ref[:, hd * V_HEAD:(hd + 1) * V_HEAD])
            dkr = dkr + dkc_ref[:, o + QK_NOPE:o + HEAD_PAD].astype(F32)
        dq = jnp.concatenate(dq_parts, axis=1)
        dkv = jnp.concatenate(dkv_parts, axis=1)
        dwuq_ref[...] += scale * lax.dot_general((qn * gqv).astype(BF16), dq, TN, preferred_element_type=F32)
        dwukv_ref[...] += lax.dot_general((kvn * gkvv).astype(BF16), dkv, TN, preferred_element_type=F32)
        dqn = scale * lax.dot_general(dq, wuq_ref[...], NT, preferred_element_type=F32)
        dkvn = lax.dot_general(dkv, wukv_ref[...], NT, preferred_element_type=F32)
        dqlat, dgq = _rms_bwd(dqn, qn, rq, gqv)
        dkvlat, dgkv = _rms_bwd(dkvn, kvn, rkv, gkvv)
        dgq_ref[...] += dgq
        dgkv_ref[...] += dgkv
        dlat_ref[:, :Q_LORA] = dqlat
        dlat_ref[:, Q_LORA:LAT] = dkvlat
        dlat_ref[:, LAT:LAT_PAD] = _unrope(dkr, c, s, lane)

    full = lambda shape: pl.BlockSpec(shape, lambda i: (0, 0))
    rowb = lambda n: pl.BlockSpec((tr, n), lambda i: (i, 0))
    return pl.pallas_call(
        body, name="mla_qkv_bwd", grid=(rows // tr,),
        in_specs=[rowb(LAT_PAD), full((1, Q_LORA)), full((1, KV_LORA)), full((Q_LORA, HEADS * HEAD_PAD)),
                  full((KV_LORA, HEADS * HEAD_PAD)), rowb(LANES), rowb(LANES),
                  rowb(HEADS * HEAD_PAD), rowb(HEADS * HEAD_PAD), rowb(HEADS * V_HEAD)],
        out_specs=[rowb(LAT_PAD), full((Q_LORA, HEADS * HEAD_PAD)), full((KV_LORA, HEADS * HEAD_PAD)),
                   full((1, Q_LORA)), full((1, KV_LORA))],
        out_shape=[jax.ShapeDtypeStruct((rows, LAT_PAD), F32),
                   jax.ShapeDtypeStruct((Q_LORA, HEADS * HEAD_PAD), F32),
                   jax.ShapeDtypeStruct((KV_LORA, HEADS * HEAD_PAD), F32),
                   jax.ShapeDtypeStruct((1, Q_LORA), F32),
                   jax.ShapeDtypeStruct((1, KV_LORA), F32)],
        compiler_params=_cparams(("arbitrary",)),
    )(lat, gq, gkv, wuq, wukv, cos, sin, dqc, dkc, dv)


def _causal_mask(t):
    row = lax.broadcasted_iota(jnp.int32, (t, t), 0)
    col = lax.broadcasted_iota(jnp.int32, (t, t), 1)
    return col <= row


def _attn_fwd(qc, kc, v):
    rows = qc.shape[0]
    t = _row_block(rows)
    nblk = rows // t

    def body(q_ref, k_ref, v_ref, o_ref, lse_ref, m_ref, l_ref, acc_ref):
        i = pl.program_id(1)
        m_ref[...] = jnp.full_like(m_ref, MASK_VALUE)
        l_ref[...] = jnp.zeros_like(l_ref)
        acc_ref[...] = jnp.zeros_like(acc_ref)
        q = q_ref[...]

        def step(j, masked):
            rs = pl.ds(pl.multiple_of(j * t, t), t)
            s = lax.dot_general(q, k_ref[rs, :], NT, preferred_element_type=F32)
            if masked:
                s = jnp.where(_causal_mask(t), s, MASK_VALUE)
            m_prev = m_ref[...]
            m_new = jnp.maximum(m_prev, jnp.max(s, axis=-1, keepdims=True))
            alpha = jnp.exp(m_prev - m_new)
            p = jnp.exp(s - m_new)
            l_ref[...] = alpha * l_ref[...] + jnp.sum(p, axis=-1, keepdims=True)
            acc_ref[...] = alpha * acc_ref[...] + jnp.dot(p.astype(BF16), v_ref[rs, :], preferred_element_type=F32)
            m_ref[...] = m_new

        def loop_body(j, carry):
            step(j, False)
            return carry

        lax.fori_loop(0, i, loop_body, 0)
        step(i, True)
        o_ref[...] = acc_ref[...] / l_ref[...]
        lse_ref[0] = m_ref[...] + jnp.log(l_ref[...])

    return pl.pallas_call(
        body, name="attn_fwd", grid=(HEADS, nblk),
        in_specs=[pl.BlockSpec((t, HEAD_PAD), lambda h, i: (i, h)),
                  pl.BlockSpec((rows, HEAD_PAD), lambda h, i: (0, h)),
                  pl.BlockSpec((rows, V_HEAD), lambda h, i: (0, h))],
        out_specs=[pl.BlockSpec((t, V_HEAD), lambda h, i: (i, h)),
                   pl.BlockSpec((1, t, 1), lambda h, i: (h, i, 0))],
        out_shape=[jax.ShapeDtypeStruct((rows, HEADS * V_HEAD), F32),
                   jax.ShapeDtypeStruct((HEADS, rows, 1), F32)],
        scratch_shapes=[pltpu.VMEM((t, 1), F32), pltpu.VMEM((t, 1), F32), pltpu.VMEM((t, V_HEAD), F32)],
        compiler_params=_cparams(("parallel", "arbitrary")),
    )(qc, kc, v)


def _attn_bwd(qc, kc, v, o, lse, do):
    rows = qc.shape[0]
    t = _row_block(rows)
    nblk = rows // t

    def body(q_ref, k_ref, v_ref, o_ref, lse_ref, do_ref, dq_ref, dk_ref, dv_ref, dq_acc, dk_acc, dv_acc):
        j = pl.program_id(1)

        @pl.when(j == 0)
        def _():
            dq_acc[...] = jnp.zeros_like(dq_acc)

        dk_acc[...] = jnp.zeros_like(dk_acc)
        dv_acc[...] = jnp.zeros_like(dv_acc)
        k = k_ref[...]
        vv = v_ref[...]

        def step(i, masked):
            rs = pl.ds(pl.multiple_of(i * t, t), t)
            q = q_ref[rs, :]
            dob = do_ref[rs, :]
            s = lax.dot_general(q, k, NT, preferred_element_type=F32)
            if masked:
                s = jnp.where(_causal_mask(t), s, MASK_VALUE)
            p = jnp.exp(s - lse_ref[0, rs, :])
            dv_acc[...] += lax.dot_general(p.astype(BF16), dob, TN, preferred_element_type=F32)
            dp = lax.dot_general(dob, vv, NT, preferred_element_type=F32)
            delta = jnp.sum(dob.astype(F32) * o_ref[rs, :], axis=-1, keepdims=True)
            ds = (p * (dp - delta)).astype(BF16)
            dk_acc[...] += lax.dot_general(ds, q, TN, preferred_element_type=F32)
            dq_acc[rs, :] += jnp.dot(ds, k, preferred_element_type=F32)

        def loop_body(i, carry):
            step(i, False)
            return carry

        step(j, True)
        lax.fori_loop(j + 1, nblk, loop_body, 0)
        dk_ref[...] = dk_acc[...].astype(BF16)
        dv_ref[...] = dv_acc[...].astype(BF16)

        @pl.when(j == nblk - 1)
        def _():
            dq_ref[...] = dq_acc[...].astype(BF16)

    return pl.pallas_call(
        body, name="attn_bwd", grid=(HEADS, nblk),
        in_specs=[pl.BlockSpec((rows, HEAD_PAD), lambda h, j: (0, h)),
                  pl.BlockSpec((t, HEAD_PAD), lambda h, j: (j, h)),
                  pl.BlockSpec((t, V_HEAD), lambda h, j: (j, h)),
                  pl.BlockSpec((rows, V_HEAD), lambda h, j: (0, h)),
                  pl.BlockSpec((1, rows, 1), lambda h, j: (h, 0, 0)),
                  pl.BlockSpec((rows, V_HEAD), lambda h, j: (0, h))],
        out_specs=[pl.BlockSpec((rows, HEAD_PAD), lambda h, j: (0, h)),
                   pl.BlockSpec((t, HEAD_PAD), lambda h, j: (j, h)),
                   pl.BlockSpec((t, V_HEAD), lambda h, j: (j, h))],
        out_shape=[jax.ShapeDtypeStruct((rows, HEADS * HEAD_PAD), BF16),
                   jax.ShapeDtypeStruct((rows, HEADS * HEAD_PAD), BF16),
                   jax.ShapeDtypeStruct((rows, HEADS * V_HEAD), BF16)],
        scratch_shapes=[pltpu.VMEM((rows, HEAD_PAD), F32), pltpu.VMEM((t, HEAD_PAD), F32),
                        pltpu.VMEM((t, V_HEAD), F32)],
        compiler_params=_cparams(("arbitrary", "arbitrary")),
    )(qc, kc, v, o, lse, do)


def _shift_down(prev_tile, x, k):
    xx = jnp.concatenate([prev_tile, x], axis=0)
    return pltpu.roll(xx, k, 0)[SUBLANES:]


def _shift_up(x, next_tile, k):
    n = x.shape[0]
    xx = jnp.concatenate([x, next_tile], axis=0)
    return pltpu.roll(xx, n + SUBLANES - k, 0)[:n]


def _lru_gates(u, u_prev, cw_ref, cb_ref, wrg_ref, brg_ref, wig_ref, big_ref, lam_ref, first_block):
    taps = [_shift_down(u_prev, u, CONV_WIDTH - 1 - j) if j < CONV_WIDTH - 1 else u for j in range(CONV_WIDTH)]
    uc = cb_ref[...] + taps[0] * cw_ref[0:1, :]
    for j in range(1, CONV_WIDTH):
        uc = uc + taps[j] * cw_ref[j:j + 1, :]
    ub = uc.astype(BF16)
    zr = jnp.concatenate([jnp.dot(ub[:, g * LRU_BLOCK:(g + 1) * LRU_BLOCK], wrg_ref[g], preferred_element_type=F32)
                          for g in range(LRU_BLOCKS)], axis=1) + brg_ref[...]
    zi = jnp.concatenate([jnp.dot(ub[:, g * LRU_BLOCK:(g + 1) * LRU_BLOCK], wig_ref[g], preferred_element_type=F32)
                          for g in range(LRU_BLOCKS)], axis=1) + big_ref[...]
    r = jax.nn.sigmoid(zr)
    ig = jax.nn.sigmoid(zi)
    sp = _softplus_neg(lam_ref[...])
    log_a = (-LRU_C) * r * sp
    a = jnp.exp(log_a)
    mult_raw = jnp.sqrt(-_expm1_neg(2.0 * log_a))
    row = lax.broadcasted_iota(jnp.int32, u.shape, 0)
    is_start = jnp.logical_and(first_block, row == 0)
    mult = jnp.where(is_start, 1.0, mult_raw)
    return dict(taps=taps, uc=uc, ub=ub, r=r, ig=ig, sp=sp, a=a, mult=mult, mult_raw=mult_raw, is_start=is_start)


def _rglru_fwd(u, cw, cb, wrg, brg, wig, big, lam):
    rows = u.shape[0]
    tb = _row_block(rows)

    def body(u_ref, cw_ref, cb_ref, wrg_ref, brg_ref, wig_ref, big_ref, lam_ref, hs_ref, utail, hcar, a_s, b_s):
        i = pl.program_id(0)

        @pl.when(i == 0)
        def _():
            utail[...] = jnp.zeros_like(utail)
            hcar[...] = jnp.zeros_like(hcar)

        u = u_ref[...]
        gt = _lru_gates(u, utail[...], cw_ref, cb_ref, wrg_ref, brg_ref, wig_ref, big_ref, lam_ref, i == 0)
        a_s[...] = gt["a"]
        b_s[...] = gt["mult"] * (gt["ig"] * gt["uc"])
        row8 = lax.broadcasted_iota(jnp.int32, (SUBLANES, LRU_WIDTH), 0)

        def tile(tix, carry):
            rs = pl.ds(pl.multiple_of(tix * SUBLANES, SUBLANES), SUBLANES)
            av, bv = a_s[rs, :], b_s[rs, :]
            for k in (1, 2, 4):
                keep = row8 >= k
                bv = jnp.where(keep, av * pltpu.roll(bv, k, 0) + bv, bv)
                av = jnp.where(keep, av * pltpu.roll(av, k, 0), av)
            h8 = av * carry + bv
            hs_ref[rs, :] = h8
            return jnp.broadcast_to(h8[SUBLANES - 1:SUBLANES, :], (SUBLANES, LRU_WIDTH))

        hcar[...] = lax.fori_loop(0, tb // SUBLANES, tile, hcar[...])
        utail[...] = u[tb - SUBLANES:, :]

    full2 = lambda shape: pl.BlockSpec(shape, lambda i: (0, 0))
    full3 = lambda shape: pl.BlockSpec(shape, lambda i: (0, 0, 0))
    blk = pl.BlockSpec((tb, LRU_WIDTH), lambda i: (i, 0))
    return pl.pallas_call(
        body, name="rglru_fwd", grid=(rows // tb,),
        in_specs=[blk, full2((CONV_WIDTH, LRU_WIDTH)), full2((1, LRU_WIDTH)),
                  full3((LRU_BLOCKS, LRU_BLOCK, LRU_BLOCK)), full2((1, LRU_WIDTH)),
                  full3((LRU_BLOCKS, LRU_BLOCK, LRU_BLOCK)), full2((1, LRU_WIDTH)), full2((1, LRU_WIDTH))],
        out_specs=blk,
        out_shape=jax.ShapeDtypeStruct((rows, LRU_WIDTH), F32),
        scratch_shapes=[pltpu.VMEM((SUBLANES, LRU_WIDTH), F32), pltpu.VMEM((SUBLANES, LRU_WIDTH), F32),
                        pltpu.VMEM((tb, LRU_WIDTH), F32), pltpu.VMEM((tb, LRU_WIDTH), F32)],
        compiler_params=_cparams(("arbitrary",)),
    )(u, cw, cb, wrg, brg, wig, big, lam)


def _rglru_bwd(u, hs, dhs, cw, cb, wrg, brg, wig, big, lam):
    rows = u.shape[0]
    tb = _row_block(rows)
    nblk = rows // tb
    tiles = tb // SUBLANES

    def body(u_ref, up_ref, hs_ref, hp_ref, dhs_ref, cw_ref, cb_ref, wrg_ref, brg_ref, wig_ref, big_ref, lam_ref,
             du_ref, dcw_ref, dcb_ref, dwrg_ref, dbrg_ref, dwig_ref, dbig_ref, dlam_ref,
             gcar, duc_head, a_s, b_s, g_s, dsp_acc):
        step = pl.program_id(0)
        blk_ix = nblk - 1 - step

        @pl.when(step == 0)
        def _():
            for ref in (dcw_ref, dcb_ref, dwrg_ref, dbrg_ref, dwig_ref, dbig_ref, gcar, duc_head, dsp_acc):
                ref[...] = jnp.zeros_like(ref)

        first = blk_ix == 0
        u = u_ref[...]
        u_prev = jnp.where(first, 0.0, up_ref[...])
        h_prev_tile = jnp.where(first, 0.0, hp_ref[...])
        gt = _lru_gates(u, u_prev, cw_ref, cb_ref, wrg_ref, brg_ref, wig_ref, big_ref, lam_ref, first)
        a, r, ig, uc, mult = gt["a"], gt["r"], gt["ig"], gt["uc"], gt["mult"]
        dhs_v = dhs_ref[...]

        a_s[...] = a
        b_s[...] = a * dhs_v
        row8 = lax.broadcasted_iota(jnp.int32, (SUBLANES, LRU_WIDTH), 0)

        def tile(tix, carry):
            rs = pl.ds(pl.multiple_of((tiles - 1 - tix) * SUBLANES, SUBLANES), SUBLANES)
            av, bv = a_s[rs, :], b_s[rs, :]
            for k in (1, 2, 4):
                keep = row8 < SUBLANES - k
                bv = jnp.where(keep, av * pltpu.roll(bv, SUBLANES - k, 0) + bv, bv)
                av = jnp.where(keep, av * pltpu.roll(av, SUBLANES - k, 0), av)
            g8 = av * carry + bv
            g_s[rs, :] = g8
            return jnp.broadcast_to(g8[0:1, :], (SUBLANES, LRU_WIDTH))

        g_next = gcar[...]
        gcar[...] = lax.fori_loop(0, tiles, tile, g_next)
        g = dhs_v + _shift_up(g_s[...], g_next, 1)

        h_prev = _shift_down(h_prev_tile, hs_ref[...], 1)
        da = g * h_prev
        iu = ig * uc
        dmult = jnp.where(gt["is_start"], 0.0, g * iu)
        d_ig = g * mult * uc
        duc = g * mult * ig
        dlog_a = da * a - dmult * (a * a) / gt["mult_raw"]
        dzr = (dlog_a * ((-LRU_C) * gt["sp"])) * r * (1.0 - r)
        dsp_acc[...] += jnp.sum(dlog_a * ((-LRU_C) * r), axis=0, keepdims=True)
        dzi = d_ig * ig * (1.0 - ig)
        dbrg_ref[...] += jnp.sum(dzr, axis=0, keepdims=True)
        dbig_ref[...] += jnp.sum(dzi, axis=0, keepdims=True)
        dzr_b, dzi_b = dzr.astype(BF16), dzi.astype(BF16)
        ub = gt["ub"]
        duc_parts = []
        for gi in range(LRU_BLOCKS):
            cs = slice(gi * LRU_BLOCK, (gi + 1) * LRU_BLOCK)
            dwrg_ref[gi] += lax.dot_general(ub[:, cs], dzr_b[:, cs], TN, preferred_element_type=F32)
            dwig_ref[gi] += lax.dot_general(ub[:, cs], dzi_b[:, cs], TN, preferred_element_type=F32)
            duc_parts.append(lax.dot_general(dzr_b[:, cs], wrg_ref[gi], NT, preferred_element_type=F32)
                             + lax.dot_general(dzi_b[:, cs], wig_ref[gi], NT, preferred_element_type=F32))
        duc = duc + jnp.concatenate(duc_parts, axis=1)

        dcb_ref[...] += jnp.sum(duc, axis=0, keepdims=True)
        taps = gt["taps"]
        for jt in range(CONV_WIDTH):
            dcw_ref[jt:jt + 1, :] += jnp.sum(duc * taps[jt], axis=0, keepdims=True)
        head = duc_head[...]
        du = duc * cw_ref[CONV_WIDTH - 1:CONV_WIDTH, :]
        for jt in range(CONV_WIDTH - 1):
            du = du + _shift_up(duc, head, CONV_WIDTH - 1 - jt) * cw_ref[jt:jt + 1, :]
        du_ref[...] = du
        duc_head[...] = duc[:SUBLANES, :]

        @pl.when(step == nblk - 1)
        def _():
            dlam_ref[...] = -dsp_acc[...] * jax.nn.sigmoid(-lam_ref[...])

    full2 = lambda shape: pl.BlockSpec(shape, lambda s: (0, 0))
    full3 = lambda shape: pl.BlockSpec(shape, lambda s: (0, 0, 0))
    blk = pl.BlockSpec((tb, LRU_WIDTH), lambda s: (nblk - 1 - s, 0))
    prev_tile = pl.BlockSpec((SUBLANES, LRU_WIDTH), lambda s: (jnp.maximum((nblk - 1 - s) * tiles - 1, 0), 0))
    wshape = (LRU_BLOCKS, LRU_BLOCK, LRU_BLOCK)
    return pl.pallas_call(
        body, name="rglru_bwd", grid=(nblk,),
        in_specs=[blk, prev_tile, blk, prev_tile, blk, full2((CONV_WIDTH, LRU_WIDTH)), full2((1, LRU_WIDTH)),
                  full3(wshape), full2((1, LRU_WIDTH)), full3(wshape), full2((1, LRU_WIDTH)), full2((1, LRU_WIDTH))],
        out_specs=[blk, full2((CONV_WIDTH, LRU_WIDTH)), full2((1, LRU_WIDTH)), full3(wshape), full2((1, LRU_WIDTH)),
                   full3(wshape), full2((1, LRU_WIDTH)), full2((1, LRU_WIDTH))],
        out_shape=[jax.ShapeDtypeStruct((rows, LRU_WIDTH), F32),
                   jax.ShapeDtypeStruct((CONV_WIDTH, LRU_WIDTH), F32), jax.ShapeDtypeStruct((1, LRU_WIDTH), F32),
                   jax.ShapeDtypeStruct(wshape, F32), jax.ShapeDtypeStruct((1, LRU_WIDTH), F32),
                   jax.ShapeDtypeStruct(wshape, F32), jax.ShapeDtypeStruct((1, LRU_WIDTH), F32),
                   jax.ShapeDtypeStruct((1, LRU_WIDTH), F32)],
        scratch_shapes=[pltpu.VMEM((SUBLANES, LRU_WIDTH), F32), pltpu.VMEM((SUBLANES, LRU_WIDTH), F32),
                        pltpu.VMEM((tb, LRU_WIDTH), F32), pltpu.VMEM((tb, LRU_WIDTH), F32),
                        pltpu.VMEM((tb, LRU_WIDTH), F32), pltpu.VMEM((1, LRU_WIDTH), F32)],
        compiler_params=_cparams(("arbitrary",)),
    )(u, u, hs, hs, dhs, cw, cb, wrg, brg, wig, big, lam)


def _final_loss(h, gf, target, n_real):
    rows = h.shape[0]
    tr = _row_block(rows)

    def body(h_ref, g_ref, t_ref, dh_ref, loss_ref, dg_ref):
        i = pl.program_id(0)

        @pl.when(i == 0)
        def _():
            loss_ref[...] = jnp.zeros_like(loss_ref)
            dg_ref[...] = jnp.zeros_like(dg_ref)

        gv = g_ref[...]
        xn, r = _rms_fwd(h_ref[...])
        row = i * tr + lax.broadcasted_iota(jnp.int32, (tr, 1), 0)
        live = jnp.logical_and(row >= N_META, row < n_real)
        err = jnp.where(live, xn * gv - t_ref[...], 0.0)
        loss_ref[...] += (0.5 / D_MODEL) * jnp.sum(jnp.sum(err * err, axis=1, keepdims=True), axis=0, keepdims=True)
        dx, dg = _rms_bwd(err * (1.0 / D_MODEL), xn, r, gv)
        dg_ref[...] += dg
        dh_ref[...] = dx

    blk = pl.BlockSpec((tr, D_MODEL), lambda i: (i, 0))
    return pl.pallas_call(
        body, name="final_loss", grid=(rows // tr,),
        in_specs=[blk, pl.BlockSpec((1, D_MODEL), lambda i: (0, 0)), blk],
        out_specs=[blk, pl.BlockSpec((1, 1), lambda i: (0, 0)), pl.BlockSpec((1, D_MODEL), lambda i: (0, 0))],
        out_shape=[jax.ShapeDtypeStruct((rows, D_MODEL), F32), jax.ShapeDtypeStruct((1, 1), F32),
                   jax.ShapeDtypeStruct((1, D_MODEL), F32)],
        compiler_params=_cparams(("arbitrary",)),
    )(h, gf, target)


def _my_place():
    x, y, c = lax.axis_index("x"), lax.axis_index("y"), lax.axis_index("c")
    return x, y, c, 4 * x + 2 * y + c


def _peer(x, y, c, k):
    px, py, pc = x ^ (k >> 2), y ^ ((k >> 1) & 1), c ^ (k & 1)
    return (px, py, pc), 4 * px + 2 * py + pc


def _all_gather(big, small):
    def body(big_ref, small_ref, obig_ref, osmall_ref, send_sems, recv_sems, local_sems):
        x, y, c, me = _my_place()
        own = [pltpu.make_async_copy(big_ref, obig_ref.at[me], local_sems.at[0]),
               pltpu.make_async_copy(small_ref, osmall_ref.at[me], local_sems.at[1])]
        for cp in own:
            cp.start()
        copies = []
        for k in range(1, N_DEV):
            peer, _ = _peer(x, y, c, k)
            for part, (src, dst) in enumerate(((big_ref, obig_ref), (small_ref, osmall_ref))):
                copies.append(pltpu.make_async_remote_copy(
                    src_ref=src, dst_ref=dst.at[me], send_sem=send_sems.at[part, k], recv_sem=recv_sems.at[part, k],
                    device_id=peer, device_id_type=MESH))
        for cp in copies:
            cp.start()
        for cp in copies:
            cp.wait()
        for cp in own:
            cp.wait()

    n = big.shape[0]
    hbm = pl.BlockSpec(memory_space=pl.ANY)
    return pl.pallas_call(
        body, name="weight_all_gather",
        in_specs=[hbm, hbm], out_specs=[hbm, hbm],
        out_shape=[jax.ShapeDtypeStruct((N_DEV, n, LANES), BF16), jax.ShapeDtypeStruct((N_DEV,) + small.shape, F32)],
        scratch_shapes=[pltpu.SemaphoreType.DMA((2, N_DEV)), pltpu.SemaphoreType.DMA((2, N_DEV)),
                        pltpu.SemaphoreType.DMA((2,))],
        compiler_params=pltpu.CompilerParams(has_side_effects=True),
    )(big, small)


GRAD_CHUNK = 32


def _grad_exchange(gbig, rep):
    n = gbig.shape[1]
    nrep = rep.shape[0]

    def body(gbig_ref, rep_ref, out_ref, orep_ref, land, land_rep, send_sems, recv_sems, local_sems):
        x, y, c, me = _my_place()
        own = [pltpu.make_async_copy(gbig_ref.at[me], land.at[me], local_sems.at[0]),
               pltpu.make_async_copy(rep_ref, land_rep.at[me], local_sems.at[1])]
        for cp in own:
            cp.start()
        copies = []
        for k in range(1, N_DEV):
            peer, pid = _peer(x, y, c, k)
            copies.append(pltpu.make_async_remote_copy(
                src_ref=gbig_ref.at[pid], dst_ref=land.at[me], send_sem=send_sems.at[0, k],
                recv_sem=recv_sems.at[0, k], device_id=peer, device_id_type=MESH))
            copies.append(pltpu.make_async_remote_copy(
                src_ref=rep_ref, dst_ref=land_rep.at[me], send_sem=send_sems.at[1, k],
                recv_sem=recv_sems.at[1, k], device_id=peer, device_id_type=MESH))
        for cp in copies:
            cp.start()
        for cp in copies:
            cp.wait()
        for cp in own:
            cp.wait()

        def chunk(ci, carry):
            rs = pl.ds(pl.multiple_of(ci * GRAD_CHUNK, GRAD_CHUNK), GRAD_CHUNK)
            acc = land[0, rs, :].astype(F32)
            for d in range(1, N_DEV):
                acc = acc + land[d, rs, :].astype(F32)
            out_ref[rs, :] = acc
            return carry

        lax.fori_loop(0, n // GRAD_CHUNK, chunk, 0)
        acc = land_rep[0]
        for d in range(1, N_DEV):
            acc = acc + land_rep[d]
        orep_ref[...] = acc

    return pl.pallas_call(
        body, name="grad_exchange",
        in_specs=[pl.BlockSpec(memory_space=pl.ANY), pl.BlockSpec(memory_space=pl.ANY)],
        out_specs=[pl.BlockSpec(memory_space=pltpu.VMEM), pl.BlockSpec(memory_space=pltpu.VMEM)],
        out_shape=[jax.ShapeDtypeStruct((n, LANES), F32), jax.ShapeDtypeStruct((nrep, LANES), F32)],
        scratch_shapes=[pltpu.VMEM((N_DEV, n, LANES), BF16), pltpu.VMEM((N_DEV, nrep, LANES), F32),
                        pltpu.SemaphoreType.DMA((2, N_DEV)), pltpu.SemaphoreType.DMA((2, N_DEV)),
                        pltpu.SemaphoreType.DMA((2,))],
        compiler_params=pltpu.CompilerParams(vmem_limit_bytes=VMEM_LIMIT, has_side_effects=True),
    )(gbig, rep)


def _adamw_all(ws, gs, ms, vs):
    n = len(ws)

    def body(*refs):
        w_refs, g_refs, m_refs, v_refs = refs[0:n], refs[n:2 * n], refs[2 * n:3 * n], refs[3 * n:4 * n]
        d_refs, nm_refs, nv_refs = refs[4 * n:5 * n], refs[5 * n:6 * n], refs[6 * n:7 * n]
        for w_ref, g_ref, m_ref, v_ref, d_ref, nm_ref, nv_ref in zip(w_refs, g_refs, m_refs, v_refs, d_refs, nm_refs, nv_refs):
            g = g_ref[...]
            m = ADAM_B1 * m_ref[...] + (1.0 - ADAM_B1) * g
            v = ADAM_B2 * v_ref[...] + (1.0 - ADAM_B2) * jnp.square(g)
            m_hat = m / (1.0 - ADAM_B1 ** ADAM_STEP)
            v_hat = v / (1.0 - ADAM_B2 ** ADAM_STEP)
            d_ref[...] = -ADAM_LR * (m_hat / (jnp.sqrt(v_hat) + ADAM_EPS) + ADAM_WD * w_ref[...])
            nm_ref[...] = m
            nv_ref[...] = v

    shapes = [jax.ShapeDtypeStruct(w.shape, F32) for w in ws]
    outs = pl.pallas_call(
        body, name="adamw", out_shape=shapes * 3,
        compiler_params=pltpu.CompilerParams(vmem_limit_bytes=VMEM_LIMIT),
    )(*ws, *gs, *ms, *vs)
    return outs[0:n], outs[n:2 * n], outs[2 * n:3 * n]


BIG = (("a_w_in", 1728), ("a_w_uq", 576), ("a_w_ukv", 512), ("a_w_out", 1024), ("b_w_in", 2048),
       ("b_w_rg", 256), ("b_w_ig", 256), ("b_w_out", 1024))
SMALL = (("meta_tokens", 16), ("b_norm_g", 1), ("b_conv_w", 4), ("b_conv_b", 1), ("b_b_rg", 1), ("b_b_ig", 1),
         ("b_lam", 1))
SMALL_ROWS = 32
REP = (("a_norm_g", 8), ("a_q_norm_g", 3), ("a_kv_norm_g", 2), ("final_norm_g", 8))
REP_ROWS = 24


def _offsets(table):
    out, o = {}, 0
    for name, n in table:
        out[name] = (o, n)
        o += n
    return out, o


def _rope_tables(rows):
    pos = jnp.arange(rows, dtype=F32)
    inv_freq = ROPE_BASE ** (-jnp.arange(0, QK_ROPE, 2, dtype=F32) / QK_ROPE)
    ang = pos[:, None] * inv_freq[None, :]
    cos, sin = jnp.cos(ang), jnp.sin(ang)
    zeros = jnp.zeros((rows, LANES - QK_ROPE), F32)
    return jnp.concatenate([cos, cos, zeros], axis=1), jnp.concatenate([-sin, sin, zeros], axis=1)


def kernel(x, meta_tokens, a_norm_g, a_w_in, a_q_norm_g, a_kv_norm_g, a_w_uq, a_w_ukv, a_w_out, b_norm_g, b_w_in, b_conv_w, b_conv_b, b_w_rg, b_b_rg, b_w_ig, b_b_ig, b_lam, b_w_out, final_norm_g, loss_target, m_meta_tokens, m_a_norm_g, m_a_w_in, m_a_q_norm_g, m_a_kv_norm_g, m_a_w_uq, m_a_w_ukv, m_a_w_out, m_b_norm_g, m_b_w_in, m_b_conv_w, m_b_conv_b, m_b_w_rg, m_b_b_rg, m_b_w_ig, m_b_b_ig, m_b_lam, m_b_w_out, m_final_norm_g, v_meta_tokens, v_a_norm_g, v_a_w_in, v_a_q_norm_g, v_a_kv_norm_g, v_a_w_uq, v_a_w_ukv, v_a_w_out, v_b_norm_g, v_b_w_in, v_b_conv_w, v_b_conv_b, v_b_w_rg, v_b_b_rg, v_b_w_ig, v_b_b_ig, v_b_lam, v_b_w_out, v_final_norm_g):
    names = ("meta_tokens", "a_norm_g", "a_w_in", "a_q_norm_g", "a_kv_norm_g", "a_w_uq", "a_w_ukv", "a_w_out",
             "b_norm_g", "b_w_in", "b_conv_w", "b_conv_b", "b_w_rg", "b_b_rg", "b_w_ig", "b_b_ig", "b_lam", "b_w_out",
             "final_norm_g")
    w = dict(zip(names, (meta_tokens, a_norm_g, a_w_in, a_q_norm_g, a_kv_norm_g, a_w_uq, a_w_ukv, a_w_out, b_norm_g,
                         b_w_in, b_conv_w, b_conv_b, b_w_rg, b_b_rg, b_w_ig, b_b_ig, b_lam, b_w_out, final_norm_g)))
    mom_m = dict(zip(names, (m_meta_tokens, m_a_norm_g, m_a_w_in, m_a_q_norm_g, m_a_kv_norm_g, m_a_w_uq, m_a_w_ukv,
                             m_a_w_out, m_b_norm_g, m_b_w_in, m_b_conv_w, m_b_conv_b, m_b_w_rg, m_b_b_rg, m_b_w_ig,
                             m_b_b_ig, m_b_lam, m_b_w_out, m_final_norm_g)))
    mom_v = dict(zip(names, (v_meta_tokens, v_a_norm_g, v_a_w_in, v_a_q_norm_g, v_a_kv_norm_g, v_a_w_uq, v_a_w_ukv,
                             v_a_w_out, v_b_norm_g, v_b_w_in, v_b_conv_w, v_b_conv_b, v_b_w_rg, v_b_b_rg, v_b_w_ig,
                             v_b_b_ig, v_b_lam, v_b_w_out, v_final_norm_g)))

    seq = x.shape[1]
    n_real = N_META + seq
    rows = -(-n_real // LANES) * LANES
    scale = (QK_NOPE + QK_ROPE) ** -0.5
    big_off, big_rows = _offsets(BIG)
    small_off, _ = _offsets(SMALL)
    rep_off, _ = _offsets(REP)

    send_big = jnp.concatenate([w[nm].reshape(-1, LANES) for nm, _ in BIG], axis=0).astype(BF16)
    send_small = jnp.concatenate([w[nm].reshape(-1, LANES) for nm, _ in SMALL]
                                 + [jnp.zeros((SMALL_ROWS - sum(n for _, n in SMALL), LANES), F32)], axis=0)
    all_big, all_small = _all_gather(send_big, send_small)

    def big_seg(nm):
        o, n = big_off[nm]
        return all_big[:, o:o + n, :]

    def small_seg(nm):
        o, n = small_off[nm]
        return all_small[:, o:o + n, :]

    def cols(seg, r, cdev):
        return seg.reshape(N_DEV, r, cdev).transpose(1, 0, 2).reshape(r, N_DEV * cdev)

    w_in_a = cols(big_seg("a_w_in"), D_MODEL, 216)
    w_in_a = jnp.concatenate([w_in_a[:, :LAT + QK_ROPE], jnp.zeros((D_MODEL, LAT_PAD - LAT - QK_ROPE), BF16),
                              w_in_a[:, LAT + QK_ROPE:]], axis=1)
    w_uq = jnp.pad(big_seg("a_w_uq").reshape(N_DEV, Q_LORA, QK_NOPE + QK_ROPE).transpose(1, 0, 2),
                   ((0, 0), (0, 0), (0, HEAD_PAD - QK_NOPE - QK_ROPE))).reshape(Q_LORA, HEADS * HEAD_PAD)
    w_ukv = cols(big_seg("a_w_ukv"), KV_LORA, QK_NOPE + V_HEAD)
    w_out_a = big_seg("a_w_out").reshape(D_MODEL, D_MODEL)
    w_in_b = cols(big_seg("b_w_in"), D_MODEL, 2 * LRU_WIDTH // N_DEV)
    w_rg = big_seg("b_w_rg").reshape(N_DEV, LRU_BLOCKS, LRU_BLOCK // N_DEV, LRU_BLOCK).transpose(1, 0, 2, 3).reshape(
        LRU_BLOCKS, LRU_BLOCK, LRU_BLOCK)
    w_ig = big_seg("b_w_ig").reshape(N_DEV, LRU_BLOCKS, LRU_BLOCK // N_DEV, LRU_BLOCK).transpose(1, 0, 2, 3).reshape(
        LRU_BLOCKS, LRU_BLOCK, LRU_BLOCK)
    w_out_b = big_seg("b_w_out").reshape(D_MODEL, D_MODEL)
    meta_full = small_seg("meta_tokens").transpose(1, 0, 2).reshape(N_META, D_MODEL)
    vec = lambda nm: small_seg(nm).reshape(1, D_MODEL)
    g_b, conv_b, b_rg, b_ig, lam = vec("b_norm_g"), vec("b_conv_b"), vec("b_b_rg"), vec("b_b_ig"), vec("b_lam")
    conv_w = small_seg("b_conv_w").transpose(1, 0, 2).reshape(CONV_WIDTH, LRU_WIDTH)
    g_a, g_q, g_kv = a_norm_g, a_q_norm_g, a_kv_norm_g
    g_f = final_norm_g.reshape(1, D_MODEL)

    h0 = jnp.concatenate([meta_full, x[0], jnp.zeros((rows - n_real, D_MODEL), F32)], axis=0)
    target = jnp.concatenate([jnp.zeros((N_META, D_MODEL), F32), loss_target[0],
                              jnp.zeros((rows - n_real, D_MODEL), F32)], axis=0)
    cos, sin = _rope_tables(rows)

    lat, gate_a = _norm_proj_fwd(h0, g_a, w_in_a, LAT_PAD, "a_in_fwd")
    qc, kc, v = _mla_qkv_fwd(lat, g_q, g_kv, w_uq, w_ukv, cos, sin, scale)
    o, lse = _attn_fwd(qc, kc, v)
    h1 = _gated_out_fwd(o, gate_a, h0, w_out_a, "a_out_fwd")
    u, gate_b = _norm_proj_fwd(h1, g_b, w_in_b, LRU_WIDTH, "b_in_fwd")
    hs = _rglru_fwd(u, conv_w, conv_b, w_rg, b_rg, w_ig, b_ig, lam)
    h2 = _gated_out_fwd(hs, gate_b, h1, w_out_b, "b_out_fwd")
    dh2, loss_part, dg_f = _final_loss(h2, g_f, target, n_real)

    dhs, dgate_b, dw_out_b = _gated_out_bwd(hs, gate_b, dh2, w_out_b, F32, "b_out_bwd")
    du, dconv_w, dconv_b, dw_rg, db_rg, dw_ig, db_ig, dlam = _rglru_bwd(u, hs, dhs, conv_w, conv_b, w_rg, b_rg, w_ig,
                                                                       b_ig, lam)
    dh1, dw_in_b, dg_b = _norm_proj_bwd(h1, g_b, w_in_b, du, dgate_b, dh2, "b_in_bwd")
    do, dgate_a, dw_out_a = _gated_out_bwd(o, gate_a, dh1, w_out_a, BF16, "a_out_bwd")
    dqc, dkc, dv = _attn_bwd(qc, kc, v, o, lse, do)
    dlat, dw_uq, dw_ukv, dg_q, dg_kv = _mla_qkv_bwd(lat, g_q, g_kv, w_uq, w_ukv, cos, sin, dqc, dkc, dv, scale)
    dh0, dw_in_a, dg_a = _norm_proj_bwd(h0, g_a, w_in_a, dlat, dgate_a, dh1, "a_in_bwd")

    loss = lax.psum(loss_part[0, 0], ("x", "y", "c"))
    grad_x = dh0[N_META:n_real][None]

    def to_cols(g, cdev):
        r = g.shape[0]
        return g.reshape(r, N_DEV, cdev).transpose(1, 0, 2).reshape(N_DEV, -1, LANES)

    dw_in_a_nat = jnp.concatenate([dw_in_a[:, :LAT + QK_ROPE], dw_in_a[:, LAT_PAD:]], axis=1)
    gparts = {
        "a_w_in": to_cols(dw_in_a_nat, 216),
        "a_w_uq": dw_uq.reshape(Q_LORA, HEADS, HEAD_PAD)[:, :, :QK_NOPE + QK_ROPE].transpose(1, 0, 2).reshape(
            N_DEV, -1, LANES),
        "a_w_ukv": to_cols(dw_ukv, QK_NOPE + V_HEAD),
        "a_w_out": dw_out_a.reshape(N_DEV, -1, LANES),
        "b_w_in": to_cols(dw_in_b, 2 * LRU_WIDTH // N_DEV),
        "b_w_rg": dw_rg.reshape(LRU_BLOCKS, N_DEV, LRU_BLOCK // N_DEV, LRU_BLOCK).transpose(1, 0, 2, 3).reshape(
            N_DEV, -1, LANES),
        "b_w_ig": dw_ig.reshape(LRU_BLOCKS, N_DEV, LRU_BLOCK // N_DEV, LRU_BLOCK).transpose(1, 0, 2, 3).reshape(
            N_DEV, -1, LANES),
        "b_w_out": dw_out_b.reshape(N_DEV, -1, LANES),
        "meta_tokens": to_cols(dh0[:N_META], LANES),
        "b_norm_g": to_cols(dg_b, LANES), "b_conv_w": to_cols(dconv_w, LANES), "b_conv_b": to_cols(dconv_b, LANES),
        "b_b_rg": to_cols(db_rg, LANES), "b_b_ig": to_cols(db_ig, LANES), "b_lam": to_cols(dlam, LANES),
    }
    pad_rows = SMALL_ROWS - sum(n for _, n in SMALL)
    gbig = jnp.concatenate([gparts[nm].astype(BF16) for nm, _ in BIG + SMALL]
                           + [jnp.zeros((N_DEV, pad_rows, LANES), BF16)], axis=1)
    rep_parts = {"a_norm_g": dg_a, "a_q_norm_g": dg_q, "a_kv_norm_g": dg_kv, "final_norm_g": dg_f}
    rep = jnp.concatenate([rep_parts[nm].reshape(-1, LANES) for nm, _ in REP]
                          + [jnp.zeros((REP_ROWS - sum(n for _, n in REP), LANES), F32)], axis=0)
    gsum, rep_sum = _grad_exchange(gbig, rep)

    grads = {}
    o_ = 0
    for nm, n in BIG + SMALL:
        grads[nm] = gsum[o_:o_ + n].reshape(w[nm].shape)
        o_ += n
    for nm, (o_r, n) in rep_off.items():
        grads[nm] = rep_sum[o_r:o_r + n].reshape(w[nm].shape)

    as2d = lambda a: a.reshape(1, -1) if a.ndim == 1 else a
    deltas, new_ms, new_vs = _adamw_all([as2d(w[nm]) for nm in names], [as2d(grads[nm]) for nm in names],
                                        [as2d(mom_m[nm]) for nm in names], [as2d(mom_v[nm]) for nm in names])
    shaped = lambda arrs: [a.reshape(w[nm].shape) for a, nm in zip(arrs, names)]
    return (loss, grad_x, *[grads[nm] for nm in names], *shaped(deltas), *shaped(new_ms), *shaped(new_vs))
```

```python
import functools

import jax
import jax.numpy as jnp
from jax import lax
from jax.experimental import pallas as pl
from jax.experimental.pallas import tpu as pltpu

F32 = jnp.float32
BF16 = jnp.bfloat16

D_MODEL = 1024
N_META = 16
RMS_EPS = 1e-6
HEADS = 8
QK_NOPE = 128
QK_ROPE = 64
V_HEAD = 128
Q_LORA = 384
KV_LORA = 256
HEAD_PAD = 256
LAT = Q_LORA + KV_LORA
LAT_PAD = LAT + 128
ROPE_BASE = 10000.0
MASK_VALUE = -1e30
LRU_WIDTH = 1024
LRU_BLOCKS = 4
LRU_BLOCK = 256
CONV_WIDTH = 4
LRU_C = 8.0
N_DEV = 8
ADAM_LR, ADAM_B1, ADAM_B2, ADAM_EPS, ADAM_WD, ADAM_STEP = 0.001, 0.9, 0.999, 1e-08, 0.01, 10

LANES = 128
SUBLANES = 8
VMEM_LIMIT = 56 * 1024 * 1024
MESH = pl.DeviceIdType.MESH

NT = (((1,), (1,)), ((), ()))
TN = (((0,), (0,)), ((), ()))


def _row_block(rows):
    return 384 if rows % 384 == 0 else 128


def _cparams(sem):
    return pltpu.CompilerParams(dimension_semantics=sem, vmem_limit_bytes=VMEM_LIMIT)


def _silu(x):
    return x * jax.nn.sigmoid(x)


def _dsilu(x):
    s = jax.nn.sigmoid(x)
    return s * (1.0 + x * (1.0 - s))


def _rms_fwd(x):
    r = lax.rsqrt(jnp.mean(x * x, axis=-1, keepdims=True) + RMS_EPS)
    return x * r, r


def _rms_bwd(dy, xn, r, g):
    t = dy * g
    dx = r * (t - xn * jnp.mean(t * xn, axis=-1, keepdims=True))
    return dx, jnp.sum(dy * xn, axis=0, keepdims=True)


def _expm1_neg(x):
    small = x * (1.0 + x * (1 / 2) * (1.0 + x * (1 / 3) * (1.0 + x * (1 / 4) * (1.0 + x * (1 / 5) * (
        1.0 + x * (1 / 6) * (1.0 + x * (1 / 7)))))))
    return jnp.where(x > -0.25, small, jnp.exp(x) - 1.0)


def _softplus_neg(lam):
    z = jnp.exp(-jnp.abs(lam))
    w = z / (2.0 + z)
    w2 = w * w
    series = 2.0 * w * (1.0 + w2 * (1 / 3) + w2 * w2 * (1 / 5))
    return jnp.maximum(-lam, 0.0) + jnp.where(z < 0.1, series, jnp.log(1.0 + z))


def _norm_proj_fwd(h, g, w, n1, name):
    rows, n = h.shape[0], w.shape[1]
    tr = _row_block(rows)

    def body(h_ref, g_ref, w_ref, p1_ref, p2_ref):
        xn, _ = _rms_fwd(h_ref[...])
        hn = (xn * g_ref[...]).astype(BF16)
        p = jnp.dot(hn, w_ref[...], preferred_element_type=F32)
        p1_ref[...] = p[:, :n1]
        p2_ref[...] = p[:, n1:]

    return pl.pallas_call(
        body, name=name, grid=(rows // tr,),
        in_specs=[pl.BlockSpec((tr, D_MODEL), lambda i: (i, 0)),
                  pl.BlockSpec((1, D_MODEL), lambda i: (0, 0)),
                  pl.BlockSpec((D_MODEL, n), lambda i: (0, 0))],
        out_specs=[pl.BlockSpec((tr, n1), lambda i: (i, 0)),
                   pl.BlockSpec((tr, n - n1), lambda i: (i, 0))],
        out_shape=[jax.ShapeDtypeStruct((rows, n1), F32), jax.ShapeDtypeStruct((rows, n - n1), F32)],
        compiler_params=_cparams(("parallel",)),
    )(h, g, w)


def _norm_proj_bwd(h, g, w, dp1, dp2, dh_in, name):
    rows, n = h.shape[0], w.shape[1]
    n1 = dp1.shape[1]
    tr = _row_block(rows)

    def body(h_ref, g_ref, w_ref, dp1_ref, dp2_ref, dhin_ref, dh_ref, dw_ref, dg_ref):
        @pl.when(pl.program_id(0) == 0)
        def _():
            dw_ref[...] = jnp.zeros_like(dw_ref)
            dg_ref[...] = jnp.zeros_like(dg_ref)

        gv = g_ref[...]
        xn, r = _rms_fwd(h_ref[...])
        hn = (xn * gv).astype(BF16)
        dp = jnp.concatenate([dp1_ref[...].astype(BF16), dp2_ref[...].astype(BF16)], axis=1)
        dw_ref[...] += lax.dot_general(hn, dp, TN, preferred_element_type=F32)
        dhn = lax.dot_general(dp, w_ref[...], NT, preferred_element_type=F32)
        dx, dg = _rms_bwd(dhn, xn, r, gv)
        dg_ref[...] += dg
        dh_ref[...] = dhin_ref[...] + dx

    return pl.pallas_call(
        body, name=name, grid=(rows // tr,),
        in_specs=[pl.BlockSpec((tr, D_MODEL), lambda i: (i, 0)),
                  pl.BlockSpec((1, D_MODEL), lambda i: (0, 0)),
                  pl.BlockSpec((D_MODEL, n), lambda i: (0, 0)),
                  pl.BlockSpec((tr, n1), lambda i: (i, 0)),
                  pl.BlockSpec((tr, n - n1), lambda i: (i, 0)),
                  pl.BlockSpec((tr, D_MODEL), lambda i: (i, 0))],
        out_specs=[pl.BlockSpec((tr, D_MODEL), lambda i: (i, 0)),
                   pl.BlockSpec((D_MODEL, n), lambda i: (0, 0)),
                   pl.BlockSpec((1, D_MODEL), lambda i: (0, 0))],
        out_shape=[jax.ShapeDtypeStruct((rows, D_MODEL), F32),
                   jax.ShapeDtypeStruct((D_MODEL, n), F32),
                   jax.ShapeDtypeStruct((1, D_MODEL), F32)],
        compiler_params=_cparams(("arbitrary",)),
    )(h, g, w, dp1, dp2, dh_in)


def _gated_out_fwd(a, gate, h, w, name):
    rows = a.shape[0]
    tr = _row_block(rows)

    def body(a_ref, gate_ref, h_ref, w_ref, o_ref):
        y = (a_ref[...] * _silu(gate_ref[...])).astype(BF16)
        o_ref[...] = h_ref[...] + jnp.dot(y, w_ref[...], preferred_element_type=F32)

    blk = pl.BlockSpec((tr, D_MODEL), lambda i: (i, 0))
    return pl.pallas_call(
        body, name=name, grid=(rows // tr,),
        in_specs=[blk, blk, blk, pl.BlockSpec((D_MODEL, D_MODEL), lambda i: (0, 0))],
        out_specs=blk,
        out_shape=jax.ShapeDtypeStruct((rows, D_MODEL), F32),
        compiler_params=_cparams(("parallel",)),
    )(a, gate, h, w)


def _gated_out_bwd(a, gate, dh, w, da_dtype, with_delta, name):
    rows = a.shape[0]
    tr = _row_block(rows)

    def body(a_ref, gate_ref, dh_ref, w_ref, da_ref, dgate_ref, dw_ref, *delta_ref):
        @pl.when(pl.program_id(0) == 0)
        def _():
            dw_ref[...] = jnp.zeros_like(dw_ref)

        av, gv = a_ref[...], gate_ref[...]
        sg = _silu(gv)
        dhb = dh_ref[...].astype(BF16)
        dw_ref[...] += lax.dot_general((av * sg).astype(BF16), dhb, TN, preferred_element_type=F32)
        dy = lax.dot_general(dhb, w_ref[...], NT, preferred_element_type=F32)
        da = (dy * sg).astype(da_dtype)
        da_ref[...] = da
        dgate_ref[...] = dy * av * _dsilu(gv)
        if with_delta:
            prod = da.astype(F32) * av
            lane = lax.broadcasted_iota(jnp.int32, (tr, LANES), 1)
            per_head = jnp.zeros((tr, LANES), F32)
            for hd in range(HEADS):
                dsum = jnp.sum(prod[:, hd * V_HEAD:(hd + 1) * V_HEAD], axis=1, keepdims=True)
                per_head = jnp.where(lane == hd, dsum, per_head)
            delta_t = per_head.T
            for hd in range(HEADS):
                delta_ref[0][hd, 0] = delta_t[hd:hd + 1, :]

    blk = pl.BlockSpec((tr, D_MODEL), lambda i: (i, 0))
    wblk = pl.BlockSpec((D_MODEL, D_MODEL), lambda i: (0, 0))
    out_specs = [blk, blk, wblk]
    out_shape = [jax.ShapeDtypeStruct((rows, D_MODEL), da_dtype),
                 jax.ShapeDtypeStruct((rows, D_MODEL), F32),
                 jax.ShapeDtypeStruct((D_MODEL, D_MODEL), F32)]
    if with_delta:
        out_specs.append(pl.BlockSpec((HEADS, 1, 1, tr), lambda i: (0, i, 0, 0)))
        out_shape.append(jax.ShapeDtypeStruct((HEADS, rows // tr, 1, tr), F32))
    return pl.pallas_call(
        body, name=name, grid=(rows // tr,),
        in_specs=[blk, blk, blk, wblk],
        out_specs=out_specs, out_shape=out_shape,
        compiler_params=_cparams(("arbitrary",)),
    )(a, gate, dh, w)


def _rope(v, cos, sin, lane):
    swapped = jnp.where(lane < QK_ROPE // 2, pltpu.roll(v, LANES - QK_ROPE // 2, 1), pltpu.roll(v, QK_ROPE // 2, 1))
    return v * cos + swapped * sin


def _unrope(dv, cos, sin, lane):
    t = dv * sin
    swapped = jnp.where(lane < QK_ROPE // 2, pltpu.roll(t, LANES - QK_ROPE // 2, 1), pltpu.roll(t, QK_ROPE // 2, 1))
    return dv * cos + swapped


def _mla_qkv_fwd(lat, gq, gkv, wuq, wukv, cos, sin, scale):
    rows = lat.shape[0]
    tr = _row_block(rows)

    def body(lat_ref, gq_ref, gkv_ref, wuq_ref, wukv_ref, cos_ref, sin_ref, qc_ref, kc_ref, v_ref, vt_ref):
        qn, _ = _rms_fwd(lat_ref[:, :Q_LORA])
        kvn, _ = _rms_fwd(lat_ref[:, Q_LORA:LAT])
        q = jnp.dot((qn * gq_ref[...]).astype(BF16), wuq_ref[...], preferred_element_type=F32)
        kv = jnp.dot((kvn * gkv_ref[...]).astype(BF16), wukv_ref[...], preferred_element_type=F32)
        c, s = cos_ref[...], sin_ref[...]
        lane = lax.broadcasted_iota(jnp.int32, (tr, LANES), 1)
        kr = _rope(lat_ref[:, LAT:LAT_PAD], c, s, lane).astype(BF16)
        for hd in range(HEADS):
            o = hd * HEAD_PAD
            qc_ref[:, o:o + QK_NOPE] = (q[:, o:o + QK_NOPE] * scale).astype(BF16)
            qc_ref[:, o + QK_NOPE:o + HEAD_PAD] = (_rope(q[:, o + QK_NOPE:o + HEAD_PAD], c, s, lane) * scale).astype(BF16)
            kc_ref[:, o:o + QK_NOPE] = kv[:, o:o + QK_NOPE].astype(BF16)
            kc_ref[:, o + QK_NOPE:o + HEAD_PAD] = kr
            vh = kv[:, o + QK_NOPE:o + HEAD_PAD]
            v_ref[:, hd * V_HEAD:(hd + 1) * V_HEAD] = vh.astype(BF16)
            vt_ref[hd, 0] = vh.T.astype(BF16)

    full = lambda shape: pl.BlockSpec(shape, lambda i: (0, 0))
    rowb = lambda n: pl.BlockSpec((tr, n), lambda i: (i, 0))
    return pl.pallas_call(
        body, name="mla_qkv_fwd", grid=(rows // tr,),
        in_specs=[rowb(LAT_PAD), full((1, Q_LORA)), full((1, KV_LORA)), full((Q_LORA, HEADS * HEAD_PAD)),
                  full((KV_LORA, HEADS * HEAD_PAD)), rowb(LANES), rowb(LANES)],
        out_specs=[rowb(HEADS * HEAD_PAD), rowb(HEADS * HEAD_PAD), rowb(HEADS * V_HEAD),
                   pl.BlockSpec((HEADS, 1, V_HEAD, tr), lambda i: (0, i, 0, 0))],
        out_shape=[jax.ShapeDtypeStruct((rows, HEADS * HEAD_PAD), BF16),
                   jax.ShapeDtypeStruct((rows, HEADS * HEAD_PAD), BF16),
                   jax.ShapeDtypeStruct((rows, HEADS * V_HEAD), BF16),
                   jax.ShapeDtypeStruct((HEADS, rows // tr, V_HEAD, tr), BF16)],
        compiler_params=_cparams(("parallel",)),
    )(lat, gq, gkv, wuq, wukv, cos, sin)


def _mla_qkv_bwd(lat, gq, gkv, wuq, wukv, cos, sin, dqc, dkc, dv, scale):
    rows = lat.shape[0]
    tr = _row_block(rows)

    def body(lat_ref, gq_ref, gkv_ref, wuq_ref, wukv_ref, cos_ref, sin_ref, dqc_ref, dkc_ref, dv_ref,
             dlat_ref, dwuq_ref, dwukv_ref, dgq_ref, dgkv_ref):
        @pl.when(pl.program_id(0) == 0)
        def _():
            dwuq_ref[...] = jnp.zeros_like(dwuq_ref)
            dwukv_ref[...] = jnp.zeros_like(dwukv_ref)
            dgq_ref[...] = jnp.zeros_like(dgq_ref)
            dgkv_ref[...] = jnp.zeros_like(dgkv_ref)

        c, s = cos_ref[...], sin_ref[...]
        lane = lax.broadcasted_iota(jnp.int32, (tr, LANES), 1)
        gqv, gkvv = gq_ref[...], gkv_ref[...]
        qn, rq = _rms_fwd(lat_ref[:, :Q_LORA])
        kvn, rkv = _rms_fwd(lat_ref[:, Q_LORA:LAT])
        dq_parts, dkv_parts = [], []
        dkr = jnp.zeros((tr, LANES), F32)
        for hd in range(HEADS):
            o = hd * HEAD_PAD
            dq_parts.append(dqc_ref[:, o:o + QK_NOPE])
            dq_parts.append(_unrope(dqc_ref[:, o + QK_NOPE:o + HEAD_PAD].astype(F32), c, s, lane).astype(BF16))
            dkv_parts.append(dkc_ref[:, o:o + QK_NOPE])
            dkv_parts.append(dv_ref[:, hd * V_HEAD:(hd + 1) * V_HEAD])
            dkr = dkr + dkc_ref[:, o + QK_NOPE:o + HEAD_PAD].astype(F32)
        dq = jnp.concatenate(dq_parts, axis=1)
        dkv = jnp.concatenate(dkv_parts, axis=1)
        dwuq_ref[...] += scale * lax.dot_general((qn * gqv).astype(BF16), dq, TN, preferred_element_type=F32)
        dwukv_ref[...] += lax.dot_general((kvn * gkvv).astype(BF16), dkv, TN, preferred_element_type=F32)
        dqn = scale * lax.dot_general(dq, wuq_ref[...], NT, preferred_element_type=F32)
        dkvn = lax.dot_general(dkv, wukv_ref[...], NT, preferred_element_type=F32)
        dqlat, dgq = _rms_bwd(dqn, qn, rq, gqv)
        dkvlat, dgkv = _rms_bwd(dkvn, kvn, rkv, gkvv)
        dgq_ref[...] += dgq
        dgkv_ref[...] += dgkv
        dlat_ref[:, :Q_LORA] = dqlat
        dlat_ref[:, Q_LORA:LAT] = dkvlat
        dlat_ref[:, LAT:LAT_PAD] = _unrope(dkr, c, s, lane)

    full = lambda shape: pl.BlockSpec(shape, lambda i: (0, 0))
    rowb = lambda n: pl.BlockSpec((tr, n), lambda i: (i, 0))
    return pl.pallas_call(
        body, name="mla_qkv_bwd", grid=(rows // tr,),
        in_specs=[rowb(LAT_PAD), full((1, Q_LORA)), full((1, KV_LORA)), full((Q_LORA, HEADS * HEAD_PAD)),
                  full((KV_LORA, HEADS * HEAD_PAD)), rowb(LANES), rowb(LANES),
                  rowb(HEADS * HEAD_PAD), rowb(HEADS * HEAD_PAD), rowb(HEADS * V_HEAD)],
        out_specs=[rowb(LAT_PAD), full((Q_LORA, HEADS * HEAD_PAD)), full((KV_LORA, HEADS * HEAD_PAD)),
                   full((1, Q_LORA)), full((1, KV_LORA))],
        out_shape=[jax.ShapeDtypeStruct((rows, LAT_PAD), F32),
                   jax.ShapeDtypeStruct((Q_LORA, HEADS * HEAD_PAD), F32),
                   jax.ShapeDtypeStruct((KV_LORA, HEADS * HEAD_PAD), F32),
                   jax.ShapeDtypeStruct((1, Q_LORA), F32),
                   jax.ShapeDtypeStruct((1, KV_LORA), F32)],
        compiler_params=_cparams(("arbitrary",)),
    )(lat, gq, gkv, wuq, wukv, cos, sin, dqc, dkc, dv)


def _causal_mask_t(t):
    key = lax.broadcasted_iota(jnp.int32, (t, t), 0)
    query = lax.broadcasted_iota(jnp.int32, (t, t), 1)
    return key <= query


def _attn_fwd(qc, kc, vt):
    rows = qc.shape[0]
    t = _row_block(rows)
    nblk = rows // t

    def body(q_ref, k_ref, vt_ref, o_ref, lse_ref, m_ref, l_ref, acc_ref, st_a, st_b):
        i = pl.program_id(1)
        m_ref[...] = jnp.full_like(m_ref, MASK_VALUE)
        l_ref[...] = jnp.zeros_like(l_ref)
        acc_ref[...] = jnp.zeros_like(acc_ref)
        q = q_ref[...]

        def scores(j, st_ref):
            rs = pl.ds(pl.multiple_of(j * t, t), t)
            st_ref[...] = lax.dot_general(k_ref[rs, :], q, NT, preferred_element_type=F32)

        def consume(j, st_ref, masked):
            st = st_ref[...]
            if masked:
                st = jnp.where(_causal_mask_t(t), st, MASK_VALUE)
            m_prev = m_ref[...]
            m_new = jnp.maximum(m_prev, jnp.max(st, axis=0, keepdims=True))
            alpha = jnp.exp(m_prev - m_new)
            pt = jnp.exp(st - m_new)
            l_ref[...] = alpha * l_ref[...] + jnp.sum(pt, axis=0, keepdims=True)
            acc_ref[...] = alpha * acc_ref[...] + jnp.dot(vt_ref[0, j], pt.astype(BF16), preferred_element_type=F32)
            m_ref[...] = m_new

        scores(0, st_a)

        def pair(jj, carry):
            j0 = 2 * jj
            scores(j0 + 1, st_b)
            consume(j0, st_a, False)
            scores(j0 + 2, st_a)
            consume(j0 + 1, st_b, False)
            return carry

        lax.fori_loop(0, i // 2, pair, 0)

        @pl.when(i % 2 == 0)
        def _():
            consume(i, st_a, True)

        @pl.when(i % 2 == 1)
        def _():
            scores(i, st_b)
            consume(i - 1, st_a, False)
            consume(i, st_b, True)

        o_ref[...] = (acc_ref[...] / l_ref[...]).T
        lse_ref[0, 0] = m_ref[...] + jnp.log(l_ref[...])

    return pl.pallas_call(
        body, name="attn_fwd", grid=(HEADS, nblk),
        in_specs=[pl.BlockSpec((t, HEAD_PAD), lambda h, i: (i, h)),
                  pl.BlockSpec((rows, HEAD_PAD), lambda h, i: (0, h)),
                  pl.BlockSpec((1, nblk, V_HEAD, t), lambda h, i: (h, 0, 0, 0))],
        out_specs=[pl.BlockSpec((t, V_HEAD), lambda h, i: (i, h)),
                   pl.BlockSpec((1, 1, 1, t), lambda h, i: (h, i, 0, 0))],
        out_shape=[jax.ShapeDtypeStruct((rows, HEADS * V_HEAD), F32),
                   jax.ShapeDtypeStruct((HEADS, nblk, 1, t), F32)],
        scratch_shapes=[pltpu.VMEM((1, t), F32), pltpu.VMEM((1, t), F32), pltpu.VMEM((V_HEAD, t), F32),
                        pltpu.VMEM((t, t), F32), pltpu.VMEM((t, t), F32)],
        compiler_params=_cparams(("parallel", "arbitrary")),
    )(qc, kc, vt)


def _attn_bwd(qc, kc, v, lse, delta, do):
    rows = qc.shape[0]
    t = _row_block(rows)
    nblk = rows // t

    def body(q_ref, k_ref, v_ref, lse_ref, delta_ref, do_ref, dq_ref, dk_ref, dv_ref, dq_acc, dk_acc, dv_acc,
             st_a, dp_a, st_b, dp_b):
        j = pl.program_id(1)

        @pl.when(j == 0)
        def _():
            dq_acc[...] = jnp.zeros_like(dq_acc)

        dk_acc[...] = jnp.zeros_like(dk_acc)
        dv_acc[...] = jnp.zeros_like(dv_acc)
        k = k_ref[...]
        vv = v_ref[...]

        def products(i, st_ref, dp_ref):
            rs = pl.ds(pl.multiple_of(i * t, t), t)
            st_ref[...] = lax.dot_general(k, q_ref[rs, :], NT, preferred_element_type=F32)
            dp_ref[...] = lax.dot_general(vv, do_ref[rs, :], NT, preferred_element_type=F32)

        def consume(i, st_ref, dp_ref):
            rs = pl.ds(pl.multiple_of(i * t, t), t)
            q = q_ref[rs, :]
            dob = do_ref[rs, :]
            st = jnp.where(jnp.logical_or(_causal_mask_t(t), i != j), st_ref[...], MASK_VALUE)
            pt = jnp.exp(st - lse_ref[0, i])
            dv_acc[...] += jnp.dot(pt.astype(BF16), dob, preferred_element_type=F32)
            dst = (pt * (dp_ref[...] - delta_ref[0, i])).astype(BF16)
            dk_acc[...] += jnp.dot(dst, q, preferred_element_type=F32)
            dq_acc[rs, :] += lax.dot_general(dst, k, TN, preferred_element_type=F32)

        last = nblk - 1 - j
        products(j, st_a, dp_a)

        def pair(pp, carry):
            i0 = j + 2 * pp
            products(i0 + 1, st_b, dp_b)
            consume(i0, st_a, dp_a)
            products(i0 + 2, st_a, dp_a)
            consume(i0 + 1, st_b, dp_b)
            return carry

        lax.fori_loop(0, last // 2, pair, 0)

        @pl.when(last % 2 == 0)
        def _():
            consume(nblk - 1, st_a, dp_a)

        @pl.when(last % 2 == 1)
        def _():
            products(nblk - 1, st_b, dp_b)
            consume(nblk - 2, st_a, dp_a)
            consume(nblk - 1, st_b, dp_b)

        dk_ref[...] = dk_acc[...].astype(BF16)
        dv_ref[...] = dv_acc[...].astype(BF16)

        @pl.when(j == nblk - 1)
        def _():
            dq_ref[...] = dq_acc[...].astype(BF16)

    stat = pl.BlockSpec((1, nblk, 1, t), lambda h, j: (h, 0, 0, 0))
    return pl.pallas_call(
        body, name="attn_bwd", grid=(HEADS, nblk),
        in_specs=[pl.BlockSpec((rows, HEAD_PAD), lambda h, j: (0, h)),
                  pl.BlockSpec((t, HEAD_PAD), lambda h, j: (j, h)),
                  pl.BlockSpec((t, V_HEAD), lambda h, j: (j, h)),
                  stat, stat,
                  pl.BlockSpec((rows, V_HEAD), lambda h, j: (0, h))],
        out_specs=[pl.BlockSpec((rows, HEAD_PAD), lambda h, j: (0, h)),
                   pl.BlockSpec((t, HEAD_PAD), lambda h, j: (j, h)),
                   pl.BlockSpec((t, V_HEAD), lambda h, j: (j, h))],
        out_shape=[jax.ShapeDtypeStruct((rows, HEADS * HEAD_PAD), BF16),
                   jax.ShapeDtypeStruct((rows, HEADS * HEAD_PAD), BF16),
                   jax.ShapeDtypeStruct((rows, HEADS * V_HEAD), BF16)],
        scratch_shapes=[pltpu.VMEM((rows, HEAD_PAD), F32), pltpu.VMEM((t, HEAD_PAD), F32),
                        pltpu.VMEM((t, V_HEAD), F32)] + [pltpu.VMEM((t, t), F32)] * 4,
        compiler_params=_cparams(("arbitrary", "arbitrary")),
    )(qc, kc, v, lse, delta, do)


def _shift_down(prev_tile, x, k):
    xx = jnp.concatenate([prev_tile, x], axis=0)
    return pltpu.roll(xx, k, 0)[SUBLANES:]


def _shift_up(x, next_tile, k):
    n = x.shape[0]
    xx = jnp.concatenate([x, next_tile], axis=0)
    return pltpu.roll(xx, n + SUBLANES - k, 0)[:n]


def _lru_gates(u, u_prev, cw_ref, cb_ref, wrg_ref, brg_ref, wig_ref, big_ref, lam_ref, first_block):
    taps = [_shift_down(u_prev, u, CONV_WIDTH - 1 - j) if j < CONV_WIDTH - 1 else u for j in range(CONV_WIDTH)]
    uc = cb_ref[...] + taps[0] * cw_ref[0:1, :]
    for j in range(1, CONV_WIDTH):
        uc = uc + taps[j] * cw_ref[j:j + 1, :]
    ub = uc.astype(BF16)
    zr = jnp.concatenate([jnp.dot(ub[:, g * LRU_BLOCK:(g + 1) * LRU_BLOCK], wrg_ref[g], preferred_element_type=F32)
                          for g in range(LRU_BLOCKS)], axis=1) + brg_ref[...]
    zi = jnp.concatenate([jnp.dot(ub[:, g * LRU_BLOCK:(g + 1) * LRU_BLOCK], wig_ref[g], preferred_element_type=F32)
                          for g in range(LRU_BLOCKS)], axis=1) + big_ref[...]
    r = jax.nn.sigmoid(zr)
    ig = jax.nn.sigmoid(zi)
    sp = _softplus_neg(lam_ref[...])
    log_a = (-LRU_C) * r * sp
    a = jnp.exp(log_a)
    mult_raw = jnp.sqrt(-_expm1_neg(2.0 * log_a))
    row = lax.broadcasted_iota(jnp.int32, u.shape, 0)
    is_start = jnp.logical_and(first_block, row == 0)
    mult = jnp.where(is_start, 1.0, mult_raw)
    return dict(taps=taps, uc=uc, ub=ub, r=r, ig=ig, sp=sp, a=a, mult=mult, mult_raw=mult_raw, is_start=is_start)


def _rglru_fwd(u, cw, cb, wrg, brg, wig, big, lam):
    rows = u.shape[0]
    tb = _row_block(rows)

    def body(u_ref, cw_ref, cb_ref, wrg_ref, brg_ref, wig_ref, big_ref, lam_ref, hs_ref, utail, hcar, a_s, b_s):
        i = pl.program_id(0)

        @pl.when(i == 0)
        def _():
            utail[...] = jnp.zeros_like(utail)
            hcar[...] = jnp.zeros_like(hcar)

        u = u_ref[...]
        gt = _lru_gates(u, utail[...], cw_ref, cb_ref, wrg_ref, brg_ref, wig_ref, big_ref, lam_ref, i == 0)
        a_s[...] = gt["a"]
        b_s[...] = gt["mult"] * (gt["ig"] * gt["uc"])
        row8 = lax.broadcasted_iota(jnp.int32, (SUBLANES, LRU_WIDTH), 0)

        def tile(tix, carry):
            rs = pl.ds(pl.multiple_of(tix * SUBLANES, SUBLANES), SUBLANES)
            av, bv = a_s[rs, :], b_s[rs, :]
            for k in (1, 2, 4):
                keep = row8 >= k
                bv = jnp.where(keep, av * pltpu.roll(bv, k, 0) + bv, bv)
                av = jnp.where(keep, av * pltpu.roll(av, k, 0), av)
            h8 = av * carry + bv
            hs_ref[rs, :] = h8
            return jnp.broadcast_to(h8[SUBLANES - 1:SUBLANES, :], (SUBLANES, LRU_WIDTH))

        hcar[...] = lax.fori_loop(0, tb // SUBLANES, tile, hcar[...])
        utail[...] = u[tb - SUBLANES:, :]

    full2 = lambda shape: pl.BlockSpec(shape, lambda i: (0, 0))
    full3 = lambda shape: pl.BlockSpec(shape, lambda i: (0, 0, 0))
    blk = pl.BlockSpec((tb, LRU_WIDTH), lambda i: (i, 0))
    return pl.pallas_call(
        body, name="rglru_fwd", grid=(rows // tb,),
        in_specs=[blk, full2((CONV_WIDTH, LRU_WIDTH)), full2((1, LRU_WIDTH)),
                  full3((LRU_BLOCKS, LRU_BLOCK, LRU_BLOCK)), full2((1, LRU_WIDTH)),
                  full3((LRU_BLOCKS, LRU_BLOCK, LRU_BLOCK)), full2((1, LRU_WIDTH)), full2((1, LRU_WIDTH))],
        out_specs=blk,
        out_shape=jax.ShapeDtypeStruct((rows, LRU_WIDTH), F32),
        scratch_shapes=[pltpu.VMEM((SUBLANES, LRU_WIDTH), F32), pltpu.VMEM((SUBLANES, LRU_WIDTH), F32),
                        pltpu.VMEM((tb, LRU_WIDTH), F32), pltpu.VMEM((tb, LRU_WIDTH), F32)],
        compiler_params=_cparams(("arbitrary",)),
    )(u, cw, cb, wrg, brg, wig, big, lam)


def _rglru_bwd(u, hs, dhs, cw, cb, wrg, brg, wig, big, lam):
    rows = u.shape[0]
    tb = _row_block(rows)
    nblk = rows // tb
    tiles = tb // SUBLANES

    def body(u_ref, up_ref, hs_ref, hp_ref, dhs_ref, cw_ref, cb_ref, wrg_ref, brg_ref, wig_ref, big_ref, lam_ref,
             du_ref, dcw_ref, dcb_ref, dwrg_ref, dbrg_ref, dwig_ref, dbig_ref, dlam_ref,
             gcar, duc_head, a_s, b_s, g_s, dsp_acc):
        step = pl.program_id(0)
        blk_ix = nblk - 1 - step

        @pl.when(step == 0)
        def _():
            for ref in (dcw_ref, dcb_ref, dwrg_ref, dbrg_ref, dwig_ref, dbig_ref, gcar, duc_head, dsp_acc):
                ref[...] = jnp.zeros_like(ref)

        first = blk_ix == 0
        u = u_ref[...]
        u_prev = jnp.where(first, 0.0, up_ref[...])
        h_prev_tile = jnp.where(first, 0.0, hp_ref[...])
        gt = _lru_gates(u, u_prev, cw_ref, cb_ref, wrg_ref, brg_ref, wig_ref, big_ref, lam_ref, first)
        a, r, ig, uc, mult = gt["a"], gt["r"], gt["ig"], gt["uc"], gt["mult"]
        dhs_v = dhs_ref[...]

        a_s[...] = a
        b_s[...] = a * dhs_v
        row8 = lax.broadcasted_iota(jnp.int32, (SUBLANES, LRU_WIDTH), 0)

        def tile(tix, carry):
            rs = pl.ds(pl.multiple_of((tiles - 1 - tix) * SUBLANES, SUBLANES), SUBLANES)
            av, bv = a_s[rs, :], b_s[rs, :]
            for k in (1, 2, 4):
                keep = row8 < SUBLANES - k
                bv = jnp.where(keep, av * pltpu.roll(bv, SUBLANES - k, 0) + bv, bv)
                av = jnp.where(keep, av * pltpu.roll(av, SUBLANES - k, 0), av)
            g8 = av * carry + bv
            g_s[rs, :] = g8
            return jnp.broadcast_to(g8[0:1, :], (SUBLANES, LRU_WIDTH))

        g_next = gcar[...]
        gcar[...] = lax.fori_loop(0, tiles, tile, g_next)
        g = dhs_v + _shift_up(g_s[...], g_next, 1)

        h_prev = _shift_down(h_prev_tile, hs_ref[...], 1)
        da = g * h_prev
        iu = ig * uc
        dmult = jnp.where(gt["is_start"], 0.0, g * iu)
        d_ig = g * mult * uc
        duc = g * mult * ig
        dlog_a = da * a - dmult * (a * a) / gt["mult_raw"]
        dzr = (dlog_a * ((-LRU_C) * gt["sp"])) * r * (1.0 - r)
        dsp_acc[...] += jnp.sum(dlog_a * ((-LRU_C) * r), axis=0, keepdims=True)
        dzi = d_ig * ig * (1.0 - ig)
        dbrg_ref[...] += jnp.sum(dzr, axis=0, keepdims=True)
        dbig_ref[...] += jnp.sum(dzi, axis=0, keepdims=True)
        dzr_b, dzi_b = dzr.astype(BF16), dzi.astype(BF16)
        ub = gt["ub"]
        duc_parts = []
        for gi in range(LRU_BLOCKS):
            cs = slice(gi * LRU_BLOCK, (gi + 1) * LRU_BLOCK)
            dwrg_ref[gi] += lax.dot_general(ub[:, cs], dzr_b[:, cs], TN, preferred_element_type=F32)
            dwig_ref[gi] += lax.dot_general(ub[:, cs], dzi_b[:, cs], TN, preferred_element_type=F32)
            duc_parts.append(lax.dot_general(dzr_b[:, cs], wrg_ref[gi], NT, preferred_element_type=F32)
                             + lax.dot_general(dzi_b[:, cs], wig_ref[gi], NT, preferred_element_type=F32))
        duc = duc + jnp.concatenate(duc_parts, axis=1)

        dcb_ref[...] += jnp.sum(duc, axis=0, keepdims=True)
        taps = gt["taps"]
        for jt in range(CONV_WIDTH):
            dcw_ref[jt:jt + 1, :] += jnp.sum(duc * taps[jt], axis=0, keepdims=True)
        head = duc_head[...]
        du = duc * cw_ref[CONV_WIDTH - 1:CONV_WIDTH, :]
        for jt in range(CONV_WIDTH - 1):
            du = du + _shift_up(duc, head, CONV_WIDTH - 1 - jt) * cw_ref[jt:jt + 1, :]
        du_ref[...] = du
        duc_head[...] = duc[:SUBLANES, :]

        @pl.when(step == nblk - 1)
        def _():
            dlam_ref[...] = -dsp_acc[...] * jax.nn.sigmoid(-lam_ref[...])

    full2 = lambda shape: pl.BlockSpec(shape, lambda s: (0, 0))
    full3 = lambda shape: pl.BlockSpec(shape, lambda s: (0, 0, 0))
    blk = pl.BlockSpec((tb, LRU_WIDTH), lambda s: (nblk - 1 - s, 0))
    prev_tile = pl.BlockSpec((SUBLANES, LRU_WIDTH), lambda s: (jnp.maximum((nblk - 1 - s) * tiles - 1, 0), 0))
    wshape = (LRU_BLOCKS, LRU_BLOCK, LRU_BLOCK)
    return pl.pallas_call(
        body, name="rglru_bwd", grid=(nblk,),
        in_specs=[blk, prev_tile, blk, prev_tile, blk, full2((CONV_WIDTH, LRU_WIDTH)), full2((1, LRU_WIDTH)),
                  full3(wshape), full2((1, LRU_WIDTH)), full3(wshape), full2((1, LRU_WIDTH)), full2((1, LRU_WIDTH))],
        out_specs=[blk, full2((CONV_WIDTH, LRU_WIDTH)), full2((1, LRU_WIDTH)), full3(wshape), full2((1, LRU_WIDTH)),
                   full3(wshape), full2((1, LRU_WIDTH)), full2((1, LRU_WIDTH))],
        out_shape=[jax.ShapeDtypeStruct((rows, LRU_WIDTH), F32),
                   jax.ShapeDtypeStruct((CONV_WIDTH, LRU_WIDTH), F32), jax.ShapeDtypeStruct((1, LRU_WIDTH), F32),
                   jax.ShapeDtypeStruct(wshape, F32), jax.ShapeDtypeStruct((1, LRU_WIDTH), F32),
                   jax.ShapeDtypeStruct(wshape, F32), jax.ShapeDtypeStruct((1, LRU_WIDTH), F32),
                   jax.ShapeDtypeStruct((1, LRU_WIDTH), F32)],
        scratch_shapes=[pltpu.VMEM((SUBLANES, LRU_WIDTH), F32), pltpu.VMEM((SUBLANES, LRU_WIDTH), F32),
                        pltpu.VMEM((tb, LRU_WIDTH), F32), pltpu.VMEM((tb, LRU_WIDTH), F32),
                        pltpu.VMEM((tb, LRU_WIDTH), F32), pltpu.VMEM((1, LRU_WIDTH), F32)],
        compiler_params=_cparams(("arbitrary",)),
    )(u, u, hs, hs, dhs, cw, cb, wrg, brg, wig, big, lam)


def _final_loss(h, gf, target, n_real):
    rows = h.shape[0]
    tr = _row_block(rows)

    def body(h_ref, g_ref, t_ref, dh_ref, loss_ref, dg_ref):
        i = pl.program_id(0)

        @pl.when(i == 0)
        def _():
            loss_ref[...] = jnp.zeros_like(loss_ref)
            dg_ref[...] = jnp.zeros_like(dg_ref)

        gv = g_ref[...]
        xn, r = _rms_fwd(h_ref[...])
        row = i * tr + lax.broadcasted_iota(jnp.int32, (tr, 1), 0)
        live = jnp.logical_and(row >= N_META, row < n_real)
        err = jnp.where(live, xn * gv - t_ref[...], 0.0)
        loss_ref[...] += (0.5 / D_MODEL) * jnp.sum(jnp.sum(err * err, axis=1, keepdims=True), axis=0, keepdims=True)
        dx, dg = _rms_bwd(err * (1.0 / D_MODEL), xn, r, gv)
        dg_ref[...] += dg
        dh_ref[...] = dx

    blk = pl.BlockSpec((tr, D_MODEL), lambda i: (i, 0))
    return pl.pallas_call(
        body, name="final_loss", grid=(rows // tr,),
        in_specs=[blk, pl.BlockSpec((1, D_MODEL), lambda i: (0, 0)), blk],
        out_specs=[blk, pl.BlockSpec((1, 1), lambda i: (0, 0)), pl.BlockSpec((1, D_MODEL), lambda i: (0, 0))],
        out_shape=[jax.ShapeDtypeStruct((rows, D_MODEL), F32), jax.ShapeDtypeStruct((1, 1), F32),
                   jax.ShapeDtypeStruct((1, D_MODEL), F32)],
        compiler_params=_cparams(("arbitrary",)),
    )(h, gf, target)


def _my_place():
    x, y, c = lax.axis_index("x"), lax.axis_index("y"), lax.axis_index("c")
    return x, y, c, 4 * x + 2 * y + c


def _peer(x, y, c, k):
    px, py, pc = x ^ (k >> 2), y ^ ((k >> 1) & 1), c ^ (k & 1)
    return (px, py, pc), 4 * px + 2 * py + pc


def _all_gather(big, small):
    def body(big_ref, small_ref, obig_ref, osmall_ref, send_sems, recv_sems, local_sems):
        x, y, c, me = _my_place()
        own = [pltpu.make_async_copy(big_ref, obig_ref.at[me], local_sems.at[0]),
               pltpu.make_async_copy(small_ref, osmall_ref.at[me], local_sems.at[1])]
        for cp in own:
            cp.start()
        copies = []
        for k in range(1, N_DEV):
            peer, _ = _peer(x, y, c, k)
            for part, (src, dst) in enumerate(((big_ref, obig_ref), (small_ref, osmall_ref))):
                copies.append(pltpu.make_async_remote_copy(
                    src_ref=src, dst_ref=dst.at[me], send_sem=send_sems.at[part, k], recv_sem=recv_sems.at[part, k],
                    device_id=peer, device_id_type=MESH))
        for cp in copies:
            cp.start()
        for cp in copies:
            cp.wait()
        for cp in own:
            cp.wait()

    n = big.shape[0]
    hbm = pl.BlockSpec(memory_space=pl.ANY)
    return pl.pallas_call(
        body, name="weight_all_gather",
        in_specs=[hbm, hbm], out_specs=[hbm, hbm],
        out_shape=[jax.ShapeDtypeStruct((N_DEV, n, LANES), BF16), jax.ShapeDtypeStruct((N_DEV,) + small.shape, F32)],
        scratch_shapes=[pltpu.SemaphoreType.DMA((2, N_DEV)), pltpu.SemaphoreType.DMA((2, N_DEV)),
                        pltpu.SemaphoreType.DMA((2,))],
        compiler_params=pltpu.CompilerParams(has_side_effects=True),
    )(big, small)


GRAD_CHUNK = 32


def _grad_exchange(gbig, rep):
    n = gbig.shape[1]
    nrep = rep.shape[0]

    def body(gbig_ref, rep_ref, out_ref, orep_ref, land, land_rep, send_sems, recv_sems, local_sems):
        x, y, c, me = _my_place()
        own = [pltpu.make_async_copy(gbig_ref.at[me], land.at[me], local_sems.at[0]),
               pltpu.make_async_copy(rep_ref, land_rep.at[me], local_sems.at[1])]
        for cp in own:
            cp.start()
        copies = []
        for k in range(1, N_DEV):
            peer, pid = _peer(x, y, c, k)
            copies.append(pltpu.make_async_remote_copy(
                src_ref=gbig_ref.at[pid], dst_ref=land.at[me], send_sem=send_sems.at[0, k],
                recv_sem=recv_sems.at[0, k], device_id=peer, device_id_type=MESH))
            copies.append(pltpu.make_async_remote_copy(
                src_ref=rep_ref, dst_ref=land_rep.at[me], send_sem=send_sems.at[1, k],
                recv_sem=recv_sems.at[1, k], device_id=peer, device_id_type=MESH))
        for cp in copies:
            cp.start()
        for cp in copies:
            cp.wait()
        for cp in own:
            cp.wait()

        def chunk(ci, carry):
            rs = pl.ds(pl.multiple_of(ci * GRAD_CHUNK, GRAD_CHUNK), GRAD_CHUNK)
            acc = land[0, rs, :].astype(F32)
            for d in range(1, N_DEV):
                acc = acc + land[d, rs, :].astype(F32)
            out_ref[rs, :] = acc
            return carry

        lax.fori_loop(0, n // GRAD_CHUNK, chunk, 0)
        acc = land_rep[0]
        for d in range(1, N_DEV):
            acc = acc + land_rep[d]
        orep_ref[...] = acc

    return pl.pallas_call(
        body, name="grad_exchange",
        in_specs=[pl.BlockSpec(memory_space=pl.ANY), pl.BlockSpec(memory_space=pl.ANY)],
        out_specs=[pl.BlockSpec(memory_space=pltpu.VMEM), pl.BlockSpec(memory_space=pltpu.VMEM)],
        out_shape=[jax.ShapeDtypeStruct((n, LANES), F32), jax.ShapeDtypeStruct((nrep, LANES), F32)],
        scratch_shapes=[pltpu.VMEM((N_DEV, n, LANES), BF16), pltpu.VMEM((N_DEV, nrep, LANES), F32),
                        pltpu.SemaphoreType.DMA((2, N_DEV)), pltpu.SemaphoreType.DMA((2, N_DEV)),
                        pltpu.SemaphoreType.DMA((2,))],
        compiler_params=pltpu.CompilerParams(vmem_limit_bytes=VMEM_LIMIT, has_side_effects=True),
    )(gbig, rep)


def _adamw_all(ws, gs, ms, vs):
    n = len(ws)

    def body(*refs):
        w_refs, g_refs, m_refs, v_refs = refs[0:n], refs[n:2 * n], refs[2 * n:3 * n], refs[3 * n:4 * n]
        d_refs, nm_refs, nv_refs = refs[4 * n:5 * n], refs[5 * n:6 * n], refs[6 * n:7 * n]
        for w_ref, g_ref, m_ref, v_ref, d_ref, nm_ref, nv_ref in zip(w_refs, g_refs, m_refs, v_refs, d_refs, nm_refs, nv_refs):
            g = g_ref[...]
            m = ADAM_B1 * m_ref[...] + (1.0 - ADAM_B1) * g
            v = ADAM_B2 * v_ref[...] + (1.0 - ADAM_B2) * jnp.square(g)
            m_hat = m / (1.0 - ADAM_B1 ** ADAM_STEP)
            v_hat = v / (1.0 - ADAM_B2 ** ADAM_STEP)
            d_ref[...] = -ADAM_LR * (m_hat / (jnp.sqrt(v_hat) + ADAM_EPS) + ADAM_WD * w_ref[...])
            nm_ref[...] = m
            nv_ref[...] = v

    shapes = [jax.ShapeDtypeStruct(w.shape, F32) for w in ws]
    outs = pl.pallas_call(
        body, name="adamw", out_shape=shapes * 3,
        compiler_params=pltpu.CompilerParams(vmem_limit_bytes=VMEM_LIMIT),
    )(*ws, *gs, *ms, *vs)
    return outs[0:n], outs[n:2 * n], outs[2 * n:3 * n]


BIG = (("a_w_in", 1728), ("a_w_uq", 576), ("a_w_ukv", 512), ("a_w_out", 1024), ("b_w_in", 2048),
       ("b_w_rg", 256), ("b_w_ig", 256), ("b_w_out", 1024))
SMALL = (("meta_tokens", 16), ("b_norm_g", 1), ("b_conv_w", 4), ("b_conv_b", 1), ("b_b_rg", 1), ("b_b_ig", 1),
         ("b_lam", 1))
REP = (("a_norm_g", 8), ("a_q_norm_g", 3), ("a_kv_norm_g", 2), ("final_norm_g", 8))
SLOT = 16


def _offsets(table, slot=1, start=0):
    out, o = {}, start
    for name, n in table:
        out[name] = (o, n)
        o += -(-n // slot) * slot
    return out, o


def _slotted(a, axis):
    pad = -a.shape[axis] % SLOT
    if not pad:
        return a
    widths = [(0, 0)] * a.ndim
    widths[axis] = (0, pad)
    return jnp.pad(a, widths)


def _rope_tables(rows):
    pos = jnp.arange(rows, dtype=F32)
    inv_freq = ROPE_BASE ** (-jnp.arange(0, QK_ROPE, 2, dtype=F32) / QK_ROPE)
    ang = pos[:, None] * inv_freq[None, :]
    cos, sin = jnp.cos(ang), jnp.sin(ang)
    zeros = jnp.zeros((rows, LANES - QK_ROPE), F32)
    return jnp.concatenate([cos, cos, zeros], axis=1), jnp.concatenate([-sin, sin, zeros], axis=1)


def kernel(x, meta_tokens, a_norm_g, a_w_in, a_q_norm_g, a_kv_norm_g, a_w_uq, a_w_ukv, a_w_out, b_norm_g, b_w_in, b_conv_w, b_conv_b, b_w_rg, b_b_rg, b_w_ig, b_b_ig, b_lam, b_w_out, final_norm_g, loss_target, m_meta_tokens, m_a_norm_g, m_a_w_in, m_a_q_norm_g, m_a_kv_norm_g, m_a_w_uq, m_a_w_ukv, m_a_w_out, m_b_norm_g, m_b_w_in, m_b_conv_w, m_b_conv_b, m_b_w_rg, m_b_b_rg, m_b_w_ig, m_b_b_ig, m_b_lam, m_b_w_out, m_final_norm_g, v_meta_tokens, v_a_norm_g, v_a_w_in, v_a_q_norm_g, v_a_kv_norm_g, v_a_w_uq, v_a_w_ukv, v_a_w_out, v_b_norm_g, v_b_w_in, v_b_conv_w, v_b_conv_b, v_b_w_rg, v_b_b_rg, v_b_w_ig, v_b_b_ig, v_b_lam, v_b_w_out, v_final_norm_g):
    names = ("meta_tokens", "a_norm_g", "a_w_in", "a_q_norm_g", "a_kv_norm_g", "a_w_uq", "a_w_ukv", "a_w_out",
             "b_norm_g", "b_w_in", "b_conv_w", "b_conv_b", "b_w_rg", "b_b_rg", "b_w_ig", "b_b_ig", "b_lam", "b_w_out",
             "final_norm_g")
    w = dict(zip(names, (meta_tokens, a_norm_g, a_w_in, a_q_norm_g, a_kv_norm_g, a_w_uq, a_w_ukv, a_w_out, b_norm_g,
                         b_w_in, b_conv_w, b_conv_b, b_w_rg, b_b_rg, b_w_ig, b_b_ig, b_lam, b_w_out, final_norm_g)))
    mom_m = dict(zip(names, (m_meta_tokens, m_a_norm_g, m_a_w_in, m_a_q_norm_g, m_a_kv_norm_g, m_a_w_uq, m_a_w_ukv,
                             m_a_w_out, m_b_norm_g, m_b_w_in, m_b_conv_w, m_b_conv_b, m_b_w_rg, m_b_b_rg, m_b_w_ig,
                             m_b_b_ig, m_b_lam, m_b_w_out, m_final_norm_g)))
    mom_v = dict(zip(names, (v_meta_tokens, v_a_norm_g, v_a_w_in, v_a_q_norm_g, v_a_kv_norm_g, v_a_w_uq, v_a_w_ukv,
                             v_a_w_out, v_b_norm_g, v_b_w_in, v_b_conv_w, v_b_conv_b, v_b_w_rg, v_b_b_rg, v_b_w_ig,
                             v_b_b_ig, v_b_lam, v_b_w_out, v_final_norm_g)))

    seq = x.shape[1]
    n_real = N_META + seq
    rows = -(-n_real // LANES) * LANES
    scale = (QK_NOPE + QK_ROPE) ** -0.5
    big_off, big_rows = _offsets(BIG)
    small_off, _ = _offsets(SMALL, SLOT)
    gsmall_off, grad_rows = _offsets(SMALL, SLOT, big_rows)
    grad_rows = -(-grad_rows // GRAD_CHUNK) * GRAD_CHUNK
    rep_off, _ = _offsets(REP, SLOT)

    send_big = jnp.concatenate([w[nm].reshape(-1, LANES) for nm, _ in BIG], axis=0).astype(BF16)
    send_small = jnp.concatenate([_slotted(w[nm].reshape(-1, LANES), 0) for nm, _ in SMALL], axis=0)
    all_big, all_small = _all_gather(send_big, send_small)

    def big_seg(nm):
        o, n = big_off[nm]
        return all_big[:, o:o + n, :]

    def small_seg(nm):
        o, n = small_off[nm]
        return all_small[:, o:o + n, :]

    def cols(seg, r, cdev):
        return seg.reshape(N_DEV, r, cdev).transpose(1, 0, 2).reshape(r, N_DEV * cdev)

    w_in_a = cols(big_seg("a_w_in"), D_MODEL, 216)
    w_in_a = jnp.concatenate([w_in_a[:, :LAT + QK_ROPE], jnp.zeros((D_MODEL, LAT_PAD - LAT - QK_ROPE), BF16),
                              w_in_a[:, LAT + QK_ROPE:]], axis=1)
    w_uq = jnp.pad(big_seg("a_w_uq").reshape(N_DEV, Q_LORA, QK_NOPE + QK_ROPE).transpose(1, 0, 2),
                   ((0, 0), (0, 0), (0, HEAD_PAD - QK_NOPE - QK_ROPE))).reshape(Q_LORA, HEADS * HEAD_PAD)
    w_ukv = cols(big_seg("a_w_ukv"), KV_LORA, QK_NOPE + V_HEAD)
    w_out_a = big_seg("a_w_out").reshape(D_MODEL, D_MODEL)
    w_in_b = cols(big_seg("b_w_in"), D_MODEL, 2 * LRU_WIDTH // N_DEV)
    w_rg = big_seg("b_w_rg").reshape(N_DEV, LRU_BLOCKS, LRU_BLOCK // N_DEV, LRU_BLOCK).transpose(1, 0, 2, 3).reshape(
        LRU_BLOCKS, LRU_BLOCK, LRU_BLOCK)
    w_ig = big_seg("b_w_ig").reshape(N_DEV, LRU_BLOCKS, LRU_BLOCK // N_DEV, LRU_BLOCK).transpose(1, 0, 2, 3).reshape(
        LRU_BLOCKS, LRU_BLOCK, LRU_BLOCK)
    w_out_b = big_seg("b_w_out").reshape(D_MODEL, D_MODEL)
    meta_full = small_seg("meta_tokens").transpose(1, 0, 2).reshape(N_META, D_MODEL)
    vec = lambda nm: small_seg(nm).reshape(1, D_MODEL)
    g_b, conv_b, b_rg, b_ig, lam = vec("b_norm_g"), vec("b_conv_b"), vec("b_b_rg"), vec("b_b_ig"), vec("b_lam")
    conv_w = small_seg("b_conv_w").transpose(1, 0, 2).reshape(CONV_WIDTH, LRU_WIDTH)
    g_a, g_q, g_kv = a_norm_g, a_q_norm_g, a_kv_norm_g
    g_f = final_norm_g.reshape(1, D_MODEL)

    h0 = jnp.concatenate([meta_full, x[0], jnp.zeros((rows - n_real, D_MODEL), F32)], axis=0)
    target = jnp.concatenate([jnp.zeros((N_META, D_MODEL), F32), loss_target[0],
                              jnp.zeros((rows - n_real, D_MODEL), F32)], axis=0)
    cos, sin = _rope_tables(rows)

    lat, gate_a = _norm_proj_fwd(h0, g_a, w_in_a, LAT_PAD, "a_in_fwd")
    qc, kc, v, vt = _mla_qkv_fwd(lat, g_q, g_kv, w_uq, w_ukv, cos, sin, scale)
    o, lse = _attn_fwd(qc, kc, vt)
    h1 = _gated_out_fwd(o, gate_a, h0, w_out_a, "a_out_fwd")
    u, gate_b = _norm_proj_fwd(h1, g_b, w_in_b, LRU_WIDTH, "b_in_fwd")
    hs = _rglru_fwd(u, conv_w, conv_b, w_rg, b_rg, w_ig, b_ig, lam)
    h2 = _gated_out_fwd(hs, gate_b, h1, w_out_b, "b_out_fwd")
    dh2, loss_part, dg_f = _final_loss(h2, g_f, target, n_real)

    dhs, dgate_b, dw_out_b = _gated_out_bwd(hs, gate_b, dh2, w_out_b, F32, False, "b_out_bwd")
    du, dconv_w, dconv_b, dw_rg, db_rg, dw_ig, db_ig, dlam = _rglru_bwd(u, hs, dhs, conv_w, conv_b, w_rg, b_rg, w_ig,
                                                                       b_ig, lam)
    dh1, dw_in_b, dg_b = _norm_proj_bwd(h1, g_b, w_in_b, du, dgate_b, dh2, "b_in_bwd")
    do, dgate_a, dw_out_a, delta = _gated_out_bwd(o, gate_a, dh1, w_out_a, BF16, True, "a_out_bwd")
    dqc, dkc, dv = _attn_bwd(qc, kc, v, lse, delta, do)
    dlat, dw_uq, dw_ukv, dg_q, dg_kv = _mla_qkv_bwd(lat, g_q, g_kv, w_uq, w_ukv, cos, sin, dqc, dkc, dv, scale)
    dh0, dw_in_a, dg_a = _norm_proj_bwd(h0, g_a, w_in_a, dlat, dgate_a, dh1, "a_in_bwd")

    loss = lax.psum(loss_part[0, 0], ("x", "y", "c"))
    grad_x = dh0[N_META:n_real][None]

    def to_cols(g, cdev):
        r = g.shape[0]
        return g.reshape(r, N_DEV, cdev).transpose(1, 0, 2).reshape(N_DEV, -1, LANES)

    dw_in_a_nat = jnp.concatenate([dw_in_a[:, :LAT + QK_ROPE], dw_in_a[:, LAT_PAD:]], axis=1)
    gparts = {
        "a_w_in": to_cols(dw_in_a_nat, 216),
        "a_w_uq": dw_uq.reshape(Q_LORA, HEADS, HEAD_PAD)[:, :, :QK_NOPE + QK_ROPE].transpose(1, 0, 2).reshape(
            N_DEV, -1, LANES),
        "a_w_ukv": to_cols(dw_ukv, QK_NOPE + V_HEAD),
        "a_w_out": dw_out_a.reshape(N_DEV, -1, LANES),
        "b_w_in": to_cols(dw_in_b, 2 * LRU_WIDTH // N_DEV),
        "b_w_rg": dw_rg.reshape(LRU_BLOCKS, N_DEV, LRU_BLOCK // N_DEV, LRU_BLOCK).transpose(1, 0, 2, 3).reshape(
            N_DEV, -1, LANES),
        "b_w_ig": dw_ig.reshape(LRU_BLOCKS, N_DEV, LRU_BLOCK // N_DEV, LRU_BLOCK).transpose(1, 0, 2, 3).reshape(
            N_DEV, -1, LANES),
        "b_w_out": dw_out_b.reshape(N_DEV, -1, LANES),
        "meta_tokens": to_cols(dh0[:N_META], LANES),
        "b_norm_g": to_cols(dg_b, LANES), "b_conv_w": to_cols(dconv_w, LANES), "b_conv_b": to_cols(dconv_b, LANES),
        "b_b_rg": to_cols(db_rg, LANES), "b_b_ig": to_cols(db_ig, LANES), "b_lam": to_cols(dlam, LANES),
    }
    pieces = [gparts[nm].astype(BF16) for nm, _ in BIG] + [_slotted(gparts[nm].astype(BF16), 1) for nm, _ in SMALL]
    used = sum(p.shape[1] for p in pieces)
    if grad_rows > used:
        pieces.append(jnp.zeros((N_DEV, grad_rows - used, LANES), BF16))
    gbig = jnp.concatenate(pieces, axis=1)
    rep_parts = {"a_norm_g": dg_a, "a_q_norm_g": dg_q, "a_kv_norm_g": dg_kv, "final_norm_g": dg_f}
    rep = jnp.concatenate([_slotted(rep_parts[nm].reshape(-1, LANES), 0) for nm, _ in REP], axis=0)
    gsum, rep_sum = _grad_exchange(gbig, rep)

    grads = {}
    for off, src in ((big_off, gsum), (gsmall_off, gsum), (rep_off, rep_sum)):
        for nm, (o_r, n) in off.items():
            grads[nm] = src[o_r:o_r + n].reshape(w[nm].shape)

    as2d = lambda a: a.reshape(1, -1) if a.ndim == 1 else a
    deltas, new_ms, new_vs = _adamw_all([as2d(w[nm]) for nm in names], [as2d(grads[nm]) for nm in names],
                                        [as2d(mom_m[nm]) for nm in names], [as2d(mom_v[nm]) for nm in names])
    shaped = lambda arrs: [a.reshape(w[nm].shape) for a, nm in zip(arrs, names)]
    return (loss, grad_x, *[grads[nm] for nm in names], *shaped(deltas), *shaped(new_ms), *shaped(new_vs))
```

```python
import functools

import jax
import jax.numpy as jnp
from jax import lax
from jax.experimental import pallas as pl
from jax.experimental.pallas import tpu as pltpu

F32 = jnp.float32
BF16 = jnp.bfloat16

D_MODEL = 1024
N_META = 16
RMS_EPS = 1e-6
HEADS = 8
QK_NOPE = 128
QK_ROPE = 64
V_HEAD = 128
Q_LORA = 384
KV_LORA = 256
HEAD_PAD = 256
LAT = Q_LORA + KV_LORA
LAT_PAD = LAT + 128
ROPE_BASE = 10000.0
MASK_VALUE = -1e30
LRU_WIDTH = 1024
LRU_BLOCKS = 4
LRU_BLOCK = 256
CONV_WIDTH = 4
LRU_C = 8.0
N_DEV = 8
ADAM_LR, ADAM_B1, ADAM_B2, ADAM_EPS, ADAM_WD, ADAM_STEP = 0.001, 0.9, 0.999, 1e-08, 0.01, 10

LANES = 128
SUBLANES = 8
VMEM_LIMIT = 56 * 1024 * 1024
MESH = pl.DeviceIdType.MESH

NT = (((1,), (1,)), ((), ()))
TN = (((0,), (0,)), ((), ()))


def _row_block(rows):
    return 384 if rows % 384 == 0 else 128


def _cparams(sem):
    return pltpu.CompilerParams(dimension_semantics=sem, vmem_limit_bytes=VMEM_LIMIT)


def _silu(x):
    return x * jax.nn.sigmoid(x)


def _dsilu(x):
    s = jax.nn.sigmoid(x)
    return s * (1.0 + x * (1.0 - s))


def _rms_fwd(x):
    r = lax.rsqrt(jnp.mean(x * x, axis=-1, keepdims=True) + RMS_EPS)
    return x * r, r


def _rms_bwd(dy, xn, r, g):
    t = dy * g
    dx = r * (t - xn * jnp.mean(t * xn, axis=-1, keepdims=True))
    return dx, jnp.sum(dy * xn, axis=0, keepdims=True)


def _expm1_neg(x):
    small = x * (1.0 + x * (1 / 2) * (1.0 + x * (1 / 3) * (1.0 + x * (1 / 4) * (1.0 + x * (1 / 5) * (
        1.0 + x * (1 / 6) * (1.0 + x * (1 / 7)))))))
    return jnp.where(x > -0.25, small, jnp.exp(x) - 1.0)


def _softplus_neg(lam):
    z = jnp.exp(-jnp.abs(lam))
    w = z / (2.0 + z)
    w2 = w * w
    series = 2.0 * w * (1.0 + w2 * (1 / 3) + w2 * w2 * (1 / 5))
    return jnp.maximum(-lam, 0.0) + jnp.where(z < 0.1, series, jnp.log(1.0 + z))


def _norm_proj_fwd(h, g, w, n1, name):
    rows, n = h.shape[0], w.shape[1]
    tr = _row_block(rows)

    def body(h_ref, g_ref, w_ref, p1_ref, p2_ref):
        xn, _ = _rms_fwd(h_ref[...])
        hn = (xn * g_ref[...]).astype(BF16)
        p = jnp.dot(hn, w_ref[...], preferred_element_type=F32)
        p1_ref[...] = p[:, :n1]
        p2_ref[...] = p[:, n1:]

    return pl.pallas_call(
        body, name=name, grid=(rows // tr,),
        in_specs=[pl.BlockSpec((tr, D_MODEL), lambda i: (i, 0)),
                  pl.BlockSpec((1, D_MODEL), lambda i: (0, 0)),
                  pl.BlockSpec((D_MODEL, n), lambda i: (0, 0))],
        out_specs=[pl.BlockSpec((tr, n1), lambda i: (i, 0)),
                   pl.BlockSpec((tr, n - n1), lambda i: (i, 0))],
        out_shape=[jax.ShapeDtypeStruct((rows, n1), F32), jax.ShapeDtypeStruct((rows, n - n1), F32)],
        compiler_params=_cparams(("parallel",)),
    )(h, g, w)


def _norm_proj_bwd(h, g, w, dp1, dp2, dh_in, name):
    rows, n = h.shape[0], w.shape[1]
    n1 = dp1.shape[1]
    tr = _row_block(rows)

    def body(h_ref, g_ref, w_ref, dp1_ref, dp2_ref, dhin_ref, dh_ref, dw_ref, dg_ref):
        @pl.when(pl.program_id(0) == 0)
        def _():
            dw_ref[...] = jnp.zeros_like(dw_ref)
            dg_ref[...] = jnp.zeros_like(dg_ref)

        gv = g_ref[...]
        xn, r = _rms_fwd(h_ref[...])
        hn = (xn * gv).astype(BF16)
        dp = jnp.concatenate([dp1_ref[...].astype(BF16), dp2_ref[...].astype(BF16)], axis=1)
        dw_ref[...] += lax.dot_general(hn, dp, TN, preferred_element_type=F32)
        dhn = lax.dot_general(dp, w_ref[...], NT, preferred_element_type=F32)
        dx, dg = _rms_bwd(dhn, xn, r, gv)
        dg_ref[...] += dg
        dh_ref[...] = dhin_ref[...] + dx

    return pl.pallas_call(
        body, name=name, grid=(rows // tr,),
        in_specs=[pl.BlockSpec((tr, D_MODEL), lambda i: (i, 0)),
                  pl.BlockSpec((1, D_MODEL), lambda i: (0, 0)),
                  pl.BlockSpec((D_MODEL, n), lambda i: (0, 0)),
                  pl.BlockSpec((tr, n1), lambda i: (i, 0)),
                  pl.BlockSpec((tr, n - n1), lambda i: (i, 0)),
                  pl.BlockSpec((tr, D_MODEL), lambda i: (i, 0))],
        out_specs=[pl.BlockSpec((tr, D_MODEL), lambda i: (i, 0)),
                   pl.BlockSpec((D_MODEL, n), lambda i: (0, 0)),
                   pl.BlockSpec((1, D_MODEL), lambda i: (0, 0))],
        out_shape=[jax.ShapeDtypeStruct((rows, D_MODEL), F32),
                   jax.ShapeDtypeStruct((D_MODEL, n), F32),
                   jax.ShapeDtypeStruct((1, D_MODEL), F32)],
        compiler_params=_cparams(("arbitrary",)),
    )(h, g, w, dp1, dp2, dh_in)


def _gated_out_fwd(a, gate, h, w, name):
    rows = a.shape[0]
    tr = _row_block(rows)

    def body(a_ref, gate_ref, h_ref, w_ref, o_ref):
        y = (a_ref[...] * _silu(gate_ref[...])).astype(BF16)
        o_ref[...] = h_ref[...] + jnp.dot(y, w_ref[...], preferred_element_type=F32)

    blk = pl.BlockSpec((tr, D_MODEL), lambda i: (i, 0))
    return pl.pallas_call(
        body, name=name, grid=(rows // tr,),
        in_specs=[blk, blk, blk, pl.BlockSpec((D_MODEL, D_MODEL), lambda i: (0, 0))],
        out_specs=blk,
        out_shape=jax.ShapeDtypeStruct((rows, D_MODEL), F32),
        compiler_params=_cparams(("parallel",)),
    )(a, gate, h, w)


def _gated_out_bwd(a, gate, dh, w, da_dtype, with_delta, name):
    rows = a.shape[0]
    tr = _row_block(rows)

    def body(a_ref, gate_ref, dh_ref, w_ref, da_ref, dgate_ref, dw_ref, *delta_ref):
        @pl.when(pl.program_id(0) == 0)
        def _():
            dw_ref[...] = jnp.zeros_like(dw_ref)

        av, gv = a_ref[...], gate_ref[...]
        sg = _silu(gv)
        dhb = dh_ref[...].astype(BF16)
        dw_ref[...] += lax.dot_general((av * sg).astype(BF16), dhb, TN, preferred_element_type=F32)
        dy = lax.dot_general(dhb, w_ref[...], NT, preferred_element_type=F32)
        da = (dy * sg).astype(da_dtype)
        da_ref[...] = da
        dgate_ref[...] = dy * av * _dsilu(gv)
        if with_delta:
            prod = da.astype(F32) * av
            lane = lax.broadcasted_iota(jnp.int32, (tr, LANES), 1)
            per_head = jnp.zeros((tr, LANES), F32)
            for hd in range(HEADS):
                dsum = jnp.sum(prod[:, hd * V_HEAD:(hd + 1) * V_HEAD], axis=1, keepdims=True)
                per_head = jnp.where(lane == hd, dsum, per_head)
            delta_t = per_head.T
            for hd in range(HEADS):
                delta_ref[0][hd, 0] = delta_t[hd:hd + 1, :]

    blk = pl.BlockSpec((tr, D_MODEL), lambda i: (i, 0))
    wblk = pl.BlockSpec((D_MODEL, D_MODEL), lambda i: (0, 0))
    out_specs = [blk, blk, wblk]
    out_shape = [jax.ShapeDtypeStruct((rows, D_MODEL), da_dtype),
                 jax.ShapeDtypeStruct((rows, D_MODEL), F32),
                 jax.ShapeDtypeStruct((D_MODEL, D_MODEL), F32)]
    if with_delta:
        out_specs.append(pl.BlockSpec((HEADS, 1, 1, tr), lambda i: (0, i, 0, 0)))
        out_shape.append(jax.ShapeDtypeStruct((HEADS, rows // tr, 1, tr), F32))
    return pl.pallas_call(
        body, name=name, grid=(rows // tr,),
        in_specs=[blk, blk, blk, wblk],
        out_specs=out_specs, out_shape=out_shape,
        compiler_params=_cparams(("arbitrary",)),
    )(a, gate, dh, w)


def _rope(v, cos, sin, lane):
    swapped = jnp.where(lane < QK_ROPE // 2, pltpu.roll(v, LANES - QK_ROPE // 2, 1), pltpu.roll(v, QK_ROPE // 2, 1))
    return v * cos + swapped * sin


def _unrope(dv, cos, sin, lane):
    t = dv * sin
    swapped = jnp.where(lane < QK_ROPE // 2, pltpu.roll(t, LANES - QK_ROPE // 2, 1), pltpu.roll(t, QK_ROPE // 2, 1))
    return dv * cos + swapped


def _mla_qkv_fwd(lat, gq, gkv, wuq, wukv, cos, sin, scale):
    rows = lat.shape[0]
    tr = _row_block(rows)

    def body(lat_ref, gq_ref, gkv_ref, wuq_ref, wukv_ref, cos_ref, sin_ref, qc_ref, kc_ref, v_ref, vt_ref):
        qn, _ = _rms_fwd(lat_ref[:, :Q_LORA])
        kvn, _ = _rms_fwd(lat_ref[:, Q_LORA:LAT])
        q = jnp.dot((qn * gq_ref[...]).astype(BF16), wuq_ref[...], preferred_element_type=F32)
        kv = jnp.dot((kvn * gkv_ref[...]).astype(BF16), wukv_ref[...], preferred_element_type=F32)
        c, s = cos_ref[...], sin_ref[...]
        lane = lax.broadcasted_iota(jnp.int32, (tr, LANES), 1)
        kr = _rope(lat_ref[:, LAT:LAT_PAD], c, s, lane).astype(BF16)
        for hd in range(HEADS):
            o = hd * HEAD_PAD
            qc_ref[:, o:o + QK_NOPE] = (q[:, o:o + QK_NOPE] * scale).astype(BF16)
            qc_ref[:, o + QK_NOPE:o + HEAD_PAD] = (_rope(q[:, o + QK_NOPE:o + HEAD_PAD], c, s, lane) * scale).astype(BF16)
            kc_ref[:, o:o + QK_NOPE] = kv[:, o:o + QK_NOPE].astype(BF16)
            kc_ref[:, o + QK_NOPE:o + HEAD_PAD] = kr
            vh = kv[:, o + QK_NOPE:o + HEAD_PAD]
            v_ref[:, hd * V_HEAD:(hd + 1) * V_HEAD] = vh.astype(BF16)
            vt_ref[hd, 0] = vh.T.astype(BF16)

    full = lambda shape: pl.BlockSpec(shape, lambda i: (0, 0))
    rowb = lambda n: pl.BlockSpec((tr, n), lambda i: (i, 0))
    return pl.pallas_call(
        body, name="mla_qkv_fwd", grid=(rows // tr,),
        in_specs=[rowb(LAT_PAD), full((1, Q_LORA)), full((1, KV_LORA)), full((Q_LORA, HEADS * HEAD_PAD)),
                  full((KV_LORA, HEADS * HEAD_PAD)), rowb(LANES), rowb(LANES)],
        out_specs=[rowb(HEADS * HEAD_PAD), rowb(HEADS * HEAD_PAD), rowb(HEADS * V_HEAD),
                   pl.BlockSpec((HEADS, 1, V_HEAD, tr), lambda i: (0, i, 0, 0))],
        out_shape=[jax.ShapeDtypeStruct((rows, HEADS * HEAD_PAD), BF16),
                   jax.ShapeDtypeStruct((rows, HEADS * HEAD_PAD), BF16),
                   jax.ShapeDtypeStruct((rows, HEADS * V_HEAD), BF16),
                   jax.ShapeDtypeStruct((HEADS, rows // tr, V_HEAD, tr), BF16)],
        compiler_params=_cparams(("parallel",)),
    )(lat, gq, gkv, wuq, wukv, cos, sin)


def _mla_qkv_bwd(lat, gq, gkv, wuq, wukv, cos, sin, dqc, dkc, dv, scale):
    rows = lat.shape[0]
    tr = _row_block(rows)

    def body(lat_ref, gq_ref, gkv_ref, wuq_ref, wukv_ref, cos_ref, sin_ref, dqc_ref, dkc_ref, dv_ref,
             dlat_ref, dwuq_ref, dwukv_ref, dgq_ref, dgkv_ref):
        @pl.when(pl.program_id(0) == 0)
        def _():
            dwuq_ref[...] = jnp.zeros_like(dwuq_ref)
            dwukv_ref[...] = jnp.zeros_like(dwukv_ref)
            dgq_ref[...] = jnp.zeros_like(dgq_ref)
            dgkv_ref[...] = jnp.zeros_like(dgkv_ref)

        c, s = cos_ref[...], sin_ref[...]
        lane = lax.broadcasted_iota(jnp.int32, (tr, LANES), 1)
        gqv, gkvv = gq_ref[...], gkv_ref[...]
        qn, rq = _rms_fwd(lat_ref[:, :Q_LORA])
        kvn, rkv = _rms_fwd(lat_ref[:, Q_LORA:LAT])
        dq_parts, dkv_parts = [], []
        dkr = jnp.zeros((tr, LANES), F32)
        for hd in range(HEADS):
            o = hd * HEAD_PAD
            dq_parts.append(dqc_ref[:, o:o + QK_NOPE])
            dq_parts.append(_unrope(dqc_ref[:, o + QK_NOPE:o + HEAD_PAD].astype(F32), c, s, lane).astype(BF16))
            dkv_parts.append(dkc_ref[:, o:o + QK_NOPE])
            dkv_parts.append(dv_ref[:, hd * V_HEAD:(hd + 1) * V_HEAD])
            dkr = dkr + dkc_ref[:, o + QK_NOPE:o + HEAD_PAD].astype(F32)
        dq = jnp.concatenate(dq_parts, axis=1)
        dkv = jnp.concatenate(dkv_parts, axis=1)
        dwuq_ref[...] += scale * lax.dot_general((qn * gqv).astype(BF16), dq, TN, preferred_element_type=F32)
        dwukv_ref[...] += lax.dot_general((kvn * gkvv).astype(BF16), dkv, TN, preferred_element_type=F32)
        dqn = scale * lax.dot_general(dq, wuq_ref[...], NT, preferred_element_type=F32)
        dkvn = lax.dot_general(dkv, wukv_ref[...], NT, preferred_element_type=F32)
        dqlat, dgq = _rms_bwd(dqn, qn, rq, gqv)
        dkvlat, dgkv = _rms_bwd(dkvn, kvn, rkv, gkvv)
        dgq_ref[...] += dgq
        dgkv_ref[...] += dgkv
        dlat_ref[:, :Q_LORA] = dqlat
        dlat_ref[:, Q_LORA:LAT] = dkvlat
        dlat_ref[:, LAT:LAT_PAD] = _unrope(dkr, c, s, lane)

    full = lambda shape: pl.BlockSpec(shape, lambda i: (0, 0))
    rowb = lambda n: pl.BlockSpec((tr, n), lambda i: (i, 0))
    return pl.pallas_call(
        body, name="mla_qkv_bwd", grid=(rows // tr,),
        in_specs=[rowb(LAT_PAD), full((1, Q_LORA)), full((1, KV_LORA)), full((Q_LORA, HEADS * HEAD_PAD)),
                  full((KV_LORA, HEADS * HEAD_PAD)), rowb(LANES), rowb(LANES),
                  rowb(HEADS * HEAD_PAD), rowb(HEADS * HEAD_PAD), rowb(HEADS * V_HEAD)],
        out_specs=[rowb(LAT_PAD), full((Q_LORA, HEADS * HEAD_PAD)), full((KV_LORA, HEADS * HEAD_PAD)),
                   full((1, Q_LORA)), full((1, KV_LORA))],
        out_shape=[jax.ShapeDtypeStruct((rows, LAT_PAD), F32),
                   jax.ShapeDtypeStruct((Q_LORA, HEADS * HEAD_PAD), F32),
                   jax.ShapeDtypeStruct((KV_LORA, HEADS * HEAD_PAD), F32),
                   jax.ShapeDtypeStruct((1, Q_LORA), F32),
                   jax.ShapeDtypeStruct((1, KV_LORA), F32)],
        compiler_params=_cparams(("arbitrary",)),
    )(lat, gq, gkv, wuq, wukv, cos, sin, dqc, dkc, dv)


def _causal_mask_t(t):
    key = lax.broadcasted_iota(jnp.int32, (t, t), 0)
    query = lax.broadcasted_iota(jnp.int32, (t, t), 1)
    return key <= query


def _attn_fwd(qc, kc, vt, wsend):
    rows = qc.shape[0]
    t = _row_block(rows)
    nblk = rows // t

    def body(q_ref, k_ref, vt_ref, wsend_ref, o_ref, lse_ref, wall_ref, m_ref, l_ref, acc_ref, st_a, st_b,
             send_sems, recv_sems, local_sem):
        i = pl.program_id(1)
        gather = lambda: _exchange_copies(lambda d: wsend_ref, wall_ref, send_sems, recv_sems, local_sem.at[0])

        @pl.when(jnp.logical_and(pl.program_id(0) == 0, i == 0))
        def _():
            for cp in gather():
                cp.start()

        m_ref[...] = jnp.full_like(m_ref, MASK_VALUE)
        l_ref[...] = jnp.zeros_like(l_ref)
        acc_ref[...] = jnp.zeros_like(acc_ref)
        q = q_ref[...]

        def scores(j, st_ref):
            rs = pl.ds(pl.multiple_of(j * t, t), t)
            st_ref[...] = lax.dot_general(k_ref[rs, :], q, NT, preferred_element_type=F32)

        def consume(j, st_ref, masked):
            st = st_ref[...]
            if masked:
                st = jnp.where(_causal_mask_t(t), st, MASK_VALUE)
            m_prev = m_ref[...]
            m_new = jnp.maximum(m_prev, jnp.max(st, axis=0, keepdims=True))
            alpha = jnp.exp(m_prev - m_new)
            pt = jnp.exp(st - m_new)
            l_ref[...] = alpha * l_ref[...] + jnp.sum(pt, axis=0, keepdims=True)
            acc_ref[...] = alpha * acc_ref[...] + jnp.dot(vt_ref[0, j], pt.astype(BF16), preferred_element_type=F32)
            m_ref[...] = m_new

        scores(0, st_a)

        def pair(jj, carry):
            j0 = 2 * jj
            scores(j0 + 1, st_b)
            consume(j0, st_a, False)
            scores(j0 + 2, st_a)
            consume(j0 + 1, st_b, False)
            return carry

        lax.fori_loop(0, i // 2, pair, 0)

        @pl.when(i % 2 == 0)
        def _():
            consume(i, st_a, True)

        @pl.when(i % 2 == 1)
        def _():
            scores(i, st_b)
            consume(i - 1, st_a, False)
            consume(i, st_b, True)

        o_ref[...] = (acc_ref[...] / l_ref[...]).T
        lse_ref[0, 0] = m_ref[...] + jnp.log(l_ref[...])

        @pl.when(jnp.logical_and(pl.program_id(0) == HEADS - 1, i == nblk - 1))
        def _():
            for cp in gather():
                cp.wait()

    hbm = pl.BlockSpec(memory_space=pl.ANY)
    return pl.pallas_call(
        body, name="attn_fwd", grid=(HEADS, nblk),
        in_specs=[pl.BlockSpec((t, HEAD_PAD), lambda h, i: (i, h)),
                  pl.BlockSpec((rows, HEAD_PAD), lambda h, i: (0, h)),
                  pl.BlockSpec((1, nblk, V_HEAD, t), lambda h, i: (h, 0, 0, 0)),
                  hbm],
        out_specs=[pl.BlockSpec((t, V_HEAD), lambda h, i: (i, h)),
                   pl.BlockSpec((1, 1, 1, t), lambda h, i: (h, i, 0, 0)),
                   hbm],
        out_shape=[jax.ShapeDtypeStruct((rows, HEADS * V_HEAD), F32),
                   jax.ShapeDtypeStruct((HEADS, nblk, 1, t), F32),
                   jax.ShapeDtypeStruct((N_DEV,) + wsend.shape, wsend.dtype)],
        scratch_shapes=[pltpu.VMEM((1, t), F32), pltpu.VMEM((1, t), F32), pltpu.VMEM((V_HEAD, t), F32),
                        pltpu.VMEM((t, t), F32), pltpu.VMEM((t, t), F32)] + _exchange_sems(),
        compiler_params=_cparams(("arbitrary", "arbitrary")),
    )(qc, kc, vt, wsend)


def _attn_bwd(qc, kc, v, lse, delta, do, gsend):
    rows = qc.shape[0]
    t = _row_block(rows)
    nblk = rows // t

    def body(q_ref, k_ref, v_ref, lse_ref, delta_ref, do_ref, gsend_ref, dq_ref, dk_ref, dv_ref, land_ref,
             dq_acc, dk_acc, dv_acc, st_a, dp_a, st_b, dp_b, send_sems, recv_sems, local_sem):
        j = pl.program_id(1)
        exchange = lambda: _exchange_copies(lambda d: gsend_ref.at[d], land_ref, send_sems, recv_sems, local_sem.at[0])

        @pl.when(jnp.logical_and(pl.program_id(0) == 0, j == 0))
        def _():
            for cp in exchange():
                cp.start()

        @pl.when(j == 0)
        def _():
            dq_acc[...] = jnp.zeros_like(dq_acc)

        dk_acc[...] = jnp.zeros_like(dk_acc)
        dv_acc[...] = jnp.zeros_like(dv_acc)
        k = k_ref[...]
        vv = v_ref[...]

        def products(i, st_ref, dp_ref):
            rs = pl.ds(pl.multiple_of(i * t, t), t)
            st_ref[...] = lax.dot_general(k, q_ref[rs, :], NT, preferred_element_type=F32)
            dp_ref[...] = lax.dot_general(vv, do_ref[rs, :], NT, preferred_element_type=F32)

        def consume(i, st_ref, dp_ref):
            rs = pl.ds(pl.multiple_of(i * t, t), t)
            q = q_ref[rs, :]
            dob = do_ref[rs, :]
            st = jnp.where(jnp.logical_or(_causal_mask_t(t), i != j), st_ref[...], MASK_VALUE)
            pt = jnp.exp(st - lse_ref[0, i])
            dv_acc[...] += jnp.dot(pt.astype(BF16), dob, preferred_element_type=F32)
            dst = (pt * (dp_ref[...] - delta_ref[0, i])).astype(BF16)
            dk_acc[...] += jnp.dot(dst, q, preferred_element_type=F32)
            dq_acc[rs, :] += lax.dot_general(dst, k, TN, preferred_element_type=F32)

        last = nblk - 1 - j
        products(j, st_a, dp_a)

        def pair(pp, carry):
            i0 = j + 2 * pp
            products(i0 + 1, st_b, dp_b)
            consume(i0, st_a, dp_a)
            products(i0 + 2, st_a, dp_a)
            consume(i0 + 1, st_b, dp_b)
            return carry

        lax.fori_loop(0, last // 2, pair, 0)

        @pl.when(last % 2 == 0)
        def _():
            consume(nblk - 1, st_a, dp_a)

        @pl.when(last % 2 == 1)
        def _():
            products(nblk - 1, st_b, dp_b)
            consume(nblk - 2, st_a, dp_a)
            consume(nblk - 1, st_b, dp_b)

        dk_ref[...] = dk_acc[...].astype(BF16)
        dv_ref[...] = dv_acc[...].astype(BF16)

        @pl.when(j == nblk - 1)
        def _():
            dq_ref[...] = dq_acc[...].astype(BF16)

        @pl.when(jnp.logical_and(pl.program_id(0) == HEADS - 1, j == nblk - 1))
        def _():
            for cp in exchange():
                cp.wait()

    stat = pl.BlockSpec((1, nblk, 1, t), lambda h, j: (h, 0, 0, 0))
    hbm = pl.BlockSpec(memory_space=pl.ANY)
    return pl.pallas_call(
        body, name="attn_bwd", grid=(HEADS, nblk),
        in_specs=[pl.BlockSpec((rows, HEAD_PAD), lambda h, j: (0, h)),
                  pl.BlockSpec((t, HEAD_PAD), lambda h, j: (j, h)),
                  pl.BlockSpec((t, V_HEAD), lambda h, j: (j, h)),
                  stat, stat,
                  pl.BlockSpec((rows, V_HEAD), lambda h, j: (0, h)),
                  hbm],
        out_specs=[pl.BlockSpec((rows, HEAD_PAD), lambda h, j: (0, h)),
                   pl.BlockSpec((t, HEAD_PAD), lambda h, j: (j, h)),
                   pl.BlockSpec((t, V_HEAD), lambda h, j: (j, h)),
                   hbm],
        out_shape=[jax.ShapeDtypeStruct((rows, HEADS * HEAD_PAD), BF16),
                   jax.ShapeDtypeStruct((rows, HEADS * HEAD_PAD), BF16),
                   jax.ShapeDtypeStruct((rows, HEADS * V_HEAD), BF16),
                   jax.ShapeDtypeStruct(gsend.shape, gsend.dtype)],
        scratch_shapes=[pltpu.VMEM((rows, HEAD_PAD), F32), pltpu.VMEM((t, HEAD_PAD), F32),
                        pltpu.VMEM((t, V_HEAD), F32)] + [pltpu.VMEM((t, t), F32)] * 4 + _exchange_sems(),
        compiler_params=_cparams(("arbitrary", "arbitrary")),
    )(qc, kc, v, lse, delta, do, gsend)


def _shift_down(prev_tile, x, k):
    xx = jnp.concatenate([prev_tile, x], axis=0)
    return pltpu.roll(xx, k, 0)[SUBLANES:]


def _shift_up(x, next_tile, k):
    n = x.shape[0]
    xx = jnp.concatenate([x, next_tile], axis=0)
    return pltpu.roll(xx, n + SUBLANES - k, 0)[:n]


def _lru_gates(u, u_prev, cw_ref, cb_ref, wrg_ref, brg_ref, wig_ref, big_ref, lam_ref, first_block):
    taps = [_shift_down(u_prev, u, CONV_WIDTH - 1 - j) if j < CONV_WIDTH - 1 else u for j in range(CONV_WIDTH)]
    uc = cb_ref[...] + taps[0] * cw_ref[0:1, :]
    for j in range(1, CONV_WIDTH):
        uc = uc + taps[j] * cw_ref[j:j + 1, :]
    ub = uc.astype(BF16)
    zr = jnp.concatenate([jnp.dot(ub[:, g * LRU_BLOCK:(g + 1) * LRU_BLOCK], wrg_ref[g], preferred_element_type=F32)
                          for g in range(LRU_BLOCKS)], axis=1) + brg_ref[...]
    zi = jnp.concatenate([jnp.dot(ub[:, g * LRU_BLOCK:(g + 1) * LRU_BLOCK], wig_ref[g], preferred_element_type=F32)
                          for g in range(LRU_BLOCKS)], axis=1) + big_ref[...]
    r = jax.nn.sigmoid(zr)
    ig = jax.nn.sigmoid(zi)
    sp = _softplus_neg(lam_ref[...])
    log_a = (-LRU_C) * r * sp
    a = jnp.exp(log_a)
    mult_raw = jnp.sqrt(-_expm1_neg(2.0 * log_a))
    row = lax.broadcasted_iota(jnp.int32, u.shape, 0)
    is_start = jnp.logical_and(first_block, row == 0)
    mult = jnp.where(is_start, 1.0, mult_raw)
    return dict(taps=taps, uc=uc, ub=ub, r=r, ig=ig, sp=sp, a=a, mult=mult, mult_raw=mult_raw, is_start=is_start)


def _rglru_fwd(u, cw, cb, wrg, brg, wig, big, lam):
    rows = u.shape[0]
    tb = _row_block(rows)

    def body(u_ref, cw_ref, cb_ref, wrg_ref, brg_ref, wig_ref, big_ref, lam_ref, hs_ref, utail, hcar, a_s, b_s):
        i = pl.program_id(0)

        @pl.when(i == 0)
        def _():
            utail[...] = jnp.zeros_like(utail)
            hcar[...] = jnp.zeros_like(hcar)

        u = u_ref[...]
        gt = _lru_gates(u, utail[...], cw_ref, cb_ref, wrg_ref, brg_ref, wig_ref, big_ref, lam_ref, i == 0)
        a_s[...] = gt["a"]
        b_s[...] = gt["mult"] * (gt["ig"] * gt["uc"])
        row8 = lax.broadcasted_iota(jnp.int32, (SUBLANES, LRU_WIDTH), 0)

        def tile(tix, carry):
            rs = pl.ds(pl.multiple_of(tix * SUBLANES, SUBLANES), SUBLANES)
            av, bv = a_s[rs, :], b_s[rs, :]
            for k in (1, 2, 4):
                keep = row8 >= k
                bv = jnp.where(keep, av * pltpu.roll(bv, k, 0) + bv, bv)
                av = jnp.where(keep, av * pltpu.roll(av, k, 0), av)
            h8 = av * carry + bv
            hs_ref[rs, :] = h8
            return jnp.broadcast_to(h8[SUBLANES - 1:SUBLANES, :], (SUBLANES, LRU_WIDTH))

        hcar[...] = lax.fori_loop(0, tb // SUBLANES, tile, hcar[...])
        utail[...] = u[tb - SUBLANES:, :]

    full2 = lambda shape: pl.BlockSpec(shape, lambda i: (0, 0))
    full3 = lambda shape: pl.BlockSpec(shape, lambda i: (0, 0, 0))
    blk = pl.BlockSpec((tb, LRU_WIDTH), lambda i: (i, 0))
    return pl.pallas_call(
        body, name="rglru_fwd", grid=(rows // tb,),
        in_specs=[blk, full2((CONV_WIDTH, LRU_WIDTH)), full2((1, LRU_WIDTH)),
                  full3((LRU_BLOCKS, LRU_BLOCK, LRU_BLOCK)), full2((1, LRU_WIDTH)),
                  full3((LRU_BLOCKS, LRU_BLOCK, LRU_BLOCK)), full2((1, LRU_WIDTH)), full2((1, LRU_WIDTH))],
        out_specs=blk,
        out_shape=jax.ShapeDtypeStruct((rows, LRU_WIDTH), F32),
        scratch_shapes=[pltpu.VMEM((SUBLANES, LRU_WIDTH), F32), pltpu.VMEM((SUBLANES, LRU_WIDTH), F32),
                        pltpu.VMEM((tb, LRU_WIDTH), F32), pltpu.VMEM((tb, LRU_WIDTH), F32)],
        compiler_params=_cparams(("arbitrary",)),
    )(u, cw, cb, wrg, brg, wig, big, lam)


def _rglru_bwd(u, hs, dhs, cw, cb, wrg, brg, wig, big, lam):
    rows = u.shape[0]
    tb = _row_block(rows)
    nblk = rows // tb
    tiles = tb // SUBLANES

    def body(u_ref, up_ref, hs_ref, hp_ref, dhs_ref, cw_ref, cb_ref, wrg_ref, brg_ref, wig_ref, big_ref, lam_ref,
             du_ref, dcw_ref, dcb_ref, dwrg_ref, dbrg_ref, dwig_ref, dbig_ref, dlam_ref,
             gcar, duc_head, a_s, b_s, g_s, dsp_acc):
        step = pl.program_id(0)
        blk_ix = nblk - 1 - step

        @pl.when(step == 0)
        def _():
            for ref in (dcw_ref, dcb_ref, dwrg_ref, dbrg_ref, dwig_ref, dbig_ref, gcar, duc_head, dsp_acc):
                ref[...] = jnp.zeros_like(ref)

        first = blk_ix == 0
        u = u_ref[...]
        u_prev = jnp.where(first, 0.0, up_ref[...])
        h_prev_tile = jnp.where(first, 0.0, hp_ref[...])
        gt = _lru_gates(u, u_prev, cw_ref, cb_ref, wrg_ref, brg_ref, wig_ref, big_ref, lam_ref, first)
        a, r, ig, uc, mult = gt["a"], gt["r"], gt["ig"], gt["uc"], gt["mult"]
        dhs_v = dhs_ref[...]

        a_s[...] = a
        b_s[...] = a * dhs_v
        row8 = lax.broadcasted_iota(jnp.int32, (SUBLANES, LRU_WIDTH), 0)

        def tile(tix, carry):
            rs = pl.ds(pl.multiple_of((tiles - 1 - tix) * SUBLANES, SUBLANES), SUBLANES)
            av, bv = a_s[rs, :], b_s[rs, :]
            for k in (1, 2, 4):
                keep = row8 < SUBLANES - k
                bv = jnp.where(keep, av * pltpu.roll(bv, SUBLANES - k, 0) + bv, bv)
                av = jnp.where(keep, av * pltpu.roll(av, SUBLANES - k, 0), av)
            g8 = av * carry + bv
            g_s[rs, :] = g8
            return jnp.broadcast_to(g8[0:1, :], (SUBLANES, LRU_WIDTH))

        g_next = gcar[...]
        gcar[...] = lax.fori_loop(0, tiles, tile, g_next)
        g = dhs_v + _shift_up(g_s[...], g_next, 1)

        h_prev = _shift_down(h_prev_tile, hs_ref[...], 1)
        da = g * h_prev
        iu = ig * uc
        dmult = jnp.where(gt["is_start"], 0.0, g * iu)
        d_ig = g * mult * uc
        duc = g * mult * ig
        dlog_a = da * a - dmult * (a * a) / gt["mult_raw"]
        dzr = (dlog_a * ((-LRU_C) * gt["sp"])) * r * (1.0 - r)
        dsp_acc[...] += jnp.sum(dlog_a * ((-LRU_C) * r), axis=0, keepdims=True)
        dzi = d_ig * ig * (1.0 - ig)
        dbrg_ref[...] += jnp.sum(dzr, axis=0, keepdims=True)
        dbig_ref[...] += jnp.sum(dzi, axis=0, keepdims=True)
        dzr_b, dzi_b = dzr.astype(BF16), dzi.astype(BF16)
        ub = gt["ub"]
        duc_parts = []
        for gi in range(LRU_BLOCKS):
            cs = slice(gi * LRU_BLOCK, (gi + 1) * LRU_BLOCK)
            dwrg_ref[gi] += lax.dot_general(ub[:, cs], dzr_b[:, cs], TN, preferred_element_type=F32)
            dwig_ref[gi] += lax.dot_general(ub[:, cs], dzi_b[:, cs], TN, preferred_element_type=F32)
            duc_parts.append(lax.dot_general(dzr_b[:, cs], wrg_ref[gi], NT, preferred_element_type=F32)
                             + lax.dot_general(dzi_b[:, cs], wig_ref[gi], NT, preferred_element_type=F32))
        duc = duc + jnp.concatenate(duc_parts, axis=1)

        dcb_ref[...] += jnp.sum(duc, axis=0, keepdims=True)
        taps = gt["taps"]
        for jt in range(CONV_WIDTH):
            dcw_ref[jt:jt + 1, :] += jnp.sum(duc * taps[jt], axis=0, keepdims=True)
        head = duc_head[...]
        du = duc * cw_ref[CONV_WIDTH - 1:CONV_WIDTH, :]
        for jt in range(CONV_WIDTH - 1):
            du = du + _shift_up(duc, head, CONV_WIDTH - 1 - jt) * cw_ref[jt:jt + 1, :]
        du_ref[...] = du
        duc_head[...] = duc[:SUBLANES, :]

        @pl.when(step == nblk - 1)
        def _():
            dlam_ref[...] = -dsp_acc[...] * jax.nn.sigmoid(-lam_ref[...])

    full2 = lambda shape: pl.BlockSpec(shape, lambda s: (0, 0))
    full3 = lambda shape: pl.BlockSpec(shape, lambda s: (0, 0, 0))
    blk = pl.BlockSpec((tb, LRU_WIDTH), lambda s: (nblk - 1 - s, 0))
    prev_tile = pl.BlockSpec((SUBLANES, LRU_WIDTH), lambda s: (jnp.maximum((nblk - 1 - s) * tiles - 1, 0), 0))
    wshape = (LRU_BLOCKS, LRU_BLOCK, LRU_BLOCK)
    return pl.pallas_call(
        body, name="rglru_bwd", grid=(nblk,),
        in_specs=[blk, prev_tile, blk, prev_tile, blk, full2((CONV_WIDTH, LRU_WIDTH)), full2((1, LRU_WIDTH)),
                  full3(wshape), full2((1, LRU_WIDTH)), full3(wshape), full2((1, LRU_WIDTH)), full2((1, LRU_WIDTH))],
        out_specs=[blk, full2((CONV_WIDTH, LRU_WIDTH)), full2((1, LRU_WIDTH)), full3(wshape), full2((1, LRU_WIDTH)),
                   full3(wshape), full2((1, LRU_WIDTH)), full2((1, LRU_WIDTH))],
        out_shape=[jax.ShapeDtypeStruct((rows, LRU_WIDTH), F32),
                   jax.ShapeDtypeStruct((CONV_WIDTH, LRU_WIDTH), F32), jax.ShapeDtypeStruct((1, LRU_WIDTH), F32),
                   jax.ShapeDtypeStruct(wshape, F32), jax.ShapeDtypeStruct((1, LRU_WIDTH), F32),
                   jax.ShapeDtypeStruct(wshape, F32), jax.ShapeDtypeStruct((1, LRU_WIDTH), F32),
                   jax.ShapeDtypeStruct((1, LRU_WIDTH), F32)],
        scratch_shapes=[pltpu.VMEM((SUBLANES, LRU_WIDTH), F32), pltpu.VMEM((SUBLANES, LRU_WIDTH), F32),
                        pltpu.VMEM((tb, LRU_WIDTH), F32), pltpu.VMEM((tb, LRU_WIDTH), F32),
                        pltpu.VMEM((tb, LRU_WIDTH), F32), pltpu.VMEM((1, LRU_WIDTH), F32)],
        compiler_params=_cparams(("arbitrary",)),
    )(u, u, hs, hs, dhs, cw, cb, wrg, brg, wig, big, lam)


def _final_loss(h, gf, target, n_real):
    rows = h.shape[0]
    tr = _row_block(rows)

    def body(h_ref, g_ref, t_ref, dh_ref, loss_ref, dg_ref):
        i = pl.program_id(0)

        @pl.when(i == 0)
        def _():
            loss_ref[...] = jnp.zeros_like(loss_ref)
            dg_ref[...] = jnp.zeros_like(dg_ref)

        gv = g_ref[...]
        xn, r = _rms_fwd(h_ref[...])
        row = i * tr + lax.broadcasted_iota(jnp.int32, (tr, 1), 0)
        live = jnp.logical_and(row >= N_META, row < n_real)
        err = jnp.where(live, xn * gv - t_ref[...], 0.0)
        loss_ref[...] += (0.5 / D_MODEL) * jnp.sum(jnp.sum(err * err, axis=1, keepdims=True), axis=0, keepdims=True)
        dx, dg = _rms_bwd(err * (1.0 / D_MODEL), xn, r, gv)
        dg_ref[...] += dg
        dh_ref[...] = dx

    blk = pl.BlockSpec((tr, D_MODEL), lambda i: (i, 0))
    return pl.pallas_call(
        body, name="final_loss", grid=(rows // tr,),
        in_specs=[blk, pl.BlockSpec((1, D_MODEL), lambda i: (0, 0)), blk],
        out_specs=[blk, pl.BlockSpec((1, 1), lambda i: (0, 0)), pl.BlockSpec((1, D_MODEL), lambda i: (0, 0))],
        out_shape=[jax.ShapeDtypeStruct((rows, D_MODEL), F32), jax.ShapeDtypeStruct((1, 1), F32),
                   jax.ShapeDtypeStruct((1, D_MODEL), F32)],
        compiler_params=_cparams(("arbitrary",)),
    )(h, gf, target)


def _my_place():
    x, y, c = lax.axis_index("x"), lax.axis_index("y"), lax.axis_index("c")
    return x, y, c, 4 * x + 2 * y + c


def _peer(x, y, c, k):
    px, py, pc = x ^ (k >> 2), y ^ ((k >> 1) & 1), c ^ (k & 1)
    return (px, py, pc), 4 * px + 2 * py + pc


def _exchange_copies(src_of, dst_ref, send_sems, recv_sems, local_sem):
    x, y, c, me = _my_place()
    copies = [pltpu.make_async_copy(src_of(me), dst_ref.at[me], local_sem)]
    for k in range(1, N_DEV):
        peer, pid = _peer(x, y, c, k)
        copies.append(pltpu.make_async_remote_copy(
            src_ref=src_of(pid), dst_ref=dst_ref.at[me], send_sem=send_sems.at[k], recv_sem=recv_sems.at[k],
            device_id=peer, device_id_type=MESH))
    return copies


def _exchange_sems():
    return [pltpu.SemaphoreType.DMA((N_DEV,)), pltpu.SemaphoreType.DMA((N_DEV,)), pltpu.SemaphoreType.DMA((1,))]


def _sum_blocks(land):
    n = land.shape[1]
    rc = next(c for c in (672, 512, 256, 128, 64, 32, 16) if n % c == 0)

    def body(land_ref, out_ref):
        acc = land_ref[0].astype(F32)
        for d in range(1, N_DEV):
            acc = acc + land_ref[d].astype(F32)
        out_ref[...] = acc

    return pl.pallas_call(
        body, name="sum_blocks", grid=(n // rc,),
        in_specs=[pl.BlockSpec((N_DEV, rc, LANES), lambda i: (0, i, 0))],
        out_specs=pl.BlockSpec((rc, LANES), lambda i: (i, 0)),
        out_shape=jax.ShapeDtypeStruct((n, LANES), F32),
        compiler_params=_cparams(("parallel",)),
    )(land)


def _all_gather(big, small):
    def body(big_ref, small_ref, obig_ref, osmall_ref, send_sems, recv_sems, local_sems):
        x, y, c, me = _my_place()
        own = [pltpu.make_async_copy(big_ref, obig_ref.at[me], local_sems.at[0]),
               pltpu.make_async_copy(small_ref, osmall_ref.at[me], local_sems.at[1])]
        for cp in own:
            cp.start()
        copies = []
        for k in range(1, N_DEV):
            peer, _ = _peer(x, y, c, k)
            for part, (src, dst) in enumerate(((big_ref, obig_ref), (small_ref, osmall_ref))):
                copies.append(pltpu.make_async_remote_copy(
                    src_ref=src, dst_ref=dst.at[me], send_sem=send_sems.at[part, k], recv_sem=recv_sems.at[part, k],
                    device_id=peer, device_id_type=MESH))
        for cp in copies:
            cp.start()
        for cp in copies:
            cp.wait()
        for cp in own:
            cp.wait()

    n = big.shape[0]
    hbm = pl.BlockSpec(memory_space=pl.ANY)
    return pl.pallas_call(
        body, name="weight_all_gather",
        in_specs=[hbm, hbm], out_specs=[hbm, hbm],
        out_shape=[jax.ShapeDtypeStruct((N_DEV, n, LANES), BF16), jax.ShapeDtypeStruct((N_DEV,) + small.shape, F32)],
        scratch_shapes=[pltpu.SemaphoreType.DMA((2, N_DEV)), pltpu.SemaphoreType.DMA((2, N_DEV)),
                        pltpu.SemaphoreType.DMA((2,))],
        compiler_params=pltpu.CompilerParams(has_side_effects=True),
    )(big, small)


GRAD_CHUNK = 32


def _grad_exchange(gbig, rep):
    n = gbig.shape[1]
    nrep = rep.shape[0]

    def body(gbig_ref, rep_ref, out_ref, orep_ref, land, land_rep, send_sems, recv_sems, local_sems):
        x, y, c, me = _my_place()
        own = [pltpu.make_async_copy(gbig_ref.at[me], land.at[me], local_sems.at[0]),
               pltpu.make_async_copy(rep_ref, land_rep.at[me], local_sems.at[1])]
        for cp in own:
            cp.start()
        copies = []
        for k in range(1, N_DEV):
            peer, pid = _peer(x, y, c, k)
            copies.append(pltpu.make_async_remote_copy(
                src_ref=gbig_ref.at[pid], dst_ref=land.at[me], send_sem=send_sems.at[0, k],
                recv_sem=recv_sems.at[0, k], device_id=peer, device_id_type=MESH))
            copies.append(pltpu.make_async_remote_copy(
                src_ref=rep_ref, dst_ref=land_rep.at[me], send_sem=send_sems.at[1, k],
                recv_sem=recv_sems.at[1, k], device_id=peer, device_id_type=MESH))
        for cp in copies:
            cp.start()
        for cp in copies:
            cp.wait()
        for cp in own:
            cp.wait()

        def chunk(ci, carry):
            rs = pl.ds(pl.multiple_of(ci * GRAD_CHUNK, GRAD_CHUNK), GRAD_CHUNK)
            acc = land[0, rs, :].astype(F32)
            for d in range(1, N_DEV):
                acc = acc + land[d, rs, :].astype(F32)
            out_ref[rs, :] = acc
            return carry

        lax.fori_loop(0, n // GRAD_CHUNK, chunk, 0)
        acc = land_rep[0]
        for d in range(1, N_DEV):
            acc = acc + land_rep[d]
        orep_ref[...] = acc

    return pl.pallas_call(
        body, name="grad_exchange",
        in_specs=[pl.BlockSpec(memory_space=pl.ANY), pl.BlockSpec(memory_space=pl.ANY)],
        out_specs=[pl.BlockSpec(memory_space=pltpu.VMEM), pl.BlockSpec(memory_space=pltpu.VMEM)],
        out_shape=[jax.ShapeDtypeStruct((n, LANES), F32), jax.ShapeDtypeStruct((nrep, LANES), F32)],
        scratch_shapes=[pltpu.VMEM((N_DEV, n, LANES), BF16), pltpu.VMEM((N_DEV, nrep, LANES), F32),
                        pltpu.SemaphoreType.DMA((2, N_DEV)), pltpu.SemaphoreType.DMA((2, N_DEV)),
                        pltpu.SemaphoreType.DMA((2,))],
        compiler_params=pltpu.CompilerParams(vmem_limit_bytes=VMEM_LIMIT, has_side_effects=True),
    )(gbig, rep)


def _adamw_all(ws, gs, ms, vs):
    n = len(ws)

    def body(*refs):
        w_refs, g_refs, m_refs, v_refs = refs[0:n], refs[n:2 * n], refs[2 * n:3 * n], refs[3 * n:4 * n]
        d_refs, nm_refs, nv_refs = refs[4 * n:5 * n], refs[5 * n:6 * n], refs[6 * n:7 * n]
        for w_ref, g_ref, m_ref, v_ref, d_ref, nm_ref, nv_ref in zip(w_refs, g_refs, m_refs, v_refs, d_refs, nm_refs, nv_refs):
            g = g_ref[...]
            m = ADAM_B1 * m_ref[...] + (1.0 - ADAM_B1) * g
            v = ADAM_B2 * v_ref[...] + (1.0 - ADAM_B2) * jnp.square(g)
            m_hat = m / (1.0 - ADAM_B1 ** ADAM_STEP)
            v_hat = v / (1.0 - ADAM_B2 ** ADAM_STEP)
            d_ref[...] = -ADAM_LR * (m_hat / (jnp.sqrt(v_hat) + ADAM_EPS) + ADAM_WD * w_ref[...])
            nm_ref[...] = m
            nv_ref[...] = v

    shapes = [jax.ShapeDtypeStruct(w.shape, F32) for w in ws]
    outs = pl.pallas_call(
        body, name="adamw", out_shape=shapes * 3,
        compiler_params=pltpu.CompilerParams(vmem_limit_bytes=VMEM_LIMIT),
    )(*ws, *gs, *ms, *vs)
    return outs[0:n], outs[n:2 * n], outs[2 * n:3 * n]


BIG_A = (("a_w_in", 1728), ("a_w_uq", 576), ("a_w_ukv", 512))
BIG_B = (("a_w_out", 1024), ("b_w_in", 2048), ("b_w_rg", 256), ("b_w_ig", 256), ("b_w_out", 1024))
SMALL_A = (("meta_tokens", 16),)
SMALL_B = (("b_norm_g", 1), ("b_conv_w", 4), ("b_conv_b", 1), ("b_b_rg", 1), ("b_b_ig", 1), ("b_lam", 1))
REP = (("a_norm_g", 8), ("a_q_norm_g", 3), ("a_kv_norm_g", 2), ("final_norm_g", 8))
SLOT = 16


def _offsets(table, slot=1, start=0):
    out, o = {}, start
    for name, n in table:
        out[name] = (o, n)
        o += -(-n // slot) * slot
    return out, o


def _slotted(a, axis):
    pad = -a.shape[axis] % SLOT
    if not pad:
        return a
    widths = [(0, 0)] * a.ndim
    widths[axis] = (0, pad)
    return jnp.pad(a, widths)


def _rope_tables(rows):
    pos = jnp.arange(rows, dtype=F32)
    inv_freq = ROPE_BASE ** (-jnp.arange(0, QK_ROPE, 2, dtype=F32) / QK_ROPE)
    ang = pos[:, None] * inv_freq[None, :]
    cos, sin = jnp.cos(ang), jnp.sin(ang)
    zeros = jnp.zeros((rows, LANES - QK_ROPE), F32)
    return jnp.concatenate([cos, cos, zeros], axis=1), jnp.concatenate([-sin, sin, zeros], axis=1)


def kernel(x, meta_tokens, a_norm_g, a_w_in, a_q_norm_g, a_kv_norm_g, a_w_uq, a_w_ukv, a_w_out, b_norm_g, b_w_in, b_conv_w, b_conv_b, b_w_rg, b_b_rg, b_w_ig, b_b_ig, b_lam, b_w_out, final_norm_g, loss_target, m_meta_tokens, m_a_norm_g, m_a_w_in, m_a_q_norm_g, m_a_kv_norm_g, m_a_w_uq, m_a_w_ukv, m_a_w_out, m_b_norm_g, m_b_w_in, m_b_conv_w, m_b_conv_b, m_b_w_rg, m_b_b_rg, m_b_w_ig, m_b_b_ig, m_b_lam, m_b_w_out, m_final_norm_g, v_meta_tokens, v_a_norm_g, v_a_w_in, v_a_q_norm_g, v_a_kv_norm_g, v_a_w_uq, v_a_w_ukv, v_a_w_out, v_b_norm_g, v_b_w_in, v_b_conv_w, v_b_conv_b, v_b_w_rg, v_b_b_rg, v_b_w_ig, v_b_b_ig, v_b_lam, v_b_w_out, v_final_norm_g):
    names = ("meta_tokens", "a_norm_g", "a_w_in", "a_q_norm_g", "a_kv_norm_g", "a_w_uq", "a_w_ukv", "a_w_out",
             "b_norm_g", "b_w_in", "b_conv_w", "b_conv_b", "b_w_rg", "b_b_rg", "b_w_ig", "b_b_ig", "b_lam", "b_w_out",
             "final_norm_g")
    w = dict(zip(names, (meta_tokens, a_norm_g, a_w_in, a_q_norm_g, a_kv_norm_g, a_w_uq, a_w_ukv, a_w_out, b_norm_g,
                         b_w_in, b_conv_w, b_conv_b, b_w_rg, b_b_rg, b_w_ig, b_b_ig, b_lam, b_w_out, final_norm_g)))
    mom_m = dict(zip(names, (m_meta_tokens, m_a_norm_g, m_a_w_in, m_a_q_norm_g, m_a_kv_norm_g, m_a_w_uq, m_a_w_ukv,
                             m_a_w_out, m_b_norm_g, m_b_w_in, m_b_conv_w, m_b_conv_b, m_b_w_rg, m_b_b_rg, m_b_w_ig,
                             m_b_b_ig, m_b_lam, m_b_w_out, m_final_norm_g)))
    mom_v = dict(zip(names, (v_meta_tokens, v_a_norm_g, v_a_w_in, v_a_q_norm_g, v_a_kv_norm_g, v_a_w_uq, v_a_w_ukv,
                             v_a_w_out, v_b_norm_g, v_b_w_in, v_b_conv_w, v_b_conv_b, v_b_w_rg, v_b_b_rg, v_b_w_ig,
                             v_b_b_ig, v_b_lam, v_b_w_out, v_final_norm_g)))

    seq = x.shape[1]
    n_real = N_META + seq
    rows = -(-n_real // LANES) * LANES
    scale = (QK_NOPE + QK_ROPE) ** -0.5
    biga_off, biga_rows = _offsets(BIG_A)
    bigb_off, bigb_rows = _offsets(BIG_B)
    small_off, _ = _offsets(SMALL_A + SMALL_B, SLOT)
    gsmalla_off, grada_rows = _offsets(SMALL_A, SLOT, biga_rows)
    gsmallb_off, gradb_rows = _offsets(SMALL_B, SLOT, bigb_rows)
    grada_rows = -(-grada_rows // GRAD_CHUNK) * GRAD_CHUNK
    rep_off, _ = _offsets(REP, SLOT)

    send_a = jnp.concatenate([w[nm].reshape(-1, LANES) for nm, _ in BIG_A], axis=0).astype(BF16)
    send_b = jnp.concatenate([w[nm].reshape(-1, LANES) for nm, _ in BIG_B], axis=0).astype(BF16)
    send_small = jnp.concatenate([_slotted(w[nm].reshape(-1, LANES), 0) for nm, _ in SMALL_A + SMALL_B], axis=0)
    all_a, all_small = _all_gather(send_a, send_small)

    def small_seg(nm):
        o, n = small_off[nm]
        return all_small[:, o:o + n, :]

    def cols(seg, r, cdev):
        return seg.reshape(N_DEV, r, cdev).transpose(1, 0, 2).reshape(r, N_DEV * cdev)

    def seg(gathered, off, nm):
        o, n = off[nm]
        return gathered[:, o:o + n, :]

    w_in_a = cols(seg(all_a, biga_off, "a_w_in"), D_MODEL, 216)
    w_in_a = jnp.concatenate([w_in_a[:, :LAT + QK_ROPE], jnp.zeros((D_MODEL, LAT_PAD - LAT - QK_ROPE), BF16),
                              w_in_a[:, LAT + QK_ROPE:]], axis=1)
    w_uq = jnp.pad(seg(all_a, biga_off, "a_w_uq").reshape(N_DEV, Q_LORA, QK_NOPE + QK_ROPE).transpose(1, 0, 2),
                   ((0, 0), (0, 0), (0, HEAD_PAD - QK_NOPE - QK_ROPE))).reshape(Q_LORA, HEADS * HEAD_PAD)
    w_ukv = cols(seg(all_a, biga_off, "a_w_ukv"), KV_LORA, QK_NOPE + V_HEAD)
    meta_full = small_seg("meta_tokens").transpose(1, 0, 2).reshape(N_META, D_MODEL)
    vec = lambda nm: small_seg(nm).reshape(1, D_MODEL)
    g_b, conv_b, b_rg, b_ig, lam = vec("b_norm_g"), vec("b_conv_b"), vec("b_b_rg"), vec("b_b_ig"), vec("b_lam")
    conv_w = small_seg("b_conv_w").transpose(1, 0, 2).reshape(CONV_WIDTH, LRU_WIDTH)
    g_a, g_q, g_kv = a_norm_g, a_q_norm_g, a_kv_norm_g
    g_f = final_norm_g.reshape(1, D_MODEL)

    h0 = jnp.concatenate([meta_full, x[0], jnp.zeros((rows - n_real, D_MODEL), F32)], axis=0)
    target = jnp.concatenate([jnp.zeros((N_META, D_MODEL), F32), loss_target[0],
                              jnp.zeros((rows - n_real, D_MODEL), F32)], axis=0)
    cos, sin = _rope_tables(rows)

    lat, gate_a = _norm_proj_fwd(h0, g_a, w_in_a, LAT_PAD, "a_in_fwd")
    qc, kc, v, vt = _mla_qkv_fwd(lat, g_q, g_kv, w_uq, w_ukv, cos, sin, scale)
    o, lse, all_b = _attn_fwd(qc, kc, vt, send_b)

    lru_w = lambda nm: seg(all_b, bigb_off, nm).reshape(N_DEV, LRU_BLOCKS, LRU_BLOCK // N_DEV, LRU_BLOCK).transpose(
        1, 0, 2, 3).reshape(LRU_BLOCKS, LRU_BLOCK, LRU_BLOCK)
    w_out_a = seg(all_b, bigb_off, "a_w_out").reshape(D_MODEL, D_MODEL)
    w_in_b = cols(seg(all_b, bigb_off, "b_w_in"), D_MODEL, 2 * LRU_WIDTH // N_DEV)
    w_rg, w_ig = lru_w("b_w_rg"), lru_w("b_w_ig")
    w_out_b = seg(all_b, bigb_off, "b_w_out").reshape(D_MODEL, D_MODEL)

    h1 = _gated_out_fwd(o, gate_a, h0, w_out_a, "a_out_fwd")
    u, gate_b = _norm_proj_fwd(h1, g_b, w_in_b, LRU_WIDTH, "b_in_fwd")
    hs = _rglru_fwd(u, conv_w, conv_b, w_rg, b_rg, w_ig, b_ig, lam)
    h2 = _gated_out_fwd(hs, gate_b, h1, w_out_b, "b_out_fwd")
    dh2, loss_part, dg_f = _final_loss(h2, g_f, target, n_real)

    dhs, dgate_b, dw_out_b = _gated_out_bwd(hs, gate_b, dh2, w_out_b, F32, False, "b_out_bwd")
    du, dconv_w, dconv_b, dw_rg, db_rg, dw_ig, db_ig, dlam = _rglru_bwd(u, hs, dhs, conv_w, conv_b, w_rg, b_rg, w_ig,
                                                                       b_ig, lam)
    dh1, dw_in_b, dg_b = _norm_proj_bwd(h1, g_b, w_in_b, du, dgate_b, dh2, "b_in_bwd")
    do, dgate_a, dw_out_a, delta = _gated_out_bwd(o, gate_a, dh1, w_out_a, BF16, True, "a_out_bwd")

    def to_cols(g, cdev):
        r = g.shape[0]
        return g.reshape(r, N_DEV, cdev).transpose(1, 0, 2).reshape(N_DEV, -1, LANES)

    def packed(parts, big, small, total):
        pieces = [parts[nm].astype(BF16) for nm, _ in big] + [_slotted(parts[nm].astype(BF16), 1) for nm, _ in small]
        used = sum(p.shape[1] for p in pieces)
        if total > used:
            pieces.append(jnp.zeros((N_DEV, total - used, LANES), BF16))
        return jnp.concatenate(pieces, axis=1)

    lru_g = lambda g: g.reshape(LRU_BLOCKS, N_DEV, LRU_BLOCK // N_DEV, LRU_BLOCK).transpose(1, 0, 2, 3).reshape(
        N_DEV, -1, LANES)
    gsend_b = packed({
        "a_w_out": dw_out_a.reshape(N_DEV, -1, LANES),
        "b_w_in": to_cols(dw_in_b, 2 * LRU_WIDTH // N_DEV),
        "b_w_rg": lru_g(dw_rg), "b_w_ig": lru_g(dw_ig),
        "b_w_out": dw_out_b.reshape(N_DEV, -1, LANES),
        "b_norm_g": to_cols(dg_b, LANES), "b_conv_w": to_cols(dconv_w, LANES), "b_conv_b": to_cols(dconv_b, LANES),
        "b_b_rg": to_cols(db_rg, LANES), "b_b_ig": to_cols(db_ig, LANES), "b_lam": to_cols(dlam, LANES),
    }, BIG_B, SMALL_B, gradb_rows)

    dqc, dkc, dv, land_b = _attn_bwd(qc, kc, v, lse, delta, do, gsend_b)
    gsum_b = _sum_blocks(land_b)
    dlat, dw_uq, dw_ukv, dg_q, dg_kv = _mla_qkv_bwd(lat, g_q, g_kv, w_uq, w_ukv, cos, sin, dqc, dkc, dv, scale)
    dh0, dw_in_a, dg_a = _norm_proj_bwd(h0, g_a, w_in_a, dlat, dgate_a, dh1, "a_in_bwd")

    loss = lax.psum(loss_part[0, 0], ("x", "y", "c"))
    grad_x = dh0[N_META:n_real][None]

    dw_in_a_nat = jnp.concatenate([dw_in_a[:, :LAT + QK_ROPE], dw_in_a[:, LAT_PAD:]], axis=1)
    gsend_a = packed({
        "a_w_in": to_cols(dw_in_a_nat, 216),
        "a_w_uq": dw_uq.reshape(Q_LORA, HEADS, HEAD_PAD)[:, :, :QK_NOPE + QK_ROPE].transpose(1, 0, 2).reshape(
            N_DEV, -1, LANES),
        "a_w_ukv": to_cols(dw_ukv, QK_NOPE + V_HEAD),
        "meta_tokens": to_cols(dh0[:N_META], LANES),
    }, BIG_A, SMALL_A, grada_rows)
    rep_parts = {"a_norm_g": dg_a, "a_q_norm_g": dg_q, "a_kv_norm_g": dg_kv, "final_norm_g": dg_f}
    rep = jnp.concatenate([_slotted(rep_parts[nm].reshape(-1, LANES), 0) for nm, _ in REP], axis=0)
    gsum_a, rep_sum = _grad_exchange(gsend_a, rep)

    grads = {}
    for off, src in ((biga_off, gsum_a), (gsmalla_off, gsum_a), (bigb_off, gsum_b), (gsmallb_off, gsum_b),
                     (rep_off, rep_sum)):
        for nm, (o_r, n) in off.items():
            grads[nm] = src[o_r:o_r + n].reshape(w[nm].shape)

    as2d = lambda a: a.reshape(1, -1) if a.ndim == 1 else a
    deltas, new_ms, new_vs = _adamw_all([as2d(w[nm]) for nm in names], [as2d(grads[nm]) for nm in names],
                                        [as2d(mom_m[nm]) for nm in names], [as2d(mom_v[nm]) for nm in names])
    shaped = lambda arrs: [a.reshape(w[nm].shape) for a, nm in zip(arrs, names)]
    return (loss, grad_x, *[grads[nm] for nm in names], *shaped(deltas), *shaped(new_ms), *shaped(new_vs))
```

```python
import functools

import numpy as np
import jax
import jax.numpy as jnp
from jax import lax
from jax.experimental import pallas as pl
from jax.experimental.pallas import tpu as pltpu

F32 = jnp.float32
BF16 = jnp.bfloat16

D_MODEL = 1024
N_META = 16
RMS_EPS = 1e-6
HEADS = 8
QK_NOPE = 128
QK_ROPE = 64
V_HEAD = 128
Q_LORA = 384
KV_LORA = 256
HEAD_PAD = 256
LAT = Q_LORA + KV_LORA
LAT_PAD = LAT + 128
ROPE_BASE = 10000.0
MASK_VALUE = -1e30
LRU_WIDTH = 1024
LRU_BLOCKS = 4
LRU_BLOCK = 256
CONV_WIDTH = 4
LRU_C = 8.0
N_DEV = 8
ADAM_LR, ADAM_B1, ADAM_B2, ADAM_EPS, ADAM_WD, ADAM_STEP = 0.001, 0.9, 0.999, 1e-08, 0.01, 10

LANES = 128
SUBLANES = 8
VMEM_LIMIT = 56 * 1024 * 1024
MESH = pl.DeviceIdType.MESH

NT = (((1,), (1,)), ((), ()))
TN = (((0,), (0,)), ((), ()))


def _row_block(rows):
    return 384 if rows % 384 == 0 else 128


def _cparams(sem):
    return pltpu.CompilerParams(dimension_semantics=sem, vmem_limit_bytes=VMEM_LIMIT)


def _silu(x):
    return x * jax.nn.sigmoid(x)


def _dsilu(x):
    s = jax.nn.sigmoid(x)
    return s * (1.0 + x * (1.0 - s))


def _rms_fwd(x):
    r = lax.rsqrt(jnp.mean(x * x, axis=-1, keepdims=True) + RMS_EPS)
    return x * r, r


def _rms_bwd(dy, xn, r, g):
    t = dy * g
    dx = r * (t - xn * jnp.mean(t * xn, axis=-1, keepdims=True))
    return dx, jnp.sum(dy * xn, axis=0, keepdims=True)


def _expm1_neg(x):
    small = x * (1.0 + x * (1 / 2 + x * (1 / 6 + x * (1 / 24))))
    return jnp.where(x > -0.05, small, jnp.exp(x) - 1.0)


def _softplus_neg(lam):
    z = jnp.exp(-jnp.abs(lam))
    w = z / (2.0 + z)
    w2 = w * w
    series = 2.0 * w * (1.0 + w2 * (1 / 3) + w2 * w2 * (1 / 5))
    return jnp.maximum(-lam, 0.0) + jnp.where(z < 0.1, series, jnp.log(1.0 + z))


def _rider(send, refs, first, last, all_to_all):
    if send is None:
        return (lambda: None), (lambda: None)
    send_ref, result_ref, send_sems, recv_sems, local_sem = refs
    src_of = (lambda d: send_ref.at[d]) if all_to_all else (lambda d: send_ref)
    copies = lambda: _exchange_copies(src_of, result_ref, send_sems, recv_sems, local_sem.at[0])

    def start():
        @pl.when(first)
        def _():
            for cp in copies():
                cp.start()

    def wait():
        @pl.when(last)
        def _():
            for cp in copies():
                cp.wait()

    return start, wait


def _rider_specs(send, all_to_all):
    if send is None:
        return [], [], [], []
    shape = send.shape if all_to_all else (N_DEV,) + send.shape
    hbm = pl.BlockSpec(memory_space=pl.ANY)
    return [hbm], [hbm], [jax.ShapeDtypeStruct(shape, send.dtype)], _exchange_sems()


def _norm_proj_fwd(h, g, w, n1, name, wsend=None):
    rows, n = h.shape[0], w.shape[1]
    tr = _row_block(rows)
    nsteps = rows // tr
    extra = 0 if wsend is None else 1

    def body(h_ref, g_ref, w_ref, *rest):
        p1_ref, p2_ref = rest[extra], rest[extra + 1]
        i = pl.program_id(0)
        start, wait = _rider(wsend, rest[:extra] + rest[extra + 2:], i == 0, i == nsteps - 1, False)
        start()
        xn, _ = _rms_fwd(h_ref[...])
        hn = (xn * g_ref[...]).astype(BF16)
        p = jnp.dot(hn, w_ref[...], preferred_element_type=F32)
        p1_ref[...] = p[:, :n1]
        p2_ref[...] = p[:, n1:]
        wait()

    r_in, r_out, r_shape, r_scratch = _rider_specs(wsend, False)
    return pl.pallas_call(
        body, name=name, grid=(nsteps,),
        in_specs=[pl.BlockSpec((tr, D_MODEL), lambda i: (i, 0)),
                  pl.BlockSpec((1, D_MODEL), lambda i: (0, 0)),
                  pl.BlockSpec((D_MODEL, n), lambda i: (0, 0))] + r_in,
        out_specs=[pl.BlockSpec((tr, n1), lambda i: (i, 0)),
                   pl.BlockSpec((tr, n - n1), lambda i: (i, 0))] + r_out,
        out_shape=[jax.ShapeDtypeStruct((rows, n1), F32), jax.ShapeDtypeStruct((rows, n - n1), F32)] + r_shape,
        scratch_shapes=r_scratch,
        compiler_params=_cparams(("arbitrary",)),
    )(h, g, w, *([] if wsend is None else [wsend]))


def _norm_proj_bwd(h, g, w, dp1, dp2, dh_in, name, gsend=None):
    rows, n = h.shape[0], w.shape[1]
    n1 = dp1.shape[1]
    tr = _row_block(rows)
    nsteps = rows // tr
    extra = 0 if gsend is None else 1

    def body(h_ref, g_ref, w_ref, dp1_ref, dp2_ref, dhin_ref, *rest):
        dh_ref, dw_ref, dg_ref = rest[extra:extra + 3]
        i = pl.program_id(0)
        start, wait = _rider(gsend, rest[:extra] + rest[extra + 3:], i == 0, i == nsteps - 1, True)
        start()

        @pl.when(i == 0)
        def _():
            dw_ref[...] = jnp.zeros_like(dw_ref)
            dg_ref[...] = jnp.zeros_like(dg_ref)

        gv = g_ref[...]
        xn, r = _rms_fwd(h_ref[...])
        hn = (xn * gv).astype(BF16)
        dp = jnp.concatenate([dp1_ref[...].astype(BF16), dp2_ref[...].astype(BF16)], axis=1)
        dw_ref[...] += lax.dot_general(hn, dp, TN, preferred_element_type=F32)
        dhn = lax.dot_general(dp, w_ref[...], NT, preferred_element_type=F32)
        dx, dg = _rms_bwd(dhn, xn, r, gv)
        dg_ref[...] += dg
        dh_ref[...] = dhin_ref[...] + dx
        wait()

    r_in, r_out, r_shape, r_scratch = _rider_specs(gsend, True)
    return pl.pallas_call(
        body, name=name, grid=(nsteps,),
        in_specs=[pl.BlockSpec((tr, D_MODEL), lambda i: (i, 0)),
                  pl.BlockSpec((1, D_MODEL), lambda i: (0, 0)),
                  pl.BlockSpec((D_MODEL, n), lambda i: (0, 0)),
                  pl.BlockSpec((tr, n1), lambda i: (i, 0)),
                  pl.BlockSpec((tr, n - n1), lambda i: (i, 0)),
                  pl.BlockSpec((tr, D_MODEL), lambda i: (i, 0))] + r_in,
        out_specs=[pl.BlockSpec((tr, D_MODEL), lambda i: (i, 0)),
                   pl.BlockSpec((D_MODEL, n), lambda i: (0, 0)),
                   pl.BlockSpec((1, D_MODEL), lambda i: (0, 0))] + r_out,
        out_shape=[jax.ShapeDtypeStruct((rows, D_MODEL), F32),
                   jax.ShapeDtypeStruct((D_MODEL, n), F32),
                   jax.ShapeDtypeStruct((1, D_MODEL), F32)] + r_shape,
        scratch_shapes=r_scratch,
        compiler_params=_cparams(("arbitrary",)),
    )(h, g, w, dp1, dp2, dh_in, *([] if gsend is None else [gsend]))


def _gated_out_fwd(a, gate, h, w, name):
    rows = a.shape[0]
    tr = _row_block(rows)

    def body(a_ref, gate_ref, h_ref, w_ref, o_ref):
        y = (a_ref[...] * _silu(gate_ref[...])).astype(BF16)
        o_ref[...] = h_ref[...] + jnp.dot(y, w_ref[...], preferred_element_type=F32)

    blk = pl.BlockSpec((tr, D_MODEL), lambda i: (i, 0))
    return pl.pallas_call(
        body, name=name, grid=(rows // tr,),
        in_specs=[blk, blk, blk, pl.BlockSpec((D_MODEL, D_MODEL), lambda i: (0, 0))],
        out_specs=blk,
        out_shape=jax.ShapeDtypeStruct((rows, D_MODEL), F32),
        compiler_params=_cparams(("parallel",)),
    )(a, gate, h, w)


def _gated_out_bwd(a, gate, dh, w, da_dtype, with_delta, name):
    rows = a.shape[0]
    tr = _row_block(rows)

    def body(a_ref, gate_ref, dh_ref, w_ref, da_ref, dgate_ref, dw_ref, *delta_ref):
        @pl.when(pl.program_id(0) == 0)
        def _():
            dw_ref[...] = jnp.zeros_like(dw_ref)

        av, gv = a_ref[...], gate_ref[...]
        sg = _silu(gv)
        dhb = dh_ref[...].astype(BF16)
        dw_ref[...] += lax.dot_general((av * sg).astype(BF16), dhb, TN, preferred_element_type=F32)
        dy = lax.dot_general(dhb, w_ref[...], NT, preferred_element_type=F32)
        da = (dy * sg).astype(da_dtype)
        da_ref[...] = da
        dgate_ref[...] = dy * av * _dsilu(gv)
        if with_delta:
            prod = da.astype(F32) * av
            lane = lax.broadcasted_iota(jnp.int32, (tr, LANES), 1)
            per_head = jnp.zeros((tr, LANES), F32)
            for hd in range(HEADS):
                dsum = jnp.sum(prod[:, hd * V_HEAD:(hd + 1) * V_HEAD], axis=1, keepdims=True)
                per_head = jnp.where(lane == hd, dsum, per_head)
            delta_t = per_head.T
            for hd in range(HEADS):
                delta_ref[0][hd, 0] = delta_t[hd:hd + 1, :]

    blk = pl.BlockSpec((tr, D_MODEL), lambda i: (i, 0))
    wblk = pl.BlockSpec((D_MODEL, D_MODEL), lambda i: (0, 0))
    out_specs = [blk, blk, wblk]
    out_shape = [jax.ShapeDtypeStruct((rows, D_MODEL), da_dtype),
                 jax.ShapeDtypeStruct((rows, D_MODEL), F32),
                 jax.ShapeDtypeStruct((D_MODEL, D_MODEL), F32)]
    if with_delta:
        out_specs.append(pl.BlockSpec((HEADS, 1, 1, tr), lambda i: (0, i, 0, 0)))
        out_shape.append(jax.ShapeDtypeStruct((HEADS, rows // tr, 1, tr), F32))
    return pl.pallas_call(
        body, name=name, grid=(rows // tr,),
        in_specs=[blk, blk, blk, wblk],
        out_specs=out_specs, out_shape=out_shape,
        compiler_params=_cparams(("arbitrary",)),
    )(a, gate, dh, w)


def _rope(v, cos, sin, lane):
    swapped = jnp.where(lane < QK_ROPE // 2, pltpu.roll(v, LANES - QK_ROPE // 2, 1), pltpu.roll(v, QK_ROPE // 2, 1))
    return v * cos + swapped * sin


def _unrope(dv, cos, sin, lane):
    t = dv * sin
    swapped = jnp.where(lane < QK_ROPE // 2, pltpu.roll(t, LANES - QK_ROPE // 2, 1), pltpu.roll(t, QK_ROPE // 2, 1))
    return dv * cos + swapped


def _mla_qkv_fwd(lat, gq, gkv, wuq, wukv, cos, sin, scale):
    rows = lat.shape[0]
    tr = _row_block(rows)

    def body(lat_ref, gq_ref, gkv_ref, wuq_ref, wukv_ref, cos_ref, sin_ref, qc_ref, kc_ref, v_ref, vt_ref):
        qn, _ = _rms_fwd(lat_ref[:, :Q_LORA])
        kvn, _ = _rms_fwd(lat_ref[:, Q_LORA:LAT])
        q = jnp.dot((qn * gq_ref[...]).astype(BF16), wuq_ref[...], preferred_element_type=F32)
        kv = jnp.dot((kvn * gkv_ref[...]).astype(BF16), wukv_ref[...], preferred_element_type=F32)
        c, s = cos_ref[...], sin_ref[...]
        lane = lax.broadcasted_iota(jnp.int32, (tr, LANES), 1)
        kr = _rope(lat_ref[:, LAT:LAT_PAD], c, s, lane).astype(BF16)
        for hd in range(HEADS):
            o = hd * HEAD_PAD
            qc_ref[:, o:o + QK_NOPE] = (q[:, o:o + QK_NOPE] * scale).astype(BF16)
            qc_ref[:, o + QK_NOPE:o + HEAD_PAD] = (_rope(q[:, o + QK_NOPE:o + HEAD_PAD], c, s, lane) * scale).astype(BF16)
            kc_ref[:, o:o + QK_NOPE] = kv[:, o:o + QK_NOPE].astype(BF16)
            kc_ref[:, o + QK_NOPE:o + HEAD_PAD] = kr
            vh = kv[:, o + QK_NOPE:o + HEAD_PAD]
            v_ref[:, hd * V_HEAD:(hd + 1) * V_HEAD] = vh.astype(BF16)
            vt_ref[hd, 0] = vh.T.astype(BF16)

    full = lambda shape: pl.BlockSpec(shape, lambda i: (0, 0))
    rowb = lambda n: pl.BlockSpec((tr, n), lambda i: (i, 0))
    return pl.pallas_call(
        body, name="mla_qkv_fwd", grid=(rows // tr,),
        in_specs=[rowb(LAT_PAD), full((1, Q_LORA)), full((1, KV_LORA)), full((Q_LORA, HEADS * HEAD_PAD)),
                  full((KV_LORA, HEADS * HEAD_PAD)), rowb(LANES), rowb(LANES)],
        out_specs=[rowb(HEADS * HEAD_PAD), rowb(HEADS * HEAD_PAD), rowb(HEADS * V_HEAD),
                   pl.BlockSpec((HEADS, 1, V_HEAD, tr), lambda i: (0, i, 0, 0))],
        out_shape=[jax.ShapeDtypeStruct((rows, HEADS * HEAD_PAD), BF16),
                   jax.ShapeDtypeStruct((rows, HEADS * HEAD_PAD), BF16),
                   jax.ShapeDtypeStruct((rows, HEADS * V_HEAD), BF16),
                   jax.ShapeDtypeStruct((HEADS, rows // tr, V_HEAD, tr), BF16)],
        compiler_params=_cparams(("parallel",)),
    )(lat, gq, gkv, wuq, wukv, cos, sin)


def _mla_qkv_bwd(lat, gq, gkv, wuq, wukv, cos, sin, dqc, dkc, dv, scale):
    rows = lat.shape[0]
    tr = _row_block(rows)

    def body(lat_ref, gq_ref, gkv_ref, wuq_ref, wukv_ref, cos_ref, sin_ref, dqc_ref, dkc_ref, dv_ref,
             dlat_ref, dwuq_ref, dwukv_ref, dgq_ref, dgkv_ref):
        @pl.when(pl.program_id(0) == 0)
        def _():
            dwuq_ref[...] = jnp.zeros_like(dwuq_ref)
            dwukv_ref[...] = jnp.zeros_like(dwukv_ref)
            dgq_ref[...] = jnp.zeros_like(dgq_ref)
            dgkv_ref[...] = jnp.zeros_like(dgkv_ref)

        c, s = cos_ref[...], sin_ref[...]
        lane = lax.broadcasted_iota(jnp.int32, (tr, LANES), 1)
        gqv, gkvv = gq_ref[...], gkv_ref[...]
        qn, rq = _rms_fwd(lat_ref[:, :Q_LORA])
        kvn, rkv = _rms_fwd(lat_ref[:, Q_LORA:LAT])
        dq_parts, dkv_parts = [], []
        dkr = jnp.zeros((tr, LANES), F32)
        for hd in range(HEADS):
            o = hd * HEAD_PAD
            dq_parts.append(dqc_ref[:, o:o + QK_NOPE])
            dq_parts.append(_unrope(dqc_ref[:, o + QK_NOPE:o + HEAD_PAD].astype(F32), c, s, lane).astype(BF16))
            dkv_parts.append(dkc_ref[:, o:o + QK_NOPE])
            dkv_parts.append(dv_ref[:, hd * V_HEAD:(hd + 1) * V_HEAD])
            dkr = dkr + dkc_ref[:, o + QK_NOPE:o + HEAD_PAD].astype(F32)
        dq = jnp.concatenate(dq_parts, axis=1)
        dkv = jnp.concatenate(dkv_parts, axis=1)
        dwuq_ref[...] += scale * lax.dot_general((qn * gqv).astype(BF16), dq, TN, preferred_element_type=F32)
        dwukv_ref[...] += lax.dot_general((kvn * gkvv).astype(BF16), dkv, TN, preferred_element_type=F32)
        dqn = scale * lax.dot_general(dq, wuq_ref[...], NT, preferred_element_type=F32)
        dkvn = lax.dot_general(dkv, wukv_ref[...], NT, preferred_element_type=F32)
        dqlat, dgq = _rms_bwd(dqn, qn, rq, gqv)
        dkvlat, dgkv = _rms_bwd(dkvn, kvn, rkv, gkvv)
        dgq_ref[...] += dgq
        dgkv_ref[...] += dgkv
        dlat_ref[:, :Q_LORA] = dqlat
        dlat_ref[:, Q_LORA:LAT] = dkvlat
        dlat_ref[:, LAT:LAT_PAD] = _unrope(dkr, c, s, lane)

    full = lambda shape: pl.BlockSpec(shape, lambda i: (0, 0))
    rowb = lambda n: pl.BlockSpec((tr, n), lambda i: (i, 0))
    return pl.pallas_call(
        body, name="mla_qkv_bwd", grid=(rows // tr,),
        in_specs=[rowb(LAT_PAD), full((1, Q_LORA)), full((1, KV_LORA)), full((Q_LORA, HEADS * HEAD_PAD)),
                  full((KV_LORA, HEADS * HEAD_PAD)), rowb(LANES), rowb(LANES),
                  rowb(HEADS * HEAD_PAD), rowb(HEADS * HEAD_PAD), rowb(HEADS * V_HEAD)],
        out_specs=[rowb(LAT_PAD), full((Q_LORA, HEADS * HEAD_PAD)), full((KV_LORA, HEADS * HEAD_PAD)),
                   full((1, Q_LORA)), full((1, KV_LORA))],
        out_shape=[jax.ShapeDtypeStruct((rows, LAT_PAD), F32),
                   jax.ShapeDtypeStruct((Q_LORA, HEADS * HEAD_PAD), F32),
                   jax.ShapeDtypeStruct((KV_LORA, HEADS * HEAD_PAD), F32),
                   jax.ShapeDtypeStruct((1, Q_LORA), F32),
                   jax.ShapeDtypeStruct((1, KV_LORA), F32)],
        compiler_params=_cparams(("arbitrary",)),
    )(lat, gq, gkv, wuq, wukv, cos, sin, dqc, dkc, dv)


def _causal_mask_t(t):
    key = lax.broadcasted_iota(jnp.int32, (t, t), 0)
    query = lax.broadcasted_iota(jnp.int32, (t, t), 1)
    return key <= query


def _attn_fwd(qc, kc, vt, wsend):
    rows = qc.shape[0]
    t = _row_block(rows)
    nblk = rows // t

    def body(q_ref, k_ref, vt_ref, wsend_ref, o_ref, lse_ref, wall_ref, m_ref, l_ref, acc_ref, st_a, st_b,
             send_sems, recv_sems, local_sem):
        i = pl.program_id(1)
        gather = lambda: _exchange_copies(lambda d: wsend_ref, wall_ref, send_sems, recv_sems, local_sem.at[0])

        @pl.when(jnp.logical_and(pl.program_id(0) == 0, i == 0))
        def _():
            for cp in gather():
                cp.start()

        m_ref[...] = jnp.full_like(m_ref, MASK_VALUE)
        l_ref[...] = jnp.zeros_like(l_ref)
        acc_ref[...] = jnp.zeros_like(acc_ref)
        q = q_ref[...]

        def scores(j, st_ref):
            rs = pl.ds(pl.multiple_of(j * t, t), t)
            st_ref[...] = lax.dot_general(k_ref[rs, :], q, NT, preferred_element_type=F32)

        def consume(j, st_ref, masked):
            st = st_ref[...]
            if masked:
                st = jnp.where(_causal_mask_t(t), st, MASK_VALUE)
            m_prev = m_ref[...]
            m_new = jnp.maximum(m_prev, jnp.max(st, axis=0, keepdims=True))
            alpha = jnp.exp(m_prev - m_new)
            pt = jnp.exp(st - m_new)
            l_ref[...] = alpha * l_ref[...] + jnp.sum(pt, axis=0, keepdims=True)
            acc_ref[...] = alpha * acc_ref[...] + jnp.dot(vt_ref[0, j], pt.astype(BF16), preferred_element_type=F32)
            m_ref[...] = m_new

        scores(0, st_a)

        def pair(jj, carry):
            j0 = 2 * jj
            scores(j0 + 1, st_b)
            consume(j0, st_a, False)
            scores(j0 + 2, st_a)
            consume(j0 + 1, st_b, False)
            return carry

        lax.fori_loop(0, i // 2, pair, 0)

        @pl.when(i % 2 == 0)
        def _():
            consume(i, st_a, True)

        @pl.when(i % 2 == 1)
        def _():
            scores(i, st_b)
            consume(i - 1, st_a, False)
            consume(i, st_b, True)

        o_ref[...] = (acc_ref[...] / l_ref[...]).T
        lse_ref[0, 0] = m_ref[...] + jnp.log(l_ref[...])

        @pl.when(jnp.logical_and(pl.program_id(0) == HEADS - 1, i == nblk - 1))
        def _():
            for cp in gather():
                cp.wait()

    hbm = pl.BlockSpec(memory_space=pl.ANY)
    return pl.pallas_call(
        body, name="attn_fwd", grid=(HEADS, nblk),
        in_specs=[pl.BlockSpec((t, HEAD_PAD), lambda h, i: (i, h)),
                  pl.BlockSpec((rows, HEAD_PAD), lambda h, i: (0, h)),
                  pl.BlockSpec((1, nblk, V_HEAD, t), lambda h, i: (h, 0, 0, 0)),
                  hbm],
        out_specs=[pl.BlockSpec((t, V_HEAD), lambda h, i: (i, h)),
                   pl.BlockSpec((1, 1, 1, t), lambda h, i: (h, i, 0, 0)),
                   hbm],
        out_shape=[jax.ShapeDtypeStruct((rows, HEADS * V_HEAD), F32),
                   jax.ShapeDtypeStruct((HEADS, nblk, 1, t), F32),
                   jax.ShapeDtypeStruct((N_DEV,) + wsend.shape, wsend.dtype)],
        scratch_shapes=[pltpu.VMEM((1, t), F32), pltpu.VMEM((1, t), F32), pltpu.VMEM((V_HEAD, t), F32),
                        pltpu.VMEM((t, t), F32), pltpu.VMEM((t, t), F32)] + _exchange_sems(),
        compiler_params=_cparams(("arbitrary", "arbitrary")),
    )(qc, kc, vt, wsend)


def _attn_bwd(qc, kc, v, lse, delta, do, gsend):
    rows = qc.shape[0]
    t = _row_block(rows)
    nblk = rows // t

    def body(q_ref, k_ref, v_ref, lse_ref, delta_ref, do_ref, gsend_ref, dq_ref, dk_ref, dv_ref, land_ref,
             dq_acc, dk_acc, dv_acc, st_a, dp_a, st_b, dp_b, send_sems, recv_sems, local_sem):
        j = pl.program_id(1)
        exchange = lambda: _exchange_copies(lambda d: gsend_ref.at[d], land_ref, send_sems, recv_sems, local_sem.at[0])

        @pl.when(jnp.logical_and(pl.program_id(0) == 0, j == 0))
        def _():
            for cp in exchange():
                cp.start()

        @pl.when(j == 0)
        def _():
            dq_acc[...] = jnp.zeros_like(dq_acc)

        dk_acc[...] = jnp.zeros_like(dk_acc)
        dv_acc[...] = jnp.zeros_like(dv_acc)
        k = k_ref[...]
        vv = v_ref[...]

        def products(i, st_ref, dp_ref):
            rs = pl.ds(pl.multiple_of(i * t, t), t)
            st_ref[...] = lax.dot_general(k, q_ref[rs, :], NT, preferred_element_type=F32)
            dp_ref[...] = lax.dot_general(vv, do_ref[rs, :], NT, preferred_element_type=F32)

        def consume(i, st_ref, dp_ref):
            rs = pl.ds(pl.multiple_of(i * t, t), t)
            q = q_ref[rs, :]
            dob = do_ref[rs, :]
            st = jnp.where(jnp.logical_or(_causal_mask_t(t), i != j), st_ref[...], MASK_VALUE)
            pt = jnp.exp(st - lse_ref[0, i])
            dv_acc[...] += jnp.dot(pt.astype(BF16), dob, preferred_element_type=F32)
            dst = (pt * (dp_ref[...] - delta_ref[0, i])).astype(BF16)
            dk_acc[...] += jnp.dot(dst, q, preferred_element_type=F32)
            dq_acc[rs, :] += lax.dot_general(dst, k, TN, preferred_element_type=F32)

        last = nblk - 1 - j
        products(j, st_a, dp_a)

        def pair(pp, carry):
            i0 = j + 2 * pp
            products(i0 + 1, st_b, dp_b)
            consume(i0, st_a, dp_a)
            products(i0 + 2, st_a, dp_a)
            consume(i0 + 1, st_b, dp_b)
            return carry

        lax.fori_loop(0, last // 2, pair, 0)

        @pl.when(last % 2 == 0)
        def _():
            consume(nblk - 1, st_a, dp_a)

        @pl.when(last % 2 == 1)
        def _():
            products(nblk - 1, st_b, dp_b)
            consume(nblk - 2, st_a, dp_a)
            consume(nblk - 1, st_b, dp_b)

        dk_ref[...] = dk_acc[...].astype(BF16)
        dv_ref[...] = dv_acc[...].astype(BF16)

        @pl.when(j == nblk - 1)
        def _():
            dq_ref[...] = dq_acc[...].astype(BF16)

        @pl.when(jnp.logical_and(pl.program_id(0) == HEADS - 1, j == nblk - 1))
        def _():
            for cp in exchange():
                cp.wait()

    stat = pl.BlockSpec((1, nblk, 1, t), lambda h, j: (h, 0, 0, 0))
    hbm = pl.BlockSpec(memory_space=pl.ANY)
    return pl.pallas_call(
        body, name="attn_bwd", grid=(HEADS, nblk),
        in_specs=[pl.BlockSpec((rows, HEAD_PAD), lambda h, j: (0, h)),
                  pl.BlockSpec((t, HEAD_PAD), lambda h, j: (j, h)),
                  pl.BlockSpec((t, V_HEAD), lambda h, j: (j, h)),
                  stat, stat,
                  pl.BlockSpec((rows, V_HEAD), lambda h, j: (0, h)),
                  hbm],
        out_specs=[pl.BlockSpec((rows, HEAD_PAD), lambda h, j: (0, h)),
                   pl.BlockSpec((t, HEAD_PAD), lambda h, j: (j, h)),
                   pl.BlockSpec((t, V_HEAD), lambda h, j: (j, h)),
                   hbm],
        out_shape=[jax.ShapeDtypeStruct((rows, HEADS * HEAD_PAD), BF16),
                   jax.ShapeDtypeStruct((rows, HEADS * HEAD_PAD), BF16),
                   jax.ShapeDtypeStruct((rows, HEADS * V_HEAD), BF16),
                   jax.ShapeDtypeStruct(gsend.shape, gsend.dtype)],
        scratch_shapes=[pltpu.VMEM((rows, HEAD_PAD), F32), pltpu.VMEM((t, HEAD_PAD), F32),
                        pltpu.VMEM((t, V_HEAD), F32)] + [pltpu.VMEM((t, t), F32)] * 4 + _exchange_sems(),
        compiler_params=_cparams(("arbitrary", "arbitrary")),
    )(qc, kc, v, lse, delta, do, gsend)


def _shift_down(prev_tile, x, k):
    xx = jnp.concatenate([prev_tile, x], axis=0)
    return pltpu.roll(xx, k, 0)[SUBLANES:]


def _shift_up(x, next_tile, k):
    n = x.shape[0]
    xx = jnp.concatenate([x, next_tile], axis=0)
    return pltpu.roll(xx, n + SUBLANES - k, 0)[:n]


def _lru_gates(u, u_prev, cw_ref, cb_ref, wrg_ref, brg_ref, wig_ref, big_ref, lam_ref, first_block):
    taps = [_shift_down(u_prev, u, CONV_WIDTH - 1 - j) if j < CONV_WIDTH - 1 else u for j in range(CONV_WIDTH)]
    uc = cb_ref[...] + taps[0] * cw_ref[0:1, :]
    for j in range(1, CONV_WIDTH):
        uc = uc + taps[j] * cw_ref[j:j + 1, :]
    ub = uc.astype(BF16)
    zr = jnp.concatenate([jnp.dot(ub[:, g * LRU_BLOCK:(g + 1) * LRU_BLOCK], wrg_ref[g], preferred_element_type=F32)
                          for g in range(LRU_BLOCKS)], axis=1) + brg_ref[...]
    zi = jnp.concatenate([jnp.dot(ub[:, g * LRU_BLOCK:(g + 1) * LRU_BLOCK], wig_ref[g], preferred_element_type=F32)
                          for g in range(LRU_BLOCKS)], axis=1) + big_ref[...]
    r = jax.nn.sigmoid(zr)
    ig = jax.nn.sigmoid(zi)
    sp = _softplus_neg(lam_ref[...])
    log_a = (-LRU_C) * r * sp
    a = jnp.exp(log_a)
    mult_raw = jnp.sqrt(-_expm1_neg(2.0 * log_a))
    row = lax.broadcasted_iota(jnp.int32, u.shape, 0)
    is_start = jnp.logical_and(first_block, row == 0)
    mult = jnp.where(is_start, 1.0, mult_raw)
    return dict(taps=taps, uc=uc, ub=ub, r=r, ig=ig, sp=sp, a=a, mult=mult, mult_raw=mult_raw, is_start=is_start)


def _rglru_fwd(u, cw, cb, wrg, brg, wig, big, lam):
    rows = u.shape[0]
    tb = _row_block(rows)

    def body(u_ref, cw_ref, cb_ref, wrg_ref, brg_ref, wig_ref, big_ref, lam_ref, hs_ref, utail, hcar, a_s, b_s):
        i = pl.program_id(0)

        @pl.when(i == 0)
        def _():
            utail[...] = jnp.zeros_like(utail)
            hcar[...] = jnp.zeros_like(hcar)

        u = u_ref[...]
        gt = _lru_gates(u, utail[...], cw_ref, cb_ref, wrg_ref, brg_ref, wig_ref, big_ref, lam_ref, i == 0)
        a_s[...] = gt["a"]
        b_s[...] = gt["mult"] * (gt["ig"] * gt["uc"])
        row8 = lax.broadcasted_iota(jnp.int32, (SUBLANES, LRU_WIDTH), 0)

        def tile(tix, carry):
            rs = pl.ds(pl.multiple_of(tix * SUBLANES, SUBLANES), SUBLANES)
            av, bv = a_s[rs, :], b_s[rs, :]
            for k in (1, 2, 4):
                keep = row8 >= k
                bv = jnp.where(keep, av * pltpu.roll(bv, k, 0) + bv, bv)
                av = jnp.where(keep, av * pltpu.roll(av, k, 0), av)
            h8 = av * carry + bv
            hs_ref[rs, :] = h8
            return jnp.broadcast_to(h8[SUBLANES - 1:SUBLANES, :], (SUBLANES, LRU_WIDTH))

        hcar[...] = lax.fori_loop(0, tb // SUBLANES, tile, hcar[...])
        utail[...] = u[tb - SUBLANES:, :]

    full2 = lambda shape: pl.BlockSpec(shape, lambda i: (0, 0))
    full3 = lambda shape: pl.BlockSpec(shape, lambda i: (0, 0, 0))
    blk = pl.BlockSpec((tb, LRU_WIDTH), lambda i: (i, 0))
    return pl.pallas_call(
        body, name="rglru_fwd", grid=(rows // tb,),
        in_specs=[blk, full2((CONV_WIDTH, LRU_WIDTH)), full2((1, LRU_WIDTH)),
                  full3((LRU_BLOCKS, LRU_BLOCK, LRU_BLOCK)), full2((1, LRU_WIDTH)),
                  full3((LRU_BLOCKS, LRU_BLOCK, LRU_BLOCK)), full2((1, LRU_WIDTH)), full2((1, LRU_WIDTH))],
        out_specs=blk,
        out_shape=jax.ShapeDtypeStruct((rows, LRU_WIDTH), F32),
        scratch_shapes=[pltpu.VMEM((SUBLANES, LRU_WIDTH), F32), pltpu.VMEM((SUBLANES, LRU_WIDTH), F32),
                        pltpu.VMEM((tb, LRU_WIDTH), F32), pltpu.VMEM((tb, LRU_WIDTH), F32)],
        compiler_params=_cparams(("arbitrary",)),
    )(u, cw, cb, wrg, brg, wig, big, lam)


def _rglru_bwd(u, hs, dhs, cw, cb, wrg, brg, wig, big, lam):
    rows = u.shape[0]
    tb = _row_block(rows)
    nblk = rows // tb
    tiles = tb // SUBLANES

    def body(u_ref, up_ref, hs_ref, hp_ref, dhs_ref, cw_ref, cb_ref, wrg_ref, brg_ref, wig_ref, big_ref, lam_ref,
             du_ref, dcw_ref, dcb_ref, dwrg_ref, dbrg_ref, dwig_ref, dbig_ref, dlam_ref,
             gcar, duc_head, a_s, b_s, g_s, dsp_acc):
        step = pl.program_id(0)
        blk_ix = nblk - 1 - step

        @pl.when(step == 0)
        def _():
            for ref in (dcw_ref, dcb_ref, dwrg_ref, dbrg_ref, dwig_ref, dbig_ref, gcar, duc_head, dsp_acc):
                ref[...] = jnp.zeros_like(ref)

        first = blk_ix == 0
        u = u_ref[...]
        u_prev = jnp.where(first, 0.0, up_ref[...])
        h_prev_tile = jnp.where(first, 0.0, hp_ref[...])
        gt = _lru_gates(u, u_prev, cw_ref, cb_ref, wrg_ref, brg_ref, wig_ref, big_ref, lam_ref, first)
        a, r, ig, uc, mult = gt["a"], gt["r"], gt["ig"], gt["uc"], gt["mult"]
        dhs_v = dhs_ref[...]

        a_s[...] = a
        b_s[...] = a * dhs_v
        row8 = lax.broadcasted_iota(jnp.int32, (SUBLANES, LRU_WIDTH), 0)

        def tile(tix, carry):
            rs = pl.ds(pl.multiple_of((tiles - 1 - tix) * SUBLANES, SUBLANES), SUBLANES)
            av, bv = a_s[rs, :], b_s[rs, :]
            for k in (1, 2, 4):
                keep = row8 < SUBLANES - k
                bv = jnp.where(keep, av * pltpu.roll(bv, SUBLANES - k, 0) + bv, bv)
                av = jnp.where(keep, av * pltpu.roll(av, SUBLANES - k, 0), av)
            g8 = av * carry + bv
            g_s[rs, :] = g8
            return jnp.broadcast_to(g8[0:1, :], (SUBLANES, LRU_WIDTH))

        g_next = gcar[...]
        gcar[...] = lax.fori_loop(0, tiles, tile, g_next)
        g = dhs_v + _shift_up(g_s[...], g_next, 1)

        h_prev = _shift_down(h_prev_tile, hs_ref[...], 1)
        da = g * h_prev
        iu = ig * uc
        dmult = jnp.where(gt["is_start"], 0.0, g * iu)
        d_ig = g * mult * uc
        duc = g * mult * ig
        dlog_a = da * a - dmult * (a * a) / gt["mult_raw"]
        dzr = (dlog_a * ((-LRU_C) * gt["sp"])) * r * (1.0 - r)
        dsp_acc[...] += jnp.sum(dlog_a * ((-LRU_C) * r), axis=0, keepdims=True)
        dzi = d_ig * ig * (1.0 - ig)
        dbrg_ref[...] += jnp.sum(dzr, axis=0, keepdims=True)
        dbig_ref[...] += jnp.sum(dzi, axis=0, keepdims=True)
        dzr_b, dzi_b = dzr.astype(BF16), dzi.astype(BF16)
        ub = gt["ub"]
        duc_parts = []
        for gi in range(LRU_BLOCKS):
            cs = slice(gi * LRU_BLOCK, (gi + 1) * LRU_BLOCK)
            dwrg_ref[gi] += lax.dot_general(ub[:, cs], dzr_b[:, cs], TN, preferred_element_type=F32)
            dwig_ref[gi] += lax.dot_general(ub[:, cs], dzi_b[:, cs], TN, preferred_element_type=F32)
            duc_parts.append(lax.dot_general(dzr_b[:, cs], wrg_ref[gi], NT, preferred_element_type=F32)
                             + lax.dot_general(dzi_b[:, cs], wig_ref[gi], NT, preferred_element_type=F32))
        duc = duc + jnp.concatenate(duc_parts, axis=1)

        dcb_ref[...] += jnp.sum(duc, axis=0, keepdims=True)
        taps = gt["taps"]
        for jt in range(CONV_WIDTH):
            dcw_ref[jt:jt + 1, :] += jnp.sum(duc * taps[jt], axis=0, keepdims=True)
        head = duc_head[...]
        du = duc * cw_ref[CONV_WIDTH - 1:CONV_WIDTH, :]
        for jt in range(CONV_WIDTH - 1):
            du = du + _shift_up(duc, head, CONV_WIDTH - 1 - jt) * cw_ref[jt:jt + 1, :]
        du_ref[...] = du
        duc_head[...] = duc[:SUBLANES, :]

        @pl.when(step == nblk - 1)
        def _():
            dlam_ref[...] = -dsp_acc[...] * jax.nn.sigmoid(-lam_ref[...])

    full2 = lambda shape: pl.BlockSpec(shape, lambda s: (0, 0))
    full3 = lambda shape: pl.BlockSpec(shape, lambda s: (0, 0, 0))
    blk = pl.BlockSpec((tb, LRU_WIDTH), lambda s: (nblk - 1 - s, 0))
    prev_tile = pl.BlockSpec((SUBLANES, LRU_WIDTH), lambda s: (jnp.maximum((nblk - 1 - s) * tiles - 1, 0), 0))
    wshape = (LRU_BLOCKS, LRU_BLOCK, LRU_BLOCK)
    return pl.pallas_call(
        body, name="rglru_bwd", grid=(nblk,),
        in_specs=[blk, prev_tile, blk, prev_tile, blk, full2((CONV_WIDTH, LRU_WIDTH)), full2((1, LRU_WIDTH)),
                  full3(wshape), full2((1, LRU_WIDTH)), full3(wshape), full2((1, LRU_WIDTH)), full2((1, LRU_WIDTH))],
        out_specs=[blk, full2((CONV_WIDTH, LRU_WIDTH)), full2((1, LRU_WIDTH)), full3(wshape), full2((1, LRU_WIDTH)),
                   full3(wshape), full2((1, LRU_WIDTH)), full2((1, LRU_WIDTH))],
        out_shape=[jax.ShapeDtypeStruct((rows, LRU_WIDTH), F32),
                   jax.ShapeDtypeStruct((CONV_WIDTH, LRU_WIDTH), F32), jax.ShapeDtypeStruct((1, LRU_WIDTH), F32),
                   jax.ShapeDtypeStruct(wshape, F32), jax.ShapeDtypeStruct((1, LRU_WIDTH), F32),
                   jax.ShapeDtypeStruct(wshape, F32), jax.ShapeDtypeStruct((1, LRU_WIDTH), F32),
                   jax.ShapeDtypeStruct((1, LRU_WIDTH), F32)],
        scratch_shapes=[pltpu.VMEM((SUBLANES, LRU_WIDTH), F32), pltpu.VMEM((SUBLANES, LRU_WIDTH), F32),
                        pltpu.VMEM((tb, LRU_WIDTH), F32), pltpu.VMEM((tb, LRU_WIDTH), F32),
                        pltpu.VMEM((tb, LRU_WIDTH), F32), pltpu.VMEM((1, LRU_WIDTH), F32)],
        compiler_params=_cparams(("arbitrary",)),
    )(u, u, hs, hs, dhs, cw, cb, wrg, brg, wig, big, lam)


def _final_loss(h, gf, target, n_real):
    rows = h.shape[0]
    tr = _row_block(rows)

    def body(h_ref, g_ref, t_ref, dh_ref, loss_ref, dg_ref):
        i = pl.program_id(0)

        @pl.when(i == 0)
        def _():
            loss_ref[...] = jnp.zeros_like(loss_ref)
            dg_ref[...] = jnp.zeros_like(dg_ref)

        gv = g_ref[...]
        xn, r = _rms_fwd(h_ref[...])
        row = i * tr + lax.broadcasted_iota(jnp.int32, (tr, 1), 0)
        live = jnp.logical_and(row >= N_META, row < n_real)
        err = jnp.where(live, xn * gv - t_ref[...], 0.0)
        loss_ref[...] += (0.5 / D_MODEL) * jnp.sum(jnp.sum(err * err, axis=1, keepdims=True), axis=0, keepdims=True)
        dx, dg = _rms_bwd(err * (1.0 / D_MODEL), xn, r, gv)
        dg_ref[...] += dg
        dh_ref[...] = dx

    blk = pl.BlockSpec((tr, D_MODEL), lambda i: (i, 0))
    return pl.pallas_call(
        body, name="final_loss", grid=(rows // tr,),
        in_specs=[blk, pl.BlockSpec((1, D_MODEL), lambda i: (0, 0)), blk],
        out_specs=[blk, pl.BlockSpec((1, 1), lambda i: (0, 0)), pl.BlockSpec((1, D_MODEL), lambda i: (0, 0))],
        out_shape=[jax.ShapeDtypeStruct((rows, D_MODEL), F32), jax.ShapeDtypeStruct((1, 1), F32),
                   jax.ShapeDtypeStruct((1, D_MODEL), F32)],
        compiler_params=_cparams(("arbitrary",)),
    )(h, gf, target)


def _my_place():
    x, y, c = lax.axis_index("x"), lax.axis_index("y"), lax.axis_index("c")
    return x, y, c, 4 * x + 2 * y + c


def _peer(x, y, c, k):
    px, py, pc = x ^ (k >> 2), y ^ ((k >> 1) & 1), c ^ (k & 1)
    return (px, py, pc), 4 * px + 2 * py + pc


def _exchange_copies(src_of, dst_ref, send_sems, recv_sems, local_sem):
    x, y, c, me = _my_place()
    copies = [pltpu.make_async_copy(src_of(me), dst_ref.at[me], local_sem)]
    for k in range(1, N_DEV):
        peer, pid = _peer(x, y, c, k)
        copies.append(pltpu.make_async_remote_copy(
            src_ref=src_of(pid), dst_ref=dst_ref.at[me], send_sem=send_sems.at[k], recv_sem=recv_sems.at[k],
            device_id=peer, device_id_type=MESH))
    return copies


def _exchange_sems():
    return [pltpu.SemaphoreType.DMA((N_DEV,)), pltpu.SemaphoreType.DMA((N_DEV,)), pltpu.SemaphoreType.DMA((1,))]


def _sum_blocks(land):
    n = land.shape[1]
    rc = next(c for c in (672, 512, 256, 128, 64, 32, 16) if n % c == 0)

    def body(land_ref, out_ref):
        acc = land_ref[0].astype(F32)
        for d in range(1, N_DEV):
            acc = acc + land_ref[d].astype(F32)
        out_ref[...] = acc

    return pl.pallas_call(
        body, name="sum_blocks", grid=(n // rc,),
        in_specs=[pl.BlockSpec((N_DEV, rc, LANES), lambda i: (0, i, 0))],
        out_specs=pl.BlockSpec((rc, LANES), lambda i: (i, 0)),
        out_shape=jax.ShapeDtypeStruct((n, LANES), F32),
        compiler_params=_cparams(("parallel",)),
    )(land)


def _all_gather(big, small):
    def body(big_ref, small_ref, obig_ref, osmall_ref, send_sems, recv_sems, local_sems):
        x, y, c, me = _my_place()
        own = [pltpu.make_async_copy(big_ref, obig_ref.at[me], local_sems.at[0]),
               pltpu.make_async_copy(small_ref, osmall_ref.at[me], local_sems.at[1])]
        for cp in own:
            cp.start()
        copies = []
        for k in range(1, N_DEV):
            peer, _ = _peer(x, y, c, k)
            for part, (src, dst) in enumerate(((big_ref, obig_ref), (small_ref, osmall_ref))):
                copies.append(pltpu.make_async_remote_copy(
                    src_ref=src, dst_ref=dst.at[me], send_sem=send_sems.at[part, k], recv_sem=recv_sems.at[part, k],
                    device_id=peer, device_id_type=MESH))
        for cp in copies:
            cp.start()
        for cp in copies:
            cp.wait()
        for cp in own:
            cp.wait()

    n = big.shape[0]
    hbm = pl.BlockSpec(memory_space=pl.ANY)
    return pl.pallas_call(
        body, name="weight_all_gather",
        in_specs=[hbm, hbm], out_specs=[hbm, hbm],
        out_shape=[jax.ShapeDtypeStruct((N_DEV, n, LANES), BF16), jax.ShapeDtypeStruct((N_DEV,) + small.shape, F32)],
        scratch_shapes=[pltpu.SemaphoreType.DMA((2, N_DEV)), pltpu.SemaphoreType.DMA((2, N_DEV)),
                        pltpu.SemaphoreType.DMA((2,))],
        compiler_params=pltpu.CompilerParams(has_side_effects=True),
    )(big, small)


GRAD_CHUNK = 32


def _grad_exchange(gbig, rep):
    n = gbig.shape[1]
    nrep = rep.shape[0]

    def body(gbig_ref, rep_ref, out_ref, orep_ref, land, land_rep, send_sems, recv_sems, local_sems):
        x, y, c, me = _my_place()
        own = [pltpu.make_async_copy(gbig_ref.at[me], land.at[me], local_sems.at[0]),
               pltpu.make_async_copy(rep_ref, land_rep.at[me], local_sems.at[1])]
        for cp in own:
            cp.start()
        copies = []
        for k in range(1, N_DEV):
            peer, pid = _peer(x, y, c, k)
            copies.append(pltpu.make_async_remote_copy(
                src_ref=gbig_ref.at[pid], dst_ref=land.at[me], send_sem=send_sems.at[0, k],
                recv_sem=recv_sems.at[0, k], device_id=peer, device_id_type=MESH))
            copies.append(pltpu.make_async_remote_copy(
                src_ref=rep_ref, dst_ref=land_rep.at[me], send_sem=send_sems.at[1, k],
                recv_sem=recv_sems.at[1, k], device_id=peer, device_id_type=MESH))
        for cp in copies:
            cp.start()
        for cp in copies:
            cp.wait()
        for cp in own:
            cp.wait()

        def chunk(ci, carry):
            rs = pl.ds(pl.multiple_of(ci * GRAD_CHUNK, GRAD_CHUNK), GRAD_CHUNK)
            acc = land[0, rs, :].astype(F32)
            for d in range(1, N_DEV):
                acc = acc + land[d, rs, :].astype(F32)
            out_ref[rs, :] = acc
            return carry

        lax.fori_loop(0, n // GRAD_CHUNK, chunk, 0)
        acc = land_rep[0]
        for d in range(1, N_DEV):
            acc = acc + land_rep[d]
        orep_ref[...] = acc

    return pl.pallas_call(
        body, name="grad_exchange",
        in_specs=[pl.BlockSpec(memory_space=pl.ANY), pl.BlockSpec(memory_space=pl.ANY)],
        out_specs=[pl.BlockSpec(memory_space=pltpu.VMEM), pl.BlockSpec(memory_space=pltpu.VMEM)],
        out_shape=[jax.ShapeDtypeStruct((n, LANES), F32), jax.ShapeDtypeStruct((nrep, LANES), F32)],
        scratch_shapes=[pltpu.VMEM((N_DEV, n, LANES), BF16), pltpu.VMEM((N_DEV, nrep, LANES), F32),
                        pltpu.SemaphoreType.DMA((2, N_DEV)), pltpu.SemaphoreType.DMA((2, N_DEV)),
                        pltpu.SemaphoreType.DMA((2,))],
        compiler_params=pltpu.CompilerParams(vmem_limit_bytes=VMEM_LIMIT, has_side_effects=True),
    )(gbig, rep)


def _adamw_all(ws, gs, ms, vs):
    n = len(ws)

    def body(*refs):
        w_refs, g_refs, m_refs, v_refs = refs[0:n], refs[n:2 * n], refs[2 * n:3 * n], refs[3 * n:4 * n]
        d_refs, nm_refs, nv_refs = refs[4 * n:5 * n], refs[5 * n:6 * n], refs[6 * n:7 * n]
        for w_ref, g_ref, m_ref, v_ref, d_ref, nm_ref, nv_ref in zip(w_refs, g_refs, m_refs, v_refs, d_refs, nm_refs, nv_refs):
            g = g_ref[...]
            m = ADAM_B1 * m_ref[...] + (1.0 - ADAM_B1) * g
            v = ADAM_B2 * v_ref[...] + (1.0 - ADAM_B2) * jnp.square(g)
            m_hat = m / (1.0 - ADAM_B1 ** ADAM_STEP)
            v_hat = v / (1.0 - ADAM_B2 ** ADAM_STEP)
            d_ref[...] = -ADAM_LR * (m_hat / (jnp.sqrt(v_hat) + ADAM_EPS) + ADAM_WD * w_ref[...])
            nm_ref[...] = m
            nv_ref[...] = v

    shapes = [jax.ShapeDtypeStruct(w.shape, F32) for w in ws]
    outs = pl.pallas_call(
        body, name="adamw", out_shape=shapes * 3,
        compiler_params=pltpu.CompilerParams(vmem_limit_bytes=VMEM_LIMIT),
    )(*ws, *gs, *ms, *vs)
    return outs[0:n], outs[n:2 * n], outs[2 * n:3 * n]


BIG_A0 = (("a_w_in", 1728),)
BIG_A1 = (("a_w_uq", 576), ("a_w_ukv", 512))
BIG_B = (("a_w_out", 1024), ("b_w_in", 2048), ("b_w_rg", 256), ("b_w_ig", 256), ("b_w_out", 1024))
SMALL_A = (("meta_tokens", 16),)
SMALL_B = (("b_norm_g", 1), ("b_conv_w", 4), ("b_conv_b", 1), ("b_b_rg", 1), ("b_b_ig", 1), ("b_lam", 1))
REP = (("a_norm_g", 8), ("a_q_norm_g", 3), ("a_kv_norm_g", 2), ("final_norm_g", 8), ("loss", 1))
SLOT = 16


def _offsets(table, slot=1, start=0):
    out, o = {}, start
    for name, n in table:
        out[name] = (o, n)
        o += -(-n // slot) * slot
    return out, o


def _slotted(a, axis):
    pad = -a.shape[axis] % SLOT
    if not pad:
        return a
    widths = [(0, 0)] * a.ndim
    widths[axis] = (0, pad)
    return jnp.pad(a, widths)


def _rope_tables(rows):
    pos = np.arange(rows, dtype=np.float32)
    inv_freq = (np.float32(ROPE_BASE) ** (-np.arange(0, QK_ROPE, 2, dtype=np.float32) / np.float32(QK_ROPE))).astype(
        np.float32)
    ang = pos[:, None] * inv_freq[None, :]
    cos, sin = np.cos(ang).astype(np.float32), np.sin(ang).astype(np.float32)
    zeros = np.zeros((rows, LANES - QK_ROPE), np.float32)
    return jnp.asarray(np.concatenate([cos, cos, zeros], axis=1)), jnp.asarray(np.concatenate([-sin, sin, zeros], axis=1))


def kernel(x, meta_tokens, a_norm_g, a_w_in, a_q_norm_g, a_kv_norm_g, a_w_uq, a_w_ukv, a_w_out, b_norm_g, b_w_in, b_conv_w, b_conv_b, b_w_rg, b_b_rg, b_w_ig, b_b_ig, b_lam, b_w_out, final_norm_g, loss_target, m_meta_tokens, m_a_norm_g, m_a_w_in, m_a_q_norm_g, m_a_kv_norm_g, m_a_w_uq, m_a_w_ukv, m_a_w_out, m_b_norm_g, m_b_w_in, m_b_conv_w, m_b_conv_b, m_b_w_rg, m_b_b_rg, m_b_w_ig, m_b_b_ig, m_b_lam, m_b_w_out, m_final_norm_g, v_meta_tokens, v_a_norm_g, v_a_w_in, v_a_q_norm_g, v_a_kv_norm_g, v_a_w_uq, v_a_w_ukv, v_a_w_out, v_b_norm_g, v_b_w_in, v_b_conv_w, v_b_conv_b, v_b_w_rg, v_b_b_rg, v_b_w_ig, v_b_b_ig, v_b_lam, v_b_w_out, v_final_norm_g):
    names = ("meta_tokens", "a_norm_g", "a_w_in", "a_q_norm_g", "a_kv_norm_g", "a_w_uq", "a_w_ukv", "a_w_out",
             "b_norm_g", "b_w_in", "b_conv_w", "b_conv_b", "b_w_rg", "b_b_rg", "b_w_ig", "b_b_ig", "b_lam", "b_w_out",
             "final_norm_g")
    w = dict(zip(names, (meta_tokens, a_norm_g, a_w_in, a_q_norm_g, a_kv_norm_g, a_w_uq, a_w_ukv, a_w_out, b_norm_g,
                         b_w_in, b_conv_w, b_conv_b, b_w_rg, b_b_rg, b_w_ig, b_b_ig, b_lam, b_w_out, final_norm_g)))
    mom_m = dict(zip(names, (m_meta_tokens, m_a_norm_g, m_a_w_in, m_a_q_norm_g, m_a_kv_norm_g, m_a_w_uq, m_a_w_ukv,
                             m_a_w_out, m_b_norm_g, m_b_w_in, m_b_conv_w, m_b_conv_b, m_b_w_rg, m_b_b_rg, m_b_w_ig,
                             m_b_b_ig, m_b_lam, m_b_w_out, m_final_norm_g)))
    mom_v = dict(zip(names, (v_meta_tokens, v_a_norm_g, v_a_w_in, v_a_q_norm_g, v_a_kv_norm_g, v_a_w_uq, v_a_w_ukv,
                             v_a_w_out, v_b_norm_g, v_b_w_in, v_b_conv_w, v_b_conv_b, v_b_w_rg, v_b_b_rg, v_b_w_ig,
                             v_b_b_ig, v_b_lam, v_b_w_out, v_final_norm_g)))

    seq = x.shape[1]
    n_real = N_META + seq
    rows = -(-n_real // LANES) * LANES
    scale = (QK_NOPE + QK_ROPE) ** -0.5
    biga0_off, biga0_rows = _offsets(BIG_A0)
    biga1_off, biga1_rows = _offsets(BIG_A1)
    bigb_off, bigb_rows = _offsets(BIG_B)
    small_off, _ = _offsets(SMALL_A + SMALL_B, SLOT)
    gsmalla_off, grada_rows = _offsets(SMALL_A, SLOT, biga0_rows)
    gsmallb_off, gradb_rows = _offsets(SMALL_B, SLOT, bigb_rows)
    grada_rows = -(-grada_rows // GRAD_CHUNK) * GRAD_CHUNK
    rep_off, _ = _offsets(REP, SLOT)

    pack = lambda table: jnp.concatenate([w[nm].reshape(-1, LANES) for nm, _ in table], axis=0).astype(BF16)
    send_a0, send_a1, send_b = pack(BIG_A0), pack(BIG_A1), pack(BIG_B)
    send_small = jnp.concatenate([_slotted(w[nm].reshape(-1, LANES), 0) for nm, _ in SMALL_A + SMALL_B], axis=0)
    all_a0, all_small = _all_gather(send_a0, send_small)

    def small_seg(nm):
        o, n = small_off[nm]
        return all_small[:, o:o + n, :]

    def cols(seg, r, cdev):
        return seg.reshape(N_DEV, r, cdev).transpose(1, 0, 2).reshape(r, N_DEV * cdev)

    def seg(gathered, off, nm):
        o, n = off[nm]
        return gathered[:, o:o + n, :]

    w_in_a = cols(seg(all_a0, biga0_off, "a_w_in"), D_MODEL, 216)
    w_in_a = jnp.concatenate([w_in_a[:, :LAT + QK_ROPE], jnp.zeros((D_MODEL, LAT_PAD - LAT - QK_ROPE), BF16),
                              w_in_a[:, LAT + QK_ROPE:]], axis=1)
    meta_full = small_seg("meta_tokens").transpose(1, 0, 2).reshape(N_META, D_MODEL)
    vec = lambda nm: small_seg(nm).reshape(1, D_MODEL)
    g_b, conv_b, b_rg, b_ig, lam = vec("b_norm_g"), vec("b_conv_b"), vec("b_b_rg"), vec("b_b_ig"), vec("b_lam")
    conv_w = small_seg("b_conv_w").transpose(1, 0, 2).reshape(CONV_WIDTH, LRU_WIDTH)
    g_a, g_q, g_kv = a_norm_g, a_q_norm_g, a_kv_norm_g
    g_f = final_norm_g.reshape(1, D_MODEL)

    h0 = jnp.concatenate([meta_full, x[0], jnp.zeros((rows - n_real, D_MODEL), F32)], axis=0)
    target = jnp.concatenate([jnp.zeros((N_META, D_MODEL), F32), loss_target[0],
                              jnp.zeros((rows - n_real, D_MODEL), F32)], axis=0)
    cos, sin = _rope_tables(rows)

    lat, gate_a, all_a1 = _norm_proj_fwd(h0, g_a, w_in_a, LAT_PAD, "a_in_fwd", send_a1)
    w_uq = jnp.pad(seg(all_a1, biga1_off, "a_w_uq").reshape(N_DEV, Q_LORA, QK_NOPE + QK_ROPE).transpose(1, 0, 2),
                   ((0, 0), (0, 0), (0, HEAD_PAD - QK_NOPE - QK_ROPE))).reshape(Q_LORA, HEADS * HEAD_PAD)
    w_ukv = cols(seg(all_a1, biga1_off, "a_w_ukv"), KV_LORA, QK_NOPE + V_HEAD)
    qc, kc, v, vt = _mla_qkv_fwd(lat, g_q, g_kv, w_uq, w_ukv, cos, sin, scale)
    o, lse, all_b = _attn_fwd(qc, kc, vt, send_b)

    lru_w = lambda nm: seg(all_b, bigb_off, nm).reshape(N_DEV, LRU_BLOCKS, LRU_BLOCK // N_DEV, LRU_BLOCK).transpose(
        1, 0, 2, 3).reshape(LRU_BLOCKS, LRU_BLOCK, LRU_BLOCK)
    w_out_a = seg(all_b, bigb_off, "a_w_out").reshape(D_MODEL, D_MODEL)
    w_in_b = cols(seg(all_b, bigb_off, "b_w_in"), D_MODEL, 2 * LRU_WIDTH // N_DEV)
    w_rg, w_ig = lru_w("b_w_rg"), lru_w("b_w_ig")
    w_out_b = seg(all_b, bigb_off, "b_w_out").reshape(D_MODEL, D_MODEL)

    h1 = _gated_out_fwd(o, gate_a, h0, w_out_a, "a_out_fwd")
    u, gate_b = _norm_proj_fwd(h1, g_b, w_in_b, LRU_WIDTH, "b_in_fwd")
    hs = _rglru_fwd(u, conv_w, conv_b, w_rg, b_rg, w_ig, b_ig, lam)
    h2 = _gated_out_fwd(hs, gate_b, h1, w_out_b, "b_out_fwd")
    dh2, loss_part, dg_f = _final_loss(h2, g_f, target, n_real)

    dhs, dgate_b, dw_out_b = _gated_out_bwd(hs, gate_b, dh2, w_out_b, F32, False, "b_out_bwd")
    du, dconv_w, dconv_b, dw_rg, db_rg, dw_ig, db_ig, dlam = _rglru_bwd(u, hs, dhs, conv_w, conv_b, w_rg, b_rg, w_ig,
                                                                       b_ig, lam)
    dh1, dw_in_b, dg_b = _norm_proj_bwd(h1, g_b, w_in_b, du, dgate_b, dh2, "b_in_bwd")
    do, dgate_a, dw_out_a, delta = _gated_out_bwd(o, gate_a, dh1, w_out_a, BF16, True, "a_out_bwd")

    def to_cols(g, cdev):
        r = g.shape[0]
        return g.reshape(r, N_DEV, cdev).transpose(1, 0, 2).reshape(N_DEV, -1, LANES)

    def packed(parts, big, small, total):
        pieces = [parts[nm].astype(BF16) for nm, _ in big] + [_slotted(parts[nm].astype(BF16), 1) for nm, _ in small]
        used = sum(p.shape[1] for p in pieces)
        if total > used:
            pieces.append(jnp.zeros((N_DEV, total - used, LANES), BF16))
        return jnp.concatenate(pieces, axis=1)

    lru_g = lambda g: g.reshape(LRU_BLOCKS, N_DEV, LRU_BLOCK // N_DEV, LRU_BLOCK).transpose(1, 0, 2, 3).reshape(
        N_DEV, -1, LANES)
    gsend_b = packed({
        "a_w_out": dw_out_a.reshape(N_DEV, -1, LANES),
        "b_w_in": to_cols(dw_in_b, 2 * LRU_WIDTH // N_DEV),
        "b_w_rg": lru_g(dw_rg), "b_w_ig": lru_g(dw_ig),
        "b_w_out": dw_out_b.reshape(N_DEV, -1, LANES),
        "b_norm_g": to_cols(dg_b, LANES), "b_conv_w": to_cols(dconv_w, LANES), "b_conv_b": to_cols(dconv_b, LANES),
        "b_b_rg": to_cols(db_rg, LANES), "b_b_ig": to_cols(db_ig, LANES), "b_lam": to_cols(dlam, LANES),
    }, BIG_B, SMALL_B, gradb_rows)

    dqc, dkc, dv, land_b = _attn_bwd(qc, kc, v, lse, delta, do, gsend_b)
    gsum_b = _sum_blocks(land_b)
    dlat, dw_uq, dw_ukv, dg_q, dg_kv = _mla_qkv_bwd(lat, g_q, g_kv, w_uq, w_ukv, cos, sin, dqc, dkc, dv, scale)
    gsend_a1 = packed({
        "a_w_uq": dw_uq.reshape(Q_LORA, HEADS, HEAD_PAD)[:, :, :QK_NOPE + QK_ROPE].transpose(1, 0, 2).reshape(
            N_DEV, -1, LANES),
        "a_w_ukv": to_cols(dw_ukv, QK_NOPE + V_HEAD),
    }, BIG_A1, (), biga1_rows)
    dh0, dw_in_a, dg_a, land_a1 = _norm_proj_bwd(h0, g_a, w_in_a, dlat, dgate_a, dh1, "a_in_bwd", gsend_a1)
    gsum_a1 = _sum_blocks(land_a1)

    grad_x = dh0[N_META:n_real][None]

    dw_in_a_nat = jnp.concatenate([dw_in_a[:, :LAT + QK_ROPE], dw_in_a[:, LAT_PAD:]], axis=1)
    gsend_a0 = packed({
        "a_w_in": to_cols(dw_in_a_nat, 216),
        "meta_tokens": to_cols(dh0[:N_META], LANES),
    }, BIG_A0, SMALL_A, grada_rows)
    rep_parts = {"a_norm_g": dg_a, "a_q_norm_g": dg_q, "a_kv_norm_g": dg_kv, "final_norm_g": dg_f,
                 "loss": jnp.broadcast_to(loss_part, (1, LANES))}
    rep = jnp.concatenate([_slotted(rep_parts[nm].reshape(-1, LANES), 0) for nm, _ in REP], axis=0)
    gsum_a0, rep_sum = _grad_exchange(gsend_a0, rep)

    grads = {}
    for off, src in ((biga0_off, gsum_a0), (gsmalla_off, gsum_a0), (biga1_off, gsum_a1), (bigb_off, gsum_b),
                     (gsmallb_off, gsum_b), (rep_off, rep_sum)):
        for nm, (o_r, n) in off.items():
            if nm in w:
                grads[nm] = src[o_r:o_r + n].reshape(w[nm].shape)
    loss = rep_sum[rep_off["loss"][0], 0]

    as2d = lambda a: a.reshape(1, -1) if a.ndim == 1 else a
    deltas, new_ms, new_vs = _adamw_all([as2d(w[nm]) for nm in names], [as2d(grads[nm]) for nm in names],
                                        [as2d(mom_m[nm]) for nm in names], [as2d(mom_v[nm]) for nm in names])
    shaped = lambda arrs: [a.reshape(w[nm].shape) for a, nm in zip(arrs, names)]
    return (loss, grad_x, *[grads[nm] for nm in names], *shaped(deltas), *shaped(new_ms), *shaped(new_vs))
```

```python
import functools

import numpy as np
import jax
import jax.numpy as jnp
from jax import lax
from jax.experimental import pallas as pl
from jax.experimental.pallas import tpu as pltpu

F32 = jnp.float32
BF16 = jnp.bfloat16

D_MODEL = 1024
N_META = 16
RMS_EPS = 1e-6
HEADS = 8
QK_NOPE = 128
QK_ROPE = 64
V_HEAD = 128
Q_LORA = 384
KV_LORA = 256
HEAD_PAD = 256
LAT = Q_LORA + KV_LORA
LAT_PAD = LAT + 128
ROPE_BASE = 10000.0
MASK_VALUE = -1e30
LRU_WIDTH = 1024
LRU_BLOCKS = 4
LRU_BLOCK = 256
CONV_WIDTH = 4
LRU_C = 8.0
N_DEV = 8
ADAM_LR, ADAM_B1, ADAM_B2, ADAM_EPS, ADAM_WD, ADAM_STEP = 0.001, 0.9, 0.999, 1e-08, 0.01, 10

LANES = 128
SUBLANES = 8
VMEM_LIMIT = 56 * 1024 * 1024
MESH = pl.DeviceIdType.MESH

NT = (((1,), (1,)), ((), ()))
TN = (((0,), (0,)), ((), ()))


def _row_block(rows):
    return 384 if rows % 384 == 0 else 128


def _cparams(sem):
    return pltpu.CompilerParams(dimension_semantics=sem, vmem_limit_bytes=VMEM_LIMIT)


def _silu(x):
    return x * jax.nn.sigmoid(x)


def _dsilu(x):
    s = jax.nn.sigmoid(x)
    return s * (1.0 + x * (1.0 - s))


def _rms_fwd(x):
    r = lax.rsqrt(jnp.mean(x * x, axis=-1, keepdims=True) + RMS_EPS)
    return x * r, r


def _rms_bwd(dy, xn, r, g):
    t = dy * g
    dx = r * (t - xn * jnp.mean(t * xn, axis=-1, keepdims=True))
    return dx, jnp.sum(dy * xn, axis=0, keepdims=True)


def _expm1_neg(x):
    small = x * (1.0 + x * (1 / 2 + x * (1 / 6 + x * (1 / 24))))
    return jnp.where(x > -0.05, small, jnp.exp(x) - 1.0)


def _softplus_neg(lam):
    z = jnp.exp(-jnp.abs(lam))
    w = z / (2.0 + z)
    w2 = w * w
    series = 2.0 * w * (1.0 + w2 * (1 / 3) + w2 * w2 * (1 / 5))
    return jnp.maximum(-lam, 0.0) + jnp.where(z < 0.1, series, jnp.log(1.0 + z))


def _rider(send, refs, first, last, all_to_all):
    if send is None:
        return (lambda: None), (lambda: None)
    send_ref, result_ref, send_sems, recv_sems, local_sem = refs
    src_of = (lambda d: send_ref.at[d]) if all_to_all else (lambda d: send_ref)
    copies = lambda: _exchange_copies(src_of, result_ref, send_sems, recv_sems, local_sem.at[0])

    def start():
        @pl.when(first)
        def _():
            for cp in copies():
                cp.start()

    def wait():
        @pl.when(last)
        def _():
            for cp in copies():
                cp.wait()

    return start, wait


def _rider_specs(send, all_to_all):
    if send is None:
        return [], [], [], []
    shape = send.shape if all_to_all else (N_DEV,) + send.shape
    hbm = pl.BlockSpec(memory_space=pl.ANY)
    return [hbm], [hbm], [jax.ShapeDtypeStruct(shape, send.dtype)], _exchange_sems()


def _norm_proj_fwd(h, g, w, n1, name, wsend=None):
    rows, n = h.shape[0], w.shape[1]
    tr = _row_block(rows)
    nsteps = rows // tr
    extra = 0 if wsend is None else 1

    def body(h_ref, g_ref, w_ref, *rest):
        p1_ref, p2_ref = rest[extra], rest[extra + 1]
        i = pl.program_id(0)
        start, wait = _rider(wsend, rest[:extra] + rest[extra + 2:], i == 0, i == nsteps - 1, False)
        start()
        xn, _ = _rms_fwd(h_ref[...])
        hn = (xn * g_ref[...]).astype(BF16)
        p = jnp.dot(hn, w_ref[...], preferred_element_type=F32)
        p1_ref[...] = p[:, :n1]
        p2_ref[...] = p[:, n1:]
        wait()

    r_in, r_out, r_shape, r_scratch = _rider_specs(wsend, False)
    return pl.pallas_call(
        body, name=name, grid=(nsteps,),
        in_specs=[pl.BlockSpec((tr, D_MODEL), lambda i: (i, 0)),
                  pl.BlockSpec((1, D_MODEL), lambda i: (0, 0)),
                  pl.BlockSpec((D_MODEL, n), lambda i: (0, 0))] + r_in,
        out_specs=[pl.BlockSpec((tr, n1), lambda i: (i, 0)),
                   pl.BlockSpec((tr, n - n1), lambda i: (i, 0))] + r_out,
        out_shape=[jax.ShapeDtypeStruct((rows, n1), F32), jax.ShapeDtypeStruct((rows, n - n1), F32)] + r_shape,
        scratch_shapes=r_scratch,
        compiler_params=_cparams(("arbitrary",)),
    )(h, g, w, *([] if wsend is None else [wsend]))


def _norm_proj_bwd(h, g, w, dp1, dp2, dh_in, name, gsend=None):
    rows, n = h.shape[0], w.shape[1]
    n1 = dp1.shape[1]
    tr = _row_block(rows)
    nsteps = rows // tr
    extra = 0 if gsend is None else 1

    def body(h_ref, g_ref, w_ref, dp1_ref, dp2_ref, dhin_ref, *rest):
        dh_ref, dw_ref, dg_ref = rest[extra:extra + 3]
        i = pl.program_id(0)
        start, wait = _rider(gsend, rest[:extra] + rest[extra + 3:], i == 0, i == nsteps - 1, True)
        start()

        @pl.when(i == 0)
        def _():
            dw_ref[...] = jnp.zeros_like(dw_ref)
            dg_ref[...] = jnp.zeros_like(dg_ref)

        gv = g_ref[...]
        xn, r = _rms_fwd(h_ref[...])
        hn = (xn * gv).astype(BF16)
        dp = jnp.concatenate([dp1_ref[...].astype(BF16), dp2_ref[...].astype(BF16)], axis=1)
        dw_ref[...] += lax.dot_general(hn, dp, TN, preferred_element_type=F32)
        dhn = lax.dot_general(dp, w_ref[...], NT, preferred_element_type=F32)
        dx, dg = _rms_bwd(dhn, xn, r, gv)
        dg_ref[...] += dg
        dh_ref[...] = dhin_ref[...] + dx
        wait()

    r_in, r_out, r_shape, r_scratch = _rider_specs(gsend, True)
    return pl.pallas_call(
        body, name=name, grid=(nsteps,),
        in_specs=[pl.BlockSpec((tr, D_MODEL), lambda i: (i, 0)),
                  pl.BlockSpec((1, D_MODEL), lambda i: (0, 0)),
                  pl.BlockSpec((D_MODEL, n), lambda i: (0, 0)),
                  pl.BlockSpec((tr, n1), lambda i: (i, 0)),
                  pl.BlockSpec((tr, n - n1), lambda i: (i, 0)),
                  pl.BlockSpec((tr, D_MODEL), lambda i: (i, 0))] + r_in,
        out_specs=[pl.BlockSpec((tr, D_MODEL), lambda i: (i, 0)),
                   pl.BlockSpec((D_MODEL, n), lambda i: (0, 0)),
                   pl.BlockSpec((1, D_MODEL), lambda i: (0, 0))] + r_out,
        out_shape=[jax.ShapeDtypeStruct((rows, D_MODEL), F32),
                   jax.ShapeDtypeStruct((D_MODEL, n), F32),
                   jax.ShapeDtypeStruct((1, D_MODEL), F32)] + r_shape,
        scratch_shapes=r_scratch,
        compiler_params=_cparams(("arbitrary",)),
    )(h, g, w, dp1, dp2, dh_in, *([] if gsend is None else [gsend]))


def _gated_out_fwd(a, gate, h, w, name):
    rows = a.shape[0]
    tr = _row_block(rows)

    def body(a_ref, gate_ref, h_ref, w_ref, o_ref):
        y = (a_ref[...] * _silu(gate_ref[...])).astype(BF16)
        o_ref[...] = h_ref[...] + jnp.dot(y, w_ref[...], preferred_element_type=F32)

    blk = pl.BlockSpec((tr, D_MODEL), lambda i: (i, 0))
    return pl.pallas_call(
        body, name=name, grid=(rows // tr,),
        in_specs=[blk, blk, blk, pl.BlockSpec((D_MODEL, D_MODEL), lambda i: (0, 0))],
        out_specs=blk,
        out_shape=jax.ShapeDtypeStruct((rows, D_MODEL), F32),
        compiler_params=_cparams(("parallel",)),
    )(a, gate, h, w)


def _gated_out_bwd(a, gate, dh, w, da_dtype, with_delta, name):
    rows = a.shape[0]
    tr = _row_block(rows)

    def body(a_ref, gate_ref, dh_ref, w_ref, da_ref, dgate_ref, dw_ref, *delta_ref):
        @pl.when(pl.program_id(0) == 0)
        def _():
            dw_ref[...] = jnp.zeros_like(dw_ref)

        av, gv = a_ref[...], gate_ref[...]
        sg = _silu(gv)
        dhb = dh_ref[...].astype(BF16)
        dw_ref[...] += lax.dot_general((av * sg).astype(BF16), dhb, TN, preferred_element_type=F32)
        dy = lax.dot_general(dhb, w_ref[...], NT, preferred_element_type=F32)
        da = (dy * sg).astype(da_dtype)
        da_ref[...] = da
        dgate_ref[...] = dy * av * _dsilu(gv)
        if with_delta:
            prod = da.astype(F32) * av
            lane = lax.broadcasted_iota(jnp.int32, (tr, LANES), 1)
            per_head = jnp.zeros((tr, LANES), F32)
            for hd in range(HEADS):
                dsum = jnp.sum(prod[:, hd * V_HEAD:(hd + 1) * V_HEAD], axis=1, keepdims=True)
                per_head = jnp.where(lane == hd, dsum, per_head)
            delta_t = per_head.T
            for hd in range(HEADS):
                delta_ref[0][hd, 0] = delta_t[hd:hd + 1, :]

    blk = pl.BlockSpec((tr, D_MODEL), lambda i: (i, 0))
    wblk = pl.BlockSpec((D_MODEL, D_MODEL), lambda i: (0, 0))
    out_specs = [blk, blk, wblk]
    out_shape = [jax.ShapeDtypeStruct((rows, D_MODEL), da_dtype),
                 jax.ShapeDtypeStruct((rows, D_MODEL), F32),
                 jax.ShapeDtypeStruct((D_MODEL, D_MODEL), F32)]
    if with_delta:
        out_specs.append(pl.BlockSpec((HEADS, 1, 1, tr), lambda i: (0, i, 0, 0)))
        out_shape.append(jax.ShapeDtypeStruct((HEADS, rows // tr, 1, tr), F32))
    return pl.pallas_call(
        body, name=name, grid=(rows // tr,),
        in_specs=[blk, blk, blk, wblk],
        out_specs=out_specs, out_shape=out_shape,
        compiler_params=_cparams(("arbitrary",)),
    )(a, gate, dh, w)


def _rope(v, cos, sin, lane):
    swapped = jnp.where(lane < QK_ROPE // 2, pltpu.roll(v, LANES - QK_ROPE // 2, 1), pltpu.roll(v, QK_ROPE // 2, 1))
    return v * cos + swapped * sin


def _unrope(dv, cos, sin, lane):
    t = dv * sin
    swapped = jnp.where(lane < QK_ROPE // 2, pltpu.roll(t, LANES - QK_ROPE // 2, 1), pltpu.roll(t, QK_ROPE // 2, 1))
    return dv * cos + swapped


def _mla_qkv_fwd(lat, gq, gkv, wuq, wukv, cos, sin, scale):
    rows = lat.shape[0]
    tr = _row_block(rows)

    def body(lat_ref, gq_ref, gkv_ref, wuq_ref, wukv_ref, cos_ref, sin_ref, qc_ref, kc_ref, v_ref, vt_ref):
        qn, _ = _rms_fwd(lat_ref[:, :Q_LORA])
        kvn, _ = _rms_fwd(lat_ref[:, Q_LORA:LAT])
        q = jnp.dot((qn * gq_ref[...]).astype(BF16), wuq_ref[...], preferred_element_type=F32)
        kv = jnp.dot((kvn * gkv_ref[...]).astype(BF16), wukv_ref[...], preferred_element_type=F32)
        c, s = cos_ref[...], sin_ref[...]
        lane = lax.broadcasted_iota(jnp.int32, (tr, LANES), 1)
        kr = _rope(lat_ref[:, LAT:LAT_PAD], c, s, lane).astype(BF16)
        for hd in range(HEADS):
            o = hd * HEAD_PAD
            qc_ref[:, o:o + QK_NOPE] = (q[:, o:o + QK_NOPE] * scale).astype(BF16)
            qc_ref[:, o + QK_NOPE:o + HEAD_PAD] = (_rope(q[:, o + QK_NOPE:o + HEAD_PAD], c, s, lane) * scale).astype(BF16)
            kc_ref[:, o:o + QK_NOPE] = kv[:, o:o + QK_NOPE].astype(BF16)
            kc_ref[:, o + QK_NOPE:o + HEAD_PAD] = kr
            vh = kv[:, o + QK_NOPE:o + HEAD_PAD]
            v_ref[:, hd * V_HEAD:(hd + 1) * V_HEAD] = vh.astype(BF16)
            vt_ref[hd, 0] = vh.T.astype(BF16)

    full = lambda shape: pl.BlockSpec(shape, lambda i: (0, 0))
    rowb = lambda n: pl.BlockSpec((tr, n), lambda i: (i, 0))
    return pl.pallas_call(
        body, name="mla_qkv_fwd", grid=(rows // tr,),
        in_specs=[rowb(LAT_PAD), full((1, Q_LORA)), full((1, KV_LORA)), full((Q_LORA, HEADS * HEAD_PAD)),
                  full((KV_LORA, HEADS * HEAD_PAD)), rowb(LANES), rowb(LANES)],
        out_specs=[rowb(HEADS * HEAD_PAD), rowb(HEADS * HEAD_PAD), rowb(HEADS * V_HEAD),
                   pl.BlockSpec((HEADS, 1, V_HEAD, tr), lambda i: (0, i, 0, 0))],
        out_shape=[jax.ShapeDtypeStruct((rows, HEADS * HEAD_PAD), BF16),
                   jax.ShapeDtypeStruct((rows, HEADS * HEAD_PAD), BF16),
                   jax.ShapeDtypeStruct((rows, HEADS * V_HEAD), BF16),
                   jax.ShapeDtypeStruct((HEADS, rows // tr, V_HEAD, tr), BF16)],
        compiler_params=_cparams(("parallel",)),
    )(lat, gq, gkv, wuq, wukv, cos, sin)


def _mla_qkv_bwd(lat, gq, gkv, wuq, wukv, cos, sin, dqc, dkc, dv, scale):
    rows = lat.shape[0]
    tr = _row_block(rows)

    def body(lat_ref, gq_ref, gkv_ref, wuq_ref, wukv_ref, cos_ref, sin_ref, dqc_ref, dkc_ref, dv_ref,
             dlat_ref, dwuq_ref, dwukv_ref, dgq_ref, dgkv_ref):
        @pl.when(pl.program_id(0) == 0)
        def _():
            dwuq_ref[...] = jnp.zeros_like(dwuq_ref)
            dwukv_ref[...] = jnp.zeros_like(dwukv_ref)
            dgq_ref[...] = jnp.zeros_like(dgq_ref)
            dgkv_ref[...] = jnp.zeros_like(dgkv_ref)

        c, s = cos_ref[...], sin_ref[...]
        lane = lax.broadcasted_iota(jnp.int32, (tr, LANES), 1)
        gqv, gkvv = gq_ref[...], gkv_ref[...]
        qn, rq = _rms_fwd(lat_ref[:, :Q_LORA])
        kvn, rkv = _rms_fwd(lat_ref[:, Q_LORA:LAT])
        dq_parts, dkv_parts = [], []
        dkr = jnp.zeros((tr, LANES), F32)
        for hd in range(HEADS):
            o = hd * HEAD_PAD
            dq_parts.append(dqc_ref[:, o:o + QK_NOPE])
            dq_parts.append(_unrope(dqc_ref[:, o + QK_NOPE:o + HEAD_PAD].astype(F32), c, s, lane).astype(BF16))
            dkv_parts.append(dkc_ref[:, o:o + QK_NOPE])
            dkv_parts.append(dv_ref[:, hd * V_HEAD:(hd + 1) * V_HEAD])
            dkr = dkr + dkc_ref[:, o + QK_NOPE:o + HEAD_PAD].astype(F32)
        dq = jnp.concatenate(dq_parts, axis=1)
        dkv = jnp.concatenate(dkv_parts, axis=1)
        dwuq_ref[...] += scale * lax.dot_general((qn * gqv).astype(BF16), dq, TN, preferred_element_type=F32)
        dwukv_ref[...] += lax.dot_general((kvn * gkvv).astype(BF16), dkv, TN, preferred_element_type=F32)
        dqn = scale * lax.dot_general(dq, wuq_ref[...], NT, preferred_element_type=F32)
        dkvn = lax.dot_general(dkv, wukv_ref[...], NT, preferred_element_type=F32)
        dqlat, dgq = _rms_bwd(dqn, qn, rq, gqv)
        dkvlat, dgkv = _rms_bwd(dkvn, kvn, rkv, gkvv)
        dgq_ref[...] += dgq
        dgkv_ref[...] += dgkv
        dlat_ref[:, :Q_LORA] = dqlat
        dlat_ref[:, Q_LORA:LAT] = dkvlat
        dlat_ref[:, LAT:LAT_PAD] = _unrope(dkr, c, s, lane)

    full = lambda shape: pl.BlockSpec(shape, lambda i: (0, 0))
    rowb = lambda n: pl.BlockSpec((tr, n), lambda i: (i, 0))
    return pl.pallas_call(
        body, name="mla_qkv_bwd", grid=(rows // tr,),
        in_specs=[rowb(LAT_PAD), full((1, Q_LORA)), full((1, KV_LORA)), full((Q_LORA, HEADS * HEAD_PAD)),
                  full((KV_LORA, HEADS * HEAD_PAD)), rowb(LANES), rowb(LANES),
                  rowb(HEADS * HEAD_PAD), rowb(HEADS * HEAD_PAD), rowb(HEADS * V_HEAD)],
        out_specs=[rowb(LAT_PAD), full((Q_LORA, HEADS * HEAD_PAD)), full((KV_LORA, HEADS * HEAD_PAD)),
                   full((1, Q_LORA)), full((1, KV_LORA))],
        out_shape=[jax.ShapeDtypeStruct((rows, LAT_PAD), F32),
                   jax.ShapeDtypeStruct((Q_LORA, HEADS * HEAD_PAD), F32),
                   jax.ShapeDtypeStruct((KV_LORA, HEADS * HEAD_PAD), F32),
                   jax.ShapeDtypeStruct((1, Q_LORA), F32),
                   jax.ShapeDtypeStruct((1, KV_LORA), F32)],
        compiler_params=_cparams(("arbitrary",)),
    )(lat, gq, gkv, wuq, wukv, cos, sin, dqc, dkc, dv)


ATTN_UNROLL = 4


def _causal_mask_t(t):
    key = lax.broadcasted_iota(jnp.int32, (t, t), 0)
    query = lax.broadcasted_iota(jnp.int32, (t, t), 1)
    return key <= query


def _attn_fwd(qc, kc, vt, wsend):
    rows = qc.shape[0]
    t = _row_block(rows)
    nblk = rows // t

    def body(q_ref, k_ref, vt_ref, wsend_ref, o_ref, lse_ref, wall_ref, m_ref, l_ref, acc_ref, st_a, st_b,
             send_sems, recv_sems, local_sem):
        i = pl.program_id(1)
        gather = lambda: _exchange_copies(lambda d: wsend_ref, wall_ref, send_sems, recv_sems, local_sem.at[0])

        @pl.when(jnp.logical_and(pl.program_id(0) == 0, i == 0))
        def _():
            for cp in gather():
                cp.start()

        m_ref[...] = jnp.full_like(m_ref, MASK_VALUE)
        l_ref[...] = jnp.zeros_like(l_ref)
        acc_ref[...] = jnp.zeros_like(acc_ref)
        q = q_ref[...]

        def scores(j, st_ref):
            rs = pl.ds(pl.multiple_of(j * t, t), t)
            st_ref[...] = lax.dot_general(k_ref[rs, :], q, NT, preferred_element_type=F32)

        def consume(j, st_ref, masked):
            st = st_ref[...]
            if masked:
                st = jnp.where(_causal_mask_t(t), st, MASK_VALUE)
            m_prev = m_ref[...]
            m_new = jnp.maximum(m_prev, jnp.max(st, axis=0, keepdims=True))
            alpha = jnp.exp(m_prev - m_new)
            pt = jnp.exp(st - m_new)
            l_ref[...] = alpha * l_ref[...] + jnp.sum(pt, axis=0, keepdims=True)
            acc_ref[...] = alpha * acc_ref[...] + jnp.dot(vt_ref[0, j], pt.astype(BF16), preferred_element_type=F32)
            m_ref[...] = m_new

        bufs = (st_a, st_b)

        def step(j, parity, issue_next, masked):
            if issue_next:
                scores(j + 1, bufs[1 - parity])
            consume(j, bufs[parity], masked)

        scores(0, st_a)

        def trip(it, carry):
            for u in range(ATTN_UNROLL):
                step(it * ATTN_UNROLL + u, u % 2, True, False)
            return carry

        trips = i // ATTN_UNROLL
        lax.fori_loop(0, trips, trip, 0)
        j0 = trips * ATTN_UNROLL
        for rest in range(1, ATTN_UNROLL + 1):
            @pl.when(i + 1 - j0 == rest)
            def _(rest=rest):
                for u in range(rest):
                    step(j0 + u, u % 2, u < rest - 1, u == rest - 1)

        o_ref[...] = (acc_ref[...] / l_ref[...]).T
        lse_ref[0, 0] = m_ref[...] + jnp.log(l_ref[...])

        @pl.when(jnp.logical_and(pl.program_id(0) == HEADS - 1, i == nblk - 1))
        def _():
            for cp in gather():
                cp.wait()

    hbm = pl.BlockSpec(memory_space=pl.ANY)
    return pl.pallas_call(
        body, name="attn_fwd", grid=(HEADS, nblk),
        in_specs=[pl.BlockSpec((t, HEAD_PAD), lambda h, i: (i, h)),
                  pl.BlockSpec((rows, HEAD_PAD), lambda h, i: (0, h)),
                  pl.BlockSpec((1, nblk, V_HEAD, t), lambda h, i: (h, 0, 0, 0)),
                  hbm],
        out_specs=[pl.BlockSpec((t, V_HEAD), lambda h, i: (i, h)),
                   pl.BlockSpec((1, 1, 1, t), lambda h, i: (h, i, 0, 0)),
                   hbm],
        out_shape=[jax.ShapeDtypeStruct((rows, HEADS * V_HEAD), F32),
                   jax.ShapeDtypeStruct((HEADS, nblk, 1, t), F32),
                   jax.ShapeDtypeStruct((N_DEV,) + wsend.shape, wsend.dtype)],
        scratch_shapes=[pltpu.VMEM((1, t), F32), pltpu.VMEM((1, t), F32), pltpu.VMEM((V_HEAD, t), F32),
                        pltpu.VMEM((t, t), F32), pltpu.VMEM((t, t), F32)] + _exchange_sems(),
        compiler_params=_cparams(("arbitrary", "arbitrary")),
    )(qc, kc, vt, wsend)


def _attn_bwd(qc, kc, v, lse, delta, do, gsend):
    rows = qc.shape[0]
    t = _row_block(rows)
    nblk = rows // t

    def body(q_ref, k_ref, v_ref, lse_ref, delta_ref, do_ref, gsend_ref, dq_ref, dk_ref, dv_ref, land_ref,
             dq_acc, dk_acc, dv_acc, st_a, dp_a, st_b, dp_b, send_sems, recv_sems, local_sem):
        j = pl.program_id(1)
        exchange = lambda: _exchange_copies(lambda d: gsend_ref.at[d], land_ref, send_sems, recv_sems, local_sem.at[0])

        @pl.when(jnp.logical_and(pl.program_id(0) == 0, j == 0))
        def _():
            for cp in exchange():
                cp.start()

        @pl.when(j == 0)
        def _():
            dq_acc[...] = jnp.zeros_like(dq_acc)

        dk_acc[...] = jnp.zeros_like(dk_acc)
        dv_acc[...] = jnp.zeros_like(dv_acc)
        k = k_ref[...]
        vv = v_ref[...]

        def products(i, st_ref, dp_ref):
            rs = pl.ds(pl.multiple_of(i * t, t), t)
            st_ref[...] = lax.dot_general(k, q_ref[rs, :], NT, preferred_element_type=F32)
            dp_ref[...] = lax.dot_general(vv, do_ref[rs, :], NT, preferred_element_type=F32)

        def consume(i, st_ref, dp_ref):
            rs = pl.ds(pl.multiple_of(i * t, t), t)
            q = q_ref[rs, :]
            dob = do_ref[rs, :]
            st = jnp.where(jnp.logical_or(_causal_mask_t(t), i != j), st_ref[...], MASK_VALUE)
            pt = jnp.exp(st - lse_ref[0, i])
            dv_acc[...] += jnp.dot(pt.astype(BF16), dob, preferred_element_type=F32)
            dst = (pt * (dp_ref[...] - delta_ref[0, i])).astype(BF16)
            dk_acc[...] += jnp.dot(dst, q, preferred_element_type=F32)
            dq_acc[rs, :] += lax.dot_general(dst, k, TN, preferred_element_type=F32)

        bufs = ((st_a, dp_a), (st_b, dp_b))

        def step(i, parity, issue_next):
            if issue_next:
                products(i + 1, *bufs[1 - parity])
            consume(i, *bufs[parity])

        products(j, st_a, dp_a)

        def trip(it, carry):
            for u in range(ATTN_UNROLL):
                step(j + it * ATTN_UNROLL + u, u % 2, True)
            return carry

        trips = (nblk - 1 - j) // ATTN_UNROLL
        lax.fori_loop(0, trips, trip, 0)
        i0 = j + trips * ATTN_UNROLL
        for rest in range(1, ATTN_UNROLL + 1):
            @pl.when(nblk - i0 == rest)
            def _(rest=rest):
                for u in range(rest):
                    step(i0 + u, u % 2, u < rest - 1)

        dk_ref[...] = dk_acc[...].astype(BF16)
        dv_ref[...] = dv_acc[...].astype(BF16)

        @pl.when(j == nblk - 1)
        def _():
            dq_ref[...] = dq_acc[...].astype(BF16)

        @pl.when(jnp.logical_and(pl.program_id(0) == HEADS - 1, j == nblk - 1))
        def _():
            for cp in exchange():
                cp.wait()

    stat = pl.BlockSpec((1, nblk, 1, t), lambda h, j: (h, 0, 0, 0))
    hbm = pl.BlockSpec(memory_space=pl.ANY)
    return pl.pallas_call(
        body, name="attn_bwd", grid=(HEADS, nblk),
        in_specs=[pl.BlockSpec((rows, HEAD_PAD), lambda h, j: (0, h)),
                  pl.BlockSpec((t, HEAD_PAD), lambda h, j: (j, h)),
                  pl.BlockSpec((t, V_HEAD), lambda h, j: (j, h)),
                  stat, stat,
                  pl.BlockSpec((rows, V_HEAD), lambda h, j: (0, h)),
                  hbm],
        out_specs=[pl.BlockSpec((rows, HEAD_PAD), lambda h, j: (0, h)),
                   pl.BlockSpec((t, HEAD_PAD), lambda h, j: (j, h)),
                   pl.BlockSpec((t, V_HEAD), lambda h, j: (j, h)),
                   hbm],
        out_shape=[jax.ShapeDtypeStruct((rows, HEADS * HEAD_PAD), BF16),
                   jax.ShapeDtypeStruct((rows, HEADS * HEAD_PAD), BF16),
                   jax.ShapeDtypeStruct((rows, HEADS * V_HEAD), BF16),
                   jax.ShapeDtypeStruct(gsend.shape, gsend.dtype)],
        scratch_shapes=[pltpu.VMEM((rows, HEAD_PAD), F32), pltpu.VMEM((t, HEAD_PAD), F32),
                        pltpu.VMEM((t, V_HEAD), F32)] + [pltpu.VMEM((t, t), F32)] * 4 + _exchange_sems(),
        compiler_params=_cparams(("arbitrary", "arbitrary")),
    )(qc, kc, v, lse, delta, do, gsend)


def _shift_down(prev_tile, x, k):
    xx = jnp.concatenate([prev_tile, x], axis=0)
    return pltpu.roll(xx, k, 0)[SUBLANES:]


def _shift_up(x, next_tile, k):
    n = x.shape[0]
    xx = jnp.concatenate([x, next_tile], axis=0)
    return pltpu.roll(xx, n + SUBLANES - k, 0)[:n]


def _lru_gates(u, u_prev, cw_ref, cb_ref, wrg_ref, brg_ref, wig_ref, big_ref, lam_ref, first_block):
    taps = [_shift_down(u_prev, u, CONV_WIDTH - 1 - j) if j < CONV_WIDTH - 1 else u for j in range(CONV_WIDTH)]
    uc = cb_ref[...] + taps[0] * cw_ref[0:1, :]
    for j in range(1, CONV_WIDTH):
        uc = uc + taps[j] * cw_ref[j:j + 1, :]
    ub = uc.astype(BF16)
    zr = jnp.concatenate([jnp.dot(ub[:, g * LRU_BLOCK:(g + 1) * LRU_BLOCK], wrg_ref[g], preferred_element_type=F32)
                          for g in range(LRU_BLOCKS)], axis=1) + brg_ref[...]
    zi = jnp.concatenate([jnp.dot(ub[:, g * LRU_BLOCK:(g + 1) * LRU_BLOCK], wig_ref[g], preferred_element_type=F32)
                          for g in range(LRU_BLOCKS)], axis=1) + big_ref[...]
    r = jax.nn.sigmoid(zr)
    ig = jax.nn.sigmoid(zi)
    sp = _softplus_neg(lam_ref[...])
    log_a = (-LRU_C) * r * sp
    a = jnp.exp(log_a)
    m2 = -_expm1_neg(2.0 * log_a)
    mult_raw = m2 * lax.rsqrt(jnp.maximum(m2, 1e-30))
    row = lax.broadcasted_iota(jnp.int32, u.shape, 0)
    is_start = jnp.logical_and(first_block, row == 0)
    mult = jnp.where(is_start, 1.0, mult_raw)
    return dict(taps=taps, uc=uc, ub=ub, r=r, ig=ig, sp=sp, a=a, mult=mult, mult_raw=mult_raw, is_start=is_start)


def _rglru_fwd(u, cw, cb, wrg, brg, wig, big, lam):
    rows = u.shape[0]
    tb = _row_block(rows)

    def body(u_ref, cw_ref, cb_ref, wrg_ref, brg_ref, wig_ref, big_ref, lam_ref, hs_ref, utail, hcar, a_s, b_s):
        i = pl.program_id(0)

        @pl.when(i == 0)
        def _():
            utail[...] = jnp.zeros_like(utail)
            hcar[...] = jnp.zeros_like(hcar)

        u = u_ref[...]
        gt = _lru_gates(u, utail[...], cw_ref, cb_ref, wrg_ref, brg_ref, wig_ref, big_ref, lam_ref, i == 0)
        a_s[...] = gt["a"]
        b_s[...] = gt["mult"] * (gt["ig"] * gt["uc"])
        row8 = lax.broadcasted_iota(jnp.int32, (SUBLANES, LRU_WIDTH), 0)

        def tile(tix, carry):
            rs = pl.ds(pl.multiple_of(tix * SUBLANES, SUBLANES), SUBLANES)
            av, bv = a_s[rs, :], b_s[rs, :]
            for k in (1, 2, 4):
                keep = row8 >= k
                bv = jnp.where(keep, av * pltpu.roll(bv, k, 0) + bv, bv)
                av = jnp.where(keep, av * pltpu.roll(av, k, 0), av)
            h8 = av * carry + bv
            hs_ref[rs, :] = h8
            return jnp.broadcast_to(h8[SUBLANES - 1:SUBLANES, :], (SUBLANES, LRU_WIDTH))

        hcar[...] = lax.fori_loop(0, tb // SUBLANES, tile, hcar[...])
        utail[...] = u[tb - SUBLANES:, :]

    full2 = lambda shape: pl.BlockSpec(shape, lambda i: (0, 0))
    full3 = lambda shape: pl.BlockSpec(shape, lambda i: (0, 0, 0))
    blk = pl.BlockSpec((tb, LRU_WIDTH), lambda i: (i, 0))
    return pl.pallas_call(
        body, name="rglru_fwd", grid=(rows // tb,),
        in_specs=[blk, full2((CONV_WIDTH, LRU_WIDTH)), full2((1, LRU_WIDTH)),
                  full3((LRU_BLOCKS, LRU_BLOCK, LRU_BLOCK)), full2((1, LRU_WIDTH)),
                  full3((LRU_BLOCKS, LRU_BLOCK, LRU_BLOCK)), full2((1, LRU_WIDTH)), full2((1, LRU_WIDTH))],
        out_specs=blk,
        out_shape=jax.ShapeDtypeStruct((rows, LRU_WIDTH), F32),
        scratch_shapes=[pltpu.VMEM((SUBLANES, LRU_WIDTH), F32), pltpu.VMEM((SUBLANES, LRU_WIDTH), F32),
                        pltpu.VMEM((tb, LRU_WIDTH), F32), pltpu.VMEM((tb, LRU_WIDTH), F32)],
        compiler_params=_cparams(("arbitrary",)),
    )(u, cw, cb, wrg, brg, wig, big, lam)


def _rglru_bwd(u, hs, dhs, cw, cb, wrg, brg, wig, big, lam):
    rows = u.shape[0]
    tb = _row_block(rows)
    nblk = rows // tb
    tiles = tb // SUBLANES

    def body(u_ref, up_ref, hs_ref, hp_ref, dhs_ref, cw_ref, cb_ref, wrg_ref, brg_ref, wig_ref, big_ref, lam_ref,
             du_ref, dcw_ref, dcb_ref, dwrg_ref, dbrg_ref, dwig_ref, dbig_ref, dlam_ref,
             gcar, duc_head, a_s, b_s, g_s, dsp_acc):
        step = pl.program_id(0)
        blk_ix = nblk - 1 - step

        @pl.when(step == 0)
        def _():
            for ref in (dcw_ref, dcb_ref, dwrg_ref, dbrg_ref, dwig_ref, dbig_ref, gcar, duc_head, dsp_acc):
                ref[...] = jnp.zeros_like(ref)

        first = blk_ix == 0
        u = u_ref[...]
        u_prev = jnp.where(first, 0.0, up_ref[...])
        h_prev_tile = jnp.where(first, 0.0, hp_ref[...])
        gt = _lru_gates(u, u_prev, cw_ref, cb_ref, wrg_ref, brg_ref, wig_ref, big_ref, lam_ref, first)
        a, r, ig, uc, mult = gt["a"], gt["r"], gt["ig"], gt["uc"], gt["mult"]
        dhs_v = dhs_ref[...]

        a_s[...] = a
        b_s[...] = a * dhs_v
        row8 = lax.broadcasted_iota(jnp.int32, (SUBLANES, LRU_WIDTH), 0)

        def tile(tix, carry):
            rs = pl.ds(pl.multiple_of((tiles - 1 - tix) * SUBLANES, SUBLANES), SUBLANES)
            av, bv = a_s[rs, :], b_s[rs, :]
            for k in (1, 2, 4):
                keep = row8 < SUBLANES - k
                bv = jnp.where(keep, av * pltpu.roll(bv, SUBLANES - k, 0) + bv, bv)
                av = jnp.where(keep, av * pltpu.roll(av, SUBLANES - k, 0), av)
            g8 = av * carry + bv
            g_s[rs, :] = g8
            return jnp.broadcast_to(g8[0:1, :], (SUBLANES, LRU_WIDTH))

        g_next = gcar[...]
        gcar[...] = lax.fori_loop(0, tiles, tile, g_next)
        g = dhs_v + _shift_up(g_s[...], g_next, 1)

        h_prev = _shift_down(h_prev_tile, hs_ref[...], 1)
        da = g * h_prev
        iu = ig * uc
        dmult = jnp.where(gt["is_start"], 0.0, g * iu)
        d_ig = g * mult * uc
        duc = g * mult * ig
        dlog_a = da * a - dmult * (a * a) / gt["mult_raw"]
        dzr = (dlog_a * ((-LRU_C) * gt["sp"])) * r * (1.0 - r)
        dsp_acc[...] += jnp.sum(dlog_a * ((-LRU_C) * r), axis=0, keepdims=True)
        dzi = d_ig * ig * (1.0 - ig)
        dbrg_ref[...] += jnp.sum(dzr, axis=0, keepdims=True)
        dbig_ref[...] += jnp.sum(dzi, axis=0, keepdims=True)
        dzr_b, dzi_b = dzr.astype(BF16), dzi.astype(BF16)
        ub = gt["ub"]
        duc_parts = []
        for gi in range(LRU_BLOCKS):
            cs = slice(gi * LRU_BLOCK, (gi + 1) * LRU_BLOCK)
            dwrg_ref[gi] += lax.dot_general(ub[:, cs], dzr_b[:, cs], TN, preferred_element_type=F32)
            dwig_ref[gi] += lax.dot_general(ub[:, cs], dzi_b[:, cs], TN, preferred_element_type=F32)
            duc_parts.append(lax.dot_general(dzr_b[:, cs], wrg_ref[gi], NT, preferred_element_type=F32)
                             + lax.dot_general(dzi_b[:, cs], wig_ref[gi], NT, preferred_element_type=F32))
        duc = duc + jnp.concatenate(duc_parts, axis=1)

        dcb_ref[...] += jnp.sum(duc, axis=0, keepdims=True)
        taps = gt["taps"]
        for jt in range(CONV_WIDTH):
            dcw_ref[jt:jt + 1, :] += jnp.sum(duc * taps[jt], axis=0, keepdims=True)
        head = duc_head[...]
        du = duc * cw_ref[CONV_WIDTH - 1:CONV_WIDTH, :]
        for jt in range(CONV_WIDTH - 1):
            du = du + _shift_up(duc, head, CONV_WIDTH - 1 - jt) * cw_ref[jt:jt + 1, :]
        du_ref[...] = du
        duc_head[...] = duc[:SUBLANES, :]

        @pl.when(step == nblk - 1)
        def _():
            dlam_ref[...] = -dsp_acc[...] * jax.nn.sigmoid(-lam_ref[...])

    full2 = lambda shape: pl.BlockSpec(shape, lambda s: (0, 0))
    full3 = lambda shape: pl.BlockSpec(shape, lambda s: (0, 0, 0))
    blk = pl.BlockSpec((tb, LRU_WIDTH), lambda s: (nblk - 1 - s, 0))
    prev_tile = pl.BlockSpec((SUBLANES, LRU_WIDTH), lambda s: (jnp.maximum((nblk - 1 - s) * tiles - 1, 0), 0))
    wshape = (LRU_BLOCKS, LRU_BLOCK, LRU_BLOCK)
    return pl.pallas_call(
        body, name="rglru_bwd", grid=(nblk,),
        in_specs=[blk, prev_tile, blk, prev_tile, blk, full2((CONV_WIDTH, LRU_WIDTH)), full2((1, LRU_WIDTH)),
                  full3(wshape), full2((1, LRU_WIDTH)), full3(wshape), full2((1, LRU_WIDTH)), full2((1, LRU_WIDTH))],
        out_specs=[blk, full2((CONV_WIDTH, LRU_WIDTH)), full2((1, LRU_WIDTH)), full3(wshape), full2((1, LRU_WIDTH)),
                   full3(wshape), full2((1, LRU_WIDTH)), full2((1, LRU_WIDTH))],
        out_shape=[jax.ShapeDtypeStruct((rows, LRU_WIDTH), F32),
                   jax.ShapeDtypeStruct((CONV_WIDTH, LRU_WIDTH), F32), jax.ShapeDtypeStruct((1, LRU_WIDTH), F32),
                   jax.ShapeDtypeStruct(wshape, F32), jax.ShapeDtypeStruct((1, LRU_WIDTH), F32),
                   jax.ShapeDtypeStruct(wshape, F32), jax.ShapeDtypeStruct((1, LRU_WIDTH), F32),
                   jax.ShapeDtypeStruct((1, LRU_WIDTH), F32)],
        scratch_shapes=[pltpu.VMEM((SUBLANES, LRU_WIDTH), F32), pltpu.VMEM((SUBLANES, LRU_WIDTH), F32),
                        pltpu.VMEM((tb, LRU_WIDTH), F32), pltpu.VMEM((tb, LRU_WIDTH), F32),
                        pltpu.VMEM((tb, LRU_WIDTH), F32), pltpu.VMEM((1, LRU_WIDTH), F32)],
        compiler_params=_cparams(("arbitrary",)),
    )(u, u, hs, hs, dhs, cw, cb, wrg, brg, wig, big, lam)


def _final_loss(h, gf, target, n_real):
    rows = h.shape[0]
    tr = _row_block(rows)

    def body(h_ref, g_ref, t_ref, dh_ref, loss_ref, dg_ref):
        i = pl.program_id(0)

        @pl.when(i == 0)
        def _():
            loss_ref[...] = jnp.zeros_like(loss_ref)
            dg_ref[...] = jnp.zeros_like(dg_ref)

        gv = g_ref[...]
        xn, r = _rms_fwd(h_ref[...])
        row = i * tr + lax.broadcasted_iota(jnp.int32, (tr, 1), 0)
        live = jnp.logical_and(row >= N_META, row < n_real)
        tgt = t_ref[...]
        tgt = jnp.where(i == 0, pltpu.roll(tgt, N_META, 0), tgt)
        err = jnp.where(live, xn * gv - tgt, 0.0)
        loss_ref[...] += (0.5 / D_MODEL) * jnp.sum(jnp.sum(err * err, axis=1, keepdims=True), axis=0, keepdims=True)
        dx, dg = _rms_bwd(err * (1.0 / D_MODEL), xn, r, gv)
        dg_ref[...] += dg
        dh_ref[...] = dx

    blk = pl.BlockSpec((tr, D_MODEL), lambda i: (i, 0))
    window = pl.BlockSpec((pl.Element(tr, (0, rows - n_real)), pl.Element(D_MODEL)),
                          lambda i: (pl.multiple_of(jnp.maximum(i * tr - N_META, 0), SUBLANES), 0))
    return pl.pallas_call(
        body, name="final_loss", grid=(rows // tr,),
        in_specs=[blk, pl.BlockSpec((1, D_MODEL), lambda i: (0, 0)), window],
        out_specs=[blk, pl.BlockSpec((1, 1), lambda i: (0, 0)), pl.BlockSpec((1, D_MODEL), lambda i: (0, 0))],
        out_shape=[jax.ShapeDtypeStruct((rows, D_MODEL), F32), jax.ShapeDtypeStruct((1, 1), F32),
                   jax.ShapeDtypeStruct((1, D_MODEL), F32)],
        compiler_params=_cparams(("arbitrary",)),
    )(h, gf, target)


def _my_place():
    x, y, c = lax.axis_index("x"), lax.axis_index("y"), lax.axis_index("c")
    return x, y, c, 4 * x + 2 * y + c


def _peer(x, y, c, k):
    px, py, pc = x ^ (k >> 2), y ^ ((k >> 1) & 1), c ^ (k & 1)
    return (px, py, pc), 4 * px + 2 * py + pc


def _exchange_copies(src_of, dst_ref, send_sems, recv_sems, local_sem):
    x, y, c, me = _my_place()
    copies = [pltpu.make_async_copy(src_of(me), dst_ref.at[me], local_sem)]
    for k in range(1, N_DEV):
        peer, pid = _peer(x, y, c, k)
        copies.append(pltpu.make_async_remote_copy(
            src_ref=src_of(pid), dst_ref=dst_ref.at[me], send_sem=send_sems.at[k], recv_sem=recv_sems.at[k],
            device_id=peer, device_id_type=MESH))
    return copies


def _exchange_sems():
    return [pltpu.SemaphoreType.DMA((N_DEV,)), pltpu.SemaphoreType.DMA((N_DEV,)), pltpu.SemaphoreType.DMA((1,))]


def _sum_blocks(land):
    n = land.shape[1]
    rc = max(c for c in range(16, 1025, 16) if n % c == 0)

    def body(land_ref, out_ref):
        acc = land_ref[0].astype(F32)
        for d in range(1, N_DEV):
            acc = acc + land_ref[d].astype(F32)
        out_ref[...] = acc

    return pl.pallas_call(
        body, name="sum_blocks", grid=(n // rc,),
        in_specs=[pl.BlockSpec((N_DEV, rc, LANES), lambda i: (0, i, 0))],
        out_specs=pl.BlockSpec((rc, LANES), lambda i: (i, 0)),
        out_shape=jax.ShapeDtypeStruct((n, LANES), F32),
        compiler_params=_cparams(("parallel",)),
    )(land)


def _all_gather(big, small):
    def body(big_ref, small_ref, obig_ref, osmall_ref, send_sems, recv_sems, local_sems):
        x, y, c, me = _my_place()
        own = [pltpu.make_async_copy(big_ref, obig_ref.at[me], local_sems.at[0]),
               pltpu.make_async_copy(small_ref, osmall_ref.at[me], local_sems.at[1])]
        for cp in own:
            cp.start()
        copies = []
        for k in range(1, N_DEV):
            peer, _ = _peer(x, y, c, k)
            for part, (src, dst) in enumerate(((big_ref, obig_ref), (small_ref, osmall_ref))):
                copies.append(pltpu.make_async_remote_copy(
                    src_ref=src, dst_ref=dst.at[me], send_sem=send_sems.at[part, k], recv_sem=recv_sems.at[part, k],
                    device_id=peer, device_id_type=MESH))
        for cp in copies:
            cp.start()
        for cp in copies:
            cp.wait()
        for cp in own:
            cp.wait()

    n = big.shape[0]
    hbm = pl.BlockSpec(memory_space=pl.ANY)
    return pl.pallas_call(
        body, name="weight_all_gather",
        in_specs=[hbm, hbm], out_specs=[hbm, hbm],
        out_shape=[jax.ShapeDtypeStruct((N_DEV, n, LANES), BF16), jax.ShapeDtypeStruct((N_DEV,) + small.shape, F32)],
        scratch_shapes=[pltpu.SemaphoreType.DMA((2, N_DEV)), pltpu.SemaphoreType.DMA((2, N_DEV)),
                        pltpu.SemaphoreType.DMA((2,))],
        compiler_params=pltpu.CompilerParams(has_side_effects=True),
    )(big, small)


GRAD_CHUNK = 32


def _grad_exchange(gbig, rep):
    n = gbig.shape[1]
    nrep = rep.shape[0]

    def body(gbig_ref, rep_ref, out_ref, orep_ref, land, land_rep, send_sems, recv_sems, local_sems):
        x, y, c, me = _my_place()
        own = [pltpu.make_async_copy(gbig_ref.at[me], land.at[me], local_sems.at[0]),
               pltpu.make_async_copy(rep_ref, land_rep.at[me], local_sems.at[1])]
        for cp in own:
            cp.start()
        copies = []
        for k in range(1, N_DEV):
            peer, pid = _peer(x, y, c, k)
            copies.append(pltpu.make_async_remote_copy(
                src_ref=gbig_ref.at[pid], dst_ref=land.at[me], send_sem=send_sems.at[0, k],
                recv_sem=recv_sems.at[0, k], device_id=peer, device_id_type=MESH))
            copies.append(pltpu.make_async_remote_copy(
                src_ref=rep_ref, dst_ref=land_rep.at[me], send_sem=send_sems.at[1, k],
                recv_sem=recv_sems.at[1, k], device_id=peer, device_id_type=MESH))
        for cp in copies:
            cp.start()
        for cp in copies:
            cp.wait()
        for cp in own:
            cp.wait()

        def chunk(ci, carry):
            rs = pl.ds(pl.multiple_of(ci * GRAD_CHUNK, GRAD_CHUNK), GRAD_CHUNK)
            acc = land[0, rs, :].astype(F32)
            for d in range(1, N_DEV):
                acc = acc + land[d, rs, :].astype(F32)
            out_ref[rs, :] = acc
            return carry

        lax.fori_loop(0, n // GRAD_CHUNK, chunk, 0)
        acc = land_rep[0]
        for d in range(1, N_DEV):
            acc = acc + land_rep[d]
        orep_ref[...] = acc

    return pl.pallas_call(
        body, name="grad_exchange",
        in_specs=[pl.BlockSpec(memory_space=pl.ANY), pl.BlockSpec(memory_space=pl.ANY)],
        out_specs=[pl.BlockSpec(memory_space=pltpu.VMEM), pl.BlockSpec(memory_space=pltpu.VMEM)],
        out_shape=[jax.ShapeDtypeStruct((n, LANES), F32), jax.ShapeDtypeStruct((nrep, LANES), F32)],
        scratch_shapes=[pltpu.VMEM((N_DEV, n, LANES), BF16), pltpu.VMEM((N_DEV, nrep, LANES), F32),
                        pltpu.SemaphoreType.DMA((2, N_DEV)), pltpu.SemaphoreType.DMA((2, N_DEV)),
                        pltpu.SemaphoreType.DMA((2,))],
        compiler_params=pltpu.CompilerParams(vmem_limit_bytes=VMEM_LIMIT, has_side_effects=True),
    )(gbig, rep)


def _adamw_all(ws, gs, ms, vs):
    n = len(ws)

    def body(*refs):
        w_refs, g_refs, m_refs, v_refs = refs[0:n], refs[n:2 * n], refs[2 * n:3 * n], refs[3 * n:4 * n]
        d_refs, nm_refs, nv_refs = refs[4 * n:5 * n], refs[5 * n:6 * n], refs[6 * n:7 * n]
        for w_ref, g_ref, m_ref, v_ref, d_ref, nm_ref, nv_ref in zip(w_refs, g_refs, m_refs, v_refs, d_refs, nm_refs, nv_refs):
            g = g_ref[...]
            m = ADAM_B1 * m_ref[...] + (1.0 - ADAM_B1) * g
            v = ADAM_B2 * v_ref[...] + (1.0 - ADAM_B2) * jnp.square(g)
            m_hat = m / (1.0 - ADAM_B1 ** ADAM_STEP)
            v_hat = v / (1.0 - ADAM_B2 ** ADAM_STEP)
            d_ref[...] = -ADAM_LR * (m_hat / (jnp.sqrt(v_hat) + ADAM_EPS) + ADAM_WD * w_ref[...])
            nm_ref[...] = m
            nv_ref[...] = v

    shapes = [jax.ShapeDtypeStruct(w.shape, F32) for w in ws]
    outs = pl.pallas_call(
        body, name="adamw", out_shape=shapes * 3,
        compiler_params=pltpu.CompilerParams(vmem_limit_bytes=VMEM_LIMIT),
    )(*ws, *gs, *ms, *vs)
    return outs[0:n], outs[n:2 * n], outs[2 * n:3 * n]


BIG_A0 = (("a_w_in", 1728),)
BIG_A1 = (("a_w_uq", 576), ("a_w_ukv", 512))
BIG_B = (("a_w_out", 1024), ("b_w_in", 2048), ("b_w_rg", 256), ("b_w_ig", 256), ("b_w_out", 1024))
SMALL_A = (("meta_tokens", 16),)
SMALL_B = (("b_norm_g", 1), ("b_conv_w", 4), ("b_conv_b", 1), ("b_b_rg", 1), ("b_b_ig", 1), ("b_lam", 1))
REP = (("a_norm_g", 8), ("a_q_norm_g", 3), ("a_kv_norm_g", 2), ("final_norm_g", 8), ("loss", 1))
SLOT = 16


def _offsets(table, slot=1, start=0):
    out, o = {}, start
    for name, n in table:
        out[name] = (o, n)
        o += -(-n // slot) * slot
    return out, o


def _slotted(a, axis):
    pad = -a.shape[axis] % SLOT
    if not pad:
        return a
    widths = [(0, 0)] * a.ndim
    widths[axis] = (0, pad)
    return jnp.pad(a, widths)


def _rope_tables(rows):
    pos = np.arange(rows, dtype=np.float32)
    inv_freq = (np.float32(ROPE_BASE) ** (-np.arange(0, QK_ROPE, 2, dtype=np.float32) / np.float32(QK_ROPE))).astype(
        np.float32)
    ang = pos[:, None] * inv_freq[None, :]
    cos, sin = np.cos(ang).astype(np.float32), np.sin(ang).astype(np.float32)
    zeros = np.zeros((rows, LANES - QK_ROPE), np.float32)
    return jnp.asarray(np.concatenate([cos, cos, zeros], axis=1)), jnp.asarray(np.concatenate([-sin, sin, zeros], axis=1))


def kernel(x, meta_tokens, a_norm_g, a_w_in, a_q_norm_g, a_kv_norm_g, a_w_uq, a_w_ukv, a_w_out, b_norm_g, b_w_in, b_conv_w, b_conv_b, b_w_rg, b_b_rg, b_w_ig, b_b_ig, b_lam, b_w_out, final_norm_g, loss_target, m_meta_tokens, m_a_norm_g, m_a_w_in, m_a_q_norm_g, m_a_kv_norm_g, m_a_w_uq, m_a_w_ukv, m_a_w_out, m_b_norm_g, m_b_w_in, m_b_conv_w, m_b_conv_b, m_b_w_rg, m_b_b_rg, m_b_w_ig, m_b_b_ig, m_b_lam, m_b_w_out, m_final_norm_g, v_meta_tokens, v_a_norm_g, v_a_w_in, v_a_q_norm_g, v_a_kv_norm_g, v_a_w_uq, v_a_w_ukv, v_a_w_out, v_b_norm_g, v_b_w_in, v_b_conv_w, v_b_conv_b, v_b_w_rg, v_b_b_rg, v_b_w_ig, v_b_b_ig, v_b_lam, v_b_w_out, v_final_norm_g):
    names = ("meta_tokens", "a_norm_g", "a_w_in", "a_q_norm_g", "a_kv_norm_g", "a_w_uq", "a_w_ukv", "a_w_out",
             "b_norm_g", "b_w_in", "b_conv_w", "b_conv_b", "b_w_rg", "b_b_rg", "b_w_ig", "b_b_ig", "b_lam", "b_w_out",
             "final_norm_g")
    w = dict(zip(names, (meta_tokens, a_norm_g, a_w_in, a_q_norm_g, a_kv_norm_g, a_w_uq, a_w_ukv, a_w_out, b_norm_g,
                         b_w_in, b_conv_w, b_conv_b, b_w_rg, b_b_rg, b_w_ig, b_b_ig, b_lam, b_w_out, final_norm_g)))
    mom_m = dict(zip(names, (m_meta_tokens, m_a_norm_g, m_a_w_in, m_a_q_norm_g, m_a_kv_norm_g, m_a_w_uq, m_a_w_ukv,
                             m_a_w_out, m_b_norm_g, m_b_w_in, m_b_conv_w, m_b_conv_b, m_b_w_rg, m_b_b_rg, m_b_w_ig,
                             m_b_b_ig, m_b_lam, m_b_w_out, m_final_norm_g)))
    mom_v = dict(zip(names, (v_meta_tokens, v_a_norm_g, v_a_w_in, v_a_q_norm_g, v_a_kv_norm_g, v_a_w_uq, v_a_w_ukv,
                             v_a_w_out, v_b_norm_g, v_b_w_in, v_b_conv_w, v_b_conv_b, v_b_w_rg, v_b_b_rg, v_b_w_ig,
                             v_b_b_ig, v_b_lam, v_b_w_out, v_final_norm_g)))

    seq = x.shape[1]
    n_real = N_META + seq
    rows = -(-n_real // LANES) * LANES
    scale = (QK_NOPE + QK_ROPE) ** -0.5
    biga0_off, biga0_rows = _offsets(BIG_A0)
    biga1_off, biga1_rows = _offsets(BIG_A1)
    bigb_off, bigb_rows = _offsets(BIG_B)
    small_off, _ = _offsets(SMALL_A + SMALL_B, SLOT)
    gsmalla_off, grada_rows = _offsets(SMALL_A, SLOT, biga0_rows)
    gsmallb_off, gradb_rows = _offsets(SMALL_B, SLOT, bigb_rows)
    grada_rows = -(-grada_rows // GRAD_CHUNK) * GRAD_CHUNK
    rep_off, _ = _offsets(REP, SLOT)

    pack = lambda table: jnp.concatenate([w[nm].reshape(-1, LANES) for nm, _ in table], axis=0).astype(BF16)
    send_a0, send_a1, send_b = pack(BIG_A0), pack(BIG_A1), pack(BIG_B)
    send_small = jnp.concatenate([_slotted(w[nm].reshape(-1, LANES), 0) for nm, _ in SMALL_A + SMALL_B], axis=0)
    all_a0, all_small = _all_gather(send_a0, send_small)

    def small_seg(nm):
        o, n = small_off[nm]
        return all_small[:, o:o + n, :]

    def cols(seg, r, cdev):
        return seg.reshape(N_DEV, r, cdev).transpose(1, 0, 2).reshape(r, N_DEV * cdev)

    def seg(gathered, off, nm):
        o, n = off[nm]
        return gathered[:, o:o + n, :]

    w_in_a = cols(seg(all_a0, biga0_off, "a_w_in"), D_MODEL, 216)
    w_in_a = jnp.concatenate([w_in_a[:, :LAT + QK_ROPE], jnp.zeros((D_MODEL, LAT_PAD - LAT - QK_ROPE), BF16),
                              w_in_a[:, LAT + QK_ROPE:]], axis=1)
    meta_full = small_seg("meta_tokens").transpose(1, 0, 2).reshape(N_META, D_MODEL)
    vec = lambda nm: small_seg(nm).reshape(1, D_MODEL)
    g_b, conv_b, b_rg, b_ig, lam = vec("b_norm_g"), vec("b_conv_b"), vec("b_b_rg"), vec("b_b_ig"), vec("b_lam")
    conv_w = small_seg("b_conv_w").transpose(1, 0, 2).reshape(CONV_WIDTH, LRU_WIDTH)
    g_a, g_q, g_kv = a_norm_g, a_q_norm_g, a_kv_norm_g
    g_f = final_norm_g.reshape(1, D_MODEL)

    h0 = jnp.concatenate([meta_full, x[0], jnp.zeros((rows - n_real, D_MODEL), F32)], axis=0)
    cos, sin = _rope_tables(rows)

    lat, gate_a, all_a1 = _norm_proj_fwd(h0, g_a, w_in_a, LAT_PAD, "a_in_fwd", send_a1)
    w_uq = jnp.pad(seg(all_a1, biga1_off, "a_w_uq").reshape(N_DEV, Q_LORA, QK_NOPE + QK_ROPE).transpose(1, 0, 2),
                   ((0, 0), (0, 0), (0, HEAD_PAD - QK_NOPE - QK_ROPE))).reshape(Q_LORA, HEADS * HEAD_PAD)
    w_ukv = cols(seg(all_a1, biga1_off, "a_w_ukv"), KV_LORA, QK_NOPE + V_HEAD)
    qc, kc, v, vt = _mla_qkv_fwd(lat, g_q, g_kv, w_uq, w_ukv, cos, sin, scale)
    o, lse, all_b = _attn_fwd(qc, kc, vt, send_b)

    lru_w = lambda nm: seg(all_b, bigb_off, nm).reshape(N_DEV, LRU_BLOCKS, LRU_BLOCK // N_DEV, LRU_BLOCK).transpose(
        1, 0, 2, 3).reshape(LRU_BLOCKS, LRU_BLOCK, LRU_BLOCK)
    w_out_a = seg(all_b, bigb_off, "a_w_out").reshape(D_MODEL, D_MODEL)
    w_in_b = cols(seg(all_b, bigb_off, "b_w_in"), D_MODEL, 2 * LRU_WIDTH // N_DEV)
    w_rg, w_ig = lru_w("b_w_rg"), lru_w("b_w_ig")
    w_out_b = seg(all_b, bigb_off, "b_w_out").reshape(D_MODEL, D_MODEL)

    h1 = _gated_out_fwd(o, gate_a, h0, w_out_a, "a_out_fwd")
    u, gate_b = _norm_proj_fwd(h1, g_b, w_in_b, LRU_WIDTH, "b_in_fwd")
    hs = _rglru_fwd(u, conv_w, conv_b, w_rg, b_rg, w_ig, b_ig, lam)
    h2 = _gated_out_fwd(hs, gate_b, h1, w_out_b, "b_out_fwd")
    dh2, loss_part, dg_f = _final_loss(h2, g_f, loss_target[0], n_real)

    dhs, dgate_b, dw_out_b = _gated_out_bwd(hs, gate_b, dh2, w_out_b, F32, False, "b_out_bwd")
    du, dconv_w, dconv_b, dw_rg, db_rg, dw_ig, db_ig, dlam = _rglru_bwd(u, hs, dhs, conv_w, conv_b, w_rg, b_rg, w_ig,
                                                                       b_ig, lam)
    dh1, dw_in_b, dg_b = _norm_proj_bwd(h1, g_b, w_in_b, du, dgate_b, dh2, "b_in_bwd")
    do, dgate_a, dw_out_a, delta = _gated_out_bwd(o, gate_a, dh1, w_out_a, BF16, True, "a_out_bwd")

    def to_cols(g, cdev):
        r = g.shape[0]
        return g.reshape(r, N_DEV, cdev).transpose(1, 0, 2).reshape(N_DEV, -1, LANES)

    def packed(parts, big, small, total):
        pieces = [parts[nm].astype(BF16) for nm, _ in big] + [_slotted(parts[nm].astype(BF16), 1) for nm, _ in small]
        used = sum(p.shape[1] for p in pieces)
        if total > used:
            pieces.append(jnp.zeros((N_DEV, total - used, LANES), BF16))
        return jnp.concatenate(pieces, axis=1)

    lru_g = lambda g: g.reshape(LRU_BLOCKS, N_DEV, LRU_BLOCK // N_DEV, LRU_BLOCK).transpose(1, 0, 2, 3).reshape(
        N_DEV, -1, LANES)
    gsend_b = packed({
        "a_w_out": dw_out_a.reshape(N_DEV, -1, LANES),
        "b_w_in": to_cols(dw_in_b, 2 * LRU_WIDTH // N_DEV),
        "b_w_rg": lru_g(dw_rg), "b_w_ig": lru_g(dw_ig),
        "b_w_out": dw_out_b.reshape(N_DEV, -1, LANES),
        "b_norm_g": to_cols(dg_b, LANES), "b_conv_w": to_cols(dconv_w, LANES), "b_conv_b": to_cols(dconv_b, LANES),
        "b_b_rg": to_cols(db_rg, LANES), "b_b_ig": to_cols(db_ig, LANES), "b_lam": to_cols(dlam, LANES),
    }, BIG_B, SMALL_B, gradb_rows)

    dqc, dkc, dv, land_b = _attn_bwd(qc, kc, v, lse, delta, do, gsend_b)
    gsum_b = _sum_blocks(land_b)
    dlat, dw_uq, dw_ukv, dg_q, dg_kv = _mla_qkv_bwd(lat, g_q, g_kv, w_uq, w_ukv, cos, sin, dqc, dkc, dv, scale)
    gsend_a1 = packed({
        "a_w_uq": dw_uq.reshape(Q_LORA, HEADS, HEAD_PAD)[:, :, :QK_NOPE + QK_ROPE].transpose(1, 0, 2).reshape(
            N_DEV, -1, LANES),
        "a_w_ukv": to_cols(dw_ukv, QK_NOPE + V_HEAD),
    }, BIG_A1, (), biga1_rows)
    dh0, dw_in_a, dg_a, land_a1 = _norm_proj_bwd(h0, g_a, w_in_a, dlat, dgate_a, dh1, "a_in_bwd", gsend_a1)
    gsum_a1 = _sum_blocks(land_a1)

    grad_x = dh0[N_META:n_real][None]

    dw_in_a_nat = jnp.concatenate([dw_in_a[:, :LAT + QK_ROPE], dw_in_a[:, LAT_PAD:]], axis=1)
    gsend_a0 = packed({
        "a_w_in": to_cols(dw_in_a_nat, 216),
        "meta_tokens": to_cols(dh0[:N_META], LANES),
    }, BIG_A0, SMALL_A, grada_rows)
    rep_parts = {"a_norm_g": dg_a, "a_q_norm_g": dg_q, "a_kv_norm_g": dg_kv, "final_norm_g": dg_f,
                 "loss": jnp.broadcast_to(loss_part, (1, LANES))}
    rep = jnp.concatenate([_slotted(rep_parts[nm].reshape(-1, LANES), 0) for nm, _ in REP], axis=0)
    gsum_a0, rep_sum = _grad_exchange(gsend_a0, rep)

    grads = {}
    for off, src in ((biga0_off, gsum_a0), (gsmalla_off, gsum_a0), (biga1_off, gsum_a1), (bigb_off, gsum_b),
                     (gsmallb_off, gsum_b), (rep_off, rep_sum)):
        for nm, (o_r, n) in off.items():
            if nm in w:
                grads[nm] = src[o_r:o_r + n].reshape(w[nm].shape)
    loss = rep_sum[rep_off["loss"][0], 0]

    as2d = lambda a: a.reshape(1, -1) if a.ndim == 1 else a
    deltas, new_ms, new_vs = _adamw_all([as2d(w[nm]) for nm in names], [as2d(grads[nm]) for nm in names],
                                        [as2d(mom_m[nm]) for nm in names], [as2d(mom_v[nm]) for nm in names])
    shaped = lambda arrs: [a.reshape(w[nm].shape) for a, nm in zip(arrs, names)]
    return (loss, grad_x, *[grads[nm] for nm in names], *shaped(deltas), *shaped(new_ms), *shaped(new_vs))
```

```python
import functools

import numpy as np
import jax
import jax.numpy as jnp
from jax import lax
from jax.experimental import pallas as pl
from jax.experimental.pallas import tpu as pltpu

F32 = jnp.float32
BF16 = jnp.bfloat16

D_MODEL = 1024
N_META = 16
RMS_EPS = 1e-6
HEADS = 8
QK_NOPE = 128
QK_ROPE = 64
V_HEAD = 128
Q_LORA = 384
KV_LORA = 256
HEAD_PAD = 256
LAT = Q_LORA + KV_LORA
LAT_PAD = LAT + 128
ROPE_BASE = 10000.0
MASK_VALUE = -1e30
LRU_WIDTH = 1024
LRU_BLOCKS = 4
LRU_BLOCK = 256
CONV_WIDTH = 4
LRU_C = 8.0
N_DEV = 8
ADAM_LR, ADAM_B1, ADAM_B2, ADAM_EPS, ADAM_WD, ADAM_STEP = 0.001, 0.9, 0.999, 1e-08, 0.01, 10

LANES = 128
SUBLANES = 8
VMEM_LIMIT = 56 * 1024 * 1024
MESH = pl.DeviceIdType.MESH

NT = (((1,), (1,)), ((), ()))
TN = (((0,), (0,)), ((), ()))


def _row_block(rows):
    return 384 if rows % 384 == 0 else 128


def _cparams(sem):
    return pltpu.CompilerParams(dimension_semantics=sem, vmem_limit_bytes=VMEM_LIMIT)


def _silu(x):
    return x * jax.nn.sigmoid(x)


def _dsilu(x):
    s = jax.nn.sigmoid(x)
    return s * (1.0 + x * (1.0 - s))


def _rms_fwd(x):
    r = lax.rsqrt(jnp.mean(x * x, axis=-1, keepdims=True) + RMS_EPS)
    return x * r, r


def _rms_bwd(dy, xn, r, g):
    t = dy * g
    dx = r * (t - xn * jnp.mean(t * xn, axis=-1, keepdims=True))
    return dx, jnp.sum(dy * xn, axis=0, keepdims=True)


def _expm1_neg(x):
    small = x * (1.0 + x * (1 / 2 + x * (1 / 6 + x * (1 / 24))))
    return jnp.where(x > -0.05, small, jnp.exp(x) - 1.0)


def _softplus_neg(lam):
    z = jnp.exp(-jnp.abs(lam))
    w = z / (2.0 + z)
    w2 = w * w
    series = 2.0 * w * (1.0 + w2 * (1 / 3) + w2 * w2 * (1 / 5))
    return jnp.maximum(-lam, 0.0) + jnp.where(z < 0.1, series, jnp.log(1.0 + z))


def _rider(sends, refs, first, last, all_to_all):
    nb = len(sends)
    if not nb:
        return (lambda: None), (lambda: None)
    send_refs, result_refs = refs[:nb], refs[nb:2 * nb]
    send_sems, recv_sems, local_sems = refs[2 * nb:]
    pick = (lambda ref: (lambda d: ref.at[d])) if all_to_all else (lambda ref: (lambda d: ref))

    def copies():
        out = []
        for b in range(nb):
            out += _exchange_copies(pick(send_refs[b]), result_refs[b], send_sems.at[b], recv_sems.at[b],
                                    local_sems.at[b])
        return out

    def start():
        @pl.when(first)
        def _():
            for cp in copies():
                cp.start()

    def wait():
        @pl.when(last)
        def _():
            for cp in copies():
                cp.wait()

    return start, wait


def _rider_specs(sends, all_to_all):
    nb = len(sends)
    if not nb:
        return [], [], [], []
    hbm = pl.BlockSpec(memory_space=pl.ANY)
    shapes = [jax.ShapeDtypeStruct(s.shape if all_to_all else (N_DEV,) + s.shape, s.dtype) for s in sends]
    return [hbm] * nb, [hbm] * nb, shapes, _exchange_sems(nb)


def _proj_blocks(x, w_ref):
    return jnp.concatenate([jnp.dot(x, w_ref[d], preferred_element_type=F32) for d in range(w_ref.shape[0])], axis=1)


def _norm_proj_fwd(h, g, w, n1, name, wsends=()):
    rows = h.shape[0]
    nb, _, cb = w.shape
    n = nb * cb
    tr = _row_block(rows)
    nsteps = rows // tr
    extra = len(wsends)

    def body(h_ref, g_ref, w_ref, *rest):
        p1_ref, p2_ref = rest[extra], rest[extra + 1]
        i = pl.program_id(0)
        start, wait = _rider(wsends, rest[:extra] + rest[extra + 2:], i == 0, i == nsteps - 1, False)
        start()
        xn, _ = _rms_fwd(h_ref[...])
        p = _proj_blocks((xn * g_ref[...]).astype(BF16), w_ref)
        p1_ref[...] = p[:, :n1]
        p2_ref[...] = p[:, n1:]
        wait()

    r_in, r_out, r_shape, r_scratch = _rider_specs(wsends, False)
    return pl.pallas_call(
        body, name=name, grid=(nsteps,),
        in_specs=[pl.BlockSpec((tr, D_MODEL), lambda i: (i, 0)),
                  pl.BlockSpec((1, D_MODEL), lambda i: (0, 0)),
                  pl.BlockSpec((nb, D_MODEL, cb), lambda i: (0, 0, 0))] + r_in,
        out_specs=[pl.BlockSpec((tr, n1), lambda i: (i, 0)),
                   pl.BlockSpec((tr, n - n1), lambda i: (i, 0))] + r_out,
        out_shape=[jax.ShapeDtypeStruct((rows, n1), F32), jax.ShapeDtypeStruct((rows, n - n1), F32)] + r_shape,
        scratch_shapes=r_scratch,
        compiler_params=_cparams(("arbitrary",)),
    )(h, g, w, *wsends)


def _out_proj_in_proj(a, gate, h, w_out, g, w_in, n1, name):
    rows = h.shape[0]
    nb, _, cb = w_in.shape
    n = nb * cb
    tr = _row_block(rows)

    def body(a_ref, gate_ref, h_ref, wo_ref, g_ref, wi_ref, hn_ref, p1_ref, p2_ref):
        y = (a_ref[...] * _silu(gate_ref[...])).astype(BF16)
        h_new = h_ref[...] + jnp.dot(y, wo_ref[...], preferred_element_type=F32)
        hn_ref[...] = h_new
        xn, _ = _rms_fwd(h_new)
        p = _proj_blocks((xn * g_ref[...]).astype(BF16), wi_ref)
        p1_ref[...] = p[:, :n1]
        p2_ref[...] = p[:, n1:]

    blk = pl.BlockSpec((tr, D_MODEL), lambda i: (i, 0))
    return pl.pallas_call(
        body, name=name, grid=(rows // tr,),
        in_specs=[blk, blk, blk, pl.BlockSpec((D_MODEL, D_MODEL), lambda i: (0, 0)),
                  pl.BlockSpec((1, D_MODEL), lambda i: (0, 0)), pl.BlockSpec((nb, D_MODEL, cb), lambda i: (0, 0, 0))],
        out_specs=[blk, pl.BlockSpec((tr, n1), lambda i: (i, 0)), pl.BlockSpec((tr, n - n1), lambda i: (i, 0))],
        out_shape=[jax.ShapeDtypeStruct((rows, D_MODEL), F32), jax.ShapeDtypeStruct((rows, n1), F32),
                   jax.ShapeDtypeStruct((rows, n - n1), F32)],
        compiler_params=_cparams(("parallel",)),
    )(a, gate, h, w_out, g, w_in)


def _norm_proj_bwd(h, g, w, dp1, dp2, dh_in, name, gsends=()):
    rows = h.shape[0]
    nb, _, cb = w.shape
    n1 = dp1.shape[1]
    n2 = nb * cb - n1
    tr = _row_block(rows)
    nsteps = rows // tr
    extra = len(gsends)

    def body(h_ref, g_ref, w_ref, dp1_ref, dp2_ref, dhin_ref, *rest):
        dh_ref, dw_ref, dg_ref = rest[extra:extra + 3]
        i = pl.program_id(0)
        start, wait = _rider(gsends, rest[:extra] + rest[extra + 3:], i == 0, i == nsteps - 1, True)
        start()

        @pl.when(i == 0)
        def _():
            dw_ref[...] = jnp.zeros_like(dw_ref)
            dg_ref[...] = jnp.zeros_like(dg_ref)

        gv = g_ref[...]
        xn, r = _rms_fwd(h_ref[...])
        hn = (xn * gv).astype(BF16)
        dp = jnp.concatenate([dp1_ref[...].astype(BF16), dp2_ref[...].astype(BF16)], axis=1)
        dhn = jnp.zeros((tr, D_MODEL), F32)
        for d in range(nb):
            dpd = dp[:, d * cb:(d + 1) * cb]
            dw_ref[d] += lax.dot_general(hn, dpd, TN, preferred_element_type=F32)
            dhn = dhn + lax.dot_general(dpd, w_ref[d], NT, preferred_element_type=F32)
        dx, dg = _rms_bwd(dhn, xn, r, gv)
        dg_ref[...] += dg
        dh_ref[...] = dhin_ref[...] + dx
        wait()

    r_in, r_out, r_shape, r_scratch = _rider_specs(gsends, True)
    wblk = pl.BlockSpec((nb, D_MODEL, cb), lambda i: (0, 0, 0))
    return pl.pallas_call(
        body, name=name, grid=(nsteps,),
        in_specs=[pl.BlockSpec((tr, D_MODEL), lambda i: (i, 0)),
                  pl.BlockSpec((1, D_MODEL), lambda i: (0, 0)),
                  wblk,
                  pl.BlockSpec((tr, n1), lambda i: (i, 0)),
                  pl.BlockSpec((tr, n2), lambda i: (i, 0)),
                  pl.BlockSpec((tr, D_MODEL), lambda i: (i, 0))] + r_in,
        out_specs=[pl.BlockSpec((tr, D_MODEL), lambda i: (i, 0)), wblk,
                   pl.BlockSpec((1, D_MODEL), lambda i: (0, 0))] + r_out,
        out_shape=[jax.ShapeDtypeStruct((rows, D_MODEL), F32),
                   jax.ShapeDtypeStruct((nb, D_MODEL, cb), F32),
                   jax.ShapeDtypeStruct((1, D_MODEL), F32)] + r_shape,
        scratch_shapes=r_scratch,
        compiler_params=_cparams(("arbitrary",)),
    )(h, g, w, dp1, dp2, dh_in, *gsends)


def _gated_out_bwd(a, gate, dh, w, da_dtype, with_delta, name):
    rows = a.shape[0]
    tr = _row_block(rows)

    def body(a_ref, gate_ref, dh_ref, w_ref, da_ref, dgate_ref, dw_ref, *delta_ref):
        @pl.when(pl.program_id(0) == 0)
        def _():
            dw_ref[...] = jnp.zeros_like(dw_ref)

        av, gv = a_ref[...], gate_ref[...]
        sg = _silu(gv)
        dhb = dh_ref[...].astype(BF16)
        dw_ref[...] += lax.dot_general((av * sg).astype(BF16), dhb, TN, preferred_element_type=F32)
        dy = lax.dot_general(dhb, w_ref[...], NT, preferred_element_type=F32)
        da = (dy * sg).astype(da_dtype)
        da_ref[...] = da
        dgate_ref[...] = (dy * av * _dsilu(gv)).astype(BF16)
        if with_delta:
            prod = da.astype(F32) * av
            lane = lax.broadcasted_iota(jnp.int32, (tr, LANES), 1)
            per_head = jnp.zeros((tr, LANES), F32)
            for hd in range(HEADS):
                dsum = jnp.sum(prod[:, hd * V_HEAD:(hd + 1) * V_HEAD], axis=1, keepdims=True)
                per_head = jnp.where(lane == hd, dsum, per_head)
            delta_t = per_head.T
            for hd in range(HEADS):
                delta_ref[0][hd, 0] = delta_t[hd:hd + 1, :]

    blk = pl.BlockSpec((tr, D_MODEL), lambda i: (i, 0))
    wblk = pl.BlockSpec((D_MODEL, D_MODEL), lambda i: (0, 0))
    out_specs = [blk, blk, wblk]
    out_shape = [jax.ShapeDtypeStruct((rows, D_MODEL), da_dtype),
                 jax.ShapeDtypeStruct((rows, D_MODEL), BF16),
                 jax.ShapeDtypeStruct((D_MODEL, D_MODEL), F32)]
    if with_delta:
        out_specs.append(pl.BlockSpec((HEADS, 1, 1, tr), lambda i: (0, i, 0, 0)))
        out_shape.append(jax.ShapeDtypeStruct((HEADS, rows // tr, 1, tr), F32))
    return pl.pallas_call(
        body, name=name, grid=(rows // tr,),
        in_specs=[blk, blk, blk, wblk],
        out_specs=out_specs, out_shape=out_shape,
        compiler_params=_cparams(("arbitrary",)),
    )(a, gate, dh, w)


def _rope(v, cos, sin, lane):
    swapped = jnp.where(lane < QK_ROPE // 2, pltpu.roll(v, LANES - QK_ROPE // 2, 1), pltpu.roll(v, QK_ROPE // 2, 1))
    return v * cos + swapped * sin


def _unrope(dv, cos, sin, lane):
    t = dv * sin
    swapped = jnp.where(lane < QK_ROPE // 2, pltpu.roll(t, LANES - QK_ROPE // 2, 1), pltpu.roll(t, QK_ROPE // 2, 1))
    return dv * cos + swapped


def _mla_qkv_fwd(lat, gq, gkv, wuq, wukv, cos, sin, scale):
    rows = lat.shape[0]
    tr = _row_block(rows)

    def body(lat_ref, gq_ref, gkv_ref, wuq_ref, wukv_ref, cos_ref, sin_ref, qc_ref, kc_ref, v_ref, vt_ref):
        qn, _ = _rms_fwd(lat_ref[:, :Q_LORA])
        kvn, _ = _rms_fwd(lat_ref[:, Q_LORA:LAT])
        qnb = (qn * gq_ref[...]).astype(BF16)
        kvnb = (kvn * gkv_ref[...]).astype(BF16)
        c, s = cos_ref[...], sin_ref[...]
        lane = lax.broadcasted_iota(jnp.int32, (tr, LANES), 1)
        kr = _rope(lat_ref[:, LAT:LAT_PAD], c, s, lane).astype(BF16)
        for hd in range(HEADS):
            o = hd * HEAD_PAD
            q = jnp.dot(qnb, wuq_ref[hd], preferred_element_type=F32)
            kv = jnp.dot(kvnb, wukv_ref[hd], preferred_element_type=F32)
            qc_ref[:, o:o + QK_NOPE] = (q[:, :QK_NOPE] * scale).astype(BF16)
            qc_ref[:, o + QK_NOPE:o + HEAD_PAD] = (_rope(q[:, QK_NOPE:], c, s, lane) * scale).astype(BF16)
            kc_ref[:, o:o + QK_NOPE] = kv[:, :QK_NOPE].astype(BF16)
            kc_ref[:, o + QK_NOPE:o + HEAD_PAD] = kr
            vh = kv[:, QK_NOPE:]
            v_ref[:, hd * V_HEAD:(hd + 1) * V_HEAD] = vh.astype(BF16)
            vt_ref[hd, 0] = vh.T.astype(BF16)

    full = lambda shape: pl.BlockSpec(shape, lambda i: (0,) * len(shape))
    rowb = lambda n: pl.BlockSpec((tr, n), lambda i: (i, 0))
    return pl.pallas_call(
        body, name="mla_qkv_fwd", grid=(rows // tr,),
        in_specs=[rowb(LAT_PAD), full((1, Q_LORA)), full((1, KV_LORA)), full((HEADS, Q_LORA, HEAD_PAD)),
                  full((HEADS, KV_LORA, HEAD_PAD)), rowb(LANES), rowb(LANES)],
        out_specs=[rowb(HEADS * HEAD_PAD), rowb(HEADS * HEAD_PAD), rowb(HEADS * V_HEAD),
                   pl.BlockSpec((HEADS, 1, V_HEAD, tr), lambda i: (0, i, 0, 0))],
        out_shape=[jax.ShapeDtypeStruct((rows, HEADS * HEAD_PAD), BF16),
                   jax.ShapeDtypeStruct((rows, HEADS * HEAD_PAD), BF16),
                   jax.ShapeDtypeStruct((rows, HEADS * V_HEAD), BF16),
                   jax.ShapeDtypeStruct((HEADS, rows // tr, V_HEAD, tr), BF16)],
        compiler_params=_cparams(("parallel",)),
    )(lat, gq, gkv, wuq, wukv, cos, sin)


def _mla_qkv_bwd(lat, gq, gkv, wuq, wukv, cos, sin, dqc, dkc, dv, scale):
    rows = lat.shape[0]
    tr = _row_block(rows)

    def body(lat_ref, gq_ref, gkv_ref, wuq_ref, wukv_ref, cos_ref, sin_ref, dqc_ref, dkc_ref, dv_ref,
             dlat_ref, dwuq_ref, dwukv_ref, dgq_ref, dgkv_ref):
        @pl.when(pl.program_id(0) == 0)
        def _():
            dwuq_ref[...] = jnp.zeros_like(dwuq_ref)
            dwukv_ref[...] = jnp.zeros_like(dwukv_ref)
            dgq_ref[...] = jnp.zeros_like(dgq_ref)
            dgkv_ref[...] = jnp.zeros_like(dgkv_ref)

        c, s = cos_ref[...], sin_ref[...]
        lane = lax.broadcasted_iota(jnp.int32, (tr, LANES), 1)
        gqv, gkvv = gq_ref[...], gkv_ref[...]
        qn, rq = _rms_fwd(lat_ref[:, :Q_LORA])
        kvn, rkv = _rms_fwd(lat_ref[:, Q_LORA:LAT])
        qnb = (qn * gqv).astype(BF16)
        kvnb = (kvn * gkvv).astype(BF16)
        dkr = jnp.zeros((tr, LANES), F32)
        dqn = jnp.zeros((tr, Q_LORA), F32)
        dkvn = jnp.zeros((tr, KV_LORA), F32)
        for hd in range(HEADS):
            o = hd * HEAD_PAD
            dq = jnp.concatenate(
                [dqc_ref[:, o:o + QK_NOPE],
                 _unrope(dqc_ref[:, o + QK_NOPE:o + HEAD_PAD].astype(F32), c, s, lane).astype(BF16)], axis=1)
            dkv = jnp.concatenate([dkc_ref[:, o:o + QK_NOPE], dv_ref[:, hd * V_HEAD:(hd + 1) * V_HEAD]], axis=1)
            dkr = dkr + dkc_ref[:, o + QK_NOPE:o + HEAD_PAD].astype(F32)
            dwuq_ref[hd] += scale * lax.dot_general(qnb, dq, TN, preferred_element_type=F32)
            dwukv_ref[hd] += lax.dot_general(kvnb, dkv, TN, preferred_element_type=F32)
            dqn = dqn + lax.dot_general(dq, wuq_ref[hd], NT, preferred_element_type=F32)
            dkvn = dkvn + lax.dot_general(dkv, wukv_ref[hd], NT, preferred_element_type=F32)
        dqn = scale * dqn
        dqlat, dgq = _rms_bwd(dqn, qn, rq, gqv)
        dkvlat, dgkv = _rms_bwd(dkvn, kvn, rkv, gkvv)
        dgq_ref[...] += dgq
        dgkv_ref[...] += dgkv
        dlat_ref[:, :Q_LORA] = dqlat.astype(BF16)
        dlat_ref[:, Q_LORA:LAT] = dkvlat.astype(BF16)
        dlat_ref[:, LAT:LAT_PAD] = _unrope(dkr, c, s, lane).astype(BF16)

    full = lambda shape: pl.BlockSpec(shape, lambda i: (0,) * len(shape))
    rowb = lambda n: pl.BlockSpec((tr, n), lambda i: (i, 0))
    return pl.pallas_call(
        body, name="mla_qkv_bwd", grid=(rows // tr,),
        in_specs=[rowb(LAT_PAD), full((1, Q_LORA)), full((1, KV_LORA)), full((HEADS, Q_LORA, HEAD_PAD)),
                  full((HEADS, KV_LORA, HEAD_PAD)), rowb(LANES), rowb(LANES),
                  rowb(HEADS * HEAD_PAD), rowb(HEADS * HEAD_PAD), rowb(HEADS * V_HEAD)],
        out_specs=[rowb(LAT_PAD), full((HEADS, Q_LORA, HEAD_PAD)), full((HEADS, KV_LORA, HEAD_PAD)),
                   full((1, Q_LORA)), full((1, KV_LORA))],
        out_shape=[jax.ShapeDtypeStruct((rows, LAT_PAD), BF16),
                   jax.ShapeDtypeStruct((HEADS, Q_LORA, HEAD_PAD), F32),
                   jax.ShapeDtypeStruct((HEADS, KV_LORA, HEAD_PAD), F32),
                   jax.ShapeDtypeStruct((1, Q_LORA), F32),
                   jax.ShapeDtypeStruct((1, KV_LORA), F32)],
        compiler_params=_cparams(("arbitrary",)),
    )(lat, gq, gkv, wuq, wukv, cos, sin, dqc, dkc, dv)


ATTN_UNROLL = 4


def _causal_mask_t(t):
    key = lax.broadcasted_iota(jnp.int32, (t, t), 0)
    query = lax.broadcasted_iota(jnp.int32, (t, t), 1)
    return key <= query


def _attn_fwd(qc, kc, vt, wsends):
    rows = qc.shape[0]
    t = _row_block(rows)
    nblk = rows // t
    nw = len(wsends)

    def body(q_ref, k_ref, vt_ref, *rest):
        o_ref, lse_ref = rest[nw:nw + 2]
        m_ref, l_ref, acc_ref, st_a, st_b = rest[2 * nw + 2:2 * nw + 7]
        i = pl.program_id(1)
        start, wait = _rider(wsends, rest[:nw] + rest[nw + 2:2 * nw + 2] + rest[2 * nw + 7:],
                             jnp.logical_and(pl.program_id(0) == 0, i == 0),
                             jnp.logical_and(pl.program_id(0) == HEADS - 1, i == nblk - 1), False)
        start()

        m_ref[...] = jnp.full_like(m_ref, MASK_VALUE)
        l_ref[...] = jnp.zeros_like(l_ref)
        acc_ref[...] = jnp.zeros_like(acc_ref)
        q = q_ref[...]

        def scores(j, st_ref):
            rs = pl.ds(pl.multiple_of(j * t, t), t)
            st_ref[...] = lax.dot_general(k_ref[rs, :], q, NT, preferred_element_type=F32)

        def consume(j, st_ref, masked):
            st = st_ref[...]
            if masked:
                st = jnp.where(_causal_mask_t(t), st, MASK_VALUE)
            m_prev = m_ref[...]
            m_new = jnp.maximum(m_prev, jnp.max(st, axis=0, keepdims=True))
            alpha = jnp.exp(m_prev - m_new)
            pt = jnp.exp(st - m_new)
            l_ref[...] = alpha * l_ref[...] + jnp.sum(pt, axis=0, keepdims=True)
            acc_ref[...] = alpha * acc_ref[...] + jnp.dot(vt_ref[0, j], pt.astype(BF16), preferred_element_type=F32)
            m_ref[...] = m_new

        bufs = (st_a, st_b)

        def step(j, parity, issue_next, masked):
            if issue_next:
                scores(j + 1, bufs[1 - parity])
            consume(j, bufs[parity], masked)

        scores(0, st_a)

        def trip(it, carry):
            for u in range(ATTN_UNROLL):
                step(it * ATTN_UNROLL + u, u % 2, True, False)
            return carry

        trips = i // ATTN_UNROLL
        lax.fori_loop(0, trips, trip, 0)
        j0 = trips * ATTN_UNROLL
        for left in range(1, ATTN_UNROLL + 1):
            @pl.when(i + 1 - j0 == left)
            def _(left=left):
                for u in range(left):
                    step(j0 + u, u % 2, u < left - 1, u == left - 1)

        o_ref[...] = (acc_ref[...] / l_ref[...]).T
        lse_ref[0, 0] = m_ref[...] + jnp.log(l_ref[...])
        wait()

    r_in, r_out, r_shape, r_scratch = _rider_specs(wsends, False)
    return pl.pallas_call(
        body, name="attn_fwd", grid=(HEADS, nblk),
        in_specs=[pl.BlockSpec((t, HEAD_PAD), lambda h, i: (i, h)),
                  pl.BlockSpec((rows, HEAD_PAD), lambda h, i: (0, h)),
                  pl.BlockSpec((1, nblk, V_HEAD, t), lambda h, i: (h, 0, 0, 0))] + r_in,
        out_specs=[pl.BlockSpec((t, V_HEAD), lambda h, i: (i, h)),
                   pl.BlockSpec((1, 1, 1, t), lambda h, i: (h, i, 0, 0))] + r_out,
        out_shape=[jax.ShapeDtypeStruct((rows, HEADS * V_HEAD), F32),
                   jax.ShapeDtypeStruct((HEADS, nblk, 1, t), F32)] + r_shape,
        scratch_shapes=[pltpu.VMEM((1, t), F32), pltpu.VMEM((1, t), F32), pltpu.VMEM((V_HEAD, t), F32),
                        pltpu.VMEM((t, t), F32), pltpu.VMEM((t, t), F32)] + r_scratch,
        compiler_params=_cparams(("arbitrary", "arbitrary")),
    )(qc, kc, vt, *wsends)


def _attn_bwd(qc, kc, v, lse, delta, do, gsends):
    rows = qc.shape[0]
    t = _row_block(rows)
    nblk = rows // t
    ng = len(gsends)

    def body(q_ref, k_ref, v_ref, lse_ref, delta_ref, do_ref, *rest):
        dq_ref, dk_ref, dv_ref = rest[ng:ng + 3]
        dq_acc, dk_acc, dv_acc, st_a, dp_a, st_b, dp_b = rest[2 * ng + 3:2 * ng + 10]
        j = pl.program_id(1)
        start, wait = _rider(gsends, rest[:ng] + rest[ng + 3:2 * ng + 3] + rest[2 * ng + 10:],
                             jnp.logical_and(pl.program_id(0) == 0, j == 0),
                             jnp.logical_and(pl.program_id(0) == HEADS - 1, j == nblk - 1), True)
        start()

        @pl.when(j == 0)
        def _():
            dq_acc[...] = jnp.zeros_like(dq_acc)

        dk_acc[...] = jnp.zeros_like(dk_acc)
        dv_acc[...] = jnp.zeros_like(dv_acc)
        k = k_ref[...]
        vv = v_ref[...]

        def products(i, st_ref, dp_ref):
            rs = pl.ds(pl.multiple_of(i * t, t), t)
            st_ref[...] = lax.dot_general(k, q_ref[rs, :], NT, preferred_element_type=F32)
            dp_ref[...] = lax.dot_general(vv, do_ref[rs, :], NT, preferred_element_type=F32)

        def consume(i, st_ref, dp_ref):
            rs = pl.ds(pl.multiple_of(i * t, t), t)
            q = q_ref[rs, :]
            dob = do_ref[rs, :]
            st = jnp.where(jnp.logical_or(_causal_mask_t(t), i != j), st_ref[...], MASK_VALUE)
            pt = jnp.exp(st - lse_ref[0, i])
            dv_acc[...] += jnp.dot(pt.astype(BF16), dob, preferred_element_type=F32)
            dst = (pt * (dp_ref[...] - delta_ref[0, i])).astype(BF16)
            dk_acc[...] += jnp.dot(dst, q, preferred_element_type=F32)
            dq_acc[rs, :] += lax.dot_general(dst, k, TN, preferred_element_type=F32)

        bufs = ((st_a, dp_a), (st_b, dp_b))

        def step(i, parity, issue_next):
            if issue_next:
                products(i + 1, *bufs[1 - parity])
            consume(i, *bufs[parity])

        products(j, st_a, dp_a)

        def trip(it, carry):
            for u in range(ATTN_UNROLL):
                step(j + it * ATTN_UNROLL + u, u % 2, True)
            return carry

        trips = (nblk - 1 - j) // ATTN_UNROLL
        lax.fori_loop(0, trips, trip, 0)
        i0 = j + trips * ATTN_UNROLL
        for left in range(1, ATTN_UNROLL + 1):
            @pl.when(nblk - i0 == left)
            def _(left=left):
                for u in range(left):
                    step(i0 + u, u % 2, u < left - 1)

        dk_ref[...] = dk_acc[...].astype(BF16)
        dv_ref[...] = dv_acc[...].astype(BF16)

        @pl.when(j == nblk - 1)
        def _():
            dq_ref[...] = dq_acc[...].astype(BF16)

        wait()

    stat = pl.BlockSpec((1, nblk, 1, t), lambda h, j: (h, 0, 0, 0))
    r_in, r_out, r_shape, r_scratch = _rider_specs(gsends, True)
    return pl.pallas_call(
        body, name="attn_bwd", grid=(HEADS, nblk),
        in_specs=[pl.BlockSpec((rows, HEAD_PAD), lambda h, j: (0, h)),
                  pl.BlockSpec((t, HEAD_PAD), lambda h, j: (j, h)),
                  pl.BlockSpec((t, V_HEAD), lambda h, j: (j, h)),
                  stat, stat,
                  pl.BlockSpec((rows, V_HEAD), lambda h, j: (0, h))] + r_in,
        out_specs=[pl.BlockSpec((rows, HEAD_PAD), lambda h, j: (0, h)),
                   pl.BlockSpec((t, HEAD_PAD), lambda h, j: (j, h)),
                   pl.BlockSpec((t, V_HEAD), lambda h, j: (j, h))] + r_out,
        out_shape=[jax.ShapeDtypeStruct((rows, HEADS * HEAD_PAD), BF16),
                   jax.ShapeDtypeStruct((rows, HEADS * HEAD_PAD), BF16),
                   jax.ShapeDtypeStruct((rows, HEADS * V_HEAD), BF16)] + r_shape,
        scratch_shapes=[pltpu.VMEM((rows, HEAD_PAD), F32), pltpu.VMEM((t, HEAD_PAD), F32),
                        pltpu.VMEM((t, V_HEAD), F32)] + [pltpu.VMEM((t, t), F32)] * 4 + r_scratch,
        compiler_params=_cparams(("arbitrary", "arbitrary")),
    )(qc, kc, v, lse, delta, do, *gsends)


def _shift_down(prev_tile, x, k):
    xx = jnp.concatenate([prev_tile, x], axis=0)
    return pltpu.roll(xx, k, 0)[SUBLANES:]


def _shift_up(x, next_tile, k):
    n = x.shape[0]
    xx = jnp.concatenate([x, next_tile], axis=0)
    return pltpu.roll(xx, n + SUBLANES - k, 0)[:n]


def _lru_gates(u, u_prev, cw_ref, cb_ref, wrg_ref, brg_ref, wig_ref, big_ref, lam_ref, first_block):
    taps = [_shift_down(u_prev, u, CONV_WIDTH - 1 - j) if j < CONV_WIDTH - 1 else u for j in range(CONV_WIDTH)]
    uc = cb_ref[...] + taps[0] * cw_ref[0:1, :]
    for j in range(1, CONV_WIDTH):
        uc = uc + taps[j] * cw_ref[j:j + 1, :]
    ub = uc.astype(BF16)
    zr = jnp.concatenate([jnp.dot(ub[:, g * LRU_BLOCK:(g + 1) * LRU_BLOCK], wrg_ref[g], preferred_element_type=F32)
                          for g in range(LRU_BLOCKS)], axis=1) + brg_ref[...]
    zi = jnp.concatenate([jnp.dot(ub[:, g * LRU_BLOCK:(g + 1) * LRU_BLOCK], wig_ref[g], preferred_element_type=F32)
                          for g in range(LRU_BLOCKS)], axis=1) + big_ref[...]
    r = jax.nn.sigmoid(zr)
    ig = jax.nn.sigmoid(zi)
    sp = _softplus_neg(lam_ref[...])
    log_a = (-LRU_C) * r * sp
    a = jnp.exp(log_a)
    m2 = -_expm1_neg(2.0 * log_a)
    mult_raw = m2 * lax.rsqrt(jnp.maximum(m2, 1e-30))
    row = lax.broadcasted_iota(jnp.int32, u.shape, 0)
    is_start = jnp.logical_and(first_block, row == 0)
    mult = jnp.where(is_start, 1.0, mult_raw)
    return dict(taps=taps, uc=uc, ub=ub, r=r, ig=ig, sp=sp, a=a, mult=mult, mult_raw=mult_raw, is_start=is_start)


def _rglru_fwd(u, cw, cb, wrg, brg, wig, big, lam):
    rows = u.shape[0]
    tb = _row_block(rows)

    def body(u_ref, cw_ref, cb_ref, wrg_ref, brg_ref, wig_ref, big_ref, lam_ref, hs_ref, utail, hcar, a_s, b_s):
        i = pl.program_id(0)

        @pl.when(i == 0)
        def _():
            utail[...] = jnp.zeros_like(utail)
            hcar[...] = jnp.zeros_like(hcar)

        u = u_ref[...]
        gt = _lru_gates(u, utail[...], cw_ref, cb_ref, wrg_ref, brg_ref, wig_ref, big_ref, lam_ref, i == 0)
        a_s[...] = gt["a"]
        b_s[...] = gt["mult"] * (gt["ig"] * gt["uc"])
        row8 = lax.broadcasted_iota(jnp.int32, (SUBLANES, LRU_WIDTH), 0)

        def tile(tix, carry):
            rs = pl.ds(pl.multiple_of(tix * SUBLANES, SUBLANES), SUBLANES)
            av, bv = a_s[rs, :], b_s[rs, :]
            for k in (1, 2, 4):
                keep = row8 >= k
                bv = jnp.where(keep, av * pltpu.roll(bv, k, 0) + bv, bv)
                av = jnp.where(keep, av * pltpu.roll(av, k, 0), av)
            h8 = av * carry + bv
            hs_ref[rs, :] = h8
            return jnp.broadcast_to(h8[SUBLANES - 1:SUBLANES, :], (SUBLANES, LRU_WIDTH))

        hcar[...] = lax.fori_loop(0, tb // SUBLANES, tile, hcar[...])
        utail[...] = u[tb - SUBLANES:, :]

    full2 = lambda shape: pl.BlockSpec(shape, lambda i: (0, 0))
    full3 = lambda shape: pl.BlockSpec(shape, lambda i: (0, 0, 0))
    blk = pl.BlockSpec((tb, LRU_WIDTH), lambda i: (i, 0))
    return pl.pallas_call(
        body, name="rglru_fwd", grid=(rows // tb,),
        in_specs=[blk, full2((CONV_WIDTH, LRU_WIDTH)), full2((1, LRU_WIDTH)),
                  full3((LRU_BLOCKS, LRU_BLOCK, LRU_BLOCK)), full2((1, LRU_WIDTH)),
                  full3((LRU_BLOCKS, LRU_BLOCK, LRU_BLOCK)), full2((1, LRU_WIDTH)), full2((1, LRU_WIDTH))],
        out_specs=blk,
        out_shape=jax.ShapeDtypeStruct((rows, LRU_WIDTH), F32),
        scratch_shapes=[pltpu.VMEM((SUBLANES, LRU_WIDTH), F32), pltpu.VMEM((SUBLANES, LRU_WIDTH), F32),
                        pltpu.VMEM((tb, LRU_WIDTH), F32), pltpu.VMEM((tb, LRU_WIDTH), F32)],
        compiler_params=_cparams(("arbitrary",)),
    )(u, cw, cb, wrg, brg, wig, big, lam)


def _rglru_bwd(u, hs, dhs, cw, cb, wrg, brg, wig, big, lam):
    rows = u.shape[0]
    tb = _row_block(rows)
    nblk = rows // tb
    tiles = tb // SUBLANES

    def body(u_ref, up_ref, hs_ref, hp_ref, dhs_ref, cw_ref, cb_ref, wrg_ref, brg_ref, wig_ref, big_ref, lam_ref,
             du_ref, dcw_ref, dcb_ref, dwrg_ref, dbrg_ref, dwig_ref, dbig_ref, dlam_ref,
             gcar, duc_head, a_s, b_s, g_s, dsp_acc):
        step = pl.program_id(0)
        blk_ix = nblk - 1 - step

        @pl.when(step == 0)
        def _():
            for ref in (dcw_ref, dcb_ref, dwrg_ref, dbrg_ref, dwig_ref, dbig_ref, gcar, duc_head, dsp_acc):
                ref[...] = jnp.zeros_like(ref)

        first = blk_ix == 0
        u = u_ref[...]
        u_prev = jnp.where(first, 0.0, up_ref[...])
        h_prev_tile = jnp.where(first, 0.0, hp_ref[...])
        gt = _lru_gates(u, u_prev, cw_ref, cb_ref, wrg_ref, brg_ref, wig_ref, big_ref, lam_ref, first)
        a, r, ig, uc, mult = gt["a"], gt["r"], gt["ig"], gt["uc"], gt["mult"]
        dhs_v = dhs_ref[...]

        a_s[...] = a
        b_s[...] = a * dhs_v
        row8 = lax.broadcasted_iota(jnp.int32, (SUBLANES, LRU_WIDTH), 0)

        def tile(tix, carry):
            rs = pl.ds(pl.multiple_of((tiles - 1 - tix) * SUBLANES, SUBLANES), SUBLANES)
            av, bv = a_s[rs, :], b_s[rs, :]
            for k in (1, 2, 4):
                keep = row8 < SUBLANES - k
                bv = jnp.where(keep, av * pltpu.roll(bv, SUBLANES - k, 0) + bv, bv)
                av = jnp.where(keep, av * pltpu.roll(av, SUBLANES - k, 0), av)
            g8 = av * carry + bv
            g_s[rs, :] = g8
            return jnp.broadcast_to(g8[0:1, :], (SUBLANES, LRU_WIDTH))

        g_next = gcar[...]
        gcar[...] = lax.fori_loop(0, tiles, tile, g_next)
        g = dhs_v + _shift_up(g_s[...], g_next, 1)

        h_prev = _shift_down(h_prev_tile, hs_ref[...], 1)
        da = g * h_prev
        iu = ig * uc
        dmult = jnp.where(gt["is_start"], 0.0, g * iu)
        d_ig = g * mult * uc
        duc = g * mult * ig
        dlog_a = da * a - dmult * (a * a) / gt["mult_raw"]
        dzr = (dlog_a * ((-LRU_C) * gt["sp"])) * r * (1.0 - r)
        dsp_acc[...] += jnp.sum(dlog_a * ((-LRU_C) * r), axis=0, keepdims=True)
        dzi = d_ig * ig * (1.0 - ig)
        dbrg_ref[...] += jnp.sum(dzr, axis=0, keepdims=True)
        dbig_ref[...] += jnp.sum(dzi, axis=0, keepdims=True)
        dzr_b, dzi_b = dzr.astype(BF16), dzi.astype(BF16)
        ub = gt["ub"]
        duc_parts = []
        for gi in range(LRU_BLOCKS):
            cs = slice(gi * LRU_BLOCK, (gi + 1) * LRU_BLOCK)
            dwrg_ref[gi] += lax.dot_general(ub[:, cs], dzr_b[:, cs], TN, preferred_element_type=F32)
            dwig_ref[gi] += lax.dot_general(ub[:, cs], dzi_b[:, cs], TN, preferred_element_type=F32)
            duc_parts.append(lax.dot_general(dzr_b[:, cs], wrg_ref[gi], NT, preferred_element_type=F32)
                             + lax.dot_general(dzi_b[:, cs], wig_ref[gi], NT, preferred_element_type=F32))
        duc = duc + jnp.concatenate(duc_parts, axis=1)

        dcb_ref[...] += jnp.sum(duc, axis=0, keepdims=True)
        taps = gt["taps"]
        for jt in range(CONV_WIDTH):
            dcw_ref[jt:jt + 1, :] += jnp.sum(duc * taps[jt], axis=0, keepdims=True)
        head = duc_head[...]
        du = duc * cw_ref[CONV_WIDTH - 1:CONV_WIDTH, :]
        for jt in range(CONV_WIDTH - 1):
            du = du + _shift_up(duc, head, CONV_WIDTH - 1 - jt) * cw_ref[jt:jt + 1, :]
        du_ref[...] = du.astype(BF16)
        duc_head[...] = duc[:SUBLANES, :]

        @pl.when(step == nblk - 1)
        def _():
            dlam_ref[...] = -dsp_acc[...] * jax.nn.sigmoid(-lam_ref[...])

    full2 = lambda shape: pl.BlockSpec(shape, lambda s: (0, 0))
    full3 = lambda shape: pl.BlockSpec(shape, lambda s: (0, 0, 0))
    blk = pl.BlockSpec((tb, LRU_WIDTH), lambda s: (nblk - 1 - s, 0))
    prev_tile = pl.BlockSpec((SUBLANES, LRU_WIDTH), lambda s: (jnp.maximum((nblk - 1 - s) * tiles - 1, 0), 0))
    wshape = (LRU_BLOCKS, LRU_BLOCK, LRU_BLOCK)
    return pl.pallas_call(
        body, name="rglru_bwd", grid=(nblk,),
        in_specs=[blk, prev_tile, blk, prev_tile, blk, full2((CONV_WIDTH, LRU_WIDTH)), full2((1, LRU_WIDTH)),
                  full3(wshape), full2((1, LRU_WIDTH)), full3(wshape), full2((1, LRU_WIDTH)), full2((1, LRU_WIDTH))],
        out_specs=[blk, full2((CONV_WIDTH, LRU_WIDTH)), full2((1, LRU_WIDTH)), full3(wshape), full2((1, LRU_WIDTH)),
                   full3(wshape), full2((1, LRU_WIDTH)), full2((1, LRU_WIDTH))],
        out_shape=[jax.ShapeDtypeStruct((rows, LRU_WIDTH), BF16),
                   jax.ShapeDtypeStruct((CONV_WIDTH, LRU_WIDTH), F32), jax.ShapeDtypeStruct((1, LRU_WIDTH), F32),
                   jax.ShapeDtypeStruct(wshape, F32), jax.ShapeDtypeStruct((1, LRU_WIDTH), F32),
                   jax.ShapeDtypeStruct(wshape, F32), jax.ShapeDtypeStruct((1, LRU_WIDTH), F32),
                   jax.ShapeDtypeStruct((1, LRU_WIDTH), F32)],
        scratch_shapes=[pltpu.VMEM((SUBLANES, LRU_WIDTH), F32), pltpu.VMEM((SUBLANES, LRU_WIDTH), F32),
                        pltpu.VMEM((tb, LRU_WIDTH), F32), pltpu.VMEM((tb, LRU_WIDTH), F32),
                        pltpu.VMEM((tb, LRU_WIDTH), F32), pltpu.VMEM((1, LRU_WIDTH), F32)],
        compiler_params=_cparams(("arbitrary",)),
    )(u, u, hs, hs, dhs, cw, cb, wrg, brg, wig, big, lam)


def _out_proj_loss(a, gate, h, w, gf, target, n_real):
    rows = h.shape[0]
    tr = _row_block(rows)

    def body(a_ref, gate_ref, h_ref, w_ref, g_ref, t_ref, dh_ref, loss_ref, dg_ref, da_ref, dgate_ref, dw_ref):
        i = pl.program_id(0)

        @pl.when(i == 0)
        def _():
            loss_ref[...] = jnp.zeros_like(loss_ref)
            dg_ref[...] = jnp.zeros_like(dg_ref)
            dw_ref[...] = jnp.zeros_like(dw_ref)

        gv = g_ref[...]
        av, gatev = a_ref[...], gate_ref[...]
        sg = _silu(gatev)
        y = (av * sg).astype(BF16)
        xn, r = _rms_fwd(h_ref[...] + jnp.dot(y, w_ref[...], preferred_element_type=F32))
        row = i * tr + lax.broadcasted_iota(jnp.int32, (tr, 1), 0)
        live = jnp.logical_and(row >= N_META, row < n_real)
        tgt = t_ref[...]
        tgt = jnp.where(i == 0, pltpu.roll(tgt, N_META, 0), tgt)
        err = jnp.where(live, xn * gv - tgt, 0.0)
        loss_ref[...] += (0.5 / D_MODEL) * jnp.sum(jnp.sum(err * err, axis=1, keepdims=True), axis=0, keepdims=True)
        dx, dg = _rms_bwd(err * (1.0 / D_MODEL), xn, r, gv)
        dg_ref[...] += dg
        dh_ref[...] = dx
        dhb = dx.astype(BF16)
        dw_ref[...] += lax.dot_general(y, dhb, TN, preferred_element_type=F32)
        dy = lax.dot_general(dhb, w_ref[...], NT, preferred_element_type=F32)
        da_ref[...] = dy * sg
        dgate_ref[...] = (dy * av * _dsilu(gatev)).astype(BF16)

    blk = pl.BlockSpec((tr, D_MODEL), lambda i: (i, 0))
    wblk = pl.BlockSpec((D_MODEL, D_MODEL), lambda i: (0, 0))
    window = pl.BlockSpec((pl.Element(tr, (0, rows - n_real)), pl.Element(D_MODEL)),
                          lambda i: (pl.multiple_of(jnp.maximum(i * tr - N_META, 0), SUBLANES), 0))
    return pl.pallas_call(
        body, name="b_out_loss", grid=(rows // tr,),
        in_specs=[blk, blk, blk, wblk, pl.BlockSpec((1, D_MODEL), lambda i: (0, 0)), window],
        out_specs=[blk, pl.BlockSpec((1, 1), lambda i: (0, 0)), pl.BlockSpec((1, D_MODEL), lambda i: (0, 0)),
                   blk, blk, wblk],
        out_shape=[jax.ShapeDtypeStruct((rows, D_MODEL), F32), jax.ShapeDtypeStruct((1, 1), F32),
                   jax.ShapeDtypeStruct((1, D_MODEL), F32), jax.ShapeDtypeStruct((rows, D_MODEL), F32),
                   jax.ShapeDtypeStruct((rows, D_MODEL), BF16), jax.ShapeDtypeStruct((D_MODEL, D_MODEL), F32)],
        compiler_params=_cparams(("arbitrary",)),
    )(a, gate, h, w, gf, target)


def _my_place():
    x, y, c = lax.axis_index("x"), lax.axis_index("y"), lax.axis_index("c")
    return x, y, c, 4 * x + 2 * y + c


def _peer(x, y, c, k):
    px, py, pc = x ^ (k >> 2), y ^ ((k >> 1) & 1), c ^ (k & 1)
    return (px, py, pc), 4 * px + 2 * py + pc


def _exchange_copies(src_of, dst_ref, send_sems, recv_sems, local_sem):
    x, y, c, me = _my_place()
    copies = [pltpu.make_async_copy(src_of(me), dst_ref.at[me], local_sem)]
    for k in range(1, N_DEV):
        peer, pid = _peer(x, y, c, k)
        copies.append(pltpu.make_async_remote_copy(
            src_ref=src_of(pid), dst_ref=dst_ref.at[me], send_sem=send_sems.at[k], recv_sem=recv_sems.at[k],
            device_id=peer, device_id_type=MESH))
    return copies


def _exchange_sems(nb):
    return [pltpu.SemaphoreType.DMA((nb, N_DEV)), pltpu.SemaphoreType.DMA((nb, N_DEV)), pltpu.SemaphoreType.DMA((nb,))]


def _sum_blocks(lands, name):
    n = len(lands)

    def body(*refs):
        for land_ref, out_ref in zip(refs[:n], refs[n:]):
            acc = land_ref[0].astype(F32)
            for d in range(1, N_DEV):
                acc = acc + land_ref[d].astype(F32)
            out_ref[...] = acc

    return pl.pallas_call(
        body, name=name, out_shape=[jax.ShapeDtypeStruct(l.shape[1:], F32) for l in lands],
        compiler_params=pltpu.CompilerParams(vmem_limit_bytes=VMEM_LIMIT),
    )(*lands)


def _all_gather(big, small):
    def body(big_ref, small_ref, obig_ref, osmall_ref, send_sems, recv_sems, local_sems):
        x, y, c, me = _my_place()
        own = [pltpu.make_async_copy(big_ref, obig_ref.at[me], local_sems.at[0]),
               pltpu.make_async_copy(small_ref, osmall_ref.at[me], local_sems.at[1])]
        for cp in own:
            cp.start()
        copies = []
        for k in range(1, N_DEV):
            peer, _ = _peer(x, y, c, k)
            for part, (src, dst) in enumerate(((big_ref, obig_ref), (small_ref, osmall_ref))):
                copies.append(pltpu.make_async_remote_copy(
                    src_ref=src, dst_ref=dst.at[me], send_sem=send_sems.at[part, k], recv_sem=recv_sems.at[part, k],
                    device_id=peer, device_id_type=MESH))
        for cp in copies:
            cp.start()
        for cp in copies:
            cp.wait()
        for cp in own:
            cp.wait()

    n = big.shape[0]
    hbm = pl.BlockSpec(memory_space=pl.ANY)
    return pl.pallas_call(
        body, name="weight_all_gather",
        in_specs=[hbm, hbm], out_specs=[hbm, hbm],
        out_shape=[jax.ShapeDtypeStruct((N_DEV, n, LANES), BF16), jax.ShapeDtypeStruct((N_DEV,) + small.shape, F32)],
        scratch_shapes=[pltpu.SemaphoreType.DMA((2, N_DEV)), pltpu.SemaphoreType.DMA((2, N_DEV)),
                        pltpu.SemaphoreType.DMA((2,))],
        compiler_params=pltpu.CompilerParams(has_side_effects=True),
    )(big, small)


GRAD_CHUNK = 32


def _grad_exchange(gbig, rep):
    n = gbig.shape[1]
    nrep = rep.shape[0]

    def body(gbig_ref, rep_ref, out_ref, orep_ref, land, land_rep, send_sems, recv_sems, local_sems):
        x, y, c, me = _my_place()
        own = [pltpu.make_async_copy(gbig_ref.at[me], land.at[me], local_sems.at[0]),
               pltpu.make_async_copy(rep_ref, land_rep.at[me], local_sems.at[1])]
        for cp in own:
            cp.start()
        copies = []
        for k in range(1, N_DEV):
            peer, pid = _peer(x, y, c, k)
            copies.append(pltpu.make_async_remote_copy(
                src_ref=gbig_ref.at[pid], dst_ref=land.at[me], send_sem=send_sems.at[0, k],
                recv_sem=recv_sems.at[0, k], device_id=peer, device_id_type=MESH))
            copies.append(pltpu.make_async_remote_copy(
                src_ref=rep_ref, dst_ref=land_rep.at[me], send_sem=send_sems.at[1, k],
                recv_sem=recv_sems.at[1, k], device_id=peer, device_id_type=MESH))
        for cp in copies:
            cp.start()
        for cp in copies:
            cp.wait()
        for cp in own:
            cp.wait()

        def chunk(ci, carry):
            rs = pl.ds(pl.multiple_of(ci * GRAD_CHUNK, GRAD_CHUNK), GRAD_CHUNK)
            acc = land[0, rs, :].astype(F32)
            for d in range(1, N_DEV):
                acc = acc + land[d, rs, :].astype(F32)
            out_ref[rs, :] = acc
            return carry

        lax.fori_loop(0, n // GRAD_CHUNK, chunk, 0)
        acc = land_rep[0]
        for d in range(1, N_DEV):
            acc = acc + land_rep[d]
        orep_ref[...] = acc

    return pl.pallas_call(
        body, name="grad_exchange",
        in_specs=[pl.BlockSpec(memory_space=pl.ANY), pl.BlockSpec(memory_space=pl.ANY)],
        out_specs=[pl.BlockSpec(memory_space=pltpu.VMEM), pl.BlockSpec(memory_space=pltpu.VMEM)],
        out_shape=[jax.ShapeDtypeStruct((n, LANES), F32), jax.ShapeDtypeStruct((nrep, LANES), F32)],
        scratch_shapes=[pltpu.VMEM((N_DEV, n, LANES), BF16), pltpu.VMEM((N_DEV, nrep, LANES), F32),
                        pltpu.SemaphoreType.DMA((2, N_DEV)), pltpu.SemaphoreType.DMA((2, N_DEV)),
                        pltpu.SemaphoreType.DMA((2,))],
        compiler_params=pltpu.CompilerParams(vmem_limit_bytes=VMEM_LIMIT, has_side_effects=True),
    )(gbig, rep)


def _adamw_all(ws, gs, ms, vs):
    n = len(ws)

    def body(*refs):
        w_refs, g_refs, m_refs, v_refs = refs[0:n], refs[n:2 * n], refs[2 * n:3 * n], refs[3 * n:4 * n]
        d_refs, nm_refs, nv_refs = refs[4 * n:5 * n], refs[5 * n:6 * n], refs[6 * n:7 * n]
        for w_ref, g_ref, m_ref, v_ref, d_ref, nm_ref, nv_ref in zip(w_refs, g_refs, m_refs, v_refs, d_refs, nm_refs, nv_refs):
            g = g_ref[...]
            m = ADAM_B1 * m_ref[...] + (1.0 - ADAM_B1) * g
            v = ADAM_B2 * v_ref[...] + (1.0 - ADAM_B2) * jnp.square(g)
            m_hat = m / (1.0 - ADAM_B1 ** ADAM_STEP)
            v_hat = v / (1.0 - ADAM_B2 ** ADAM_STEP)
            d_ref[...] = -ADAM_LR * (m_hat / (jnp.sqrt(v_hat) + ADAM_EPS) + ADAM_WD * w_ref[...])
            nm_ref[...] = m
            nv_ref[...] = v

    shapes = [jax.ShapeDtypeStruct(w.shape, F32) for w in ws]
    outs = pl.pallas_call(
        body, name="adamw", out_shape=shapes * 3,
        compiler_params=pltpu.CompilerParams(vmem_limit_bytes=VMEM_LIMIT),
    )(*ws, *gs, *ms, *vs)
    return outs[0:n], outs[n:2 * n], outs[2 * n:3 * n]


BIG_A0 = (("a_w_in", 1728),)
SMALL_A = (("meta_tokens", 16),)
SMALL_B = (("b_norm_g", 1), ("b_conv_w", 4), ("b_conv_b", 1), ("b_b_rg", 1), ("b_b_ig", 1), ("b_lam", 1))
REP = (("a_norm_g", 8), ("a_q_norm_g", 3), ("a_kv_norm_g", 2), ("final_norm_g", 8), ("loss", 1))
SLOT = 16


def _offsets(table, slot=1, start=0):
    out, o = {}, start
    for name, n in table:
        out[name] = (o, n)
        o += -(-n // slot) * slot
    return out, o


def _slotted(a, axis):
    pad = -a.shape[axis] % SLOT
    if not pad:
        return a
    widths = [(0, 0)] * a.ndim
    widths[axis] = (0, pad)
    return jnp.pad(a, widths)


def _rope_tables(rows):
    pos = np.arange(rows, dtype=np.float32)
    inv_freq = (np.float32(ROPE_BASE) ** (-np.arange(0, QK_ROPE, 2, dtype=np.float32) / np.float32(QK_ROPE))).astype(
        np.float32)
    ang = pos[:, None] * inv_freq[None, :]
    cos, sin = np.cos(ang).astype(np.float32), np.sin(ang).astype(np.float32)
    zeros = np.zeros((rows, LANES - QK_ROPE), np.float32)
    return jnp.asarray(np.concatenate([cos, cos, zeros], axis=1)), jnp.asarray(np.concatenate([-sin, sin, zeros], axis=1))


def kernel(x, meta_tokens, a_norm_g, a_w_in, a_q_norm_g, a_kv_norm_g, a_w_uq, a_w_ukv, a_w_out, b_norm_g, b_w_in, b_conv_w, b_conv_b, b_w_rg, b_b_rg, b_w_ig, b_b_ig, b_lam, b_w_out, final_norm_g, loss_target, m_meta_tokens, m_a_norm_g, m_a_w_in, m_a_q_norm_g, m_a_kv_norm_g, m_a_w_uq, m_a_w_ukv, m_a_w_out, m_b_norm_g, m_b_w_in, m_b_conv_w, m_b_conv_b, m_b_w_rg, m_b_b_rg, m_b_w_ig, m_b_b_ig, m_b_lam, m_b_w_out, m_final_norm_g, v_meta_tokens, v_a_norm_g, v_a_w_in, v_a_q_norm_g, v_a_kv_norm_g, v_a_w_uq, v_a_w_ukv, v_a_w_out, v_b_norm_g, v_b_w_in, v_b_conv_w, v_b_conv_b, v_b_w_rg, v_b_b_rg, v_b_w_ig, v_b_b_ig, v_b_lam, v_b_w_out, v_final_norm_g):
    names = ("meta_tokens", "a_norm_g", "a_w_in", "a_q_norm_g", "a_kv_norm_g", "a_w_uq", "a_w_ukv", "a_w_out",
             "b_norm_g", "b_w_in", "b_conv_w", "b_conv_b", "b_w_rg", "b_b_rg", "b_w_ig", "b_b_ig", "b_lam", "b_w_out",
             "final_norm_g")
    w = dict(zip(names, (meta_tokens, a_norm_g, a_w_in, a_q_norm_g, a_kv_norm_g, a_w_uq, a_w_ukv, a_w_out, b_norm_g,
                         b_w_in, b_conv_w, b_conv_b, b_w_rg, b_b_rg, b_w_ig, b_b_ig, b_lam, b_w_out, final_norm_g)))
    mom_m = dict(zip(names, (m_meta_tokens, m_a_norm_g, m_a_w_in, m_a_q_norm_g, m_a_kv_norm_g, m_a_w_uq, m_a_w_ukv,
                             m_a_w_out, m_b_norm_g, m_b_w_in, m_b_conv_w, m_b_conv_b, m_b_w_rg, m_b_b_rg, m_b_w_ig,
                             m_b_b_ig, m_b_lam, m_b_w_out, m_final_norm_g)))
    mom_v = dict(zip(names, (v_meta_tokens, v_a_norm_g, v_a_w_in, v_a_q_norm_g, v_a_kv_norm_g, v_a_w_uq, v_a_w_ukv,
                             v_a_w_out, v_b_norm_g, v_b_w_in, v_b_conv_w, v_b_conv_b, v_b_w_rg, v_b_b_rg, v_b_w_ig,
                             v_b_b_ig, v_b_lam, v_b_w_out, v_final_norm_g)))

    seq = x.shape[1]
    n_real = N_META + seq
    rows = -(-n_real // LANES) * LANES
    scale = (QK_NOPE + QK_ROPE) ** -0.5
    biga0_off, biga0_rows = _offsets(BIG_A0)
    small_off, _ = _offsets(SMALL_A + SMALL_B, SLOT)
    gsmalla_off, grada_rows = _offsets(SMALL_A, SLOT, biga0_rows)
    gsmallb_off, _ = _offsets(SMALL_B, SLOT)
    grada_rows = -(-grada_rows // GRAD_CHUNK) * GRAD_CHUNK
    rep_off, _ = _offsets(REP, SLOT)

    send_a0 = w["a_w_in"].reshape(-1, LANES).astype(BF16)
    send_small = jnp.concatenate([_slotted(w[nm].reshape(-1, LANES), 0) for nm, _ in SMALL_A + SMALL_B], axis=0)
    sends_a1 = [jnp.pad(a_w_uq[0], ((0, 0), (0, HEAD_PAD - QK_NOPE - QK_ROPE))).astype(BF16), a_w_ukv[0].astype(BF16)]
    lru_rows = LRU_BLOCKS * LRU_BLOCK // N_DEV
    sends_b = [a_w_out[0].astype(BF16), b_w_in[0].astype(BF16), b_w_rg.reshape(lru_rows, LRU_BLOCK).astype(BF16),
               b_w_ig.reshape(lru_rows, LRU_BLOCK).astype(BF16), b_w_out[0].astype(BF16)]
    all_a0, all_small = _all_gather(send_a0, send_small)

    def small_seg(nm):
        o, n = small_off[nm]
        return all_small[:, o:o + n, :]

    cdev_a = a_w_in.shape[-1]
    w_in_a = all_a0.reshape(N_DEV, D_MODEL, cdev_a).transpose(1, 0, 2).reshape(D_MODEL, N_DEV * cdev_a)
    w_in_a = jnp.concatenate([w_in_a[:, :LAT + QK_ROPE], jnp.zeros((D_MODEL, LAT_PAD - LAT - QK_ROPE), BF16),
                              w_in_a[:, LAT + QK_ROPE:]], axis=1)[None]
    meta_full = small_seg("meta_tokens").transpose(1, 0, 2).reshape(N_META, D_MODEL)
    vec = lambda nm: small_seg(nm).reshape(1, D_MODEL)
    g_b, conv_b, b_rg, b_ig, lam = vec("b_norm_g"), vec("b_conv_b"), vec("b_b_rg"), vec("b_b_ig"), vec("b_lam")
    conv_w = small_seg("b_conv_w").transpose(1, 0, 2).reshape(CONV_WIDTH, LRU_WIDTH)
    g_a, g_q, g_kv = a_norm_g, a_q_norm_g, a_kv_norm_g
    g_f = final_norm_g.reshape(1, D_MODEL)

    h0 = jnp.concatenate([meta_full, x[0], jnp.zeros((rows - n_real, D_MODEL), F32)], axis=0)
    cos, sin = _rope_tables(rows)

    lat, gate_a, w_uq, w_ukv = _norm_proj_fwd(h0, g_a, w_in_a, LAT_PAD, "a_in_fwd", sends_a1)
    qc, kc, v, vt = _mla_qkv_fwd(lat, g_q, g_kv, w_uq, w_ukv, cos, sin, scale)
    o, lse, w_out_a, w_in_b, w_rg, w_ig, w_out_b = _attn_fwd(qc, kc, vt, sends_b)

    lru_w = lambda g: g.reshape(N_DEV, LRU_BLOCKS, LRU_BLOCK // N_DEV, LRU_BLOCK).transpose(1, 0, 2, 3).reshape(
        LRU_BLOCKS, LRU_BLOCK, LRU_BLOCK)
    w_out_a, w_out_b = w_out_a.reshape(D_MODEL, D_MODEL), w_out_b.reshape(D_MODEL, D_MODEL)
    w_rg, w_ig = lru_w(w_rg), lru_w(w_ig)

    h1, u, gate_b = _out_proj_in_proj(o, gate_a, h0, w_out_a, g_b, w_in_b, LRU_WIDTH, "a_out_b_in_fwd")
    hs = _rglru_fwd(u, conv_w, conv_b, w_rg, b_rg, w_ig, b_ig, lam)
    dh2, loss_part, dg_f, dhs, dgate_b, dw_out_b = _out_proj_loss(hs, gate_b, h1, w_out_b, g_f, loss_target[0],
                                                                   n_real)

    du, dconv_w, dconv_b, dw_rg, db_rg, dw_ig, db_ig, dlam = _rglru_bwd(u, hs, dhs, conv_w, conv_b, w_rg, b_rg, w_ig,
                                                                       b_ig, lam)
    dh1, dw_in_b, dg_b = _norm_proj_bwd(h1, g_b, w_in_b, du, dgate_b, dh2, "b_in_bwd")
    do, dgate_a, dw_out_a, delta = _gated_out_bwd(o, gate_a, dh1, w_out_a, BF16, True, "a_out_bwd")

    def to_cols(g, cdev):
        r = g.shape[0]
        return g.reshape(r, N_DEV, cdev).transpose(1, 0, 2).reshape(N_DEV, -1, LANES)

    lru_g = lambda g: g.reshape(LRU_BLOCKS, N_DEV, LRU_BLOCK // N_DEV, LRU_BLOCK).transpose(1, 0, 2, 3).reshape(
        N_DEV, lru_rows, LRU_BLOCK)
    small_b = {"b_norm_g": dg_b, "b_conv_w": dconv_w, "b_conv_b": dconv_b, "b_b_rg": db_rg, "b_b_ig": db_ig,
               "b_lam": dlam}
    gsends_b = [dw_out_a.reshape(N_DEV, -1, D_MODEL).astype(BF16), dw_in_b.astype(BF16),
                lru_g(dw_rg).astype(BF16), lru_g(dw_ig).astype(BF16),
                dw_out_b.reshape(N_DEV, -1, D_MODEL).astype(BF16),
                jnp.concatenate([_slotted(to_cols(small_b[nm], LANES).astype(BF16), 1) for nm, _ in SMALL_B], axis=1)]

    dqc, dkc, dv, *lands_b = _attn_bwd(qc, kc, v, lse, delta, do, gsends_b)
    g_out_a, g_in_b, g_rg, g_ig, g_out_b, gsum_small_b = _sum_blocks(lands_b, "sum_blocks_b")
    dlat, dw_uq, dw_ukv, dg_q, dg_kv = _mla_qkv_bwd(lat, g_q, g_kv, w_uq, w_ukv, cos, sin, dqc, dkc, dv, scale)
    dh0, dw_in_a, dg_a, *lands_a1 = _norm_proj_bwd(h0, g_a, w_in_a, dlat, dgate_a, dh1, "a_in_bwd",
                                                   [dw_uq.astype(BF16), dw_ukv.astype(BF16)])
    g_uq, g_ukv = _sum_blocks(lands_a1, "sum_blocks_a1")

    grad_x = dh0[N_META:n_real][None]

    dw_in_a_nat = jnp.concatenate([dw_in_a[0, :, :LAT + QK_ROPE], dw_in_a[0, :, LAT_PAD:]], axis=1)
    pieces = [to_cols(dw_in_a_nat, cdev_a).astype(BF16), to_cols(dh0[:N_META], LANES).astype(BF16)]
    pieces.append(jnp.zeros((N_DEV, grada_rows - sum(p.shape[1] for p in pieces), LANES), BF16))
    gsend_a0 = jnp.concatenate(pieces, axis=1)
    rep_parts = {"a_norm_g": dg_a, "a_q_norm_g": dg_q, "a_kv_norm_g": dg_kv, "final_norm_g": dg_f,
                 "loss": jnp.broadcast_to(loss_part, (1, LANES))}
    rep = jnp.concatenate([_slotted(rep_parts[nm].reshape(-1, LANES), 0) for nm, _ in REP], axis=0)
    gsum_a0, rep_sum = _grad_exchange(gsend_a0, rep)

    grads = {"a_w_out": g_out_a, "b_w_in": g_in_b, "b_w_rg": g_rg, "b_w_ig": g_ig, "b_w_out": g_out_b,
             "a_w_uq": g_uq[:, :QK_NOPE + QK_ROPE], "a_w_ukv": g_ukv}
    grads = {nm: g.reshape(w[nm].shape) for nm, g in grads.items()}
    for off, src in ((biga0_off, gsum_a0), (gsmalla_off, gsum_a0), (gsmallb_off, gsum_small_b), (rep_off, rep_sum)):
        for nm, (o_r, n) in off.items():
            if nm in w:
                grads[nm] = src[o_r:o_r + n].reshape(w[nm].shape)
    loss = rep_sum[rep_off["loss"][0], 0]

    as2d = lambda a: a.reshape(1, -1) if a.ndim == 1 else a
    deltas, new_ms, new_vs = _adamw_all([as2d(w[nm]) for nm in names], [as2d(grads[nm]) for nm in names],
                                        [as2d(mom_m[nm]) for nm in names], [as2d(mom_v[nm]) for nm in names])
    shaped = lambda arrs: [a.reshape(w[nm].shape) for a, nm in zip(arrs, names)]
    return (loss, grad_x, *[grads[nm] for nm in names], *shaped(deltas), *shaped(new_ms), *shaped(new_vs))
```

```python
import functools

import numpy as np
import jax
import jax.numpy as jnp
from jax import lax
from jax.experimental import pallas as pl
from jax.experimental.pallas import tpu as pltpu

F32 = jnp.float32
BF16 = jnp.bfloat16

D_MODEL = 1024
N_META = 16
RMS_EPS = 1e-6
HEADS = 8
QK_NOPE = 128
QK_ROPE = 64
V_HEAD = 128
Q_LORA = 384
KV_LORA = 256
HEAD_PAD = 256
LAT = Q_LORA + KV_LORA
LAT_PAD = LAT + 128
ROPE_BASE = 10000.0
MASK_VALUE = -1e30
LRU_WIDTH = 1024
LRU_BLOCKS = 4
LRU_BLOCK = 256
CONV_WIDTH = 4
LRU_C = 8.0
N_DEV = 8
ADAM_LR, ADAM_B1, ADAM_B2, ADAM_EPS, ADAM_WD, ADAM_STEP = 0.001, 0.9, 0.999, 1e-08, 0.01, 10

LANES = 128
SUBLANES = 8
VMEM_LIMIT = 56 * 1024 * 1024
MESH = pl.DeviceIdType.MESH

NT = (((1,), (1,)), ((), ()))
TN = (((0,), (0,)), ((), ()))


def _row_block(rows):
    return 384 if rows % 384 == 0 else 128


def _cparams(sem):
    return pltpu.CompilerParams(dimension_semantics=sem, vmem_limit_bytes=VMEM_LIMIT)


def _silu(x):
    return x * jax.nn.sigmoid(x)


def _dsilu(x):
    s = jax.nn.sigmoid(x)
    return s * (1.0 + x * (1.0 - s))


def _rms_fwd(x):
    r = lax.rsqrt(jnp.mean(x * x, axis=-1, keepdims=True) + RMS_EPS)
    return x * r, r


def _rms_bwd(dy, xn, r, g):
    t = dy * g
    dx = r * (t - xn * jnp.mean(t * xn, axis=-1, keepdims=True))
    return dx, jnp.sum(dy * xn, axis=0, keepdims=True)


def _expm1_neg(x):
    small = x * (1.0 + x * (1 / 2 + x * (1 / 6 + x * (1 / 24))))
    return jnp.where(x > -0.05, small, jnp.exp(x) - 1.0)


def _softplus_neg(lam):
    z = jnp.exp(-jnp.abs(lam))
    w = z / (2.0 + z)
    w2 = w * w
    series = 2.0 * w * (1.0 + w2 * (1 / 3) + w2 * w2 * (1 / 5))
    return jnp.maximum(-lam, 0.0) + jnp.where(z < 0.1, series, jnp.log(1.0 + z))


def _rider(sends, refs, first, last, all_to_all):
    nb = len(sends)
    if not nb:
        return (lambda: None), (lambda: None)
    send_refs, result_refs = refs[:nb], refs[nb:2 * nb]
    send_sems, recv_sems, local_sems = refs[2 * nb:]
    pick = (lambda ref: (lambda d: ref.at[d])) if all_to_all else (lambda ref: (lambda d: ref))

    def copies():
        out = []
        for b in range(nb):
            out += _exchange_copies(pick(send_refs[b]), result_refs[b], send_sems.at[b], recv_sems.at[b],
                                    local_sems.at[b])
        return out

    def start():
        @pl.when(first)
        def _():
            for cp in copies():
                cp.start()

    def wait():
        @pl.when(last)
        def _():
            for cp in copies():
                cp.wait()

    return start, wait


def _rider_specs(sends, all_to_all):
    nb = len(sends)
    if not nb:
        return [], [], [], []
    hbm = pl.BlockSpec(memory_space=pl.ANY)
    shapes = [jax.ShapeDtypeStruct(s.shape if all_to_all else (N_DEV,) + s.shape, s.dtype) for s in sends]
    return [hbm] * nb, [hbm] * nb, shapes, _exchange_sems(nb)


def _proj_blocks(x, w_ref):
    return jnp.concatenate([jnp.dot(x, w_ref[d], preferred_element_type=F32) for d in range(w_ref.shape[0])], axis=1)


def _norm_proj_fwd(h, g, w, n1, name, wsends=()):
    rows = h.shape[0]
    nb, _, cb = w.shape
    n = nb * cb
    tr = _row_block(rows)
    nsteps = rows // tr
    extra = len(wsends)

    def body(h_ref, g_ref, w_ref, *rest):
        p1_ref, p2_ref = rest[extra], rest[extra + 1]
        i = pl.program_id(0)
        start, wait = _rider(wsends, rest[:extra] + rest[extra + 2:], i == 0, i == nsteps - 1, False)
        start()
        xn, _ = _rms_fwd(h_ref[...])
        p = _proj_blocks((xn * g_ref[...]).astype(BF16), w_ref)
        p1_ref[...] = p[:, :n1]
        p2_ref[...] = p[:, n1:]
        wait()

    r_in, r_out, r_shape, r_scratch = _rider_specs(wsends, False)
    return pl.pallas_call(
        body, name=name, grid=(nsteps,),
        in_specs=[pl.BlockSpec((tr, D_MODEL), lambda i: (i, 0)),
                  pl.BlockSpec((1, D_MODEL), lambda i: (0, 0)),
                  pl.BlockSpec((nb, D_MODEL, cb), lambda i: (0, 0, 0))] + r_in,
        out_specs=[pl.BlockSpec((tr, n1), lambda i: (i, 0)),
                   pl.BlockSpec((tr, n - n1), lambda i: (i, 0))] + r_out,
        out_shape=[jax.ShapeDtypeStruct((rows, n1), F32), jax.ShapeDtypeStruct((rows, n - n1), F32)] + r_shape,
        scratch_shapes=r_scratch,
        compiler_params=_cparams(("arbitrary",)),
    )(h, g, w, *wsends)


def _out_proj_in_proj(a, gate, h, w_out, g, w_in, n1, name):
    rows = h.shape[0]
    nb, _, cb = w_in.shape
    n = nb * cb
    tr = _row_block(rows)

    def body(a_ref, gate_ref, h_ref, wo_ref, g_ref, wi_ref, hn_ref, p1_ref, p2_ref):
        y = (a_ref[...] * _silu(gate_ref[...])).astype(BF16)
        h_new = h_ref[...] + jnp.dot(y, wo_ref[...], preferred_element_type=F32)
        hn_ref[...] = h_new
        xn, _ = _rms_fwd(h_new)
        p = _proj_blocks((xn * g_ref[...]).astype(BF16), wi_ref)
        p1_ref[...] = p[:, :n1]
        p2_ref[...] = p[:, n1:]

    blk = pl.BlockSpec((tr, D_MODEL), lambda i: (i, 0))
    return pl.pallas_call(
        body, name=name, grid=(rows // tr,),
        in_specs=[blk, blk, blk, pl.BlockSpec((D_MODEL, D_MODEL), lambda i: (0, 0)),
                  pl.BlockSpec((1, D_MODEL), lambda i: (0, 0)), pl.BlockSpec((nb, D_MODEL, cb), lambda i: (0, 0, 0))],
        out_specs=[blk, pl.BlockSpec((tr, n1), lambda i: (i, 0)), pl.BlockSpec((tr, n - n1), lambda i: (i, 0))],
        out_shape=[jax.ShapeDtypeStruct((rows, D_MODEL), F32), jax.ShapeDtypeStruct((rows, n1), F32),
                   jax.ShapeDtypeStruct((rows, n - n1), F32)],
        compiler_params=_cparams(("parallel",)),
    )(a, gate, h, w_out, g, w_in)


def _norm_proj_bwd(h, g, w, dp1, dp2, dh_in, name, gsends=()):
    rows = h.shape[0]
    nb, _, cb = w.shape
    n1 = dp1.shape[1]
    n2 = nb * cb - n1
    tr = _row_block(rows)
    nsteps = rows // tr
    extra = len(gsends)

    def body(h_ref, g_ref, w_ref, dp1_ref, dp2_ref, dhin_ref, *rest):
        dh_ref, dw_ref, dg_ref = rest[extra:extra + 3]
        dw_acc = rest[2 * extra + 3]
        i = pl.program_id(0)
        start, wait = _rider(gsends, rest[:extra] + rest[extra + 3:2 * extra + 3] + rest[2 * extra + 4:],
                             i == 0, i == nsteps - 1, True)
        start()

        @pl.when(i == 0)
        def _():
            dw_acc[...] = jnp.zeros_like(dw_acc)
            dg_ref[...] = jnp.zeros_like(dg_ref)

        gv = g_ref[...]
        xn, r = _rms_fwd(h_ref[...])
        hn = (xn * gv).astype(BF16)
        dp = jnp.concatenate([dp1_ref[...].astype(BF16), dp2_ref[...].astype(BF16)], axis=1)
        dhn = jnp.zeros((tr, D_MODEL), F32)
        for d in range(nb):
            dpd = dp[:, d * cb:(d + 1) * cb]
            dw_acc[d] += lax.dot_general(hn, dpd, TN, preferred_element_type=F32)
            dhn = dhn + lax.dot_general(dpd, w_ref[d], NT, preferred_element_type=F32)
        dx, dg = _rms_bwd(dhn, xn, r, gv)
        dg_ref[...] += dg
        dh_ref[...] = dhin_ref[...] + dx

        @pl.when(i == nsteps - 1)
        def _():
            dw_ref[...] = dw_acc[...].astype(BF16)

        wait()

    r_in, r_out, r_shape, r_scratch = _rider_specs(gsends, True)
    wblk = pl.BlockSpec((nb, D_MODEL, cb), lambda i: (0, 0, 0))
    return pl.pallas_call(
        body, name=name, grid=(nsteps,),
        in_specs=[pl.BlockSpec((tr, D_MODEL), lambda i: (i, 0)),
                  pl.BlockSpec((1, D_MODEL), lambda i: (0, 0)),
                  wblk,
                  pl.BlockSpec((tr, n1), lambda i: (i, 0)),
                  pl.BlockSpec((tr, n2), lambda i: (i, 0)),
                  pl.BlockSpec((tr, D_MODEL), lambda i: (i, 0))] + r_in,
        out_specs=[pl.BlockSpec((tr, D_MODEL), lambda i: (i, 0)), wblk,
                   pl.BlockSpec((1, D_MODEL), lambda i: (0, 0))] + r_out,
        out_shape=[jax.ShapeDtypeStruct((rows, D_MODEL), F32),
                   jax.ShapeDtypeStruct((nb, D_MODEL, cb), BF16),
                   jax.ShapeDtypeStruct((1, D_MODEL), F32)] + r_shape,
        scratch_shapes=[pltpu.VMEM((nb, D_MODEL, cb), F32)] + r_scratch,
        compiler_params=_cparams(("arbitrary",)),
    )(h, g, w, dp1, dp2, dh_in, *gsends)


def _attn_out_bwd(o, gate, dh, w):
    rows = o.shape[0]
    tr = _row_block(rows)
    nsteps = rows // tr

    def body(o_ref, gate_ref, dh_ref, w_ref, do_ref, dgate_ref, dw_ref, delta_ref, dw_acc):
        i = pl.program_id(0)

        @pl.when(i == 0)
        def _():
            dw_acc[...] = jnp.zeros_like(dw_acc)

        ov, gv = o_ref[...], gate_ref[...]
        sg = _silu(gv)
        dhb = dh_ref[...].astype(BF16)
        dw_acc[...] += lax.dot_general((ov * sg).astype(BF16), dhb, TN, preferred_element_type=F32)
        dy = lax.dot_general(dhb, w_ref[...], NT, preferred_element_type=F32)
        do = (dy * sg).astype(BF16)
        do_ref[...] = do
        dgate_ref[...] = (dy * ov * _dsilu(gv)).astype(BF16)
        prod = do.astype(F32) * ov
        lane = lax.broadcasted_iota(jnp.int32, (tr, LANES), 1)
        per_head = jnp.zeros((tr, LANES), F32)
        for hd in range(HEADS):
            dsum = jnp.sum(prod[:, hd * V_HEAD:(hd + 1) * V_HEAD], axis=1, keepdims=True)
            per_head = jnp.where(lane == hd, dsum, per_head)
        delta_t = per_head.T
        for hd in range(HEADS):
            delta_ref[hd, 0] = delta_t[hd:hd + 1, :]

        @pl.when(i == nsteps - 1)
        def _():
            dw_ref[...] = dw_acc[...].astype(BF16)

    blk = pl.BlockSpec((tr, D_MODEL), lambda i: (i, 0))
    wblk = pl.BlockSpec((D_MODEL, D_MODEL), lambda i: (0, 0))
    return pl.pallas_call(
        body, name="a_out_bwd", grid=(nsteps,),
        in_specs=[blk, blk, blk, wblk],
        out_specs=[blk, blk, wblk, pl.BlockSpec((HEADS, 1, 1, tr), lambda i: (0, i, 0, 0))],
        out_shape=[jax.ShapeDtypeStruct((rows, D_MODEL), BF16), jax.ShapeDtypeStruct((rows, D_MODEL), BF16),
                   jax.ShapeDtypeStruct((D_MODEL, D_MODEL), BF16),
                   jax.ShapeDtypeStruct((HEADS, nsteps, 1, tr), F32)],
        scratch_shapes=[pltpu.VMEM((D_MODEL, D_MODEL), F32)],
        compiler_params=_cparams(("arbitrary",)),
    )(o, gate, dh, w)


def _rope(v, cos, sin, lane):
    swapped = jnp.where(lane < QK_ROPE // 2, pltpu.roll(v, LANES - QK_ROPE // 2, 1), pltpu.roll(v, QK_ROPE // 2, 1))
    return v * cos + swapped * sin


def _unrope(dv, cos, sin, lane):
    t = dv * sin
    swapped = jnp.where(lane < QK_ROPE // 2, pltpu.roll(t, LANES - QK_ROPE // 2, 1), pltpu.roll(t, QK_ROPE // 2, 1))
    return dv * cos + swapped


def _mla_qkv_fwd(lat, gq, gkv, wuq, wukv, cos, sin, scale):
    rows = lat.shape[0]
    tr = _row_block(rows)

    def body(lat_ref, gq_ref, gkv_ref, wuq_ref, wukv_ref, cos_ref, sin_ref, qc_ref, kc_ref, v_ref, vt_ref):
        qn, _ = _rms_fwd(lat_ref[:, :Q_LORA])
        kvn, _ = _rms_fwd(lat_ref[:, Q_LORA:LAT])
        qnb = (qn * gq_ref[...]).astype(BF16)
        kvnb = (kvn * gkv_ref[...]).astype(BF16)
        c, s = cos_ref[...], sin_ref[...]
        lane = lax.broadcasted_iota(jnp.int32, (tr, LANES), 1)
        kr = _rope(lat_ref[:, LAT:LAT_PAD], c, s, lane).astype(BF16)
        for hd in range(HEADS):
            o = hd * HEAD_PAD
            q = jnp.dot(qnb, wuq_ref[hd], preferred_element_type=F32)
            kv = jnp.dot(kvnb, wukv_ref[hd], preferred_element_type=F32)
            qc_ref[:, o:o + QK_NOPE] = (q[:, :QK_NOPE] * scale).astype(BF16)
            qc_ref[:, o + QK_NOPE:o + HEAD_PAD] = (_rope(q[:, QK_NOPE:], c, s, lane) * scale).astype(BF16)
            kc_ref[:, o:o + QK_NOPE] = kv[:, :QK_NOPE].astype(BF16)
            kc_ref[:, o + QK_NOPE:o + HEAD_PAD] = kr
            vh = kv[:, QK_NOPE:]
            v_ref[:, hd * V_HEAD:(hd + 1) * V_HEAD] = vh.astype(BF16)
            vt_ref[hd, 0] = vh.T.astype(BF16)

    full = lambda shape: pl.BlockSpec(shape, lambda i: (0,) * len(shape))
    rowb = lambda n: pl.BlockSpec((tr, n), lambda i: (i, 0))
    return pl.pallas_call(
        body, name="mla_qkv_fwd", grid=(rows // tr,),
        in_specs=[rowb(LAT_PAD), full((1, Q_LORA)), full((1, KV_LORA)), full((HEADS, Q_LORA, HEAD_PAD)),
                  full((HEADS, KV_LORA, HEAD_PAD)), rowb(LANES), rowb(LANES)],
        out_specs=[rowb(HEADS * HEAD_PAD), rowb(HEADS * HEAD_PAD), rowb(HEADS * V_HEAD),
                   pl.BlockSpec((HEADS, 1, V_HEAD, tr), lambda i: (0, i, 0, 0))],
        out_shape=[jax.ShapeDtypeStruct((rows, HEADS * HEAD_PAD), BF16),
                   jax.ShapeDtypeStruct((rows, HEADS * HEAD_PAD), BF16),
                   jax.ShapeDtypeStruct((rows, HEADS * V_HEAD), BF16),
                   jax.ShapeDtypeStruct((HEADS, rows // tr, V_HEAD, tr), BF16)],
        compiler_params=_cparams(("parallel",)),
    )(lat, gq, gkv, wuq, wukv, cos, sin)


def _mla_qkv_bwd(lat, gq, gkv, wuq, wukv, cos, sin, dqc, dkc, dv, scale):
    rows = lat.shape[0]
    tr = _row_block(rows)
    nsteps = rows // tr

    def body(lat_ref, gq_ref, gkv_ref, wuq_ref, wukv_ref, cos_ref, sin_ref, dqc_ref, dkc_ref, dv_ref,
             dlat_ref, dwuq_out, dwukv_out, dgq_ref, dgkv_ref, dwuq_ref, dwukv_ref):
        @pl.when(pl.program_id(0) == 0)
        def _():
            dwuq_ref[...] = jnp.zeros_like(dwuq_ref)
            dwukv_ref[...] = jnp.zeros_like(dwukv_ref)
            dgq_ref[...] = jnp.zeros_like(dgq_ref)
            dgkv_ref[...] = jnp.zeros_like(dgkv_ref)

        c, s = cos_ref[...], sin_ref[...]
        lane = lax.broadcasted_iota(jnp.int32, (tr, LANES), 1)
        gqv, gkvv = gq_ref[...], gkv_ref[...]
        qn, rq = _rms_fwd(lat_ref[:, :Q_LORA])
        kvn, rkv = _rms_fwd(lat_ref[:, Q_LORA:LAT])
        qnb = (qn * gqv).astype(BF16)
        kvnb = (kvn * gkvv).astype(BF16)
        dkr = jnp.zeros((tr, LANES), F32)
        dqn = jnp.zeros((tr, Q_LORA), F32)
        dkvn = jnp.zeros((tr, KV_LORA), F32)
        for hd in range(HEADS):
            o = hd * HEAD_PAD
            dq = jnp.concatenate(
                [dqc_ref[:, o:o + QK_NOPE],
                 _unrope(dqc_ref[:, o + QK_NOPE:o + HEAD_PAD].astype(F32), c, s, lane).astype(BF16)], axis=1)
            dkv = jnp.concatenate([dkc_ref[:, o:o + QK_NOPE], dv_ref[:, hd * V_HEAD:(hd + 1) * V_HEAD]], axis=1)
            dkr = dkr + dkc_ref[:, o + QK_NOPE:o + HEAD_PAD].astype(F32)
            dwuq_ref[hd] += scale * lax.dot_general(qnb, dq, TN, preferred_element_type=F32)
            dwukv_ref[hd] += lax.dot_general(kvnb, dkv, TN, preferred_element_type=F32)
            dqn = dqn + lax.dot_general(dq, wuq_ref[hd], NT, preferred_element_type=F32)
            dkvn = dkvn + lax.dot_general(dkv, wukv_ref[hd], NT, preferred_element_type=F32)
        dqn = scale * dqn
        dqlat, dgq = _rms_bwd(dqn, qn, rq, gqv)
        dkvlat, dgkv = _rms_bwd(dkvn, kvn, rkv, gkvv)
        dgq_ref[...] += dgq
        dgkv_ref[...] += dgkv
        dlat_ref[:, :Q_LORA] = dqlat.astype(BF16)
        dlat_ref[:, Q_LORA:LAT] = dkvlat.astype(BF16)
        dlat_ref[:, LAT:LAT_PAD] = _unrope(dkr, c, s, lane).astype(BF16)

        @pl.when(pl.program_id(0) == nsteps - 1)
        def _():
            dwuq_out[...] = dwuq_ref[...].astype(BF16)
            dwukv_out[...] = dwukv_ref[...].astype(BF16)

    full = lambda shape: pl.BlockSpec(shape, lambda i: (0,) * len(shape))
    rowb = lambda n: pl.BlockSpec((tr, n), lambda i: (i, 0))
    return pl.pallas_call(
        body, name="mla_qkv_bwd", grid=(nsteps,),
        in_specs=[rowb(LAT_PAD), full((1, Q_LORA)), full((1, KV_LORA)), full((HEADS, Q_LORA, HEAD_PAD)),
                  full((HEADS, KV_LORA, HEAD_PAD)), rowb(LANES), rowb(LANES),
                  rowb(HEADS * HEAD_PAD), rowb(HEADS * HEAD_PAD), rowb(HEADS * V_HEAD)],
        out_specs=[rowb(LAT_PAD), full((HEADS, Q_LORA, HEAD_PAD)), full((HEADS, KV_LORA, HEAD_PAD)),
                   full((1, Q_LORA)), full((1, KV_LORA))],
        out_shape=[jax.ShapeDtypeStruct((rows, LAT_PAD), BF16),
                   jax.ShapeDtypeStruct((HEADS, Q_LORA, HEAD_PAD), BF16),
                   jax.ShapeDtypeStruct((HEADS, KV_LORA, HEAD_PAD), BF16),
                   jax.ShapeDtypeStruct((1, Q_LORA), F32),
                   jax.ShapeDtypeStruct((1, KV_LORA), F32)],
        scratch_shapes=[pltpu.VMEM((HEADS, Q_LORA, HEAD_PAD), F32), pltpu.VMEM((HEADS, KV_LORA, HEAD_PAD), F32)],
        compiler_params=_cparams(("arbitrary",)),
    )(lat, gq, gkv, wuq, wukv, cos, sin, dqc, dkc, dv)


ATTN_UNROLL = 4


def _causal_mask_t(t):
    key = lax.broadcasted_iota(jnp.int32, (t, t), 0)
    query = lax.broadcasted_iota(jnp.int32, (t, t), 1)
    return key <= query


def _attn_fwd(qc, kc, vt, wsends):
    rows = qc.shape[0]
    t = _row_block(rows)
    nblk = rows // t
    nw = len(wsends)

    def body(q_ref, k_ref, vt_ref, *rest):
        o_ref, lse_ref = rest[nw:nw + 2]
        m_ref, l_ref, acc_ref, st_a, st_b = rest[2 * nw + 2:2 * nw + 7]
        i = pl.program_id(1)
        start, wait = _rider(wsends, rest[:nw] + rest[nw + 2:2 * nw + 2] + rest[2 * nw + 7:],
                             jnp.logical_and(pl.program_id(0) == 0, i == 0),
                             jnp.logical_and(pl.program_id(0) == HEADS - 1, i == nblk - 1), False)
        start()

        m_ref[...] = jnp.full_like(m_ref, MASK_VALUE)
        l_ref[...] = jnp.zeros_like(l_ref)
        acc_ref[...] = jnp.zeros_like(acc_ref)
        q = q_ref[...]

        def scores(j, st_ref):
            rs = pl.ds(pl.multiple_of(j * t, t), t)
            st_ref[...] = lax.dot_general(k_ref[rs, :], q, NT, preferred_element_type=F32)

        def consume(j, st_ref, masked):
            st = st_ref[...]
            if masked:
                st = jnp.where(_causal_mask_t(t), st, MASK_VALUE)
            m_prev = m_ref[...]
            m_new = jnp.maximum(m_prev, jnp.max(st, axis=0, keepdims=True))
            alpha = jnp.exp(m_prev - m_new)
            pt = jnp.exp(st - m_new)
            l_ref[...] = alpha * l_ref[...] + jnp.sum(pt, axis=0, keepdims=True)
            acc_ref[...] = alpha * acc_ref[...] + jnp.dot(vt_ref[0, j], pt.astype(BF16), preferred_element_type=F32)
            m_ref[...] = m_new

        bufs = (st_a, st_b)

        def step(j, parity, issue_next, masked):
            if issue_next:
                scores(j + 1, bufs[1 - parity])
            consume(j, bufs[parity], masked)

        scores(0, st_a)

        def trip(it, carry):
            for u in range(ATTN_UNROLL):
                step(it * ATTN_UNROLL + u, u % 2, True, False)
            return carry

        trips = i // ATTN_UNROLL
        lax.fori_loop(0, trips, trip, 0)
        j0 = trips * ATTN_UNROLL
        for left in range(1, ATTN_UNROLL + 1):
            @pl.when(i + 1 - j0 == left)
            def _(left=left):
                for u in range(left):
                    step(j0 + u, u % 2, u < left - 1, u == left - 1)

        o_ref[...] = (acc_ref[...] / l_ref[...]).T
        lse_ref[0, 0] = m_ref[...] + jnp.log(l_ref[...])
        wait()

    r_in, r_out, r_shape, r_scratch = _rider_specs(wsends, False)
    return pl.pallas_call(
        body, name="attn_fwd", grid=(HEADS, nblk),
        in_specs=[pl.BlockSpec((t, HEAD_PAD), lambda h, i: (i, h)),
                  pl.BlockSpec((rows, HEAD_PAD), lambda h, i: (0, h)),
                  pl.BlockSpec((1, nblk, V_HEAD, t), lambda h, i: (h, 0, 0, 0))] + r_in,
        out_specs=[pl.BlockSpec((t, V_HEAD), lambda h, i: (i, h)),
                   pl.BlockSpec((1, 1, 1, t), lambda h, i: (h, i, 0, 0))] + r_out,
        out_shape=[jax.ShapeDtypeStruct((rows, HEADS * V_HEAD), F32),
                   jax.ShapeDtypeStruct((HEADS, nblk, 1, t), F32)] + r_shape,
        scratch_shapes=[pltpu.VMEM((1, t), F32), pltpu.VMEM((1, t), F32), pltpu.VMEM((V_HEAD, t), F32),
                        pltpu.VMEM((t, t), F32), pltpu.VMEM((t, t), F32)] + r_scratch,
        compiler_params=_cparams(("arbitrary", "arbitrary")),
    )(qc, kc, vt, *wsends)


def _attn_bwd(qc, kc, v, lse, delta, do, gsends):
    rows = qc.shape[0]
    t = _row_block(rows)
    nblk = rows // t
    ng = len(gsends)

    def body(q_ref, k_ref, v_ref, lse_ref, delta_ref, do_ref, *rest):
        dq_ref, dk_ref, dv_ref = rest[ng:ng + 3]
        dq_acc, dk_acc, dv_acc, st_a, dp_a, st_b, dp_b = rest[2 * ng + 3:2 * ng + 10]
        j = pl.program_id(1)
        start, wait = _rider(gsends, rest[:ng] + rest[ng + 3:2 * ng + 3] + rest[2 * ng + 10:],
                             jnp.logical_and(pl.program_id(0) == 0, j == 0),
                             jnp.logical_and(pl.program_id(0) == HEADS - 1, j == nblk - 1), True)
        start()

        @pl.when(j == 0)
        def _():
            dq_acc[...] = jnp.zeros_like(dq_acc)

        dk_acc[...] = jnp.zeros_like(dk_acc)
        dv_acc[...] = jnp.zeros_like(dv_acc)
        k = k_ref[...]
        vv = v_ref[...]

        def products(i, st_ref, dp_ref):
            rs = pl.ds(pl.multiple_of(i * t, t), t)
            st_ref[...] = lax.dot_general(k, q_ref[rs, :], NT, preferred_element_type=F32)
            dp_ref[...] = lax.dot_general(vv, do_ref[rs, :], NT, preferred_element_type=F32)

        def consume(i, st_ref, dp_ref):
            rs = pl.ds(pl.multiple_of(i * t, t), t)
            q = q_ref[rs, :]
            dob = do_ref[rs, :]
            st = jnp.where(jnp.logical_or(_causal_mask_t(t), i != j), st_ref[...], MASK_VALUE)
            pt = jnp.exp(st - lse_ref[0, i])
            dv_acc[...] += jnp.dot(pt.astype(BF16), dob, preferred_element_type=F32)
            dst = (pt * (dp_ref[...] - delta_ref[0, i])).astype(BF16)
            dk_acc[...] += jnp.dot(dst, q, preferred_element_type=F32)
            dq_acc[rs, :] += lax.dot_general(dst, k, TN, preferred_element_type=F32)

        bufs = ((st_a, dp_a), (st_b, dp_b))

        def step(i, parity, issue_next):
            if issue_next:
                products(i + 1, *bufs[1 - parity])
            consume(i, *bufs[parity])

        products(j, st_a, dp_a)

        def trip(it, carry):
            for u in range(ATTN_UNROLL):
                step(j + it * ATTN_UNROLL + u, u % 2, True)
            return carry

        trips = (nblk - 1 - j) // ATTN_UNROLL
        lax.fori_loop(0, trips, trip, 0)
        i0 = j + trips * ATTN_UNROLL
        for left in range(1, ATTN_UNROLL + 1):
            @pl.when(nblk - i0 == left)
            def _(left=left):
                for u in range(left):
                    step(i0 + u, u % 2, u < left - 1)

        dk_ref[...] = dk_acc[...].astype(BF16)
        dv_ref[...] = dv_acc[...].astype(BF16)

        @pl.when(j == nblk - 1)
        def _():
            dq_ref[...] = dq_acc[...].astype(BF16)

        wait()

    stat = pl.BlockSpec((1, nblk, 1, t), lambda h, j: (h, 0, 0, 0))
    r_in, r_out, r_shape, r_scratch = _rider_specs(gsends, True)
    return pl.pallas_call(
        body, name="attn_bwd", grid=(HEADS, nblk),
        in_specs=[pl.BlockSpec((rows, HEAD_PAD), lambda h, j: (0, h)),
                  pl.BlockSpec((t, HEAD_PAD), lambda h, j: (j, h)),
                  pl.BlockSpec((t, V_HEAD), lambda h, j: (j, h)),
                  stat, stat,
                  pl.BlockSpec((rows, V_HEAD), lambda h, j: (0, h))] + r_in,
        out_specs=[pl.BlockSpec((rows, HEAD_PAD), lambda h, j: (0, h)),
                   pl.BlockSpec((t, HEAD_PAD), lambda h, j: (j, h)),
                   pl.BlockSpec((t, V_HEAD), lambda h, j: (j, h))] + r_out,
        out_shape=[jax.ShapeDtypeStruct((rows, HEADS * HEAD_PAD), BF16),
                   jax.ShapeDtypeStruct((rows, HEADS * HEAD_PAD), BF16),
                   jax.ShapeDtypeStruct((rows, HEADS * V_HEAD), BF16)] + r_shape,
        scratch_shapes=[pltpu.VMEM((rows, HEAD_PAD), F32), pltpu.VMEM((t, HEAD_PAD), F32),
                        pltpu.VMEM((t, V_HEAD), F32)] + [pltpu.VMEM((t, t), F32)] * 4 + r_scratch,
        compiler_params=_cparams(("arbitrary", "arbitrary")),
    )(qc, kc, v, lse, delta, do, *gsends)


def _shift_down(prev_tile, x, k):
    xx = jnp.concatenate([prev_tile, x], axis=0)
    return pltpu.roll(xx, k, 0)[SUBLANES:]


def _shift_up(x, next_tile, k):
    n = x.shape[0]
    xx = jnp.concatenate([x, next_tile], axis=0)
    return pltpu.roll(xx, n + SUBLANES - k, 0)[:n]


def _lru_gates(u, u_prev, cw_ref, cb_ref, wrg_ref, brg_ref, wig_ref, big_ref, lam_ref, first_block):
    taps = [_shift_down(u_prev, u, CONV_WIDTH - 1 - j) if j < CONV_WIDTH - 1 else u for j in range(CONV_WIDTH)]
    uc = cb_ref[...] + taps[0] * cw_ref[0:1, :]
    for j in range(1, CONV_WIDTH):
        uc = uc + taps[j] * cw_ref[j:j + 1, :]
    ub = uc.astype(BF16)
    zr = jnp.concatenate([jnp.dot(ub[:, g * LRU_BLOCK:(g + 1) * LRU_BLOCK], wrg_ref[g], preferred_element_type=F32)
                          for g in range(LRU_BLOCKS)], axis=1) + brg_ref[...]
    zi = jnp.concatenate([jnp.dot(ub[:, g * LRU_BLOCK:(g + 1) * LRU_BLOCK], wig_ref[g], preferred_element_type=F32)
                          for g in range(LRU_BLOCKS)], axis=1) + big_ref[...]
    r = jax.nn.sigmoid(zr)
    ig = jax.nn.sigmoid(zi)
    sp = _softplus_neg(lam_ref[...])
    log_a = (-LRU_C) * r * sp
    a = jnp.exp(log_a)
    m2 = -_expm1_neg(2.0 * log_a)
    mult_raw = m2 * lax.rsqrt(jnp.maximum(m2, 1e-30))
    row = lax.broadcasted_iota(jnp.int32, u.shape, 0)
    is_start = jnp.logical_and(first_block, row == 0)
    mult = jnp.where(is_start, 1.0, mult_raw)
    return dict(taps=taps, uc=uc, ub=ub, r=r, ig=ig, sp=sp, a=a, mult=mult, mult_raw=mult_raw, is_start=is_start)


def _rglru_fwd(u, cw, cb, wrg, brg, wig, big, lam):
    rows = u.shape[0]
    tb = _row_block(rows)

    def body(u_ref, cw_ref, cb_ref, wrg_ref, brg_ref, wig_ref, big_ref, lam_ref, hs_ref, utail, hcar, a_s, b_s):
        i = pl.program_id(0)

        @pl.when(i == 0)
        def _():
            utail[...] = jnp.zeros_like(utail)
            hcar[...] = jnp.zeros_like(hcar)

        u = u_ref[...]
        gt = _lru_gates(u, utail[...], cw_ref, cb_ref, wrg_ref, brg_ref, wig_ref, big_ref, lam_ref, i == 0)
        a_s[...] = gt["a"]
        b_s[...] = gt["mult"] * (gt["ig"] * gt["uc"])
        row8 = lax.broadcasted_iota(jnp.int32, (SUBLANES, LRU_WIDTH), 0)

        def tile(tix, carry):
            rs = pl.ds(pl.multiple_of(tix * SUBLANES, SUBLANES), SUBLANES)
            av, bv = a_s[rs, :], b_s[rs, :]
            for k in (1, 2, 4):
                keep = row8 >= k
                bv = jnp.where(keep, av * pltpu.roll(bv, k, 0) + bv, bv)
                av = jnp.where(keep, av * pltpu.roll(av, k, 0), av)
            h8 = av * carry + bv
            hs_ref[rs, :] = h8
            return jnp.broadcast_to(h8[SUBLANES - 1:SUBLANES, :], (SUBLANES, LRU_WIDTH))

        hcar[...] = lax.fori_loop(0, tb // SUBLANES, tile, hcar[...])
        utail[...] = u[tb - SUBLANES:, :]

    full2 = lambda shape: pl.BlockSpec(shape, lambda i: (0, 0))
    full3 = lambda shape: pl.BlockSpec(shape, lambda i: (0, 0, 0))
    blk = pl.BlockSpec((tb, LRU_WIDTH), lambda i: (i, 0))
    return pl.pallas_call(
        body, name="rglru_fwd", grid=(rows // tb,),
        in_specs=[blk, full2((CONV_WIDTH, LRU_WIDTH)), full2((1, LRU_WIDTH)),
                  full3((LRU_BLOCKS, LRU_BLOCK, LRU_BLOCK)), full2((1, LRU_WIDTH)),
                  full3((LRU_BLOCKS, LRU_BLOCK, LRU_BLOCK)), full2((1, LRU_WIDTH)), full2((1, LRU_WIDTH))],
        out_specs=blk,
        out_shape=jax.ShapeDtypeStruct((rows, LRU_WIDTH), F32),
        scratch_shapes=[pltpu.VMEM((SUBLANES, LRU_WIDTH), F32), pltpu.VMEM((SUBLANES, LRU_WIDTH), F32),
                        pltpu.VMEM((tb, LRU_WIDTH), F32), pltpu.VMEM((tb, LRU_WIDTH), F32)],
        compiler_params=_cparams(("arbitrary",)),
    )(u, cw, cb, wrg, brg, wig, big, lam)


def _rglru_bwd(u, hs, dhs, cw, cb, wrg, brg, wig, big, lam):
    rows = u.shape[0]
    tb = _row_block(rows)
    nblk = rows // tb
    tiles = tb // SUBLANES

    def body(u_ref, up_ref, hs_ref, hp_ref, dhs_ref, cw_ref, cb_ref, wrg_ref, brg_ref, wig_ref, big_ref, lam_ref,
             du_ref, dcw_ref, dcb_ref, dwrg_ref, dbrg_ref, dwig_ref, dbig_ref, dlam_ref,
             gcar, duc_head, a_s, b_s, g_s, dsp_acc):
        step = pl.program_id(0)
        blk_ix = nblk - 1 - step

        @pl.when(step == 0)
        def _():
            for ref in (dcw_ref, dcb_ref, dwrg_ref, dbrg_ref, dwig_ref, dbig_ref, gcar, duc_head, dsp_acc):
                ref[...] = jnp.zeros_like(ref)

        first = blk_ix == 0
        u = u_ref[...]
        u_prev = jnp.where(first, 0.0, up_ref[...])
        h_prev_tile = jnp.where(first, 0.0, hp_ref[...])
        gt = _lru_gates(u, u_prev, cw_ref, cb_ref, wrg_ref, brg_ref, wig_ref, big_ref, lam_ref, first)
        a, r, ig, uc, mult = gt["a"], gt["r"], gt["ig"], gt["uc"], gt["mult"]
        dhs_v = dhs_ref[...]

        a_s[...] = a
        b_s[...] = a * dhs_v
        row8 = lax.broadcasted_iota(jnp.int32, (SUBLANES, LRU_WIDTH), 0)

        def tile(tix, carry):
            rs = pl.ds(pl.multiple_of((tiles - 1 - tix) * SUBLANES, SUBLANES), SUBLANES)
            av, bv = a_s[rs, :], b_s[rs, :]
            for k in (1, 2, 4):
                keep = row8 < SUBLANES - k
                bv = jnp.where(keep, av * pltpu.roll(bv, SUBLANES - k, 0) + bv, bv)
                av = jnp.where(keep, av * pltpu.roll(av, SUBLANES - k, 0), av)
            g8 = av * carry + bv
            g_s[rs, :] = g8
            return jnp.broadcast_to(g8[0:1, :], (SUBLANES, LRU_WIDTH))

        g_next = gcar[...]
        gcar[...] = lax.fori_loop(0, tiles, tile, g_next)
        g = dhs_v + _shift_up(g_s[...], g_next, 1)

        h_prev = _shift_down(h_prev_tile, hs_ref[...], 1)
        da = g * h_prev
        iu = ig * uc
        dmult = jnp.where(gt["is_start"], 0.0, g * iu)
        d_ig = g * mult * uc
        duc = g * mult * ig
        dlog_a = da * a - dmult * (a * a) / gt["mult_raw"]
        dzr = (dlog_a * ((-LRU_C) * gt["sp"])) * r * (1.0 - r)
        dsp_acc[...] += jnp.sum(dlog_a * ((-LRU_C) * r), axis=0, keepdims=True)
        dzi = d_ig * ig * (1.0 - ig)
        dbrg_ref[...] += jnp.sum(dzr, axis=0, keepdims=True)
        dbig_ref[...] += jnp.sum(dzi, axis=0, keepdims=True)
        dzr_b, dzi_b = dzr.astype(BF16), dzi.astype(BF16)
        ub = gt["ub"]
        duc_parts = []
        for gi in range(LRU_BLOCKS):
            cs = slice(gi * LRU_BLOCK, (gi + 1) * LRU_BLOCK)
            dwrg_ref[gi] += lax.dot_general(ub[:, cs], dzr_b[:, cs], TN, preferred_element_type=F32)
            dwig_ref[gi] += lax.dot_general(ub[:, cs], dzi_b[:, cs], TN, preferred_element_type=F32)
            duc_parts.append(lax.dot_general(dzr_b[:, cs], wrg_ref[gi], NT, preferred_element_type=F32)
                             + lax.dot_general(dzi_b[:, cs], wig_ref[gi], NT, preferred_element_type=F32))
        duc = duc + jnp.concatenate(duc_parts, axis=1)

        dcb_ref[...] += jnp.sum(duc, axis=0, keepdims=True)
        taps = gt["taps"]
        for jt in range(CONV_WIDTH):
            dcw_ref[jt:jt + 1, :] += jnp.sum(duc * taps[jt], axis=0, keepdims=True)
        head = duc_head[...]
        du = duc * cw_ref[CONV_WIDTH - 1:CONV_WIDTH, :]
        for jt in range(CONV_WIDTH - 1):
            du = du + _shift_up(duc, head, CONV_WIDTH - 1 - jt) * cw_ref[jt:jt + 1, :]
        du_ref[...] = du.astype(BF16)
        duc_head[...] = duc[:SUBLANES, :]

        @pl.when(step == nblk - 1)
        def _():
            dlam_ref[...] = -dsp_acc[...] * jax.nn.sigmoid(-lam_ref[...])

    full2 = lambda shape: pl.BlockSpec(shape, lambda s: (0, 0))
    full3 = lambda shape: pl.BlockSpec(shape, lambda s: (0, 0, 0))
    blk = pl.BlockSpec((tb, LRU_WIDTH), lambda s: (nblk - 1 - s, 0))
    prev_tile = pl.BlockSpec((SUBLANES, LRU_WIDTH), lambda s: (jnp.maximum((nblk - 1 - s) * tiles - 1, 0), 0))
    wshape = (LRU_BLOCKS, LRU_BLOCK, LRU_BLOCK)
    return pl.pallas_call(
        body, name="rglru_bwd", grid=(nblk,),
        in_specs=[blk, prev_tile, blk, prev_tile, blk, full2((CONV_WIDTH, LRU_WIDTH)), full2((1, LRU_WIDTH)),
                  full3(wshape), full2((1, LRU_WIDTH)), full3(wshape), full2((1, LRU_WIDTH)), full2((1, LRU_WIDTH))],
        out_specs=[blk, full2((CONV_WIDTH, LRU_WIDTH)), full2((1, LRU_WIDTH)), full3(wshape), full2((1, LRU_WIDTH)),
                   full3(wshape), full2((1, LRU_WIDTH)), full2((1, LRU_WIDTH))],
        out_shape=[jax.ShapeDtypeStruct((rows, LRU_WIDTH), BF16),
                   jax.ShapeDtypeStruct((CONV_WIDTH, LRU_WIDTH), F32), jax.ShapeDtypeStruct((1, LRU_WIDTH), F32),
                   jax.ShapeDtypeStruct(wshape, F32), jax.ShapeDtypeStruct((1, LRU_WIDTH), F32),
                   jax.ShapeDtypeStruct(wshape, F32), jax.ShapeDtypeStruct((1, LRU_WIDTH), F32),
                   jax.ShapeDtypeStruct((1, LRU_WIDTH), F32)],
        scratch_shapes=[pltpu.VMEM((SUBLANES, LRU_WIDTH), F32), pltpu.VMEM((SUBLANES, LRU_WIDTH), F32),
                        pltpu.VMEM((tb, LRU_WIDTH), F32), pltpu.VMEM((tb, LRU_WIDTH), F32),
                        pltpu.VMEM((tb, LRU_WIDTH), F32), pltpu.VMEM((1, LRU_WIDTH), F32)],
        compiler_params=_cparams(("arbitrary",)),
    )(u, u, hs, hs, dhs, cw, cb, wrg, brg, wig, big, lam)


def _out_proj_loss(a, gate, h, w, gf, target, n_real):
    rows = h.shape[0]
    tr = _row_block(rows)

    def body(a_ref, gate_ref, h_ref, w_ref, g_ref, t_ref, dh_ref, loss_ref, dg_ref, da_ref, dgate_ref, dw_ref,
             dw_acc):
        i = pl.program_id(0)

        @pl.when(i == 0)
        def _():
            loss_ref[...] = jnp.zeros_like(loss_ref)
            dg_ref[...] = jnp.zeros_like(dg_ref)
            dw_acc[...] = jnp.zeros_like(dw_acc)

        gv = g_ref[...]
        av, gatev = a_ref[...], gate_ref[...]
        sg = _silu(gatev)
        y = (av * sg).astype(BF16)
        xn, r = _rms_fwd(h_ref[...] + jnp.dot(y, w_ref[...], preferred_element_type=F32))
        row = i * tr + lax.broadcasted_iota(jnp.int32, (tr, 1), 0)
        live = jnp.logical_and(row >= N_META, row < n_real)
        tgt = t_ref[...]
        tgt = jnp.where(i == 0, pltpu.roll(tgt, N_META, 0), tgt)
        err = jnp.where(live, xn * gv - tgt, 0.0)
        loss_ref[...] += (0.5 / D_MODEL) * jnp.sum(jnp.sum(err * err, axis=1, keepdims=True), axis=0, keepdims=True)
        dx, dg = _rms_bwd(err * (1.0 / D_MODEL), xn, r, gv)
        dg_ref[...] += dg
        dh_ref[...] = dx
        dhb = dx.astype(BF16)
        dw_acc[...] += lax.dot_general(y, dhb, TN, preferred_element_type=F32)
        dy = lax.dot_general(dhb, w_ref[...], NT, preferred_element_type=F32)
        da_ref[...] = dy * sg
        dgate_ref[...] = (dy * av * _dsilu(gatev)).astype(BF16)

        @pl.when(i == rows // tr - 1)
        def _():
            dw_ref[...] = dw_acc[...].astype(BF16)

    blk = pl.BlockSpec((tr, D_MODEL), lambda i: (i, 0))
    wblk = pl.BlockSpec((D_MODEL, D_MODEL), lambda i: (0, 0))
    window = pl.BlockSpec((pl.Element(tr, (0, rows - n_real)), pl.Element(D_MODEL)),
                          lambda i: (pl.multiple_of(jnp.maximum(i * tr - N_META, 0), SUBLANES), 0))
    return pl.pallas_call(
        body, name="b_out_loss", grid=(rows // tr,),
        in_specs=[blk, blk, blk, wblk, pl.BlockSpec((1, D_MODEL), lambda i: (0, 0)), window],
        out_specs=[blk, pl.BlockSpec((1, 1), lambda i: (0, 0)), pl.BlockSpec((1, D_MODEL), lambda i: (0, 0)),
                   blk, blk, wblk],
        out_shape=[jax.ShapeDtypeStruct((rows, D_MODEL), F32), jax.ShapeDtypeStruct((1, 1), F32),
                   jax.ShapeDtypeStruct((1, D_MODEL), F32), jax.ShapeDtypeStruct((rows, D_MODEL), F32),
                   jax.ShapeDtypeStruct((rows, D_MODEL), BF16), jax.ShapeDtypeStruct((D_MODEL, D_MODEL), BF16)],
        scratch_shapes=[pltpu.VMEM((D_MODEL, D_MODEL), F32)],
        compiler_params=_cparams(("arbitrary",)),
    )(a, gate, h, w, gf, target)


def _my_place():
    x, y, c = lax.axis_index("x"), lax.axis_index("y"), lax.axis_index("c")
    return x, y, c, 4 * x + 2 * y + c


def _peer(x, y, c, k):
    px, py, pc = x ^ (k >> 2), y ^ ((k >> 1) & 1), c ^ (k & 1)
    return (px, py, pc), 4 * px + 2 * py + pc


def _exchange_copies(src_of, dst_ref, send_sems, recv_sems, local_sem):
    x, y, c, me = _my_place()
    copies = [pltpu.make_async_copy(src_of(me), dst_ref.at[me], local_sem)]
    for k in range(1, N_DEV):
        peer, pid = _peer(x, y, c, k)
        copies.append(pltpu.make_async_remote_copy(
            src_ref=src_of(pid), dst_ref=dst_ref.at[me], send_sem=send_sems.at[k], recv_sem=recv_sems.at[k],
            device_id=peer, device_id_type=MESH))
    return copies


def _exchange_sems(nb):
    return [pltpu.SemaphoreType.DMA((nb, N_DEV)), pltpu.SemaphoreType.DMA((nb, N_DEV)), pltpu.SemaphoreType.DMA((nb,))]


def _sum_blocks(lands, name):
    n = len(lands)

    def body(*refs):
        for land_ref, out_ref in zip(refs[:n], refs[n:]):
            acc = land_ref[0].astype(F32)
            for d in range(1, N_DEV):
                acc = acc + land_ref[d].astype(F32)
            out_ref[...] = acc

    return pl.pallas_call(
        body, name=name, out_shape=[jax.ShapeDtypeStruct(l.shape[1:], F32) for l in lands],
        compiler_params=pltpu.CompilerParams(vmem_limit_bytes=VMEM_LIMIT),
    )(*lands)


def _all_gather(big, small):
    def body(big_ref, small_ref, obig_ref, osmall_ref, send_sems, recv_sems, local_sems):
        x, y, c, _ = _my_place()
        me, sibling = (x, y, c), (x, y, 1 - c)
        chips = [(1 - x, y), (x, 1 - y), (1 - x, 1 - y)]
        parts = ((big_ref, obig_ref), (small_ref, osmall_ref))

        def slot(dst, place):
            return dst.at[4 * place[0] + 2 * place[1] + place[2]]

        def copy(part, k, block, to, first_hand=False):
            src, dst = parts[part]
            return pltpu.make_async_remote_copy(
                src_ref=src if first_hand else slot(dst, block), dst_ref=slot(dst, block),
                send_sem=send_sems.at[part, k], recv_sem=recv_sems.at[part, k], device_id=to, device_id_type=MESH)

        own = [pltpu.make_async_copy(src, slot(dst, me), local_sems.at[part]) for part, (src, dst) in enumerate(parts)]
        for cp in own:
            cp.start()
        first = []
        for part in range(len(parts)):
            first.append(copy(part, 0, me, sibling, True))
            first += [copy(part, 1 + j, me, (*chip, c), True) for j, chip in enumerate(chips)]
        for cp in first:
            cp.start()
        passed = []
        for j, chip in enumerate(chips):
            for part in range(len(parts)):
                copy(part, 1 + j, (*chip, c), me).wait_recv()
                passed.append(copy(part, 4 + j, (*chip, c), sibling))
                passed[-1].start()
        for part in range(len(parts)):
            copy(part, 0, sibling, me).wait_recv()
            for j, chip in enumerate(chips):
                copy(part, 4 + j, (*chip, 1 - c), me).wait_recv()
        for cp in first + passed:
            cp.wait_send()
        for cp in own:
            cp.wait()

    n = big.shape[0]
    hbm = pl.BlockSpec(memory_space=pl.ANY)
    return pl.pallas_call(
        body, name="weight_all_gather",
        in_specs=[hbm, hbm], out_specs=[hbm, hbm],
        out_shape=[jax.ShapeDtypeStruct((N_DEV, n, LANES), BF16), jax.ShapeDtypeStruct((N_DEV,) + small.shape, F32)],
        scratch_shapes=[pltpu.SemaphoreType.DMA((2, N_DEV)), pltpu.SemaphoreType.DMA((2, N_DEV)),
                        pltpu.SemaphoreType.DMA((2,))],
        compiler_params=pltpu.CompilerParams(has_side_effects=True),
    )(big, small)


GRAD_CHUNK = 32


def _grad_exchange(gbig, rep):
    n = gbig.shape[1]
    nrep = rep.shape[0]

    def body(gbig_ref, rep_ref, out_ref, orep_ref, land, land_rep, send_sems, recv_sems, local_sems):
        x, y, c, me = _my_place()
        own = [pltpu.make_async_copy(gbig_ref.at[me], land.at[me], local_sems.at[0]),
               pltpu.make_async_copy(rep_ref, land_rep.at[me], local_sems.at[1])]
        for cp in own:
            cp.start()
        copies = []
        for k in range(1, N_DEV):
            peer, pid = _peer(x, y, c, k)
            copies.append(pltpu.make_async_remote_copy(
                src_ref=gbig_ref.at[pid], dst_ref=land.at[me], send_sem=send_sems.at[0, k],
                recv_sem=recv_sems.at[0, k], device_id=peer, device_id_type=MESH))
            copies.append(pltpu.make_async_remote_copy(
                src_ref=rep_ref, dst_ref=land_rep.at[me], send_sem=send_sems.at[1, k],
                recv_sem=recv_sems.at[1, k], device_id=peer, device_id_type=MESH))
        for cp in copies:
            cp.start()
        for cp in copies:
            cp.wait()
        for cp in own:
            cp.wait()

        def chunk(ci, carry):
            rs = pl.ds(pl.multiple_of(ci * GRAD_CHUNK, GRAD_CHUNK), GRAD_CHUNK)
            acc = land[0, rs, :].astype(F32)
            for d in range(1, N_DEV):
                acc = acc + land[d, rs, :].astype(F32)
            out_ref[rs, :] = acc
            return carry

        lax.fori_loop(0, n // GRAD_CHUNK, chunk, 0)
        acc = land_rep[0]
        for d in range(1, N_DEV):
            acc = acc + land_rep[d]
        orep_ref[...] = acc

    return pl.pallas_call(
        body, name="grad_exchange",
        in_specs=[pl.BlockSpec(memory_space=pl.ANY), pl.BlockSpec(memory_space=pl.ANY)],
        out_specs=[pl.BlockSpec(memory_space=pltpu.VMEM), pl.BlockSpec(memory_space=pltpu.VMEM)],
        out_shape=[jax.ShapeDtypeStruct((n, LANES), F32), jax.ShapeDtypeStruct((nrep, LANES), F32)],
        scratch_shapes=[pltpu.VMEM((N_DEV, n, LANES), BF16), pltpu.VMEM((N_DEV, nrep, LANES), F32),
                        pltpu.SemaphoreType.DMA((2, N_DEV)), pltpu.SemaphoreType.DMA((2, N_DEV)),
                        pltpu.SemaphoreType.DMA((2,))],
        compiler_params=pltpu.CompilerParams(vmem_limit_bytes=VMEM_LIMIT, has_side_effects=True),
    )(gbig, rep)


def _adamw_all(ws, gs, ms, vs):
    n = len(ws)

    def body(*refs):
        w_refs, g_refs, m_refs, v_refs = refs[0:n], refs[n:2 * n], refs[2 * n:3 * n], refs[3 * n:4 * n]
        d_refs, nm_refs, nv_refs = refs[4 * n:5 * n], refs[5 * n:6 * n], refs[6 * n:7 * n]
        for w_ref, g_ref, m_ref, v_ref, d_ref, nm_ref, nv_ref in zip(w_refs, g_refs, m_refs, v_refs, d_refs, nm_refs, nv_refs):
            g = g_ref[...]
            m = ADAM_B1 * m_ref[...] + (1.0 - ADAM_B1) * g
            v = ADAM_B2 * v_ref[...] + (1.0 - ADAM_B2) * jnp.square(g)
            m_hat = m / (1.0 - ADAM_B1 ** ADAM_STEP)
            v_hat = v / (1.0 - ADAM_B2 ** ADAM_STEP)
            d_ref[...] = -ADAM_LR * (m_hat / (jnp.sqrt(v_hat) + ADAM_EPS) + ADAM_WD * w_ref[...])
            nm_ref[...] = m
            nv_ref[...] = v

    shapes = [jax.ShapeDtypeStruct(w.shape, F32) for w in ws]
    outs = pl.pallas_call(
        body, name="adamw", out_shape=shapes * 3,
        compiler_params=pltpu.CompilerParams(vmem_limit_bytes=VMEM_LIMIT),
    )(*ws, *gs, *ms, *vs)
    return outs[0:n], outs[n:2 * n], outs[2 * n:3 * n]


BIG_A0 = (("a_w_in", 1728),)
SMALL_A = (("meta_tokens", 16),)
SMALL_B = (("b_norm_g", 1), ("b_conv_w", 4), ("b_conv_b", 1), ("b_b_rg", 1), ("b_b_ig", 1), ("b_lam", 1))
REP = (("a_norm_g", 8), ("a_q_norm_g", 3), ("a_kv_norm_g", 2), ("final_norm_g", 8), ("loss", 1))
SLOT = 16


def _offsets(table, slot=1, start=0):
    out, o = {}, start
    for name, n in table:
        out[name] = (o, n)
        o += -(-n // slot) * slot
    return out, o


def _slotted(a, axis):
    pad = -a.shape[axis] % SLOT
    if not pad:
        return a
    widths = [(0, 0)] * a.ndim
    widths[axis] = (0, pad)
    return jnp.pad(a, widths)


def _rope_tables(rows):
    pos = np.arange(rows, dtype=np.float32)
    inv_freq = (np.float32(ROPE_BASE) ** (-np.arange(0, QK_ROPE, 2, dtype=np.float32) / np.float32(QK_ROPE))).astype(
        np.float32)
    ang = pos[:, None] * inv_freq[None, :]
    cos, sin = np.cos(ang).astype(np.float32), np.sin(ang).astype(np.float32)
    zeros = np.zeros((rows, LANES - QK_ROPE), np.float32)
    return jnp.asarray(np.concatenate([cos, cos, zeros], axis=1)), jnp.asarray(np.concatenate([-sin, sin, zeros], axis=1))


def kernel(x, meta_tokens, a_norm_g, a_w_in, a_q_norm_g, a_kv_norm_g, a_w_uq, a_w_ukv, a_w_out, b_norm_g, b_w_in, b_conv_w, b_conv_b, b_w_rg, b_b_rg, b_w_ig, b_b_ig, b_lam, b_w_out, final_norm_g, loss_target, m_meta_tokens, m_a_norm_g, m_a_w_in, m_a_q_norm_g, m_a_kv_norm_g, m_a_w_uq, m_a_w_ukv, m_a_w_out, m_b_norm_g, m_b_w_in, m_b_conv_w, m_b_conv_b, m_b_w_rg, m_b_b_rg, m_b_w_ig, m_b_b_ig, m_b_lam, m_b_w_out, m_final_norm_g, v_meta_tokens, v_a_norm_g, v_a_w_in, v_a_q_norm_g, v_a_kv_norm_g, v_a_w_uq, v_a_w_ukv, v_a_w_out, v_b_norm_g, v_b_w_in, v_b_conv_w, v_b_conv_b, v_b_w_rg, v_b_b_rg, v_b_w_ig, v_b_b_ig, v_b_lam, v_b_w_out, v_final_norm_g):
    names = ("meta_tokens", "a_norm_g", "a_w_in", "a_q_norm_g", "a_kv_norm_g", "a_w_uq", "a_w_ukv", "a_w_out",
             "b_norm_g", "b_w_in", "b_conv_w", "b_conv_b", "b_w_rg", "b_b_rg", "b_w_ig", "b_b_ig", "b_lam", "b_w_out",
             "final_norm_g")
    w = dict(zip(names, (meta_tokens, a_norm_g, a_w_in, a_q_norm_g, a_kv_norm_g, a_w_uq, a_w_ukv, a_w_out, b_norm_g,
                         b_w_in, b_conv_w, b_conv_b, b_w_rg, b_b_rg, b_w_ig, b_b_ig, b_lam, b_w_out, final_norm_g)))
    mom_m = dict(zip(names, (m_meta_tokens, m_a_norm_g, m_a_w_in, m_a_q_norm_g, m_a_kv_norm_g, m_a_w_uq, m_a_w_ukv,
                             m_a_w_out, m_b_norm_g, m_b_w_in, m_b_conv_w, m_b_conv_b, m_b_w_rg, m_b_b_rg, m_b_w_ig,
                             m_b_b_ig, m_b_lam, m_b_w_out, m_final_norm_g)))
    mom_v = dict(zip(names, (v_meta_tokens, v_a_norm_g, v_a_w_in, v_a_q_norm_g, v_a_kv_norm_g, v_a_w_uq, v_a_w_ukv,
                             v_a_w_out, v_b_norm_g, v_b_w_in, v_b_conv_w, v_b_conv_b, v_b_w_rg, v_b_b_rg, v_b_w_ig,
                             v_b_b_ig, v_b_lam, v_b_w_out, v_final_norm_g)))

    seq = x.shape[1]
    n_real = N_META + seq
    rows = -(-n_real // LANES) * LANES
    scale = (QK_NOPE + QK_ROPE) ** -0.5
    biga0_off, biga0_rows = _offsets(BIG_A0)
    small_off, _ = _offsets(SMALL_A + SMALL_B, SLOT)
    gsmalla_off, grada_rows = _offsets(SMALL_A, SLOT, biga0_rows)
    gsmallb_off, _ = _offsets(SMALL_B, SLOT)
    grada_rows = -(-grada_rows // GRAD_CHUNK) * GRAD_CHUNK
    rep_off, _ = _offsets(REP, SLOT)

    send_a0 = w["a_w_in"].reshape(-1, LANES).astype(BF16)
    send_small = jnp.concatenate([_slotted(w[nm].reshape(-1, LANES), 0) for nm, _ in SMALL_A + SMALL_B], axis=0)
    sends_a1 = [jnp.pad(a_w_uq[0], ((0, 0), (0, HEAD_PAD - QK_NOPE - QK_ROPE))).astype(BF16), a_w_ukv[0].astype(BF16)]
    lru_rows = LRU_BLOCKS * LRU_BLOCK // N_DEV
    sends_b = [a_w_out[0].astype(BF16), b_w_in[0].astype(BF16), b_w_rg.reshape(lru_rows, LRU_BLOCK).astype(BF16),
               b_w_ig.reshape(lru_rows, LRU_BLOCK).astype(BF16), b_w_out[0].astype(BF16)]
    all_a0, all_small = _all_gather(send_a0, send_small)

    def small_seg(nm):
        o, n = small_off[nm]
        return all_small[:, o:o + n, :]

    cdev_a = a_w_in.shape[-1]
    w_in_a = all_a0.reshape(N_DEV, D_MODEL, cdev_a).transpose(1, 0, 2).reshape(D_MODEL, N_DEV * cdev_a)
    w_in_a = jnp.concatenate([w_in_a[:, :LAT + QK_ROPE], jnp.zeros((D_MODEL, LAT_PAD - LAT - QK_ROPE), BF16),
                              w_in_a[:, LAT + QK_ROPE:]], axis=1)[None]
    meta_full = small_seg("meta_tokens").transpose(1, 0, 2).reshape(N_META, D_MODEL)
    vec = lambda nm: small_seg(nm).reshape(1, D_MODEL)
    g_b, conv_b, b_rg, b_ig, lam = vec("b_norm_g"), vec("b_conv_b"), vec("b_b_rg"), vec("b_b_ig"), vec("b_lam")
    conv_w = small_seg("b_conv_w").transpose(1, 0, 2).reshape(CONV_WIDTH, LRU_WIDTH)
    g_a, g_q, g_kv = a_norm_g, a_q_norm_g, a_kv_norm_g
    g_f = final_norm_g.reshape(1, D_MODEL)

    h0 = jnp.concatenate([meta_full, x[0], jnp.zeros((rows - n_real, D_MODEL), F32)], axis=0)
    cos, sin = _rope_tables(rows)

    lat, gate_a, w_uq, w_ukv = _norm_proj_fwd(h0, g_a, w_in_a, LAT_PAD, "a_in_fwd", sends_a1)
    qc, kc, v, vt = _mla_qkv_fwd(lat, g_q, g_kv, w_uq, w_ukv, cos, sin, scale)
    o, lse, w_out_a, w_in_b, w_rg, w_ig, w_out_b = _attn_fwd(qc, kc, vt, sends_b)

    lru_w = lambda g: g.reshape(N_DEV, LRU_BLOCKS, LRU_BLOCK // N_DEV, LRU_BLOCK).transpose(1, 0, 2, 3).reshape(
        LRU_BLOCKS, LRU_BLOCK, LRU_BLOCK)
    w_out_a, w_out_b = w_out_a.reshape(D_MODEL, D_MODEL), w_out_b.reshape(D_MODEL, D_MODEL)
    w_rg, w_ig = lru_w(w_rg), lru_w(w_ig)

    h1, u, gate_b = _out_proj_in_proj(o, gate_a, h0, w_out_a, g_b, w_in_b, LRU_WIDTH, "a_out_b_in_fwd")
    hs = _rglru_fwd(u, conv_w, conv_b, w_rg, b_rg, w_ig, b_ig, lam)
    dh2, loss_part, dg_f, dhs, dgate_b, dw_out_b = _out_proj_loss(hs, gate_b, h1, w_out_b, g_f, loss_target[0],
                                                                   n_real)

    du, dconv_w, dconv_b, dw_rg, db_rg, dw_ig, db_ig, dlam = _rglru_bwd(u, hs, dhs, conv_w, conv_b, w_rg, b_rg, w_ig,
                                                                       b_ig, lam)
    dh1, dw_in_b, dg_b = _norm_proj_bwd(h1, g_b, w_in_b, du, dgate_b, dh2, "b_in_bwd")
    do, dgate_a, dw_out_a, delta = _attn_out_bwd(o, gate_a, dh1, w_out_a)

    def to_cols(g, cdev):
        r = g.shape[0]
        return g.reshape(r, N_DEV, cdev).transpose(1, 0, 2).reshape(N_DEV, -1, LANES)

    lru_g = lambda g: g.reshape(LRU_BLOCKS, N_DEV, LRU_BLOCK // N_DEV, LRU_BLOCK).transpose(1, 0, 2, 3).reshape(
        N_DEV, lru_rows, LRU_BLOCK)
    small_b = {"b_norm_g": dg_b, "b_conv_w": dconv_w, "b_conv_b": dconv_b, "b_b_rg": db_rg, "b_b_ig": db_ig,
               "b_lam": dlam}
    gsends_b = [dw_out_a.reshape(N_DEV, -1, D_MODEL), dw_in_b, lru_g(dw_rg).astype(BF16), lru_g(dw_ig).astype(BF16),
                dw_out_b.reshape(N_DEV, -1, D_MODEL),
                jnp.concatenate([_slotted(to_cols(small_b[nm], LANES).astype(BF16), 1) for nm, _ in SMALL_B], axis=1)]

    dqc, dkc, dv, *lands_b = _attn_bwd(qc, kc, v, lse, delta, do, gsends_b)
    g_out_a, g_in_b, g_rg, g_ig, g_out_b, gsum_small_b = _sum_blocks(lands_b, "sum_blocks_b")
    dlat, dw_uq, dw_ukv, dg_q, dg_kv = _mla_qkv_bwd(lat, g_q, g_kv, w_uq, w_ukv, cos, sin, dqc, dkc, dv, scale)
    dh0, dw_in_a, dg_a, *lands_a1 = _norm_proj_bwd(h0, g_a, w_in_a, dlat, dgate_a, dh1, "a_in_bwd", [dw_uq, dw_ukv])
    g_uq, g_ukv = _sum_blocks(lands_a1, "sum_blocks_a1")

    grad_x = dh0[N_META:n_real][None]

    dw_in_a_nat = jnp.concatenate([dw_in_a[0, :, :LAT + QK_ROPE], dw_in_a[0, :, LAT_PAD:]], axis=1)
    pieces = [to_cols(dw_in_a_nat, cdev_a), to_cols(dh0[:N_META], LANES).astype(BF16)]
    pieces.append(jnp.zeros((N_DEV, grada_rows - sum(p.shape[1] for p in pieces), LANES), BF16))
    gsend_a0 = jnp.concatenate(pieces, axis=1)
    rep_parts = {"a_norm_g": dg_a, "a_q_norm_g": dg_q, "a_kv_norm_g": dg_kv, "final_norm_g": dg_f,
                 "loss": jnp.broadcast_to(loss_part, (1, LANES))}
    rep = jnp.concatenate([_slotted(rep_parts[nm].reshape(-1, LANES), 0) for nm, _ in REP], axis=0)
    gsum_a0, rep_sum = _grad_exchange(gsend_a0, rep)

    grads = {"a_w_out": g_out_a, "b_w_in": g_in_b, "b_w_rg": g_rg, "b_w_ig": g_ig, "b_w_out": g_out_b,
             "a_w_uq": g_uq[:, :QK_NOPE + QK_ROPE], "a_w_ukv": g_ukv}
    grads = {nm: g.reshape(w[nm].shape) for nm, g in grads.items()}
    for off, src in ((biga0_off, gsum_a0), (gsmalla_off, gsum_a0), (gsmallb_off, gsum_small_b), (rep_off, rep_sum)):
        for nm, (o_r, n) in off.items():
            if nm in w:
                grads[nm] = src[o_r:o_r + n].reshape(w[nm].shape)
    loss = rep_sum[rep_off["loss"][0], 0]

    as2d = lambda a: a.reshape(1, -1) if a.ndim == 1 else a
    deltas, new_ms, new_vs = _adamw_all([as2d(w[nm]) for nm in names], [as2d(grads[nm]) for nm in names],
                                        [as2d(mom_m[nm]) for nm in names], [as2d(mom_v[nm]) for nm in names])
    shaped = lambda arrs: [a.reshape(w[nm].shape) for a, nm in zip(arrs, names)]
    return (loss, grad_x, *[grads[nm] for nm in names], *shaped(deltas), *shaped(new_ms), *shaped(new_vs))
```

```python
import functools

import numpy as np
import jax
import jax.numpy as jnp
from jax import lax
from jax.experimental import pallas as pl
from jax.experimental.pallas import tpu as pltpu

F32 = jnp.float32
BF16 = jnp.bfloat16

D_MODEL = 1024
N_META = 16
RMS_EPS = 1e-6
HEADS = 8
QK_NOPE = 128
QK_ROPE = 64
V_HEAD = 128
Q_LORA = 384
KV_LORA = 256
HEAD_PAD = 256
LAT = Q_LORA + KV_LORA
LAT_PAD = LAT + 128
ROPE_BASE = 10000.0
MASK_VALUE = -1e30
LRU_WIDTH = 1024
LRU_BLOCKS = 4
LRU_BLOCK = 256
CONV_WIDTH = 4
LRU_C = 8.0
N_DEV = 8
ADAM_LR, ADAM_B1, ADAM_B2, ADAM_EPS, ADAM_WD, ADAM_STEP = 0.001, 0.9, 0.999, 1e-08, 0.01, 10

LANES = 128
SUBLANES = 8
VMEM_LIMIT = 56 * 1024 * 1024
MESH = pl.DeviceIdType.MESH

NT = (((1,), (1,)), ((), ()))
TN = (((0,), (0,)), ((), ()))


def _row_block(rows):
    return 384 if rows % 384 == 0 else 128


def _cparams(sem):
    return pltpu.CompilerParams(dimension_semantics=sem, vmem_limit_bytes=VMEM_LIMIT)


def _silu(x):
    return x * jax.nn.sigmoid(x)


def _dsilu(x):
    s = jax.nn.sigmoid(x)
    return s * (1.0 + x * (1.0 - s))


def _rms_fwd(x):
    r = lax.rsqrt(jnp.mean(x * x, axis=-1, keepdims=True) + RMS_EPS)
    return x * r, r


def _rms_bwd(dy, xn, r, g):
    t = dy * g
    dx = r * (t - xn * jnp.mean(t * xn, axis=-1, keepdims=True))
    return dx, jnp.sum(dy * xn, axis=0, keepdims=True)


def _expm1_neg(x):
    small = x * (1.0 + x * (1 / 2 + x * (1 / 6 + x * (1 / 24))))
    return jnp.where(x > -0.05, small, jnp.exp(x) - 1.0)


def _softplus_neg(lam):
    z = jnp.exp(-jnp.abs(lam))
    w = z / (2.0 + z)
    w2 = w * w
    series = 2.0 * w * (1.0 + w2 * (1 / 3) + w2 * w2 * (1 / 5))
    return jnp.maximum(-lam, 0.0) + jnp.where(z < 0.1, series, jnp.log(1.0 + z))


def _rider(sends, refs, first, last, all_to_all):
    nb = len(sends)
    if not nb:
        return (lambda: None), (lambda: None)
    send_refs, result_refs = refs[:nb], refs[nb:2 * nb]
    send_sems, recv_sems, local_sems = refs[2 * nb:]
    pick = (lambda ref: (lambda d: ref.at[d])) if all_to_all else (lambda ref: (lambda d: ref))

    def copies():
        out = []
        for b in range(nb):
            out += _exchange_copies(pick(send_refs[b]), result_refs[b], send_sems.at[b], recv_sems.at[b],
                                    local_sems.at[b])
        return out

    def start():
        @pl.when(first)
        def _():
            for cp in copies():
                cp.start()

    def wait():
        @pl.when(last)
        def _():
            for cp in copies():
                cp.wait()

    return start, wait


def _rider_specs(sends, all_to_all):
    nb = len(sends)
    if not nb:
        return [], [], [], []
    hbm = pl.BlockSpec(memory_space=pl.ANY)
    shapes = [jax.ShapeDtypeStruct(s.shape if all_to_all else (N_DEV,) + s.shape, s.dtype) for s in sends]
    return [hbm] * nb, [hbm] * nb, shapes, _exchange_sems(nb)


def _proj_blocks(x, w_ref):
    return jnp.concatenate([jnp.dot(x, w_ref[d], preferred_element_type=F32) for d in range(w_ref.shape[0])], axis=1)


def _norm_proj_fwd(h, g, w, n1, name, wsends=()):
    rows = h.shape[0]
    nb, _, cb = w.shape
    n = nb * cb
    tr = _row_block(rows)
    nsteps = rows // tr
    extra = len(wsends)

    def body(h_ref, g_ref, w_ref, *rest):
        p1_ref, p2_ref = rest[extra], rest[extra + 1]
        i = pl.program_id(0)
        start, wait = _rider(wsends, rest[:extra] + rest[extra + 2:], i == 0, i == nsteps - 1, False)
        start()
        xn, _ = _rms_fwd(h_ref[...])
        p = _proj_blocks((xn * g_ref[...]).astype(BF16), w_ref)
        p1_ref[...] = p[:, :n1]
        p2_ref[...] = p[:, n1:]
        wait()

    r_in, r_out, r_shape, r_scratch = _rider_specs(wsends, False)
    return pl.pallas_call(
        body, name=name, grid=(nsteps,),
        in_specs=[pl.BlockSpec((tr, D_MODEL), lambda i: (i, 0)),
                  pl.BlockSpec((1, D_MODEL), lambda i: (0, 0)),
                  pl.BlockSpec((nb, D_MODEL, cb), lambda i: (0, 0, 0))] + r_in,
        out_specs=[pl.BlockSpec((tr, n1), lambda i: (i, 0)),
                   pl.BlockSpec((tr, n - n1), lambda i: (i, 0))] + r_out,
        out_shape=[jax.ShapeDtypeStruct((rows, n1), F32), jax.ShapeDtypeStruct((rows, n - n1), F32)] + r_shape,
        scratch_shapes=r_scratch,
        compiler_params=_cparams(("arbitrary",)),
    )(h, g, w, *wsends)


def _out_proj_in_proj(a, gate, h, w_out, g, w_in, n1, name):
    rows = h.shape[0]
    nb, _, cb = w_in.shape
    n = nb * cb
    tr = _row_block(rows)

    def body(a_ref, gate_ref, h_ref, wo_ref, g_ref, wi_ref, hn_ref, p1_ref, p2_ref):
        y = (a_ref[...] * _silu(gate_ref[...])).astype(BF16)
        h_new = h_ref[...] + jnp.dot(y, wo_ref[...], preferred_element_type=F32)
        hn_ref[...] = h_new
        xn, _ = _rms_fwd(h_new)
        p = _proj_blocks((xn * g_ref[...]).astype(BF16), wi_ref)
        p1_ref[...] = p[:, :n1]
        p2_ref[...] = p[:, n1:]

    blk = pl.BlockSpec((tr, D_MODEL), lambda i: (i, 0))
    return pl.pallas_call(
        body, name=name, grid=(rows // tr,),
        in_specs=[blk, blk, blk, pl.BlockSpec((D_MODEL, D_MODEL), lambda i: (0, 0)),
                  pl.BlockSpec((1, D_MODEL), lambda i: (0, 0)), pl.BlockSpec((nb, D_MODEL, cb), lambda i: (0, 0, 0))],
        out_specs=[blk, pl.BlockSpec((tr, n1), lambda i: (i, 0)), pl.BlockSpec((tr, n - n1), lambda i: (i, 0))],
        out_shape=[jax.ShapeDtypeStruct((rows, D_MODEL), F32), jax.ShapeDtypeStruct((rows, n1), F32),
                   jax.ShapeDtypeStruct((rows, n - n1), F32)],
        compiler_params=_cparams(("parallel",)),
    )(a, gate, h, w_out, g, w_in)


def _norm_proj_bwd(h, g, w, dp1, dp2, dh_in, name, gsends=()):
    rows = h.shape[0]
    nb, _, cb = w.shape
    n1 = dp1.shape[1]
    n2 = nb * cb - n1
    tr = _row_block(rows)
    nsteps = rows // tr
    extra = len(gsends)

    def body(h_ref, g_ref, w_ref, dp1_ref, dp2_ref, dhin_ref, *rest):
        dh_ref, dw_ref, dg_ref = rest[extra:extra + 3]
        dw_acc = rest[2 * extra + 3]
        i = pl.program_id(0)
        start, wait = _rider(gsends, rest[:extra] + rest[extra + 3:2 * extra + 3] + rest[2 * extra + 4:],
                             i == 0, i == nsteps - 1, True)
        start()

        @pl.when(i == 0)
        def _():
            dw_acc[...] = jnp.zeros_like(dw_acc)
            dg_ref[...] = jnp.zeros_like(dg_ref)

        gv = g_ref[...]
        xn, r = _rms_fwd(h_ref[...])
        hn = (xn * gv).astype(BF16)
        dp = jnp.concatenate([dp1_ref[...].astype(BF16), dp2_ref[...].astype(BF16)], axis=1)
        dhn = jnp.zeros((tr, D_MODEL), F32)
        for d in range(nb):
            dpd = dp[:, d * cb:(d + 1) * cb]
            dw_acc[d] += lax.dot_general(hn, dpd, TN, preferred_element_type=F32)
            dhn = dhn + lax.dot_general(dpd, w_ref[d], NT, preferred_element_type=F32)
        dx, dg = _rms_bwd(dhn, xn, r, gv)
        dg_ref[...] += dg
        dh_ref[...] = dhin_ref[...] + dx

        @pl.when(i == nsteps - 1)
        def _():
            dw_ref[...] = dw_acc[...].astype(BF16)

        wait()

    r_in, r_out, r_shape, r_scratch = _rider_specs(gsends, True)
    wblk = pl.BlockSpec((nb, D_MODEL, cb), lambda i: (0, 0, 0))
    return pl.pallas_call(
        body, name=name, grid=(nsteps,),
        in_specs=[pl.BlockSpec((tr, D_MODEL), lambda i: (i, 0)),
                  pl.BlockSpec((1, D_MODEL), lambda i: (0, 0)),
                  wblk,
                  pl.BlockSpec((tr, n1), lambda i: (i, 0)),
                  pl.BlockSpec((tr, n2), lambda i: (i, 0)),
                  pl.BlockSpec((tr, D_MODEL), lambda i: (i, 0))] + r_in,
        out_specs=[pl.BlockSpec((tr, D_MODEL), lambda i: (i, 0)), wblk,
                   pl.BlockSpec((1, D_MODEL), lambda i: (0, 0))] + r_out,
        out_shape=[jax.ShapeDtypeStruct((rows, D_MODEL), F32),
                   jax.ShapeDtypeStruct((nb, D_MODEL, cb), BF16),
                   jax.ShapeDtypeStruct((1, D_MODEL), F32)] + r_shape,
        scratch_shapes=[pltpu.VMEM((nb, D_MODEL, cb), F32)] + r_scratch,
        compiler_params=_cparams(("arbitrary",)),
    )(h, g, w, dp1, dp2, dh_in, *gsends)


def _attn_out_bwd(o, gate, dh, w):
    rows = o.shape[0]
    tr = _row_block(rows)
    nsteps = rows // tr

    def body(o_ref, gate_ref, dh_ref, w_ref, do_ref, dgate_ref, dw_ref, delta_ref, dw_acc):
        i = pl.program_id(0)

        @pl.when(i == 0)
        def _():
            dw_acc[...] = jnp.zeros_like(dw_acc)

        ov, gv = o_ref[...], gate_ref[...]
        sg = _silu(gv)
        dhb = dh_ref[...].astype(BF16)
        dw_acc[...] += lax.dot_general((ov * sg).astype(BF16), dhb, TN, preferred_element_type=F32)
        dy = lax.dot_general(dhb, w_ref[...], NT, preferred_element_type=F32)
        do = (dy * sg).astype(BF16)
        do_ref[...] = do
        dgate_ref[...] = (dy * ov * _dsilu(gv)).astype(BF16)
        prod = do.astype(F32) * ov
        lane = lax.broadcasted_iota(jnp.int32, (tr, LANES), 1)
        per_head = jnp.zeros((tr, LANES), F32)
        for hd in range(HEADS):
            dsum = jnp.sum(prod[:, hd * V_HEAD:(hd + 1) * V_HEAD], axis=1, keepdims=True)
            per_head = jnp.where(lane == hd, dsum, per_head)
        delta_t = per_head.T
        for hd in range(HEADS):
            delta_ref[hd, 0] = delta_t[hd:hd + 1, :]

        @pl.when(i == nsteps - 1)
        def _():
            dw_ref[...] = dw_acc[...].astype(BF16)

    blk = pl.BlockSpec((tr, D_MODEL), lambda i: (i, 0))
    wblk = pl.BlockSpec((D_MODEL, D_MODEL), lambda i: (0, 0))
    return pl.pallas_call(
        body, name="a_out_bwd", grid=(nsteps,),
        in_specs=[blk, blk, blk, wblk],
        out_specs=[blk, blk, wblk, pl.BlockSpec((HEADS, 1, 1, tr), lambda i: (0, i, 0, 0))],
        out_shape=[jax.ShapeDtypeStruct((rows, D_MODEL), BF16), jax.ShapeDtypeStruct((rows, D_MODEL), BF16),
                   jax.ShapeDtypeStruct((D_MODEL, D_MODEL), BF16),
                   jax.ShapeDtypeStruct((HEADS, nsteps, 1, tr), F32)],
        scratch_shapes=[pltpu.VMEM((D_MODEL, D_MODEL), F32)],
        compiler_params=_cparams(("arbitrary",)),
    )(o, gate, dh, w)


def _rope(v, cos, sin, lane):
    swapped = jnp.where(lane < QK_ROPE // 2, pltpu.roll(v, LANES - QK_ROPE // 2, 1), pltpu.roll(v, QK_ROPE // 2, 1))
    return v * cos + swapped * sin


def _unrope(dv, cos, sin, lane):
    t = dv * sin
    swapped = jnp.where(lane < QK_ROPE // 2, pltpu.roll(t, LANES - QK_ROPE // 2, 1), pltpu.roll(t, QK_ROPE // 2, 1))
    return dv * cos + swapped


def _mla_qkv_fwd(lat, gq, gkv, wuq, wukv, cos, sin, scale):
    rows = lat.shape[0]
    tr = _row_block(rows)

    def body(lat_ref, gq_ref, gkv_ref, wuq_ref, wukv_ref, cos_ref, sin_ref, qc_ref, kc_ref, v_ref, vt_ref):
        qn, _ = _rms_fwd(lat_ref[:, :Q_LORA])
        kvn, _ = _rms_fwd(lat_ref[:, Q_LORA:LAT])
        qnb = (qn * gq_ref[...]).astype(BF16)
        kvnb = (kvn * gkv_ref[...]).astype(BF16)
        c, s = cos_ref[...], sin_ref[...]
        lane = lax.broadcasted_iota(jnp.int32, (tr, LANES), 1)
        kr = _rope(lat_ref[:, LAT:LAT_PAD], c, s, lane).astype(BF16)
        for hd in range(HEADS):
            o = hd * HEAD_PAD
            q = jnp.dot(qnb, wuq_ref[hd], preferred_element_type=F32)
            kv = jnp.dot(kvnb, wukv_ref[hd], preferred_element_type=F32)
            qc_ref[:, o:o + QK_NOPE] = (q[:, :QK_NOPE] * scale).astype(BF16)
            qc_ref[:, o + QK_NOPE:o + HEAD_PAD] = (_rope(q[:, QK_NOPE:], c, s, lane) * scale).astype(BF16)
            kc_ref[:, o:o + QK_NOPE] = kv[:, :QK_NOPE].astype(BF16)
            kc_ref[:, o + QK_NOPE:o + HEAD_PAD] = kr
            vh = kv[:, QK_NOPE:]
            v_ref[:, hd * V_HEAD:(hd + 1) * V_HEAD] = vh.astype(BF16)
            vt_ref[hd, 0] = vh.T.astype(BF16)

    full = lambda shape: pl.BlockSpec(shape, lambda i: (0,) * len(shape))
    rowb = lambda n: pl.BlockSpec((tr, n), lambda i: (i, 0))
    return pl.pallas_call(
        body, name="mla_qkv_fwd", grid=(rows // tr,),
        in_specs=[rowb(LAT_PAD), full((1, Q_LORA)), full((1, KV_LORA)), full((HEADS, Q_LORA, HEAD_PAD)),
                  full((HEADS, KV_LORA, HEAD_PAD)), rowb(LANES), rowb(LANES)],
        out_specs=[rowb(HEADS * HEAD_PAD), rowb(HEADS * HEAD_PAD), rowb(HEADS * V_HEAD),
                   pl.BlockSpec((HEADS, 1, V_HEAD, tr), lambda i: (0, i, 0, 0))],
        out_shape=[jax.ShapeDtypeStruct((rows, HEADS * HEAD_PAD), BF16),
                   jax.ShapeDtypeStruct((rows, HEADS * HEAD_PAD), BF16),
                   jax.ShapeDtypeStruct((rows, HEADS * V_HEAD), BF16),
                   jax.ShapeDtypeStruct((HEADS, rows // tr, V_HEAD, tr), BF16)],
        compiler_params=_cparams(("parallel",)),
    )(lat, gq, gkv, wuq, wukv, cos, sin)


def _mla_qkv_bwd(lat, gq, gkv, wuq, wukv, cos, sin, dqc, dkc, dv, scale):
    rows = lat.shape[0]
    tr = _row_block(rows)
    nsteps = rows // tr

    def body(lat_ref, gq_ref, gkv_ref, wuq_ref, wukv_ref, cos_ref, sin_ref, dqc_ref, dkc_ref, dv_ref,
             dlat_ref, dwuq_out, dwukv_out, dgq_ref, dgkv_ref, dwuq_ref, dwukv_ref):
        @pl.when(pl.program_id(0) == 0)
        def _():
            dwuq_ref[...] = jnp.zeros_like(dwuq_ref)
            dwukv_ref[...] = jnp.zeros_like(dwukv_ref)
            dgq_ref[...] = jnp.zeros_like(dgq_ref)
            dgkv_ref[...] = jnp.zeros_like(dgkv_ref)

        c, s = cos_ref[...], sin_ref[...]
        lane = lax.broadcasted_iota(jnp.int32, (tr, LANES), 1)
        gqv, gkvv = gq_ref[...], gkv_ref[...]
        qn, rq = _rms_fwd(lat_ref[:, :Q_LORA])
        kvn, rkv = _rms_fwd(lat_ref[:, Q_LORA:LAT])
        qnb = (qn * gqv).astype(BF16)
        kvnb = (kvn * gkvv).astype(BF16)
        dkr = jnp.zeros((tr, LANES), F32)
        dqn = jnp.zeros((tr, Q_LORA), F32)
        dkvn = jnp.zeros((tr, KV_LORA), F32)
        for hd in range(HEADS):
            o = hd * HEAD_PAD
            dq = jnp.concatenate(
                [dqc_ref[:, o:o + QK_NOPE],
                 _unrope(dqc_ref[:, o + QK_NOPE:o + HEAD_PAD].astype(F32), c, s, lane).astype(BF16)], axis=1)
            dkv = jnp.concatenate([dkc_ref[:, o:o + QK_NOPE], dv_ref[:, hd * V_HEAD:(hd + 1) * V_HEAD]], axis=1)
            dkr = dkr + dkc_ref[:, o + QK_NOPE:o + HEAD_PAD].astype(F32)
            dwuq_ref[hd] += scale * lax.dot_general(qnb, dq, TN, preferred_element_type=F32)
            dwukv_ref[hd] += lax.dot_general(kvnb, dkv, TN, preferred_element_type=F32)
            dqn = dqn + lax.dot_general(dq, wuq_ref[hd], NT, preferred_element_type=F32)
            dkvn = dkvn + lax.dot_general(dkv, wukv_ref[hd], NT, preferred_element_type=F32)
        dqn = scale * dqn
        dqlat, dgq = _rms_bwd(dqn, qn, rq, gqv)
        dkvlat, dgkv = _rms_bwd(dkvn, kvn, rkv, gkvv)
        dgq_ref[...] += dgq
        dgkv_ref[...] += dgkv
        dlat_ref[:, :Q_LORA] = dqlat.astype(BF16)
        dlat_ref[:, Q_LORA:LAT] = dkvlat.astype(BF16)
        dlat_ref[:, LAT:LAT_PAD] = _unrope(dkr, c, s, lane).astype(BF16)

        @pl.when(pl.program_id(0) == nsteps - 1)
        def _():
            dwuq_out[...] = dwuq_ref[...].astype(BF16)
            dwukv_out[...] = dwukv_ref[...].astype(BF16)

    full = lambda shape: pl.BlockSpec(shape, lambda i: (0,) * len(shape))
    rowb = lambda n: pl.BlockSpec((tr, n), lambda i: (i, 0))
    return pl.pallas_call(
        body, name="mla_qkv_bwd", grid=(nsteps,),
        in_specs=[rowb(LAT_PAD), full((1, Q_LORA)), full((1, KV_LORA)), full((HEADS, Q_LORA, HEAD_PAD)),
                  full((HEADS, KV_LORA, HEAD_PAD)), rowb(LANES), rowb(LANES),
                  rowb(HEADS * HEAD_PAD), rowb(HEADS * HEAD_PAD), rowb(HEADS * V_HEAD)],
        out_specs=[rowb(LAT_PAD), full((HEADS, Q_LORA, HEAD_PAD)), full((HEADS, KV_LORA, HEAD_PAD)),
                   full((1, Q_LORA)), full((1, KV_LORA))],
        out_shape=[jax.ShapeDtypeStruct((rows, LAT_PAD), BF16),
                   jax.ShapeDtypeStruct((HEADS, Q_LORA, HEAD_PAD), BF16),
                   jax.ShapeDtypeStruct((HEADS, KV_LORA, HEAD_PAD), BF16),
                   jax.ShapeDtypeStruct((1, Q_LORA), F32),
                   jax.ShapeDtypeStruct((1, KV_LORA), F32)],
        scratch_shapes=[pltpu.VMEM((HEADS, Q_LORA, HEAD_PAD), F32), pltpu.VMEM((HEADS, KV_LORA, HEAD_PAD), F32)],
        compiler_params=_cparams(("arbitrary",)),
    )(lat, gq, gkv, wuq, wukv, cos, sin, dqc, dkc, dv)


ATTN_UNROLL = 4


def _causal_mask_t(t):
    key = lax.broadcasted_iota(jnp.int32, (t, t), 0)
    query = lax.broadcasted_iota(jnp.int32, (t, t), 1)
    return key <= query


def _attn_fwd(qc, kc, vt, wsends):
    rows = qc.shape[0]
    t = _row_block(rows)
    nblk = rows // t
    nw = len(wsends)

    def body(q_ref, k_ref, vt_ref, *rest):
        o_ref, lse_ref = rest[nw:nw + 2]
        m_ref, l_ref, acc_ref, st_a, st_b = rest[2 * nw + 2:2 * nw + 7]
        i = pl.program_id(1)
        start, wait = _rider(wsends, rest[:nw] + rest[nw + 2:2 * nw + 2] + rest[2 * nw + 7:],
                             jnp.logical_and(pl.program_id(0) == 0, i == 0),
                             jnp.logical_and(pl.program_id(0) == HEADS - 1, i == nblk - 1), False)
        start()

        m_ref[...] = jnp.full_like(m_ref, MASK_VALUE)
        l_ref[...] = jnp.zeros_like(l_ref)
        acc_ref[...] = jnp.zeros_like(acc_ref)
        q = q_ref[...]

        def scores(j, st_ref):
            rs = pl.ds(pl.multiple_of(j * t, t), t)
            st_ref[...] = lax.dot_general(k_ref[rs, :], q, NT, preferred_element_type=F32)

        def consume(j, st_ref, masked):
            st = st_ref[...]
            if masked:
                st = jnp.where(_causal_mask_t(t), st, MASK_VALUE)
            m_prev = m_ref[...]
            m_new = jnp.maximum(m_prev, jnp.max(st, axis=0, keepdims=True))
            alpha = jnp.exp(m_prev - m_new)
            pt = jnp.exp(st - m_new)
            l_ref[...] = alpha * l_ref[...] + jnp.sum(pt, axis=0, keepdims=True)
            acc_ref[...] = alpha * acc_ref[...] + jnp.dot(vt_ref[0, j], pt.astype(BF16), preferred_element_type=F32)
            m_ref[...] = m_new

        bufs = (st_a, st_b)

        def step(j, parity, issue_next, masked):
            if issue_next:
                scores(j + 1, bufs[1 - parity])
            consume(j, bufs[parity], masked)

        scores(0, st_a)

        def trip(it, carry):
            for u in range(ATTN_UNROLL):
                step(it * ATTN_UNROLL + u, u % 2, True, False)
            return carry

        trips = i // ATTN_UNROLL
        lax.fori_loop(0, trips, trip, 0)
        j0 = trips * ATTN_UNROLL
        for left in range(1, ATTN_UNROLL + 1):
            @pl.when(i + 1 - j0 == left)
            def _(left=left):
                for u in range(left):
                    step(j0 + u, u % 2, u < left - 1, u == left - 1)

        o_ref[...] = (acc_ref[...] / l_ref[...]).T
        lse_ref[0, 0] = m_ref[...] + jnp.log(l_ref[...])
        wait()

    r_in, r_out, r_shape, r_scratch = _rider_specs(wsends, False)
    return pl.pallas_call(
        body, name="attn_fwd", grid=(HEADS, nblk),
        in_specs=[pl.BlockSpec((t, HEAD_PAD), lambda h, i: (i, h)),
                  pl.BlockSpec((rows, HEAD_PAD), lambda h, i: (0, h)),
                  pl.BlockSpec((1, nblk, V_HEAD, t), lambda h, i: (h, 0, 0, 0))] + r_in,
        out_specs=[pl.BlockSpec((t, V_HEAD), lambda h, i: (i, h)),
                   pl.BlockSpec((1, 1, 1, t), lambda h, i: (h, i, 0, 0))] + r_out,
        out_shape=[jax.ShapeDtypeStruct((rows, HEADS * V_HEAD), F32),
                   jax.ShapeDtypeStruct((HEADS, nblk, 1, t), F32)] + r_shape,
        scratch_shapes=[pltpu.VMEM((1, t), F32), pltpu.VMEM((1, t), F32), pltpu.VMEM((V_HEAD, t), F32),
                        pltpu.VMEM((t, t), F32), pltpu.VMEM((t, t), F32)] + r_scratch,
        compiler_params=_cparams(("arbitrary", "arbitrary")),
    )(qc, kc, vt, *wsends)


def _attn_bwd(qc, kc, v, lse, delta, do, gsends):
    rows = qc.shape[0]
    t = _row_block(rows)
    nblk = rows // t
    ng = len(gsends)

    def body(q_ref, k_ref, v_ref, lse_ref, delta_ref, do_ref, *rest):
        dq_ref, dk_ref, dv_ref = rest[ng:ng + 3]
        dq_acc, dk_acc, dv_acc, st_a, dp_a, st_b, dp_b = rest[2 * ng + 3:2 * ng + 10]
        j = pl.program_id(1)
        start, wait = _rider(gsends, rest[:ng] + rest[ng + 3:2 * ng + 3] + rest[2 * ng + 10:],
                             jnp.logical_and(pl.program_id(0) == 0, j == 0),
                             jnp.logical_and(pl.program_id(0) == HEADS - 1, j == nblk - 1), True)
        start()

        @pl.when(j == 0)
        def _():
            dq_acc[...] = jnp.zeros_like(dq_acc)

        dk_acc[...] = jnp.zeros_like(dk_acc)
        dv_acc[...] = jnp.zeros_like(dv_acc)
        k = k_ref[...]
        vv = v_ref[...]

        def products(i, st_ref, dp_ref):
            rs = pl.ds(pl.multiple_of(i * t, t), t)
            st_ref[...] = lax.dot_general(k, q_ref[rs, :], NT, preferred_element_type=F32)
            dp_ref[...] = lax.dot_general(vv, do_ref[rs, :], NT, preferred_element_type=F32)

        def consume(i, st_ref, dp_ref):
            rs = pl.ds(pl.multiple_of(i * t, t), t)
            q = q_ref[rs, :]
            dob = do_ref[rs, :]
            st = jnp.where(jnp.logical_or(_causal_mask_t(t), i != j), st_ref[...], MASK_VALUE)
            pt = jnp.exp(st - lse_ref[0, i])
            dv_acc[...] += jnp.dot(pt.astype(BF16), dob, preferred_element_type=F32)
            dst = (pt * (dp_ref[...] - delta_ref[0, i])).astype(BF16)
            dk_acc[...] += jnp.dot(dst, q, preferred_element_type=F32)
            dq_acc[rs, :] += lax.dot_general(dst, k, TN, preferred_element_type=F32)

        bufs = ((st_a, dp_a), (st_b, dp_b))

        def step(i, parity, issue_next):
            if issue_next:
                products(i + 1, *bufs[1 - parity])
            consume(i, *bufs[parity])

        products(j, st_a, dp_a)

        def trip(it, carry):
            for u in range(ATTN_UNROLL):
                step(j + it * ATTN_UNROLL + u, u % 2, True)
            return carry

        trips = (nblk - 1 - j) // ATTN_UNROLL
        lax.fori_loop(0, trips, trip, 0)
        i0 = j + trips * ATTN_UNROLL
        for left in range(1, ATTN_UNROLL + 1):
            @pl.when(nblk - i0 == left)
            def _(left=left):
                for u in range(left):
                    step(i0 + u, u % 2, u < left - 1)

        dk_ref[...] = dk_acc[...].astype(BF16)
        dv_ref[...] = dv_acc[...].astype(BF16)

        @pl.when(j == nblk - 1)
        def _():
            dq_ref[...] = dq_acc[...].astype(BF16)

        wait()

    stat = pl.BlockSpec((1, nblk, 1, t), lambda h, j: (h, 0, 0, 0))
    r_in, r_out, r_shape, r_scratch = _rider_specs(gsends, True)
    return pl.pallas_call(
        body, name="attn_bwd", grid=(HEADS, nblk),
        in_specs=[pl.BlockSpec((rows, HEAD_PAD), lambda h, j: (0, h)),
                  pl.BlockSpec((t, HEAD_PAD), lambda h, j: (j, h)),
                  pl.BlockSpec((t, V_HEAD), lambda h, j: (j, h)),
                  stat, stat,
                  pl.BlockSpec((rows, V_HEAD), lambda h, j: (0, h))] + r_in,
        out_specs=[pl.BlockSpec((rows, HEAD_PAD), lambda h, j: (0, h)),
                   pl.BlockSpec((t, HEAD_PAD), lambda h, j: (j, h)),
                   pl.BlockSpec((t, V_HEAD), lambda h, j: (j, h))] + r_out,
        out_shape=[jax.ShapeDtypeStruct((rows, HEADS * HEAD_PAD), BF16),
                   jax.ShapeDtypeStruct((rows, HEADS * HEAD_PAD), BF16),
                   jax.ShapeDtypeStruct((rows, HEADS * V_HEAD), BF16)] + r_shape,
        scratch_shapes=[pltpu.VMEM((rows, HEAD_PAD), F32), pltpu.VMEM((t, HEAD_PAD), F32),
                        pltpu.VMEM((t, V_HEAD), F32)] + [pltpu.VMEM((t, t), F32)] * 4 + r_scratch,
        compiler_params=_cparams(("arbitrary", "arbitrary")),
    )(qc, kc, v, lse, delta, do, *gsends)


def _shift_down(prev_tile, x, k):
    xx = jnp.concatenate([prev_tile, x], axis=0)
    return pltpu.roll(xx, k, 0)[SUBLANES:]


def _shift_up(x, next_tile, k):
    n = x.shape[0]
    xx = jnp.concatenate([x, next_tile], axis=0)
    return pltpu.roll(xx, n + SUBLANES - k, 0)[:n]


def _lru_gates(u, u_prev, cw_ref, cb_ref, wrg_ref, brg_ref, wig_ref, big_ref, lam_ref, first_block):
    taps = [_shift_down(u_prev, u, CONV_WIDTH - 1 - j) if j < CONV_WIDTH - 1 else u for j in range(CONV_WIDTH)]
    uc = cb_ref[...] + taps[0] * cw_ref[0:1, :]
    for j in range(1, CONV_WIDTH):
        uc = uc + taps[j] * cw_ref[j:j + 1, :]
    ub = uc.astype(BF16)
    zr = jnp.concatenate([jnp.dot(ub[:, g * LRU_BLOCK:(g + 1) * LRU_BLOCK], wrg_ref[g], preferred_element_type=F32)
                          for g in range(LRU_BLOCKS)], axis=1) + brg_ref[...]
    zi = jnp.concatenate([jnp.dot(ub[:, g * LRU_BLOCK:(g + 1) * LRU_BLOCK], wig_ref[g], preferred_element_type=F32)
                          for g in range(LRU_BLOCKS)], axis=1) + big_ref[...]
    r = jax.nn.sigmoid(zr)
    ig = jax.nn.sigmoid(zi)
    sp = _softplus_neg(lam_ref[...])
    log_a = (-LRU_C) * r * sp
    a = jnp.exp(log_a)
    m2 = -_expm1_neg(2.0 * log_a)
    mult_raw = m2 * lax.rsqrt(jnp.maximum(m2, 1e-30))
    row = lax.broadcasted_iota(jnp.int32, u.shape, 0)
    is_start = jnp.logical_and(first_block, row == 0)
    mult = jnp.where(is_start, 1.0, mult_raw)
    return dict(taps=taps, uc=uc, ub=ub, r=r, ig=ig, sp=sp, a=a, mult=mult, mult_raw=mult_raw, is_start=is_start)


def _rglru_fwd(u, cw, cb, wrg, brg, wig, big, lam):
    rows = u.shape[0]
    tb = _row_block(rows)

    def body(u_ref, cw_ref, cb_ref, wrg_ref, brg_ref, wig_ref, big_ref, lam_ref, hs_ref, utail, hcar, a_s, b_s):
        i = pl.program_id(0)

        @pl.when(i == 0)
        def _():
            utail[...] = jnp.zeros_like(utail)
            hcar[...] = jnp.zeros_like(hcar)

        u = u_ref[...]
        gt = _lru_gates(u, utail[...], cw_ref, cb_ref, wrg_ref, brg_ref, wig_ref, big_ref, lam_ref, i == 0)
        a_s[...] = gt["a"]
        b_s[...] = gt["mult"] * (gt["ig"] * gt["uc"])
        row8 = lax.broadcasted_iota(jnp.int32, (SUBLANES, LRU_WIDTH), 0)

        def tile(tix, carry):
            rs = pl.ds(pl.multiple_of(tix * SUBLANES, SUBLANES), SUBLANES)
            av, bv = a_s[rs, :], b_s[rs, :]
            for k in (1, 2, 4):
                keep = row8 >= k
                bv = jnp.where(keep, av * pltpu.roll(bv, k, 0) + bv, bv)
                av = jnp.where(keep, av * pltpu.roll(av, k, 0), av)
            h8 = av * carry + bv
            hs_ref[rs, :] = h8
            return jnp.broadcast_to(h8[SUBLANES - 1:SUBLANES, :], (SUBLANES, LRU_WIDTH))

        hcar[...] = lax.fori_loop(0, tb // SUBLANES, tile, hcar[...])
        utail[...] = u[tb - SUBLANES:, :]

    full2 = lambda shape: pl.BlockSpec(shape, lambda i: (0, 0))
    full3 = lambda shape: pl.BlockSpec(shape, lambda i: (0, 0, 0))
    blk = pl.BlockSpec((tb, LRU_WIDTH), lambda i: (i, 0))
    return pl.pallas_call(
        body, name="rglru_fwd", grid=(rows // tb,),
        in_specs=[blk, full2((CONV_WIDTH, LRU_WIDTH)), full2((1, LRU_WIDTH)),
                  full3((LRU_BLOCKS, LRU_BLOCK, LRU_BLOCK)), full2((1, LRU_WIDTH)),
                  full3((LRU_BLOCKS, LRU_BLOCK, LRU_BLOCK)), full2((1, LRU_WIDTH)), full2((1, LRU_WIDTH))],
        out_specs=blk,
        out_shape=jax.ShapeDtypeStruct((rows, LRU_WIDTH), F32),
        scratch_shapes=[pltpu.VMEM((SUBLANES, LRU_WIDTH), F32), pltpu.VMEM((SUBLANES, LRU_WIDTH), F32),
                        pltpu.VMEM((tb, LRU_WIDTH), F32), pltpu.VMEM((tb, LRU_WIDTH), F32)],
        compiler_params=_cparams(("arbitrary",)),
    )(u, cw, cb, wrg, brg, wig, big, lam)


def _rglru_bwd(u, hs, dhs, cw, cb, wrg, brg, wig, big, lam):
    rows = u.shape[0]
    tb = _row_block(rows)
    nblk = rows // tb
    tiles = tb // SUBLANES

    def body(u_ref, up_ref, hs_ref, hp_ref, dhs_ref, cw_ref, cb_ref, wrg_ref, brg_ref, wig_ref, big_ref, lam_ref,
             du_ref, dcw_ref, dcb_ref, dwrg_ref, dbrg_ref, dwig_ref, dbig_ref, dlam_ref,
             gcar, duc_head, a_s, b_s, g_s, dsp_acc):
        step = pl.program_id(0)
        blk_ix = nblk - 1 - step

        @pl.when(step == 0)
        def _():
            for ref in (dcw_ref, dcb_ref, dwrg_ref, dbrg_ref, dwig_ref, dbig_ref, gcar, duc_head, dsp_acc):
                ref[...] = jnp.zeros_like(ref)

        first = blk_ix == 0
        u = u_ref[...]
        u_prev = jnp.where(first, 0.0, up_ref[...])
        h_prev_tile = jnp.where(first, 0.0, hp_ref[...])
        gt = _lru_gates(u, u_prev, cw_ref, cb_ref, wrg_ref, brg_ref, wig_ref, big_ref, lam_ref, first)
        a, r, ig, uc, mult = gt["a"], gt["r"], gt["ig"], gt["uc"], gt["mult"]
        dhs_v = dhs_ref[...]

        a_s[...] = a
        b_s[...] = a * dhs_v
        row8 = lax.broadcasted_iota(jnp.int32, (SUBLANES, LRU_WIDTH), 0)

        def tile(tix, carry):
            rs = pl.ds(pl.multiple_of((tiles - 1 - tix) * SUBLANES, SUBLANES), SUBLANES)
            av, bv = a_s[rs, :], b_s[rs, :]
            for k in (1, 2, 4):
                keep = row8 < SUBLANES - k
                bv = jnp.where(keep, av * pltpu.roll(bv, SUBLANES - k, 0) + bv, bv)
                av = jnp.where(keep, av * pltpu.roll(av, SUBLANES - k, 0), av)
            g8 = av * carry + bv
            g_s[rs, :] = g8
            return jnp.broadcast_to(g8[0:1, :], (SUBLANES, LRU_WIDTH))

        g_next = gcar[...]
        gcar[...] = lax.fori_loop(0, tiles, tile, g_next)
        g = dhs_v + _shift_up(g_s[...], g_next, 1)

        h_prev = _shift_down(h_prev_tile, hs_ref[...], 1)
        da = g * h_prev
        iu = ig * uc
        dmult = jnp.where(gt["is_start"], 0.0, g * iu)
        d_ig = g * mult * uc
        duc = g * mult * ig
        dlog_a = da * a - dmult * (a * a) / gt["mult_raw"]
        dzr = (dlog_a * ((-LRU_C) * gt["sp"])) * r * (1.0 - r)
        dsp_acc[...] += jnp.sum(dlog_a * ((-LRU_C) * r), axis=0, keepdims=True)
        dzi = d_ig * ig * (1.0 - ig)
        dbrg_ref[...] += jnp.sum(dzr, axis=0, keepdims=True)
        dbig_ref[...] += jnp.sum(dzi, axis=0, keepdims=True)
        dzr_b, dzi_b = dzr.astype(BF16), dzi.astype(BF16)
        ub = gt["ub"]
        duc_parts = []
        for gi in range(LRU_BLOCKS):
            cs = slice(gi * LRU_BLOCK, (gi + 1) * LRU_BLOCK)
            dwrg_ref[gi] += lax.dot_general(ub[:, cs], dzr_b[:, cs], TN, preferred_element_type=F32)
            dwig_ref[gi] += lax.dot_general(ub[:, cs], dzi_b[:, cs], TN, preferred_element_type=F32)
            duc_parts.append(lax.dot_general(dzr_b[:, cs], wrg_ref[gi], NT, preferred_element_type=F32)
                             + lax.dot_general(dzi_b[:, cs], wig_ref[gi], NT, preferred_element_type=F32))
        duc = duc + jnp.concatenate(duc_parts, axis=1)

        dcb_ref[...] += jnp.sum(duc, axis=0, keepdims=True)
        taps = gt["taps"]
        for jt in range(CONV_WIDTH):
            dcw_ref[jt:jt + 1, :] += jnp.sum(duc * taps[jt], axis=0, keepdims=True)
        head = duc_head[...]
        du = duc * cw_ref[CONV_WIDTH - 1:CONV_WIDTH, :]
        for jt in range(CONV_WIDTH - 1):
            du = du + _shift_up(duc, head, CONV_WIDTH - 1 - jt) * cw_ref[jt:jt + 1, :]
        du_ref[...] = du.astype(BF16)
        duc_head[...] = duc[:SUBLANES, :]

        @pl.when(step == nblk - 1)
        def _():
            dlam_ref[...] = -dsp_acc[...] * jax.nn.sigmoid(-lam_ref[...])

    full2 = lambda shape: pl.BlockSpec(shape, lambda s: (0, 0))
    full3 = lambda shape: pl.BlockSpec(shape, lambda s: (0, 0, 0))
    blk = pl.BlockSpec((tb, LRU_WIDTH), lambda s: (nblk - 1 - s, 0))
    prev_tile = pl.BlockSpec((SUBLANES, LRU_WIDTH), lambda s: (jnp.maximum((nblk - 1 - s) * tiles - 1, 0), 0))
    wshape = (LRU_BLOCKS, LRU_BLOCK, LRU_BLOCK)
    return pl.pallas_call(
        body, name="rglru_bwd", grid=(nblk,),
        in_specs=[blk, prev_tile, blk, prev_tile, blk, full2((CONV_WIDTH, LRU_WIDTH)), full2((1, LRU_WIDTH)),
                  full3(wshape), full2((1, LRU_WIDTH)), full3(wshape), full2((1, LRU_WIDTH)), full2((1, LRU_WIDTH))],
        out_specs=[blk, full2((CONV_WIDTH, LRU_WIDTH)), full2((1, LRU_WIDTH)), full3(wshape), full2((1, LRU_WIDTH)),
                   full3(wshape), full2((1, LRU_WIDTH)), full2((1, LRU_WIDTH))],
        out_shape=[jax.ShapeDtypeStruct((rows, LRU_WIDTH), BF16),
                   jax.ShapeDtypeStruct((CONV_WIDTH, LRU_WIDTH), F32), jax.ShapeDtypeStruct((1, LRU_WIDTH), F32),
                   jax.ShapeDtypeStruct(wshape, F32), jax.ShapeDtypeStruct((1, LRU_WIDTH), F32),
                   jax.ShapeDtypeStruct(wshape, F32), jax.ShapeDtypeStruct((1, LRU_WIDTH), F32),
                   jax.ShapeDtypeStruct((1, LRU_WIDTH), F32)],
        scratch_shapes=[pltpu.VMEM((SUBLANES, LRU_WIDTH), F32), pltpu.VMEM((SUBLANES, LRU_WIDTH), F32),
                        pltpu.VMEM((tb, LRU_WIDTH), F32), pltpu.VMEM((tb, LRU_WIDTH), F32),
                        pltpu.VMEM((tb, LRU_WIDTH), F32), pltpu.VMEM((1, LRU_WIDTH), F32)],
        compiler_params=_cparams(("arbitrary",)),
    )(u, u, hs, hs, dhs, cw, cb, wrg, brg, wig, big, lam)


def _out_proj_loss(a, gate, h, w, gf, target, n_real):
    rows = h.shape[0]
    tr = _row_block(rows)

    def body(a_ref, gate_ref, h_ref, w_ref, g_ref, t_ref, dh_ref, loss_ref, dg_ref, da_ref, dgate_ref, dw_ref,
             dw_acc):
        i = pl.program_id(0)

        @pl.when(i == 0)
        def _():
            loss_ref[...] = jnp.zeros_like(loss_ref)
            dg_ref[...] = jnp.zeros_like(dg_ref)
            dw_acc[...] = jnp.zeros_like(dw_acc)

        gv = g_ref[...]
        av, gatev = a_ref[...], gate_ref[...]
        sg = _silu(gatev)
        y = (av * sg).astype(BF16)
        xn, r = _rms_fwd(h_ref[...] + jnp.dot(y, w_ref[...], preferred_element_type=F32))
        row = i * tr + lax.broadcasted_iota(jnp.int32, (tr, 1), 0)
        live = jnp.logical_and(row >= N_META, row < n_real)
        tgt = t_ref[...]
        tgt = jnp.where(i == 0, pltpu.roll(tgt, N_META, 0), tgt)
        err = jnp.where(live, xn * gv - tgt, 0.0)
        loss_ref[...] += (0.5 / D_MODEL) * jnp.sum(jnp.sum(err * err, axis=1, keepdims=True), axis=0, keepdims=True)
        dx, dg = _rms_bwd(err * (1.0 / D_MODEL), xn, r, gv)
        dg_ref[...] += dg
        dh_ref[...] = dx
        dhb = dx.astype(BF16)
        dw_acc[...] += lax.dot_general(y, dhb, TN, preferred_element_type=F32)
        dy = lax.dot_general(dhb, w_ref[...], NT, preferred_element_type=F32)
        da_ref[...] = dy * sg
        dgate_ref[...] = (dy * av * _dsilu(gatev)).astype(BF16)

        @pl.when(i == rows // tr - 1)
        def _():
            dw_ref[...] = dw_acc[...].astype(BF16)

    blk = pl.BlockSpec((tr, D_MODEL), lambda i: (i, 0))
    wblk = pl.BlockSpec((D_MODEL, D_MODEL), lambda i: (0, 0))
    window = pl.BlockSpec((pl.Element(tr, (0, rows - n_real)), pl.Element(D_MODEL)),
                          lambda i: (pl.multiple_of(jnp.maximum(i * tr - N_META, 0), SUBLANES), 0))
    return pl.pallas_call(
        body, name="b_out_loss", grid=(rows // tr,),
        in_specs=[blk, blk, blk, wblk, pl.BlockSpec((1, D_MODEL), lambda i: (0, 0)), window],
        out_specs=[blk, pl.BlockSpec((1, 1), lambda i: (0, 0)), pl.BlockSpec((1, D_MODEL), lambda i: (0, 0)),
                   blk, blk, wblk],
        out_shape=[jax.ShapeDtypeStruct((rows, D_MODEL), F32), jax.ShapeDtypeStruct((1, 1), F32),
                   jax.ShapeDtypeStruct((1, D_MODEL), F32), jax.ShapeDtypeStruct((rows, D_MODEL), F32),
                   jax.ShapeDtypeStruct((rows, D_MODEL), BF16), jax.ShapeDtypeStruct((D_MODEL, D_MODEL), BF16)],
        scratch_shapes=[pltpu.VMEM((D_MODEL, D_MODEL), F32)],
        compiler_params=_cparams(("arbitrary",)),
    )(a, gate, h, w, gf, target)


def _my_place():
    x, y, c = lax.axis_index("x"), lax.axis_index("y"), lax.axis_index("c")
    return x, y, c, 4 * x + 2 * y + c


def _peer(x, y, c, k):
    px, py, pc = x ^ (k >> 2), y ^ ((k >> 1) & 1), c ^ (k & 1)
    return (px, py, pc), 4 * px + 2 * py + pc


def _exchange_copies(src_of, dst_ref, send_sems, recv_sems, local_sem):
    x, y, c, me = _my_place()
    copies = [pltpu.make_async_copy(src_of(me), dst_ref.at[me], local_sem)]
    for k in range(1, N_DEV):
        peer, pid = _peer(x, y, c, k)
        copies.append(pltpu.make_async_remote_copy(
            src_ref=src_of(pid), dst_ref=dst_ref.at[me], send_sem=send_sems.at[k], recv_sem=recv_sems.at[k],
            device_id=peer, device_id_type=MESH))
    return copies


def _exchange_sems(nb):
    return [pltpu.SemaphoreType.DMA((nb, N_DEV)), pltpu.SemaphoreType.DMA((nb, N_DEV)), pltpu.SemaphoreType.DMA((nb,))]


def _sum_blocks(lands, name):
    n = len(lands)

    def body(*refs):
        for land_ref, out_ref in zip(refs[:n], refs[n:]):
            acc = land_ref[0].astype(F32)
            for d in range(1, N_DEV):
                acc = acc + land_ref[d].astype(F32)
            out_ref[...] = acc

    return pl.pallas_call(
        body, name=name, out_shape=[jax.ShapeDtypeStruct(l.shape[1:], F32) for l in lands],
        compiler_params=pltpu.CompilerParams(vmem_limit_bytes=VMEM_LIMIT),
    )(*lands)


def _all_gather(big, small):
    def body(big_ref, small_ref, obig_ref, osmall_ref, send_sems, recv_sems, local_sems):
        x, y, c, _ = _my_place()
        me, sibling = (x, y, c), (x, y, 1 - c)
        chips = [(1 - x, y), (x, 1 - y), (1 - x, 1 - y)]
        parts = ((big_ref, obig_ref), (small_ref, osmall_ref))

        def slot(dst, place):
            return dst.at[4 * place[0] + 2 * place[1] + place[2]]

        def copy(part, k, block, to, first_hand=False):
            src, dst = parts[part]
            return pltpu.make_async_remote_copy(
                src_ref=src if first_hand else slot(dst, block), dst_ref=slot(dst, block),
                send_sem=send_sems.at[part, k], recv_sem=recv_sems.at[part, k], device_id=to, device_id_type=MESH)

        own = [pltpu.make_async_copy(src, slot(dst, me), local_sems.at[part]) for part, (src, dst) in enumerate(parts)]
        for cp in own:
            cp.start()
        first = []
        for part in range(len(parts)):
            first.append(copy(part, 0, me, sibling, True))
            first += [copy(part, 1 + j, me, (*chip, c), True) for j, chip in enumerate(chips)]
        for cp in first:
            cp.start()
        passed = []
        for j, chip in enumerate(chips):
            for part in range(len(parts)):
                copy(part, 1 + j, (*chip, c), me).wait_recv()
                passed.append(copy(part, 4 + j, (*chip, c), sibling))
                passed[-1].start()
        for part in range(len(parts)):
            copy(part, 0, sibling, me).wait_recv()
            for j, chip in enumerate(chips):
                copy(part, 4 + j, (*chip, 1 - c), me).wait_recv()
        for cp in first + passed:
            cp.wait_send()
        for cp in own:
            cp.wait()

    n = big.shape[0]
    hbm = pl.BlockSpec(memory_space=pl.ANY)
    return pl.pallas_call(
        body, name="weight_all_gather",
        in_specs=[hbm, hbm], out_specs=[hbm, hbm],
        out_shape=[jax.ShapeDtypeStruct((N_DEV,) + big.shape, BF16), jax.ShapeDtypeStruct((N_DEV,) + small.shape, F32)],
        scratch_shapes=[pltpu.SemaphoreType.DMA((2, N_DEV)), pltpu.SemaphoreType.DMA((2, N_DEV)),
                        pltpu.SemaphoreType.DMA((2,))],
        compiler_params=pltpu.CompilerParams(has_side_effects=True),
    )(big, small)


GRAD_CHUNK = 32


def _grad_exchange(gbig, rep):
    n, width = gbig.shape[1:]
    nrep = rep.shape[0]
    n_chips = N_DEV // 2

    def body(gbig_ref, rep_ref, out_ref, orep_ref, pre, stage, got, own_sum, land_rep, send_sems, recv_sems,
             local_sem):
        x, y, c, me = _my_place()
        my_chip = 2 * x + y
        sibling = (x, y, 1 - c)

        local = pltpu.make_async_copy(rep_ref, land_rep.at[me], local_sem.at[0])
        local.start()
        rep_copies = []
        for k in range(1, N_DEV):
            peer, _ = _peer(x, y, c, k)
            rep_copies.append(pltpu.make_async_remote_copy(
                src_ref=rep_ref, dst_ref=land_rep.at[me], send_sem=send_sems.at[6 + k], recv_sem=recv_sems.at[6 + k],
                device_id=peer, device_id_type=MESH))
        swaps = [pltpu.make_async_remote_copy(
            src_ref=gbig_ref.at[2 * q + (1 - c)], dst_ref=pre.at[q], send_sem=send_sems.at[q], recv_sem=recv_sems.at[q],
            device_id=sibling, device_id_type=MESH) for q in range(n_chips)]
        for cp in rep_copies + swaps:
            cp.start()
        for cp in swaps:
            cp.wait_recv()

        def pair_sums(ci, carry):
            rs = pl.ds(pl.multiple_of(ci * GRAD_CHUNK, GRAD_CHUNK), GRAD_CHUNK)
            for q in range(n_chips):
                stage[q, rs, :] = (gbig_ref[2 * q + c, rs, :].astype(F32) + pre[q, rs, :].astype(F32)).astype(BF16)
            own_sum[rs, :] = gbig_ref[me, rs, :].astype(F32) + pre[my_chip, rs, :].astype(F32)
            return carry

        lax.fori_loop(0, n // GRAD_CHUNK, pair_sums, 0)

        hops = []
        for rel in range(1, n_chips):
            qx, qy = x ^ (rel >> 1), y ^ (rel & 1)
            hops.append(pltpu.make_async_remote_copy(
                src_ref=stage.at[2 * qx + qy], dst_ref=got.at[my_chip], send_sem=send_sems.at[3 + rel],
                recv_sem=recv_sems.at[3 + rel], device_id=(qx, qy, c), device_id_type=MESH))
        for cp in hops:
            cp.start()
        for cp in hops:
            cp.wait_recv()

        def chip_sums(ci, carry):
            rs = pl.ds(pl.multiple_of(ci * GRAD_CHUNK, GRAD_CHUNK), GRAD_CHUNK)
            mine = own_sum[rs, :]
            acc = jnp.where(my_chip == 0, mine, got[0, rs, :].astype(F32))
            for q in range(1, n_chips):
                acc = acc + jnp.where(my_chip == q, mine, got[q, rs, :].astype(F32))
            out_ref[rs, :] = acc
            return carry

        lax.fori_loop(0, n // GRAD_CHUNK, chip_sums, 0)

        for cp in rep_copies:
            cp.wait_recv()
        local.wait()
        acc = land_rep[0]
        for d in range(1, N_DEV):
            acc = acc + land_rep[d]
        orep_ref[...] = acc
        for cp in swaps + hops + rep_copies:
            cp.wait_send()

    return pl.pallas_call(
        body, name="grad_exchange",
        in_specs=[pl.BlockSpec(memory_space=pltpu.VMEM), pl.BlockSpec(memory_space=pltpu.VMEM)],
        out_specs=[pl.BlockSpec(memory_space=pltpu.VMEM), pl.BlockSpec(memory_space=pltpu.VMEM)],
        out_shape=[jax.ShapeDtypeStruct((n, width), F32), jax.ShapeDtypeStruct((nrep, LANES), F32)],
        scratch_shapes=[pltpu.VMEM((n_chips, n, width), BF16), pltpu.VMEM((n_chips, n, width), BF16),
                        pltpu.VMEM((n_chips, n, width), BF16), pltpu.VMEM((n, width), F32),
                        pltpu.VMEM((N_DEV, nrep, LANES), F32),
                        pltpu.SemaphoreType.DMA((2 * N_DEV - 2,)), pltpu.SemaphoreType.DMA((2 * N_DEV - 2,)),
                        pltpu.SemaphoreType.DMA((1,))],
        compiler_params=pltpu.CompilerParams(vmem_limit_bytes=VMEM_LIMIT, has_side_effects=True),
    )(gbig, rep)


def _adamw_all(ws, gs, ms, vs):
    n = len(ws)

    def body(*refs):
        w_refs, g_refs, m_refs, v_refs = refs[0:n], refs[n:2 * n], refs[2 * n:3 * n], refs[3 * n:4 * n]
        d_refs, nm_refs, nv_refs = refs[4 * n:5 * n], refs[5 * n:6 * n], refs[6 * n:7 * n]
        for w_ref, g_ref, m_ref, v_ref, d_ref, nm_ref, nv_ref in zip(w_refs, g_refs, m_refs, v_refs, d_refs, nm_refs, nv_refs):
            g = g_ref[...]
            m = ADAM_B1 * m_ref[...] + (1.0 - ADAM_B1) * g
            v = ADAM_B2 * v_ref[...] + (1.0 - ADAM_B2) * jnp.square(g)
            m_hat = m / (1.0 - ADAM_B1 ** ADAM_STEP)
            v_hat = v / (1.0 - ADAM_B2 ** ADAM_STEP)
            d_ref[...] = -ADAM_LR * (m_hat / (jnp.sqrt(v_hat) + ADAM_EPS) + ADAM_WD * w_ref[...])
            nm_ref[...] = m
            nv_ref[...] = v

    shapes = [jax.ShapeDtypeStruct(w.shape, F32) for w in ws]
    outs = pl.pallas_call(
        body, name="adamw", out_shape=shapes * 3,
        compiler_params=pltpu.CompilerParams(vmem_limit_bytes=VMEM_LIMIT),
    )(*ws, *gs, *ms, *vs)
    return outs[0:n], outs[n:2 * n], outs[2 * n:3 * n]


SMALL_A = (("meta_tokens", 16),)
SMALL_B = (("b_norm_g", 1), ("b_conv_w", 4), ("b_conv_b", 1), ("b_b_rg", 1), ("b_b_ig", 1), ("b_lam", 1))
REP = (("a_norm_g", 8), ("a_q_norm_g", 3), ("a_kv_norm_g", 2), ("final_norm_g", 8), ("loss", 1))
SLOT = 16


def _offsets(table, slot=1, start=0):
    out, o = {}, start
    for name, n in table:
        out[name] = (o, n)
        o += -(-n // slot) * slot
    return out, o


def _slotted(a, axis):
    pad = -a.shape[axis] % SLOT
    if not pad:
        return a
    widths = [(0, 0)] * a.ndim
    widths[axis] = (0, pad)
    return jnp.pad(a, widths)


def _rope_tables(rows):
    pos = np.arange(rows, dtype=np.float32)
    inv_freq = (np.float32(ROPE_BASE) ** (-np.arange(0, QK_ROPE, 2, dtype=np.float32) / np.float32(QK_ROPE))).astype(
        np.float32)
    ang = pos[:, None] * inv_freq[None, :]
    cos, sin = np.cos(ang).astype(np.float32), np.sin(ang).astype(np.float32)
    zeros = np.zeros((rows, LANES - QK_ROPE), np.float32)
    return jnp.asarray(np.concatenate([cos, cos, zeros], axis=1)), jnp.asarray(np.concatenate([-sin, sin, zeros], axis=1))


def kernel(x, meta_tokens, a_norm_g, a_w_in, a_q_norm_g, a_kv_norm_g, a_w_uq, a_w_ukv, a_w_out, b_norm_g, b_w_in, b_conv_w, b_conv_b, b_w_rg, b_b_rg, b_w_ig, b_b_ig, b_lam, b_w_out, final_norm_g, loss_target, m_meta_tokens, m_a_norm_g, m_a_w_in, m_a_q_norm_g, m_a_kv_norm_g, m_a_w_uq, m_a_w_ukv, m_a_w_out, m_b_norm_g, m_b_w_in, m_b_conv_w, m_b_conv_b, m_b_w_rg, m_b_b_rg, m_b_w_ig, m_b_b_ig, m_b_lam, m_b_w_out, m_final_norm_g, v_meta_tokens, v_a_norm_g, v_a_w_in, v_a_q_norm_g, v_a_kv_norm_g, v_a_w_uq, v_a_w_ukv, v_a_w_out, v_b_norm_g, v_b_w_in, v_b_conv_w, v_b_conv_b, v_b_w_rg, v_b_b_rg, v_b_w_ig, v_b_b_ig, v_b_lam, v_b_w_out, v_final_norm_g):
    names = ("meta_tokens", "a_norm_g", "a_w_in", "a_q_norm_g", "a_kv_norm_g", "a_w_uq", "a_w_ukv", "a_w_out",
             "b_norm_g", "b_w_in", "b_conv_w", "b_conv_b", "b_w_rg", "b_b_rg", "b_w_ig", "b_b_ig", "b_lam", "b_w_out",
             "final_norm_g")
    w = dict(zip(names, (meta_tokens, a_norm_g, a_w_in, a_q_norm_g, a_kv_norm_g, a_w_uq, a_w_ukv, a_w_out, b_norm_g,
                         b_w_in, b_conv_w, b_conv_b, b_w_rg, b_b_rg, b_w_ig, b_b_ig, b_lam, b_w_out, final_norm_g)))
    mom_m = dict(zip(names, (m_meta_tokens, m_a_norm_g, m_a_w_in, m_a_q_norm_g, m_a_kv_norm_g, m_a_w_uq, m_a_w_ukv,
                             m_a_w_out, m_b_norm_g, m_b_w_in, m_b_conv_w, m_b_conv_b, m_b_w_rg, m_b_b_rg, m_b_w_ig,
                             m_b_b_ig, m_b_lam, m_b_w_out, m_final_norm_g)))
    mom_v = dict(zip(names, (v_meta_tokens, v_a_norm_g, v_a_w_in, v_a_q_norm_g, v_a_kv_norm_g, v_a_w_uq, v_a_w_ukv,
                             v_a_w_out, v_b_norm_g, v_b_w_in, v_b_conv_w, v_b_conv_b, v_b_w_rg, v_b_b_rg, v_b_w_ig,
                             v_b_b_ig, v_b_lam, v_b_w_out, v_final_norm_g)))

    seq = x.shape[1]
    n_real = N_META + seq
    rows = -(-n_real // LANES) * LANES
    scale = (QK_NOPE + QK_ROPE) ** -0.5
    small_off, _ = _offsets(SMALL_A + SMALL_B, SLOT)
    gsmallb_off, _ = _offsets(SMALL_B, SLOT)
    rep_off, _ = _offsets(REP, SLOT)
    cdev_a = a_w_in.shape[-1]
    wide = 2 * LANES

    send_a0 = jnp.pad(a_w_in[0], ((0, 0), (0, wide - cdev_a))).astype(BF16)
    send_small = jnp.concatenate([_slotted(w[nm].reshape(-1, LANES), 0) for nm, _ in SMALL_A + SMALL_B], axis=0)
    sends_a1 = [jnp.pad(a_w_uq[0], ((0, 0), (0, HEAD_PAD - QK_NOPE - QK_ROPE))).astype(BF16), a_w_ukv[0].astype(BF16)]
    lru_rows = LRU_BLOCKS * LRU_BLOCK // N_DEV
    sends_b = [a_w_out[0].astype(BF16), b_w_in[0].astype(BF16), b_w_rg.reshape(lru_rows, LRU_BLOCK).astype(BF16),
               b_w_ig.reshape(lru_rows, LRU_BLOCK).astype(BF16), b_w_out[0].astype(BF16)]
    all_a0, all_small = _all_gather(send_a0, send_small)

    def small_seg(nm):
        o, n = small_off[nm]
        return all_small[:, o:o + n, :]

    w_in_a = all_a0[:, :, :cdev_a].transpose(1, 0, 2).reshape(D_MODEL, N_DEV * cdev_a)
    w_in_a = jnp.concatenate([w_in_a[:, :LAT + QK_ROPE], jnp.zeros((D_MODEL, LAT_PAD - LAT - QK_ROPE), BF16),
                              w_in_a[:, LAT + QK_ROPE:]], axis=1)[None]
    meta_full = small_seg("meta_tokens").transpose(1, 0, 2).reshape(N_META, D_MODEL)
    vec = lambda nm: small_seg(nm).reshape(1, D_MODEL)
    g_b, conv_b, b_rg, b_ig, lam = vec("b_norm_g"), vec("b_conv_b"), vec("b_b_rg"), vec("b_b_ig"), vec("b_lam")
    conv_w = small_seg("b_conv_w").transpose(1, 0, 2).reshape(CONV_WIDTH, LRU_WIDTH)
    g_a, g_q, g_kv = a_norm_g, a_q_norm_g, a_kv_norm_g
    g_f = final_norm_g.reshape(1, D_MODEL)

    h0 = jnp.concatenate([meta_full, x[0], jnp.zeros((rows - n_real, D_MODEL), F32)], axis=0)
    cos, sin = _rope_tables(rows)

    lat, gate_a, w_uq, w_ukv = _norm_proj_fwd(h0, g_a, w_in_a, LAT_PAD, "a_in_fwd", sends_a1)
    qc, kc, v, vt = _mla_qkv_fwd(lat, g_q, g_kv, w_uq, w_ukv, cos, sin, scale)
    o, lse, w_out_a, w_in_b, w_rg, w_ig, w_out_b = _attn_fwd(qc, kc, vt, sends_b)

    lru_w = lambda g: g.reshape(N_DEV, LRU_BLOCKS, LRU_BLOCK // N_DEV, LRU_BLOCK).transpose(1, 0, 2, 3).reshape(
        LRU_BLOCKS, LRU_BLOCK, LRU_BLOCK)
    w_out_a, w_out_b = w_out_a.reshape(D_MODEL, D_MODEL), w_out_b.reshape(D_MODEL, D_MODEL)
    w_rg, w_ig = lru_w(w_rg), lru_w(w_ig)

    h1, u, gate_b = _out_proj_in_proj(o, gate_a, h0, w_out_a, g_b, w_in_b, LRU_WIDTH, "a_out_b_in_fwd")
    hs = _rglru_fwd(u, conv_w, conv_b, w_rg, b_rg, w_ig, b_ig, lam)
    dh2, loss_part, dg_f, dhs, dgate_b, dw_out_b = _out_proj_loss(hs, gate_b, h1, w_out_b, g_f, loss_target[0],
                                                                   n_real)

    du, dconv_w, dconv_b, dw_rg, db_rg, dw_ig, db_ig, dlam = _rglru_bwd(u, hs, dhs, conv_w, conv_b, w_rg, b_rg, w_ig,
                                                                       b_ig, lam)
    dh1, dw_in_b, dg_b = _norm_proj_bwd(h1, g_b, w_in_b, du, dgate_b, dh2, "b_in_bwd")
    do, dgate_a, dw_out_a, delta = _attn_out_bwd(o, gate_a, dh1, w_out_a)

    def to_cols(g, cdev):
        r = g.shape[0]
        return g.reshape(r, N_DEV, cdev).transpose(1, 0, 2).reshape(N_DEV, -1, LANES)

    lru_g = lambda g: g.reshape(LRU_BLOCKS, N_DEV, LRU_BLOCK // N_DEV, LRU_BLOCK).transpose(1, 0, 2, 3).reshape(
        N_DEV, lru_rows, LRU_BLOCK)
    small_b = {"b_norm_g": dg_b, "b_conv_w": dconv_w, "b_conv_b": dconv_b, "b_b_rg": db_rg, "b_b_ig": db_ig,
               "b_lam": dlam}
    gsends_b = [dw_out_a.reshape(N_DEV, -1, D_MODEL), dw_in_b, lru_g(dw_rg).astype(BF16), lru_g(dw_ig).astype(BF16),
                dw_out_b.reshape(N_DEV, -1, D_MODEL),
                jnp.concatenate([_slotted(to_cols(small_b[nm], LANES).astype(BF16), 1) for nm, _ in SMALL_B], axis=1)]

    dqc, dkc, dv, *lands_b = _attn_bwd(qc, kc, v, lse, delta, do, gsends_b)
    g_out_a, g_in_b, g_rg, g_ig, g_out_b, gsum_small_b = _sum_blocks(lands_b, "sum_blocks_b")
    dlat, dw_uq, dw_ukv, dg_q, dg_kv = _mla_qkv_bwd(lat, g_q, g_kv, w_uq, w_ukv, cos, sin, dqc, dkc, dv, scale)
    dh0, dw_in_a, dg_a, *lands_a1 = _norm_proj_bwd(h0, g_a, w_in_a, dlat, dgate_a, dh1, "a_in_bwd", [dw_uq, dw_ukv])
    g_uq, g_ukv = _sum_blocks(lands_a1, "sum_blocks_a1")

    grad_x = dh0[N_META:n_real][None]

    dw_in_a_nat = jnp.concatenate([dw_in_a[0, :, :LAT + QK_ROPE], dw_in_a[0, :, LAT_PAD:]], axis=1)
    in_lanes = lambda g, cdev: jnp.pad(g.reshape(g.shape[0], N_DEV, cdev).transpose(1, 0, 2),
                                       ((0, 0), (0, 0), (0, wide - cdev)))
    pieces = [in_lanes(dw_in_a_nat, cdev_a), in_lanes(dh0[:N_META].astype(BF16), LANES)]
    used = sum(p.shape[1] for p in pieces)
    pieces.append(jnp.zeros((N_DEV, -used % GRAD_CHUNK, wide), BF16))
    gsend_a0 = jnp.concatenate(pieces, axis=1)
    rep_parts = {"a_norm_g": dg_a, "a_q_norm_g": dg_q, "a_kv_norm_g": dg_kv, "final_norm_g": dg_f,
                 "loss": jnp.broadcast_to(loss_part, (1, LANES))}
    rep = jnp.concatenate([_slotted(rep_parts[nm].reshape(-1, LANES), 0) for nm, _ in REP], axis=0)
    gsum_a0, rep_sum = _grad_exchange(gsend_a0, rep)

    grads = {"a_w_out": g_out_a, "b_w_in": g_in_b, "b_w_rg": g_rg, "b_w_ig": g_ig, "b_w_out": g_out_b,
             "a_w_uq": g_uq[:, :QK_NOPE + QK_ROPE], "a_w_ukv": g_ukv}
    grads = {nm: g.reshape(w[nm].shape) for nm, g in grads.items()}
    grads["a_w_in"] = gsum_a0[:D_MODEL, :cdev_a].reshape(w["a_w_in"].shape)
    grads["meta_tokens"] = gsum_a0[D_MODEL:D_MODEL + N_META, :LANES]
    for off, src in ((gsmallb_off, gsum_small_b), (rep_off, rep_sum)):
        for nm, (o_r, n) in off.items():
            if nm in w:
                grads[nm] = src[o_r:o_r + n].reshape(w[nm].shape)
    loss = rep_sum[rep_off["loss"][0], 0]

    as2d = lambda a: a.reshape(1, -1) if a.ndim == 1 else a
    deltas, new_ms, new_vs = _adamw_all([as2d(w[nm]) for nm in names], [as2d(grads[nm]) for nm in names],
                                        [as2d(mom_m[nm]) for nm in names], [as2d(mom_v[nm]) for nm in names])
    shaped = lambda arrs: [a.reshape(w[nm].shape) for a, nm in zip(arrs, names)]
    return (loss, grad_x, *[grads[nm] for nm in names], *shaped(deltas), *shaped(new_ms), *shaped(new_vs))
```

```python
import functools

import numpy as np
import jax
import jax.numpy as jnp
from jax import lax
from jax.experimental import pallas as pl
from jax.experimental.pallas import tpu as pltpu

F32 = jnp.float32
BF16 = jnp.bfloat16

D_MODEL = 1024
N_META = 16
RMS_EPS = 1e-6
HEADS = 8
QK_NOPE = 128
QK_ROPE = 64
V_HEAD = 128
Q_LORA = 384
KV_LORA = 256
HEAD_PAD = 256
LAT = Q_LORA + KV_LORA
LAT_PAD = LAT + 128
ROPE_BASE = 10000.0
MASK_VALUE = -1e30
LRU_WIDTH = 1024
LRU_BLOCKS = 4
LRU_BLOCK = 256
CONV_WIDTH = 4
LRU_C = 8.0
N_DEV = 8
ADAM_LR, ADAM_B1, ADAM_B2, ADAM_EPS, ADAM_WD, ADAM_STEP = 0.001, 0.9, 0.999, 1e-08, 0.01, 10

LANES = 128
SUBLANES = 8
VMEM_LIMIT = 56 * 1024 * 1024
MESH = pl.DeviceIdType.MESH

NT = (((1,), (1,)), ((), ()))
TN = (((0,), (0,)), ((), ()))


def _row_block(rows):
    return 384 if rows % 384 == 0 else 128


def _cparams(sem):
    return pltpu.CompilerParams(dimension_semantics=sem, vmem_limit_bytes=VMEM_LIMIT)


def _silu(x):
    return x * jax.nn.sigmoid(x)


def _dsilu(x):
    s = jax.nn.sigmoid(x)
    return s * (1.0 + x * (1.0 - s))


def _rms_fwd(x):
    r = lax.rsqrt(jnp.mean(x * x, axis=-1, keepdims=True) + RMS_EPS)
    return x * r, r


def _rms_bwd(dy, xn, r, g):
    t = dy * g
    dx = r * (t - xn * jnp.mean(t * xn, axis=-1, keepdims=True))
    return dx, jnp.sum(dy * xn, axis=0, keepdims=True)


def _expm1_neg(x):
    small = x * (1.0 + x * (1 / 2 + x * (1 / 6 + x * (1 / 24))))
    return jnp.where(x > -0.05, small, jnp.exp(x) - 1.0)


def _softplus_neg(lam):
    z = jnp.exp(-jnp.abs(lam))
    w = z / (2.0 + z)
    w2 = w * w
    series = 2.0 * w * (1.0 + w2 * (1 / 3) + w2 * w2 * (1 / 5))
    return jnp.maximum(-lam, 0.0) + jnp.where(z < 0.1, series, jnp.log(1.0 + z))


def _rider(sends, refs, first, last, all_to_all):
    nb = len(sends)
    if not nb:
        return (lambda: None), (lambda: None)
    send_refs, result_refs = refs[:nb], refs[nb:2 * nb]
    send_sems, recv_sems, local_sems = refs[2 * nb:]
    pick = (lambda ref: (lambda d: ref.at[d])) if all_to_all else (lambda ref: (lambda d: ref))

    def copies():
        out = []
        for b in range(nb):
            out += _exchange_copies(pick(send_refs[b]), result_refs[b], send_sems.at[b], recv_sems.at[b],
                                    local_sems.at[b])
        return out

    def start():
        @pl.when(first)
        def _():
            for cp in copies():
                cp.start()

    def wait():
        @pl.when(last)
        def _():
            for cp in copies():
                cp.wait()

    return start, wait


def _rider_specs(sends, all_to_all):
    nb = len(sends)
    if not nb:
        return [], [], [], []
    hbm = pl.BlockSpec(memory_space=pl.ANY)
    shapes = [jax.ShapeDtypeStruct(s.shape if all_to_all else (N_DEV,) + s.shape, s.dtype) for s in sends]
    return [hbm] * nb, [hbm] * nb, shapes, _exchange_sems(nb)


def _proj_blocks(x, w_ref):
    return jnp.concatenate([jnp.dot(x, w_ref[d], preferred_element_type=F32) for d in range(w_ref.shape[0])], axis=1)


def _norm_proj_fwd(h, g, w, n1, name, wsends=()):
    rows = h.shape[0]
    nb, _, cb = w.shape
    n = nb * cb
    tr = _row_block(rows)
    nsteps = rows // tr
    extra = len(wsends)

    def body(h_ref, g_ref, w_ref, *rest):
        p1_ref, p2_ref = rest[extra], rest[extra + 1]
        i = pl.program_id(0)
        start, wait = _rider(wsends, rest[:extra] + rest[extra + 2:], i == 0, i == nsteps - 1, False)
        start()
        xn, _ = _rms_fwd(h_ref[...])
        p = _proj_blocks((xn * g_ref[...]).astype(BF16), w_ref)
        p1_ref[...] = p[:, :n1]
        p2_ref[...] = p[:, n1:]
        wait()

    r_in, r_out, r_shape, r_scratch = _rider_specs(wsends, False)
    return pl.pallas_call(
        body, name=name, grid=(nsteps,),
        in_specs=[pl.BlockSpec((tr, D_MODEL), lambda i: (i, 0)),
                  pl.BlockSpec((1, D_MODEL), lambda i: (0, 0)),
                  pl.BlockSpec((nb, D_MODEL, cb), lambda i: (0, 0, 0))] + r_in,
        out_specs=[pl.BlockSpec((tr, n1), lambda i: (i, 0)),
                   pl.BlockSpec((tr, n - n1), lambda i: (i, 0))] + r_out,
        out_shape=[jax.ShapeDtypeStruct((rows, n1), F32), jax.ShapeDtypeStruct((rows, n - n1), F32)] + r_shape,
        scratch_shapes=r_scratch,
        compiler_params=_cparams(("arbitrary",)),
    )(h, g, w, *wsends)


def _out_proj_in_proj(a, gate, h, w_out, g, w_in, n1, name):
    rows = h.shape[0]
    nb, _, cb = w_in.shape
    n = nb * cb
    tr = _row_block(rows)

    def body(a_ref, gate_ref, h_ref, wo_ref, g_ref, wi_ref, hn_ref, p1_ref, p2_ref):
        y = (a_ref[...] * _silu(gate_ref[...])).astype(BF16)
        h_new = h_ref[...] + jnp.dot(y, wo_ref[...], preferred_element_type=F32)
        hn_ref[...] = h_new
        xn, _ = _rms_fwd(h_new)
        p = _proj_blocks((xn * g_ref[...]).astype(BF16), wi_ref)
        p1_ref[...] = p[:, :n1]
        p2_ref[...] = p[:, n1:]

    blk = pl.BlockSpec((tr, D_MODEL), lambda i: (i, 0))
    return pl.pallas_call(
        body, name=name, grid=(rows // tr,),
        in_specs=[blk, blk, blk, pl.BlockSpec((D_MODEL, D_MODEL), lambda i: (0, 0)),
                  pl.BlockSpec((1, D_MODEL), lambda i: (0, 0)), pl.BlockSpec((nb, D_MODEL, cb), lambda i: (0, 0, 0))],
        out_specs=[blk, pl.BlockSpec((tr, n1), lambda i: (i, 0)), pl.BlockSpec((tr, n - n1), lambda i: (i, 0))],
        out_shape=[jax.ShapeDtypeStruct((rows, D_MODEL), F32), jax.ShapeDtypeStruct((rows, n1), F32),
                   jax.ShapeDtypeStruct((rows, n - n1), F32)],
        compiler_params=_cparams(("parallel",)),
    )(a, gate, h, w_out, g, w_in)


def _norm_proj_bwd(h, g, w, dp1, dp2, dh_in, name, gsends=()):
    rows = h.shape[0]
    nb, _, cb = w.shape
    n1 = dp1.shape[1]
    n2 = nb * cb - n1
    tr = _row_block(rows)
    nsteps = rows // tr
    extra = len(gsends)

    def body(h_ref, g_ref, w_ref, dp1_ref, dp2_ref, dhin_ref, *rest):
        dh_ref, dw_ref, dg_ref = rest[extra:extra + 3]
        dw_acc = rest[2 * extra + 3]
        i = pl.program_id(0)
        start, wait = _rider(gsends, rest[:extra] + rest[extra + 3:2 * extra + 3] + rest[2 * extra + 4:],
                             i == 0, i == nsteps - 1, True)
        start()

        @pl.when(i == 0)
        def _():
            dw_acc[...] = jnp.zeros_like(dw_acc)
            dg_ref[...] = jnp.zeros_like(dg_ref)

        gv = g_ref[...]
        xn, r = _rms_fwd(h_ref[...])
        hn = (xn * gv).astype(BF16)
        dp = jnp.concatenate([dp1_ref[...].astype(BF16), dp2_ref[...].astype(BF16)], axis=1)
        dhn = jnp.zeros((tr, D_MODEL), F32)
        for d in range(nb):
            dpd = dp[:, d * cb:(d + 1) * cb]
            dw_acc[d] += lax.dot_general(hn, dpd, TN, preferred_element_type=F32)
            dhn = dhn + lax.dot_general(dpd, w_ref[d], NT, preferred_element_type=F32)
        dx, dg = _rms_bwd(dhn, xn, r, gv)
        dg_ref[...] += dg
        dh_ref[...] = dhin_ref[...] + dx

        @pl.when(i == nsteps - 1)
        def _():
            dw_ref[...] = dw_acc[...].astype(BF16)

        wait()

    r_in, r_out, r_shape, r_scratch = _rider_specs(gsends, True)
    wblk = pl.BlockSpec((nb, D_MODEL, cb), lambda i: (0, 0, 0))
    return pl.pallas_call(
        body, name=name, grid=(nsteps,),
        in_specs=[pl.BlockSpec((tr, D_MODEL), lambda i: (i, 0)),
                  pl.BlockSpec((1, D_MODEL), lambda i: (0, 0)),
                  wblk,
                  pl.BlockSpec((tr, n1), lambda i: (i, 0)),
                  pl.BlockSpec((tr, n2), lambda i: (i, 0)),
                  pl.BlockSpec((tr, D_MODEL), lambda i: (i, 0))] + r_in,
        out_specs=[pl.BlockSpec((tr, D_MODEL), lambda i: (i, 0)), wblk,
                   pl.BlockSpec((1, D_MODEL), lambda i: (0, 0))] + r_out,
        out_shape=[jax.ShapeDtypeStruct((rows, D_MODEL), F32),
                   jax.ShapeDtypeStruct((nb, D_MODEL, cb), BF16),
                   jax.ShapeDtypeStruct((1, D_MODEL), F32)] + r_shape,
        scratch_shapes=[pltpu.VMEM((nb, D_MODEL, cb), F32)] + r_scratch,
        compiler_params=_cparams(("arbitrary",)),
    )(h, g, w, dp1, dp2, dh_in, *gsends)


def _attn_out_bwd(o, gate, dh, w):
    rows = o.shape[0]
    tr = _row_block(rows)
    nsteps = rows // tr

    def body(o_ref, gate_ref, dh_ref, w_ref, do_ref, dgate_ref, dw_ref, delta_ref, dw_acc):
        i = pl.program_id(0)

        @pl.when(i == 0)
        def _():
            dw_acc[...] = jnp.zeros_like(dw_acc)

        ov, gv = o_ref[...], gate_ref[...]
        sg = _silu(gv)
        dhb = dh_ref[...].astype(BF16)
        dw_acc[...] += lax.dot_general((ov * sg).astype(BF16), dhb, TN, preferred_element_type=F32)
        dy = lax.dot_general(dhb, w_ref[...], NT, preferred_element_type=F32)
        do = (dy * sg).astype(BF16)
        do_ref[...] = do
        dgate_ref[...] = (dy * ov * _dsilu(gv)).astype(BF16)
        prod = do.astype(F32) * ov
        lane = lax.broadcasted_iota(jnp.int32, (tr, LANES), 1)
        per_head = jnp.zeros((tr, LANES), F32)
        for hd in range(HEADS):
            dsum = jnp.sum(prod[:, hd * V_HEAD:(hd + 1) * V_HEAD], axis=1, keepdims=True)
            per_head = jnp.where(lane == hd, dsum, per_head)
        delta_t = per_head.T
        for hd in range(HEADS):
            delta_ref[hd, 0] = delta_t[hd:hd + 1, :]

        @pl.when(i == nsteps - 1)
        def _():
            dw_ref[...] = dw_acc[...].astype(BF16)

    blk = pl.BlockSpec((tr, D_MODEL), lambda i: (i, 0))
    wblk = pl.BlockSpec((D_MODEL, D_MODEL), lambda i: (0, 0))
    return pl.pallas_call(
        body, name="a_out_bwd", grid=(nsteps,),
        in_specs=[blk, blk, blk, wblk],
        out_specs=[blk, blk, wblk, pl.BlockSpec((HEADS, 1, 1, tr), lambda i: (0, i, 0, 0))],
        out_shape=[jax.ShapeDtypeStruct((rows, D_MODEL), BF16), jax.ShapeDtypeStruct((rows, D_MODEL), BF16),
                   jax.ShapeDtypeStruct((D_MODEL, D_MODEL), BF16),
                   jax.ShapeDtypeStruct((HEADS, nsteps, 1, tr), F32)],
        scratch_shapes=[pltpu.VMEM((D_MODEL, D_MODEL), F32)],
        compiler_params=_cparams(("arbitrary",)),
    )(o, gate, dh, w)


def _rope(v, cos, sin, lane):
    swapped = jnp.where(lane < QK_ROPE // 2, pltpu.roll(v, LANES - QK_ROPE // 2, 1), pltpu.roll(v, QK_ROPE // 2, 1))
    return v * cos + swapped * sin


def _unrope(dv, cos, sin, lane):
    t = dv * sin
    swapped = jnp.where(lane < QK_ROPE // 2, pltpu.roll(t, LANES - QK_ROPE // 2, 1), pltpu.roll(t, QK_ROPE // 2, 1))
    return dv * cos + swapped


def _mla_qkv_fwd(lat, gq, gkv, wuq, wukv, cos, sin, scale):
    rows = lat.shape[0]
    tr = _row_block(rows)

    def body(lat_ref, gq_ref, gkv_ref, wuq_ref, wukv_ref, cos_ref, sin_ref, qc_ref, kc_ref, v_ref, vt_ref):
        qn, _ = _rms_fwd(lat_ref[:, :Q_LORA])
        kvn, _ = _rms_fwd(lat_ref[:, Q_LORA:LAT])
        qnb = (qn * gq_ref[...]).astype(BF16)
        kvnb = (kvn * gkv_ref[...]).astype(BF16)
        c, s = cos_ref[...], sin_ref[...]
        lane = lax.broadcasted_iota(jnp.int32, (tr, LANES), 1)
        kr = _rope(lat_ref[:, LAT:LAT_PAD], c, s, lane).astype(BF16)
        for hd in range(HEADS):
            o = hd * HEAD_PAD
            q = jnp.dot(qnb, wuq_ref[hd], preferred_element_type=F32)
            kv = jnp.dot(kvnb, wukv_ref[hd], preferred_element_type=F32)
            qc_ref[:, o:o + QK_NOPE] = (q[:, :QK_NOPE] * scale).astype(BF16)
            qc_ref[:, o + QK_NOPE:o + HEAD_PAD] = (_rope(q[:, QK_NOPE:], c, s, lane) * scale).astype(BF16)
            kc_ref[:, o:o + QK_NOPE] = kv[:, :QK_NOPE].astype(BF16)
            kc_ref[:, o + QK_NOPE:o + HEAD_PAD] = kr
            vh = kv[:, QK_NOPE:]
            v_ref[:, hd * V_HEAD:(hd + 1) * V_HEAD] = vh.astype(BF16)
            vt_ref[hd, 0] = vh.T.astype(BF16)

    full = lambda shape: pl.BlockSpec(shape, lambda i: (0,) * len(shape))
    rowb = lambda n: pl.BlockSpec((tr, n), lambda i: (i, 0))
    return pl.pallas_call(
        body, name="mla_qkv_fwd", grid=(rows // tr,),
        in_specs=[rowb(LAT_PAD), full((1, Q_LORA)), full((1, KV_LORA)), full((HEADS, Q_LORA, HEAD_PAD)),
                  full((HEADS, KV_LORA, HEAD_PAD)), rowb(LANES), rowb(LANES)],
        out_specs=[rowb(HEADS * HEAD_PAD), rowb(HEADS * HEAD_PAD), rowb(HEADS * V_HEAD),
                   pl.BlockSpec((HEADS, 1, V_HEAD, tr), lambda i: (0, i, 0, 0))],
        out_shape=[jax.ShapeDtypeStruct((rows, HEADS * HEAD_PAD), BF16),
                   jax.ShapeDtypeStruct((rows, HEADS * HEAD_PAD), BF16),
                   jax.ShapeDtypeStruct((rows, HEADS * V_HEAD), BF16),
                   jax.ShapeDtypeStruct((HEADS, rows // tr, V_HEAD, tr), BF16)],
        compiler_params=_cparams(("parallel",)),
    )(lat, gq, gkv, wuq, wukv, cos, sin)


def _mla_qkv_bwd(lat, gq, gkv, wuq, wukv, cos, sin, dqc, dkc, dv, scale):
    rows = lat.shape[0]
    tr = _row_block(rows)
    nsteps = rows // tr

    def body(lat_ref, gq_ref, gkv_ref, wuq_ref, wukv_ref, cos_ref, sin_ref, dqc_ref, dkc_ref, dv_ref,
             dlat_ref, dwuq_out, dwukv_out, dgq_ref, dgkv_ref, dwuq_ref, dwukv_ref):
        @pl.when(pl.program_id(0) == 0)
        def _():
            dwuq_ref[...] = jnp.zeros_like(dwuq_ref)
            dwukv_ref[...] = jnp.zeros_like(dwukv_ref)
            dgq_ref[...] = jnp.zeros_like(dgq_ref)
            dgkv_ref[...] = jnp.zeros_like(dgkv_ref)

        c, s = cos_ref[...], sin_ref[...]
        lane = lax.broadcasted_iota(jnp.int32, (tr, LANES), 1)
        gqv, gkvv = gq_ref[...], gkv_ref[...]
        qn, rq = _rms_fwd(lat_ref[:, :Q_LORA])
        kvn, rkv = _rms_fwd(lat_ref[:, Q_LORA:LAT])
        qnb = (qn * gqv).astype(BF16)
        kvnb = (kvn * gkvv).astype(BF16)
        dkr = jnp.zeros((tr, LANES), F32)
        dqn = jnp.zeros((tr, Q_LORA), F32)
        dkvn = jnp.zeros((tr, KV_LORA), F32)
        for hd in range(HEADS):
            o = hd * HEAD_PAD
            dq = jnp.concatenate(
                [dqc_ref[:, o:o + QK_NOPE],
                 _unrope(dqc_ref[:, o + QK_NOPE:o + HEAD_PAD].astype(F32), c, s, lane).astype(BF16)], axis=1)
            dkv = jnp.concatenate([dkc_ref[:, o:o + QK_NOPE], dv_ref[:, hd * V_HEAD:(hd + 1) * V_HEAD]], axis=1)
            dkr = dkr + dkc_ref[:, o + QK_NOPE:o + HEAD_PAD].astype(F32)
            dwuq_ref[hd] += scale * lax.dot_general(qnb, dq, TN, preferred_element_type=F32)
            dwukv_ref[hd] += lax.dot_general(kvnb, dkv, TN, preferred_element_type=F32)
            dqn = dqn + lax.dot_general(dq, wuq_ref[hd], NT, preferred_element_type=F32)
            dkvn = dkvn + lax.dot_general(dkv, wukv_ref[hd], NT, preferred_element_type=F32)
        dqn = scale * dqn
        dqlat, dgq = _rms_bwd(dqn, qn, rq, gqv)
        dkvlat, dgkv = _rms_bwd(dkvn, kvn, rkv, gkvv)
        dgq_ref[...] += dgq
        dgkv_ref[...] += dgkv
        dlat_ref[:, :Q_LORA] = dqlat.astype(BF16)
        dlat_ref[:, Q_LORA:LAT] = dkvlat.astype(BF16)
        dlat_ref[:, LAT:LAT_PAD] = _unrope(dkr, c, s, lane).astype(BF16)

        @pl.when(pl.program_id(0) == nsteps - 1)
        def _():
            dwuq_out[...] = dwuq_ref[...].astype(BF16)
            dwukv_out[...] = dwukv_ref[...].astype(BF16)

    full = lambda shape: pl.BlockSpec(shape, lambda i: (0,) * len(shape))
    rowb = lambda n: pl.BlockSpec((tr, n), lambda i: (i, 0))
    return pl.pallas_call(
        body, name="mla_qkv_bwd", grid=(nsteps,),
        in_specs=[rowb(LAT_PAD), full((1, Q_LORA)), full((1, KV_LORA)), full((HEADS, Q_LORA, HEAD_PAD)),
                  full((HEADS, KV_LORA, HEAD_PAD)), rowb(LANES), rowb(LANES),
                  rowb(HEADS * HEAD_PAD), rowb(HEADS * HEAD_PAD), rowb(HEADS * V_HEAD)],
        out_specs=[rowb(LAT_PAD), full((HEADS, Q_LORA, HEAD_PAD)), full((HEADS, KV_LORA, HEAD_PAD)),
                   full((1, Q_LORA)), full((1, KV_LORA))],
        out_shape=[jax.ShapeDtypeStruct((rows, LAT_PAD), BF16),
                   jax.ShapeDtypeStruct((HEADS, Q_LORA, HEAD_PAD), BF16),
                   jax.ShapeDtypeStruct((HEADS, KV_LORA, HEAD_PAD), BF16),
                   jax.ShapeDtypeStruct((1, Q_LORA), F32),
                   jax.ShapeDtypeStruct((1, KV_LORA), F32)],
        scratch_shapes=[pltpu.VMEM((HEADS, Q_LORA, HEAD_PAD), F32), pltpu.VMEM((HEADS, KV_LORA, HEAD_PAD), F32)],
        compiler_params=_cparams(("arbitrary",)),
    )(lat, gq, gkv, wuq, wukv, cos, sin, dqc, dkc, dv)


ATTN_UNROLL = 8


def _causal_mask_t(t):
    key = lax.broadcasted_iota(jnp.int32, (t, t), 0)
    query = lax.broadcasted_iota(jnp.int32, (t, t), 1)
    return key <= query


def _attn_fwd(qc, kc, vt, wsends):
    rows = qc.shape[0]
    t = _row_block(rows)
    nblk = rows // t
    nw = len(wsends)

    def body(q_ref, k_ref, vt_ref, *rest):
        o_ref, lse_ref = rest[nw:nw + 2]
        m_ref, l_ref, acc_ref, st_a, st_b = rest[2 * nw + 2:2 * nw + 7]
        i = pl.program_id(1)
        start, wait = _rider(wsends, rest[:nw] + rest[nw + 2:2 * nw + 2] + rest[2 * nw + 7:],
                             jnp.logical_and(pl.program_id(0) == 0, i == 0),
                             jnp.logical_and(pl.program_id(0) == HEADS - 1, i == nblk - 1), False)
        start()

        m_ref[...] = jnp.full_like(m_ref, MASK_VALUE)
        l_ref[...] = jnp.zeros_like(l_ref)
        acc_ref[...] = jnp.zeros_like(acc_ref)
        q = q_ref[...]

        def scores(j, st_ref):
            rs = pl.ds(pl.multiple_of(j * t, t), t)
            st_ref[...] = lax.dot_general(k_ref[rs, :], q, NT, preferred_element_type=F32)

        def consume(j, st_ref, masked):
            st = st_ref[...]
            if masked:
                st = jnp.where(_causal_mask_t(t), st, MASK_VALUE)
            m_prev = m_ref[...]
            m_new = jnp.maximum(m_prev, jnp.max(st, axis=0, keepdims=True))
            alpha = jnp.exp(m_prev - m_new)
            pt = jnp.exp(st - m_new)
            l_ref[...] = alpha * l_ref[...] + jnp.sum(pt, axis=0, keepdims=True)
            acc_ref[...] = alpha * acc_ref[...] + jnp.dot(vt_ref[0, j], pt.astype(BF16), preferred_element_type=F32)
            m_ref[...] = m_new

        bufs = (st_a, st_b)

        def step(j, parity, issue_next, masked):
            if issue_next:
                scores(j + 1, bufs[1 - parity])
            consume(j, bufs[parity], masked)

        scores(0, st_a)

        def trip(it, carry):
            for u in range(ATTN_UNROLL):
                step(it * ATTN_UNROLL + u, u % 2, True, False)
            return carry

        trips = i // ATTN_UNROLL
        lax.fori_loop(0, trips, trip, 0)
        j0 = trips * ATTN_UNROLL
        for left in range(1, ATTN_UNROLL + 1):
            @pl.when(i + 1 - j0 == left)
            def _(left=left):
                for u in range(left):
                    step(j0 + u, u % 2, u < left - 1, u == left - 1)

        o_ref[...] = (acc_ref[...] / l_ref[...]).T
        lse_ref[0, 0] = m_ref[...] + jnp.log(l_ref[...])
        wait()

    r_in, r_out, r_shape, r_scratch = _rider_specs(wsends, False)
    return pl.pallas_call(
        body, name="attn_fwd", grid=(HEADS, nblk),
        in_specs=[pl.BlockSpec((t, HEAD_PAD), lambda h, i: (i, h)),
                  pl.BlockSpec((rows, HEAD_PAD), lambda h, i: (0, h)),
                  pl.BlockSpec((1, nblk, V_HEAD, t), lambda h, i: (h, 0, 0, 0))] + r_in,
        out_specs=[pl.BlockSpec((t, V_HEAD), lambda h, i: (i, h)),
                   pl.BlockSpec((1, 1, 1, t), lambda h, i: (h, i, 0, 0))] + r_out,
        out_shape=[jax.ShapeDtypeStruct((rows, HEADS * V_HEAD), F32),
                   jax.ShapeDtypeStruct((HEADS, nblk, 1, t), F32)] + r_shape,
        scratch_shapes=[pltpu.VMEM((1, t), F32), pltpu.VMEM((1, t), F32), pltpu.VMEM((V_HEAD, t), F32),
                        pltpu.VMEM((t, t), F32), pltpu.VMEM((t, t), F32)] + r_scratch,
        compiler_params=_cparams(("arbitrary", "arbitrary")),
    )(qc, kc, vt, *wsends)


def _attn_bwd(qc, kc, v, lse, delta, do, gsends):
    rows = qc.shape[0]
    t = _row_block(rows)
    nblk = rows // t
    ng = len(gsends)

    def body(q_ref, k_ref, v_ref, lse_ref, delta_ref, do_ref, *rest):
        dq_ref, dk_ref, dv_ref = rest[ng:ng + 3]
        dq_acc, dk_acc, dv_acc, st_a, dp_a, st_b, dp_b = rest[2 * ng + 3:2 * ng + 10]
        j = pl.program_id(1)
        start, wait = _rider(gsends, rest[:ng] + rest[ng + 3:2 * ng + 3] + rest[2 * ng + 10:],
                             jnp.logical_and(pl.program_id(0) == 0, j == 0),
                             jnp.logical_and(pl.program_id(0) == HEADS - 1, j == nblk - 1), True)
        start()

        @pl.when(j == 0)
        def _():
            dq_acc[...] = jnp.zeros_like(dq_acc)

        dk_acc[...] = jnp.zeros_like(dk_acc)
        dv_acc[...] = jnp.zeros_like(dv_acc)
        k = k_ref[...]
        vv = v_ref[...]

        def products(i, st_ref, dp_ref):
            rs = pl.ds(pl.multiple_of(i * t, t), t)
            st_ref[...] = lax.dot_general(k, q_ref[rs, :], NT, preferred_element_type=F32)
            dp_ref[...] = lax.dot_general(vv, do_ref[rs, :], NT, preferred_element_type=F32)

        def consume(i, st_ref, dp_ref):
            rs = pl.ds(pl.multiple_of(i * t, t), t)
            q = q_ref[rs, :]
            dob = do_ref[rs, :]
            st = jnp.where(jnp.logical_or(_causal_mask_t(t), i != j), st_ref[...], MASK_VALUE)
            pt = jnp.exp(st - lse_ref[0, i])
            dv_acc[...] += jnp.dot(pt.astype(BF16), dob, preferred_element_type=F32)
            dst = (pt * (dp_ref[...] - delta_ref[0, i])).astype(BF16)
            dk_acc[...] += jnp.dot(dst, q, preferred_element_type=F32)
            dq_acc[rs, :] += lax.dot_general(dst, k, TN, preferred_element_type=F32)

        bufs = ((st_a, dp_a), (st_b, dp_b))

        def step(i, parity, issue_next):
            if issue_next:
                products(i + 1, *bufs[1 - parity])
            consume(i, *bufs[parity])

        products(j, st_a, dp_a)

        def trip(it, carry):
            for u in range(ATTN_UNROLL):
                step(j + it * ATTN_UNROLL + u, u % 2, True)
            return carry

        trips = (nblk - 1 - j) // ATTN_UNROLL
        lax.fori_loop(0, trips, trip, 0)
        i0 = j + trips * ATTN_UNROLL
        for left in range(1, ATTN_UNROLL + 1):
            @pl.when(nblk - i0 == left)
            def _(left=left):
                for u in range(left):
                    step(i0 + u, u % 2, u < left - 1)

        dk_ref[...] = dk_acc[...].astype(BF16)
        dv_ref[...] = dv_acc[...].astype(BF16)

        @pl.when(j == nblk - 1)
        def _():
            dq_ref[...] = dq_acc[...].astype(BF16)

        wait()

    stat = pl.BlockSpec((1, nblk, 1, t), lambda h, j: (h, 0, 0, 0))
    r_in, r_out, r_shape, r_scratch = _rider_specs(gsends, True)
    return pl.pallas_call(
        body, name="attn_bwd", grid=(HEADS, nblk),
        in_specs=[pl.BlockSpec((rows, HEAD_PAD), lambda h, j: (0, h)),
                  pl.BlockSpec((t, HEAD_PAD), lambda h, j: (j, h)),
                  pl.BlockSpec((t, V_HEAD), lambda h, j: (j, h)),
                  stat, stat,
                  pl.BlockSpec((rows, V_HEAD), lambda h, j: (0, h))] + r_in,
        out_specs=[pl.BlockSpec((rows, HEAD_PAD), lambda h, j: (0, h)),
                   pl.BlockSpec((t, HEAD_PAD), lambda h, j: (j, h)),
                   pl.BlockSpec((t, V_HEAD), lambda h, j: (j, h))] + r_out,
        out_shape=[jax.ShapeDtypeStruct((rows, HEADS * HEAD_PAD), BF16),
                   jax.ShapeDtypeStruct((rows, HEADS * HEAD_PAD), BF16),
                   jax.ShapeDtypeStruct((rows, HEADS * V_HEAD), BF16)] + r_shape,
        scratch_shapes=[pltpu.VMEM((rows, HEAD_PAD), F32), pltpu.VMEM((t, HEAD_PAD), F32),
                        pltpu.VMEM((t, V_HEAD), F32)] + [pltpu.VMEM((t, t), F32)] * 4 + r_scratch,
        compiler_params=_cparams(("arbitrary", "arbitrary")),
    )(qc, kc, v, lse, delta, do, *gsends)


def _shift_down(prev_tile, x, k):
    xx = jnp.concatenate([prev_tile, x], axis=0)
    return pltpu.roll(xx, k, 0)[SUBLANES:]


def _shift_up(x, next_tile, k):
    n = x.shape[0]
    xx = jnp.concatenate([x, next_tile], axis=0)
    return pltpu.roll(xx, n + SUBLANES - k, 0)[:n]


def _lru_gates(u, u_prev, cw_ref, cb_ref, wrg_ref, brg_ref, wig_ref, big_ref, lam_ref, first_block):
    taps = [_shift_down(u_prev, u, CONV_WIDTH - 1 - j) if j < CONV_WIDTH - 1 else u for j in range(CONV_WIDTH)]
    uc = cb_ref[...] + taps[0] * cw_ref[0:1, :]
    for j in range(1, CONV_WIDTH):
        uc = uc + taps[j] * cw_ref[j:j + 1, :]
    ub = uc.astype(BF16)
    zr = jnp.concatenate([jnp.dot(ub[:, g * LRU_BLOCK:(g + 1) * LRU_BLOCK], wrg_ref[g], preferred_element_type=F32)
                          for g in range(LRU_BLOCKS)], axis=1) + brg_ref[...]
    zi = jnp.concatenate([jnp.dot(ub[:, g * LRU_BLOCK:(g + 1) * LRU_BLOCK], wig_ref[g], preferred_element_type=F32)
                          for g in range(LRU_BLOCKS)], axis=1) + big_ref[...]
    r = jax.nn.sigmoid(zr)
    ig = jax.nn.sigmoid(zi)
    sp = _softplus_neg(lam_ref[...])
    log_a = (-LRU_C) * r * sp
    a = jnp.exp(log_a)
    m2 = -_expm1_neg(2.0 * log_a)
    mult_raw = m2 * lax.rsqrt(jnp.maximum(m2, 1e-30))
    row = lax.broadcasted_iota(jnp.int32, u.shape, 0)
    is_start = jnp.logical_and(first_block, row == 0)
    mult = jnp.where(is_start, 1.0, mult_raw)
    return dict(taps=taps, uc=uc, ub=ub, r=r, ig=ig, sp=sp, a=a, mult=mult, mult_raw=mult_raw, is_start=is_start)


def _rglru_fwd(u, cw, cb, wrg, brg, wig, big, lam):
    rows = u.shape[0]
    tb = _row_block(rows)

    def body(u_ref, cw_ref, cb_ref, wrg_ref, brg_ref, wig_ref, big_ref, lam_ref, hs_ref, utail, hcar, a_s, b_s):
        i = pl.program_id(0)

        @pl.when(i == 0)
        def _():
            utail[...] = jnp.zeros_like(utail)
            hcar[...] = jnp.zeros_like(hcar)

        u = u_ref[...]
        gt = _lru_gates(u, utail[...], cw_ref, cb_ref, wrg_ref, brg_ref, wig_ref, big_ref, lam_ref, i == 0)
        a_s[...] = gt["a"]
        b_s[...] = gt["mult"] * (gt["ig"] * gt["uc"])
        row8 = lax.broadcasted_iota(jnp.int32, (SUBLANES, LRU_WIDTH), 0)

        def tile(tix, carry):
            rs = pl.ds(pl.multiple_of(tix * SUBLANES, SUBLANES), SUBLANES)
            av, bv = a_s[rs, :], b_s[rs, :]
            for k in (1, 2, 4):
                keep = row8 >= k
                bv = jnp.where(keep, av * pltpu.roll(bv, k, 0) + bv, bv)
                av = jnp.where(keep, av * pltpu.roll(av, k, 0), av)
            h8 = av * carry + bv
            hs_ref[rs, :] = h8
            return jnp.broadcast_to(h8[SUBLANES - 1:SUBLANES, :], (SUBLANES, LRU_WIDTH))

        hcar[...] = lax.fori_loop(0, tb // SUBLANES, tile, hcar[...])
        utail[...] = u[tb - SUBLANES:, :]

    full2 = lambda shape: pl.BlockSpec(shape, lambda i: (0, 0))
    full3 = lambda shape: pl.BlockSpec(shape, lambda i: (0, 0, 0))
    blk = pl.BlockSpec((tb, LRU_WIDTH), lambda i: (i, 0))
    return pl.pallas_call(
        body, name="rglru_fwd", grid=(rows // tb,),
        in_specs=[blk, full2((CONV_WIDTH, LRU_WIDTH)), full2((1, LRU_WIDTH)),
                  full3((LRU_BLOCKS, LRU_BLOCK, LRU_BLOCK)), full2((1, LRU_WIDTH)),
                  full3((LRU_BLOCKS, LRU_BLOCK, LRU_BLOCK)), full2((1, LRU_WIDTH)), full2((1, LRU_WIDTH))],
        out_specs=blk,
        out_shape=jax.ShapeDtypeStruct((rows, LRU_WIDTH), F32),
        scratch_shapes=[pltpu.VMEM((SUBLANES, LRU_WIDTH), F32), pltpu.VMEM((SUBLANES, LRU_WIDTH), F32),
                        pltpu.VMEM((tb, LRU_WIDTH), F32), pltpu.VMEM((tb, LRU_WIDTH), F32)],
        compiler_params=_cparams(("arbitrary",)),
    )(u, cw, cb, wrg, brg, wig, big, lam)


def _rglru_bwd(u, hs, dhs, cw, cb, wrg, brg, wig, big, lam):
    rows = u.shape[0]
    tb = _row_block(rows)
    nblk = rows // tb
    tiles = tb // SUBLANES

    def body(u_ref, up_ref, hs_ref, hp_ref, dhs_ref, cw_ref, cb_ref, wrg_ref, brg_ref, wig_ref, big_ref, lam_ref,
             du_ref, dcw_ref, dcb_ref, dwrg_ref, dbrg_ref, dwig_ref, dbig_ref, dlam_ref,
             gcar, duc_head, a_s, b_s, g_s, dsp_acc):
        step = pl.program_id(0)
        blk_ix = nblk - 1 - step

        @pl.when(step == 0)
        def _():
            for ref in (dcw_ref, dcb_ref, dwrg_ref, dbrg_ref, dwig_ref, dbig_ref, gcar, duc_head, dsp_acc):
                ref[...] = jnp.zeros_like(ref)

        first = blk_ix == 0
        u = u_ref[...]
        u_prev = jnp.where(first, 0.0, up_ref[...])
        h_prev_tile = jnp.where(first, 0.0, hp_ref[...])
        gt = _lru_gates(u, u_prev, cw_ref, cb_ref, wrg_ref, brg_ref, wig_ref, big_ref, lam_ref, first)
        a, r, ig, uc, mult = gt["a"], gt["r"], gt["ig"], gt["uc"], gt["mult"]
        dhs_v = dhs_ref[...]

        a_s[...] = a
        b_s[...] = a * dhs_v
        row8 = lax.broadcasted_iota(jnp.int32, (SUBLANES, LRU_WIDTH), 0)

        def tile(tix, carry):
            rs = pl.ds(pl.multiple_of((tiles - 1 - tix) * SUBLANES, SUBLANES), SUBLANES)
            av, bv = a_s[rs, :], b_s[rs, :]
            for k in (1, 2, 4):
                keep = row8 < SUBLANES - k
                bv = jnp.where(keep, av * pltpu.roll(bv, SUBLANES - k, 0) + bv, bv)
                av = jnp.where(keep, av * pltpu.roll(av, SUBLANES - k, 0), av)
            g8 = av * carry + bv
            g_s[rs, :] = g8
            return jnp.broadcast_to(g8[0:1, :], (SUBLANES, LRU_WIDTH))

        g_next = gcar[...]
        gcar[...] = lax.fori_loop(0, tiles, tile, g_next)
        g = dhs_v + _shift_up(g_s[...], g_next, 1)

        h_prev = _shift_down(h_prev_tile, hs_ref[...], 1)
        da = g * h_prev
        iu = ig * uc
        dmult = jnp.where(gt["is_start"], 0.0, g * iu)
        d_ig = g * mult * uc
        duc = g * mult * ig
        dlog_a = da * a - dmult * (a * a) / gt["mult_raw"]
        dzr = (dlog_a * ((-LRU_C) * gt["sp"])) * r * (1.0 - r)
        dsp_acc[...] += jnp.sum(dlog_a * ((-LRU_C) * r), axis=0, keepdims=True)
        dzi = d_ig * ig * (1.0 - ig)
        dbrg_ref[...] += jnp.sum(dzr, axis=0, keepdims=True)
        dbig_ref[...] += jnp.sum(dzi, axis=0, keepdims=True)
        dzr_b, dzi_b = dzr.astype(BF16), dzi.astype(BF16)
        ub = gt["ub"]
        duc_parts = []
        for gi in range(LRU_BLOCKS):
            cs = slice(gi * LRU_BLOCK, (gi + 1) * LRU_BLOCK)
            dwrg_ref[gi] += lax.dot_general(ub[:, cs], dzr_b[:, cs], TN, preferred_element_type=F32)
            dwig_ref[gi] += lax.dot_general(ub[:, cs], dzi_b[:, cs], TN, preferred_element_type=F32)
            duc_parts.append(lax.dot_general(dzr_b[:, cs], wrg_ref[gi], NT, preferred_element_type=F32)
                             + lax.dot_general(dzi_b[:, cs], wig_ref[gi], NT, preferred_element_type=F32))
        duc = duc + jnp.concatenate(duc_parts, axis=1)

        dcb_ref[...] += jnp.sum(duc, axis=0, keepdims=True)
        taps = gt["taps"]
        for jt in range(CONV_WIDTH):
            dcw_ref[jt:jt + 1, :] += jnp.sum(duc * taps[jt], axis=0, keepdims=True)
        head = duc_head[...]
        du = duc * cw_ref[CONV_WIDTH - 1:CONV_WIDTH, :]
        for jt in range(CONV_WIDTH - 1):
            du = du + _shift_up(duc, head, CONV_WIDTH - 1 - jt) * cw_ref[jt:jt + 1, :]
        du_ref[...] = du.astype(BF16)
        duc_head[...] = duc[:SUBLANES, :]

        @pl.when(step == nblk - 1)
        def _():
            dlam_ref[...] = -dsp_acc[...] * jax.nn.sigmoid(-lam_ref[...])

    full2 = lambda shape: pl.BlockSpec(shape, lambda s: (0, 0))
    full3 = lambda shape: pl.BlockSpec(shape, lambda s: (0, 0, 0))
    blk = pl.BlockSpec((tb, LRU_WIDTH), lambda s: (nblk - 1 - s, 0))
    prev_tile = pl.BlockSpec((SUBLANES, LRU_WIDTH), lambda s: (jnp.maximum((nblk - 1 - s) * tiles - 1, 0), 0))
    wshape = (LRU_BLOCKS, LRU_BLOCK, LRU_BLOCK)
    return pl.pallas_call(
        body, name="rglru_bwd", grid=(nblk,),
        in_specs=[blk, prev_tile, blk, prev_tile, blk, full2((CONV_WIDTH, LRU_WIDTH)), full2((1, LRU_WIDTH)),
                  full3(wshape), full2((1, LRU_WIDTH)), full3(wshape), full2((1, LRU_WIDTH)), full2((1, LRU_WIDTH))],
        out_specs=[blk, full2((CONV_WIDTH, LRU_WIDTH)), full2((1, LRU_WIDTH)), full3(wshape), full2((1, LRU_WIDTH)),
                   full3(wshape), full2((1, LRU_WIDTH)), full2((1, LRU_WIDTH))],
        out_shape=[jax.ShapeDtypeStruct((rows, LRU_WIDTH), BF16),
                   jax.ShapeDtypeStruct((CONV_WIDTH, LRU_WIDTH), F32), jax.ShapeDtypeStruct((1, LRU_WIDTH), F32),
                   jax.ShapeDtypeStruct(wshape, F32), jax.ShapeDtypeStruct((1, LRU_WIDTH), F32),
                   jax.ShapeDtypeStruct(wshape, F32), jax.ShapeDtypeStruct((1, LRU_WIDTH), F32),
                   jax.ShapeDtypeStruct((1, LRU_WIDTH), F32)],
        scratch_shapes=[pltpu.VMEM((SUBLANES, LRU_WIDTH), F32), pltpu.VMEM((SUBLANES, LRU_WIDTH), F32),
                        pltpu.VMEM((tb, LRU_WIDTH), F32), pltpu.VMEM((tb, LRU_WIDTH), F32),
                        pltpu.VMEM((tb, LRU_WIDTH), F32), pltpu.VMEM((1, LRU_WIDTH), F32)],
        compiler_params=_cparams(("arbitrary",)),
    )(u, u, hs, hs, dhs, cw, cb, wrg, brg, wig, big, lam)


def _out_proj_loss(a, gate, h, w, gf, target, n_real):
    rows = h.shape[0]
    tr = _row_block(rows)

    def body(a_ref, gate_ref, h_ref, w_ref, g_ref, t_ref, dh_ref, loss_ref, dg_ref, da_ref, dgate_ref, dw_ref,
             dw_acc):
        i = pl.program_id(0)

        @pl.when(i == 0)
        def _():
            loss_ref[...] = jnp.zeros_like(loss_ref)
            dg_ref[...] = jnp.zeros_like(dg_ref)
            dw_acc[...] = jnp.zeros_like(dw_acc)

        gv = g_ref[...]
        av, gatev = a_ref[...], gate_ref[...]
        sg = _silu(gatev)
        y = (av * sg).astype(BF16)
        xn, r = _rms_fwd(h_ref[...] + jnp.dot(y, w_ref[...], preferred_element_type=F32))
        row = i * tr + lax.broadcasted_iota(jnp.int32, (tr, 1), 0)
        live = jnp.logical_and(row >= N_META, row < n_real)
        tgt = t_ref[...]
        tgt = jnp.where(i == 0, pltpu.roll(tgt, N_META, 0), tgt)
        err = jnp.where(live, xn * gv - tgt, 0.0)
        loss_ref[...] += (0.5 / D_MODEL) * jnp.sum(jnp.sum(err * err, axis=1, keepdims=True), axis=0, keepdims=True)
        dx, dg = _rms_bwd(err * (1.0 / D_MODEL), xn, r, gv)
        dg_ref[...] += dg
        dh_ref[...] = dx
        dhb = dx.astype(BF16)
        dw_acc[...] += lax.dot_general(y, dhb, TN, preferred_element_type=F32)
        dy = lax.dot_general(dhb, w_ref[...], NT, preferred_element_type=F32)
        da_ref[...] = dy * sg
        dgate_ref[...] = (dy * av * _dsilu(gatev)).astype(BF16)

        @pl.when(i == rows // tr - 1)
        def _():
            dw_ref[...] = dw_acc[...].astype(BF16)

    blk = pl.BlockSpec((tr, D_MODEL), lambda i: (i, 0))
    wblk = pl.BlockSpec((D_MODEL, D_MODEL), lambda i: (0, 0))
    window = pl.BlockSpec((pl.Element(tr, (0, rows - n_real)), pl.Element(D_MODEL)),
                          lambda i: (pl.multiple_of(jnp.maximum(i * tr - N_META, 0), SUBLANES), 0))
    return pl.pallas_call(
        body, name="b_out_loss", grid=(rows // tr,),
        in_specs=[blk, blk, blk, wblk, pl.BlockSpec((1, D_MODEL), lambda i: (0, 0)), window],
        out_specs=[blk, pl.BlockSpec((1, 1), lambda i: (0, 0)), pl.BlockSpec((1, D_MODEL), lambda i: (0, 0)),
                   blk, blk, wblk],
        out_shape=[jax.ShapeDtypeStruct((rows, D_MODEL), F32), jax.ShapeDtypeStruct((1, 1), F32),
                   jax.ShapeDtypeStruct((1, D_MODEL), F32), jax.ShapeDtypeStruct((rows, D_MODEL), F32),
                   jax.ShapeDtypeStruct((rows, D_MODEL), BF16), jax.ShapeDtypeStruct((D_MODEL, D_MODEL), BF16)],
        scratch_shapes=[pltpu.VMEM((D_MODEL, D_MODEL), F32)],
        compiler_params=_cparams(("arbitrary",)),
    )(a, gate, h, w, gf, target)


def _my_place():
    x, y, c = lax.axis_index("x"), lax.axis_index("y"), lax.axis_index("c")
    return x, y, c, 4 * x + 2 * y + c


def _peer(x, y, c, k):
    px, py, pc = x ^ (k >> 2), y ^ ((k >> 1) & 1), c ^ (k & 1)
    return (px, py, pc), 4 * px + 2 * py + pc


def _exchange_copies(src_of, dst_ref, send_sems, recv_sems, local_sem):
    x, y, c, me = _my_place()
    copies = [pltpu.make_async_copy(src_of(me), dst_ref.at[me], local_sem)]
    for k in range(1, N_DEV):
        peer, pid = _peer(x, y, c, k)
        copies.append(pltpu.make_async_remote_copy(
            src_ref=src_of(pid), dst_ref=dst_ref.at[me], send_sem=send_sems.at[k], recv_sem=recv_sems.at[k],
            device_id=peer, device_id_type=MESH))
    return copies


def _exchange_sems(nb):
    return [pltpu.SemaphoreType.DMA((nb, N_DEV)), pltpu.SemaphoreType.DMA((nb, N_DEV)), pltpu.SemaphoreType.DMA((nb,))]


def _sum_blocks(lands, name):
    n = len(lands)

    def body(*refs):
        for land_ref, out_ref in zip(refs[:n], refs[n:]):
            acc = land_ref[0].astype(F32)
            for d in range(1, N_DEV):
                acc = acc + land_ref[d].astype(F32)
            out_ref[...] = acc

    return pl.pallas_call(
        body, name=name, out_shape=[jax.ShapeDtypeStruct(l.shape[1:], F32) for l in lands],
        compiler_params=pltpu.CompilerParams(vmem_limit_bytes=VMEM_LIMIT),
    )(*lands)


def _all_gather(big, small):
    def body(big_ref, small_ref, obig_ref, osmall_ref, send_sems, recv_sems, local_sems):
        x, y, c, _ = _my_place()
        me, sibling = (x, y, c), (x, y, 1 - c)
        chips = [(1 - x, y), (x, 1 - y), (1 - x, 1 - y)]
        parts = ((big_ref, obig_ref), (small_ref, osmall_ref))

        def slot(dst, place):
            return dst.at[4 * place[0] + 2 * place[1] + place[2]]

        def copy(part, k, block, to, first_hand=False):
            src, dst = parts[part]
            return pltpu.make_async_remote_copy(
                src_ref=src if first_hand else slot(dst, block), dst_ref=slot(dst, block),
                send_sem=send_sems.at[part, k], recv_sem=recv_sems.at[part, k], device_id=to, device_id_type=MESH)

        own = [pltpu.make_async_copy(src, slot(dst, me), local_sems.at[part]) for part, (src, dst) in enumerate(parts)]
        for cp in own:
            cp.start()
        first = []
        for part in range(len(parts)):
            first.append(copy(part, 0, me, sibling, True))
            first += [copy(part, 1 + j, me, (*chip, c), True) for j, chip in enumerate(chips)]
        for cp in first:
            cp.start()
        passed = []
        for j, chip in enumerate(chips):
            for part in range(len(parts)):
                copy(part, 1 + j, (*chip, c), me).wait_recv()
                passed.append(copy(part, 4 + j, (*chip, c), sibling))
                passed[-1].start()
        for part in range(len(parts)):
            copy(part, 0, sibling, me).wait_recv()
            for j, chip in enumerate(chips):
                copy(part, 4 + j, (*chip, 1 - c), me).wait_recv()
        for cp in first + passed:
            cp.wait_send()
        for cp in own:
            cp.wait()

    n = big.shape[0]
    hbm = pl.BlockSpec(memory_space=pl.ANY)
    return pl.pallas_call(
        body, name="weight_all_gather",
        in_specs=[hbm, hbm], out_specs=[hbm, hbm],
        out_shape=[jax.ShapeDtypeStruct((N_DEV,) + big.shape, BF16), jax.ShapeDtypeStruct((N_DEV,) + small.shape, F32)],
        scratch_shapes=[pltpu.SemaphoreType.DMA((2, N_DEV)), pltpu.SemaphoreType.DMA((2, N_DEV)),
                        pltpu.SemaphoreType.DMA((2,))],
        compiler_params=pltpu.CompilerParams(has_side_effects=True),
    )(big, small)


GRAD_CHUNK = 32


def _grad_exchange(gbig, rep):
    n, width = gbig.shape[1:]
    nrep = rep.shape[0]
    n_chips = N_DEV // 2

    def body(gbig_ref, rep_ref, out_ref, orep_ref, pre, stage, got, own_sum, land_rep, send_sems, recv_sems,
             local_sem):
        x, y, c, me = _my_place()
        my_chip = 2 * x + y
        sibling = (x, y, 1 - c)

        local = pltpu.make_async_copy(rep_ref, land_rep.at[me], local_sem.at[0])
        local.start()
        rep_copies = []
        for k in range(1, N_DEV):
            peer, _ = _peer(x, y, c, k)
            rep_copies.append(pltpu.make_async_remote_copy(
                src_ref=rep_ref, dst_ref=land_rep.at[me], send_sem=send_sems.at[6 + k], recv_sem=recv_sems.at[6 + k],
                device_id=peer, device_id_type=MESH))
        swaps = [pltpu.make_async_remote_copy(
            src_ref=gbig_ref.at[2 * q + (1 - c)], dst_ref=pre.at[q], send_sem=send_sems.at[q], recv_sem=recv_sems.at[q],
            device_id=sibling, device_id_type=MESH) for q in range(n_chips)]
        for cp in rep_copies + swaps:
            cp.start()
        for cp in swaps:
            cp.wait_recv()

        def pair_sums(ci, carry):
            rs = pl.ds(pl.multiple_of(ci * GRAD_CHUNK, GRAD_CHUNK), GRAD_CHUNK)
            for q in range(n_chips):
                stage[q, rs, :] = (gbig_ref[2 * q + c, rs, :].astype(F32) + pre[q, rs, :].astype(F32)).astype(BF16)
            own_sum[rs, :] = gbig_ref[me, rs, :].astype(F32) + pre[my_chip, rs, :].astype(F32)
            return carry

        lax.fori_loop(0, n // GRAD_CHUNK, pair_sums, 0)

        hops = []
        for rel in range(1, n_chips):
            qx, qy = x ^ (rel >> 1), y ^ (rel & 1)
            hops.append(pltpu.make_async_remote_copy(
                src_ref=stage.at[2 * qx + qy], dst_ref=got.at[my_chip], send_sem=send_sems.at[3 + rel],
                recv_sem=recv_sems.at[3 + rel], device_id=(qx, qy, c), device_id_type=MESH))
        for cp in hops:
            cp.start()
        for cp in hops:
            cp.wait_recv()

        def chip_sums(ci, carry):
            rs = pl.ds(pl.multiple_of(ci * GRAD_CHUNK, GRAD_CHUNK), GRAD_CHUNK)
            mine = own_sum[rs, :]
            acc = jnp.where(my_chip == 0, mine, got[0, rs, :].astype(F32))
            for q in range(1, n_chips):
                acc = acc + jnp.where(my_chip == q, mine, got[q, rs, :].astype(F32))
            out_ref[rs, :] = acc
            return carry

        lax.fori_loop(0, n // GRAD_CHUNK, chip_sums, 0)

        for cp in rep_copies:
            cp.wait_recv()
        local.wait()
        acc = land_rep[0]
        for d in range(1, N_DEV):
            acc = acc + land_rep[d]
        orep_ref[...] = acc
        for cp in swaps + hops + rep_copies:
            cp.wait_send()

    return pl.pallas_call(
        body, name="grad_exchange",
        in_specs=[pl.BlockSpec(memory_space=pltpu.VMEM), pl.BlockSpec(memory_space=pltpu.VMEM)],
        out_specs=[pl.BlockSpec(memory_space=pltpu.VMEM), pl.BlockSpec(memory_space=pltpu.VMEM)],
        out_shape=[jax.ShapeDtypeStruct((n, width), F32), jax.ShapeDtypeStruct((nrep, LANES), F32)],
        scratch_shapes=[pltpu.VMEM((n_chips, n, width), BF16), pltpu.VMEM((n_chips, n, width), BF16),
                        pltpu.VMEM((n_chips, n, width), BF16), pltpu.VMEM((n, width), F32),
                        pltpu.VMEM((N_DEV, nrep, LANES), F32),
                        pltpu.SemaphoreType.DMA((2 * N_DEV - 2,)), pltpu.SemaphoreType.DMA((2 * N_DEV - 2,)),
                        pltpu.SemaphoreType.DMA((1,))],
        compiler_params=pltpu.CompilerParams(vmem_limit_bytes=VMEM_LIMIT, has_side_effects=True),
    )(gbig, rep)


def _adamw_all(ws, gs, ms, vs):
    n = len(ws)

    def body(*refs):
        w_refs, g_refs, m_refs, v_refs = refs[0:n], refs[n:2 * n], refs[2 * n:3 * n], refs[3 * n:4 * n]
        d_refs, nm_refs, nv_refs = refs[4 * n:5 * n], refs[5 * n:6 * n], refs[6 * n:7 * n]
        for w_ref, g_ref, m_ref, v_ref, d_ref, nm_ref, nv_ref in zip(w_refs, g_refs, m_refs, v_refs, d_refs, nm_refs, nv_refs):
            g = g_ref[...]
            m = ADAM_B1 * m_ref[...] + (1.0 - ADAM_B1) * g
            v = ADAM_B2 * v_ref[...] + (1.0 - ADAM_B2) * jnp.square(g)
            m_hat = m / (1.0 - ADAM_B1 ** ADAM_STEP)
            v_hat = v / (1.0 - ADAM_B2 ** ADAM_STEP)
            d_ref[...] = -ADAM_LR * (m_hat / (jnp.sqrt(v_hat) + ADAM_EPS) + ADAM_WD * w_ref[...])
            nm_ref[...] = m
            nv_ref[...] = v

    shapes = [jax.ShapeDtypeStruct(w.shape, F32) for w in ws]
    outs = pl.pallas_call(
        body, name="adamw", out_shape=shapes * 3,
        compiler_params=pltpu.CompilerParams(vmem_limit_bytes=VMEM_LIMIT),
    )(*ws, *gs, *ms, *vs)
    return outs[0:n], outs[n:2 * n], outs[2 * n:3 * n]


SMALL_A = (("meta_tokens", 16),)
SMALL_B = (("b_norm_g", 1), ("b_conv_w", 4), ("b_conv_b", 1), ("b_b_rg", 1), ("b_b_ig", 1), ("b_lam", 1))
REP = (("a_norm_g", 8), ("a_q_norm_g", 3), ("a_kv_norm_g", 2), ("final_norm_g", 8), ("loss", 1))
SLOT = 16


def _offsets(table, slot=1, start=0):
    out, o = {}, start
    for name, n in table:
        out[name] = (o, n)
        o += -(-n // slot) * slot
    return out, o


def _slotted(a, axis):
    pad = -a.shape[axis] % SLOT
    if not pad:
        return a
    widths = [(0, 0)] * a.ndim
    widths[axis] = (0, pad)
    return jnp.pad(a, widths)


def _rope_tables(rows):
    pos = np.arange(rows, dtype=np.float32)
    inv_freq = (np.float32(ROPE_BASE) ** (-np.arange(0, QK_ROPE, 2, dtype=np.float32) / np.float32(QK_ROPE))).astype(
        np.float32)
    ang = pos[:, None] * inv_freq[None, :]
    cos, sin = np.cos(ang).astype(np.float32), np.sin(ang).astype(np.float32)
    zeros = np.zeros((rows, LANES - QK_ROPE), np.float32)
    return jnp.asarray(np.concatenate([cos, cos, zeros], axis=1)), jnp.asarray(np.concatenate([-sin, sin, zeros], axis=1))


def kernel(x, meta_tokens, a_norm_g, a_w_in, a_q_norm_g, a_kv_norm_g, a_w_uq, a_w_ukv, a_w_out, b_norm_g, b_w_in, b_conv_w, b_conv_b, b_w_rg, b_b_rg, b_w_ig, b_b_ig, b_lam, b_w_out, final_norm_g, loss_target, m_meta_tokens, m_a_norm_g, m_a_w_in, m_a_q_norm_g, m_a_kv_norm_g, m_a_w_uq, m_a_w_ukv, m_a_w_out, m_b_norm_g, m_b_w_in, m_b_conv_w, m_b_conv_b, m_b_w_rg, m_b_b_rg, m_b_w_ig, m_b_b_ig, m_b_lam, m_b_w_out, m_final_norm_g, v_meta_tokens, v_a_norm_g, v_a_w_in, v_a_q_norm_g, v_a_kv_norm_g, v_a_w_uq, v_a_w_ukv, v_a_w_out, v_b_norm_g, v_b_w_in, v_b_conv_w, v_b_conv_b, v_b_w_rg, v_b_b_rg, v_b_w_ig, v_b_b_ig, v_b_lam, v_b_w_out, v_final_norm_g):
    names = ("meta_tokens", "a_norm_g", "a_w_in", "a_q_norm_g", "a_kv_norm_g", "a_w_uq", "a_w_ukv", "a_w_out",
             "b_norm_g", "b_w_in", "b_conv_w", "b_conv_b", "b_w_rg", "b_b_rg", "b_w_ig", "b_b_ig", "b_lam", "b_w_out",
             "final_norm_g")
    w = dict(zip(names, (meta_tokens, a_norm_g, a_w_in, a_q_norm_g, a_kv_norm_g, a_w_uq, a_w_ukv, a_w_out, b_norm_g,
                         b_w_in, b_conv_w, b_conv_b, b_w_rg, b_b_rg, b_w_ig, b_b_ig, b_lam, b_w_out, final_norm_g)))
    mom_m = dict(zip(names, (m_meta_tokens, m_a_norm_g, m_a_w_in, m_a_q_norm_g, m_a_kv_norm_g, m_a_w_uq, m_a_w_ukv,
                             m_a_w_out, m_b_norm_g, m_b_w_in, m_b_conv_w, m_b_conv_b, m_b_w_rg, m_b_b_rg, m_b_w_ig,
                             m_b_b_ig, m_b_lam, m_b_w_out, m_final_norm_g)))
    mom_v = dict(zip(names, (v_meta_tokens, v_a_norm_g, v_a_w_in, v_a_q_norm_g, v_a_kv_norm_g, v_a_w_uq, v_a_w_ukv,
                             v_a_w_out, v_b_norm_g, v_b_w_in, v_b_conv_w, v_b_conv_b, v_b_w_rg, v_b_b_rg, v_b_w_ig,
                             v_b_b_ig, v_b_lam, v_b_w_out, v_final_norm_g)))

    seq = x.shape[1]
    n_real = N_META + seq
    rows = -(-n_real // LANES) * LANES
    scale = (QK_NOPE + QK_ROPE) ** -0.5
    small_off, _ = _offsets(SMALL_A + SMALL_B, SLOT)
    gsmallb_off, _ = _offsets(SMALL_B, SLOT)
    rep_off, _ = _offsets(REP, SLOT)
    cdev_a = a_w_in.shape[-1]
    wide = 2 * LANES

    send_a0 = jnp.pad(a_w_in[0], ((0, 0), (0, wide - cdev_a))).astype(BF16)
    send_small = jnp.concatenate([_slotted(w[nm].reshape(-1, LANES), 0) for nm, _ in SMALL_A + SMALL_B], axis=0)
    sends_a1 = [jnp.pad(a_w_uq[0], ((0, 0), (0, HEAD_PAD - QK_NOPE - QK_ROPE))).astype(BF16), a_w_ukv[0].astype(BF16)]
    lru_rows = LRU_BLOCKS * LRU_BLOCK // N_DEV
    sends_b = [a_w_out[0].astype(BF16), b_w_in[0].astype(BF16), b_w_rg.reshape(lru_rows, LRU_BLOCK).astype(BF16),
               b_w_ig.reshape(lru_rows, LRU_BLOCK).astype(BF16), b_w_out[0].astype(BF16)]
    all_a0, all_small = _all_gather(send_a0, send_small)

    def small_seg(nm):
        o, n = small_off[nm]
        return all_small[:, o:o + n, :]

    w_in_a = all_a0[:, :, :cdev_a].transpose(1, 0, 2).reshape(D_MODEL, N_DEV * cdev_a)
    w_in_a = jnp.concatenate([w_in_a[:, :LAT + QK_ROPE], jnp.zeros((D_MODEL, LAT_PAD - LAT - QK_ROPE), BF16),
                              w_in_a[:, LAT + QK_ROPE:]], axis=1)[None]
    meta_full = small_seg("meta_tokens").transpose(1, 0, 2).reshape(N_META, D_MODEL)
    vec = lambda nm: small_seg(nm).reshape(1, D_MODEL)
    g_b, conv_b, b_rg, b_ig, lam = vec("b_norm_g"), vec("b_conv_b"), vec("b_b_rg"), vec("b_b_ig"), vec("b_lam")
    conv_w = small_seg("b_conv_w").transpose(1, 0, 2).reshape(CONV_WIDTH, LRU_WIDTH)
    g_a, g_q, g_kv = a_norm_g, a_q_norm_g, a_kv_norm_g
    g_f = final_norm_g.reshape(1, D_MODEL)

    h0 = jnp.concatenate([meta_full, x[0], jnp.zeros((rows - n_real, D_MODEL), F32)], axis=0)
    cos, sin = _rope_tables(rows)

    lat, gate_a, w_uq, w_ukv = _norm_proj_fwd(h0, g_a, w_in_a, LAT_PAD, "a_in_fwd", sends_a1)
    qc, kc, v, vt = _mla_qkv_fwd(lat, g_q, g_kv, w_uq, w_ukv, cos, sin, scale)
    o, lse, w_out_a, w_in_b, w_rg, w_ig, w_out_b = _attn_fwd(qc, kc, vt, sends_b)

    lru_w = lambda g: g.reshape(N_DEV, LRU_BLOCKS, LRU_BLOCK // N_DEV, LRU_BLOCK).transpose(1, 0, 2, 3).reshape(
        LRU_BLOCKS, LRU_BLOCK, LRU_BLOCK)
    w_out_a, w_out_b = w_out_a.reshape(D_MODEL, D_MODEL), w_out_b.reshape(D_MODEL, D_MODEL)
    w_rg, w_ig = lru_w(w_rg), lru_w(w_ig)

    h1, u, gate_b = _out_proj_in_proj(o, gate_a, h0, w_out_a, g_b, w_in_b, LRU_WIDTH, "a_out_b_in_fwd")
    hs = _rglru_fwd(u, conv_w, conv_b, w_rg, b_rg, w_ig, b_ig, lam)
    dh2, loss_part, dg_f, dhs, dgate_b, dw_out_b = _out_proj_loss(hs, gate_b, h1, w_out_b, g_f, loss_target[0],
                                                                   n_real)

    du, dconv_w, dconv_b, dw_rg, db_rg, dw_ig, db_ig, dlam = _rglru_bwd(u, hs, dhs, conv_w, conv_b, w_rg, b_rg, w_ig,
                                                                       b_ig, lam)
    dh1, dw_in_b, dg_b = _norm_proj_bwd(h1, g_b, w_in_b, du, dgate_b, dh2, "b_in_bwd")
    do, dgate_a, dw_out_a, delta = _attn_out_bwd(o, gate_a, dh1, w_out_a)

    def to_cols(g, cdev):
        r = g.shape[0]
        return g.reshape(r, N_DEV, cdev).transpose(1, 0, 2).reshape(N_DEV, -1, LANES)

    lru_g = lambda g: g.reshape(LRU_BLOCKS, N_DEV, LRU_BLOCK // N_DEV, LRU_BLOCK).transpose(1, 0, 2, 3).reshape(
        N_DEV, lru_rows, LRU_BLOCK)
    small_b = {"b_norm_g": dg_b, "b_conv_w": dconv_w, "b_conv_b": dconv_b, "b_b_rg": db_rg, "b_b_ig": db_ig,
               "b_lam": dlam}
    gsends_b = [dw_out_a.reshape(N_DEV, -1, D_MODEL), dw_in_b, lru_g(dw_rg).astype(BF16), lru_g(dw_ig).astype(BF16),
                dw_out_b.reshape(N_DEV, -1, D_MODEL),
                jnp.concatenate([_slotted(to_cols(small_b[nm], LANES).astype(BF16), 1) for nm, _ in SMALL_B], axis=1)]

    dqc, dkc, dv, *lands_b = _attn_bwd(qc, kc, v, lse, delta, do, gsends_b)
    g_out_a, g_in_b, g_rg, g_ig, g_out_b, gsum_small_b = _sum_blocks(lands_b, "sum_blocks_b")
    dlat, dw_uq, dw_ukv, dg_q, dg_kv = _mla_qkv_bwd(lat, g_q, g_kv, w_uq, w_ukv, cos, sin, dqc, dkc, dv, scale)
    dh0, dw_in_a, dg_a, *lands_a1 = _norm_proj_bwd(h0, g_a, w_in_a, dlat, dgate_a, dh1, "a_in_bwd", [dw_uq, dw_ukv])
    g_uq, g_ukv = _sum_blocks(lands_a1, "sum_blocks_a1")

    grad_x = dh0[N_META:n_real][None]

    dw_in_a_nat = jnp.concatenate([dw_in_a[0, :, :LAT + QK_ROPE], dw_in_a[0, :, LAT_PAD:]], axis=1)
    in_lanes = lambda g, cdev: jnp.pad(g.reshape(g.shape[0], N_DEV, cdev).transpose(1, 0, 2),
                                       ((0, 0), (0, 0), (0, wide - cdev)))
    pieces = [in_lanes(dw_in_a_nat, cdev_a), in_lanes(dh0[:N_META].astype(BF16), LANES)]
    used = sum(p.shape[1] for p in pieces)
    pieces.append(jnp.zeros((N_DEV, -used % GRAD_CHUNK, wide), BF16))
    gsend_a0 = jnp.concatenate(pieces, axis=1)
    rep_parts = {"a_norm_g": dg_a, "a_q_norm_g": dg_q, "a_kv_norm_g": dg_kv, "final_norm_g": dg_f,
                 "loss": jnp.broadcast_to(loss_part, (1, LANES))}
    rep = jnp.concatenate([_slotted(rep_parts[nm].reshape(-1, LANES), 0) for nm, _ in REP], axis=0)
    gsum_a0, rep_sum = _grad_exchange(gsend_a0, rep)

    grads = {"a_w_out": g_out_a, "b_w_in": g_in_b, "b_w_rg": g_rg, "b_w_ig": g_ig, "b_w_out": g_out_b,
             "a_w_uq": g_uq[:, :QK_NOPE + QK_ROPE], "a_w_ukv": g_ukv}
    grads = {nm: g.reshape(w[nm].shape) for nm, g in grads.items()}
    grads["a_w_in"] = gsum_a0[:D_MODEL, :cdev_a].reshape(w["a_w_in"].shape)
    grads["meta_tokens"] = gsum_a0[D_MODEL:D_MODEL + N_META, :LANES]
    for off, src in ((gsmallb_off, gsum_small_b), (rep_off, rep_sum)):
        for nm, (o_r, n) in off.items():
            if nm in w:
                grads[nm] = src[o_r:o_r + n].reshape(w[nm].shape)
    loss = rep_sum[rep_off["loss"][0], 0]

    as2d = lambda a: a.reshape(1, -1) if a.ndim == 1 else a
    deltas, new_ms, new_vs = _adamw_all([as2d(w[nm]) for nm in names], [as2d(grads[nm]) for nm in names],
                                        [as2d(mom_m[nm]) for nm in names], [as2d(mom_v[nm]) for nm in names])
    shaped = lambda arrs: [a.reshape(w[nm].shape) for a, nm in zip(arrs, names)]
    return (loss, grad_x, *[grads[nm] for nm in names], *shaped(deltas), *shaped(new_ms), *shaped(new_vs))
```

```python
import functools

import numpy as np
import jax
import jax.numpy as jnp
from jax import lax
from jax.experimental import pallas as pl
from jax.experimental.pallas import tpu as pltpu

F32 = jnp.float32
BF16 = jnp.bfloat16

D_MODEL = 1024
N_META = 16
RMS_EPS = 1e-6
HEADS = 8
QK_NOPE = 128
QK_ROPE = 64
V_HEAD = 128
Q_LORA = 384
KV_LORA = 256
HEAD_PAD = 256
LAT = Q_LORA + KV_LORA
LAT_PAD = LAT + 128
ROPE_BASE = 10000.0
MASK_VALUE = -1e30
LRU_WIDTH = 1024
LRU_BLOCKS = 4
LRU_BLOCK = 256
CONV_WIDTH = 4
LRU_C = 8.0
N_DEV = 8
ADAM_LR, ADAM_B1, ADAM_B2, ADAM_EPS, ADAM_WD, ADAM_STEP = 0.001, 0.9, 0.999, 1e-08, 0.01, 10

LANES = 128
SUBLANES = 8
VMEM_LIMIT = 56 * 1024 * 1024
MESH = pl.DeviceIdType.MESH

NT = (((1,), (1,)), ((), ()))
TN = (((0,), (0,)), ((), ()))


def _row_block(rows):
    return 384 if rows % 384 == 0 else 128


def _cparams(sem):
    return pltpu.CompilerParams(dimension_semantics=sem, vmem_limit_bytes=VMEM_LIMIT)


def _silu(x):
    return x * jax.nn.sigmoid(x)


def _dsilu(x):
    s = jax.nn.sigmoid(x)
    return s * (1.0 + x * (1.0 - s))


def _rms_fwd(x):
    r = lax.rsqrt(jnp.mean(x * x, axis=-1, keepdims=True) + RMS_EPS)
    return x * r, r


def _rms_bwd(dy, xn, r, g):
    t = dy * g
    dx = r * (t - xn * jnp.mean(t * xn, axis=-1, keepdims=True))
    return dx, jnp.sum(dy * xn, axis=0, keepdims=True)


def _expm1_neg(x):
    small = x * (1.0 + x * (1 / 2 + x * (1 / 6 + x * (1 / 24))))
    return jnp.where(x > -0.05, small, jnp.exp(x) - 1.0)


def _softplus_neg(lam):
    z = jnp.exp(-jnp.abs(lam))
    w = z / (2.0 + z)
    w2 = w * w
    series = 2.0 * w * (1.0 + w2 * (1 / 3) + w2 * w2 * (1 / 5))
    return jnp.maximum(-lam, 0.0) + jnp.where(z < 0.1, series, jnp.log(1.0 + z))


def _rider(sends, refs, first, last, all_to_all):
    nb = len(sends)
    if not nb:
        return (lambda: None), (lambda: None)
    send_refs, result_refs = refs[:nb], refs[nb:2 * nb]
    send_sems, recv_sems, local_sems = refs[2 * nb:]
    pick = (lambda ref: (lambda d: ref.at[d])) if all_to_all else (lambda ref: (lambda d: ref))

    def copies():
        out = []
        for b in range(nb):
            out += _exchange_copies(pick(send_refs[b]), result_refs[b], send_sems.at[b], recv_sems.at[b],
                                    local_sems.at[b])
        return out

    def start():
        @pl.when(first)
        def _():
            for cp in copies():
                cp.start()

    def wait():
        @pl.when(last)
        def _():
            for cp in copies():
                cp.wait()

    return start, wait


def _rider_specs(sends, all_to_all):
    nb = len(sends)
    if not nb:
        return [], [], [], []
    hbm = pl.BlockSpec(memory_space=pl.ANY)
    shapes = [jax.ShapeDtypeStruct(s.shape if all_to_all else (N_DEV,) + s.shape, s.dtype) for s in sends]
    return [hbm] * nb, [hbm] * nb, shapes, _exchange_sems(nb)


def _proj_blocks(x, w_ref):
    return jnp.concatenate([jnp.dot(x, w_ref[d], preferred_element_type=F32) for d in range(w_ref.shape[0])], axis=1)


def _embed_norm_proj_fwd(x, meta, rows, g, w, n1, name, wsends=()):
    n_real = N_META + x.shape[0]
    nb, _, cb = w.shape
    n = nb * cb
    tr = _row_block(rows)
    nsteps = rows // tr
    extra = len(wsends)

    def body(x_ref, meta_ref, g_ref, w_ref, *rest):
        h_ref, p1_ref, p2_ref = rest[extra:extra + 3]
        i = pl.program_id(0)
        start, wait = _rider(wsends, rest[:extra] + rest[extra + 3:], i == 0, i == nsteps - 1, False)
        start()
        xw = x_ref[...]
        xw = jnp.where(i == 0, pltpu.roll(xw, N_META, 0), xw)
        row = i * tr + lax.broadcasted_iota(jnp.int32, (tr, 1), 0)
        meta_rows = jnp.concatenate([meta_ref[...], jnp.zeros((tr - N_META, D_MODEL), F32)], axis=0)
        h = jnp.where(row < N_META, meta_rows, jnp.where(row < n_real, xw, 0.0))
        h_ref[...] = h
        xn, _ = _rms_fwd(h)
        p = _proj_blocks((xn * g_ref[...]).astype(BF16), w_ref)
        p1_ref[...] = p[:, :n1]
        p2_ref[...] = p[:, n1:]
        wait()

    r_in, r_out, r_shape, r_scratch = _rider_specs(wsends, False)
    window = pl.BlockSpec((pl.Element(tr, (0, rows - n_real)), pl.Element(D_MODEL)),
                          lambda i: (pl.multiple_of(jnp.maximum(i * tr - N_META, 0), SUBLANES), 0))
    return pl.pallas_call(
        body, name=name, grid=(nsteps,),
        in_specs=[window,
                  pl.BlockSpec((N_META, D_MODEL), lambda i: (0, 0)),
                  pl.BlockSpec((1, D_MODEL), lambda i: (0, 0)),
                  pl.BlockSpec((nb, D_MODEL, cb), lambda i: (0, 0, 0))] + r_in,
        out_specs=[pl.BlockSpec((tr, D_MODEL), lambda i: (i, 0)),
                   pl.BlockSpec((tr, n1), lambda i: (i, 0)),
                   pl.BlockSpec((tr, n - n1), lambda i: (i, 0))] + r_out,
        out_shape=[jax.ShapeDtypeStruct((rows, D_MODEL), F32), jax.ShapeDtypeStruct((rows, n1), F32),
                   jax.ShapeDtypeStruct((rows, n - n1), F32)] + r_shape,
        scratch_shapes=r_scratch,
        compiler_params=_cparams(("arbitrary",)),
    )(x, meta, g, w, *wsends)


def _out_proj_in_proj(a, gate, h, w_out, g, w_in, n1, name):
    rows = h.shape[0]
    nb, _, cb = w_in.shape
    n = nb * cb
    tr = _row_block(rows)

    def body(a_ref, gate_ref, h_ref, wo_ref, g_ref, wi_ref, hn_ref, p1_ref, p2_ref):
        y = (a_ref[...] * _silu(gate_ref[...])).astype(BF16)
        h_new = h_ref[...] + jnp.dot(y, wo_ref[...], preferred_element_type=F32)
        hn_ref[...] = h_new
        xn, _ = _rms_fwd(h_new)
        p = _proj_blocks((xn * g_ref[...]).astype(BF16), wi_ref)
        p1_ref[...] = p[:, :n1]
        p2_ref[...] = p[:, n1:]

    blk = pl.BlockSpec((tr, D_MODEL), lambda i: (i, 0))
    return pl.pallas_call(
        body, name=name, grid=(rows // tr,),
        in_specs=[blk, blk, blk, pl.BlockSpec((D_MODEL, D_MODEL), lambda i: (0, 0)),
                  pl.BlockSpec((1, D_MODEL), lambda i: (0, 0)), pl.BlockSpec((nb, D_MODEL, cb), lambda i: (0, 0, 0))],
        out_specs=[blk, pl.BlockSpec((tr, n1), lambda i: (i, 0)), pl.BlockSpec((tr, n - n1), lambda i: (i, 0))],
        out_shape=[jax.ShapeDtypeStruct((rows, D_MODEL), F32), jax.ShapeDtypeStruct((rows, n1), F32),
                   jax.ShapeDtypeStruct((rows, n - n1), F32)],
        compiler_params=_cparams(("parallel",)),
    )(a, gate, h, w_out, g, w_in)


def _norm_proj_bwd(h, g, w, dp1, dp2, dh_in, name, gsends=()):
    rows = h.shape[0]
    nb, _, cb = w.shape
    n1 = dp1.shape[1]
    n2 = nb * cb - n1
    tr = _row_block(rows)
    nsteps = rows // tr
    extra = len(gsends)

    def body(h_ref, g_ref, w_ref, dp1_ref, dp2_ref, dhin_ref, *rest):
        dh_ref, dw_ref, dg_ref = rest[extra:extra + 3]
        dw_acc = rest[2 * extra + 3]
        i = pl.program_id(0)
        start, wait = _rider(gsends, rest[:extra] + rest[extra + 3:2 * extra + 3] + rest[2 * extra + 4:],
                             i == 0, i == nsteps - 1, True)
        start()

        @pl.when(i == 0)
        def _():
            dw_acc[...] = jnp.zeros_like(dw_acc)
            dg_ref[...] = jnp.zeros_like(dg_ref)

        gv = g_ref[...]
        xn, r = _rms_fwd(h_ref[...])
        hn = (xn * gv).astype(BF16)
        dp = jnp.concatenate([dp1_ref[...].astype(BF16), dp2_ref[...].astype(BF16)], axis=1)
        dhn = jnp.zeros((tr, D_MODEL), F32)
        for d in range(nb):
            dpd = dp[:, d * cb:(d + 1) * cb]
            dw_acc[d] += lax.dot_general(hn, dpd, TN, preferred_element_type=F32)
            dhn = dhn + lax.dot_general(dpd, w_ref[d], NT, preferred_element_type=F32)
        dx, dg = _rms_bwd(dhn, xn, r, gv)
        dg_ref[...] += dg
        dh_ref[...] = dhin_ref[...] + dx

        @pl.when(i == nsteps - 1)
        def _():
            dw_ref[...] = dw_acc[...].astype(BF16)

        wait()

    r_in, r_out, r_shape, r_scratch = _rider_specs(gsends, True)
    wblk = pl.BlockSpec((nb, D_MODEL, cb), lambda i: (0, 0, 0))
    return pl.pallas_call(
        body, name=name, grid=(nsteps,),
        in_specs=[pl.BlockSpec((tr, D_MODEL), lambda i: (i, 0)),
                  pl.BlockSpec((1, D_MODEL), lambda i: (0, 0)),
                  wblk,
                  pl.BlockSpec((tr, n1), lambda i: (i, 0)),
                  pl.BlockSpec((tr, n2), lambda i: (i, 0)),
                  pl.BlockSpec((tr, D_MODEL), lambda i: (i, 0))] + r_in,
        out_specs=[pl.BlockSpec((tr, D_MODEL), lambda i: (i, 0)), wblk,
                   pl.BlockSpec((1, D_MODEL), lambda i: (0, 0))] + r_out,
        out_shape=[jax.ShapeDtypeStruct((rows, D_MODEL), F32),
                   jax.ShapeDtypeStruct((nb, D_MODEL, cb), BF16),
                   jax.ShapeDtypeStruct((1, D_MODEL), F32)] + r_shape,
        scratch_shapes=[pltpu.VMEM((nb, D_MODEL, cb), F32)] + r_scratch,
        compiler_params=_cparams(("arbitrary",)),
    )(h, g, w, dp1, dp2, dh_in, *gsends)


def _attn_out_bwd(o, gate, dh, w):
    rows = o.shape[0]
    tr = _row_block(rows)
    nsteps = rows // tr

    def body(o_ref, gate_ref, dh_ref, w_ref, do_ref, dgate_ref, dw_ref, delta_ref, dw_acc):
        i = pl.program_id(0)

        @pl.when(i == 0)
        def _():
            dw_acc[...] = jnp.zeros_like(dw_acc)

        ov, gv = o_ref[...], gate_ref[...]
        sg = _silu(gv)
        dhb = dh_ref[...].astype(BF16)
        dw_acc[...] += lax.dot_general((ov * sg).astype(BF16), dhb, TN, preferred_element_type=F32)
        dy = lax.dot_general(dhb, w_ref[...], NT, preferred_element_type=F32)
        do = (dy * sg).astype(BF16)
        do_ref[...] = do
        dgate_ref[...] = (dy * ov * _dsilu(gv)).astype(BF16)
        prod = do.astype(F32) * ov
        lane = lax.broadcasted_iota(jnp.int32, (tr, LANES), 1)
        per_head = jnp.zeros((tr, LANES), F32)
        for hd in range(HEADS):
            dsum = jnp.sum(prod[:, hd * V_HEAD:(hd + 1) * V_HEAD], axis=1, keepdims=True)
            per_head = jnp.where(lane == hd, dsum, per_head)
        delta_t = per_head.T
        for hd in range(HEADS):
            delta_ref[hd, 0] = delta_t[hd:hd + 1, :]

        @pl.when(i == nsteps - 1)
        def _():
            dw_ref[...] = dw_acc[...].astype(BF16)

    blk = pl.BlockSpec((tr, D_MODEL), lambda i: (i, 0))
    wblk = pl.BlockSpec((D_MODEL, D_MODEL), lambda i: (0, 0))
    return pl.pallas_call(
        body, name="a_out_bwd", grid=(nsteps,),
        in_specs=[blk, blk, blk, wblk],
        out_specs=[blk, blk, wblk, pl.BlockSpec((HEADS, 1, 1, tr), lambda i: (0, i, 0, 0))],
        out_shape=[jax.ShapeDtypeStruct((rows, D_MODEL), BF16), jax.ShapeDtypeStruct((rows, D_MODEL), BF16),
                   jax.ShapeDtypeStruct((D_MODEL, D_MODEL), BF16),
                   jax.ShapeDtypeStruct((HEADS, nsteps, 1, tr), F32)],
        scratch_shapes=[pltpu.VMEM((D_MODEL, D_MODEL), F32)],
        compiler_params=_cparams(("arbitrary",)),
    )(o, gate, dh, w)


def _rope(v, cos, sin, lane):
    swapped = jnp.where(lane < QK_ROPE // 2, pltpu.roll(v, LANES - QK_ROPE // 2, 1), pltpu.roll(v, QK_ROPE // 2, 1))
    return v * cos + swapped * sin


def _unrope(dv, cos, sin, lane):
    t = dv * sin
    swapped = jnp.where(lane < QK_ROPE // 2, pltpu.roll(t, LANES - QK_ROPE // 2, 1), pltpu.roll(t, QK_ROPE // 2, 1))
    return dv * cos + swapped


def _mla_qkv_fwd(lat, gq, gkv, wuq, wukv, cos, sin, scale):
    rows = lat.shape[0]
    tr = _row_block(rows)

    def body(lat_ref, gq_ref, gkv_ref, wuq_ref, wukv_ref, cos_ref, sin_ref, qc_ref, kc_ref, v_ref, vt_ref):
        qn, _ = _rms_fwd(lat_ref[:, :Q_LORA])
        kvn, _ = _rms_fwd(lat_ref[:, Q_LORA:LAT])
        qnb = (qn * gq_ref[...]).astype(BF16)
        kvnb = (kvn * gkv_ref[...]).astype(BF16)
        c, s = cos_ref[...], sin_ref[...]
        lane = lax.broadcasted_iota(jnp.int32, (tr, LANES), 1)
        kr = _rope(lat_ref[:, LAT:LAT_PAD], c, s, lane).astype(BF16)
        for hd in range(HEADS):
            o = hd * HEAD_PAD
            q = jnp.dot(qnb, wuq_ref[hd], preferred_element_type=F32)
            kv = jnp.dot(kvnb, wukv_ref[hd], preferred_element_type=F32)
            qc_ref[:, o:o + QK_NOPE] = (q[:, :QK_NOPE] * scale).astype(BF16)
            qc_ref[:, o + QK_NOPE:o + HEAD_PAD] = (_rope(q[:, QK_NOPE:], c, s, lane) * scale).astype(BF16)
            kc_ref[:, o:o + QK_NOPE] = kv[:, :QK_NOPE].astype(BF16)
            kc_ref[:, o + QK_NOPE:o + HEAD_PAD] = kr
            vh = kv[:, QK_NOPE:]
            v_ref[:, hd * V_HEAD:(hd + 1) * V_HEAD] = vh.astype(BF16)
            vt_ref[hd, 0] = vh.T.astype(BF16)

    full = lambda shape: pl.BlockSpec(shape, lambda i: (0,) * len(shape))
    rowb = lambda n: pl.BlockSpec((tr, n), lambda i: (i, 0))
    return pl.pallas_call(
        body, name="mla_qkv_fwd", grid=(rows // tr,),
        in_specs=[rowb(LAT_PAD), full((1, Q_LORA)), full((1, KV_LORA)), full((HEADS, Q_LORA, HEAD_PAD)),
                  full((HEADS, KV_LORA, HEAD_PAD)), rowb(LANES), rowb(LANES)],
        out_specs=[rowb(HEADS * HEAD_PAD), rowb(HEADS * HEAD_PAD), rowb(HEADS * V_HEAD),
                   pl.BlockSpec((HEADS, 1, V_HEAD, tr), lambda i: (0, i, 0, 0))],
        out_shape=[jax.ShapeDtypeStruct((rows, HEADS * HEAD_PAD), BF16),
                   jax.ShapeDtypeStruct((rows, HEADS * HEAD_PAD), BF16),
                   jax.ShapeDtypeStruct((rows, HEADS * V_HEAD), BF16),
                   jax.ShapeDtypeStruct((HEADS, rows // tr, V_HEAD, tr), BF16)],
        compiler_params=_cparams(("parallel",)),
    )(lat, gq, gkv, wuq, wukv, cos, sin)


def _mla_qkv_bwd(lat, gq, gkv, wuq, wukv, cos, sin, dqc, dkc, dv, scale):
    rows = lat.shape[0]
    tr = _row_block(rows)
    nsteps = rows // tr

    def body(lat_ref, gq_ref, gkv_ref, wuq_ref, wukv_ref, cos_ref, sin_ref, dqc_ref, dkc_ref, dv_ref,
             dlat_ref, dwuq_out, dwukv_out, dgq_ref, dgkv_ref, dwuq_ref, dwukv_ref):
        @pl.when(pl.program_id(0) == 0)
        def _():
            dwuq_ref[...] = jnp.zeros_like(dwuq_ref)
            dwukv_ref[...] = jnp.zeros_like(dwukv_ref)
            dgq_ref[...] = jnp.zeros_like(dgq_ref)
            dgkv_ref[...] = jnp.zeros_like(dgkv_ref)

        c, s = cos_ref[...], sin_ref[...]
        lane = lax.broadcasted_iota(jnp.int32, (tr, LANES), 1)
        gqv, gkvv = gq_ref[...], gkv_ref[...]
        qn, rq = _rms_fwd(lat_ref[:, :Q_LORA])
        kvn, rkv = _rms_fwd(lat_ref[:, Q_LORA:LAT])
        qnb = (qn * gqv).astype(BF16)
        kvnb = (kvn * gkvv).astype(BF16)
        dkr = jnp.zeros((tr, LANES), F32)
        dqn = jnp.zeros((tr, Q_LORA), F32)
        dkvn = jnp.zeros((tr, KV_LORA), F32)
        for hd in range(HEADS):
            o = hd * HEAD_PAD
            dq = jnp.concatenate(
                [dqc_ref[:, o:o + QK_NOPE],
                 _unrope(dqc_ref[:, o + QK_NOPE:o + HEAD_PAD].astype(F32), c, s, lane).astype(BF16)], axis=1)
            dkv = jnp.concatenate([dkc_ref[:, o:o + QK_NOPE], dv_ref[:, hd * V_HEAD:(hd + 1) * V_HEAD]], axis=1)
            dkr = dkr + dkc_ref[:, o + QK_NOPE:o + HEAD_PAD].astype(F32)
            dwuq_ref[hd] += scale * lax.dot_general(qnb, dq, TN, preferred_element_type=F32)
            dwukv_ref[hd] += lax.dot_general(kvnb, dkv, TN, preferred_element_type=F32)
            dqn = dqn + lax.dot_general(dq, wuq_ref[hd], NT, preferred_element_type=F32)
            dkvn = dkvn + lax.dot_general(dkv, wukv_ref[hd], NT, preferred_element_type=F32)
        dqn = scale * dqn
        dqlat, dgq = _rms_bwd(dqn, qn, rq, gqv)
        dkvlat, dgkv = _rms_bwd(dkvn, kvn, rkv, gkvv)
        dgq_ref[...] += dgq
        dgkv_ref[...] += dgkv
        dlat_ref[:, :Q_LORA] = dqlat.astype(BF16)
        dlat_ref[:, Q_LORA:LAT] = dkvlat.astype(BF16)
        dlat_ref[:, LAT:LAT_PAD] = _unrope(dkr, c, s, lane).astype(BF16)

        @pl.when(pl.program_id(0) == nsteps - 1)
        def _():
            dwuq_out[...] = dwuq_ref[...].astype(BF16)
            dwukv_out[...] = dwukv_ref[...].astype(BF16)

    full = lambda shape: pl.BlockSpec(shape, lambda i: (0,) * len(shape))
    rowb = lambda n: pl.BlockSpec((tr, n), lambda i: (i, 0))
    return pl.pallas_call(
        body, name="mla_qkv_bwd", grid=(nsteps,),
        in_specs=[rowb(LAT_PAD), full((1, Q_LORA)), full((1, KV_LORA)), full((HEADS, Q_LORA, HEAD_PAD)),
                  full((HEADS, KV_LORA, HEAD_PAD)), rowb(LANES), rowb(LANES),
                  rowb(HEADS * HEAD_PAD), rowb(HEADS * HEAD_PAD), rowb(HEADS * V_HEAD)],
        out_specs=[rowb(LAT_PAD), full((HEADS, Q_LORA, HEAD_PAD)), full((HEADS, KV_LORA, HEAD_PAD)),
                   full((1, Q_LORA)), full((1, KV_LORA))],
        out_shape=[jax.ShapeDtypeStruct((rows, LAT_PAD), BF16),
                   jax.ShapeDtypeStruct((HEADS, Q_LORA, HEAD_PAD), BF16),
                   jax.ShapeDtypeStruct((HEADS, KV_LORA, HEAD_PAD), BF16),
                   jax.ShapeDtypeStruct((1, Q_LORA), F32),
                   jax.ShapeDtypeStruct((1, KV_LORA), F32)],
        scratch_shapes=[pltpu.VMEM((HEADS, Q_LORA, HEAD_PAD), F32), pltpu.VMEM((HEADS, KV_LORA, HEAD_PAD), F32)],
        compiler_params=_cparams(("arbitrary",)),
    )(lat, gq, gkv, wuq, wukv, cos, sin, dqc, dkc, dv)


ATTN_UNROLL = 8


def _causal_mask_t(t):
    key = lax.broadcasted_iota(jnp.int32, (t, t), 0)
    query = lax.broadcasted_iota(jnp.int32, (t, t), 1)
    return key <= query


def _attn_fwd(qc, kc, vt, wsends):
    rows = qc.shape[0]
    t = _row_block(rows)
    nblk = rows // t
    nw = len(wsends)

    def body(q_ref, k_ref, vt_ref, *rest):
        o_ref, lse_ref = rest[nw:nw + 2]
        m_ref, l_ref, acc_ref, st_a, st_b = rest[2 * nw + 2:2 * nw + 7]
        i = pl.program_id(1)
        start, wait = _rider(wsends, rest[:nw] + rest[nw + 2:2 * nw + 2] + rest[2 * nw + 7:],
                             jnp.logical_and(pl.program_id(0) == 0, i == 0),
                             jnp.logical_and(pl.program_id(0) == HEADS - 1, i == nblk - 1), False)
        start()

        m_ref[...] = jnp.full_like(m_ref, MASK_VALUE)
        l_ref[...] = jnp.zeros_like(l_ref)
        acc_ref[...] = jnp.zeros_like(acc_ref)
        q = q_ref[...]

        def scores(j, st_ref):
            rs = pl.ds(pl.multiple_of(j * t, t), t)
            st_ref[...] = lax.dot_general(k_ref[rs, :], q, NT, preferred_element_type=F32)

        def consume(j, st_ref, masked):
            st = st_ref[...]
            if masked:
                st = jnp.where(_causal_mask_t(t), st, MASK_VALUE)
            m_prev = m_ref[...]
            m_new = jnp.maximum(m_prev, jnp.max(st, axis=0, keepdims=True))
            alpha = jnp.exp(m_prev - m_new)
            pt = jnp.exp(st - m_new)
            l_ref[...] = alpha * l_ref[...] + jnp.sum(pt, axis=0, keepdims=True)
            acc_ref[...] = alpha * acc_ref[...] + jnp.dot(vt_ref[0, j], pt.astype(BF16), preferred_element_type=F32)
            m_ref[...] = m_new

        bufs = (st_a, st_b)

        def step(j, parity, issue_next, masked):
            if issue_next:
                scores(j + 1, bufs[1 - parity])
            consume(j, bufs[parity], masked)

        scores(0, st_a)

        def trip(it, carry):
            for u in range(ATTN_UNROLL):
                step(it * ATTN_UNROLL + u, u % 2, True, False)
            return carry

        trips = i // ATTN_UNROLL
        lax.fori_loop(0, trips, trip, 0)
        j0 = trips * ATTN_UNROLL
        for left in range(1, ATTN_UNROLL + 1):
            @pl.when(i + 1 - j0 == left)
            def _(left=left):
                for u in range(left):
                    step(j0 + u, u % 2, u < left - 1, u == left - 1)

        o_ref[...] = (acc_ref[...] / l_ref[...]).T
        lse_ref[0, 0] = m_ref[...] + jnp.log(l_ref[...])
        wait()

    r_in, r_out, r_shape, r_scratch = _rider_specs(wsends, False)
    return pl.pallas_call(
        body, name="attn_fwd", grid=(HEADS, nblk),
        in_specs=[pl.BlockSpec((t, HEAD_PAD), lambda h, i: (i, h)),
                  pl.BlockSpec((rows, HEAD_PAD), lambda h, i: (0, h)),
                  pl.BlockSpec((1, nblk, V_HEAD, t), lambda h, i: (h, 0, 0, 0))] + r_in,
        out_specs=[pl.BlockSpec((t, V_HEAD), lambda h, i: (i, h)),
                   pl.BlockSpec((1, 1, 1, t), lambda h, i: (h, i, 0, 0))] + r_out,
        out_shape=[jax.ShapeDtypeStruct((rows, HEADS * V_HEAD), F32),
                   jax.ShapeDtypeStruct((HEADS, nblk, 1, t), F32)] + r_shape,
        scratch_shapes=[pltpu.VMEM((1, t), F32), pltpu.VMEM((1, t), F32), pltpu.VMEM((V_HEAD, t), F32),
                        pltpu.VMEM((t, t), F32), pltpu.VMEM((t, t), F32)] + r_scratch,
        compiler_params=_cparams(("arbitrary", "arbitrary")),
    )(qc, kc, vt, *wsends)


def _attn_bwd(qc, kc, v, lse, delta, do, gsends):
    rows = qc.shape[0]
    t = _row_block(rows)
    nblk = rows // t
    ng = len(gsends)

    def body(q_ref, k_ref, v_ref, lse_ref, delta_ref, do_ref, *rest):
        dq_ref, dk_ref, dv_ref = rest[ng:ng + 3]
        dq_acc, dk_acc, dv_acc, st_a, dp_a, st_b, dp_b = rest[2 * ng + 3:2 * ng + 10]
        j = pl.program_id(1)
        start, wait = _rider(gsends, rest[:ng] + rest[ng + 3:2 * ng + 3] + rest[2 * ng + 10:],
                             jnp.logical_and(pl.program_id(0) == 0, j == 0),
                             jnp.logical_and(pl.program_id(0) == HEADS - 1, j == nblk - 1), True)
        start()

        @pl.when(j == 0)
        def _():
            dq_acc[...] = jnp.zeros_like(dq_acc)

        dk_acc[...] = jnp.zeros_like(dk_acc)
        dv_acc[...] = jnp.zeros_like(dv_acc)
        k = k_ref[...]
        vv = v_ref[...]

        def products(i, st_ref, dp_ref):
            rs = pl.ds(pl.multiple_of(i * t, t), t)
            st_ref[...] = lax.dot_general(k, q_ref[rs, :], NT, preferred_element_type=F32)
            dp_ref[...] = lax.dot_general(vv, do_ref[rs, :], NT, preferred_element_type=F32)

        def consume(i, st_ref, dp_ref):
            rs = pl.ds(pl.multiple_of(i * t, t), t)
            q = q_ref[rs, :]
            dob = do_ref[rs, :]
            st = jnp.where(jnp.logical_or(_causal_mask_t(t), i != j), st_ref[...], MASK_VALUE)
            pt = jnp.exp(st - lse_ref[0, i])
            dv_acc[...] += jnp.dot(pt.astype(BF16), dob, preferred_element_type=F32)
            dst = (pt * (dp_ref[...] - delta_ref[0, i])).astype(BF16)
            dk_acc[...] += jnp.dot(dst, q, preferred_element_type=F32)
            dq_acc[rs, :] += lax.dot_general(dst, k, TN, preferred_element_type=F32)

        bufs = ((st_a, dp_a), (st_b, dp_b))

        def step(i, parity, issue_next):
            if issue_next:
                products(i + 1, *bufs[1 - parity])
            consume(i, *bufs[parity])

        products(j, st_a, dp_a)

        def trip(it, carry):
            for u in range(ATTN_UNROLL):
                step(j + it * ATTN_UNROLL + u, u % 2, True)
            return carry

        trips = (nblk - 1 - j) // ATTN_UNROLL
        lax.fori_loop(0, trips, trip, 0)
        i0 = j + trips * ATTN_UNROLL
        for left in range(1, ATTN_UNROLL + 1):
            @pl.when(nblk - i0 == left)
            def _(left=left):
                for u in range(left):
                    step(i0 + u, u % 2, u < left - 1)

        dk_ref[...] = dk_acc[...].astype(BF16)
        dv_ref[...] = dv_acc[...].astype(BF16)

        @pl.when(j == nblk - 1)
        def _():
            dq_ref[...] = dq_acc[...].astype(BF16)

        wait()

    stat = pl.BlockSpec((1, nblk, 1, t), lambda h, j: (h, 0, 0, 0))
    r_in, r_out, r_shape, r_scratch = _rider_specs(gsends, True)
    return pl.pallas_call(
        body, name="attn_bwd", grid=(HEADS, nblk),
        in_specs=[pl.BlockSpec((rows, HEAD_PAD), lambda h, j: (0, h)),
                  pl.BlockSpec((t, HEAD_PAD), lambda h, j: (j, h)),
                  pl.BlockSpec((t, V_HEAD), lambda h, j: (j, h)),
                  stat, stat,
                  pl.BlockSpec((rows, V_HEAD), lambda h, j: (0, h))] + r_in,
        out_specs=[pl.BlockSpec((rows, HEAD_PAD), lambda h, j: (0, h)),
                   pl.BlockSpec((t, HEAD_PAD), lambda h, j: (j, h)),
                   pl.BlockSpec((t, V_HEAD), lambda h, j: (j, h))] + r_out,
        out_shape=[jax.ShapeDtypeStruct((rows, HEADS * HEAD_PAD), BF16),
                   jax.ShapeDtypeStruct((rows, HEADS * HEAD_PAD), BF16),
                   jax.ShapeDtypeStruct((rows, HEADS * V_HEAD), BF16)] + r_shape,
        scratch_shapes=[pltpu.VMEM((rows, HEAD_PAD), F32), pltpu.VMEM((t, HEAD_PAD), F32),
                        pltpu.VMEM((t, V_HEAD), F32)] + [pltpu.VMEM((t, t), F32)] * 4 + r_scratch,
        compiler_params=_cparams(("arbitrary", "arbitrary")),
    )(qc, kc, v, lse, delta, do, *gsends)


def _shift_down(prev_tile, x, k):
    xx = jnp.concatenate([prev_tile, x], axis=0)
    return pltpu.roll(xx, k, 0)[SUBLANES:]


def _shift_up(x, next_tile, k):
    n = x.shape[0]
    xx = jnp.concatenate([x, next_tile], axis=0)
    return pltpu.roll(xx, n + SUBLANES - k, 0)[:n]


def _lru_gates(u, u_prev, cw_ref, cb_ref, wrg_ref, brg_ref, wig_ref, big_ref, lam_ref, first_block):
    taps = [_shift_down(u_prev, u, CONV_WIDTH - 1 - j) if j < CONV_WIDTH - 1 else u for j in range(CONV_WIDTH)]
    uc = cb_ref[...] + taps[0] * cw_ref[0:1, :]
    for j in range(1, CONV_WIDTH):
        uc = uc + taps[j] * cw_ref[j:j + 1, :]
    ub = uc.astype(BF16)
    zr = jnp.concatenate([jnp.dot(ub[:, g * LRU_BLOCK:(g + 1) * LRU_BLOCK], wrg_ref[g], preferred_element_type=F32)
                          for g in range(LRU_BLOCKS)], axis=1) + brg_ref[...]
    zi = jnp.concatenate([jnp.dot(ub[:, g * LRU_BLOCK:(g + 1) * LRU_BLOCK], wig_ref[g], preferred_element_type=F32)
                          for g in range(LRU_BLOCKS)], axis=1) + big_ref[...]
    r = jax.nn.sigmoid(zr)
    ig = jax.nn.sigmoid(zi)
    sp = _softplus_neg(lam_ref[...])
    log_a = (-LRU_C) * r * sp
    a = jnp.exp(log_a)
    m2 = -_expm1_neg(2.0 * log_a)
    mult_raw = m2 * lax.rsqrt(jnp.maximum(m2, 1e-30))
    row = lax.broadcasted_iota(jnp.int32, u.shape, 0)
    is_start = jnp.logical_and(first_block, row == 0)
    mult = jnp.where(is_start, 1.0, mult_raw)
    return dict(taps=taps, uc=uc, ub=ub, r=r, ig=ig, sp=sp, a=a, mult=mult, mult_raw=mult_raw, is_start=is_start)


def _rglru_fwd(u, cw, cb, wrg, brg, wig, big, lam):
    rows = u.shape[0]
    tb = _row_block(rows)

    def body(u_ref, cw_ref, cb_ref, wrg_ref, brg_ref, wig_ref, big_ref, lam_ref, hs_ref, utail, hcar, a_s, b_s):
        i = pl.program_id(0)

        @pl.when(i == 0)
        def _():
            utail[...] = jnp.zeros_like(utail)
            hcar[...] = jnp.zeros_like(hcar)

        u = u_ref[...]
        gt = _lru_gates(u, utail[...], cw_ref, cb_ref, wrg_ref, brg_ref, wig_ref, big_ref, lam_ref, i == 0)
        a_s[...] = gt["a"]
        b_s[...] = gt["mult"] * (gt["ig"] * gt["uc"])
        row8 = lax.broadcasted_iota(jnp.int32, (SUBLANES, LRU_WIDTH), 0)

        def tile(tix, carry):
            rs = pl.ds(pl.multiple_of(tix * SUBLANES, SUBLANES), SUBLANES)
            av, bv = a_s[rs, :], b_s[rs, :]
            for k in (1, 2, 4):
                keep = row8 >= k
                bv = jnp.where(keep, av * pltpu.roll(bv, k, 0) + bv, bv)
                av = jnp.where(keep, av * pltpu.roll(av, k, 0), av)
            h8 = av * carry + bv
            hs_ref[rs, :] = h8
            return jnp.broadcast_to(h8[SUBLANES - 1:SUBLANES, :], (SUBLANES, LRU_WIDTH))

        hcar[...] = lax.fori_loop(0, tb // SUBLANES, tile, hcar[...])
        utail[...] = u[tb - SUBLANES:, :]

    full2 = lambda shape: pl.BlockSpec(shape, lambda i: (0, 0))
    full3 = lambda shape: pl.BlockSpec(shape, lambda i: (0, 0, 0))
    blk = pl.BlockSpec((tb, LRU_WIDTH), lambda i: (i, 0))
    return pl.pallas_call(
        body, name="rglru_fwd", grid=(rows // tb,),
        in_specs=[blk, full2((CONV_WIDTH, LRU_WIDTH)), full2((1, LRU_WIDTH)),
                  full3((LRU_BLOCKS, LRU_BLOCK, LRU_BLOCK)), full2((1, LRU_WIDTH)),
                  full3((LRU_BLOCKS, LRU_BLOCK, LRU_BLOCK)), full2((1, LRU_WIDTH)), full2((1, LRU_WIDTH))],
        out_specs=blk,
        out_shape=jax.ShapeDtypeStruct((rows, LRU_WIDTH), F32),
        scratch_shapes=[pltpu.VMEM((SUBLANES, LRU_WIDTH), F32), pltpu.VMEM((SUBLANES, LRU_WIDTH), F32),
                        pltpu.VMEM((tb, LRU_WIDTH), F32), pltpu.VMEM((tb, LRU_WIDTH), F32)],
        compiler_params=_cparams(("arbitrary",)),
    )(u, cw, cb, wrg, brg, wig, big, lam)


def _rglru_bwd(u, hs, dhs, cw, cb, wrg, brg, wig, big, lam):
    rows = u.shape[0]
    tb = _row_block(rows)
    nblk = rows // tb
    tiles = tb // SUBLANES

    def body(u_ref, up_ref, hs_ref, hp_ref, dhs_ref, cw_ref, cb_ref, wrg_ref, brg_ref, wig_ref, big_ref, lam_ref,
             du_ref, dcw_ref, dcb_ref, dwrg_ref, dbrg_ref, dwig_ref, dbig_ref, dlam_ref,
             gcar, duc_head, a_s, b_s, g_s, dsp_acc):
        step = pl.program_id(0)
        blk_ix = nblk - 1 - step

        @pl.when(step == 0)
        def _():
            for ref in (dcw_ref, dcb_ref, dwrg_ref, dbrg_ref, dwig_ref, dbig_ref, gcar, duc_head, dsp_acc):
                ref[...] = jnp.zeros_like(ref)

        first = blk_ix == 0
        u = u_ref[...]
        u_prev = jnp.where(first, 0.0, up_ref[...])
        h_prev_tile = jnp.where(first, 0.0, hp_ref[...])
        gt = _lru_gates(u, u_prev, cw_ref, cb_ref, wrg_ref, brg_ref, wig_ref, big_ref, lam_ref, first)
        a, r, ig, uc, mult = gt["a"], gt["r"], gt["ig"], gt["uc"], gt["mult"]
        dhs_v = dhs_ref[...]

        a_s[...] = a
        b_s[...] = a * dhs_v
        row8 = lax.broadcasted_iota(jnp.int32, (SUBLANES, LRU_WIDTH), 0)

        def tile(tix, carry):
            rs = pl.ds(pl.multiple_of((tiles - 1 - tix) * SUBLANES, SUBLANES), SUBLANES)
            av, bv = a_s[rs, :], b_s[rs, :]
            for k in (1, 2, 4):
                keep = row8 < SUBLANES - k
                bv = jnp.where(keep, av * pltpu.roll(bv, SUBLANES - k, 0) + bv, bv)
                av = jnp.where(keep, av * pltpu.roll(av, SUBLANES - k, 0), av)
            g8 = av * carry + bv
            g_s[rs, :] = g8
            return jnp.broadcast_to(g8[0:1, :], (SUBLANES, LRU_WIDTH))

        g_next = gcar[...]
        gcar[...] = lax.fori_loop(0, tiles, tile, g_next)
        g = dhs_v + _shift_up(g_s[...], g_next, 1)

        h_prev = _shift_down(h_prev_tile, hs_ref[...], 1)
        da = g * h_prev
        iu = ig * uc
        dmult = jnp.where(gt["is_start"], 0.0, g * iu)
        d_ig = g * mult * uc
        duc = g * mult * ig
        dlog_a = da * a - dmult * (a * a) / gt["mult_raw"]
        dzr = (dlog_a * ((-LRU_C) * gt["sp"])) * r * (1.0 - r)
        dsp_acc[...] += jnp.sum(dlog_a * ((-LRU_C) * r), axis=0, keepdims=True)
        dzi = d_ig * ig * (1.0 - ig)
        dbrg_ref[...] += jnp.sum(dzr, axis=0, keepdims=True)
        dbig_ref[...] += jnp.sum(dzi, axis=0, keepdims=True)
        dzr_b, dzi_b = dzr.astype(BF16), dzi.astype(BF16)
        ub = gt["ub"]
        duc_parts = []
        for gi in range(LRU_BLOCKS):
            cs = slice(gi * LRU_BLOCK, (gi + 1) * LRU_BLOCK)
            dwrg_ref[gi] += lax.dot_general(ub[:, cs], dzr_b[:, cs], TN, preferred_element_type=F32)
            dwig_ref[gi] += lax.dot_general(ub[:, cs], dzi_b[:, cs], TN, preferred_element_type=F32)
            duc_parts.append(lax.dot_general(dzr_b[:, cs], wrg_ref[gi], NT, preferred_element_type=F32)
                             + lax.dot_general(dzi_b[:, cs], wig_ref[gi], NT, preferred_element_type=F32))
        duc = duc + jnp.concatenate(duc_parts, axis=1)

        dcb_ref[...] += jnp.sum(duc, axis=0, keepdims=True)
        taps = gt["taps"]
        for jt in range(CONV_WIDTH):
            dcw_ref[jt:jt + 1, :] += jnp.sum(duc * taps[jt], axis=0, keepdims=True)
        head = duc_head[...]
        du = duc * cw_ref[CONV_WIDTH - 1:CONV_WIDTH, :]
        for jt in range(CONV_WIDTH - 1):
            du = du + _shift_up(duc, head, CONV_WIDTH - 1 - jt) * cw_ref[jt:jt + 1, :]
        du_ref[...] = du.astype(BF16)
        duc_head[...] = duc[:SUBLANES, :]

        @pl.when(step == nblk - 1)
        def _():
            dlam_ref[...] = -dsp_acc[...] * jax.nn.sigmoid(-lam_ref[...])

    full2 = lambda shape: pl.BlockSpec(shape, lambda s: (0, 0))
    full3 = lambda shape: pl.BlockSpec(shape, lambda s: (0, 0, 0))
    blk = pl.BlockSpec((tb, LRU_WIDTH), lambda s: (nblk - 1 - s, 0))
    prev_tile = pl.BlockSpec((SUBLANES, LRU_WIDTH), lambda s: (jnp.maximum((nblk - 1 - s) * tiles - 1, 0), 0))
    wshape = (LRU_BLOCKS, LRU_BLOCK, LRU_BLOCK)
    return pl.pallas_call(
        body, name="rglru_bwd", grid=(nblk,),
        in_specs=[blk, prev_tile, blk, prev_tile, blk, full2((CONV_WIDTH, LRU_WIDTH)), full2((1, LRU_WIDTH)),
                  full3(wshape), full2((1, LRU_WIDTH)), full3(wshape), full2((1, LRU_WIDTH)), full2((1, LRU_WIDTH))],
        out_specs=[blk, full2((CONV_WIDTH, LRU_WIDTH)), full2((1, LRU_WIDTH)), full3(wshape), full2((1, LRU_WIDTH)),
                   full3(wshape), full2((1, LRU_WIDTH)), full2((1, LRU_WIDTH))],
        out_shape=[jax.ShapeDtypeStruct((rows, LRU_WIDTH), BF16),
                   jax.ShapeDtypeStruct((CONV_WIDTH, LRU_WIDTH), F32), jax.ShapeDtypeStruct((1, LRU_WIDTH), F32),
                   jax.ShapeDtypeStruct(wshape, F32), jax.ShapeDtypeStruct((1, LRU_WIDTH), F32),
                   jax.ShapeDtypeStruct(wshape, F32), jax.ShapeDtypeStruct((1, LRU_WIDTH), F32),
                   jax.ShapeDtypeStruct((1, LRU_WIDTH), F32)],
        scratch_shapes=[pltpu.VMEM((SUBLANES, LRU_WIDTH), F32), pltpu.VMEM((SUBLANES, LRU_WIDTH), F32),
                        pltpu.VMEM((tb, LRU_WIDTH), F32), pltpu.VMEM((tb, LRU_WIDTH), F32),
                        pltpu.VMEM((tb, LRU_WIDTH), F32), pltpu.VMEM((1, LRU_WIDTH), F32)],
        compiler_params=_cparams(("arbitrary",)),
    )(u, u, hs, hs, dhs, cw, cb, wrg, brg, wig, big, lam)


def _out_proj_loss(a, gate, h, w, gf, target, n_real):
    rows = h.shape[0]
    tr = _row_block(rows)

    def body(a_ref, gate_ref, h_ref, w_ref, g_ref, t_ref, dh_ref, loss_ref, dg_ref, da_ref, dgate_ref, dw_ref,
             dw_acc):
        i = pl.program_id(0)

        @pl.when(i == 0)
        def _():
            loss_ref[...] = jnp.zeros_like(loss_ref)
            dg_ref[...] = jnp.zeros_like(dg_ref)
            dw_acc[...] = jnp.zeros_like(dw_acc)

        gv = g_ref[...]
        av, gatev = a_ref[...], gate_ref[...]
        sg = _silu(gatev)
        y = (av * sg).astype(BF16)
        xn, r = _rms_fwd(h_ref[...] + jnp.dot(y, w_ref[...], preferred_element_type=F32))
        row = i * tr + lax.broadcasted_iota(jnp.int32, (tr, 1), 0)
        live = jnp.logical_and(row >= N_META, row < n_real)
        tgt = t_ref[...]
        tgt = jnp.where(i == 0, pltpu.roll(tgt, N_META, 0), tgt)
        err = jnp.where(live, xn * gv - tgt, 0.0)
        loss_ref[...] += (0.5 / D_MODEL) * jnp.sum(jnp.sum(err * err, axis=1, keepdims=True), axis=0, keepdims=True)
        dx, dg = _rms_bwd(err * (1.0 / D_MODEL), xn, r, gv)
        dg_ref[...] += dg
        dh_ref[...] = dx
        dhb = dx.astype(BF16)
        dw_acc[...] += lax.dot_general(y, dhb, TN, preferred_element_type=F32)
        dy = lax.dot_general(dhb, w_ref[...], NT, preferred_element_type=F32)
        da_ref[...] = dy * sg
        dgate_ref[...] = (dy * av * _dsilu(gatev)).astype(BF16)

        @pl.when(i == rows // tr - 1)
        def _():
            dw_ref[...] = dw_acc[...].astype(BF16)

    blk = pl.BlockSpec((tr, D_MODEL), lambda i: (i, 0))
    wblk = pl.BlockSpec((D_MODEL, D_MODEL), lambda i: (0, 0))
    window = pl.BlockSpec((pl.Element(tr, (0, rows - n_real)), pl.Element(D_MODEL)),
                          lambda i: (pl.multiple_of(jnp.maximum(i * tr - N_META, 0), SUBLANES), 0))
    return pl.pallas_call(
        body, name="b_out_loss", grid=(rows // tr,),
        in_specs=[blk, blk, blk, wblk, pl.BlockSpec((1, D_MODEL), lambda i: (0, 0)), window],
        out_specs=[blk, pl.BlockSpec((1, 1), lambda i: (0, 0)), pl.BlockSpec((1, D_MODEL), lambda i: (0, 0)),
                   blk, blk, wblk],
        out_shape=[jax.ShapeDtypeStruct((rows, D_MODEL), F32), jax.ShapeDtypeStruct((1, 1), F32),
                   jax.ShapeDtypeStruct((1, D_MODEL), F32), jax.ShapeDtypeStruct((rows, D_MODEL), F32),
                   jax.ShapeDtypeStruct((rows, D_MODEL), BF16), jax.ShapeDtypeStruct((D_MODEL, D_MODEL), BF16)],
        scratch_shapes=[pltpu.VMEM((D_MODEL, D_MODEL), F32)],
        compiler_params=_cparams(("arbitrary",)),
    )(a, gate, h, w, gf, target)


def _my_place():
    x, y, c = lax.axis_index("x"), lax.axis_index("y"), lax.axis_index("c")
    return x, y, c, 4 * x + 2 * y + c


def _peer(x, y, c, k):
    px, py, pc = x ^ (k >> 2), y ^ ((k >> 1) & 1), c ^ (k & 1)
    return (px, py, pc), 4 * px + 2 * py + pc


def _exchange_copies(src_of, dst_ref, send_sems, recv_sems, local_sem):
    x, y, c, me = _my_place()
    copies = [pltpu.make_async_copy(src_of(me), dst_ref.at[me], local_sem)]
    for k in range(1, N_DEV):
        peer, pid = _peer(x, y, c, k)
        copies.append(pltpu.make_async_remote_copy(
            src_ref=src_of(pid), dst_ref=dst_ref.at[me], send_sem=send_sems.at[k], recv_sem=recv_sems.at[k],
            device_id=peer, device_id_type=MESH))
    return copies


def _exchange_sems(nb):
    return [pltpu.SemaphoreType.DMA((nb, N_DEV)), pltpu.SemaphoreType.DMA((nb, N_DEV)), pltpu.SemaphoreType.DMA((nb,))]


def _sum_blocks(lands, name):
    n = len(lands)

    def body(*refs):
        for land_ref, out_ref in zip(refs[:n], refs[n:]):
            acc = land_ref[0].astype(F32)
            for d in range(1, N_DEV):
                acc = acc + land_ref[d].astype(F32)
            out_ref[...] = acc

    return pl.pallas_call(
        body, name=name, out_shape=[jax.ShapeDtypeStruct(l.shape[1:], F32) for l in lands],
        compiler_params=pltpu.CompilerParams(vmem_limit_bytes=VMEM_LIMIT),
    )(*lands)


def _all_gather(big, small):
    def body(big_ref, small_ref, obig_ref, osmall_ref, send_sems, recv_sems, local_sems):
        x, y, c, _ = _my_place()
        me, sibling = (x, y, c), (x, y, 1 - c)
        chips = [(1 - x, y), (x, 1 - y), (1 - x, 1 - y)]
        parts = ((big_ref, obig_ref), (small_ref, osmall_ref))

        def slot(dst, place):
            return dst.at[4 * place[0] + 2 * place[1] + place[2]]

        def copy(part, k, block, to, first_hand=False):
            src, dst = parts[part]
            return pltpu.make_async_remote_copy(
                src_ref=src if first_hand else slot(dst, block), dst_ref=slot(dst, block),
                send_sem=send_sems.at[part, k], recv_sem=recv_sems.at[part, k], device_id=to, device_id_type=MESH)

        own = [pltpu.make_async_copy(src, slot(dst, me), local_sems.at[part]) for part, (src, dst) in enumerate(parts)]
        for cp in own:
            cp.start()
        first = []
        for part in range(len(parts)):
            first.append(copy(part, 0, me, sibling, True))
            first += [copy(part, 1 + j, me, (*chip, c), True) for j, chip in enumerate(chips)]
        for cp in first:
            cp.start()
        passed = []
        for j, chip in enumerate(chips):
            for part in range(len(parts)):
                copy(part, 1 + j, (*chip, c), me).wait_recv()
                passed.append(copy(part, 4 + j, (*chip, c), sibling))
                passed[-1].start()
        for part in range(len(parts)):
            copy(part, 0, sibling, me).wait_recv()
            for j, chip in enumerate(chips):
                copy(part, 4 + j, (*chip, 1 - c), me).wait_recv()
        for cp in first + passed:
            cp.wait_send()
        for cp in own:
            cp.wait()

    n = big.shape[0]
    hbm = pl.BlockSpec(memory_space=pl.ANY)
    return pl.pallas_call(
        body, name="weight_all_gather",
        in_specs=[hbm, hbm], out_specs=[hbm, hbm],
        out_shape=[jax.ShapeDtypeStruct((N_DEV,) + big.shape, BF16), jax.ShapeDtypeStruct((N_DEV,) + small.shape, F32)],
        scratch_shapes=[pltpu.SemaphoreType.DMA((2, N_DEV)), pltpu.SemaphoreType.DMA((2, N_DEV)),
                        pltpu.SemaphoreType.DMA((2,))],
        compiler_params=pltpu.CompilerParams(has_side_effects=True),
    )(big, small)


GRAD_CHUNK = 32


def _grad_exchange(gbig, rep):
    n, width = gbig.shape[1:]
    nrep = rep.shape[0]
    n_chips = N_DEV // 2

    def body(gbig_ref, rep_ref, out_ref, orep_ref, pre, stage, got, own_sum, land_rep, send_sems, recv_sems,
             local_sem):
        x, y, c, me = _my_place()
        my_chip = 2 * x + y
        sibling = (x, y, 1 - c)

        local = pltpu.make_async_copy(rep_ref, land_rep.at[me], local_sem.at[0])
        local.start()
        rep_copies = []
        for k in range(1, N_DEV):
            peer, _ = _peer(x, y, c, k)
            rep_copies.append(pltpu.make_async_remote_copy(
                src_ref=rep_ref, dst_ref=land_rep.at[me], send_sem=send_sems.at[6 + k], recv_sem=recv_sems.at[6 + k],
                device_id=peer, device_id_type=MESH))
        swaps = [pltpu.make_async_remote_copy(
            src_ref=gbig_ref.at[2 * q + (1 - c)], dst_ref=pre.at[q], send_sem=send_sems.at[q], recv_sem=recv_sems.at[q],
            device_id=sibling, device_id_type=MESH) for q in range(n_chips)]
        for cp in rep_copies + swaps:
            cp.start()
        for cp in swaps:
            cp.wait_recv()

        def pair_sums(ci, carry):
            rs = pl.ds(pl.multiple_of(ci * GRAD_CHUNK, GRAD_CHUNK), GRAD_CHUNK)
            for q in range(n_chips):
                stage[q, rs, :] = (gbig_ref[2 * q + c, rs, :].astype(F32) + pre[q, rs, :].astype(F32)).astype(BF16)
            own_sum[rs, :] = gbig_ref[me, rs, :].astype(F32) + pre[my_chip, rs, :].astype(F32)
            return carry

        lax.fori_loop(0, n // GRAD_CHUNK, pair_sums, 0)

        hops = []
        for rel in range(1, n_chips):
            qx, qy = x ^ (rel >> 1), y ^ (rel & 1)
            hops.append(pltpu.make_async_remote_copy(
                src_ref=stage.at[2 * qx + qy], dst_ref=got.at[my_chip], send_sem=send_sems.at[3 + rel],
                recv_sem=recv_sems.at[3 + rel], device_id=(qx, qy, c), device_id_type=MESH))
        for cp in hops:
            cp.start()
        for cp in hops:
            cp.wait_recv()

        def chip_sums(ci, carry):
            rs = pl.ds(pl.multiple_of(ci * GRAD_CHUNK, GRAD_CHUNK), GRAD_CHUNK)
            mine = own_sum[rs, :]
            acc = jnp.where(my_chip == 0, mine, got[0, rs, :].astype(F32))
            for q in range(1, n_chips):
                acc = acc + jnp.where(my_chip == q, mine, got[q, rs, :].astype(F32))
            out_ref[rs, :] = acc
            return carry

        lax.fori_loop(0, n // GRAD_CHUNK, chip_sums, 0)

        for cp in rep_copies:
            cp.wait_recv()
        local.wait()
        acc = land_rep[0]
        for d in range(1, N_DEV):
            acc = acc + land_rep[d]
        orep_ref[...] = acc
        for cp in swaps + hops + rep_copies:
            cp.wait_send()

    return pl.pallas_call(
        body, name="grad_exchange",
        in_specs=[pl.BlockSpec(memory_space=pltpu.VMEM), pl.BlockSpec(memory_space=pltpu.VMEM)],
        out_specs=[pl.BlockSpec(memory_space=pltpu.VMEM), pl.BlockSpec(memory_space=pltpu.VMEM)],
        out_shape=[jax.ShapeDtypeStruct((n, width), F32), jax.ShapeDtypeStruct((nrep, LANES), F32)],
        scratch_shapes=[pltpu.VMEM((n_chips, n, width), BF16), pltpu.VMEM((n_chips, n, width), BF16),
                        pltpu.VMEM((n_chips, n, width), BF16), pltpu.VMEM((n, width), F32),
                        pltpu.VMEM((N_DEV, nrep, LANES), F32),
                        pltpu.SemaphoreType.DMA((2 * N_DEV - 2,)), pltpu.SemaphoreType.DMA((2 * N_DEV - 2,)),
                        pltpu.SemaphoreType.DMA((1,))],
        compiler_params=pltpu.CompilerParams(vmem_limit_bytes=VMEM_LIMIT, has_side_effects=True),
    )(gbig, rep)


def _adamw_all(ws, gs, ms, vs):
    n = len(ws)

    def body(*refs):
        w_refs, g_refs, m_refs, v_refs = refs[0:n], refs[n:2 * n], refs[2 * n:3 * n], refs[3 * n:4 * n]
        d_refs, nm_refs, nv_refs = refs[4 * n:5 * n], refs[5 * n:6 * n], refs[6 * n:7 * n]
        for w_ref, g_ref, m_ref, v_ref, d_ref, nm_ref, nv_ref in zip(w_refs, g_refs, m_refs, v_refs, d_refs, nm_refs, nv_refs):
            g = g_ref[...]
            m = ADAM_B1 * m_ref[...] + (1.0 - ADAM_B1) * g
            v = ADAM_B2 * v_ref[...] + (1.0 - ADAM_B2) * jnp.square(g)
            m_hat = m / (1.0 - ADAM_B1 ** ADAM_STEP)
            v_hat = v / (1.0 - ADAM_B2 ** ADAM_STEP)
            d_ref[...] = -ADAM_LR * (m_hat / (jnp.sqrt(v_hat) + ADAM_EPS) + ADAM_WD * w_ref[...])
            nm_ref[...] = m
            nv_ref[...] = v

    shapes = [jax.ShapeDtypeStruct(w.shape, F32) for w in ws]
    outs = pl.pallas_call(
        body, name="adamw", out_shape=shapes * 3,
        compiler_params=pltpu.CompilerParams(vmem_limit_bytes=VMEM_LIMIT),
    )(*ws, *gs, *ms, *vs)
    return outs[0:n], outs[n:2 * n], outs[2 * n:3 * n]


SMALL_A = (("meta_tokens", 16),)
SMALL_B = (("b_norm_g", 1), ("b_conv_w", 4), ("b_conv_b", 1), ("b_b_rg", 1), ("b_b_ig", 1), ("b_lam", 1))
REP = (("a_norm_g", 8), ("a_q_norm_g", 3), ("a_kv_norm_g", 2), ("final_norm_g", 8), ("loss", 1))
SLOT = 16


def _offsets(table, slot=1, start=0):
    out, o = {}, start
    for name, n in table:
        out[name] = (o, n)
        o += -(-n // slot) * slot
    return out, o


def _slotted(a, axis):
    pad = -a.shape[axis] % SLOT
    if not pad:
        return a
    widths = [(0, 0)] * a.ndim
    widths[axis] = (0, pad)
    return jnp.pad(a, widths)


def _rope_tables(rows):
    pos = np.arange(rows, dtype=np.float32)
    inv_freq = (np.float32(ROPE_BASE) ** (-np.arange(0, QK_ROPE, 2, dtype=np.float32) / np.float32(QK_ROPE))).astype(
        np.float32)
    ang = pos[:, None] * inv_freq[None, :]
    cos, sin = np.cos(ang).astype(np.float32), np.sin(ang).astype(np.float32)
    zeros = np.zeros((rows, LANES - QK_ROPE), np.float32)
    return jnp.asarray(np.concatenate([cos, cos, zeros], axis=1)), jnp.asarray(np.concatenate([-sin, sin, zeros], axis=1))


def kernel(x, meta_tokens, a_norm_g, a_w_in, a_q_norm_g, a_kv_norm_g, a_w_uq, a_w_ukv, a_w_out, b_norm_g, b_w_in, b_conv_w, b_conv_b, b_w_rg, b_b_rg, b_w_ig, b_b_ig, b_lam, b_w_out, final_norm_g, loss_target, m_meta_tokens, m_a_norm_g, m_a_w_in, m_a_q_norm_g, m_a_kv_norm_g, m_a_w_uq, m_a_w_ukv, m_a_w_out, m_b_norm_g, m_b_w_in, m_b_conv_w, m_b_conv_b, m_b_w_rg, m_b_b_rg, m_b_w_ig, m_b_b_ig, m_b_lam, m_b_w_out, m_final_norm_g, v_meta_tokens, v_a_norm_g, v_a_w_in, v_a_q_norm_g, v_a_kv_norm_g, v_a_w_uq, v_a_w_ukv, v_a_w_out, v_b_norm_g, v_b_w_in, v_b_conv_w, v_b_conv_b, v_b_w_rg, v_b_b_rg, v_b_w_ig, v_b_b_ig, v_b_lam, v_b_w_out, v_final_norm_g):
    names = ("meta_tokens", "a_norm_g", "a_w_in", "a_q_norm_g", "a_kv_norm_g", "a_w_uq", "a_w_ukv", "a_w_out",
             "b_norm_g", "b_w_in", "b_conv_w", "b_conv_b", "b_w_rg", "b_b_rg", "b_w_ig", "b_b_ig", "b_lam", "b_w_out",
             "final_norm_g")
    w = dict(zip(names, (meta_tokens, a_norm_g, a_w_in, a_q_norm_g, a_kv_norm_g, a_w_uq, a_w_ukv, a_w_out, b_norm_g,
                         b_w_in, b_conv_w, b_conv_b, b_w_rg, b_b_rg, b_w_ig, b_b_ig, b_lam, b_w_out, final_norm_g)))
    mom_m = dict(zip(names, (m_meta_tokens, m_a_norm_g, m_a_w_in, m_a_q_norm_g, m_a_kv_norm_g, m_a_w_uq, m_a_w_ukv,
                             m_a_w_out, m_b_norm_g, m_b_w_in, m_b_conv_w, m_b_conv_b, m_b_w_rg, m_b_b_rg, m_b_w_ig,
                             m_b_b_ig, m_b_lam, m_b_w_out, m_final_norm_g)))
    mom_v = dict(zip(names, (v_meta_tokens, v_a_norm_g, v_a_w_in, v_a_q_norm_g, v_a_kv_norm_g, v_a_w_uq, v_a_w_ukv,
                             v_a_w_out, v_b_norm_g, v_b_w_in, v_b_conv_w, v_b_conv_b, v_b_w_rg, v_b_b_rg, v_b_w_ig,
                             v_b_b_ig, v_b_lam, v_b_w_out, v_final_norm_g)))

    seq = x.shape[1]
    n_real = N_META + seq
    rows = -(-n_real // LANES) * LANES
    scale = (QK_NOPE + QK_ROPE) ** -0.5
    small_off, _ = _offsets(SMALL_A + SMALL_B, SLOT)
    gsmallb_off, _ = _offsets(SMALL_B, SLOT)
    rep_off, _ = _offsets(REP, SLOT)
    cdev_a = a_w_in.shape[-1]
    wide = 2 * LANES

    send_a0 = jnp.pad(a_w_in[0], ((0, 0), (0, wide - cdev_a))).astype(BF16)
    send_small = jnp.concatenate([_slotted(w[nm].reshape(-1, LANES), 0) for nm, _ in SMALL_A + SMALL_B], axis=0)
    sends_a1 = [jnp.pad(a_w_uq[0], ((0, 0), (0, HEAD_PAD - QK_NOPE - QK_ROPE))).astype(BF16), a_w_ukv[0].astype(BF16)]
    lru_rows = LRU_BLOCKS * LRU_BLOCK // N_DEV
    sends_b = [a_w_out[0].astype(BF16), b_w_in[0].astype(BF16), b_w_rg.reshape(lru_rows, LRU_BLOCK).astype(BF16),
               b_w_ig.reshape(lru_rows, LRU_BLOCK).astype(BF16), b_w_out[0].astype(BF16)]
    all_a0, all_small = _all_gather(send_a0, send_small)

    def small_seg(nm):
        o, n = small_off[nm]
        return all_small[:, o:o + n, :]

    w_in_a = all_a0[:, :, :cdev_a].transpose(1, 0, 2).reshape(D_MODEL, N_DEV * cdev_a)
    w_in_a = jnp.concatenate([w_in_a[:, :LAT + QK_ROPE], jnp.zeros((D_MODEL, LAT_PAD - LAT - QK_ROPE), BF16),
                              w_in_a[:, LAT + QK_ROPE:]], axis=1)[None]
    meta_full = small_seg("meta_tokens").transpose(1, 0, 2).reshape(N_META, D_MODEL)
    vec = lambda nm: small_seg(nm).reshape(1, D_MODEL)
    g_b, conv_b, b_rg, b_ig, lam = vec("b_norm_g"), vec("b_conv_b"), vec("b_b_rg"), vec("b_b_ig"), vec("b_lam")
    conv_w = small_seg("b_conv_w").transpose(1, 0, 2).reshape(CONV_WIDTH, LRU_WIDTH)
    g_a, g_q, g_kv = a_norm_g, a_q_norm_g, a_kv_norm_g
    g_f = final_norm_g.reshape(1, D_MODEL)

    cos, sin = _rope_tables(rows)

    h0, lat, gate_a, w_uq, w_ukv = _embed_norm_proj_fwd(x[0], meta_full, rows, g_a, w_in_a, LAT_PAD, "a_in_fwd",
                                                        sends_a1)
    qc, kc, v, vt = _mla_qkv_fwd(lat, g_q, g_kv, w_uq, w_ukv, cos, sin, scale)
    o, lse, w_out_a, w_in_b, w_rg, w_ig, w_out_b = _attn_fwd(qc, kc, vt, sends_b)

    lru_w = lambda g: g.reshape(N_DEV, LRU_BLOCKS, LRU_BLOCK // N_DEV, LRU_BLOCK).transpose(1, 0, 2, 3).reshape(
        LRU_BLOCKS, LRU_BLOCK, LRU_BLOCK)
    w_out_a, w_out_b = w_out_a.reshape(D_MODEL, D_MODEL), w_out_b.reshape(D_MODEL, D_MODEL)
    w_rg, w_ig = lru_w(w_rg), lru_w(w_ig)

    h1, u, gate_b = _out_proj_in_proj(o, gate_a, h0, w_out_a, g_b, w_in_b, LRU_WIDTH, "a_out_b_in_fwd")
    hs = _rglru_fwd(u, conv_w, conv_b, w_rg, b_rg, w_ig, b_ig, lam)
    dh2, loss_part, dg_f, dhs, dgate_b, dw_out_b = _out_proj_loss(hs, gate_b, h1, w_out_b, g_f, loss_target[0],
                                                                   n_real)

    du, dconv_w, dconv_b, dw_rg, db_rg, dw_ig, db_ig, dlam = _rglru_bwd(u, hs, dhs, conv_w, conv_b, w_rg, b_rg, w_ig,
                                                                       b_ig, lam)
    dh1, dw_in_b, dg_b = _norm_proj_bwd(h1, g_b, w_in_b, du, dgate_b, dh2, "b_in_bwd")
    do, dgate_a, dw_out_a, delta = _attn_out_bwd(o, gate_a, dh1, w_out_a)

    def to_cols(g, cdev):
        r = g.shape[0]
        return g.reshape(r, N_DEV, cdev).transpose(1, 0, 2).reshape(N_DEV, -1, LANES)

    lru_g = lambda g: g.reshape(LRU_BLOCKS, N_DEV, LRU_BLOCK // N_DEV, LRU_BLOCK).transpose(1, 0, 2, 3).reshape(
        N_DEV, lru_rows, LRU_BLOCK)
    small_b = {"b_norm_g": dg_b, "b_conv_w": dconv_w, "b_conv_b": dconv_b, "b_b_rg": db_rg, "b_b_ig": db_ig,
               "b_lam": dlam}
    gsends_b = [dw_out_a.reshape(N_DEV, -1, D_MODEL), dw_in_b, lru_g(dw_rg).astype(BF16), lru_g(dw_ig).astype(BF16),
                dw_out_b.reshape(N_DEV, -1, D_MODEL),
                jnp.concatenate([_slotted(to_cols(small_b[nm], LANES).astype(BF16), 1) for nm, _ in SMALL_B], axis=1)]

    dqc, dkc, dv, *lands_b = _attn_bwd(qc, kc, v, lse, delta, do, gsends_b)
    g_out_a, g_in_b, g_rg, g_ig, g_out_b, gsum_small_b = _sum_blocks(lands_b, "sum_blocks_b")
    dlat, dw_uq, dw_ukv, dg_q, dg_kv = _mla_qkv_bwd(lat, g_q, g_kv, w_uq, w_ukv, cos, sin, dqc, dkc, dv, scale)
    dh0, dw_in_a, dg_a, *lands_a1 = _norm_proj_bwd(h0, g_a, w_in_a, dlat, dgate_a, dh1, "a_in_bwd", [dw_uq, dw_ukv])
    g_uq, g_ukv = _sum_blocks(lands_a1, "sum_blocks_a1")

    grad_x = dh0[N_META:n_real][None]

    dw_in_a_nat = jnp.concatenate([dw_in_a[0, :, :LAT + QK_ROPE], dw_in_a[0, :, LAT_PAD:]], axis=1)
    in_lanes = lambda g, cdev: jnp.pad(g.reshape(g.shape[0], N_DEV, cdev).transpose(1, 0, 2),
                                       ((0, 0), (0, 0), (0, wide - cdev)))
    pieces = [in_lanes(dw_in_a_nat, cdev_a), in_lanes(dh0[:N_META].astype(BF16), LANES)]
    used = sum(p.shape[1] for p in pieces)
    pieces.append(jnp.zeros((N_DEV, -used % GRAD_CHUNK, wide), BF16))
    gsend_a0 = jnp.concatenate(pieces, axis=1)
    rep_parts = {"a_norm_g": dg_a, "a_q_norm_g": dg_q, "a_kv_norm_g": dg_kv, "final_norm_g": dg_f,
                 "loss": jnp.broadcast_to(loss_part, (1, LANES))}
    rep = jnp.concatenate([_slotted(rep_parts[nm].reshape(-1, LANES), 0) for nm, _ in REP], axis=0)
    gsum_a0, rep_sum = _grad_exchange(gsend_a0, rep)

    grads = {"a_w_out": g_out_a, "b_w_in": g_in_b, "b_w_rg": g_rg, "b_w_ig": g_ig, "b_w_out": g_out_b,
             "a_w_uq": g_uq[:, :QK_NOPE + QK_ROPE], "a_w_ukv": g_ukv}
    grads = {nm: g.reshape(w[nm].shape) for nm, g in grads.items()}
    grads["a_w_in"] = gsum_a0[:D_MODEL, :cdev_a].reshape(w["a_w_in"].shape)
    grads["meta_tokens"] = gsum_a0[D_MODEL:D_MODEL + N_META, :LANES]
    for off, src in ((gsmallb_off, gsum_small_b), (rep_off, rep_sum)):
        for nm, (o_r, n) in off.items():
            if nm in w:
                grads[nm] = src[o_r:o_r + n].reshape(w[nm].shape)
    loss = rep_sum[rep_off["loss"][0], 0]

    as2d = lambda a: a.reshape(-1, a.shape[-1])
    deltas, new_ms, new_vs = _adamw_all([as2d(w[nm]) for nm in names], [as2d(grads[nm]) for nm in names],
                                        [as2d(mom_m[nm]) for nm in names], [as2d(mom_v[nm]) for nm in names])
    shaped = lambda arrs: [a.reshape(w[nm].shape) for a, nm in zip(arrs, names)]
    return (loss, grad_x, *[grads[nm] for nm in names], *shaped(deltas), *shaped(new_ms), *shaped(new_vs))
```

```python
import functools

import numpy as np
import jax
import jax.numpy as jnp
from jax import lax
from jax.experimental import pallas as pl
from jax.experimental.pallas import tpu as pltpu

F32 = jnp.float32
BF16 = jnp.bfloat16

D_MODEL = 1024
N_META = 16
RMS_EPS = 1e-6
HEADS = 8
QK_NOPE = 128
QK_ROPE = 64
V_HEAD = 128
Q_LORA = 384
KV_LORA = 256
HEAD_PAD = 256
LAT = Q_LORA + KV_LORA
LAT_PAD = LAT + 128
ROPE_BASE = 10000.0
MASK_VALUE = -1e30
LRU_WIDTH = 1024
LRU_BLOCKS = 4
LRU_BLOCK = 256
CONV_WIDTH = 4
LRU_C = 8.0
N_DEV = 8
ADAM_LR, ADAM_B1, ADAM_B2, ADAM_EPS, ADAM_WD, ADAM_STEP = 0.001, 0.9, 0.999, 1e-08, 0.01, 10

LANES = 128
SUBLANES = 8
VMEM_LIMIT = 56 * 1024 * 1024
MESH = pl.DeviceIdType.MESH

NT = (((1,), (1,)), ((), ()))
TN = (((0,), (0,)), ((), ()))


def _row_block(rows):
    return 384 if rows % 384 == 0 else 128


def _cparams(sem):
    return pltpu.CompilerParams(dimension_semantics=sem, vmem_limit_bytes=VMEM_LIMIT)


def _silu(x):
    return x * jax.nn.sigmoid(x)


def _dsilu(x):
    s = jax.nn.sigmoid(x)
    return s * (1.0 + x * (1.0 - s))


def _rms_fwd(x):
    r = lax.rsqrt(jnp.mean(x * x, axis=-1, keepdims=True) + RMS_EPS)
    return x * r, r


def _rms_bwd(dy, xn, r, g):
    t = dy * g
    dx = r * (t - xn * jnp.mean(t * xn, axis=-1, keepdims=True))
    return dx, jnp.sum(dy * xn, axis=0, keepdims=True)


def _expm1_neg(x):
    small = x * (1.0 + x * (1 / 2 + x * (1 / 6 + x * (1 / 24))))
    return jnp.where(x > -0.05, small, jnp.exp(x) - 1.0)


def _softplus_neg(lam):
    z = jnp.exp(-jnp.abs(lam))
    w = z / (2.0 + z)
    w2 = w * w
    series = 2.0 * w * (1.0 + w2 * (1 / 3) + w2 * w2 * (1 / 5))
    return jnp.maximum(-lam, 0.0) + jnp.where(z < 0.1, series, jnp.log(1.0 + z))


def _rider(sends, refs, first, last, all_to_all):
    nb = len(sends)
    if not nb:
        return (lambda: None), (lambda: None)
    send_refs, result_refs = refs[:nb], refs[nb:2 * nb]
    send_sems, recv_sems, local_sems = refs[2 * nb:]
    pick = (lambda ref: (lambda d: ref.at[d])) if all_to_all else (lambda ref: (lambda d: ref))

    def copies():
        out = []
        for b in range(nb):
            out += _exchange_copies(pick(send_refs[b]), result_refs[b], send_sems.at[b], recv_sems.at[b],
                                    local_sems.at[b])
        return out

    def start():
        @pl.when(first)
        def _():
            for cp in copies():
                cp.start()

    def wait():
        @pl.when(last)
        def _():
            for cp in copies():
                cp.wait()

    return start, wait


def _rider_specs(sends, all_to_all):
    nb = len(sends)
    if not nb:
        return [], [], [], []
    hbm = pl.BlockSpec(memory_space=pl.ANY)
    shapes = [jax.ShapeDtypeStruct(s.shape if all_to_all else (N_DEV,) + s.shape, s.dtype) for s in sends]
    return [hbm] * nb, [hbm] * nb, shapes, _exchange_sems(nb)


def _proj_blocks(x, w_ref):
    return jnp.concatenate([jnp.dot(x, w_ref[d], preferred_element_type=F32) for d in range(w_ref.shape[0])], axis=1)


def _embed_norm_proj_fwd(x, meta, rows, g, w, n1, name, wsends=()):
    n_real = N_META + x.shape[0]
    nb, _, cb = w.shape
    n = nb * cb
    tr = _row_block(rows)
    nsteps = rows // tr
    extra = len(wsends)

    def body(x_ref, meta_ref, g_ref, w_ref, *rest):
        h_ref, p1_ref, p2_ref = rest[extra:extra + 3]
        i = pl.program_id(0)
        start, wait = _rider(wsends, rest[:extra] + rest[extra + 3:], i == 0, i == nsteps - 1, False)
        start()
        xw = x_ref[...]
        xw = jnp.where(i == 0, pltpu.roll(xw, N_META, 0), xw)
        row = i * tr + lax.broadcasted_iota(jnp.int32, (tr, 1), 0)
        meta_rows = jnp.concatenate([meta_ref[...], jnp.zeros((tr - N_META, D_MODEL), F32)], axis=0)
        h = jnp.where(row < N_META, meta_rows, jnp.where(row < n_real, xw, 0.0))
        h_ref[...] = h
        xn, _ = _rms_fwd(h)
        p = _proj_blocks((xn * g_ref[...]).astype(BF16), w_ref)
        p1_ref[...] = p[:, :n1]
        p2_ref[...] = p[:, n1:]
        wait()

    r_in, r_out, r_shape, r_scratch = _rider_specs(wsends, False)
    window = pl.BlockSpec((pl.Element(tr, (0, rows - n_real)), pl.Element(D_MODEL)),
                          lambda i: (pl.multiple_of(jnp.maximum(i * tr - N_META, 0), SUBLANES), 0))
    return pl.pallas_call(
        body, name=name, grid=(nsteps,),
        in_specs=[window,
                  pl.BlockSpec((N_META, D_MODEL), lambda i: (0, 0)),
                  pl.BlockSpec((1, D_MODEL), lambda i: (0, 0)),
                  pl.BlockSpec((nb, D_MODEL, cb), lambda i: (0, 0, 0))] + r_in,
        out_specs=[pl.BlockSpec((tr, D_MODEL), lambda i: (i, 0)),
                   pl.BlockSpec((tr, n1), lambda i: (i, 0)),
                   pl.BlockSpec((tr, n - n1), lambda i: (i, 0))] + r_out,
        out_shape=[jax.ShapeDtypeStruct((rows, D_MODEL), F32), jax.ShapeDtypeStruct((rows, n1), F32),
                   jax.ShapeDtypeStruct((rows, n - n1), F32)] + r_shape,
        scratch_shapes=r_scratch,
        compiler_params=_cparams(("arbitrary",)),
    )(x, meta, g, w, *wsends)


def _out_proj_in_proj(a, gate, h, w_out, g, w_in, n1, name):
    rows = h.shape[0]
    nb, _, cb = w_in.shape
    n = nb * cb
    tr = _row_block(rows)

    def body(a_ref, gate_ref, h_ref, wo_ref, g_ref, wi_ref, hn_ref, p1_ref, p2_ref):
        y = (a_ref[...] * _silu(gate_ref[...])).astype(BF16)
        h_new = h_ref[...] + jnp.dot(y, wo_ref[...], preferred_element_type=F32)
        hn_ref[...] = h_new
        xn, _ = _rms_fwd(h_new)
        p = _proj_blocks((xn * g_ref[...]).astype(BF16), wi_ref)
        p1_ref[...] = p[:, :n1]
        p2_ref[...] = p[:, n1:]

    blk = pl.BlockSpec((tr, D_MODEL), lambda i: (i, 0))
    return pl.pallas_call(
        body, name=name, grid=(rows // tr,),
        in_specs=[blk, blk, blk, pl.BlockSpec((D_MODEL, D_MODEL), lambda i: (0, 0)),
                  pl.BlockSpec((1, D_MODEL), lambda i: (0, 0)), pl.BlockSpec((nb, D_MODEL, cb), lambda i: (0, 0, 0))],
        out_specs=[blk, pl.BlockSpec((tr, n1), lambda i: (i, 0)), pl.BlockSpec((tr, n - n1), lambda i: (i, 0))],
        out_shape=[jax.ShapeDtypeStruct((rows, D_MODEL), F32), jax.ShapeDtypeStruct((rows, n1), F32),
                   jax.ShapeDtypeStruct((rows, n - n1), F32)],
        compiler_params=_cparams(("parallel",)),
    )(a, gate, h, w_out, g, w_in)


def _norm_proj_bwd(h, g, w, dp1, dp2, dh_in, name, gsends=()):
    rows = h.shape[0]
    nb, _, cb = w.shape
    n1 = dp1.shape[1]
    n2 = nb * cb - n1
    tr = _row_block(rows)
    nsteps = rows // tr
    extra = len(gsends)

    def body(h_ref, g_ref, w_ref, dp1_ref, dp2_ref, dhin_ref, *rest):
        dh_ref, dw_ref, dg_ref = rest[extra:extra + 3]
        dw_acc = rest[2 * extra + 3]
        i = pl.program_id(0)
        start, wait = _rider(gsends, rest[:extra] + rest[extra + 3:2 * extra + 3] + rest[2 * extra + 4:],
                             i == 0, i == nsteps - 1, True)
        start()

        @pl.when(i == 0)
        def _():
            dw_acc[...] = jnp.zeros_like(dw_acc)
            dg_ref[...] = jnp.zeros_like(dg_ref)

        gv = g_ref[...]
        xn, r = _rms_fwd(h_ref[...])
        hn = (xn * gv).astype(BF16)
        dp = jnp.concatenate([dp1_ref[...].astype(BF16), dp2_ref[...].astype(BF16)], axis=1)
        dhn = jnp.zeros((tr, D_MODEL), F32)
        for d in range(nb):
            dpd = dp[:, d * cb:(d + 1) * cb]
            dw_acc[d] += lax.dot_general(hn, dpd, TN, preferred_element_type=F32)
            dhn = dhn + lax.dot_general(dpd, w_ref[d], NT, preferred_element_type=F32)
        dx, dg = _rms_bwd(dhn, xn, r, gv)
        dg_ref[...] += dg
        dh_ref[...] = dhin_ref[...] + dx

        @pl.when(i == nsteps - 1)
        def _():
            dw_ref[...] = dw_acc[...].astype(BF16)

        wait()

    r_in, r_out, r_shape, r_scratch = _rider_specs(gsends, True)
    wblk = pl.BlockSpec((nb, D_MODEL, cb), lambda i: (0, 0, 0))
    return pl.pallas_call(
        body, name=name, grid=(nsteps,),
        in_specs=[pl.BlockSpec((tr, D_MODEL), lambda i: (i, 0)),
                  pl.BlockSpec((1, D_MODEL), lambda i: (0, 0)),
                  wblk,
                  pl.BlockSpec((tr, n1), lambda i: (i, 0)),
                  pl.BlockSpec((tr, n2), lambda i: (i, 0)),
                  pl.BlockSpec((tr, D_MODEL), lambda i: (i, 0))] + r_in,
        out_specs=[pl.BlockSpec((tr, D_MODEL), lambda i: (i, 0)), wblk,
                   pl.BlockSpec((1, D_MODEL), lambda i: (0, 0))] + r_out,
        out_shape=[jax.ShapeDtypeStruct((rows, D_MODEL), F32),
                   jax.ShapeDtypeStruct((nb, D_MODEL, cb), BF16),
                   jax.ShapeDtypeStruct((1, D_MODEL), F32)] + r_shape,
        scratch_shapes=[pltpu.VMEM((nb, D_MODEL, cb), F32)] + r_scratch,
        compiler_params=_cparams(("arbitrary",)),
    )(h, g, w, dp1, dp2, dh_in, *gsends)


def _attn_out_bwd(o, gate, dh, w):
    rows = o.shape[0]
    tr = _row_block(rows)
    nsteps = rows // tr

    def body(o_ref, gate_ref, dh_ref, w_ref, do_ref, dgate_ref, dw_ref, delta_ref, dw_acc):
        i = pl.program_id(0)

        @pl.when(i == 0)
        def _():
            dw_acc[...] = jnp.zeros_like(dw_acc)

        ov, gv = o_ref[...], gate_ref[...]
        sg = _silu(gv)
        dhb = dh_ref[...].astype(BF16)
        dw_acc[...] += lax.dot_general((ov * sg).astype(BF16), dhb, TN, preferred_element_type=F32)
        dy = lax.dot_general(dhb, w_ref[...], NT, preferred_element_type=F32)
        do = (dy * sg).astype(BF16)
        do_ref[...] = do
        dgate_ref[...] = (dy * ov * _dsilu(gv)).astype(BF16)
        prod = do.astype(F32) * ov
        lane = lax.broadcasted_iota(jnp.int32, (tr, LANES), 1)
        per_head = jnp.zeros((tr, LANES), F32)
        for hd in range(HEADS):
            dsum = jnp.sum(prod[:, hd * V_HEAD:(hd + 1) * V_HEAD], axis=1, keepdims=True)
            per_head = jnp.where(lane == hd, dsum, per_head)
        delta_t = per_head.T
        for hd in range(HEADS):
            delta_ref[hd, 0] = delta_t[hd:hd + 1, :]

        @pl.when(i == nsteps - 1)
        def _():
            dw_ref[...] = dw_acc[...].astype(BF16)

    blk = pl.BlockSpec((tr, D_MODEL), lambda i: (i, 0))
    wblk = pl.BlockSpec((D_MODEL, D_MODEL), lambda i: (0, 0))
    return pl.pallas_call(
        body, name="a_out_bwd", grid=(nsteps,),
        in_specs=[blk, blk, blk, wblk],
        out_specs=[blk, blk, wblk, pl.BlockSpec((HEADS, 1, 1, tr), lambda i: (0, i, 0, 0))],
        out_shape=[jax.ShapeDtypeStruct((rows, D_MODEL), BF16), jax.ShapeDtypeStruct((rows, D_MODEL), BF16),
                   jax.ShapeDtypeStruct((D_MODEL, D_MODEL), BF16),
                   jax.ShapeDtypeStruct((HEADS, nsteps, 1, tr), F32)],
        scratch_shapes=[pltpu.VMEM((D_MODEL, D_MODEL), F32)],
        compiler_params=_cparams(("arbitrary",)),
    )(o, gate, dh, w)


def _rope(v, cos, sin, lane):
    swapped = jnp.where(lane < QK_ROPE // 2, pltpu.roll(v, LANES - QK_ROPE // 2, 1), pltpu.roll(v, QK_ROPE // 2, 1))
    return v * cos + swapped * sin


def _unrope(dv, cos, sin, lane):
    t = dv * sin
    swapped = jnp.where(lane < QK_ROPE // 2, pltpu.roll(t, LANES - QK_ROPE // 2, 1), pltpu.roll(t, QK_ROPE // 2, 1))
    return dv * cos + swapped


def _mla_qkv_fwd(lat, gq, gkv, wuq, wukv, cos, sin, scale):
    rows = lat.shape[0]
    tr = _row_block(rows)

    def body(lat_ref, gq_ref, gkv_ref, wuq_ref, wukv_ref, cos_ref, sin_ref, qc_ref, kc_ref, v_ref, vt_ref):
        qn, _ = _rms_fwd(lat_ref[:, :Q_LORA])
        kvn, _ = _rms_fwd(lat_ref[:, Q_LORA:LAT])
        qnb = (qn * gq_ref[...]).astype(BF16)
        kvnb = (kvn * gkv_ref[...]).astype(BF16)
        c, s = cos_ref[...], sin_ref[...]
        lane = lax.broadcasted_iota(jnp.int32, (tr, LANES), 1)
        kr = _rope(lat_ref[:, LAT:LAT_PAD], c, s, lane).astype(BF16)
        for hd in range(HEADS):
            o = hd * HEAD_PAD
            q = jnp.dot(qnb, wuq_ref[hd], preferred_element_type=F32)
            kv = jnp.dot(kvnb, wukv_ref[hd], preferred_element_type=F32)
            qc_ref[:, o:o + QK_NOPE] = (q[:, :QK_NOPE] * scale).astype(BF16)
            qc_ref[:, o + QK_NOPE:o + HEAD_PAD] = (_rope(q[:, QK_NOPE:], c, s, lane) * scale).astype(BF16)
            kc_ref[:, o:o + QK_NOPE] = kv[:, :QK_NOPE].astype(BF16)
            kc_ref[:, o + QK_NOPE:o + HEAD_PAD] = kr
            vh = kv[:, QK_NOPE:]
            v_ref[:, hd * V_HEAD:(hd + 1) * V_HEAD] = vh.astype(BF16)
            vt_ref[hd, 0] = vh.T.astype(BF16)

    full = lambda shape: pl.BlockSpec(shape, lambda i: (0,) * len(shape))
    rowb = lambda n: pl.BlockSpec((tr, n), lambda i: (i, 0))
    return pl.pallas_call(
        body, name="mla_qkv_fwd", grid=(rows // tr,),
        in_specs=[rowb(LAT_PAD), full((1, Q_LORA)), full((1, KV_LORA)), full((HEADS, Q_LORA, HEAD_PAD)),
                  full((HEADS, KV_LORA, HEAD_PAD)), rowb(LANES), rowb(LANES)],
        out_specs=[rowb(HEADS * HEAD_PAD), rowb(HEADS * HEAD_PAD), rowb(HEADS * V_HEAD),
                   pl.BlockSpec((HEADS, 1, V_HEAD, tr), lambda i: (0, i, 0, 0))],
        out_shape=[jax.ShapeDtypeStruct((rows, HEADS * HEAD_PAD), BF16),
                   jax.ShapeDtypeStruct((rows, HEADS * HEAD_PAD), BF16),
                   jax.ShapeDtypeStruct((rows, HEADS * V_HEAD), BF16),
                   jax.ShapeDtypeStruct((HEADS, rows // tr, V_HEAD, tr), BF16)],
        compiler_params=_cparams(("parallel",)),
    )(lat, gq, gkv, wuq, wukv, cos, sin)


def _mla_qkv_bwd(lat, gq, gkv, wuq, wukv, cos, sin, dqc, dkc, dv, scale):
    rows = lat.shape[0]
    tr = _row_block(rows)
    nsteps = rows // tr

    def body(lat_ref, gq_ref, gkv_ref, wuq_ref, wukv_ref, cos_ref, sin_ref, dqc_ref, dkc_ref, dv_ref,
             dlat_ref, dwuq_out, dwukv_out, dgq_ref, dgkv_ref, dwuq_ref, dwukv_ref):
        @pl.when(pl.program_id(0) == 0)
        def _():
            dwuq_ref[...] = jnp.zeros_like(dwuq_ref)
            dwukv_ref[...] = jnp.zeros_like(dwukv_ref)
            dgq_ref[...] = jnp.zeros_like(dgq_ref)
            dgkv_ref[...] = jnp.zeros_like(dgkv_ref)

        c, s = cos_ref[...], sin_ref[...]
        lane = lax.broadcasted_iota(jnp.int32, (tr, LANES), 1)
        gqv, gkvv = gq_ref[...], gkv_ref[...]
        qn, rq = _rms_fwd(lat_ref[:, :Q_LORA])
        kvn, rkv = _rms_fwd(lat_ref[:, Q_LORA:LAT])
        qnb = (qn * gqv).astype(BF16)
        kvnb = (kvn * gkvv).astype(BF16)
        dkr = jnp.zeros((tr, LANES), F32)
        dqn = jnp.zeros((tr, Q_LORA), F32)
        dkvn = jnp.zeros((tr, KV_LORA), F32)
        for hd in range(HEADS):
            o = hd * HEAD_PAD
            dq = jnp.concatenate(
                [dqc_ref[:, o:o + QK_NOPE],
                 _unrope(dqc_ref[:, o + QK_NOPE:o + HEAD_PAD].astype(F32), c, s, lane).astype(BF16)], axis=1)
            dkv = jnp.concatenate([dkc_ref[:, o:o + QK_NOPE], dv_ref[:, hd * V_HEAD:(hd + 1) * V_HEAD]], axis=1)
            dkr = dkr + dkc_ref[:, o + QK_NOPE:o + HEAD_PAD].astype(F32)
            dwuq_ref[hd] += scale * lax.dot_general(qnb, dq, TN, preferred_element_type=F32)
            dwukv_ref[hd] += lax.dot_general(kvnb, dkv, TN, preferred_element_type=F32)
            dqn = dqn + lax.dot_general(dq, wuq_ref[hd], NT, preferred_element_type=F32)
            dkvn = dkvn + lax.dot_general(dkv, wukv_ref[hd], NT, preferred_element_type=F32)
        dqn = scale * dqn
        dqlat, dgq = _rms_bwd(dqn, qn, rq, gqv)
        dkvlat, dgkv = _rms_bwd(dkvn, kvn, rkv, gkvv)
        dgq_ref[...] += dgq
        dgkv_ref[...] += dgkv
        dlat_ref[:, :Q_LORA] = dqlat.astype(BF16)
        dlat_ref[:, Q_LORA:LAT] = dkvlat.astype(BF16)
        dlat_ref[:, LAT:LAT_PAD] = _unrope(dkr, c, s, lane).astype(BF16)

        @pl.when(pl.program_id(0) == nsteps - 1)
        def _():
            dwuq_out[...] = dwuq_ref[...].astype(BF16)
            dwukv_out[...] = dwukv_ref[...].astype(BF16)

    full = lambda shape: pl.BlockSpec(shape, lambda i: (0,) * len(shape))
    rowb = lambda n: pl.BlockSpec((tr, n), lambda i: (i, 0))
    return pl.pallas_call(
        body, name="mla_qkv_bwd", grid=(nsteps,),
        in_specs=[rowb(LAT_PAD), full((1, Q_LORA)), full((1, KV_LORA)), full((HEADS, Q_LORA, HEAD_PAD)),
                  full((HEADS, KV_LORA, HEAD_PAD)), rowb(LANES), rowb(LANES),
                  rowb(HEADS * HEAD_PAD), rowb(HEADS * HEAD_PAD), rowb(HEADS * V_HEAD)],
        out_specs=[rowb(LAT_PAD), full((HEADS, Q_LORA, HEAD_PAD)), full((HEADS, KV_LORA, HEAD_PAD)),
                   full((1, Q_LORA)), full((1, KV_LORA))],
        out_shape=[jax.ShapeDtypeStruct((rows, LAT_PAD), BF16),
                   jax.ShapeDtypeStruct((HEADS, Q_LORA, HEAD_PAD), BF16),
                   jax.ShapeDtypeStruct((HEADS, KV_LORA, HEAD_PAD), BF16),
                   jax.ShapeDtypeStruct((1, Q_LORA), F32),
                   jax.ShapeDtypeStruct((1, KV_LORA), F32)],
        scratch_shapes=[pltpu.VMEM((HEADS, Q_LORA, HEAD_PAD), F32), pltpu.VMEM((HEADS, KV_LORA, HEAD_PAD), F32)],
        compiler_params=_cparams(("arbitrary",)),
    )(lat, gq, gkv, wuq, wukv, cos, sin, dqc, dkc, dv)


ATTN_UNROLL = 8
ATTN_UNROLL_BWD = 4
ATTN_HEADS = 2


def _causal_mask_t(t):
    key = lax.broadcasted_iota(jnp.int32, (t, t), 0)
    query = lax.broadcasted_iota(jnp.int32, (t, t), 1)
    return key <= query


def _attn_fwd(qc, kc, vt, wsends):
    rows = qc.shape[0]
    t = _row_block(rows)
    nblk = rows // t
    nw = len(wsends)

    def body(q_ref, k_ref, vt_ref, *rest):
        o_ref, lse_ref = rest[nw:nw + 2]
        m_ref, l_ref, acc_ref, st_a, st_b = rest[2 * nw + 2:2 * nw + 7]
        i = pl.program_id(1)
        start, wait = _rider(wsends, rest[:nw] + rest[nw + 2:2 * nw + 2] + rest[2 * nw + 7:],
                             jnp.logical_and(pl.program_id(0) == 0, i == 0),
                             jnp.logical_and(pl.program_id(0) == HEADS // ATTN_HEADS - 1, i == nblk - 1), False)
        start()

        m_ref[...] = jnp.full_like(m_ref, MASK_VALUE)
        l_ref[...] = jnp.zeros_like(l_ref)
        acc_ref[...] = jnp.zeros_like(acc_ref)
        heads = range(ATTN_HEADS)
        qs = [q_ref[:, hh * HEAD_PAD:(hh + 1) * HEAD_PAD] for hh in heads]

        def scores(j, hh, st_ref):
            rs = pl.ds(pl.multiple_of(j * t, t), t)
            st_ref[hh] = lax.dot_general(k_ref[rs, hh * HEAD_PAD:(hh + 1) * HEAD_PAD], qs[hh], NT,
                                         preferred_element_type=F32)

        def consume(j, hh, st_ref, masked):
            st = st_ref[hh]
            if masked:
                st = jnp.where(_causal_mask_t(t), st, MASK_VALUE)
            m_prev = m_ref[hh]
            m_new = jnp.maximum(m_prev, jnp.max(st, axis=0, keepdims=True))
            alpha = jnp.exp(m_prev - m_new)
            pt = jnp.exp(st - m_new)
            l_ref[hh] = alpha * l_ref[hh] + jnp.sum(pt, axis=0, keepdims=True)
            acc_ref[hh] = alpha * acc_ref[hh] + jnp.dot(vt_ref[hh, j], pt.astype(BF16), preferred_element_type=F32)
            m_ref[hh] = m_new

        bufs = (st_a, st_b)

        def step(j, parity, issue_next, masked):
            if issue_next:
                for hh in heads:
                    scores(j + 1, hh, bufs[1 - parity])
            for hh in heads:
                consume(j, hh, bufs[parity], masked)

        for hh in heads:
            scores(0, hh, st_a)

        def trip(it, carry):
            for u in range(ATTN_UNROLL):
                step(it * ATTN_UNROLL + u, u % 2, True, False)
            return carry

        trips = i // ATTN_UNROLL
        lax.fori_loop(0, trips, trip, 0)
        j0 = trips * ATTN_UNROLL
        for left in range(1, ATTN_UNROLL + 1):
            @pl.when(i + 1 - j0 == left)
            def _(left=left):
                for u in range(left):
                    step(j0 + u, u % 2, u < left - 1, u == left - 1)

        for hh in heads:
            o_ref[:, hh * V_HEAD:(hh + 1) * V_HEAD] = (acc_ref[hh] / l_ref[hh]).T
            lse_ref[hh, 0] = m_ref[hh] + jnp.log(l_ref[hh])
        wait()

    r_in, r_out, r_shape, r_scratch = _rider_specs(wsends, False)
    nh = ATTN_HEADS
    return pl.pallas_call(
        body, name="attn_fwd", grid=(HEADS // nh, nblk),
        in_specs=[pl.BlockSpec((t, nh * HEAD_PAD), lambda g, i: (i, g)),
                  pl.BlockSpec((rows, nh * HEAD_PAD), lambda g, i: (0, g)),
                  pl.BlockSpec((nh, nblk, V_HEAD, t), lambda g, i: (g, 0, 0, 0))] + r_in,
        out_specs=[pl.BlockSpec((t, nh * V_HEAD), lambda g, i: (i, g)),
                   pl.BlockSpec((nh, 1, 1, t), lambda g, i: (g, i, 0, 0))] + r_out,
        out_shape=[jax.ShapeDtypeStruct((rows, HEADS * V_HEAD), F32),
                   jax.ShapeDtypeStruct((HEADS, nblk, 1, t), F32)] + r_shape,
        scratch_shapes=[pltpu.VMEM((nh, 1, t), F32), pltpu.VMEM((nh, 1, t), F32), pltpu.VMEM((nh, V_HEAD, t), F32),
                        pltpu.VMEM((nh, t, t), F32), pltpu.VMEM((nh, t, t), F32)] + r_scratch,
        compiler_params=_cparams(("arbitrary", "arbitrary")),
    )(qc, kc, vt, *wsends)


def _attn_bwd(qc, kc, v, lse, delta, do, gsends):
    rows = qc.shape[0]
    t = _row_block(rows)
    nblk = rows // t
    ng = len(gsends)

    def body(q_ref, k_ref, v_ref, lse_ref, delta_ref, do_ref, *rest):
        dq_ref, dk_ref, dv_ref = rest[ng:ng + 3]
        dq_acc, dk_acc, dv_acc, st_a, dp_a, st_b, dp_b = rest[2 * ng + 3:2 * ng + 10]
        j = pl.program_id(1)
        start, wait = _rider(gsends, rest[:ng] + rest[ng + 3:2 * ng + 3] + rest[2 * ng + 10:],
                             jnp.logical_and(pl.program_id(0) == 0, j == 0),
                             jnp.logical_and(pl.program_id(0) == HEADS // ATTN_HEADS - 1, j == nblk - 1), True)
        start()

        @pl.when(j == 0)
        def _():
            dq_acc[...] = jnp.zeros_like(dq_acc)

        dk_acc[...] = jnp.zeros_like(dk_acc)
        dv_acc[...] = jnp.zeros_like(dv_acc)
        heads = range(ATTN_HEADS)
        qk = lambda hh: slice(hh * HEAD_PAD, (hh + 1) * HEAD_PAD)
        vo = lambda hh: slice(hh * V_HEAD, (hh + 1) * V_HEAD)
        ks = [k_ref[:, qk(hh)] for hh in heads]
        vs = [v_ref[:, vo(hh)] for hh in heads]

        def products(i, hh, st_ref, dp_ref):
            rs = pl.ds(pl.multiple_of(i * t, t), t)
            st_ref[hh] = lax.dot_general(ks[hh], q_ref[rs, qk(hh)], NT, preferred_element_type=F32)
            dp_ref[hh] = lax.dot_general(vs[hh], do_ref[rs, vo(hh)], NT, preferred_element_type=F32)

        def consume(i, hh, st_ref, dp_ref):
            rs = pl.ds(pl.multiple_of(i * t, t), t)
            q = q_ref[rs, qk(hh)]
            dob = do_ref[rs, vo(hh)]
            st = jnp.where(jnp.logical_or(_causal_mask_t(t), i != j), st_ref[hh], MASK_VALUE)
            pt = jnp.exp(st - lse_ref[hh, i])
            dv_acc[hh] += jnp.dot(pt.astype(BF16), dob, preferred_element_type=F32)
            dst = (pt * (dp_ref[hh] - delta_ref[hh, i])).astype(BF16)
            dk_acc[hh] += jnp.dot(dst, q, preferred_element_type=F32)
            dq_acc[hh, rs, :] += lax.dot_general(dst, ks[hh], TN, preferred_element_type=F32)

        bufs = ((st_a, dp_a), (st_b, dp_b))

        def step(i, parity, issue_next):
            if issue_next:
                for hh in heads:
                    products(i + 1, hh, *bufs[1 - parity])
            for hh in heads:
                consume(i, hh, *bufs[parity])

        for hh in heads:
            products(j, hh, st_a, dp_a)

        def trip(it, carry):
            for u in range(ATTN_UNROLL_BWD):
                step(j + it * ATTN_UNROLL_BWD + u, u % 2, True)
            return carry

        trips = (nblk - 1 - j) // ATTN_UNROLL_BWD
        lax.fori_loop(0, trips, trip, 0)
        i0 = j + trips * ATTN_UNROLL_BWD
        for left in range(1, ATTN_UNROLL_BWD + 1):
            @pl.when(nblk - i0 == left)
            def _(left=left):
                for u in range(left):
                    step(i0 + u, u % 2, u < left - 1)

        for hh in heads:
            dk_ref[:, qk(hh)] = dk_acc[hh].astype(BF16)
            dv_ref[:, vo(hh)] = dv_acc[hh].astype(BF16)

        @pl.when(j == nblk - 1)
        def _():
            for hh in heads:
                dq_ref[:, qk(hh)] = dq_acc[hh].astype(BF16)

        wait()

    nh = ATTN_HEADS
    stat = pl.BlockSpec((nh, nblk, 1, t), lambda g, j: (g, 0, 0, 0))
    r_in, r_out, r_shape, r_scratch = _rider_specs(gsends, True)
    return pl.pallas_call(
        body, name="attn_bwd", grid=(HEADS // nh, nblk),
        in_specs=[pl.BlockSpec((rows, nh * HEAD_PAD), lambda g, j: (0, g)),
                  pl.BlockSpec((t, nh * HEAD_PAD), lambda g, j: (j, g)),
                  pl.BlockSpec((t, nh * V_HEAD), lambda g, j: (j, g)),
                  stat, stat,
                  pl.BlockSpec((rows, nh * V_HEAD), lambda g, j: (0, g))] + r_in,
        out_specs=[pl.BlockSpec((rows, nh * HEAD_PAD), lambda g, j: (0, g)),
                   pl.BlockSpec((t, nh * HEAD_PAD), lambda g, j: (j, g)),
                   pl.BlockSpec((t, nh * V_HEAD), lambda g, j: (j, g))] + r_out,
        out_shape=[jax.ShapeDtypeStruct((rows, HEADS * HEAD_PAD), BF16),
                   jax.ShapeDtypeStruct((rows, HEADS * HEAD_PAD), BF16),
                   jax.ShapeDtypeStruct((rows, HEADS * V_HEAD), BF16)] + r_shape,
        scratch_shapes=[pltpu.VMEM((nh, rows, HEAD_PAD), F32), pltpu.VMEM((nh, t, HEAD_PAD), F32),
                        pltpu.VMEM((nh, t, V_HEAD), F32)] + [pltpu.VMEM((nh, t, t), F32)] * 4 + r_scratch,
        compiler_params=_cparams(("arbitrary", "arbitrary")),
    )(qc, kc, v, lse, delta, do, *gsends)


def _shift_down(prev_tile, x, k):
    xx = jnp.concatenate([prev_tile, x], axis=0)
    return pltpu.roll(xx, k, 0)[SUBLANES:]


def _shift_up(x, next_tile, k):
    n = x.shape[0]
    xx = jnp.concatenate([x, next_tile], axis=0)
    return pltpu.roll(xx, n + SUBLANES - k, 0)[:n]


def _lru_gates(u, u_prev, cw_ref, cb_ref, wrg_ref, brg_ref, wig_ref, big_ref, lam_ref, first_block):
    taps = [_shift_down(u_prev, u, CONV_WIDTH - 1 - j) if j < CONV_WIDTH - 1 else u for j in range(CONV_WIDTH)]
    uc = cb_ref[...] + taps[0] * cw_ref[0:1, :]
    for j in range(1, CONV_WIDTH):
        uc = uc + taps[j] * cw_ref[j:j + 1, :]
    ub = uc.astype(BF16)
    zr = jnp.concatenate([jnp.dot(ub[:, g * LRU_BLOCK:(g + 1) * LRU_BLOCK], wrg_ref[g], preferred_element_type=F32)
                          for g in range(LRU_BLOCKS)], axis=1) + brg_ref[...]
    zi = jnp.concatenate([jnp.dot(ub[:, g * LRU_BLOCK:(g + 1) * LRU_BLOCK], wig_ref[g], preferred_element_type=F32)
                          for g in range(LRU_BLOCKS)], axis=1) + big_ref[...]
    r = jax.nn.sigmoid(zr)
    ig = jax.nn.sigmoid(zi)
    sp = _softplus_neg(lam_ref[...])
    log_a = (-LRU_C) * r * sp
    a = jnp.exp(log_a)
    m2 = -_expm1_neg(2.0 * log_a)
    mult_raw = m2 * lax.rsqrt(jnp.maximum(m2, 1e-30))
    row = lax.broadcasted_iota(jnp.int32, u.shape, 0)
    is_start = jnp.logical_and(first_block, row == 0)
    mult = jnp.where(is_start, 1.0, mult_raw)
    return dict(taps=taps, uc=uc, ub=ub, r=r, ig=ig, sp=sp, a=a, mult=mult, mult_raw=mult_raw, is_start=is_start)


def _rglru_fwd(u, cw, cb, wrg, brg, wig, big, lam):
    rows = u.shape[0]
    tb = _row_block(rows)

    def body(u_ref, cw_ref, cb_ref, wrg_ref, brg_ref, wig_ref, big_ref, lam_ref, hs_ref, utail, hcar, a_s, b_s):
        i = pl.program_id(0)

        @pl.when(i == 0)
        def _():
            utail[...] = jnp.zeros_like(utail)
            hcar[...] = jnp.zeros_like(hcar)

        u = u_ref[...]
        gt = _lru_gates(u, utail[...], cw_ref, cb_ref, wrg_ref, brg_ref, wig_ref, big_ref, lam_ref, i == 0)
        a_s[...] = gt["a"]
        b_s[...] = gt["mult"] * (gt["ig"] * gt["uc"])
        row8 = lax.broadcasted_iota(jnp.int32, (SUBLANES, LRU_WIDTH), 0)

        def tile(tix, carry):
            rs = pl.ds(pl.multiple_of(tix * SUBLANES, SUBLANES), SUBLANES)
            av, bv = a_s[rs, :], b_s[rs, :]
            for k in (1, 2, 4):
                keep = row8 >= k
                bv = jnp.where(keep, av * pltpu.roll(bv, k, 0) + bv, bv)
                av = jnp.where(keep, av * pltpu.roll(av, k, 0), av)
            h8 = av * carry + bv
            hs_ref[rs, :] = h8
            return jnp.broadcast_to(h8[SUBLANES - 1:SUBLANES, :], (SUBLANES, LRU_WIDTH))

        hcar[...] = lax.fori_loop(0, tb // SUBLANES, tile, hcar[...])
        utail[...] = u[tb - SUBLANES:, :]

    full2 = lambda shape: pl.BlockSpec(shape, lambda i: (0, 0))
    full3 = lambda shape: pl.BlockSpec(shape, lambda i: (0, 0, 0))
    blk = pl.BlockSpec((tb, LRU_WIDTH), lambda i: (i, 0))
    return pl.pallas_call(
        body, name="rglru_fwd", grid=(rows // tb,),
        in_specs=[blk, full2((CONV_WIDTH, LRU_WIDTH)), full2((1, LRU_WIDTH)),
                  full3((LRU_BLOCKS, LRU_BLOCK, LRU_BLOCK)), full2((1, LRU_WIDTH)),
                  full3((LRU_BLOCKS, LRU_BLOCK, LRU_BLOCK)), full2((1, LRU_WIDTH)), full2((1, LRU_WIDTH))],
        out_specs=blk,
        out_shape=jax.ShapeDtypeStruct((rows, LRU_WIDTH), F32),
        scratch_shapes=[pltpu.VMEM((SUBLANES, LRU_WIDTH), F32), pltpu.VMEM((SUBLANES, LRU_WIDTH), F32),
                        pltpu.VMEM((tb, LRU_WIDTH), F32), pltpu.VMEM((tb, LRU_WIDTH), F32)],
        compiler_params=_cparams(("arbitrary",)),
    )(u, cw, cb, wrg, brg, wig, big, lam)


def _rglru_bwd(u, hs, dhs, cw, cb, wrg, brg, wig, big, lam):
    rows = u.shape[0]
    tb = _row_block(rows)
    nblk = rows // tb
    tiles = tb // SUBLANES

    def body(u_ref, up_ref, hs_ref, hp_ref, dhs_ref, cw_ref, cb_ref, wrg_ref, brg_ref, wig_ref, big_ref, lam_ref,
             du_ref, dcw_ref, dcb_ref, dwrg_ref, dbrg_ref, dwig_ref, dbig_ref, dlam_ref,
             gcar, duc_head, a_s, b_s, g_s, dsp_acc):
        step = pl.program_id(0)
        blk_ix = nblk - 1 - step

        @pl.when(step == 0)
        def _():
            for ref in (dcw_ref, dcb_ref, dwrg_ref, dbrg_ref, dwig_ref, dbig_ref, gcar, duc_head, dsp_acc):
                ref[...] = jnp.zeros_like(ref)

        first = blk_ix == 0
        u = u_ref[...]
        u_prev = jnp.where(first, 0.0, up_ref[...])
        h_prev_tile = jnp.where(first, 0.0, hp_ref[...])
        gt = _lru_gates(u, u_prev, cw_ref, cb_ref, wrg_ref, brg_ref, wig_ref, big_ref, lam_ref, first)
        a, r, ig, uc, mult = gt["a"], gt["r"], gt["ig"], gt["uc"], gt["mult"]
        dhs_v = dhs_ref[...]

        a_s[...] = a
        b_s[...] = a * dhs_v
        row8 = lax.broadcasted_iota(jnp.int32, (SUBLANES, LRU_WIDTH), 0)

        def tile(tix, carry):
            rs = pl.ds(pl.multiple_of((tiles - 1 - tix) * SUBLANES, SUBLANES), SUBLANES)
            av, bv = a_s[rs, :], b_s[rs, :]
            for k in (1, 2, 4):
                keep = row8 < SUBLANES - k
                bv = jnp.where(keep, av * pltpu.roll(bv, SUBLANES - k, 0) + bv, bv)
                av = jnp.where(keep, av * pltpu.roll(av, SUBLANES - k, 0), av)
            g8 = av * carry + bv
            g_s[rs, :] = g8
            return jnp.broadcast_to(g8[0:1, :], (SUBLANES, LRU_WIDTH))

        g_next = gcar[...]
        gcar[...] = lax.fori_loop(0, tiles, tile, g_next)
        g = dhs_v + _shift_up(g_s[...], g_next, 1)

        h_prev = _shift_down(h_prev_tile, hs_ref[...], 1)
        da = g * h_prev
        iu = ig * uc
        dmult = jnp.where(gt["is_start"], 0.0, g * iu)
        d_ig = g * mult * uc
        duc = g * mult * ig
        dlog_a = da * a - dmult * (a * a) / gt["mult_raw"]
        dzr = (dlog_a * ((-LRU_C) * gt["sp"])) * r * (1.0 - r)
        dsp_acc[...] += jnp.sum(dlog_a * ((-LRU_C) * r), axis=0, keepdims=True)
        dzi = d_ig * ig * (1.0 - ig)
        dbrg_ref[...] += jnp.sum(dzr, axis=0, keepdims=True)
        dbig_ref[...] += jnp.sum(dzi, axis=0, keepdims=True)
        dzr_b, dzi_b = dzr.astype(BF16), dzi.astype(BF16)
        ub = gt["ub"]
        duc_parts = []
        for gi in range(LRU_BLOCKS):
            cs = slice(gi * LRU_BLOCK, (gi + 1) * LRU_BLOCK)
            dwrg_ref[gi] += lax.dot_general(ub[:, cs], dzr_b[:, cs], TN, preferred_element_type=F32)
            dwig_ref[gi] += lax.dot_general(ub[:, cs], dzi_b[:, cs], TN, preferred_element_type=F32)
            duc_parts.append(lax.dot_general(dzr_b[:, cs], wrg_ref[gi], NT, preferred_element_type=F32)
                             + lax.dot_general(dzi_b[:, cs], wig_ref[gi], NT, preferred_element_type=F32))
        duc = duc + jnp.concatenate(duc_parts, axis=1)

        dcb_ref[...] += jnp.sum(duc, axis=0, keepdims=True)
        taps = gt["taps"]
        for jt in range(CONV_WIDTH):
            dcw_ref[jt:jt + 1, :] += jnp.sum(duc * taps[jt], axis=0, keepdims=True)
        head = duc_head[...]
        du = duc * cw_ref[CONV_WIDTH - 1:CONV_WIDTH, :]
        for jt in range(CONV_WIDTH - 1):
            du = du + _shift_up(duc, head, CONV_WIDTH - 1 - jt) * cw_ref[jt:jt + 1, :]
        du_ref[...] = du.astype(BF16)
        duc_head[...] = duc[:SUBLANES, :]

        @pl.when(step == nblk - 1)
        def _():
            dlam_ref[...] = -dsp_acc[...] * jax.nn.sigmoid(-lam_ref[...])

    full2 = lambda shape: pl.BlockSpec(shape, lambda s: (0, 0))
    full3 = lambda shape: pl.BlockSpec(shape, lambda s: (0, 0, 0))
    blk = pl.BlockSpec((tb, LRU_WIDTH), lambda s: (nblk - 1 - s, 0))
    prev_tile = pl.BlockSpec((SUBLANES, LRU_WIDTH), lambda s: (jnp.maximum((nblk - 1 - s) * tiles - 1, 0), 0))
    wshape = (LRU_BLOCKS, LRU_BLOCK, LRU_BLOCK)
    return pl.pallas_call(
        body, name="rglru_bwd", grid=(nblk,),
        in_specs=[blk, prev_tile, blk, prev_tile, blk, full2((CONV_WIDTH, LRU_WIDTH)), full2((1, LRU_WIDTH)),
                  full3(wshape), full2((1, LRU_WIDTH)), full3(wshape), full2((1, LRU_WIDTH)), full2((1, LRU_WIDTH))],
        out_specs=[blk, full2((CONV_WIDTH, LRU_WIDTH)), full2((1, LRU_WIDTH)), full3(wshape), full2((1, LRU_WIDTH)),
                   full3(wshape), full2((1, LRU_WIDTH)), full2((1, LRU_WIDTH))],
        out_shape=[jax.ShapeDtypeStruct((rows, LRU_WIDTH), BF16),
                   jax.ShapeDtypeStruct((CONV_WIDTH, LRU_WIDTH), F32), jax.ShapeDtypeStruct((1, LRU_WIDTH), F32),
                   jax.ShapeDtypeStruct(wshape, F32), jax.ShapeDtypeStruct((1, LRU_WIDTH), F32),
                   jax.ShapeDtypeStruct(wshape, F32), jax.ShapeDtypeStruct((1, LRU_WIDTH), F32),
                   jax.ShapeDtypeStruct((1, LRU_WIDTH), F32)],
        scratch_shapes=[pltpu.VMEM((SUBLANES, LRU_WIDTH), F32), pltpu.VMEM((SUBLANES, LRU_WIDTH), F32),
                        pltpu.VMEM((tb, LRU_WIDTH), F32), pltpu.VMEM((tb, LRU_WIDTH), F32),
                        pltpu.VMEM((tb, LRU_WIDTH), F32), pltpu.VMEM((1, LRU_WIDTH), F32)],
        compiler_params=_cparams(("arbitrary",)),
    )(u, u, hs, hs, dhs, cw, cb, wrg, brg, wig, big, lam)


def _out_proj_loss(a, gate, h, w, gf, target, n_real):
    rows = h.shape[0]
    tr = _row_block(rows)

    def body(a_ref, gate_ref, h_ref, w_ref, g_ref, t_ref, dh_ref, loss_ref, dg_ref, da_ref, dgate_ref, dw_ref,
             dw_acc):
        i = pl.program_id(0)

        @pl.when(i == 0)
        def _():
            loss_ref[...] = jnp.zeros_like(loss_ref)
            dg_ref[...] = jnp.zeros_like(dg_ref)
            dw_acc[...] = jnp.zeros_like(dw_acc)

        gv = g_ref[...]
        av, gatev = a_ref[...], gate_ref[...]
        sg = _silu(gatev)
        y = (av * sg).astype(BF16)
        xn, r = _rms_fwd(h_ref[...] + jnp.dot(y, w_ref[...], preferred_element_type=F32))
        row = i * tr + lax.broadcasted_iota(jnp.int32, (tr, 1), 0)
        live = jnp.logical_and(row >= N_META, row < n_real)
        tgt = t_ref[...]
        tgt = jnp.where(i == 0, pltpu.roll(tgt, N_META, 0), tgt)
        err = jnp.where(live, xn * gv - tgt, 0.0)
        loss_ref[...] += (0.5 / D_MODEL) * jnp.sum(jnp.sum(err * err, axis=1, keepdims=True), axis=0, keepdims=True)
        dx, dg = _rms_bwd(err * (1.0 / D_MODEL), xn, r, gv)
        dg_ref[...] += dg
        dh_ref[...] = dx
        dhb = dx.astype(BF16)
        dw_acc[...] += lax.dot_general(y, dhb, TN, preferred_element_type=F32)
        dy = lax.dot_general(dhb, w_ref[...], NT, preferred_element_type=F32)
        da_ref[...] = dy * sg
        dgate_ref[...] = (dy * av * _dsilu(gatev)).astype(BF16)

        @pl.when(i == rows // tr - 1)
        def _():
            dw_ref[...] = dw_acc[...].astype(BF16)

    blk = pl.BlockSpec((tr, D_MODEL), lambda i: (i, 0))
    wblk = pl.BlockSpec((D_MODEL, D_MODEL), lambda i: (0, 0))
    window = pl.BlockSpec((pl.Element(tr, (0, rows - n_real)), pl.Element(D_MODEL)),
                          lambda i: (pl.multiple_of(jnp.maximum(i * tr - N_META, 0), SUBLANES), 0))
    return pl.pallas_call(
        body, name="b_out_loss", grid=(rows // tr,),
        in_specs=[blk, blk, blk, wblk, pl.BlockSpec((1, D_MODEL), lambda i: (0, 0)), window],
        out_specs=[blk, pl.BlockSpec((1, 1), lambda i: (0, 0)), pl.BlockSpec((1, D_MODEL), lambda i: (0, 0)),
                   blk, blk, wblk],
        out_shape=[jax.ShapeDtypeStruct((rows, D_MODEL), F32), jax.ShapeDtypeStruct((1, 1), F32),
                   jax.ShapeDtypeStruct((1, D_MODEL), F32), jax.ShapeDtypeStruct((rows, D_MODEL), F32),
                   jax.ShapeDtypeStruct((rows, D_MODEL), BF16), jax.ShapeDtypeStruct((D_MODEL, D_MODEL), BF16)],
        scratch_shapes=[pltpu.VMEM((D_MODEL, D_MODEL), F32)],
        compiler_params=_cparams(("arbitrary",)),
    )(a, gate, h, w, gf, target)


def _my_place():
    x, y, c = lax.axis_index("x"), lax.axis_index("y"), lax.axis_index("c")
    return x, y, c, 4 * x + 2 * y + c


def _peer(x, y, c, k):
    px, py, pc = x ^ (k >> 2), y ^ ((k >> 1) & 1), c ^ (k & 1)
    return (px, py, pc), 4 * px + 2 * py + pc


def _exchange_copies(src_of, dst_ref, send_sems, recv_sems, local_sem):
    x, y, c, me = _my_place()
    copies = [pltpu.make_async_copy(src_of(me), dst_ref.at[me], local_sem)]
    for k in range(1, N_DEV):
        peer, pid = _peer(x, y, c, k)
        copies.append(pltpu.make_async_remote_copy(
            src_ref=src_of(pid), dst_ref=dst_ref.at[me], send_sem=send_sems.at[k], recv_sem=recv_sems.at[k],
            device_id=peer, device_id_type=MESH))
    return copies


def _exchange_sems(nb):
    return [pltpu.SemaphoreType.DMA((nb, N_DEV)), pltpu.SemaphoreType.DMA((nb, N_DEV)), pltpu.SemaphoreType.DMA((nb,))]


def _sum_blocks(lands, name):
    n = len(lands)

    def body(*refs):
        for land_ref, out_ref in zip(refs[:n], refs[n:]):
            acc = land_ref[0].astype(F32)
            for d in range(1, N_DEV):
                acc = acc + land_ref[d].astype(F32)
            out_ref[...] = acc

    return pl.pallas_call(
        body, name=name, out_shape=[jax.ShapeDtypeStruct(l.shape[1:], F32) for l in lands],
        compiler_params=pltpu.CompilerParams(vmem_limit_bytes=VMEM_LIMIT),
    )(*lands)


def _all_gather(big, small):
    def body(big_ref, small_ref, obig_ref, osmall_ref, send_sems, recv_sems, local_sems):
        x, y, c, _ = _my_place()
        me, sibling = (x, y, c), (x, y, 1 - c)
        chips = [(1 - x, y), (x, 1 - y), (1 - x, 1 - y)]
        parts = ((big_ref, obig_ref), (small_ref, osmall_ref))

        def slot(dst, place):
            return dst.at[4 * place[0] + 2 * place[1] + place[2]]

        def copy(part, k, block, to, first_hand=False):
            src, dst = parts[part]
            return pltpu.make_async_remote_copy(
                src_ref=src if first_hand else slot(dst, block), dst_ref=slot(dst, block),
                send_sem=send_sems.at[part, k], recv_sem=recv_sems.at[part, k], device_id=to, device_id_type=MESH)

        own = [pltpu.make_async_copy(src, slot(dst, me), local_sems.at[part]) for part, (src, dst) in enumerate(parts)]
        for cp in own:
            cp.start()
        first = []
        for part in range(len(parts)):
            first.append(copy(part, 0, me, sibling, True))
            first += [copy(part, 1 + j, me, (*chip, c), True) for j, chip in enumerate(chips)]
        for cp in first:
            cp.start()
        passed = []
        for j, chip in enumerate(chips):
            for part in range(len(parts)):
                copy(part, 1 + j, (*chip, c), me).wait_recv()
                passed.append(copy(part, 4 + j, (*chip, c), sibling))
                passed[-1].start()
        for part in range(len(parts)):
            copy(part, 0, sibling, me).wait_recv()
            for j, chip in enumerate(chips):
                copy(part, 4 + j, (*chip, 1 - c), me).wait_recv()
        for cp in first + passed:
            cp.wait_send()
        for cp in own:
            cp.wait()

    n = big.shape[0]
    hbm = pl.BlockSpec(memory_space=pl.ANY)
    return pl.pallas_call(
        body, name="weight_all_gather",
        in_specs=[hbm, hbm], out_specs=[hbm, hbm],
        out_shape=[jax.ShapeDtypeStruct((N_DEV,) + big.shape, BF16), jax.ShapeDtypeStruct((N_DEV,) + small.shape, F32)],
        scratch_shapes=[pltpu.SemaphoreType.DMA((2, N_DEV)), pltpu.SemaphoreType.DMA((2, N_DEV)),
                        pltpu.SemaphoreType.DMA((2,))],
        compiler_params=pltpu.CompilerParams(has_side_effects=True),
    )(big, small)


GRAD_CHUNK = 32


def _grad_exchange(gbig, rep):
    n, width = gbig.shape[1:]
    nrep = rep.shape[0]
    n_chips = N_DEV // 2

    def body(gbig_ref, rep_ref, out_ref, orep_ref, pre, stage, got, own_sum, land_rep, send_sems, recv_sems,
             local_sem):
        x, y, c, me = _my_place()
        my_chip = 2 * x + y
        sibling = (x, y, 1 - c)

        local = pltpu.make_async_copy(rep_ref, land_rep.at[me], local_sem.at[0])
        local.start()
        rep_copies = []
        for k in range(1, N_DEV):
            peer, _ = _peer(x, y, c, k)
            rep_copies.append(pltpu.make_async_remote_copy(
                src_ref=rep_ref, dst_ref=land_rep.at[me], send_sem=send_sems.at[6 + k], recv_sem=recv_sems.at[6 + k],
                device_id=peer, device_id_type=MESH))
        swaps = [pltpu.make_async_remote_copy(
            src_ref=gbig_ref.at[2 * q + (1 - c)], dst_ref=pre.at[q], send_sem=send_sems.at[q], recv_sem=recv_sems.at[q],
            device_id=sibling, device_id_type=MESH) for q in range(n_chips)]
        for cp in rep_copies + swaps:
            cp.start()
        for cp in swaps:
            cp.wait_recv()

        def pair_sums(ci, carry):
            rs = pl.ds(pl.multiple_of(ci * GRAD_CHUNK, GRAD_CHUNK), GRAD_CHUNK)
            for q in range(n_chips):
                stage[q, rs, :] = (gbig_ref[2 * q + c, rs, :].astype(F32) + pre[q, rs, :].astype(F32)).astype(BF16)
            own_sum[rs, :] = gbig_ref[me, rs, :].astype(F32) + pre[my_chip, rs, :].astype(F32)
            return carry

        lax.fori_loop(0, n // GRAD_CHUNK, pair_sums, 0)

        hops = []
        for rel in range(1, n_chips):
            qx, qy = x ^ (rel >> 1), y ^ (rel & 1)
            hops.append(pltpu.make_async_remote_copy(
                src_ref=stage.at[2 * qx + qy], dst_ref=got.at[my_chip], send_sem=send_sems.at[3 + rel],
                recv_sem=recv_sems.at[3 + rel], device_id=(qx, qy, c), device_id_type=MESH))
        for cp in hops:
            cp.start()
        for cp in hops:
            cp.wait_recv()

        def chip_sums(ci, carry):
            rs = pl.ds(pl.multiple_of(ci * GRAD_CHUNK, GRAD_CHUNK), GRAD_CHUNK)
            mine = own_sum[rs, :]
            acc = jnp.where(my_chip == 0, mine, got[0, rs, :].astype(F32))
            for q in range(1, n_chips):
                acc = acc + jnp.where(my_chip == q, mine, got[q, rs, :].astype(F32))
            out_ref[rs, :] = acc
            return carry

        lax.fori_loop(0, n // GRAD_CHUNK, chip_sums, 0)

        for cp in rep_copies:
            cp.wait_recv()
        local.wait()
        acc = land_rep[0]
        for d in range(1, N_DEV):
            acc = acc + land_rep[d]
        orep_ref[...] = acc
        for cp in swaps + hops + rep_copies:
            cp.wait_send()

    return pl.pallas_call(
        body, name="grad_exchange",
        in_specs=[pl.BlockSpec(memory_space=pltpu.VMEM), pl.BlockSpec(memory_space=pltpu.VMEM)],
        out_specs=[pl.BlockSpec(memory_space=pltpu.VMEM), pl.BlockSpec(memory_space=pltpu.VMEM)],
        out_shape=[jax.ShapeDtypeStruct((n, width), F32), jax.ShapeDtypeStruct((nrep, LANES), F32)],
        scratch_shapes=[pltpu.VMEM((n_chips, n, width), BF16), pltpu.VMEM((n_chips, n, width), BF16),
                        pltpu.VMEM((n_chips, n, width), BF16), pltpu.VMEM((n, width), F32),
                        pltpu.VMEM((N_DEV, nrep, LANES), F32),
                        pltpu.SemaphoreType.DMA((2 * N_DEV - 2,)), pltpu.SemaphoreType.DMA((2 * N_DEV - 2,)),
                        pltpu.SemaphoreType.DMA((1,))],
        compiler_params=pltpu.CompilerParams(vmem_limit_bytes=VMEM_LIMIT, has_side_effects=True),
    )(gbig, rep)


def _adamw_all(ws, gs, ms, vs):
    n = len(ws)

    def body(*refs):
        w_refs, g_refs, m_refs, v_refs = refs[0:n], refs[n:2 * n], refs[2 * n:3 * n], refs[3 * n:4 * n]
        d_refs, nm_refs, nv_refs = refs[4 * n:5 * n], refs[5 * n:6 * n], refs[6 * n:7 * n]
        for w_ref, g_ref, m_ref, v_ref, d_ref, nm_ref, nv_ref in zip(w_refs, g_refs, m_refs, v_refs, d_refs, nm_refs, nv_refs):
            g = g_ref[...]
            m = ADAM_B1 * m_ref[...] + (1.0 - ADAM_B1) * g
            v = ADAM_B2 * v_ref[...] + (1.0 - ADAM_B2) * jnp.square(g)
            m_hat = m / (1.0 - ADAM_B1 ** ADAM_STEP)
            v_hat = v / (1.0 - ADAM_B2 ** ADAM_STEP)
            d_ref[...] = -ADAM_LR * (m_hat / (jnp.sqrt(v_hat) + ADAM_EPS) + ADAM_WD * w_ref[...])
            nm_ref[...] = m
            nv_ref[...] = v

    shapes = [jax.ShapeDtypeStruct(w.shape, F32) for w in ws]
    outs = pl.pallas_call(
        body, name="adamw", out_shape=shapes * 3,
        compiler_params=pltpu.CompilerParams(vmem_limit_bytes=VMEM_LIMIT),
    )(*ws, *gs, *ms, *vs)
    return outs[0:n], outs[n:2 * n], outs[2 * n:3 * n]


SMALL_A = (("meta_tokens", 16),)
SMALL_B = (("b_norm_g", 1), ("b_conv_w", 4), ("b_conv_b", 1), ("b_b_rg", 1), ("b_b_ig", 1), ("b_lam", 1))
REP = (("a_norm_g", 8), ("a_q_norm_g", 3), ("a_kv_norm_g", 2), ("final_norm_g", 8), ("loss", 1))
SLOT = 16


def _offsets(table, slot=1, start=0):
    out, o = {}, start
    for name, n in table:
        out[name] = (o, n)
        o += -(-n // slot) * slot
    return out, o


def _slotted(a, axis):
    pad = -a.shape[axis] % SLOT
    if not pad:
        return a
    widths = [(0, 0)] * a.ndim
    widths[axis] = (0, pad)
    return jnp.pad(a, widths)


def _rope_tables(rows):
    pos = np.arange(rows, dtype=np.float32)
    inv_freq = (np.float32(ROPE_BASE) ** (-np.arange(0, QK_ROPE, 2, dtype=np.float32) / np.float32(QK_ROPE))).astype(
        np.float32)
    ang = pos[:, None] * inv_freq[None, :]
    cos, sin = np.cos(ang).astype(np.float32), np.sin(ang).astype(np.float32)
    zeros = np.zeros((rows, LANES - QK_ROPE), np.float32)
    return jnp.asarray(np.concatenate([cos, cos, zeros], axis=1)), jnp.asarray(np.concatenate([-sin, sin, zeros], axis=1))


def kernel(x, meta_tokens, a_norm_g, a_w_in, a_q_norm_g, a_kv_norm_g, a_w_uq, a_w_ukv, a_w_out, b_norm_g, b_w_in, b_conv_w, b_conv_b, b_w_rg, b_b_rg, b_w_ig, b_b_ig, b_lam, b_w_out, final_norm_g, loss_target, m_meta_tokens, m_a_norm_g, m_a_w_in, m_a_q_norm_g, m_a_kv_norm_g, m_a_w_uq, m_a_w_ukv, m_a_w_out, m_b_norm_g, m_b_w_in, m_b_conv_w, m_b_conv_b, m_b_w_rg, m_b_b_rg, m_b_w_ig, m_b_b_ig, m_b_lam, m_b_w_out, m_final_norm_g, v_meta_tokens, v_a_norm_g, v_a_w_in, v_a_q_norm_g, v_a_kv_norm_g, v_a_w_uq, v_a_w_ukv, v_a_w_out, v_b_norm_g, v_b_w_in, v_b_conv_w, v_b_conv_b, v_b_w_rg, v_b_b_rg, v_b_w_ig, v_b_b_ig, v_b_lam, v_b_w_out, v_final_norm_g):
    names = ("meta_tokens", "a_norm_g", "a_w_in", "a_q_norm_g", "a_kv_norm_g", "a_w_uq", "a_w_ukv", "a_w_out",
             "b_norm_g", "b_w_in", "b_conv_w", "b_conv_b", "b_w_rg", "b_b_rg", "b_w_ig", "b_b_ig", "b_lam", "b_w_out",
             "final_norm_g")
    w = dict(zip(names, (meta_tokens, a_norm_g, a_w_in, a_q_norm_g, a_kv_norm_g, a_w_uq, a_w_ukv, a_w_out, b_norm_g,
                         b_w_in, b_conv_w, b_conv_b, b_w_rg, b_b_rg, b_w_ig, b_b_ig, b_lam, b_w_out, final_norm_g)))
    mom_m = dict(zip(names, (m_meta_tokens, m_a_norm_g, m_a_w_in, m_a_q_norm_g, m_a_kv_norm_g, m_a_w_uq, m_a_w_ukv,
                             m_a_w_out, m_b_norm_g, m_b_w_in, m_b_conv_w, m_b_conv_b, m_b_w_rg, m_b_b_rg, m_b_w_ig,
                             m_b_b_ig, m_b_lam, m_b_w_out, m_final_norm_g)))
    mom_v = dict(zip(names, (v_meta_tokens, v_a_norm_g, v_a_w_in, v_a_q_norm_g, v_a_kv_norm_g, v_a_w_uq, v_a_w_ukv,
                             v_a_w_out, v_b_norm_g, v_b_w_in, v_b_conv_w, v_b_conv_b, v_b_w_rg, v_b_b_rg, v_b_w_ig,
                             v_b_b_ig, v_b_lam, v_b_w_out, v_final_norm_g)))

    seq = x.shape[1]
    n_real = N_META + seq
    rows = -(-n_real // LANES) * LANES
    scale = (QK_NOPE + QK_ROPE) ** -0.5
    small_off, _ = _offsets(SMALL_A + SMALL_B, SLOT)
    gsmallb_off, _ = _offsets(SMALL_B, SLOT)
    rep_off, _ = _offsets(REP, SLOT)
    cdev_a = a_w_in.shape[-1]
    wide = 2 * LANES

    send_a0 = jnp.pad(a_w_in[0], ((0, 0), (0, wide - cdev_a))).astype(BF16)
    send_small = jnp.concatenate([_slotted(w[nm].reshape(-1, LANES), 0) for nm, _ in SMALL_A + SMALL_B], axis=0)
    sends_a1 = [jnp.pad(a_w_uq[0], ((0, 0), (0, HEAD_PAD - QK_NOPE - QK_ROPE))).astype(BF16), a_w_ukv[0].astype(BF16)]
    lru_rows = LRU_BLOCKS * LRU_BLOCK // N_DEV
    sends_b = [a_w_out[0].astype(BF16), b_w_in[0].astype(BF16), b_w_rg.reshape(lru_rows, LRU_BLOCK).astype(BF16),
               b_w_ig.reshape(lru_rows, LRU_BLOCK).astype(BF16), b_w_out[0].astype(BF16)]
    all_a0, all_small = _all_gather(send_a0, send_small)

    def small_seg(nm):
        o, n = small_off[nm]
        return all_small[:, o:o + n, :]

    w_in_a = all_a0[:, :, :cdev_a].transpose(1, 0, 2).reshape(D_MODEL, N_DEV * cdev_a)
    w_in_a = jnp.concatenate([w_in_a[:, :LAT + QK_ROPE], jnp.zeros((D_MODEL, LAT_PAD - LAT - QK_ROPE), BF16),
                              w_in_a[:, LAT + QK_ROPE:]], axis=1)[None]
    meta_full = small_seg("meta_tokens").transpose(1, 0, 2).reshape(N_META, D_MODEL)
    vec = lambda nm: small_seg(nm).reshape(1, D_MODEL)
    g_b, conv_b, b_rg, b_ig, lam = vec("b_norm_g"), vec("b_conv_b"), vec("b_b_rg"), vec("b_b_ig"), vec("b_lam")
    conv_w = small_seg("b_conv_w").transpose(1, 0, 2).reshape(CONV_WIDTH, LRU_WIDTH)
    g_a, g_q, g_kv = a_norm_g, a_q_norm_g, a_kv_norm_g
    g_f = final_norm_g.reshape(1, D_MODEL)

    cos, sin = _rope_tables(rows)

    h0, lat, gate_a, w_uq, w_ukv = _embed_norm_proj_fwd(x[0], meta_full, rows, g_a, w_in_a, LAT_PAD, "a_in_fwd",
                                                        sends_a1)
    qc, kc, v, vt = _mla_qkv_fwd(lat, g_q, g_kv, w_uq, w_ukv, cos, sin, scale)
    o, lse, w_out_a, w_in_b, w_rg, w_ig, w_out_b = _attn_fwd(qc, kc, vt, sends_b)

    lru_w = lambda g: g.reshape(N_DEV, LRU_BLOCKS, LRU_BLOCK // N_DEV, LRU_BLOCK).transpose(1, 0, 2, 3).reshape(
        LRU_BLOCKS, LRU_BLOCK, LRU_BLOCK)
    w_out_a, w_out_b = w_out_a.reshape(D_MODEL, D_MODEL), w_out_b.reshape(D_MODEL, D_MODEL)
    w_rg, w_ig = lru_w(w_rg), lru_w(w_ig)

    h1, u, gate_b = _out_proj_in_proj(o, gate_a, h0, w_out_a, g_b, w_in_b, LRU_WIDTH, "a_out_b_in_fwd")
    hs = _rglru_fwd(u, conv_w, conv_b, w_rg, b_rg, w_ig, b_ig, lam)
    dh2, loss_part, dg_f, dhs, dgate_b, dw_out_b = _out_proj_loss(hs, gate_b, h1, w_out_b, g_f, loss_target[0],
                                                                   n_real)

    du, dconv_w, dconv_b, dw_rg, db_rg, dw_ig, db_ig, dlam = _rglru_bwd(u, hs, dhs, conv_w, conv_b, w_rg, b_rg, w_ig,
                                                                       b_ig, lam)
    dh1, dw_in_b, dg_b = _norm_proj_bwd(h1, g_b, w_in_b, du, dgate_b, dh2, "b_in_bwd")
    do, dgate_a, dw_out_a, delta = _attn_out_bwd(o, gate_a, dh1, w_out_a)

    def to_cols(g, cdev):
        r = g.shape[0]
        return g.reshape(r, N_DEV, cdev).transpose(1, 0, 2).reshape(N_DEV, -1, LANES)

    lru_g = lambda g: g.reshape(LRU_BLOCKS, N_DEV, LRU_BLOCK // N_DEV, LRU_BLOCK).transpose(1, 0, 2, 3).reshape(
        N_DEV, lru_rows, LRU_BLOCK)
    small_b = {"b_norm_g": dg_b, "b_conv_w": dconv_w, "b_conv_b": dconv_b, "b_b_rg": db_rg, "b_b_ig": db_ig,
               "b_lam": dlam}
    gsends_b = [dw_out_a.reshape(N_DEV, -1, D_MODEL), dw_in_b, lru_g(dw_rg).astype(BF16), lru_g(dw_ig).astype(BF16),
                dw_out_b.reshape(N_DEV, -1, D_MODEL),
                jnp.concatenate([_slotted(to_cols(small_b[nm], LANES).astype(BF16), 1) for nm, _ in SMALL_B], axis=1)]

    dqc, dkc, dv, *lands_b = _attn_bwd(qc, kc, v, lse, delta, do, gsends_b)
    g_out_a, g_in_b, g_rg, g_ig, g_out_b, gsum_small_b = _sum_blocks(lands_b, "sum_blocks_b")
    dlat, dw_uq, dw_ukv, dg_q, dg_kv = _mla_qkv_bwd(lat, g_q, g_kv, w_uq, w_ukv, cos, sin, dqc, dkc, dv, scale)
    dh0, dw_in_a, dg_a, *lands_a1 = _norm_proj_bwd(h0, g_a, w_in_a, dlat, dgate_a, dh1, "a_in_bwd", [dw_uq, dw_ukv])
    g_uq, g_ukv = _sum_blocks(lands_a1, "sum_blocks_a1")

    grad_x = dh0[N_META:n_real][None]

    dw_in_a_nat = jnp.concatenate([dw_in_a[0, :, :LAT + QK_ROPE], dw_in_a[0, :, LAT_PAD:]], axis=1)
    in_lanes = lambda g, cdev: jnp.pad(g.reshape(g.shape[0], N_DEV, cdev).transpose(1, 0, 2),
                                       ((0, 0), (0, 0), (0, wide - cdev)))
    pieces = [in_lanes(dw_in_a_nat, cdev_a), in_lanes(dh0[:N_META].astype(BF16), LANES)]
    used = sum(p.shape[1] for p in pieces)
    pieces.append(jnp.zeros((N_DEV, -used % GRAD_CHUNK, wide), BF16))
    gsend_a0 = jnp.concatenate(pieces, axis=1)
    rep_parts = {"a_norm_g": dg_a, "a_q_norm_g": dg_q, "a_kv_norm_g": dg_kv, "final_norm_g": dg_f,
                 "loss": jnp.broadcast_to(loss_part, (1, LANES))}
    rep = jnp.concatenate([_slotted(rep_parts[nm].reshape(-1, LANES), 0) for nm, _ in REP], axis=0)
    gsum_a0, rep_sum = _grad_exchange(gsend_a0, rep)

    grads = {"a_w_out": g_out_a, "b_w_in": g_in_b, "b_w_rg": g_rg, "b_w_ig": g_ig, "b_w_out": g_out_b,
             "a_w_uq": g_uq[:, :QK_NOPE + QK_ROPE], "a_w_ukv": g_ukv}
    grads = {nm: g.reshape(w[nm].shape) for nm, g in grads.items()}
    grads["a_w_in"] = gsum_a0[:D_MODEL, :cdev_a].reshape(w["a_w_in"].shape)
    grads["meta_tokens"] = gsum_a0[D_MODEL:D_MODEL + N_META, :LANES]
    for off, src in ((gsmallb_off, gsum_small_b), (rep_off, rep_sum)):
        for nm, (o_r, n) in off.items():
            if nm in w:
                grads[nm] = src[o_r:o_r + n].reshape(w[nm].shape)
    loss = rep_sum[rep_off["loss"][0], 0]

    as2d = lambda a: a.reshape(-1, a.shape[-1])
    deltas, new_ms, new_vs = _adamw_all([as2d(w[nm]) for nm in names], [as2d(grads[nm]) for nm in names],
                                        [as2d(mom_m[nm]) for nm in names], [as2d(mom_v[nm]) for nm in names])
    shaped = lambda arrs: [a.reshape(w[nm].shape) for a, nm in zip(arrs, names)]
    return (loss, grad_x, *[grads[nm] for nm in names], *shaped(deltas), *shaped(new_ms), *shaped(new_vs))
```

```python
import functools

import numpy as np
import jax
import jax.numpy as jnp
from jax import lax
from jax.experimental import pallas as pl
from jax.experimental.pallas import tpu as pltpu

F32 = jnp.float32
BF16 = jnp.bfloat16

D_MODEL = 1024
N_META = 16
RMS_EPS = 1e-6
HEADS = 8
QK_NOPE = 128
QK_ROPE = 64
V_HEAD = 128
Q_LORA = 384
KV_LORA = 256
HEAD_PAD = 256
LAT = Q_LORA + KV_LORA
LAT_PAD = LAT + 128
ROPE_BASE = 10000.0
MASK_VALUE = -1e30
LRU_WIDTH = 1024
LRU_BLOCKS = 4
LRU_BLOCK = 256
CONV_WIDTH = 4
LRU_C = 8.0
N_DEV = 8
ADAM_LR, ADAM_B1, ADAM_B2, ADAM_EPS, ADAM_WD, ADAM_STEP = 0.001, 0.9, 0.999, 1e-08, 0.01, 10

LANES = 128
SUBLANES = 8
VMEM_LIMIT = 56 * 1024 * 1024
MESH = pl.DeviceIdType.MESH

NT = (((1,), (1,)), ((), ()))
TN = (((0,), (0,)), ((), ()))


def _row_block(rows):
    return 384 if rows % 384 == 0 else 128


def _cparams(sem):
    return pltpu.CompilerParams(dimension_semantics=sem, vmem_limit_bytes=VMEM_LIMIT)


def _silu(x):
    return x * jax.nn.sigmoid(x)


def _dsilu(x):
    s = jax.nn.sigmoid(x)
    return s * (1.0 + x * (1.0 - s))


def _rms_fwd(x):
    r = lax.rsqrt(jnp.mean(x * x, axis=-1, keepdims=True) + RMS_EPS)
    return x * r, r


def _rms_bwd(dy, xn, r, g):
    t = dy * g
    dx = r * (t - xn * jnp.mean(t * xn, axis=-1, keepdims=True))
    return dx, jnp.sum(dy * xn, axis=0, keepdims=True)


def _expm1_neg(x):
    small = x * (1.0 + x * (1 / 2 + x * (1 / 6 + x * (1 / 24))))
    return jnp.where(x > -0.05, small, jnp.exp(x) - 1.0)


def _softplus_neg(lam):
    z = jnp.exp(-jnp.abs(lam))
    w = z / (2.0 + z)
    w2 = w * w
    series = 2.0 * w * (1.0 + w2 * (1 / 3) + w2 * w2 * (1 / 5))
    return jnp.maximum(-lam, 0.0) + jnp.where(z < 0.1, series, jnp.log(1.0 + z))


def _rider(sends, refs, first, last, all_to_all):
    nb = len(sends)
    if not nb:
        return (lambda: None), (lambda: None)
    send_refs, result_refs = refs[:nb], refs[nb:2 * nb]
    send_sems, recv_sems, local_sems = refs[2 * nb:]
    pick = (lambda ref: (lambda d: ref.at[d])) if all_to_all else (lambda ref: (lambda d: ref))

    def copies():
        out = []
        for b in range(nb):
            out += _exchange_copies(pick(send_refs[b]), result_refs[b], send_sems.at[b], recv_sems.at[b],
                                    local_sems.at[b])
        return out

    def start():
        @pl.when(first)
        def _():
            for cp in copies():
                cp.start()

    def wait():
        @pl.when(last)
        def _():
            for cp in copies():
                cp.wait()

    return start, wait


def _rider_specs(sends, all_to_all):
    nb = len(sends)
    if not nb:
        return [], [], [], []
    hbm = pl.BlockSpec(memory_space=pl.ANY)
    shapes = [jax.ShapeDtypeStruct(s.shape if all_to_all else (N_DEV,) + s.shape, s.dtype) for s in sends]
    return [hbm] * nb, [hbm] * nb, shapes, _exchange_sems(nb)


def _proj_blocks(x, w_ref):
    return jnp.concatenate([jnp.dot(x, w_ref[d], preferred_element_type=F32) for d in range(w_ref.shape[0])], axis=1)


def _embed_norm_proj_fwd(x, meta, rows, g, w, n1, name, wsends=()):
    n_real = N_META + x.shape[0]
    nb, _, cb = w.shape
    n = nb * cb
    tr = _row_block(rows)
    nsteps = rows // tr
    extra = len(wsends)

    def body(x_ref, meta_ref, g_ref, w_ref, *rest):
        h_ref, p1_ref, p2_ref = rest[extra:extra + 3]
        i = pl.program_id(0)
        start, wait = _rider(wsends, rest[:extra] + rest[extra + 3:], i == 0, i == nsteps - 1, False)
        start()
        xw = x_ref[...]
        xw = jnp.where(i == 0, pltpu.roll(xw, N_META, 0), xw)
        row = i * tr + lax.broadcasted_iota(jnp.int32, (tr, 1), 0)
        meta_rows = jnp.concatenate([meta_ref[...], jnp.zeros((tr - N_META, D_MODEL), F32)], axis=0)
        h = jnp.where(row < N_META, meta_rows, jnp.where(row < n_real, xw, 0.0))
        h_ref[...] = h
        xn, _ = _rms_fwd(h)
        p = _proj_blocks((xn * g_ref[...]).astype(BF16), w_ref)
        p1_ref[...] = p[:, :n1]
        p2_ref[...] = p[:, n1:]
        wait()

    r_in, r_out, r_shape, r_scratch = _rider_specs(wsends, False)
    window = pl.BlockSpec((pl.Element(tr, (0, rows - n_real)), pl.Element(D_MODEL)),
                          lambda i: (pl.multiple_of(jnp.maximum(i * tr - N_META, 0), SUBLANES), 0))
    return pl.pallas_call(
        body, name=name, grid=(nsteps,),
        in_specs=[window,
                  pl.BlockSpec((N_META, D_MODEL), lambda i: (0, 0)),
                  pl.BlockSpec((1, D_MODEL), lambda i: (0, 0)),
                  pl.BlockSpec((nb, D_MODEL, cb), lambda i: (0, 0, 0))] + r_in,
        out_specs=[pl.BlockSpec((tr, D_MODEL), lambda i: (i, 0)),
                   pl.BlockSpec((tr, n1), lambda i: (i, 0)),
                   pl.BlockSpec((tr, n - n1), lambda i: (i, 0))] + r_out,
        out_shape=[jax.ShapeDtypeStruct((rows, D_MODEL), F32), jax.ShapeDtypeStruct((rows, n1), F32),
                   jax.ShapeDtypeStruct((rows, n - n1), F32)] + r_shape,
        scratch_shapes=r_scratch,
        compiler_params=_cparams(("arbitrary",)),
    )(x, meta, g, w, *wsends)


def _out_proj_in_proj(a, gate, h, w_out, g, w_in, n1, name):
    rows = h.shape[0]
    nb, _, cb = w_in.shape
    n = nb * cb
    tr = _row_block(rows)

    def body(a_ref, gate_ref, h_ref, wo_ref, g_ref, wi_ref, hn_ref, p1_ref, p2_ref):
        y = (a_ref[...] * _silu(gate_ref[...])).astype(BF16)
        h_new = h_ref[...] + jnp.dot(y, wo_ref[...], preferred_element_type=F32)
        hn_ref[...] = h_new
        xn, _ = _rms_fwd(h_new)
        p = _proj_blocks((xn * g_ref[...]).astype(BF16), wi_ref)
        p1_ref[...] = p[:, :n1]
        p2_ref[...] = p[:, n1:]

    blk = pl.BlockSpec((tr, D_MODEL), lambda i: (i, 0))
    return pl.pallas_call(
        body, name=name, grid=(rows // tr,),
        in_specs=[blk, blk, blk, pl.BlockSpec((D_MODEL, D_MODEL), lambda i: (0, 0)),
                  pl.BlockSpec((1, D_MODEL), lambda i: (0, 0)), pl.BlockSpec((nb, D_MODEL, cb), lambda i: (0, 0, 0))],
        out_specs=[blk, pl.BlockSpec((tr, n1), lambda i: (i, 0)), pl.BlockSpec((tr, n - n1), lambda i: (i, 0))],
        out_shape=[jax.ShapeDtypeStruct((rows, D_MODEL), F32), jax.ShapeDtypeStruct((rows, n1), F32),
                   jax.ShapeDtypeStruct((rows, n - n1), F32)],
        compiler_params=_cparams(("parallel",)),
    )(a, gate, h, w_out, g, w_in)


def _norm_proj_bwd(h, g, w, dp1, dp2, dh_in, name, gsends=()):
    rows = h.shape[0]
    nb, _, cb = w.shape
    n1 = dp1.shape[1]
    n2 = nb * cb - n1
    tr = _row_block(rows)
    nsteps = rows // tr
    extra = len(gsends)

    def body(h_ref, g_ref, w_ref, dp1_ref, dp2_ref, dhin_ref, *rest):
        dh_ref, dw_ref, dg_ref = rest[extra:extra + 3]
        dw_acc = rest[2 * extra + 3]
        i = pl.program_id(0)
        start, wait = _rider(gsends, rest[:extra] + rest[extra + 3:2 * extra + 3] + rest[2 * extra + 4:],
                             i == 0, i == nsteps - 1, True)
        start()

        @pl.when(i == 0)
        def _():
            dw_acc[...] = jnp.zeros_like(dw_acc)
            dg_ref[...] = jnp.zeros_like(dg_ref)

        gv = g_ref[...]
        xn, r = _rms_fwd(h_ref[...])
        hn = (xn * gv).astype(BF16)
        dp = jnp.concatenate([dp1_ref[...].astype(BF16), dp2_ref[...].astype(BF16)], axis=1)
        dhn = jnp.zeros((tr, D_MODEL), F32)
        for d in range(nb):
            dpd = dp[:, d * cb:(d + 1) * cb]
            dw_acc[d] += lax.dot_general(hn, dpd, TN, preferred_element_type=F32)
            dhn = dhn + lax.dot_general(dpd, w_ref[d], NT, preferred_element_type=F32)
        dx, dg = _rms_bwd(dhn, xn, r, gv)
        dg_ref[...] += dg
        dh_ref[...] = dhin_ref[...] + dx

        @pl.when(i == nsteps - 1)
        def _():
            dw_ref[...] = dw_acc[...].astype(BF16)

        wait()

    r_in, r_out, r_shape, r_scratch = _rider_specs(gsends, True)
    wblk = pl.BlockSpec((nb, D_MODEL, cb), lambda i: (0, 0, 0))
    return pl.pallas_call(
        body, name=name, grid=(nsteps,),
        in_specs=[pl.BlockSpec((tr, D_MODEL), lambda i: (i, 0)),
                  pl.BlockSpec((1, D_MODEL), lambda i: (0, 0)),
                  wblk,
                  pl.BlockSpec((tr, n1), lambda i: (i, 0)),
                  pl.BlockSpec((tr, n2), lambda i: (i, 0)),
                  pl.BlockSpec((tr, D_MODEL), lambda i: (i, 0))] + r_in,
        out_specs=[pl.BlockSpec((tr, D_MODEL), lambda i: (i, 0)), wblk,
                   pl.BlockSpec((1, D_MODEL), lambda i: (0, 0))] + r_out,
        out_shape=[jax.ShapeDtypeStruct((rows, D_MODEL), F32),
                   jax.ShapeDtypeStruct((nb, D_MODEL, cb), BF16),
                   jax.ShapeDtypeStruct((1, D_MODEL), F32)] + r_shape,
        scratch_shapes=[pltpu.VMEM((nb, D_MODEL, cb), F32)] + r_scratch,
        compiler_params=_cparams(("arbitrary",)),
    )(h, g, w, dp1, dp2, dh_in, *gsends)


def _attn_out_bwd(o, gate, dh, w):
    rows = o.shape[0]
    tr = _row_block(rows)
    nsteps = rows // tr

    def body(o_ref, gate_ref, dh_ref, w_ref, do_ref, dgate_ref, dw_ref, delta_ref, dw_acc):
        i = pl.program_id(0)

        @pl.when(i == 0)
        def _():
            dw_acc[...] = jnp.zeros_like(dw_acc)

        ov, gv = o_ref[...], gate_ref[...]
        sg = _silu(gv)
        dhb = dh_ref[...].astype(BF16)
        dw_acc[...] += lax.dot_general((ov * sg).astype(BF16), dhb, TN, preferred_element_type=F32)
        dy = lax.dot_general(dhb, w_ref[...], NT, preferred_element_type=F32)
        do = (dy * sg).astype(BF16)
        do_ref[...] = do
        dgate_ref[...] = (dy * ov * _dsilu(gv)).astype(BF16)
        prod = do.astype(F32) * ov
        lane = lax.broadcasted_iota(jnp.int32, (tr, LANES), 1)
        per_head = jnp.zeros((tr, LANES), F32)
        for hd in range(HEADS):
            dsum = jnp.sum(prod[:, hd * V_HEAD:(hd + 1) * V_HEAD], axis=1, keepdims=True)
            per_head = jnp.where(lane == hd, dsum, per_head)
        delta_t = per_head.T
        for hd in range(HEADS):
            delta_ref[hd, 0] = delta_t[hd:hd + 1, :]

        @pl.when(i == nsteps - 1)
        def _():
            dw_ref[...] = dw_acc[...].astype(BF16)

    blk = pl.BlockSpec((tr, D_MODEL), lambda i: (i, 0))
    wblk = pl.BlockSpec((D_MODEL, D_MODEL), lambda i: (0, 0))
    return pl.pallas_call(
        body, name="a_out_bwd", grid=(nsteps,),
        in_specs=[blk, blk, blk, wblk],
        out_specs=[blk, blk, wblk, pl.BlockSpec((HEADS, 1, 1, tr), lambda i: (0, i, 0, 0))],
        out_shape=[jax.ShapeDtypeStruct((rows, D_MODEL), BF16), jax.ShapeDtypeStruct((rows, D_MODEL), BF16),
                   jax.ShapeDtypeStruct((D_MODEL, D_MODEL), BF16),
                   jax.ShapeDtypeStruct((HEADS, nsteps, 1, tr), F32)],
        scratch_shapes=[pltpu.VMEM((D_MODEL, D_MODEL), F32)],
        compiler_params=_cparams(("arbitrary",)),
    )(o, gate, dh, w)


def _rope(v, cos, sin, lane):
    swapped = jnp.where(lane < QK_ROPE // 2, pltpu.roll(v, LANES - QK_ROPE // 2, 1), pltpu.roll(v, QK_ROPE // 2, 1))
    return v * cos + swapped * sin


def _unrope(dv, cos, sin, lane):
    t = dv * sin
    swapped = jnp.where(lane < QK_ROPE // 2, pltpu.roll(t, LANES - QK_ROPE // 2, 1), pltpu.roll(t, QK_ROPE // 2, 1))
    return dv * cos + swapped


def _mla_qkv_fwd(lat, gq, gkv, wuq, wukv, cos, sin, scale):
    rows = lat.shape[0]
    tr = _row_block(rows)

    def body(lat_ref, gq_ref, gkv_ref, wuq_ref, wukv_ref, cos_ref, sin_ref, qc_ref, kc_ref, v_ref, vt_ref):
        qn, _ = _rms_fwd(lat_ref[:, :Q_LORA])
        kvn, _ = _rms_fwd(lat_ref[:, Q_LORA:LAT])
        qnb = (qn * gq_ref[...]).astype(BF16)
        kvnb = (kvn * gkv_ref[...]).astype(BF16)
        c, s = cos_ref[...], sin_ref[...]
        lane = lax.broadcasted_iota(jnp.int32, (tr, LANES), 1)
        kr = _rope(lat_ref[:, LAT:LAT_PAD], c, s, lane).astype(BF16)
        for hd in range(HEADS):
            o = hd * HEAD_PAD
            q = jnp.dot(qnb, wuq_ref[hd], preferred_element_type=F32)
            kv = jnp.dot(kvnb, wukv_ref[hd], preferred_element_type=F32)
            qc_ref[:, o:o + QK_NOPE] = (q[:, :QK_NOPE] * scale).astype(BF16)
            qc_ref[:, o + QK_NOPE:o + HEAD_PAD] = (_rope(q[:, QK_NOPE:], c, s, lane) * scale).astype(BF16)
            kc_ref[:, o:o + QK_NOPE] = kv[:, :QK_NOPE].astype(BF16)
            kc_ref[:, o + QK_NOPE:o + HEAD_PAD] = kr
            vh = kv[:, QK_NOPE:]
            v_ref[:, hd * V_HEAD:(hd + 1) * V_HEAD] = vh.astype(BF16)
            vt_ref[hd, 0] = vh.T.astype(BF16)

    full = lambda shape: pl.BlockSpec(shape, lambda i: (0,) * len(shape))
    rowb = lambda n: pl.BlockSpec((tr, n), lambda i: (i, 0))
    return pl.pallas_call(
        body, name="mla_qkv_fwd", grid=(rows // tr,),
        in_specs=[rowb(LAT_PAD), full((1, Q_LORA)), full((1, KV_LORA)), full((HEADS, Q_LORA, HEAD_PAD)),
                  full((HEADS, KV_LORA, HEAD_PAD)), rowb(LANES), rowb(LANES)],
        out_specs=[rowb(HEADS * HEAD_PAD), rowb(HEADS * HEAD_PAD), rowb(HEADS * V_HEAD),
                   pl.BlockSpec((HEADS, 1, V_HEAD, tr), lambda i: (0, i, 0, 0))],
        out_shape=[jax.ShapeDtypeStruct((rows, HEADS * HEAD_PAD), BF16),
                   jax.ShapeDtypeStruct((rows, HEADS * HEAD_PAD), BF16),
                   jax.ShapeDtypeStruct((rows, HEADS * V_HEAD), BF16),
                   jax.ShapeDtypeStruct((HEADS, rows // tr, V_HEAD, tr), BF16)],
        compiler_params=_cparams(("parallel",)),
    )(lat, gq, gkv, wuq, wukv, cos, sin)


def _mla_qkv_bwd(lat, gq, gkv, wuq, wukv, cos, sin, dqc, dkc, dv, scale):
    rows = lat.shape[0]
    tr = _row_block(rows)
    nsteps = rows // tr

    def body(lat_ref, gq_ref, gkv_ref, wuq_ref, wukv_ref, cos_ref, sin_ref, dqc_ref, dkc_ref, dv_ref,
             dlat_ref, dwuq_out, dwukv_out, dgq_ref, dgkv_ref, dwuq_ref, dwukv_ref):
        @pl.when(pl.program_id(0) == 0)
        def _():
            dwuq_ref[...] = jnp.zeros_like(dwuq_ref)
            dwukv_ref[...] = jnp.zeros_like(dwukv_ref)
            dgq_ref[...] = jnp.zeros_like(dgq_ref)
            dgkv_ref[...] = jnp.zeros_like(dgkv_ref)

        c, s = cos_ref[...], sin_ref[...]
        lane = lax.broadcasted_iota(jnp.int32, (tr, LANES), 1)
        gqv, gkvv = gq_ref[...], gkv_ref[...]
        qn, rq = _rms_fwd(lat_ref[:, :Q_LORA])
        kvn, rkv = _rms_fwd(lat_ref[:, Q_LORA:LAT])
        qnb = (qn * gqv).astype(BF16)
        kvnb = (kvn * gkvv).astype(BF16)
        dkr = jnp.zeros((tr, LANES), F32)
        dqn = jnp.zeros((tr, Q_LORA), F32)
        dkvn = jnp.zeros((tr, KV_LORA), F32)
        for hd in range(HEADS):
            o = hd * HEAD_PAD
            dq = jnp.concatenate(
                [dqc_ref[:, o:o + QK_NOPE],
                 _unrope(dqc_ref[:, o + QK_NOPE:o + HEAD_PAD].astype(F32), c, s, lane).astype(BF16)], axis=1)
            dkv = jnp.concatenate([dkc_ref[:, o:o + QK_NOPE], dv_ref[:, hd * V_HEAD:(hd + 1) * V_HEAD]], axis=1)
            dkr = dkr + dkc_ref[:, o + QK_NOPE:o + HEAD_PAD].astype(F32)
            dwuq_ref[hd] += scale * lax.dot_general(qnb, dq, TN, preferred_element_type=F32)
            dwukv_ref[hd] += lax.dot_general(kvnb, dkv, TN, preferred_element_type=F32)
            dqn = dqn + lax.dot_general(dq, wuq_ref[hd], NT, preferred_element_type=F32)
            dkvn = dkvn + lax.dot_general(dkv, wukv_ref[hd], NT, preferred_element_type=F32)
        dqn = scale * dqn
        dqlat, dgq = _rms_bwd(dqn, qn, rq, gqv)
        dkvlat, dgkv = _rms_bwd(dkvn, kvn, rkv, gkvv)
        dgq_ref[...] += dgq
        dgkv_ref[...] += dgkv
        dlat_ref[:, :Q_LORA] = dqlat.astype(BF16)
        dlat_ref[:, Q_LORA:LAT] = dkvlat.astype(BF16)
        dlat_ref[:, LAT:LAT_PAD] = _unrope(dkr, c, s, lane).astype(BF16)

        @pl.when(pl.program_id(0) == nsteps - 1)
        def _():
            dwuq_out[...] = dwuq_ref[...].astype(BF16)
            dwukv_out[...] = dwukv_ref[...].astype(BF16)

    full = lambda shape: pl.BlockSpec(shape, lambda i: (0,) * len(shape))
    rowb = lambda n: pl.BlockSpec((tr, n), lambda i: (i, 0))
    return pl.pallas_call(
        body, name="mla_qkv_bwd", grid=(nsteps,),
        in_specs=[rowb(LAT_PAD), full((1, Q_LORA)), full((1, KV_LORA)), full((HEADS, Q_LORA, HEAD_PAD)),
                  full((HEADS, KV_LORA, HEAD_PAD)), rowb(LANES), rowb(LANES),
                  rowb(HEADS * HEAD_PAD), rowb(HEADS * HEAD_PAD), rowb(HEADS * V_HEAD)],
        out_specs=[rowb(LAT_PAD), full((HEADS, Q_LORA, HEAD_PAD)), full((HEADS, KV_LORA, HEAD_PAD)),
                   full((1, Q_LORA)), full((1, KV_LORA))],
        out_shape=[jax.ShapeDtypeStruct((rows, LAT_PAD), BF16),
                   jax.ShapeDtypeStruct((HEADS, Q_LORA, HEAD_PAD), BF16),
                   jax.ShapeDtypeStruct((HEADS, KV_LORA, HEAD_PAD), BF16),
                   jax.ShapeDtypeStruct((1, Q_LORA), F32),
                   jax.ShapeDtypeStruct((1, KV_LORA), F32)],
        scratch_shapes=[pltpu.VMEM((HEADS, Q_LORA, HEAD_PAD), F32), pltpu.VMEM((HEADS, KV_LORA, HEAD_PAD), F32)],
        compiler_params=_cparams(("arbitrary",)),
    )(lat, gq, gkv, wuq, wukv, cos, sin, dqc, dkc, dv)


ATTN_UNROLL = 4
ATTN_UNROLL_BWD = 4
ATTN_HEADS = 2
ATTN_HEADS_FWD = 4


def _causal_mask_t(t):
    key = lax.broadcasted_iota(jnp.int32, (t, t), 0)
    query = lax.broadcasted_iota(jnp.int32, (t, t), 1)
    return key <= query


def _attn_fwd(qc, kc, vt, wsends):
    rows = qc.shape[0]
    t = _row_block(rows)
    nblk = rows // t
    nw = len(wsends)

    def body(q_ref, k_ref, vt_ref, *rest):
        o_ref, lse_ref = rest[nw:nw + 2]
        m_ref, l_ref, acc_ref, st_a, st_b = rest[2 * nw + 2:2 * nw + 7]
        i = pl.program_id(1)
        start, wait = _rider(wsends, rest[:nw] + rest[nw + 2:2 * nw + 2] + rest[2 * nw + 7:],
                             jnp.logical_and(pl.program_id(0) == 0, i == 0),
                             jnp.logical_and(pl.program_id(0) == HEADS // ATTN_HEADS_FWD - 1, i == nblk - 1), False)
        start()

        m_ref[...] = jnp.full_like(m_ref, MASK_VALUE)
        l_ref[...] = jnp.zeros_like(l_ref)
        acc_ref[...] = jnp.zeros_like(acc_ref)
        heads = range(ATTN_HEADS_FWD)
        qs = [q_ref[:, hh * HEAD_PAD:(hh + 1) * HEAD_PAD] for hh in heads]

        def scores(j, hh, st_ref):
            rs = pl.ds(pl.multiple_of(j * t, t), t)
            st_ref[hh] = lax.dot_general(k_ref[rs, hh * HEAD_PAD:(hh + 1) * HEAD_PAD], qs[hh], NT,
                                         preferred_element_type=F32)

        def consume(j, hh, st_ref, masked):
            st = st_ref[hh]
            if masked:
                st = jnp.where(_causal_mask_t(t), st, MASK_VALUE)
            m_prev = m_ref[hh]
            m_new = jnp.maximum(m_prev, jnp.max(st, axis=0, keepdims=True))
            alpha = jnp.exp(m_prev - m_new)
            pt = jnp.exp(st - m_new)
            l_ref[hh] = alpha * l_ref[hh] + jnp.sum(pt, axis=0, keepdims=True)
            acc_ref[hh] = alpha * acc_ref[hh] + jnp.dot(vt_ref[hh, j], pt.astype(BF16), preferred_element_type=F32)
            m_ref[hh] = m_new

        bufs = (st_a, st_b)

        def step(j, parity, issue_next, masked):
            if issue_next:
                for hh in heads:
                    scores(j + 1, hh, bufs[1 - parity])
            for hh in heads:
                consume(j, hh, bufs[parity], masked)

        for hh in heads:
            scores(0, hh, st_a)

        def trip(it, carry):
            for u in range(ATTN_UNROLL):
                step(it * ATTN_UNROLL + u, u % 2, True, False)
            return carry

        trips = i // ATTN_UNROLL
        lax.fori_loop(0, trips, trip, 0)
        j0 = trips * ATTN_UNROLL
        for left in range(1, ATTN_UNROLL + 1):
            @pl.when(i + 1 - j0 == left)
            def _(left=left):
                for u in range(left):
                    step(j0 + u, u % 2, u < left - 1, u == left - 1)

        for hh in heads:
            o_ref[:, hh * V_HEAD:(hh + 1) * V_HEAD] = (acc_ref[hh] / l_ref[hh]).T
            lse_ref[hh, 0] = m_ref[hh] + jnp.log(l_ref[hh])
        wait()

    r_in, r_out, r_shape, r_scratch = _rider_specs(wsends, False)
    nh = ATTN_HEADS_FWD
    return pl.pallas_call(
        body, name="attn_fwd", grid=(HEADS // nh, nblk),
        in_specs=[pl.BlockSpec((t, nh * HEAD_PAD), lambda g, i: (i, g)),
                  pl.BlockSpec((rows, nh * HEAD_PAD), lambda g, i: (0, g)),
                  pl.BlockSpec((nh, nblk, V_HEAD, t), lambda g, i: (g, 0, 0, 0))] + r_in,
        out_specs=[pl.BlockSpec((t, nh * V_HEAD), lambda g, i: (i, g)),
                   pl.BlockSpec((nh, 1, 1, t), lambda g, i: (g, i, 0, 0))] + r_out,
        out_shape=[jax.ShapeDtypeStruct((rows, HEADS * V_HEAD), F32),
                   jax.ShapeDtypeStruct((HEADS, nblk, 1, t), F32)] + r_shape,
        scratch_shapes=[pltpu.VMEM((nh, 1, t), F32), pltpu.VMEM((nh, 1, t), F32), pltpu.VMEM((nh, V_HEAD, t), F32),
                        pltpu.VMEM((nh, t, t), F32), pltpu.VMEM((nh, t, t), F32)] + r_scratch,
        compiler_params=_cparams(("arbitrary", "arbitrary")),
    )(qc, kc, vt, *wsends)


def _attn_bwd(qc, kc, v, lse, delta, do, gsends):
    rows = qc.shape[0]
    t = _row_block(rows)
    nblk = rows // t
    ng = len(gsends)

    def body(q_ref, k_ref, v_ref, lse_ref, delta_ref, do_ref, *rest):
        dq_ref, dk_ref, dv_ref = rest[ng:ng + 3]
        dq_acc, dk_acc, dv_acc, st_a, dp_a, st_b, dp_b = rest[2 * ng + 3:2 * ng + 10]
        j = pl.program_id(1)
        start, wait = _rider(gsends, rest[:ng] + rest[ng + 3:2 * ng + 3] + rest[2 * ng + 10:],
                             jnp.logical_and(pl.program_id(0) == 0, j == 0),
                             jnp.logical_and(pl.program_id(0) == HEADS // ATTN_HEADS - 1, j == nblk - 1), True)
        start()

        @pl.when(j == 0)
        def _():
            dq_acc[...] = jnp.zeros_like(dq_acc)

        dk_acc[...] = jnp.zeros_like(dk_acc)
        dv_acc[...] = jnp.zeros_like(dv_acc)
        heads = range(ATTN_HEADS)
        qk = lambda hh: slice(hh * HEAD_PAD, (hh + 1) * HEAD_PAD)
        vo = lambda hh: slice(hh * V_HEAD, (hh + 1) * V_HEAD)
        ks = [k_ref[:, qk(hh)] for hh in heads]
        vs = [v_ref[:, vo(hh)] for hh in heads]

        def products(i, hh, st_ref, dp_ref):
            rs = pl.ds(pl.multiple_of(i * t, t), t)
            st_ref[hh] = lax.dot_general(ks[hh], q_ref[rs, qk(hh)], NT, preferred_element_type=F32)
            dp_ref[hh] = lax.dot_general(vs[hh], do_ref[rs, vo(hh)], NT, preferred_element_type=F32)

        def consume(i, hh, st_ref, dp_ref):
            rs = pl.ds(pl.multiple_of(i * t, t), t)
            q = q_ref[rs, qk(hh)]
            dob = do_ref[rs, vo(hh)]
            st = jnp.where(jnp.logical_or(_causal_mask_t(t), i != j), st_ref[hh], MASK_VALUE)
            pt = jnp.exp(st - lse_ref[hh, i])
            dv_acc[hh] += jnp.dot(pt.astype(BF16), dob, preferred_element_type=F32)
            dst = (pt * (dp_ref[hh] - delta_ref[hh, i])).astype(BF16)
            dk_acc[hh] += jnp.dot(dst, q, preferred_element_type=F32)
            dq_acc[hh, rs, :] += lax.dot_general(dst, ks[hh], TN, preferred_element_type=F32)

        bufs = ((st_a, dp_a), (st_b, dp_b))

        def step(i, parity, issue_next):
            if issue_next:
                for hh in heads:
                    products(i + 1, hh, *bufs[1 - parity])
            for hh in heads:
                consume(i, hh, *bufs[parity])

        for hh in heads:
            products(j, hh, st_a, dp_a)

        def trip(it, carry):
            for u in range(ATTN_UNROLL_BWD):
                step(j + it * ATTN_UNROLL_BWD + u, u % 2, True)
            return carry

        trips = (nblk - 1 - j) // ATTN_UNROLL_BWD
        lax.fori_loop(0, trips, trip, 0)
        i0 = j + trips * ATTN_UNROLL_BWD
        for left in range(1, ATTN_UNROLL_BWD + 1):
            @pl.when(nblk - i0 == left)
            def _(left=left):
                for u in range(left):
                    step(i0 + u, u % 2, u < left - 1)

        for hh in heads:
            dk_ref[:, qk(hh)] = dk_acc[hh].astype(BF16)
            dv_ref[:, vo(hh)] = dv_acc[hh].astype(BF16)

        @pl.when(j == nblk - 1)
        def _():
            for hh in heads:
                dq_ref[:, qk(hh)] = dq_acc[hh].astype(BF16)

        wait()

    nh = ATTN_HEADS
    stat = pl.BlockSpec((nh, nblk, 1, t), lambda g, j: (g, 0, 0, 0))
    r_in, r_out, r_shape, r_scratch = _rider_specs(gsends, True)
    return pl.pallas_call(
        body, name="attn_bwd", grid=(HEADS // nh, nblk),
        in_specs=[pl.BlockSpec((rows, nh * HEAD_PAD), lambda g, j: (0, g)),
                  pl.BlockSpec((t, nh * HEAD_PAD), lambda g, j: (j, g)),
                  pl.BlockSpec((t, nh * V_HEAD), lambda g, j: (j, g)),
                  stat, stat,
                  pl.BlockSpec((rows, nh * V_HEAD), lambda g, j: (0, g))] + r_in,
        out_specs=[pl.BlockSpec((rows, nh * HEAD_PAD), lambda g, j: (0, g)),
                   pl.BlockSpec((t, nh * HEAD_PAD), lambda g, j: (j, g)),
                   pl.BlockSpec((t, nh * V_HEAD), lambda g, j: (j, g))] + r_out,
        out_shape=[jax.ShapeDtypeStruct((rows, HEADS * HEAD_PAD), BF16),
                   jax.ShapeDtypeStruct((rows, HEADS * HEAD_PAD), BF16),
                   jax.ShapeDtypeStruct((rows, HEADS * V_HEAD), BF16)] + r_shape,
        scratch_shapes=[pltpu.VMEM((nh, rows, HEAD_PAD), F32), pltpu.VMEM((nh, t, HEAD_PAD), F32),
                        pltpu.VMEM((nh, t, V_HEAD), F32)] + [pltpu.VMEM((nh, t, t), F32)] * 4 + r_scratch,
        compiler_params=_cparams(("arbitrary", "arbitrary")),
    )(qc, kc, v, lse, delta, do, *gsends)


def _shift_down(prev_tile, x, k):
    xx = jnp.concatenate([prev_tile, x], axis=0)
    return pltpu.roll(xx, k, 0)[SUBLANES:]


def _shift_up(x, next_tile, k):
    n = x.shape[0]
    xx = jnp.concatenate([x, next_tile], axis=0)
    return pltpu.roll(xx, n + SUBLANES - k, 0)[:n]


def _lru_gates(u, u_prev, cw_ref, cb_ref, wrg_ref, brg_ref, wig_ref, big_ref, lam_ref, first_block):
    taps = [_shift_down(u_prev, u, CONV_WIDTH - 1 - j) if j < CONV_WIDTH - 1 else u for j in range(CONV_WIDTH)]
    uc = cb_ref[...] + taps[0] * cw_ref[0:1, :]
    for j in range(1, CONV_WIDTH):
        uc = uc + taps[j] * cw_ref[j:j + 1, :]
    ub = uc.astype(BF16)
    zr = jnp.concatenate([jnp.dot(ub[:, g * LRU_BLOCK:(g + 1) * LRU_BLOCK], wrg_ref[g], preferred_element_type=F32)
                          for g in range(LRU_BLOCKS)], axis=1) + brg_ref[...]
    zi = jnp.concatenate([jnp.dot(ub[:, g * LRU_BLOCK:(g + 1) * LRU_BLOCK], wig_ref[g], preferred_element_type=F32)
                          for g in range(LRU_BLOCKS)], axis=1) + big_ref[...]
    r = jax.nn.sigmoid(zr)
    ig = jax.nn.sigmoid(zi)
    sp = _softplus_neg(lam_ref[...])
    log_a = (-LRU_C) * r * sp
    a = jnp.exp(log_a)
    m2 = -_expm1_neg(2.0 * log_a)
    mult_raw = m2 * lax.rsqrt(jnp.maximum(m2, 1e-30))
    row = lax.broadcasted_iota(jnp.int32, u.shape, 0)
    is_start = jnp.logical_and(first_block, row == 0)
    mult = jnp.where(is_start, 1.0, mult_raw)
    return dict(taps=taps, uc=uc, ub=ub, r=r, ig=ig, sp=sp, a=a, mult=mult, mult_raw=mult_raw, is_start=is_start)


def _rglru_fwd(u, cw, cb, wrg, brg, wig, big, lam):
    rows = u.shape[0]
    tb = _row_block(rows)

    def body(u_ref, cw_ref, cb_ref, wrg_ref, brg_ref, wig_ref, big_ref, lam_ref, hs_ref, utail, hcar, a_s, b_s):
        i = pl.program_id(0)

        @pl.when(i == 0)
        def _():
            utail[...] = jnp.zeros_like(utail)
            hcar[...] = jnp.zeros_like(hcar)

        u = u_ref[...]
        gt = _lru_gates(u, utail[...], cw_ref, cb_ref, wrg_ref, brg_ref, wig_ref, big_ref, lam_ref, i == 0)
        a_s[...] = gt["a"]
        b_s[...] = gt["mult"] * (gt["ig"] * gt["uc"])
        row8 = lax.broadcasted_iota(jnp.int32, (SUBLANES, LRU_WIDTH), 0)

        def tile(tix, carry):
            rs = pl.ds(pl.multiple_of(tix * SUBLANES, SUBLANES), SUBLANES)
            av, bv = a_s[rs, :], b_s[rs, :]
            for k in (1, 2, 4):
                keep = row8 >= k
                bv = jnp.where(keep, av * pltpu.roll(bv, k, 0) + bv, bv)
                av = jnp.where(keep, av * pltpu.roll(av, k, 0), av)
            h8 = av * carry + bv
            hs_ref[rs, :] = h8
            return jnp.broadcast_to(h8[SUBLANES - 1:SUBLANES, :], (SUBLANES, LRU_WIDTH))

        hcar[...] = lax.fori_loop(0, tb // SUBLANES, tile, hcar[...])
        utail[...] = u[tb - SUBLANES:, :]

    full2 = lambda shape: pl.BlockSpec(shape, lambda i: (0, 0))
    full3 = lambda shape: pl.BlockSpec(shape, lambda i: (0, 0, 0))
    blk = pl.BlockSpec((tb, LRU_WIDTH), lambda i: (i, 0))
    return pl.pallas_call(
        body, name="rglru_fwd", grid=(rows // tb,),
        in_specs=[blk, full2((CONV_WIDTH, LRU_WIDTH)), full2((1, LRU_WIDTH)),
                  full3((LRU_BLOCKS, LRU_BLOCK, LRU_BLOCK)), full2((1, LRU_WIDTH)),
                  full3((LRU_BLOCKS, LRU_BLOCK, LRU_BLOCK)), full2((1, LRU_WIDTH)), full2((1, LRU_WIDTH))],
        out_specs=blk,
        out_shape=jax.ShapeDtypeStruct((rows, LRU_WIDTH), F32),
        scratch_shapes=[pltpu.VMEM((SUBLANES, LRU_WIDTH), F32), pltpu.VMEM((SUBLANES, LRU_WIDTH), F32),
                        pltpu.VMEM((tb, LRU_WIDTH), F32), pltpu.VMEM((tb, LRU_WIDTH), F32)],
        compiler_params=_cparams(("arbitrary",)),
    )(u, cw, cb, wrg, brg, wig, big, lam)


def _rglru_bwd(u, hs, dhs, cw, cb, wrg, brg, wig, big, lam):
    rows = u.shape[0]
    tb = _row_block(rows)
    nblk = rows // tb
    tiles = tb // SUBLANES

    def body(u_ref, up_ref, hs_ref, hp_ref, dhs_ref, cw_ref, cb_ref, wrg_ref, brg_ref, wig_ref, big_ref, lam_ref,
             du_ref, dcw_ref, dcb_ref, dwrg_ref, dbrg_ref, dwig_ref, dbig_ref, dlam_ref,
             gcar, duc_head, a_s, b_s, g_s, dsp_acc):
        step = pl.program_id(0)
        blk_ix = nblk - 1 - step

        @pl.when(step == 0)
        def _():
            for ref in (dcw_ref, dcb_ref, dwrg_ref, dbrg_ref, dwig_ref, dbig_ref, gcar, duc_head, dsp_acc):
                ref[...] = jnp.zeros_like(ref)

        first = blk_ix == 0
        u = u_ref[...]
        u_prev = jnp.where(first, 0.0, up_ref[...])
        h_prev_tile = jnp.where(first, 0.0, hp_ref[...])
        gt = _lru_gates(u, u_prev, cw_ref, cb_ref, wrg_ref, brg_ref, wig_ref, big_ref, lam_ref, first)
        a, r, ig, uc, mult = gt["a"], gt["r"], gt["ig"], gt["uc"], gt["mult"]
        dhs_v = dhs_ref[...]

        a_s[...] = a
        b_s[...] = a * dhs_v
        row8 = lax.broadcasted_iota(jnp.int32, (SUBLANES, LRU_WIDTH), 0)

        def tile(tix, carry):
            rs = pl.ds(pl.multiple_of((tiles - 1 - tix) * SUBLANES, SUBLANES), SUBLANES)
            av, bv = a_s[rs, :], b_s[rs, :]
            for k in (1, 2, 4):
                keep = row8 < SUBLANES - k
                bv = jnp.where(keep, av * pltpu.roll(bv, SUBLANES - k, 0) + bv, bv)
                av = jnp.where(keep, av * pltpu.roll(av, SUBLANES - k, 0), av)
            g8 = av * carry + bv
            g_s[rs, :] = g8
            return jnp.broadcast_to(g8[0:1, :], (SUBLANES, LRU_WIDTH))

        g_next = gcar[...]
        gcar[...] = lax.fori_loop(0, tiles, tile, g_next)
        g = dhs_v + _shift_up(g_s[...], g_next, 1)

        h_prev = _shift_down(h_prev_tile, hs_ref[...], 1)
        da = g * h_prev
        iu = ig * uc
        dmult = jnp.where(gt["is_start"], 0.0, g * iu)
        d_ig = g * mult * uc
        duc = g * mult * ig
        dlog_a = da * a - dmult * (a * a) / gt["mult_raw"]
        dzr = (dlog_a * ((-LRU_C) * gt["sp"])) * r * (1.0 - r)
        dsp_acc[...] += jnp.sum(dlog_a * ((-LRU_C) * r), axis=0, keepdims=True)
        dzi = d_ig * ig * (1.0 - ig)
        dbrg_ref[...] += jnp.sum(dzr, axis=0, keepdims=True)
        dbig_ref[...] += jnp.sum(dzi, axis=0, keepdims=True)
        dzr_b, dzi_b = dzr.astype(BF16), dzi.astype(BF16)
        ub = gt["ub"]
        duc_parts = []
        for gi in range(LRU_BLOCKS):
            cs = slice(gi * LRU_BLOCK, (gi + 1) * LRU_BLOCK)
            dwrg_ref[gi] += lax.dot_general(ub[:, cs], dzr_b[:, cs], TN, preferred_element_type=F32)
            dwig_ref[gi] += lax.dot_general(ub[:, cs], dzi_b[:, cs], TN, preferred_element_type=F32)
            duc_parts.append(lax.dot_general(dzr_b[:, cs], wrg_ref[gi], NT, preferred_element_type=F32)
                             + lax.dot_general(dzi_b[:, cs], wig_ref[gi], NT, preferred_element_type=F32))
        duc = duc + jnp.concatenate(duc_parts, axis=1)

        dcb_ref[...] += jnp.sum(duc, axis=0, keepdims=True)
        taps = gt["taps"]
        for jt in range(CONV_WIDTH):
            dcw_ref[jt:jt + 1, :] += jnp.sum(duc * taps[jt], axis=0, keepdims=True)
        head = duc_head[...]
        du = duc * cw_ref[CONV_WIDTH - 1:CONV_WIDTH, :]
        for jt in range(CONV_WIDTH - 1):
            du = du + _shift_up(duc, head, CONV_WIDTH - 1 - jt) * cw_ref[jt:jt + 1, :]
        du_ref[...] = du.astype(BF16)
        duc_head[...] = duc[:SUBLANES, :]

        @pl.when(step == nblk - 1)
        def _():
            dlam_ref[...] = -dsp_acc[...] * jax.nn.sigmoid(-lam_ref[...])

    full2 = lambda shape: pl.BlockSpec(shape, lambda s: (0, 0))
    full3 = lambda shape: pl.BlockSpec(shape, lambda s: (0, 0, 0))
    blk = pl.BlockSpec((tb, LRU_WIDTH), lambda s: (nblk - 1 - s, 0))
    prev_tile = pl.BlockSpec((SUBLANES, LRU_WIDTH), lambda s: (jnp.maximum((nblk - 1 - s) * tiles - 1, 0), 0))
    wshape = (LRU_BLOCKS, LRU_BLOCK, LRU_BLOCK)
    return pl.pallas_call(
        body, name="rglru_bwd", grid=(nblk,),
        in_specs=[blk, prev_tile, blk, prev_tile, blk, full2((CONV_WIDTH, LRU_WIDTH)), full2((1, LRU_WIDTH)),
                  full3(wshape), full2((1, LRU_WIDTH)), full3(wshape), full2((1, LRU_WIDTH)), full2((1, LRU_WIDTH))],
        out_specs=[blk, full2((CONV_WIDTH, LRU_WIDTH)), full2((1, LRU_WIDTH)), full3(wshape), full2((1, LRU_WIDTH)),
                   full3(wshape), full2((1, LRU_WIDTH)), full2((1, LRU_WIDTH))],
        out_shape=[jax.ShapeDtypeStruct((rows, LRU_WIDTH), BF16),
                   jax.ShapeDtypeStruct((CONV_WIDTH, LRU_WIDTH), F32), jax.ShapeDtypeStruct((1, LRU_WIDTH), F32),
                   jax.ShapeDtypeStruct(wshape, F32), jax.ShapeDtypeStruct((1, LRU_WIDTH), F32),
                   jax.ShapeDtypeStruct(wshape, F32), jax.ShapeDtypeStruct((1, LRU_WIDTH), F32),
                   jax.ShapeDtypeStruct((1, LRU_WIDTH), F32)],
        scratch_shapes=[pltpu.VMEM((SUBLANES, LRU_WIDTH), F32), pltpu.VMEM((SUBLANES, LRU_WIDTH), F32),
                        pltpu.VMEM((tb, LRU_WIDTH), F32), pltpu.VMEM((tb, LRU_WIDTH), F32),
                        pltpu.VMEM((tb, LRU_WIDTH), F32), pltpu.VMEM((1, LRU_WIDTH), F32)],
        compiler_params=_cparams(("arbitrary",)),
    )(u, u, hs, hs, dhs, cw, cb, wrg, brg, wig, big, lam)


def _out_proj_loss(a, gate, h, w, gf, target, n_real):
    rows = h.shape[0]
    tr = _row_block(rows)

    def body(a_ref, gate_ref, h_ref, w_ref, g_ref, t_ref, dh_ref, loss_ref, dg_ref, da_ref, dgate_ref, dw_ref,
             dw_acc):
        i = pl.program_id(0)

        @pl.when(i == 0)
        def _():
            loss_ref[...] = jnp.zeros_like(loss_ref)
            dg_ref[...] = jnp.zeros_like(dg_ref)
            dw_acc[...] = jnp.zeros_like(dw_acc)

        gv = g_ref[...]
        av, gatev = a_ref[...], gate_ref[...]
        sg = _silu(gatev)
        y = (av * sg).astype(BF16)
        xn, r = _rms_fwd(h_ref[...] + jnp.dot(y, w_ref[...], preferred_element_type=F32))
        row = i * tr + lax.broadcasted_iota(jnp.int32, (tr, 1), 0)
        live = jnp.logical_and(row >= N_META, row < n_real)
        tgt = t_ref[...]
        tgt = jnp.where(i == 0, pltpu.roll(tgt, N_META, 0), tgt)
        err = jnp.where(live, xn * gv - tgt, 0.0)
        loss_ref[...] += (0.5 / D_MODEL) * jnp.sum(jnp.sum(err * err, axis=1, keepdims=True), axis=0, keepdims=True)
        dx, dg = _rms_bwd(err * (1.0 / D_MODEL), xn, r, gv)
        dg_ref[...] += dg
        dh_ref[...] = dx
        dhb = dx.astype(BF16)
        dw_acc[...] += lax.dot_general(y, dhb, TN, preferred_element_type=F32)
        dy = lax.dot_general(dhb, w_ref[...], NT, preferred_element_type=F32)
        da_ref[...] = dy * sg
        dgate_ref[...] = (dy * av * _dsilu(gatev)).astype(BF16)

        @pl.when(i == rows // tr - 1)
        def _():
            dw_ref[...] = dw_acc[...].astype(BF16)

    blk = pl.BlockSpec((tr, D_MODEL), lambda i: (i, 0))
    wblk = pl.BlockSpec((D_MODEL, D_MODEL), lambda i: (0, 0))
    window = pl.BlockSpec((pl.Element(tr, (0, rows - n_real)), pl.Element(D_MODEL)),
                          lambda i: (pl.multiple_of(jnp.maximum(i * tr - N_META, 0), SUBLANES), 0))
    return pl.pallas_call(
        body, name="b_out_loss", grid=(rows // tr,),
        in_specs=[blk, blk, blk, wblk, pl.BlockSpec((1, D_MODEL), lambda i: (0, 0)), window],
        out_specs=[blk, pl.BlockSpec((1, 1), lambda i: (0, 0)), pl.BlockSpec((1, D_MODEL), lambda i: (0, 0)),
                   blk, blk, wblk],
        out_shape=[jax.ShapeDtypeStruct((rows, D_MODEL), F32), jax.ShapeDtypeStruct((1, 1), F32),
                   jax.ShapeDtypeStruct((1, D_MODEL), F32), jax.ShapeDtypeStruct((rows, D_MODEL), F32),
                   jax.ShapeDtypeStruct((rows, D_MODEL), BF16), jax.ShapeDtypeStruct((D_MODEL, D_MODEL), BF16)],
        scratch_shapes=[pltpu.VMEM((D_MODEL, D_MODEL), F32)],
        compiler_params=_cparams(("arbitrary",)),
    )(a, gate, h, w, gf, target)


def _my_place():
    x, y, c = lax.axis_index("x"), lax.axis_index("y"), lax.axis_index("c")
    return x, y, c, 4 * x + 2 * y + c


def _peer(x, y, c, k):
    px, py, pc = x ^ (k >> 2), y ^ ((k >> 1) & 1), c ^ (k & 1)
    return (px, py, pc), 4 * px + 2 * py + pc


def _exchange_copies(src_of, dst_ref, send_sems, recv_sems, local_sem):
    x, y, c, me = _my_place()
    copies = [pltpu.make_async_copy(src_of(me), dst_ref.at[me], local_sem)]
    for k in range(1, N_DEV):
        peer, pid = _peer(x, y, c, k)
        copies.append(pltpu.make_async_remote_copy(
            src_ref=src_of(pid), dst_ref=dst_ref.at[me], send_sem=send_sems.at[k], recv_sem=recv_sems.at[k],
            device_id=peer, device_id_type=MESH))
    return copies


def _exchange_sems(nb):
    return [pltpu.SemaphoreType.DMA((nb, N_DEV)), pltpu.SemaphoreType.DMA((nb, N_DEV)), pltpu.SemaphoreType.DMA((nb,))]


def _sum_blocks(lands, name):
    n = len(lands)

    def body(*refs):
        for land_ref, out_ref in zip(refs[:n], refs[n:]):
            acc = land_ref[0].astype(F32)
            for d in range(1, N_DEV):
                acc = acc + land_ref[d].astype(F32)
            out_ref[...] = acc

    return pl.pallas_call(
        body, name=name, out_shape=[jax.ShapeDtypeStruct(l.shape[1:], F32) for l in lands],
        compiler_params=pltpu.CompilerParams(vmem_limit_bytes=VMEM_LIMIT),
    )(*lands)


def _all_gather(big, small):
    def body(big_ref, small_ref, obig_ref, osmall_ref, send_sems, recv_sems, local_sems):
        x, y, c, _ = _my_place()
        me, sibling = (x, y, c), (x, y, 1 - c)
        chips = [(1 - x, y), (x, 1 - y), (1 - x, 1 - y)]
        parts = ((big_ref, obig_ref), (small_ref, osmall_ref))

        def slot(dst, place):
            return dst.at[4 * place[0] + 2 * place[1] + place[2]]

        def copy(part, k, block, to, first_hand=False):
            src, dst = parts[part]
            return pltpu.make_async_remote_copy(
                src_ref=src if first_hand else slot(dst, block), dst_ref=slot(dst, block),
                send_sem=send_sems.at[part, k], recv_sem=recv_sems.at[part, k], device_id=to, device_id_type=MESH)

        own = [pltpu.make_async_copy(src, slot(dst, me), local_sems.at[part]) for part, (src, dst) in enumerate(parts)]
        for cp in own:
            cp.start()
        first = []
        for part in range(len(parts)):
            first.append(copy(part, 0, me, sibling, True))
            first += [copy(part, 1 + j, me, (*chip, c), True) for j, chip in enumerate(chips)]
        for cp in first:
            cp.start()
        passed = []
        for j, chip in enumerate(chips):
            for part in range(len(parts)):
                copy(part, 1 + j, (*chip, c), me).wait_recv()
                passed.append(copy(part, 4 + j, (*chip, c), sibling))
                passed[-1].start()
        for part in range(len(parts)):
            copy(part, 0, sibling, me).wait_recv()
            for j, chip in enumerate(chips):
                copy(part, 4 + j, (*chip, 1 - c), me).wait_recv()
        for cp in first + passed:
            cp.wait_send()
        for cp in own:
            cp.wait()

    n = big.shape[0]
    hbm = pl.BlockSpec(memory_space=pl.ANY)
    return pl.pallas_call(
        body, name="weight_all_gather",
        in_specs=[hbm, hbm], out_specs=[hbm, hbm],
        out_shape=[jax.ShapeDtypeStruct((N_DEV,) + big.shape, BF16), jax.ShapeDtypeStruct((N_DEV,) + small.shape, F32)],
        scratch_shapes=[pltpu.SemaphoreType.DMA((2, N_DEV)), pltpu.SemaphoreType.DMA((2, N_DEV)),
                        pltpu.SemaphoreType.DMA((2,))],
        compiler_params=pltpu.CompilerParams(has_side_effects=True),
    )(big, small)


GRAD_CHUNK = 32


def _grad_exchange(gbig, rep):
    n, width = gbig.shape[1:]
    nrep = rep.shape[0]
    n_chips = N_DEV // 2

    def body(gbig_ref, rep_ref, out_ref, orep_ref, pre, stage, got, own_sum, land_rep, send_sems, recv_sems,
             local_sem):
        x, y, c, me = _my_place()
        my_chip = 2 * x + y
        sibling = (x, y, 1 - c)

        local = pltpu.make_async_copy(rep_ref, land_rep.at[me], local_sem.at[0])
        local.start()
        rep_copies = []
        for k in range(1, N_DEV):
            peer, _ = _peer(x, y, c, k)
            rep_copies.append(pltpu.make_async_remote_copy(
                src_ref=rep_ref, dst_ref=land_rep.at[me], send_sem=send_sems.at[6 + k], recv_sem=recv_sems.at[6 + k],
                device_id=peer, device_id_type=MESH))
        swaps = [pltpu.make_async_remote_copy(
            src_ref=gbig_ref.at[2 * q + (1 - c)], dst_ref=pre.at[q], send_sem=send_sems.at[q], recv_sem=recv_sems.at[q],
            device_id=sibling, device_id_type=MESH) for q in range(n_chips)]
        for cp in rep_copies + swaps:
            cp.start()
        for cp in swaps:
            cp.wait_recv()

        def pair_sums(ci, carry):
            rs = pl.ds(pl.multiple_of(ci * GRAD_CHUNK, GRAD_CHUNK), GRAD_CHUNK)
            for q in range(n_chips):
                stage[q, rs, :] = (gbig_ref[2 * q + c, rs, :].astype(F32) + pre[q, rs, :].astype(F32)).astype(BF16)
            own_sum[rs, :] = gbig_ref[me, rs, :].astype(F32) + pre[my_chip, rs, :].astype(F32)
            return carry

        lax.fori_loop(0, n // GRAD_CHUNK, pair_sums, 0)

        hops = []
        for rel in range(1, n_chips):
            qx, qy = x ^ (rel >> 1), y ^ (rel & 1)
            hops.append(pltpu.make_async_remote_copy(
                src_ref=stage.at[2 * qx + qy], dst_ref=got.at[my_chip], send_sem=send_sems.at[3 + rel],
                recv_sem=recv_sems.at[3 + rel], device_id=(qx, qy, c), device_id_type=MESH))
        for cp in hops:
            cp.start()
        for cp in hops:
            cp.wait_recv()

        def chip_sums(ci, carry):
            rs = pl.ds(pl.multiple_of(ci * GRAD_CHUNK, GRAD_CHUNK), GRAD_CHUNK)
            mine = own_sum[rs, :]
            acc = jnp.where(my_chip == 0, mine, got[0, rs, :].astype(F32))
            for q in range(1, n_chips):
                acc = acc + jnp.where(my_chip == q, mine, got[q, rs, :].astype(F32))
            out_ref[rs, :] = acc
            return carry

        lax.fori_loop(0, n // GRAD_CHUNK, chip_sums, 0)

        for cp in rep_copies:
            cp.wait_recv()
        local.wait()
        acc = land_rep[0]
        for d in range(1, N_DEV):
            acc = acc + land_rep[d]
        orep_ref[...] = acc
        for cp in swaps + hops + rep_copies:
            cp.wait_send()

    return pl.pallas_call(
        body, name="grad_exchange",
        in_specs=[pl.BlockSpec(memory_space=pltpu.VMEM), pl.BlockSpec(memory_space=pltpu.VMEM)],
        out_specs=[pl.BlockSpec(memory_space=pltpu.VMEM), pl.BlockSpec(memory_space=pltpu.VMEM)],
        out_shape=[jax.ShapeDtypeStruct((n, width), F32), jax.ShapeDtypeStruct((nrep, LANES), F32)],
        scratch_shapes=[pltpu.VMEM((n_chips, n, width), BF16), pltpu.VMEM((n_chips, n, width), BF16),
                        pltpu.VMEM((n_chips, n, width), BF16), pltpu.VMEM((n, width), F32),
                        pltpu.VMEM((N_DEV, nrep, LANES), F32),
                        pltpu.SemaphoreType.DMA((2 * N_DEV - 2,)), pltpu.SemaphoreType.DMA((2 * N_DEV - 2,)),
                        pltpu.SemaphoreType.DMA((1,))],
        compiler_params=pltpu.CompilerParams(vmem_limit_bytes=VMEM_LIMIT, has_side_effects=True),
    )(gbig, rep)


def _adamw_all(ws, gs, ms, vs):
    n = len(ws)

    def body(*refs):
        w_refs, g_refs, m_refs, v_refs = refs[0:n], refs[n:2 * n], refs[2 * n:3 * n], refs[3 * n:4 * n]
        d_refs, nm_refs, nv_refs = refs[4 * n:5 * n], refs[5 * n:6 * n], refs[6 * n:7 * n]
        for w_ref, g_ref, m_ref, v_ref, d_ref, nm_ref, nv_ref in zip(w_refs, g_refs, m_refs, v_refs, d_refs, nm_refs, nv_refs):
            g = g_ref[...]
            m = ADAM_B1 * m_ref[...] + (1.0 - ADAM_B1) * g
            v = ADAM_B2 * v_ref[...] + (1.0 - ADAM_B2) * jnp.square(g)
            m_hat = m / (1.0 - ADAM_B1 ** ADAM_STEP)
            v_hat = v / (1.0 - ADAM_B2 ** ADAM_STEP)
            d_ref[...] = -ADAM_LR * (m_hat / (jnp.sqrt(v_hat) + ADAM_EPS) + ADAM_WD * w_ref[...])
            nm_ref[...] = m
            nv_ref[...] = v

    shapes = [jax.ShapeDtypeStruct(w.shape, F32) for w in ws]
    outs = pl.pallas_call(
        body, name="adamw", out_shape=shapes * 3,
        compiler_params=pltpu.CompilerParams(vmem_limit_bytes=VMEM_LIMIT),
    )(*ws, *gs, *ms, *vs)
    return outs[0:n], outs[n:2 * n], outs[2 * n:3 * n]


SMALL_A = (("meta_tokens", 16),)
SMALL_B = (("b_norm_g", 1), ("b_conv_w", 4), ("b_conv_b", 1), ("b_b_rg", 1), ("b_b_ig", 1), ("b_lam", 1))
REP = (("a_norm_g", 8), ("a_q_norm_g", 3), ("a_kv_norm_g", 2), ("final_norm_g", 8), ("loss", 1))
SLOT = 16


def _offsets(table, slot=1, start=0):
    out, o = {}, start
    for name, n in table:
        out[name] = (o, n)
        o += -(-n // slot) * slot
    return out, o


def _slotted(a, axis):
    pad = -a.shape[axis] % SLOT
    if not pad:
        return a
    widths = [(0, 0)] * a.ndim
    widths[axis] = (0, pad)
    return jnp.pad(a, widths)


def _rope_tables(rows):
    pos = np.arange(rows, dtype=np.float32)
    inv_freq = (np.float32(ROPE_BASE) ** (-np.arange(0, QK_ROPE, 2, dtype=np.float32) / np.float32(QK_ROPE))).astype(
        np.float32)
    ang = pos[:, None] * inv_freq[None, :]
    cos, sin = np.cos(ang).astype(np.float32), np.sin(ang).astype(np.float32)
    zeros = np.zeros((rows, LANES - QK_ROPE), np.float32)
    return jnp.asarray(np.concatenate([cos, cos, zeros], axis=1)), jnp.asarray(np.concatenate([-sin, sin, zeros], axis=1))


def kernel(x, meta_tokens, a_norm_g, a_w_in, a_q_norm_g, a_kv_norm_g, a_w_uq, a_w_ukv, a_w_out, b_norm_g, b_w_in, b_conv_w, b_conv_b, b_w_rg, b_b_rg, b_w_ig, b_b_ig, b_lam, b_w_out, final_norm_g, loss_target, m_meta_tokens, m_a_norm_g, m_a_w_in, m_a_q_norm_g, m_a_kv_norm_g, m_a_w_uq, m_a_w_ukv, m_a_w_out, m_b_norm_g, m_b_w_in, m_b_conv_w, m_b_conv_b, m_b_w_rg, m_b_b_rg, m_b_w_ig, m_b_b_ig, m_b_lam, m_b_w_out, m_final_norm_g, v_meta_tokens, v_a_norm_g, v_a_w_in, v_a_q_norm_g, v_a_kv_norm_g, v_a_w_uq, v_a_w_ukv, v_a_w_out, v_b_norm_g, v_b_w_in, v_b_conv_w, v_b_conv_b, v_b_w_rg, v_b_b_rg, v_b_w_ig, v_b_b_ig, v_b_lam, v_b_w_out, v_final_norm_g):
    names = ("meta_tokens", "a_norm_g", "a_w_in", "a_q_norm_g", "a_kv_norm_g", "a_w_uq", "a_w_ukv", "a_w_out",
             "b_norm_g", "b_w_in", "b_conv_w", "b_conv_b", "b_w_rg", "b_b_rg", "b_w_ig", "b_b_ig", "b_lam", "b_w_out",
             "final_norm_g")
    w = dict(zip(names, (meta_tokens, a_norm_g, a_w_in, a_q_norm_g, a_kv_norm_g, a_w_uq, a_w_ukv, a_w_out, b_norm_g,
                         b_w_in, b_conv_w, b_conv_b, b_w_rg, b_b_rg, b_w_ig, b_b_ig, b_lam, b_w_out, final_norm_g)))
    mom_m = dict(zip(names, (m_meta_tokens, m_a_norm_g, m_a_w_in, m_a_q_norm_g, m_a_kv_norm_g, m_a_w_uq, m_a_w_ukv,
                             m_a_w_out, m_b_norm_g, m_b_w_in, m_b_conv_w, m_b_conv_b, m_b_w_rg, m_b_b_rg, m_b_w_ig,
                             m_b_b_ig, m_b_lam, m_b_w_out, m_final_norm_g)))
    mom_v = dict(zip(names, (v_meta_tokens, v_a_norm_g, v_a_w_in, v_a_q_norm_g, v_a_kv_norm_g, v_a_w_uq, v_a_w_ukv,
                             v_a_w_out, v_b_norm_g, v_b_w_in, v_b_conv_w, v_b_conv_b, v_b_w_rg, v_b_b_rg, v_b_w_ig,
                             v_b_b_ig, v_b_lam, v_b_w_out, v_final_norm_g)))

    seq = x.shape[1]
    n_real = N_META + seq
    rows = -(-n_real // LANES) * LANES
    scale = (QK_NOPE + QK_ROPE) ** -0.5
    small_off, _ = _offsets(SMALL_A + SMALL_B, SLOT)
    gsmallb_off, _ = _offsets(SMALL_B, SLOT)
    rep_off, _ = _offsets(REP, SLOT)
    cdev_a = a_w_in.shape[-1]
    wide = 2 * LANES

    send_a0 = jnp.pad(a_w_in[0], ((0, 0), (0, wide - cdev_a))).astype(BF16)
    send_small = jnp.concatenate([_slotted(w[nm].reshape(-1, LANES), 0) for nm, _ in SMALL_A + SMALL_B], axis=0)
    sends_a1 = [jnp.pad(a_w_uq[0], ((0, 0), (0, HEAD_PAD - QK_NOPE - QK_ROPE))).astype(BF16), a_w_ukv[0].astype(BF16)]
    lru_rows = LRU_BLOCKS * LRU_BLOCK // N_DEV
    sends_b = [a_w_out[0].astype(BF16), b_w_in[0].astype(BF16), b_w_rg.reshape(lru_rows, LRU_BLOCK).astype(BF16),
               b_w_ig.reshape(lru_rows, LRU_BLOCK).astype(BF16), b_w_out[0].astype(BF16)]
    all_a0, all_small = _all_gather(send_a0, send_small)

    def small_seg(nm):
        o, n = small_off[nm]
        return all_small[:, o:o + n, :]

    w_in_a = all_a0[:, :, :cdev_a].transpose(1, 0, 2).reshape(D_MODEL, N_DEV * cdev_a)
    w_in_a = jnp.concatenate([w_in_a[:, :LAT + QK_ROPE], jnp.zeros((D_MODEL, LAT_PAD - LAT - QK_ROPE), BF16),
                              w_in_a[:, LAT + QK_ROPE:]], axis=1)[None]
    meta_full = small_seg("meta_tokens").transpose(1, 0, 2).reshape(N_META, D_MODEL)
    vec = lambda nm: small_seg(nm).reshape(1, D_MODEL)
    g_b, conv_b, b_rg, b_ig, lam = vec("b_norm_g"), vec("b_conv_b"), vec("b_b_rg"), vec("b_b_ig"), vec("b_lam")
    conv_w = small_seg("b_conv_w").transpose(1, 0, 2).reshape(CONV_WIDTH, LRU_WIDTH)
    g_a, g_q, g_kv = a_norm_g, a_q_norm_g, a_kv_norm_g
    g_f = final_norm_g.reshape(1, D_MODEL)

    cos, sin = _rope_tables(rows)

    h0, lat, gate_a, w_uq, w_ukv = _embed_norm_proj_fwd(x[0], meta_full, rows, g_a, w_in_a, LAT_PAD, "a_in_fwd",
                                                        sends_a1)
    qc, kc, v, vt = _mla_qkv_fwd(lat, g_q, g_kv, w_uq, w_ukv, cos, sin, scale)
    o, lse, w_out_a, w_in_b, w_rg, w_ig, w_out_b = _attn_fwd(qc, kc, vt, sends_b)

    lru_w = lambda g: g.reshape(N_DEV, LRU_BLOCKS, LRU_BLOCK // N_DEV, LRU_BLOCK).transpose(1, 0, 2, 3).reshape(
        LRU_BLOCKS, LRU_BLOCK, LRU_BLOCK)
    w_out_a, w_out_b = w_out_a.reshape(D_MODEL, D_MODEL), w_out_b.reshape(D_MODEL, D_MODEL)
    w_rg, w_ig = lru_w(w_rg), lru_w(w_ig)

    h1, u, gate_b = _out_proj_in_proj(o, gate_a, h0, w_out_a, g_b, w_in_b, LRU_WIDTH, "a_out_b_in_fwd")
    hs = _rglru_fwd(u, conv_w, conv_b, w_rg, b_rg, w_ig, b_ig, lam)
    dh2, loss_part, dg_f, dhs, dgate_b, dw_out_b = _out_proj_loss(hs, gate_b, h1, w_out_b, g_f, loss_target[0],
                                                                   n_real)

    du, dconv_w, dconv_b, dw_rg, db_rg, dw_ig, db_ig, dlam = _rglru_bwd(u, hs, dhs, conv_w, conv_b, w_rg, b_rg, w_ig,
                                                                       b_ig, lam)
    dh1, dw_in_b, dg_b = _norm_proj_bwd(h1, g_b, w_in_b, du, dgate_b, dh2, "b_in_bwd")
    do, dgate_a, dw_out_a, delta = _attn_out_bwd(o, gate_a, dh1, w_out_a)

    def to_cols(g, cdev):
        r = g.shape[0]
        return g.reshape(r, N_DEV, cdev).transpose(1, 0, 2).reshape(N_DEV, -1, LANES)

    lru_g = lambda g: g.reshape(LRU_BLOCKS, N_DEV, LRU_BLOCK // N_DEV, LRU_BLOCK).transpose(1, 0, 2, 3).reshape(
        N_DEV, lru_rows, LRU_BLOCK)
    small_b = {"b_norm_g": dg_b, "b_conv_w": dconv_w, "b_conv_b": dconv_b, "b_b_rg": db_rg, "b_b_ig": db_ig,
               "b_lam": dlam}
    gsends_b = [dw_out_a.reshape(N_DEV, -1, D_MODEL), dw_in_b, lru_g(dw_rg).astype(BF16), lru_g(dw_ig).astype(BF16),
                dw_out_b.reshape(N_DEV, -1, D_MODEL),
                jnp.concatenate([_slotted(to_cols(small_b[nm], LANES).astype(BF16), 1) for nm, _ in SMALL_B], axis=1)]

    dqc, dkc, dv, *lands_b = _attn_bwd(qc, kc, v, lse, delta, do, gsends_b)
    g_out_a, g_in_b, g_rg, g_ig, g_out_b, gsum_small_b = _sum_blocks(lands_b, "sum_blocks_b")
    dlat, dw_uq, dw_ukv, dg_q, dg_kv = _mla_qkv_bwd(lat, g_q, g_kv, w_uq, w_ukv, cos, sin, dqc, dkc, dv, scale)
    dh0, dw_in_a, dg_a, *lands_a1 = _norm_proj_bwd(h0, g_a, w_in_a, dlat, dgate_a, dh1, "a_in_bwd", [dw_uq, dw_ukv])
    g_uq, g_ukv = _sum_blocks(lands_a1, "sum_blocks_a1")

    grad_x = dh0[N_META:n_real][None]

    dw_in_a_nat = jnp.concatenate([dw_in_a[0, :, :LAT + QK_ROPE], dw_in_a[0, :, LAT_PAD:]], axis=1)
    in_lanes = lambda g, cdev: jnp.pad(g.reshape(g.shape[0], N_DEV, cdev).transpose(1, 0, 2),
                                       ((0, 0), (0, 0), (0, wide - cdev)))
    pieces = [in_lanes(dw_in_a_nat, cdev_a), in_lanes(dh0[:N_META].astype(BF16), LANES)]
    used = sum(p.shape[1] for p in pieces)
    pieces.append(jnp.zeros((N_DEV, -used % GRAD_CHUNK, wide), BF16))
    gsend_a0 = jnp.concatenate(pieces, axis=1)
    rep_parts = {"a_norm_g": dg_a, "a_q_norm_g": dg_q, "a_kv_norm_g": dg_kv, "final_norm_g": dg_f,
                 "loss": jnp.broadcast_to(loss_part, (1, LANES))}
    rep = jnp.concatenate([_slotted(rep_parts[nm].reshape(-1, LANES), 0) for nm, _ in REP], axis=0)
    gsum_a0, rep_sum = _grad_exchange(gsend_a0, rep)

    grads = {"a_w_out": g_out_a, "b_w_in": g_in_b, "b_w_rg": g_rg, "b_w_ig": g_ig, "b_w_out": g_out_b,
             "a_w_uq": g_uq[:, :QK_NOPE + QK_ROPE], "a_w_ukv": g_ukv}
    grads = {nm: g.reshape(w[nm].shape) for nm, g in grads.items()}
    grads["a_w_in"] = gsum_a0[:D_MODEL, :cdev_a].reshape(w["a_w_in"].shape)
    grads["meta_tokens"] = gsum_a0[D_MODEL:D_MODEL + N_META, :LANES]
    for off, src in ((gsmallb_off, gsum_small_b), (rep_off, rep_sum)):
        for nm, (o_r, n) in off.items():
            if nm in w:
                grads[nm] = src[o_r:o_r + n].reshape(w[nm].shape)
    loss = rep_sum[rep_off["loss"][0], 0]

    as2d = lambda a: a.reshape(-1, a.shape[-1])
    deltas, new_ms, new_vs = _adamw_all([as2d(w[nm]) for nm in names], [as2d(grads[nm]) for nm in names],
                                        [as2d(mom_m[nm]) for nm in names], [as2d(mom_v[nm]) for nm in names])
    shaped = lambda arrs: [a.reshape(w[nm].shape) for a, nm in zip(arrs, names)]
    return (loss, grad_x, *[grads[nm] for nm in names], *shaped(deltas), *shaped(new_ms), *shaped(new_vs))
```

```python
import functools

import numpy as np
import jax
import jax.numpy as jnp
from jax import lax
from jax.experimental import pallas as pl
from jax.experimental.pallas import tpu as pltpu

F32 = jnp.float32
BF16 = jnp.bfloat16

D_MODEL = 1024
N_META = 16
RMS_EPS = 1e-6
HEADS = 8
QK_NOPE = 128
QK_ROPE = 64
V_HEAD = 128
Q_LORA = 384
KV_LORA = 256
HEAD_PAD = 256
LAT = Q_LORA + KV_LORA
LAT_PAD = LAT + 128
ROPE_BASE = 10000.0
MASK_VALUE = -1e30
LRU_WIDTH = 1024
LRU_BLOCKS = 4
LRU_BLOCK = 256
CONV_WIDTH = 4
LRU_C = 8.0
N_DEV = 8
ADAM_LR, ADAM_B1, ADAM_B2, ADAM_EPS, ADAM_WD, ADAM_STEP = 0.001, 0.9, 0.999, 1e-08, 0.01, 10

LANES = 128
SUBLANES = 8
VMEM_LIMIT = 56 * 1024 * 1024
MESH = pl.DeviceIdType.MESH

NT = (((1,), (1,)), ((), ()))
TN = (((0,), (0,)), ((), ()))


def _row_block(rows):
    return 384 if rows % 384 == 0 else 128


def _cparams(sem):
    return pltpu.CompilerParams(dimension_semantics=sem, vmem_limit_bytes=VMEM_LIMIT)


def _silu(x):
    return x * jax.nn.sigmoid(x)


def _dsilu(x):
    s = jax.nn.sigmoid(x)
    return s * (1.0 + x * (1.0 - s))


def _rms_fwd(x):
    r = lax.rsqrt(jnp.mean(x * x, axis=-1, keepdims=True) + RMS_EPS)
    return x * r, r


def _rms_bwd(dy, xn, r, g):
    t = dy * g
    dx = r * (t - xn * jnp.mean(t * xn, axis=-1, keepdims=True))
    return dx, jnp.sum(dy * xn, axis=0, keepdims=True)


def _expm1_neg(x):
    small = x * (1.0 + x * (1 / 2 + x * (1 / 6 + x * (1 / 24))))
    return jnp.where(x > -0.05, small, jnp.exp(x) - 1.0)


def _softplus_neg(lam):
    z = jnp.exp(-jnp.abs(lam))
    w = z / (2.0 + z)
    w2 = w * w
    series = 2.0 * w * (1.0 + w2 * (1 / 3) + w2 * w2 * (1 / 5))
    return jnp.maximum(-lam, 0.0) + jnp.where(z < 0.1, series, jnp.log(1.0 + z))


def _rider(sends, refs, first, last, all_to_all):
    nb = len(sends)
    if not nb:
        return (lambda: None), (lambda: None)
    send_refs, result_refs = refs[:nb], refs[nb:2 * nb]
    send_sems, recv_sems, local_sems = refs[2 * nb:]
    pick = (lambda ref: (lambda d: ref.at[d])) if all_to_all else (lambda ref: (lambda d: ref))

    def copies():
        out = []
        for b in range(nb):
            out += _exchange_copies(pick(send_refs[b]), result_refs[b], send_sems.at[b], recv_sems.at[b],
                                    local_sems.at[b])
        return out

    def start():
        @pl.when(first)
        def _():
            for cp in copies():
                cp.start()

    def wait():
        @pl.when(last)
        def _():
            for cp in copies():
                cp.wait()

    return start, wait


def _rider_specs(sends, all_to_all):
    nb = len(sends)
    if not nb:
        return [], [], [], []
    hbm = pl.BlockSpec(memory_space=pl.ANY)
    shapes = [jax.ShapeDtypeStruct(s.shape if all_to_all else (N_DEV,) + s.shape, s.dtype) for s in sends]
    return [hbm] * nb, [hbm] * nb, shapes, _exchange_sems(nb)


def _proj_blocks(x, w_ref):
    return jnp.concatenate([jnp.dot(x, w_ref[d], preferred_element_type=F32) for d in range(w_ref.shape[0])], axis=1)


def _embed_norm_proj_fwd(x, meta, rows, g, w, n1, name, wsends=()):
    n_real = N_META + x.shape[0]
    nb, _, cb = w.shape
    n = nb * cb
    tr = _row_block(rows)
    nsteps = rows // tr
    extra = len(wsends)

    def body(x_ref, meta_ref, g_ref, w_ref, *rest):
        h_ref, p1_ref, p2_ref = rest[extra:extra + 3]
        i = pl.program_id(0)
        issue, forward, finish = _two_level_gather(
            tuple(zip(rest[:extra], rest[extra + 3:2 * extra + 3])), *rest[2 * extra + 3:])
        pl.when(i == 0)(issue)
        pl.when(i == nsteps // 2)(forward)
        xw = x_ref[...]
        xw = jnp.where(i == 0, pltpu.roll(xw, N_META, 0), xw)
        row = i * tr + lax.broadcasted_iota(jnp.int32, (tr, 1), 0)
        meta_rows = jnp.concatenate([meta_ref[...], jnp.zeros((tr - N_META, D_MODEL), F32)], axis=0)
        h = jnp.where(row < N_META, meta_rows, jnp.where(row < n_real, xw, 0.0))
        h_ref[...] = h
        xn, _ = _rms_fwd(h)
        p = _proj_blocks((xn * g_ref[...]).astype(BF16), w_ref)
        p1_ref[...] = p[:, :n1]
        p2_ref[...] = p[:, n1:]
        pl.when(i == nsteps - 1)(finish)

    assert nsteps >= 3, "the three phases of the riding gather need three grid steps"
    r_in, r_out, r_shape, r_scratch = _rider_specs(wsends, False)
    window = pl.BlockSpec((pl.Element(tr, (0, rows - n_real)), pl.Element(D_MODEL)),
                          lambda i: (pl.multiple_of(jnp.maximum(i * tr - N_META, 0), SUBLANES), 0))
    return pl.pallas_call(
        body, name=name, grid=(nsteps,),
        in_specs=[window,
                  pl.BlockSpec((N_META, D_MODEL), lambda i: (0, 0)),
                  pl.BlockSpec((1, D_MODEL), lambda i: (0, 0)),
                  pl.BlockSpec((nb, D_MODEL, cb), lambda i: (0, 0, 0))] + r_in,
        out_specs=[pl.BlockSpec((tr, D_MODEL), lambda i: (i, 0)),
                   pl.BlockSpec((tr, n1), lambda i: (i, 0)),
                   pl.BlockSpec((tr, n - n1), lambda i: (i, 0))] + r_out,
        out_shape=[jax.ShapeDtypeStruct((rows, D_MODEL), F32), jax.ShapeDtypeStruct((rows, n1), F32),
                   jax.ShapeDtypeStruct((rows, n - n1), F32)] + r_shape,
        scratch_shapes=r_scratch,
        compiler_params=_cparams(("arbitrary",)),
    )(x, meta, g, w, *wsends)


def _out_proj_in_proj(a, gate, h, w_out, g, w_in, n1, name):
    rows = h.shape[0]
    nb, _, cb = w_in.shape
    n = nb * cb
    tr = _row_block(rows)

    def body(a_ref, gate_ref, h_ref, wo_ref, g_ref, wi_ref, hn_ref, p1_ref, p2_ref):
        y = (a_ref[...] * _silu(gate_ref[...])).astype(BF16)
        h_new = h_ref[...] + jnp.dot(y, wo_ref[...], preferred_element_type=F32)
        hn_ref[...] = h_new
        xn, _ = _rms_fwd(h_new)
        p = _proj_blocks((xn * g_ref[...]).astype(BF16), wi_ref)
        p1_ref[...] = p[:, :n1]
        p2_ref[...] = p[:, n1:]

    blk = pl.BlockSpec((tr, D_MODEL), lambda i: (i, 0))
    return pl.pallas_call(
        body, name=name, grid=(rows // tr,),
        in_specs=[blk, blk, blk, pl.BlockSpec((D_MODEL, D_MODEL), lambda i: (0, 0)),
                  pl.BlockSpec((1, D_MODEL), lambda i: (0, 0)), pl.BlockSpec((nb, D_MODEL, cb), lambda i: (0, 0, 0))],
        out_specs=[blk, pl.BlockSpec((tr, n1), lambda i: (i, 0)), pl.BlockSpec((tr, n - n1), lambda i: (i, 0))],
        out_shape=[jax.ShapeDtypeStruct((rows, D_MODEL), F32), jax.ShapeDtypeStruct((rows, n1), F32),
                   jax.ShapeDtypeStruct((rows, n - n1), F32)],
        compiler_params=_cparams(("parallel",)),
    )(a, gate, h, w_out, g, w_in)


def _norm_proj_bwd(h, g, w, dp1, dp2, dh_in, name, gsends=()):
    rows = h.shape[0]
    nb, _, cb = w.shape
    n1 = dp1.shape[1]
    n2 = nb * cb - n1
    tr = _row_block(rows)
    nsteps = rows // tr
    extra = len(gsends)

    def body(h_ref, g_ref, w_ref, dp1_ref, dp2_ref, dhin_ref, *rest):
        dh_ref, dw_ref, dg_ref = rest[extra:extra + 3]
        dw_acc = rest[2 * extra + 3]
        i = pl.program_id(0)
        start, wait = _rider(gsends, rest[:extra] + rest[extra + 3:2 * extra + 3] + rest[2 * extra + 4:],
                             i == 0, i == nsteps - 1, True)
        start()

        @pl.when(i == 0)
        def _():
            dw_acc[...] = jnp.zeros_like(dw_acc)
            dg_ref[...] = jnp.zeros_like(dg_ref)

        gv = g_ref[...]
        xn, r = _rms_fwd(h_ref[...])
        hn = (xn * gv).astype(BF16)
        dp = jnp.concatenate([dp1_ref[...].astype(BF16), dp2_ref[...].astype(BF16)], axis=1)
        dhn = jnp.zeros((tr, D_MODEL), F32)
        for d in range(nb):
            dpd = dp[:, d * cb:(d + 1) * cb]
            dw_acc[d] += lax.dot_general(hn, dpd, TN, preferred_element_type=F32)
            dhn = dhn + lax.dot_general(dpd, w_ref[d], NT, preferred_element_type=F32)
        dx, dg = _rms_bwd(dhn, xn, r, gv)
        dg_ref[...] += dg
        dh_ref[...] = dhin_ref[...] + dx

        @pl.when(i == nsteps - 1)
        def _():
            dw_ref[...] = dw_acc[...].astype(BF16)

        wait()

    r_in, r_out, r_shape, r_scratch = _rider_specs(gsends, True)
    wblk = pl.BlockSpec((nb, D_MODEL, cb), lambda i: (0, 0, 0))
    return pl.pallas_call(
        body, name=name, grid=(nsteps,),
        in_specs=[pl.BlockSpec((tr, D_MODEL), lambda i: (i, 0)),
                  pl.BlockSpec((1, D_MODEL), lambda i: (0, 0)),
                  wblk,
                  pl.BlockSpec((tr, n1), lambda i: (i, 0)),
                  pl.BlockSpec((tr, n2), lambda i: (i, 0)),
                  pl.BlockSpec((tr, D_MODEL), lambda i: (i, 0))] + r_in,
        out_specs=[pl.BlockSpec((tr, D_MODEL), lambda i: (i, 0)), wblk,
                   pl.BlockSpec((1, D_MODEL), lambda i: (0, 0))] + r_out,
        out_shape=[jax.ShapeDtypeStruct((rows, D_MODEL), F32),
                   jax.ShapeDtypeStruct((nb, D_MODEL, cb), BF16),
                   jax.ShapeDtypeStruct((1, D_MODEL), F32)] + r_shape,
        scratch_shapes=[pltpu.VMEM((nb, D_MODEL, cb), F32)] + r_scratch,
        compiler_params=_cparams(("arbitrary",)),
    )(h, g, w, dp1, dp2, dh_in, *gsends)


def _attn_out_bwd(o, gate, dh, w):
    rows = o.shape[0]
    tr = _row_block(rows)
    nsteps = rows // tr

    def body(o_ref, gate_ref, dh_ref, w_ref, do_ref, dgate_ref, dw_ref, delta_ref, dw_acc):
        i = pl.program_id(0)

        @pl.when(i == 0)
        def _():
            dw_acc[...] = jnp.zeros_like(dw_acc)

        ov, gv = o_ref[...], gate_ref[...]
        sg = _silu(gv)
        dhb = dh_ref[...].astype(BF16)
        dw_acc[...] += lax.dot_general((ov * sg).astype(BF16), dhb, TN, preferred_element_type=F32)
        dy = lax.dot_general(dhb, w_ref[...], NT, preferred_element_type=F32)
        do = (dy * sg).astype(BF16)
        do_ref[...] = do
        dgate_ref[...] = (dy * ov * _dsilu(gv)).astype(BF16)
        prod = do.astype(F32) * ov
        lane = lax.broadcasted_iota(jnp.int32, (tr, LANES), 1)
        per_head = jnp.zeros((tr, LANES), F32)
        for hd in range(HEADS):
            dsum = jnp.sum(prod[:, hd * V_HEAD:(hd + 1) * V_HEAD], axis=1, keepdims=True)
            per_head = jnp.where(lane == hd, dsum, per_head)
        delta_t = per_head.T
        for hd in range(HEADS):
            delta_ref[hd, 0] = delta_t[hd:hd + 1, :]

        @pl.when(i == nsteps - 1)
        def _():
            dw_ref[...] = dw_acc[...].astype(BF16)

    blk = pl.BlockSpec((tr, D_MODEL), lambda i: (i, 0))
    wblk = pl.BlockSpec((D_MODEL, D_MODEL), lambda i: (0, 0))
    return pl.pallas_call(
        body, name="a_out_bwd", grid=(nsteps,),
        in_specs=[blk, blk, blk, wblk],
        out_specs=[blk, blk, wblk, pl.BlockSpec((HEADS, 1, 1, tr), lambda i: (0, i, 0, 0))],
        out_shape=[jax.ShapeDtypeStruct((rows, D_MODEL), BF16), jax.ShapeDtypeStruct((rows, D_MODEL), BF16),
                   jax.ShapeDtypeStruct((D_MODEL, D_MODEL), BF16),
                   jax.ShapeDtypeStruct((HEADS, nsteps, 1, tr), F32)],
        scratch_shapes=[pltpu.VMEM((D_MODEL, D_MODEL), F32)],
        compiler_params=_cparams(("arbitrary",)),
    )(o, gate, dh, w)


def _rope(v, cos, sin, lane):
    swapped = jnp.where(lane < QK_ROPE // 2, pltpu.roll(v, LANES - QK_ROPE // 2, 1), pltpu.roll(v, QK_ROPE // 2, 1))
    return v * cos + swapped * sin


def _unrope(dv, cos, sin, lane):
    t = dv * sin
    swapped = jnp.where(lane < QK_ROPE // 2, pltpu.roll(t, LANES - QK_ROPE // 2, 1), pltpu.roll(t, QK_ROPE // 2, 1))
    return dv * cos + swapped


def _mla_qkv_fwd(lat, gq, gkv, wuq, wukv, cos, sin, scale):
    rows = lat.shape[0]
    tr = _row_block(rows)

    def body(lat_ref, gq_ref, gkv_ref, wuq_ref, wukv_ref, cos_ref, sin_ref, qc_ref, kc_ref, v_ref, vt_ref):
        qn, _ = _rms_fwd(lat_ref[:, :Q_LORA])
        kvn, _ = _rms_fwd(lat_ref[:, Q_LORA:LAT])
        qnb = (qn * gq_ref[...]).astype(BF16)
        kvnb = (kvn * gkv_ref[...]).astype(BF16)
        c, s = cos_ref[...], sin_ref[...]
        lane = lax.broadcasted_iota(jnp.int32, (tr, LANES), 1)
        kr = _rope(lat_ref[:, LAT:LAT_PAD], c, s, lane).astype(BF16)
        for hd in range(HEADS):
            o = hd * HEAD_PAD
            q = jnp.dot(qnb, wuq_ref[hd], preferred_element_type=F32)
            kv = jnp.dot(kvnb, wukv_ref[hd], preferred_element_type=F32)
            qc_ref[:, o:o + QK_NOPE] = (q[:, :QK_NOPE] * scale).astype(BF16)
            qc_ref[:, o + QK_NOPE:o + HEAD_PAD] = (_rope(q[:, QK_NOPE:], c, s, lane) * scale).astype(BF16)
            kc_ref[:, o:o + QK_NOPE] = kv[:, :QK_NOPE].astype(BF16)
            kc_ref[:, o + QK_NOPE:o + HEAD_PAD] = kr
            vh = kv[:, QK_NOPE:]
            v_ref[:, hd * V_HEAD:(hd + 1) * V_HEAD] = vh.astype(BF16)
            vt_ref[hd, 0] = vh.T.astype(BF16)

    full = lambda shape: pl.BlockSpec(shape, lambda i: (0,) * len(shape))
    rowb = lambda n: pl.BlockSpec((tr, n), lambda i: (i, 0))
    return pl.pallas_call(
        body, name="mla_qkv_fwd", grid=(rows // tr,),
        in_specs=[rowb(LAT_PAD), full((1, Q_LORA)), full((1, KV_LORA)), full((HEADS, Q_LORA, HEAD_PAD)),
                  full((HEADS, KV_LORA, HEAD_PAD)), rowb(LANES), rowb(LANES)],
        out_specs=[rowb(HEADS * HEAD_PAD), rowb(HEADS * HEAD_PAD), rowb(HEADS * V_HEAD),
                   pl.BlockSpec((HEADS, 1, V_HEAD, tr), lambda i: (0, i, 0, 0))],
        out_shape=[jax.ShapeDtypeStruct((rows, HEADS * HEAD_PAD), BF16),
                   jax.ShapeDtypeStruct((rows, HEADS * HEAD_PAD), BF16),
                   jax.ShapeDtypeStruct((rows, HEADS * V_HEAD), BF16),
                   jax.ShapeDtypeStruct((HEADS, rows // tr, V_HEAD, tr), BF16)],
        compiler_params=_cparams(("parallel",)),
    )(lat, gq, gkv, wuq, wukv, cos, sin)


def _mla_qkv_bwd(lat, gq, gkv, wuq, wukv, cos, sin, dqc, dkc, dv, scale):
    rows = lat.shape[0]
    tr = _row_block(rows)
    nsteps = rows // tr

    def body(lat_ref, gq_ref, gkv_ref, wuq_ref, wukv_ref, cos_ref, sin_ref, dqc_ref, dkc_ref, dv_ref,
             dlat_ref, dwuq_out, dwukv_out, dgq_ref, dgkv_ref, dwuq_ref, dwukv_ref):
        @pl.when(pl.program_id(0) == 0)
        def _():
            dwuq_ref[...] = jnp.zeros_like(dwuq_ref)
            dwukv_ref[...] = jnp.zeros_like(dwukv_ref)
            dgq_ref[...] = jnp.zeros_like(dgq_ref)
            dgkv_ref[...] = jnp.zeros_like(dgkv_ref)

        c, s = cos_ref[...], sin_ref[...]
        lane = lax.broadcasted_iota(jnp.int32, (tr, LANES), 1)
        gqv, gkvv = gq_ref[...], gkv_ref[...]
        qn, rq = _rms_fwd(lat_ref[:, :Q_LORA])
        kvn, rkv = _rms_fwd(lat_ref[:, Q_LORA:LAT])
        qnb = (qn * gqv).astype(BF16)
        kvnb = (kvn * gkvv).astype(BF16)
        dkr = jnp.zeros((tr, LANES), F32)
        dqn = jnp.zeros((tr, Q_LORA), F32)
        dkvn = jnp.zeros((tr, KV_LORA), F32)
        for hd in range(HEADS):
            o = hd * HEAD_PAD
            dq = jnp.concatenate(
                [dqc_ref[:, o:o + QK_NOPE],
                 _unrope(dqc_ref[:, o + QK_NOPE:o + HEAD_PAD].astype(F32), c, s, lane).astype(BF16)], axis=1)
            dkv = jnp.concatenate([dkc_ref[:, o:o + QK_NOPE], dv_ref[:, hd * V_HEAD:(hd + 1) * V_HEAD]], axis=1)
            dkr = dkr + dkc_ref[:, o + QK_NOPE:o + HEAD_PAD].astype(F32)
            dwuq_ref[hd] += scale * lax.dot_general(qnb, dq, TN, preferred_element_type=F32)
            dwukv_ref[hd] += lax.dot_general(kvnb, dkv, TN, preferred_element_type=F32)
            dqn = dqn + lax.dot_general(dq, wuq_ref[hd], NT, preferred_element_type=F32)
            dkvn = dkvn + lax.dot_general(dkv, wukv_ref[hd], NT, preferred_element_type=F32)
        dqn = scale * dqn
        dqlat, dgq = _rms_bwd(dqn, qn, rq, gqv)
        dkvlat, dgkv = _rms_bwd(dkvn, kvn, rkv, gkvv)
        dgq_ref[...] += dgq
        dgkv_ref[...] += dgkv
        dlat_ref[:, :Q_LORA] = dqlat.astype(BF16)
        dlat_ref[:, Q_LORA:LAT] = dkvlat.astype(BF16)
        dlat_ref[:, LAT:LAT_PAD] = _unrope(dkr, c, s, lane).astype(BF16)

        @pl.when(pl.program_id(0) == nsteps - 1)
        def _():
            dwuq_out[...] = dwuq_ref[...].astype(BF16)
            dwukv_out[...] = dwukv_ref[...].astype(BF16)

    full = lambda shape: pl.BlockSpec(shape, lambda i: (0,) * len(shape))
    rowb = lambda n: pl.BlockSpec((tr, n), lambda i: (i, 0))
    return pl.pallas_call(
        body, name="mla_qkv_bwd", grid=(nsteps,),
        in_specs=[rowb(LAT_PAD), full((1, Q_LORA)), full((1, KV_LORA)), full((HEADS, Q_LORA, HEAD_PAD)),
                  full((HEADS, KV_LORA, HEAD_PAD)), rowb(LANES), rowb(LANES),
                  rowb(HEADS * HEAD_PAD), rowb(HEADS * HEAD_PAD), rowb(HEADS * V_HEAD)],
        out_specs=[rowb(LAT_PAD), full((HEADS, Q_LORA, HEAD_PAD)), full((HEADS, KV_LORA, HEAD_PAD)),
                   full((1, Q_LORA)), full((1, KV_LORA))],
        out_shape=[jax.ShapeDtypeStruct((rows, LAT_PAD), BF16),
                   jax.ShapeDtypeStruct((HEADS, Q_LORA, HEAD_PAD), BF16),
                   jax.ShapeDtypeStruct((HEADS, KV_LORA, HEAD_PAD), BF16),
                   jax.ShapeDtypeStruct((1, Q_LORA), F32),
                   jax.ShapeDtypeStruct((1, KV_LORA), F32)],
        scratch_shapes=[pltpu.VMEM((HEADS, Q_LORA, HEAD_PAD), F32), pltpu.VMEM((HEADS, KV_LORA, HEAD_PAD), F32)],
        compiler_params=_cparams(("arbitrary",)),
    )(lat, gq, gkv, wuq, wukv, cos, sin, dqc, dkc, dv)


ATTN_UNROLL = 4
ATTN_UNROLL_BWD = 4
ATTN_HEADS = 2
ATTN_HEADS_FWD = 4


def _causal_mask_t(t):
    key = lax.broadcasted_iota(jnp.int32, (t, t), 0)
    query = lax.broadcasted_iota(jnp.int32, (t, t), 1)
    return key <= query


def _attn_fwd(qc, kc, vt, wsends):
    rows = qc.shape[0]
    t = _row_block(rows)
    nblk = rows // t
    nw = len(wsends)

    def body(q_ref, k_ref, vt_ref, *rest):
        o_ref, lse_ref = rest[nw:nw + 2]
        m_ref, l_ref, acc_ref, st_a, st_b = rest[2 * nw + 2:2 * nw + 7]
        i = pl.program_id(1)
        start, wait = _rider(wsends, rest[:nw] + rest[nw + 2:2 * nw + 2] + rest[2 * nw + 7:],
                             jnp.logical_and(pl.program_id(0) == 0, i == 0),
                             jnp.logical_and(pl.program_id(0) == HEADS // ATTN_HEADS_FWD - 1, i == nblk - 1), False)
        start()

        m_ref[...] = jnp.full_like(m_ref, MASK_VALUE)
        l_ref[...] = jnp.zeros_like(l_ref)
        acc_ref[...] = jnp.zeros_like(acc_ref)
        heads = range(ATTN_HEADS_FWD)
        qs = [q_ref[:, hh * HEAD_PAD:(hh + 1) * HEAD_PAD] for hh in heads]

        def scores(j, hh, st_ref):
            rs = pl.ds(pl.multiple_of(j * t, t), t)
            st_ref[hh] = lax.dot_general(k_ref[rs, hh * HEAD_PAD:(hh + 1) * HEAD_PAD], qs[hh], NT,
                                         preferred_element_type=F32)

        def consume(j, hh, st_ref, masked):
            st = st_ref[hh]
            if masked:
                st = jnp.where(_causal_mask_t(t), st, MASK_VALUE)
            m_prev = m_ref[hh]
            m_new = jnp.maximum(m_prev, jnp.max(st, axis=0, keepdims=True))
            alpha = jnp.exp(m_prev - m_new)
            pt = jnp.exp(st - m_new)
            l_ref[hh] = alpha * l_ref[hh] + jnp.sum(pt, axis=0, keepdims=True)
            acc_ref[hh] = alpha * acc_ref[hh] + jnp.dot(vt_ref[hh, j], pt.astype(BF16), preferred_element_type=F32)
            m_ref[hh] = m_new

        bufs = (st_a, st_b)

        def step(j, parity, issue_next, masked):
            if issue_next:
                for hh in heads:
                    scores(j + 1, hh, bufs[1 - parity])
            for hh in heads:
                consume(j, hh, bufs[parity], masked)

        for hh in heads:
            scores(0, hh, st_a)

        def trip(it, carry):
            for u in range(ATTN_UNROLL):
                step(it * ATTN_UNROLL + u, u % 2, True, False)
            return carry

        trips = i // ATTN_UNROLL
        lax.fori_loop(0, trips, trip, 0)
        j0 = trips * ATTN_UNROLL
        for left in range(1, ATTN_UNROLL + 1):
            @pl.when(i + 1 - j0 == left)
            def _(left=left):
                for u in range(left):
                    step(j0 + u, u % 2, u < left - 1, u == left - 1)

        for hh in heads:
            o_ref[:, hh * V_HEAD:(hh + 1) * V_HEAD] = (acc_ref[hh] / l_ref[hh]).T
            lse_ref[hh, 0] = m_ref[hh] + jnp.log(l_ref[hh])
        wait()

    r_in, r_out, r_shape, r_scratch = _rider_specs(wsends, False)
    nh = ATTN_HEADS_FWD
    return pl.pallas_call(
        body, name="attn_fwd", grid=(HEADS // nh, nblk),
        in_specs=[pl.BlockSpec((t, nh * HEAD_PAD), lambda g, i: (i, g)),
                  pl.BlockSpec((rows, nh * HEAD_PAD), lambda g, i: (0, g)),
                  pl.BlockSpec((nh, nblk, V_HEAD, t), lambda g, i: (g, 0, 0, 0))] + r_in,
        out_specs=[pl.BlockSpec((t, nh * V_HEAD), lambda g, i: (i, g)),
                   pl.BlockSpec((nh, 1, 1, t), lambda g, i: (g, i, 0, 0))] + r_out,
        out_shape=[jax.ShapeDtypeStruct((rows, HEADS * V_HEAD), F32),
                   jax.ShapeDtypeStruct((HEADS, nblk, 1, t), F32)] + r_shape,
        scratch_shapes=[pltpu.VMEM((nh, 1, t), F32), pltpu.VMEM((nh, 1, t), F32), pltpu.VMEM((nh, V_HEAD, t), F32),
                        pltpu.VMEM((nh, t, t), F32), pltpu.VMEM((nh, t, t), F32)] + r_scratch,
        compiler_params=_cparams(("arbitrary", "arbitrary")),
    )(qc, kc, vt, *wsends)


def _attn_bwd(qc, kc, v, lse, delta, do, gsends):
    rows = qc.shape[0]
    t = _row_block(rows)
    nblk = rows // t
    ng = len(gsends)

    def body(q_ref, k_ref, v_ref, lse_ref, delta_ref, do_ref, *rest):
        dq_ref, dk_ref, dv_ref = rest[ng:ng + 3]
        dq_acc, dk_acc, dv_acc, st_a, dp_a, st_b, dp_b = rest[2 * ng + 3:2 * ng + 10]
        j = pl.program_id(1)
        start, wait = _rider(gsends, rest[:ng] + rest[ng + 3:2 * ng + 3] + rest[2 * ng + 10:],
                             jnp.logical_and(pl.program_id(0) == 0, j == 0),
                             jnp.logical_and(pl.program_id(0) == HEADS // ATTN_HEADS - 1, j == nblk - 1), True)
        start()

        @pl.when(j == 0)
        def _():
            dq_acc[...] = jnp.zeros_like(dq_acc)

        dk_acc[...] = jnp.zeros_like(dk_acc)
        dv_acc[...] = jnp.zeros_like(dv_acc)
        heads = range(ATTN_HEADS)
        qk = lambda hh: slice(hh * HEAD_PAD, (hh + 1) * HEAD_PAD)
        vo = lambda hh: slice(hh * V_HEAD, (hh + 1) * V_HEAD)
        ks = [k_ref[:, qk(hh)] for hh in heads]
        vs = [v_ref[:, vo(hh)] for hh in heads]

        def products(i, hh, st_ref, dp_ref):
            rs = pl.ds(pl.multiple_of(i * t, t), t)
            st_ref[hh] = lax.dot_general(ks[hh], q_ref[rs, qk(hh)], NT, preferred_element_type=F32)
            dp_ref[hh] = lax.dot_general(vs[hh], do_ref[rs, vo(hh)], NT, preferred_element_type=F32)

        def consume(i, hh, st_ref, dp_ref):
            rs = pl.ds(pl.multiple_of(i * t, t), t)
            q = q_ref[rs, qk(hh)]
            dob = do_ref[rs, vo(hh)]
            st = jnp.where(jnp.logical_or(_causal_mask_t(t), i != j), st_ref[hh], MASK_VALUE)
            pt = jnp.exp(st - lse_ref[hh, i])
            dv_acc[hh] += jnp.dot(pt.astype(BF16), dob, preferred_element_type=F32)
            dst = (pt * (dp_ref[hh] - delta_ref[hh, i])).astype(BF16)
            dk_acc[hh] += jnp.dot(dst, q, preferred_element_type=F32)
            dq_acc[hh, rs, :] += lax.dot_general(dst, ks[hh], TN, preferred_element_type=F32)

        bufs = ((st_a, dp_a), (st_b, dp_b))

        def step(i, parity, issue_next):
            if issue_next:
                for hh in heads:
                    products(i + 1, hh, *bufs[1 - parity])
            for hh in heads:
                consume(i, hh, *bufs[parity])

        for hh in heads:
            products(j, hh, st_a, dp_a)

        def trip(it, carry):
            for u in range(ATTN_UNROLL_BWD):
                step(j + it * ATTN_UNROLL_BWD + u, u % 2, True)
            return carry

        trips = (nblk - 1 - j) // ATTN_UNROLL_BWD
        lax.fori_loop(0, trips, trip, 0)
        i0 = j + trips * ATTN_UNROLL_BWD
        for left in range(1, ATTN_UNROLL_BWD + 1):
            @pl.when(nblk - i0 == left)
            def _(left=left):
                for u in range(left):
                    step(i0 + u, u % 2, u < left - 1)

        for hh in heads:
            dk_ref[:, qk(hh)] = dk_acc[hh].astype(BF16)
            dv_ref[:, vo(hh)] = dv_acc[hh].astype(BF16)

        @pl.when(j == nblk - 1)
        def _():
            for hh in heads:
                dq_ref[:, qk(hh)] = dq_acc[hh].astype(BF16)

        wait()

    nh = ATTN_HEADS
    stat = pl.BlockSpec((nh, nblk, 1, t), lambda g, j: (g, 0, 0, 0))
    r_in, r_out, r_shape, r_scratch = _rider_specs(gsends, True)
    return pl.pallas_call(
        body, name="attn_bwd", grid=(HEADS // nh, nblk),
        in_specs=[pl.BlockSpec((rows, nh * HEAD_PAD), lambda g, j: (0, g)),
                  pl.BlockSpec((t, nh * HEAD_PAD), lambda g, j: (j, g)),
                  pl.BlockSpec((t, nh * V_HEAD), lambda g, j: (j, g)),
                  stat, stat,
                  pl.BlockSpec((rows, nh * V_HEAD), lambda g, j: (0, g))] + r_in,
        out_specs=[pl.BlockSpec((rows, nh * HEAD_PAD), lambda g, j: (0, g)),
                   pl.BlockSpec((t, nh * HEAD_PAD), lambda g, j: (j, g)),
                   pl.BlockSpec((t, nh * V_HEAD), lambda g, j: (j, g))] + r_out,
        out_shape=[jax.ShapeDtypeStruct((rows, HEADS * HEAD_PAD), BF16),
                   jax.ShapeDtypeStruct((rows, HEADS * HEAD_PAD), BF16),
                   jax.ShapeDtypeStruct((rows, HEADS * V_HEAD), BF16)] + r_shape,
        scratch_shapes=[pltpu.VMEM((nh, rows, HEAD_PAD), F32), pltpu.VMEM((nh, t, HEAD_PAD), F32),
                        pltpu.VMEM((nh, t, V_HEAD), F32)] + [pltpu.VMEM((nh, t, t), F32)] * 4 + r_scratch,
        compiler_params=_cparams(("arbitrary", "arbitrary")),
    )(qc, kc, v, lse, delta, do, *gsends)


def _shift_down(prev_tile, x, k):
    xx = jnp.concatenate([prev_tile, x], axis=0)
    return pltpu.roll(xx, k, 0)[SUBLANES:]


def _shift_up(x, next_tile, k):
    n = x.shape[0]
    xx = jnp.concatenate([x, next_tile], axis=0)
    return pltpu.roll(xx, n + SUBLANES - k, 0)[:n]


def _lru_gates(u, u_prev, cw_ref, cb_ref, wrg_ref, brg_ref, wig_ref, big_ref, lam_ref, first_block):
    taps = [_shift_down(u_prev, u, CONV_WIDTH - 1 - j) if j < CONV_WIDTH - 1 else u for j in range(CONV_WIDTH)]
    uc = cb_ref[...] + taps[0] * cw_ref[0:1, :]
    for j in range(1, CONV_WIDTH):
        uc = uc + taps[j] * cw_ref[j:j + 1, :]
    ub = uc.astype(BF16)
    zr = jnp.concatenate([jnp.dot(ub[:, g * LRU_BLOCK:(g + 1) * LRU_BLOCK], wrg_ref[g], preferred_element_type=F32)
                          for g in range(LRU_BLOCKS)], axis=1) + brg_ref[...]
    zi = jnp.concatenate([jnp.dot(ub[:, g * LRU_BLOCK:(g + 1) * LRU_BLOCK], wig_ref[g], preferred_element_type=F32)
                          for g in range(LRU_BLOCKS)], axis=1) + big_ref[...]
    r = jax.nn.sigmoid(zr)
    ig = jax.nn.sigmoid(zi)
    sp = _softplus_neg(lam_ref[...])
    log_a = (-LRU_C) * r * sp
    a = jnp.exp(log_a)
    m2 = -_expm1_neg(2.0 * log_a)
    mult_raw = m2 * lax.rsqrt(jnp.maximum(m2, 1e-30))
    row = lax.broadcasted_iota(jnp.int32, u.shape, 0)
    is_start = jnp.logical_and(first_block, row == 0)
    mult = jnp.where(is_start, 1.0, mult_raw)
    return dict(taps=taps, uc=uc, ub=ub, r=r, ig=ig, sp=sp, a=a, mult=mult, mult_raw=mult_raw, is_start=is_start)


def _rglru_fwd(u, cw, cb, wrg, brg, wig, big, lam):
    rows = u.shape[0]
    tb = _row_block(rows)

    def body(u_ref, cw_ref, cb_ref, wrg_ref, brg_ref, wig_ref, big_ref, lam_ref, hs_ref, utail, hcar, a_s, b_s):
        i = pl.program_id(0)

        @pl.when(i == 0)
        def _():
            utail[...] = jnp.zeros_like(utail)
            hcar[...] = jnp.zeros_like(hcar)

        u = u_ref[...]
        gt = _lru_gates(u, utail[...], cw_ref, cb_ref, wrg_ref, brg_ref, wig_ref, big_ref, lam_ref, i == 0)
        a_s[...] = gt["a"]
        b_s[...] = gt["mult"] * (gt["ig"] * gt["uc"])
        row8 = lax.broadcasted_iota(jnp.int32, (SUBLANES, LRU_WIDTH), 0)

        def tile(tix, carry):
            rs = pl.ds(pl.multiple_of(tix * SUBLANES, SUBLANES), SUBLANES)
            av, bv = a_s[rs, :], b_s[rs, :]
            for k in (1, 2, 4):
                keep = row8 >= k
                bv = jnp.where(keep, av * pltpu.roll(bv, k, 0) + bv, bv)
                av = jnp.where(keep, av * pltpu.roll(av, k, 0), av)
            h8 = av * carry + bv
            hs_ref[rs, :] = h8
            return jnp.broadcast_to(h8[SUBLANES - 1:SUBLANES, :], (SUBLANES, LRU_WIDTH))

        hcar[...] = lax.fori_loop(0, tb // SUBLANES, tile, hcar[...])
        utail[...] = u[tb - SUBLANES:, :]

    full2 = lambda shape: pl.BlockSpec(shape, lambda i: (0, 0))
    full3 = lambda shape: pl.BlockSpec(shape, lambda i: (0, 0, 0))
    blk = pl.BlockSpec((tb, LRU_WIDTH), lambda i: (i, 0))
    return pl.pallas_call(
        body, name="rglru_fwd", grid=(rows // tb,),
        in_specs=[blk, full2((CONV_WIDTH, LRU_WIDTH)), full2((1, LRU_WIDTH)),
                  full3((LRU_BLOCKS, LRU_BLOCK, LRU_BLOCK)), full2((1, LRU_WIDTH)),
                  full3((LRU_BLOCKS, LRU_BLOCK, LRU_BLOCK)), full2((1, LRU_WIDTH)), full2((1, LRU_WIDTH))],
        out_specs=blk,
        out_shape=jax.ShapeDtypeStruct((rows, LRU_WIDTH), F32),
        scratch_shapes=[pltpu.VMEM((SUBLANES, LRU_WIDTH), F32), pltpu.VMEM((SUBLANES, LRU_WIDTH), F32),
                        pltpu.VMEM((tb, LRU_WIDTH), F32), pltpu.VMEM((tb, LRU_WIDTH), F32)],
        compiler_params=_cparams(("arbitrary",)),
    )(u, cw, cb, wrg, brg, wig, big, lam)


def _rglru_bwd(u, hs, dhs, cw, cb, wrg, brg, wig, big, lam):
    rows = u.shape[0]
    tb = _row_block(rows)
    nblk = rows // tb
    tiles = tb // SUBLANES

    def body(u_ref, up_ref, hs_ref, hp_ref, dhs_ref, cw_ref, cb_ref, wrg_ref, brg_ref, wig_ref, big_ref, lam_ref,
             du_ref, dcw_ref, dcb_ref, dwrg_ref, dbrg_ref, dwig_ref, dbig_ref, dlam_ref,
             gcar, duc_head, a_s, b_s, g_s, dsp_acc):
        step = pl.program_id(0)
        blk_ix = nblk - 1 - step

        @pl.when(step == 0)
        def _():
            for ref in (dcw_ref, dcb_ref, dwrg_ref, dbrg_ref, dwig_ref, dbig_ref, gcar, duc_head, dsp_acc):
                ref[...] = jnp.zeros_like(ref)

        first = blk_ix == 0
        u = u_ref[...]
        u_prev = jnp.where(first, 0.0, up_ref[...])
        h_prev_tile = jnp.where(first, 0.0, hp_ref[...])
        gt = _lru_gates(u, u_prev, cw_ref, cb_ref, wrg_ref, brg_ref, wig_ref, big_ref, lam_ref, first)
        a, r, ig, uc, mult = gt["a"], gt["r"], gt["ig"], gt["uc"], gt["mult"]
        dhs_v = dhs_ref[...]

        a_s[...] = a
        b_s[...] = a * dhs_v
        row8 = lax.broadcasted_iota(jnp.int32, (SUBLANES, LRU_WIDTH), 0)

        def tile(tix, carry):
            rs = pl.ds(pl.multiple_of((tiles - 1 - tix) * SUBLANES, SUBLANES), SUBLANES)
            av, bv = a_s[rs, :], b_s[rs, :]
            for k in (1, 2, 4):
                keep = row8 < SUBLANES - k
                bv = jnp.where(keep, av * pltpu.roll(bv, SUBLANES - k, 0) + bv, bv)
                av = jnp.where(keep, av * pltpu.roll(av, SUBLANES - k, 0), av)
            g8 = av * carry + bv
            g_s[rs, :] = g8
            return jnp.broadcast_to(g8[0:1, :], (SUBLANES, LRU_WIDTH))

        g_next = gcar[...]
        gcar[...] = lax.fori_loop(0, tiles, tile, g_next)
        g = dhs_v + _shift_up(g_s[...], g_next, 1)

        h_prev = _shift_down(h_prev_tile, hs_ref[...], 1)
        da = g * h_prev
        iu = ig * uc
        dmult = jnp.where(gt["is_start"], 0.0, g * iu)
        d_ig = g * mult * uc
        duc = g * mult * ig
        dlog_a = da * a - dmult * (a * a) / gt["mult_raw"]
        dzr = (dlog_a * ((-LRU_C) * gt["sp"])) * r * (1.0 - r)
        dsp_acc[...] += jnp.sum(dlog_a * ((-LRU_C) * r), axis=0, keepdims=True)
        dzi = d_ig * ig * (1.0 - ig)
        dbrg_ref[...] += jnp.sum(dzr, axis=0, keepdims=True)
        dbig_ref[...] += jnp.sum(dzi, axis=0, keepdims=True)
        dzr_b, dzi_b = dzr.astype(BF16), dzi.astype(BF16)
        ub = gt["ub"]
        duc_parts = []
        for gi in range(LRU_BLOCKS):
            cs = slice(gi * LRU_BLOCK, (gi + 1) * LRU_BLOCK)
            dwrg_ref[gi] += lax.dot_general(ub[:, cs], dzr_b[:, cs], TN, preferred_element_type=F32)
            dwig_ref[gi] += lax.dot_general(ub[:, cs], dzi_b[:, cs], TN, preferred_element_type=F32)
            duc_parts.append(lax.dot_general(dzr_b[:, cs], wrg_ref[gi], NT, preferred_element_type=F32)
                             + lax.dot_general(dzi_b[:, cs], wig_ref[gi], NT, preferred_element_type=F32))
        duc = duc + jnp.concatenate(duc_parts, axis=1)

        dcb_ref[...] += jnp.sum(duc, axis=0, keepdims=True)
        taps = gt["taps"]
        for jt in range(CONV_WIDTH):
            dcw_ref[jt:jt + 1, :] += jnp.sum(duc * taps[jt], axis=0, keepdims=True)
        head = duc_head[...]
        du = duc * cw_ref[CONV_WIDTH - 1:CONV_WIDTH, :]
        for jt in range(CONV_WIDTH - 1):
            du = du + _shift_up(duc, head, CONV_WIDTH - 1 - jt) * cw_ref[jt:jt + 1, :]
        du_ref[...] = du.astype(BF16)
        duc_head[...] = duc[:SUBLANES, :]

        @pl.when(step == nblk - 1)
        def _():
            dlam_ref[...] = -dsp_acc[...] * jax.nn.sigmoid(-lam_ref[...])

    full2 = lambda shape: pl.BlockSpec(shape, lambda s: (0, 0))
    full3 = lambda shape: pl.BlockSpec(shape, lambda s: (0, 0, 0))
    blk = pl.BlockSpec((tb, LRU_WIDTH), lambda s: (nblk - 1 - s, 0))
    prev_tile = pl.BlockSpec((SUBLANES, LRU_WIDTH), lambda s: (jnp.maximum((nblk - 1 - s) * tiles - 1, 0), 0))
    wshape = (LRU_BLOCKS, LRU_BLOCK, LRU_BLOCK)
    return pl.pallas_call(
        body, name="rglru_bwd", grid=(nblk,),
        in_specs=[blk, prev_tile, blk, prev_tile, blk, full2((CONV_WIDTH, LRU_WIDTH)), full2((1, LRU_WIDTH)),
                  full3(wshape), full2((1, LRU_WIDTH)), full3(wshape), full2((1, LRU_WIDTH)), full2((1, LRU_WIDTH))],
        out_specs=[blk, full2((CONV_WIDTH, LRU_WIDTH)), full2((1, LRU_WIDTH)), full3(wshape), full2((1, LRU_WIDTH)),
                   full3(wshape), full2((1, LRU_WIDTH)), full2((1, LRU_WIDTH))],
        out_shape=[jax.ShapeDtypeStruct((rows, LRU_WIDTH), BF16),
                   jax.ShapeDtypeStruct((CONV_WIDTH, LRU_WIDTH), F32), jax.ShapeDtypeStruct((1, LRU_WIDTH), F32),
                   jax.ShapeDtypeStruct(wshape, F32), jax.ShapeDtypeStruct((1, LRU_WIDTH), F32),
                   jax.ShapeDtypeStruct(wshape, F32), jax.ShapeDtypeStruct((1, LRU_WIDTH), F32),
                   jax.ShapeDtypeStruct((1, LRU_WIDTH), F32)],
        scratch_shapes=[pltpu.VMEM((SUBLANES, LRU_WIDTH), F32), pltpu.VMEM((SUBLANES, LRU_WIDTH), F32),
                        pltpu.VMEM((tb, LRU_WIDTH), F32), pltpu.VMEM((tb, LRU_WIDTH), F32),
                        pltpu.VMEM((tb, LRU_WIDTH), F32), pltpu.VMEM((1, LRU_WIDTH), F32)],
        compiler_params=_cparams(("arbitrary",)),
    )(u, u, hs, hs, dhs, cw, cb, wrg, brg, wig, big, lam)


def _out_proj_loss(a, gate, h, w, gf, target, n_real):
    rows = h.shape[0]
    tr = _row_block(rows)

    def body(a_ref, gate_ref, h_ref, w_ref, g_ref, t_ref, dh_ref, loss_ref, dg_ref, da_ref, dgate_ref, dw_ref,
             dw_acc):
        i = pl.program_id(0)

        @pl.when(i == 0)
        def _():
            loss_ref[...] = jnp.zeros_like(loss_ref)
            dg_ref[...] = jnp.zeros_like(dg_ref)
            dw_acc[...] = jnp.zeros_like(dw_acc)

        gv = g_ref[...]
        av, gatev = a_ref[...], gate_ref[...]
        sg = _silu(gatev)
        y = (av * sg).astype(BF16)
        xn, r = _rms_fwd(h_ref[...] + jnp.dot(y, w_ref[...], preferred_element_type=F32))
        row = i * tr + lax.broadcasted_iota(jnp.int32, (tr, 1), 0)
        live = jnp.logical_and(row >= N_META, row < n_real)
        tgt = t_ref[...]
        tgt = jnp.where(i == 0, pltpu.roll(tgt, N_META, 0), tgt)
        err = jnp.where(live, xn * gv - tgt, 0.0)
        loss_ref[...] += (0.5 / D_MODEL) * jnp.sum(jnp.sum(err * err, axis=1, keepdims=True), axis=0, keepdims=True)
        dx, dg = _rms_bwd(err * (1.0 / D_MODEL), xn, r, gv)
        dg_ref[...] += dg
        dh_ref[...] = dx
        dhb = dx.astype(BF16)
        dw_acc[...] += lax.dot_general(y, dhb, TN, preferred_element_type=F32)
        dy = lax.dot_general(dhb, w_ref[...], NT, preferred_element_type=F32)
        da_ref[...] = dy * sg
        dgate_ref[...] = (dy * av * _dsilu(gatev)).astype(BF16)

        @pl.when(i == rows // tr - 1)
        def _():
            dw_ref[...] = dw_acc[...].astype(BF16)

    blk = pl.BlockSpec((tr, D_MODEL), lambda i: (i, 0))
    wblk = pl.BlockSpec((D_MODEL, D_MODEL), lambda i: (0, 0))
    window = pl.BlockSpec((pl.Element(tr, (0, rows - n_real)), pl.Element(D_MODEL)),
                          lambda i: (pl.multiple_of(jnp.maximum(i * tr - N_META, 0), SUBLANES), 0))
    return pl.pallas_call(
        body, name="b_out_loss", grid=(rows // tr,),
        in_specs=[blk, blk, blk, wblk, pl.BlockSpec((1, D_MODEL), lambda i: (0, 0)), window],
        out_specs=[blk, pl.BlockSpec((1, 1), lambda i: (0, 0)), pl.BlockSpec((1, D_MODEL), lambda i: (0, 0)),
                   blk, blk, wblk],
        out_shape=[jax.ShapeDtypeStruct((rows, D_MODEL), F32), jax.ShapeDtypeStruct((1, 1), F32),
                   jax.ShapeDtypeStruct((1, D_MODEL), F32), jax.ShapeDtypeStruct((rows, D_MODEL), F32),
                   jax.ShapeDtypeStruct((rows, D_MODEL), BF16), jax.ShapeDtypeStruct((D_MODEL, D_MODEL), BF16)],
        scratch_shapes=[pltpu.VMEM((D_MODEL, D_MODEL), F32)],
        compiler_params=_cparams(("arbitrary",)),
    )(a, gate, h, w, gf, target)


def _my_place():
    x, y, c = lax.axis_index("x"), lax.axis_index("y"), lax.axis_index("c")
    return x, y, c, 4 * x + 2 * y + c


def _peer(x, y, c, k):
    px, py, pc = x ^ (k >> 2), y ^ ((k >> 1) & 1), c ^ (k & 1)
    return (px, py, pc), 4 * px + 2 * py + pc


def _exchange_copies(src_of, dst_ref, send_sems, recv_sems, local_sem):
    x, y, c, me = _my_place()
    copies = [pltpu.make_async_copy(src_of(me), dst_ref.at[me], local_sem)]
    for k in range(1, N_DEV):
        peer, pid = _peer(x, y, c, k)
        copies.append(pltpu.make_async_remote_copy(
            src_ref=src_of(pid), dst_ref=dst_ref.at[me], send_sem=send_sems.at[k], recv_sem=recv_sems.at[k],
            device_id=peer, device_id_type=MESH))
    return copies


def _exchange_sems(nb):
    return [pltpu.SemaphoreType.DMA((nb, N_DEV)), pltpu.SemaphoreType.DMA((nb, N_DEV)), pltpu.SemaphoreType.DMA((nb,))]


def _sum_blocks(lands, name):
    n = len(lands)

    def body(*refs):
        for land_ref, out_ref in zip(refs[:n], refs[n:]):
            acc = land_ref[0].astype(F32)
            for d in range(1, N_DEV):
                acc = acc + land_ref[d].astype(F32)
            out_ref[...] = acc

    return pl.pallas_call(
        body, name=name, out_shape=[jax.ShapeDtypeStruct(l.shape[1:], F32) for l in lands],
        compiler_params=pltpu.CompilerParams(vmem_limit_bytes=VMEM_LIMIT),
    )(*lands)


def _two_level_gather(parts, send_sems, recv_sems, local_sems):
    x, y, c, _ = _my_place()
    me, sibling = (x, y, c), (x, y, 1 - c)
    chips = [(1 - x, y), (x, 1 - y), (1 - x, 1 - y)]
    nparts = range(len(parts))

    def slot(dst, place):
        return dst.at[4 * place[0] + 2 * place[1] + place[2]]

    def copy(part, k, block, to, first_hand=False):
        src, dst = parts[part]
        return pltpu.make_async_remote_copy(
            src_ref=src if first_hand else slot(dst, block), dst_ref=slot(dst, block),
            send_sem=send_sems.at[part, k], recv_sem=recv_sems.at[part, k], device_id=to, device_id_type=MESH)

    own = lambda: [pltpu.make_async_copy(src, slot(dst, me), local_sems.at[part])
                   for part, (src, dst) in enumerate(parts)]
    first = lambda: [cp for part in nparts for cp in
                     [copy(part, 0, me, sibling, True)] + [copy(part, 1 + j, me, (*chip, c), True)
                                                           for j, chip in enumerate(chips)]]
    passed = lambda: [copy(part, 4 + j, (*chip, c), sibling) for j, chip in enumerate(chips) for part in nparts]

    def issue():
        for cp in own() + first():
            cp.start()

    def forward():
        for j, chip in enumerate(chips):
            for part in nparts:
                copy(part, 1 + j, (*chip, c), me).wait_recv()
                copy(part, 4 + j, (*chip, c), sibling).start()

    def finish():
        for part in nparts:
            copy(part, 0, sibling, me).wait_recv()
            for j, chip in enumerate(chips):
                copy(part, 4 + j, (*chip, 1 - c), me).wait_recv()
        for cp in first() + passed():
            cp.wait_send()
        for cp in own():
            cp.wait()

    return issue, forward, finish


def _all_gather(big, small):
    def body(big_ref, small_ref, obig_ref, osmall_ref, send_sems, recv_sems, local_sems):
        issue, forward, finish = _two_level_gather(((big_ref, obig_ref), (small_ref, osmall_ref)), send_sems,
                                                   recv_sems, local_sems)
        issue()
        forward()
        finish()

    hbm = pl.BlockSpec(memory_space=pl.ANY)
    return pl.pallas_call(
        body, name="weight_all_gather",
        in_specs=[hbm, hbm], out_specs=[hbm, hbm],
        out_shape=[jax.ShapeDtypeStruct((N_DEV,) + big.shape, BF16), jax.ShapeDtypeStruct((N_DEV,) + small.shape, F32)],
        scratch_shapes=[pltpu.SemaphoreType.DMA((2, N_DEV)), pltpu.SemaphoreType.DMA((2, N_DEV)),
                        pltpu.SemaphoreType.DMA((2,))],
        compiler_params=pltpu.CompilerParams(has_side_effects=True),
    )(big, small)


GRAD_CHUNK = 32


def _grad_exchange(gbig, rep):
    n, width = gbig.shape[1:]
    nrep = rep.shape[0]
    n_chips = N_DEV // 2

    def body(gbig_ref, rep_ref, out_ref, orep_ref, pre, stage, got, own_sum, land_rep, send_sems, recv_sems,
             local_sem):
        x, y, c, me = _my_place()
        my_chip = 2 * x + y
        sibling = (x, y, 1 - c)

        local = pltpu.make_async_copy(rep_ref, land_rep.at[me], local_sem.at[0])
        local.start()
        rep_copies = []
        for k in range(1, N_DEV):
            peer, _ = _peer(x, y, c, k)
            rep_copies.append(pltpu.make_async_remote_copy(
                src_ref=rep_ref, dst_ref=land_rep.at[me], send_sem=send_sems.at[6 + k], recv_sem=recv_sems.at[6 + k],
                device_id=peer, device_id_type=MESH))
        swaps = [pltpu.make_async_remote_copy(
            src_ref=gbig_ref.at[2 * q + (1 - c)], dst_ref=pre.at[q], send_sem=send_sems.at[q], recv_sem=recv_sems.at[q],
            device_id=sibling, device_id_type=MESH) for q in range(n_chips)]
        for cp in rep_copies + swaps:
            cp.start()
        for cp in swaps:
            cp.wait_recv()

        def pair_sums(ci, carry):
            rs = pl.ds(pl.multiple_of(ci * GRAD_CHUNK, GRAD_CHUNK), GRAD_CHUNK)
            for q in range(n_chips):
                stage[q, rs, :] = (gbig_ref[2 * q + c, rs, :].astype(F32) + pre[q, rs, :].astype(F32)).astype(BF16)
            own_sum[rs, :] = gbig_ref[me, rs, :].astype(F32) + pre[my_chip, rs, :].astype(F32)
            return carry

        lax.fori_loop(0, n // GRAD_CHUNK, pair_sums, 0)

        hops = []
        for rel in range(1, n_chips):
            qx, qy = x ^ (rel >> 1), y ^ (rel & 1)
            hops.append(pltpu.make_async_remote_copy(
                src_ref=stage.at[2 * qx + qy], dst_ref=got.at[my_chip], send_sem=send_sems.at[3 + rel],
                recv_sem=recv_sems.at[3 + rel], device_id=(qx, qy, c), device_id_type=MESH))
        for cp in hops:
            cp.start()
        for cp in hops:
            cp.wait_recv()

        def chip_sums(ci, carry):
            rs = pl.ds(pl.multiple_of(ci * GRAD_CHUNK, GRAD_CHUNK), GRAD_CHUNK)
            mine = own_sum[rs, :]
            acc = jnp.where(my_chip == 0, mine, got[0, rs, :].astype(F32))
            for q in range(1, n_chips):
                acc = acc + jnp.where(my_chip == q, mine, got[q, rs, :].astype(F32))
            out_ref[rs, :] = acc
            return carry

        lax.fori_loop(0, n // GRAD_CHUNK, chip_sums, 0)

        for cp in rep_copies:
            cp.wait_recv()
        local.wait()
        acc = land_rep[0]
        for d in range(1, N_DEV):
            acc = acc + land_rep[d]
        orep_ref[...] = acc
        for cp in swaps + hops + rep_copies:
            cp.wait_send()

    return pl.pallas_call(
        body, name="grad_exchange",
        in_specs=[pl.BlockSpec(memory_space=pltpu.VMEM), pl.BlockSpec(memory_space=pltpu.VMEM)],
        out_specs=[pl.BlockSpec(memory_space=pltpu.VMEM), pl.BlockSpec(memory_space=pltpu.VMEM)],
        out_shape=[jax.ShapeDtypeStruct((n, width), F32), jax.ShapeDtypeStruct((nrep, LANES), F32)],
        scratch_shapes=[pltpu.VMEM((n_chips, n, width), BF16), pltpu.VMEM((n_chips, n, width), BF16),
                        pltpu.VMEM((n_chips, n, width), BF16), pltpu.VMEM((n, width), F32),
                        pltpu.VMEM((N_DEV, nrep, LANES), F32),
                        pltpu.SemaphoreType.DMA((2 * N_DEV - 2,)), pltpu.SemaphoreType.DMA((2 * N_DEV - 2,)),
                        pltpu.SemaphoreType.DMA((1,))],
        compiler_params=pltpu.CompilerParams(vmem_limit_bytes=VMEM_LIMIT, has_side_effects=True),
    )(gbig, rep)


def _adamw_all(ws, gs, ms, vs):
    n = len(ws)

    def body(*refs):
        w_refs, g_refs, m_refs, v_refs = refs[0:n], refs[n:2 * n], refs[2 * n:3 * n], refs[3 * n:4 * n]
        d_refs, nm_refs, nv_refs = refs[4 * n:5 * n], refs[5 * n:6 * n], refs[6 * n:7 * n]
        for w_ref, g_ref, m_ref, v_ref, d_ref, nm_ref, nv_ref in zip(w_refs, g_refs, m_refs, v_refs, d_refs, nm_refs, nv_refs):
            g = g_ref[...]
            m = ADAM_B1 * m_ref[...] + (1.0 - ADAM_B1) * g
            v = ADAM_B2 * v_ref[...] + (1.0 - ADAM_B2) * jnp.square(g)
            m_hat = m / (1.0 - ADAM_B1 ** ADAM_STEP)
            v_hat = v / (1.0 - ADAM_B2 ** ADAM_STEP)
            d_ref[...] = -ADAM_LR * (m_hat / (jnp.sqrt(v_hat) + ADAM_EPS) + ADAM_WD * w_ref[...])
            nm_ref[...] = m
            nv_ref[...] = v

    shapes = [jax.ShapeDtypeStruct(w.shape, F32) for w in ws]
    outs = pl.pallas_call(
        body, name="adamw", out_shape=shapes * 3,
        compiler_params=pltpu.CompilerParams(vmem_limit_bytes=VMEM_LIMIT),
    )(*ws, *gs, *ms, *vs)
    return outs[0:n], outs[n:2 * n], outs[2 * n:3 * n]


SMALL_A = (("meta_tokens", 16),)
SMALL_B = (("b_norm_g", 1), ("b_conv_w", 4), ("b_conv_b", 1), ("b_b_rg", 1), ("b_b_ig", 1), ("b_lam", 1))
REP = (("a_norm_g", 8), ("a_q_norm_g", 3), ("a_kv_norm_g", 2), ("final_norm_g", 8), ("loss", 1))
SLOT = 16


def _offsets(table, slot=1, start=0):
    out, o = {}, start
    for name, n in table:
        out[name] = (o, n)
        o += -(-n // slot) * slot
    return out, o


def _slotted(a, axis):
    pad = -a.shape[axis] % SLOT
    if not pad:
        return a
    widths = [(0, 0)] * a.ndim
    widths[axis] = (0, pad)
    return jnp.pad(a, widths)


def _rope_tables(rows):
    pos = np.arange(rows, dtype=np.float32)
    inv_freq = (np.float32(ROPE_BASE) ** (-np.arange(0, QK_ROPE, 2, dtype=np.float32) / np.float32(QK_ROPE))).astype(
        np.float32)
    ang = pos[:, None] * inv_freq[None, :]
    cos, sin = np.cos(ang).astype(np.float32), np.sin(ang).astype(np.float32)
    zeros = np.zeros((rows, LANES - QK_ROPE), np.float32)
    return jnp.asarray(np.concatenate([cos, cos, zeros], axis=1)), jnp.asarray(np.concatenate([-sin, sin, zeros], axis=1))


def kernel(x, meta_tokens, a_norm_g, a_w_in, a_q_norm_g, a_kv_norm_g, a_w_uq, a_w_ukv, a_w_out, b_norm_g, b_w_in, b_conv_w, b_conv_b, b_w_rg, b_b_rg, b_w_ig, b_b_ig, b_lam, b_w_out, final_norm_g, loss_target, m_meta_tokens, m_a_norm_g, m_a_w_in, m_a_q_norm_g, m_a_kv_norm_g, m_a_w_uq, m_a_w_ukv, m_a_w_out, m_b_norm_g, m_b_w_in, m_b_conv_w, m_b_conv_b, m_b_w_rg, m_b_b_rg, m_b_w_ig, m_b_b_ig, m_b_lam, m_b_w_out, m_final_norm_g, v_meta_tokens, v_a_norm_g, v_a_w_in, v_a_q_norm_g, v_a_kv_norm_g, v_a_w_uq, v_a_w_ukv, v_a_w_out, v_b_norm_g, v_b_w_in, v_b_conv_w, v_b_conv_b, v_b_w_rg, v_b_b_rg, v_b_w_ig, v_b_b_ig, v_b_lam, v_b_w_out, v_final_norm_g):
    names = ("meta_tokens", "a_norm_g", "a_w_in", "a_q_norm_g", "a_kv_norm_g", "a_w_uq", "a_w_ukv", "a_w_out",
             "b_norm_g", "b_w_in", "b_conv_w", "b_conv_b", "b_w_rg", "b_b_rg", "b_w_ig", "b_b_ig", "b_lam", "b_w_out",
             "final_norm_g")
    w = dict(zip(names, (meta_tokens, a_norm_g, a_w_in, a_q_norm_g, a_kv_norm_g, a_w_uq, a_w_ukv, a_w_out, b_norm_g,
                         b_w_in, b_conv_w, b_conv_b, b_w_rg, b_b_rg, b_w_ig, b_b_ig, b_lam, b_w_out, final_norm_g)))
    mom_m = dict(zip(names, (m_meta_tokens, m_a_norm_g, m_a_w_in, m_a_q_norm_g, m_a_kv_norm_g, m_a_w_uq, m_a_w_ukv,
                             m_a_w_out, m_b_norm_g, m_b_w_in, m_b_conv_w, m_b_conv_b, m_b_w_rg, m_b_b_rg, m_b_w_ig,
                             m_b_b_ig, m_b_lam, m_b_w_out, m_final_norm_g)))
    mom_v = dict(zip(names, (v_meta_tokens, v_a_norm_g, v_a_w_in, v_a_q_norm_g, v_a_kv_norm_g, v_a_w_uq, v_a_w_ukv,
                             v_a_w_out, v_b_norm_g, v_b_w_in, v_b_conv_w, v_b_conv_b, v_b_w_rg, v_b_b_rg, v_b_w_ig,
                             v_b_b_ig, v_b_lam, v_b_w_out, v_final_norm_g)))

    seq = x.shape[1]
    n_real = N_META + seq
    rows = -(-n_real // LANES) * LANES
    scale = (QK_NOPE + QK_ROPE) ** -0.5
    small_off, _ = _offsets(SMALL_A + SMALL_B, SLOT)
    gsmallb_off, _ = _offsets(SMALL_B, SLOT)
    rep_off, _ = _offsets(REP, SLOT)
    cdev_a = a_w_in.shape[-1]
    wide = 2 * LANES

    send_a0 = jnp.pad(a_w_in[0], ((0, 0), (0, wide - cdev_a))).astype(BF16)
    send_small = jnp.concatenate([_slotted(w[nm].reshape(-1, LANES), 0) for nm, _ in SMALL_A + SMALL_B], axis=0)
    sends_a1 = [jnp.pad(a_w_uq[0], ((0, 0), (0, HEAD_PAD - QK_NOPE - QK_ROPE))).astype(BF16), a_w_ukv[0].astype(BF16)]
    lru_rows = LRU_BLOCKS * LRU_BLOCK // N_DEV
    sends_b = [a_w_out[0].astype(BF16), b_w_in[0].astype(BF16), b_w_rg.reshape(lru_rows, LRU_BLOCK).astype(BF16),
               b_w_ig.reshape(lru_rows, LRU_BLOCK).astype(BF16), b_w_out[0].astype(BF16)]
    all_a0, all_small = _all_gather(send_a0, send_small)

    def small_seg(nm):
        o, n = small_off[nm]
        return all_small[:, o:o + n, :]

    w_in_a = all_a0[:, :, :cdev_a].transpose(1, 0, 2).reshape(D_MODEL, N_DEV * cdev_a)
    w_in_a = jnp.concatenate([w_in_a[:, :LAT + QK_ROPE], jnp.zeros((D_MODEL, LAT_PAD - LAT - QK_ROPE), BF16),
                              w_in_a[:, LAT + QK_ROPE:]], axis=1)[None]
    meta_full = small_seg("meta_tokens").transpose(1, 0, 2).reshape(N_META, D_MODEL)
    vec = lambda nm: small_seg(nm).reshape(1, D_MODEL)
    g_b, conv_b, b_rg, b_ig, lam = vec("b_norm_g"), vec("b_conv_b"), vec("b_b_rg"), vec("b_b_ig"), vec("b_lam")
    conv_w = small_seg("b_conv_w").transpose(1, 0, 2).reshape(CONV_WIDTH, LRU_WIDTH)
    g_a, g_q, g_kv = a_norm_g, a_q_norm_g, a_kv_norm_g
    g_f = final_norm_g.reshape(1, D_MODEL)

    cos, sin = _rope_tables(rows)

    h0, lat, gate_a, w_uq, w_ukv = _embed_norm_proj_fwd(x[0], meta_full, rows, g_a, w_in_a, LAT_PAD, "a_in_fwd",
                                                        sends_a1)
    qc, kc, v, vt = _mla_qkv_fwd(lat, g_q, g_kv, w_uq, w_ukv, cos, sin, scale)
    o, lse, w_out_a, w_in_b, w_rg, w_ig, w_out_b = _attn_fwd(qc, kc, vt, sends_b)

    lru_w = lambda g: g.reshape(N_DEV, LRU_BLOCKS, LRU_BLOCK // N_DEV, LRU_BLOCK).transpose(1, 0, 2, 3).reshape(
        LRU_BLOCKS, LRU_BLOCK, LRU_BLOCK)
    w_out_a, w_out_b = w_out_a.reshape(D_MODEL, D_MODEL), w_out_b.reshape(D_MODEL, D_MODEL)
    w_rg, w_ig = lru_w(w_rg), lru_w(w_ig)

    h1, u, gate_b = _out_proj_in_proj(o, gate_a, h0, w_out_a, g_b, w_in_b, LRU_WIDTH, "a_out_b_in_fwd")
    hs = _rglru_fwd(u, conv_w, conv_b, w_rg, b_rg, w_ig, b_ig, lam)
    dh2, loss_part, dg_f, dhs, dgate_b, dw_out_b = _out_proj_loss(hs, gate_b, h1, w_out_b, g_f, loss_target[0],
                                                                   n_real)

    du, dconv_w, dconv_b, dw_rg, db_rg, dw_ig, db_ig, dlam = _rglru_bwd(u, hs, dhs, conv_w, conv_b, w_rg, b_rg, w_ig,
                                                                       b_ig, lam)
    dh1, dw_in_b, dg_b = _norm_proj_bwd(h1, g_b, w_in_b, du, dgate_b, dh2, "b_in_bwd")
    do, dgate_a, dw_out_a, delta = _attn_out_bwd(o, gate_a, dh1, w_out_a)

    def to_cols(g, cdev):
        r = g.shape[0]
        return g.reshape(r, N_DEV, cdev).transpose(1, 0, 2).reshape(N_DEV, -1, LANES)

    lru_g = lambda g: g.reshape(LRU_BLOCKS, N_DEV, LRU_BLOCK // N_DEV, LRU_BLOCK).transpose(1, 0, 2, 3).reshape(
        N_DEV, lru_rows, LRU_BLOCK)
    small_b = {"b_norm_g": dg_b, "b_conv_w": dconv_w, "b_conv_b": dconv_b, "b_b_rg": db_rg, "b_b_ig": db_ig,
               "b_lam": dlam}
    gsends_b = [dw_out_a.reshape(N_DEV, -1, D_MODEL), dw_in_b, lru_g(dw_rg).astype(BF16), lru_g(dw_ig).astype(BF16),
                dw_out_b.reshape(N_DEV, -1, D_MODEL),
                jnp.concatenate([_slotted(to_cols(small_b[nm], LANES).astype(BF16), 1) for nm, _ in SMALL_B], axis=1)]

    dqc, dkc, dv, *lands_b = _attn_bwd(qc, kc, v, lse, delta, do, gsends_b)
    g_out_a, g_in_b, g_rg, g_ig, g_out_b, gsum_small_b = _sum_blocks(lands_b, "sum_blocks_b")
    dlat, dw_uq, dw_ukv, dg_q, dg_kv = _mla_qkv_bwd(lat, g_q, g_kv, w_uq, w_ukv, cos, sin, dqc, dkc, dv, scale)
    dh0, dw_in_a, dg_a, *lands_a1 = _norm_proj_bwd(h0, g_a, w_in_a, dlat, dgate_a, dh1, "a_in_bwd", [dw_uq, dw_ukv])
    g_uq, g_ukv = _sum_blocks(lands_a1, "sum_blocks_a1")

    grad_x = dh0[N_META:n_real][None]

    dw_in_a_nat = jnp.concatenate([dw_in_a[0, :, :LAT + QK_ROPE], dw_in_a[0, :, LAT_PAD:]], axis=1)
    in_lanes = lambda g, cdev: jnp.pad(g.reshape(g.shape[0], N_DEV, cdev).transpose(1, 0, 2),
                                       ((0, 0), (0, 0), (0, wide - cdev)))
    pieces = [in_lanes(dw_in_a_nat, cdev_a), in_lanes(dh0[:N_META].astype(BF16), LANES)]
    used = sum(p.shape[1] for p in pieces)
    pieces.append(jnp.zeros((N_DEV, -used % GRAD_CHUNK, wide), BF16))
    gsend_a0 = jnp.concatenate(pieces, axis=1)
    rep_parts = {"a_norm_g": dg_a, "a_q_norm_g": dg_q, "a_kv_norm_g": dg_kv, "final_norm_g": dg_f,
                 "loss": jnp.broadcast_to(loss_part, (1, LANES))}
    rep = jnp.concatenate([_slotted(rep_parts[nm].reshape(-1, LANES), 0) for nm, _ in REP], axis=0)
    gsum_a0, rep_sum = _grad_exchange(gsend_a0, rep)

    grads = {"a_w_out": g_out_a, "b_w_in": g_in_b, "b_w_rg": g_rg, "b_w_ig": g_ig, "b_w_out": g_out_b,
             "a_w_uq": g_uq[:, :QK_NOPE + QK_ROPE], "a_w_ukv": g_ukv}
    grads = {nm: g.reshape(w[nm].shape) for nm, g in grads.items()}
    grads["a_w_in"] = gsum_a0[:D_MODEL, :cdev_a].reshape(w["a_w_in"].shape)
    grads["meta_tokens"] = gsum_a0[D_MODEL:D_MODEL + N_META, :LANES]
    for off, src in ((gsmallb_off, gsum_small_b), (rep_off, rep_sum)):
        for nm, (o_r, n) in off.items():
            if nm in w:
                grads[nm] = src[o_r:o_r + n].reshape(w[nm].shape)
    loss = rep_sum[rep_off["loss"][0], 0]

    as2d = lambda a: a.reshape(-1, a.shape[-1])
    deltas, new_ms, new_vs = _adamw_all([as2d(w[nm]) for nm in names], [as2d(grads[nm]) for nm in names],
                                        [as2d(mom_m[nm]) for nm in names], [as2d(mom_v[nm]) for nm in names])
    shaped = lambda arrs: [a.reshape(w[nm].shape) for a, nm in zip(arrs, names)]
    return (loss, grad_x, *[grads[nm] for nm in names], *shaped(deltas), *shaped(new_ms), *shaped(new_vs))
```

```python
import functools

import numpy as np
import jax
import jax.numpy as jnp
from jax import lax
from jax.experimental import pallas as pl
from jax.experimental.pallas import tpu as pltpu

F32 = jnp.float32
BF16 = jnp.bfloat16

D_MODEL = 1024
N_META = 16
RMS_EPS = 1e-6
HEADS = 8
QK_NOPE = 128
QK_ROPE = 64
V_HEAD = 128
Q_LORA = 384
KV_LORA = 256
HEAD_PAD = 256
LAT = Q_LORA + KV_LORA
LAT_PAD = LAT + 128
ROPE_BASE = 10000.0
MASK_VALUE = -1e30
LRU_WIDTH = 1024
LRU_BLOCKS = 4
LRU_BLOCK = 256
CONV_WIDTH = 4
LRU_C = 8.0
N_DEV = 8
ADAM_LR, ADAM_B1, ADAM_B2, ADAM_EPS, ADAM_WD, ADAM_STEP = 0.001, 0.9, 0.999, 1e-08, 0.01, 10

LANES = 128
SUBLANES = 8
VMEM_LIMIT = 56 * 1024 * 1024
MESH = pl.DeviceIdType.MESH

NT = (((1,), (1,)), ((), ()))
TN = (((0,), (0,)), ((), ()))


def _row_block(rows):
    return 384 if rows % 384 == 0 else 128


def _cparams(sem):
    return pltpu.CompilerParams(dimension_semantics=sem, vmem_limit_bytes=VMEM_LIMIT)


def _silu(x):
    return x * jax.nn.sigmoid(x)


def _dsilu(x):
    s = jax.nn.sigmoid(x)
    return s * (1.0 + x * (1.0 - s))


def _rms_fwd(x):
    r = lax.rsqrt(jnp.mean(x * x, axis=-1, keepdims=True) + RMS_EPS)
    return x * r, r


def _rms_bwd(dy, xn, r, g):
    t = dy * g
    dx = r * (t - xn * jnp.mean(t * xn, axis=-1, keepdims=True))
    return dx, jnp.sum(dy * xn, axis=0, keepdims=True)


def _expm1_neg(x):
    small = x * (1.0 + x * (1 / 2 + x * (1 / 6 + x * (1 / 24))))
    return jnp.where(x > -0.05, small, jnp.exp(x) - 1.0)


def _softplus_neg(lam):
    z = jnp.exp(-jnp.abs(lam))
    w = z / (2.0 + z)
    w2 = w * w
    series = 2.0 * w * (1.0 + w2 * (1 / 3) + w2 * w2 * (1 / 5))
    return jnp.maximum(-lam, 0.0) + jnp.where(z < 0.1, series, jnp.log(1.0 + z))


def _rider(sends, refs, first, last, all_to_all):
    nb = len(sends)
    if not nb:
        return (lambda: None), (lambda: None)
    send_refs, result_refs = refs[:nb], refs[nb:2 * nb]
    send_sems, recv_sems, local_sems = refs[2 * nb:]
    pick = (lambda ref: (lambda d: ref.at[d])) if all_to_all else (lambda ref: (lambda d: ref))

    def copies():
        out = []
        for b in range(nb):
            out += _exchange_copies(pick(send_refs[b]), result_refs[b], send_sems.at[b], recv_sems.at[b],
                                    local_sems.at[b])
        return out

    def start():
        @pl.when(first)
        def _():
            for cp in copies():
                cp.start()

    def wait():
        @pl.when(last)
        def _():
            for cp in copies():
                cp.wait()

    return start, wait


def _rider_specs(sends, all_to_all):
    nb = len(sends)
    if not nb:
        return [], [], [], []
    hbm = pl.BlockSpec(memory_space=pl.ANY)
    shapes = [jax.ShapeDtypeStruct(s.shape if all_to_all else (N_DEV,) + s.shape, s.dtype) for s in sends]
    return [hbm] * nb, [hbm] * nb, shapes, _exchange_sems(nb)


def _proj_blocks(x, w_ref):
    return jnp.concatenate([jnp.dot(x, w_ref[d], preferred_element_type=F32) for d in range(w_ref.shape[0])], axis=1)


def _embed_norm_proj_fwd(x, meta, rows, g, w, n1, name, wsends=()):
    n_real = N_META + x.shape[0]
    nb, _, cb = w.shape
    n = nb * cb
    tr = _row_block(rows)
    nsteps = rows // tr
    extra = len(wsends)

    def body(x_ref, meta_ref, g_ref, w_ref, *rest):
        h_ref, p1_ref, p2_ref = rest[extra:extra + 3]
        i = pl.program_id(0)
        issue, forward, finish = _two_level_gather(
            tuple(zip(rest[:extra], rest[extra + 3:2 * extra + 3])), *rest[2 * extra + 3:])
        pl.when(i == 0)(issue)
        pl.when(i == nsteps // 2)(forward)
        xw = x_ref[...]
        xw = jnp.where(i == 0, pltpu.roll(xw, N_META, 0), xw)
        row = i * tr + lax.broadcasted_iota(jnp.int32, (tr, 1), 0)
        meta_rows = jnp.concatenate([meta_ref[...], jnp.zeros((tr - N_META, D_MODEL), F32)], axis=0)
        h = jnp.where(row < N_META, meta_rows, jnp.where(row < n_real, xw, 0.0))
        h_ref[...] = h
        xn, _ = _rms_fwd(h)
        p = _proj_blocks((xn * g_ref[...]).astype(BF16), w_ref)
        p1_ref[...] = p[:, :n1]
        p2_ref[...] = p[:, n1:]
        pl.when(i == nsteps - 1)(finish)

    assert nsteps >= 3, "the three phases of the riding gather need three grid steps"
    r_in, r_out, r_shape, r_scratch = _rider_specs(wsends, False)
    window = pl.BlockSpec((pl.Element(tr, (0, rows - n_real)), pl.Element(D_MODEL)),
                          lambda i: (pl.multiple_of(jnp.maximum(i * tr - N_META, 0), SUBLANES), 0))
    return pl.pallas_call(
        body, name=name, grid=(nsteps,),
        in_specs=[window,
                  pl.BlockSpec((N_META, D_MODEL), lambda i: (0, 0)),
                  pl.BlockSpec((1, D_MODEL), lambda i: (0, 0)),
                  pl.BlockSpec((nb, D_MODEL, cb), lambda i: (0, 0, 0))] + r_in,
        out_specs=[pl.BlockSpec((tr, D_MODEL), lambda i: (i, 0)),
                   pl.BlockSpec((tr, n1), lambda i: (i, 0)),
                   pl.BlockSpec((tr, n - n1), lambda i: (i, 0))] + r_out,
        out_shape=[jax.ShapeDtypeStruct((rows, D_MODEL), F32), jax.ShapeDtypeStruct((rows, n1), F32),
                   jax.ShapeDtypeStruct((rows, n - n1), F32)] + r_shape,
        scratch_shapes=r_scratch,
        compiler_params=_cparams(("arbitrary",)),
    )(x, meta, g, w, *wsends)


def _out_proj_in_proj(a, gate, h, w_out, g, w_in, n1, name):
    rows = h.shape[0]
    nb, _, cb = w_in.shape
    n = nb * cb
    tr = _row_block(rows)

    def body(a_ref, gate_ref, h_ref, wo_ref, g_ref, wi_ref, hn_ref, p1_ref, p2_ref):
        y = (a_ref[...] * _silu(gate_ref[...])).astype(BF16)
        h_new = h_ref[...] + jnp.dot(y, wo_ref[...], preferred_element_type=F32)
        hn_ref[...] = h_new
        xn, _ = _rms_fwd(h_new)
        p = _proj_blocks((xn * g_ref[...]).astype(BF16), wi_ref)
        p1_ref[...] = p[:, :n1]
        p2_ref[...] = p[:, n1:]

    blk = pl.BlockSpec((tr, D_MODEL), lambda i: (i, 0))
    return pl.pallas_call(
        body, name=name, grid=(rows // tr,),
        in_specs=[blk, blk, blk, pl.BlockSpec((D_MODEL, D_MODEL), lambda i: (0, 0)),
                  pl.BlockSpec((1, D_MODEL), lambda i: (0, 0)), pl.BlockSpec((nb, D_MODEL, cb), lambda i: (0, 0, 0))],
        out_specs=[blk, pl.BlockSpec((tr, n1), lambda i: (i, 0)), pl.BlockSpec((tr, n - n1), lambda i: (i, 0))],
        out_shape=[jax.ShapeDtypeStruct((rows, D_MODEL), F32), jax.ShapeDtypeStruct((rows, n1), F32),
                   jax.ShapeDtypeStruct((rows, n - n1), F32)],
        compiler_params=_cparams(("parallel",)),
    )(a, gate, h, w_out, g, w_in)


def _norm_proj_bwd(h, g, w, dp1, dp2, dh_in, name, gsends=()):
    rows = h.shape[0]
    nb, _, cb = w.shape
    n1 = dp1.shape[1]
    n2 = nb * cb - n1
    tr = _row_block(rows)
    nsteps = rows // tr
    extra = len(gsends)

    def body(h_ref, g_ref, w_ref, dp1_ref, dp2_ref, dhin_ref, *rest):
        dh_ref, dw_ref, dg_ref = rest[extra:extra + 3]
        dw_acc = rest[2 * extra + 3]
        i = pl.program_id(0)
        start, wait = _rider(gsends, rest[:extra] + rest[extra + 3:2 * extra + 3] + rest[2 * extra + 4:],
                             i == 0, i == nsteps - 1, True)
        start()

        @pl.when(i == 0)
        def _():
            dw_acc[...] = jnp.zeros_like(dw_acc)
            dg_ref[...] = jnp.zeros_like(dg_ref)

        gv = g_ref[...]
        xn, r = _rms_fwd(h_ref[...])
        hn = (xn * gv).astype(BF16)
        dp = jnp.concatenate([dp1_ref[...].astype(BF16), dp2_ref[...].astype(BF16)], axis=1)
        dhn = jnp.zeros((tr, D_MODEL), F32)
        for d in range(nb):
            dpd = dp[:, d * cb:(d + 1) * cb]
            dw_acc[d] += lax.dot_general(hn, dpd, TN, preferred_element_type=F32)
            dhn = dhn + lax.dot_general(dpd, w_ref[d], NT, preferred_element_type=F32)
        dx, dg = _rms_bwd(dhn, xn, r, gv)
        dg_ref[...] += dg
        dh_ref[...] = dhin_ref[...] + dx

        @pl.when(i == nsteps - 1)
        def _():
            dw_ref[...] = dw_acc[...].astype(BF16)

        wait()

    r_in, r_out, r_shape, r_scratch = _rider_specs(gsends, True)
    wblk = pl.BlockSpec((nb, D_MODEL, cb), lambda i: (0, 0, 0))
    return pl.pallas_call(
        body, name=name, grid=(nsteps,),
        in_specs=[pl.BlockSpec((tr, D_MODEL), lambda i: (i, 0)),
                  pl.BlockSpec((1, D_MODEL), lambda i: (0, 0)),
                  wblk,
                  pl.BlockSpec((tr, n1), lambda i: (i, 0)),
                  pl.BlockSpec((tr, n2), lambda i: (i, 0)),
                  pl.BlockSpec((tr, D_MODEL), lambda i: (i, 0))] + r_in,
        out_specs=[pl.BlockSpec((tr, D_MODEL), lambda i: (i, 0)), wblk,
                   pl.BlockSpec((1, D_MODEL), lambda i: (0, 0))] + r_out,
        out_shape=[jax.ShapeDtypeStruct((rows, D_MODEL), F32),
                   jax.ShapeDtypeStruct((nb, D_MODEL, cb), BF16),
                   jax.ShapeDtypeStruct((1, D_MODEL), F32)] + r_shape,
        scratch_shapes=[pltpu.VMEM((nb, D_MODEL, cb), F32)] + r_scratch,
        compiler_params=_cparams(("arbitrary",)),
    )(h, g, w, dp1, dp2, dh_in, *gsends)


def _attn_out_bwd(o, gate, dh, w):
    rows = o.shape[0]
    tr = _row_block(rows)
    nsteps = rows // tr

    def body(o_ref, gate_ref, dh_ref, w_ref, do_ref, dgate_ref, dw_ref, delta_ref, dw_acc):
        i = pl.program_id(0)

        @pl.when(i == 0)
        def _():
            dw_acc[...] = jnp.zeros_like(dw_acc)

        ov, gv = o_ref[...], gate_ref[...]
        sg = _silu(gv)
        dhb = dh_ref[...].astype(BF16)
        dw_acc[...] += lax.dot_general((ov * sg).astype(BF16), dhb, TN, preferred_element_type=F32)
        dy = lax.dot_general(dhb, w_ref[...], NT, preferred_element_type=F32)
        do = (dy * sg).astype(BF16)
        do_ref[...] = do
        dgate_ref[...] = (dy * ov * _dsilu(gv)).astype(BF16)
        prod = do.astype(F32) * ov
        lane = lax.broadcasted_iota(jnp.int32, (tr, LANES), 1)
        per_head = jnp.zeros((tr, LANES), F32)
        for hd in range(HEADS):
            dsum = jnp.sum(prod[:, hd * V_HEAD:(hd + 1) * V_HEAD], axis=1, keepdims=True)
            per_head = jnp.where(lane == hd, dsum, per_head)
        delta_t = per_head.T
        for hd in range(HEADS):
            delta_ref[hd, 0] = delta_t[hd:hd + 1, :]

        @pl.when(i == nsteps - 1)
        def _():
            dw_ref[...] = dw_acc[...].astype(BF16)

    blk = pl.BlockSpec((tr, D_MODEL), lambda i: (i, 0))
    wblk = pl.BlockSpec((D_MODEL, D_MODEL), lambda i: (0, 0))
    return pl.pallas_call(
        body, name="a_out_bwd", grid=(nsteps,),
        in_specs=[blk, blk, blk, wblk],
        out_specs=[blk, blk, wblk, pl.BlockSpec((HEADS, 1, 1, tr), lambda i: (0, i, 0, 0))],
        out_shape=[jax.ShapeDtypeStruct((rows, D_MODEL), BF16), jax.ShapeDtypeStruct((rows, D_MODEL), BF16),
                   jax.ShapeDtypeStruct((D_MODEL, D_MODEL), BF16),
                   jax.ShapeDtypeStruct((HEADS, nsteps, 1, tr), F32)],
        scratch_shapes=[pltpu.VMEM((D_MODEL, D_MODEL), F32)],
        compiler_params=_cparams(("arbitrary",)),
    )(o, gate, dh, w)


def _rope(v, cos, sin, lane):
    swapped = jnp.where(lane < QK_ROPE // 2, pltpu.roll(v, LANES - QK_ROPE // 2, 1), pltpu.roll(v, QK_ROPE // 2, 1))
    return v * cos + swapped * sin


def _unrope(dv, cos, sin, lane):
    t = dv * sin
    swapped = jnp.where(lane < QK_ROPE // 2, pltpu.roll(t, LANES - QK_ROPE // 2, 1), pltpu.roll(t, QK_ROPE // 2, 1))
    return dv * cos + swapped


def _mla_qkv_fwd(lat, gq, gkv, wuq, wukv, cos, sin, scale):
    rows = lat.shape[0]
    tr = _row_block(rows)

    def body(lat_ref, gq_ref, gkv_ref, wuq_ref, wukv_ref, cos_ref, sin_ref, qc_ref, kc_ref, v_ref, vt_ref):
        qn, _ = _rms_fwd(lat_ref[:, :Q_LORA])
        kvn, _ = _rms_fwd(lat_ref[:, Q_LORA:LAT])
        qnb = (qn * gq_ref[...]).astype(BF16)
        kvnb = (kvn * gkv_ref[...]).astype(BF16)
        c, s = cos_ref[...], sin_ref[...]
        lane = lax.broadcasted_iota(jnp.int32, (tr, LANES), 1)
        kr = _rope(lat_ref[:, LAT:LAT_PAD], c, s, lane).astype(BF16)
        for hd in range(HEADS):
            o = hd * HEAD_PAD
            q = jnp.dot(qnb, wuq_ref[hd], preferred_element_type=F32)
            kv = jnp.dot(kvnb, wukv_ref[hd], preferred_element_type=F32)
            qc_ref[:, o:o + QK_NOPE] = (q[:, :QK_NOPE] * scale).astype(BF16)
            qc_ref[:, o + QK_NOPE:o + HEAD_PAD] = (_rope(q[:, QK_NOPE:], c, s, lane) * scale).astype(BF16)
            kc_ref[:, o:o + QK_NOPE] = kv[:, :QK_NOPE].astype(BF16)
            kc_ref[:, o + QK_NOPE:o + HEAD_PAD] = kr
            vh = kv[:, QK_NOPE:]
            v_ref[:, hd * V_HEAD:(hd + 1) * V_HEAD] = vh.astype(BF16)
            vt_ref[hd, 0] = vh.T.astype(BF16)

    full = lambda shape: pl.BlockSpec(shape, lambda i: (0,) * len(shape))
    rowb = lambda n: pl.BlockSpec((tr, n), lambda i: (i, 0))
    return pl.pallas_call(
        body, name="mla_qkv_fwd", grid=(rows // tr,),
        in_specs=[rowb(LAT_PAD), full((1, Q_LORA)), full((1, KV_LORA)), full((HEADS, Q_LORA, HEAD_PAD)),
                  full((HEADS, KV_LORA, HEAD_PAD)), rowb(LANES), rowb(LANES)],
        out_specs=[rowb(HEADS * HEAD_PAD), rowb(HEADS * HEAD_PAD), rowb(HEADS * V_HEAD),
                   pl.BlockSpec((HEADS, 1, V_HEAD, tr), lambda i: (0, i, 0, 0))],
        out_shape=[jax.ShapeDtypeStruct((rows, HEADS * HEAD_PAD), BF16),
                   jax.ShapeDtypeStruct((rows, HEADS * HEAD_PAD), BF16),
                   jax.ShapeDtypeStruct((rows, HEADS * V_HEAD), BF16),
                   jax.ShapeDtypeStruct((HEADS, rows // tr, V_HEAD, tr), BF16)],
        compiler_params=_cparams(("parallel",)),
    )(lat, gq, gkv, wuq, wukv, cos, sin)


def _mla_qkv_bwd(lat, gq, gkv, wuq, wukv, cos, sin, dqc, dkc, dv, scale):
    rows = lat.shape[0]
    tr = _row_block(rows)
    nsteps = rows // tr

    def body(lat_ref, gq_ref, gkv_ref, wuq_ref, wukv_ref, cos_ref, sin_ref, dqc_ref, dkc_ref, dv_ref,
             dlat_ref, dwuq_out, dwukv_out, dgq_ref, dgkv_ref, dwuq_ref, dwukv_ref):
        @pl.when(pl.program_id(0) == 0)
        def _():
            dwuq_ref[...] = jnp.zeros_like(dwuq_ref)
            dwukv_ref[...] = jnp.zeros_like(dwukv_ref)
            dgq_ref[...] = jnp.zeros_like(dgq_ref)
            dgkv_ref[...] = jnp.zeros_like(dgkv_ref)

        c, s = cos_ref[...], sin_ref[...]
        lane = lax.broadcasted_iota(jnp.int32, (tr, LANES), 1)
        gqv, gkvv = gq_ref[...], gkv_ref[...]
        qn, rq = _rms_fwd(lat_ref[:, :Q_LORA])
        kvn, rkv = _rms_fwd(lat_ref[:, Q_LORA:LAT])
        qnb = (qn * gqv).astype(BF16)
        kvnb = (kvn * gkvv).astype(BF16)
        dkr = jnp.zeros((tr, LANES), F32)
        dqn = jnp.zeros((tr, Q_LORA), F32)
        dkvn = jnp.zeros((tr, KV_LORA), F32)
        for hd in range(HEADS):
            o = hd * HEAD_PAD
            dq = jnp.concatenate(
                [dqc_ref[:, o:o + QK_NOPE],
                 _unrope(dqc_ref[:, o + QK_NOPE:o + HEAD_PAD].astype(F32), c, s, lane).astype(BF16)], axis=1)
            dkv = jnp.concatenate([dkc_ref[:, o:o + QK_NOPE], dv_ref[:, hd * V_HEAD:(hd + 1) * V_HEAD]], axis=1)
            dkr = dkr + dkc_ref[:, o + QK_NOPE:o + HEAD_PAD].astype(F32)
            dwuq_ref[hd] += scale * lax.dot_general(qnb, dq, TN, preferred_element_type=F32)
            dwukv_ref[hd] += lax.dot_general(kvnb, dkv, TN, preferred_element_type=F32)
            dqn = dqn + lax.dot_general(dq, wuq_ref[hd], NT, preferred_element_type=F32)
            dkvn = dkvn + lax.dot_general(dkv, wukv_ref[hd], NT, preferred_element_type=F32)
        dqn = scale * dqn
        dqlat, dgq = _rms_bwd(dqn, qn, rq, gqv)
        dkvlat, dgkv = _rms_bwd(dkvn, kvn, rkv, gkvv)
        dgq_ref[...] += dgq
        dgkv_ref[...] += dgkv
        dlat_ref[:, :Q_LORA] = dqlat.astype(BF16)
        dlat_ref[:, Q_LORA:LAT] = dkvlat.astype(BF16)
        dlat_ref[:, LAT:LAT_PAD] = _unrope(dkr, c, s, lane).astype(BF16)

        @pl.when(pl.program_id(0) == nsteps - 1)
        def _():
            dwuq_out[...] = dwuq_ref[...].astype(BF16)
            dwukv_out[...] = dwukv_ref[...].astype(BF16)

    full = lambda shape: pl.BlockSpec(shape, lambda i: (0,) * len(shape))
    rowb = lambda n: pl.BlockSpec((tr, n), lambda i: (i, 0))
    return pl.pallas_call(
        body, name="mla_qkv_bwd", grid=(nsteps,),
        in_specs=[rowb(LAT_PAD), full((1, Q_LORA)), full((1, KV_LORA)), full((HEADS, Q_LORA, HEAD_PAD)),
                  full((HEADS, KV_LORA, HEAD_PAD)), rowb(LANES), rowb(LANES),
                  rowb(HEADS * HEAD_PAD), rowb(HEADS * HEAD_PAD), rowb(HEADS * V_HEAD)],
        out_specs=[rowb(LAT_PAD), full((HEADS, Q_LORA, HEAD_PAD)), full((HEADS, KV_LORA, HEAD_PAD)),
                   full((1, Q_LORA)), full((1, KV_LORA))],
        out_shape=[jax.ShapeDtypeStruct((rows, LAT_PAD), BF16),
                   jax.ShapeDtypeStruct((HEADS, Q_LORA, HEAD_PAD), BF16),
                   jax.ShapeDtypeStruct((HEADS, KV_LORA, HEAD_PAD), BF16),
                   jax.ShapeDtypeStruct((1, Q_LORA), F32),
                   jax.ShapeDtypeStruct((1, KV_LORA), F32)],
        scratch_shapes=[pltpu.VMEM((HEADS, Q_LORA, HEAD_PAD), F32), pltpu.VMEM((HEADS, KV_LORA, HEAD_PAD), F32)],
        compiler_params=_cparams(("arbitrary",)),
    )(lat, gq, gkv, wuq, wukv, cos, sin, dqc, dkc, dv)


ATTN_UNROLL = 4
ATTN_UNROLL_BWD = 6
ATTN_HEADS = 2
ATTN_HEADS_FWD = 4


def _causal_mask_t(t):
    key = lax.broadcasted_iota(jnp.int32, (t, t), 0)
    query = lax.broadcasted_iota(jnp.int32, (t, t), 1)
    return key <= query


def _attn_fwd(qc, kc, vt, wsends):
    rows = qc.shape[0]
    t = _row_block(rows)
    nblk = rows // t
    nw = len(wsends)

    def body(q_ref, k_ref, vt_ref, *rest):
        o_ref, lse_ref = rest[nw:nw + 2]
        m_ref, l_ref, acc_ref, st_a, st_b = rest[2 * nw + 2:2 * nw + 7]
        i = pl.program_id(1)
        start, wait = _rider(wsends, rest[:nw] + rest[nw + 2:2 * nw + 2] + rest[2 * nw + 7:],
                             jnp.logical_and(pl.program_id(0) == 0, i == 0),
                             jnp.logical_and(pl.program_id(0) == HEADS // ATTN_HEADS_FWD - 1, i == nblk - 1), False)
        start()

        m_ref[...] = jnp.full_like(m_ref, MASK_VALUE)
        l_ref[...] = jnp.zeros_like(l_ref)
        acc_ref[...] = jnp.zeros_like(acc_ref)
        heads = range(ATTN_HEADS_FWD)
        qs = [q_ref[:, hh * HEAD_PAD:(hh + 1) * HEAD_PAD] for hh in heads]

        def scores(j, hh, st_ref):
            rs = pl.ds(pl.multiple_of(j * t, t), t)
            st_ref[hh] = lax.dot_general(k_ref[rs, hh * HEAD_PAD:(hh + 1) * HEAD_PAD], qs[hh], NT,
                                         preferred_element_type=F32)

        def consume(j, hh, st_ref, masked):
            st = st_ref[hh]
            if masked:
                st = jnp.where(_causal_mask_t(t), st, MASK_VALUE)
            m_prev = m_ref[hh]
            m_new = jnp.maximum(m_prev, jnp.max(st, axis=0, keepdims=True))
            alpha = jnp.exp(m_prev - m_new)
            pt = jnp.exp(st - m_new)
            l_ref[hh] = alpha * l_ref[hh] + jnp.sum(pt, axis=0, keepdims=True)
            acc_ref[hh] = alpha * acc_ref[hh] + jnp.dot(vt_ref[hh, j], pt.astype(BF16), preferred_element_type=F32)
            m_ref[hh] = m_new

        bufs = (st_a, st_b)

        def step(j, parity, issue_next, masked):
            if issue_next:
                for hh in heads:
                    scores(j + 1, hh, bufs[1 - parity])
            for hh in heads:
                consume(j, hh, bufs[parity], masked)

        for hh in heads:
            scores(0, hh, st_a)

        def trip(it, carry):
            for u in range(ATTN_UNROLL):
                step(it * ATTN_UNROLL + u, u % 2, True, False)
            return carry

        trips = i // ATTN_UNROLL
        lax.fori_loop(0, trips, trip, 0)
        j0 = trips * ATTN_UNROLL
        for left in range(1, ATTN_UNROLL + 1):
            @pl.when(i + 1 - j0 == left)
            def _(left=left):
                for u in range(left):
                    step(j0 + u, u % 2, u < left - 1, u == left - 1)

        for hh in heads:
            o_ref[:, hh * V_HEAD:(hh + 1) * V_HEAD] = (acc_ref[hh] / l_ref[hh]).T
            lse_ref[hh, 0] = m_ref[hh] + jnp.log(l_ref[hh])
        wait()

    r_in, r_out, r_shape, r_scratch = _rider_specs(wsends, False)
    nh = ATTN_HEADS_FWD
    return pl.pallas_call(
        body, name="attn_fwd", grid=(HEADS // nh, nblk),
        in_specs=[pl.BlockSpec((t, nh * HEAD_PAD), lambda g, i: (i, g)),
                  pl.BlockSpec((rows, nh * HEAD_PAD), lambda g, i: (0, g)),
                  pl.BlockSpec((nh, nblk, V_HEAD, t), lambda g, i: (g, 0, 0, 0))] + r_in,
        out_specs=[pl.BlockSpec((t, nh * V_HEAD), lambda g, i: (i, g)),
                   pl.BlockSpec((nh, 1, 1, t), lambda g, i: (g, i, 0, 0))] + r_out,
        out_shape=[jax.ShapeDtypeStruct((rows, HEADS * V_HEAD), F32),
                   jax.ShapeDtypeStruct((HEADS, nblk, 1, t), F32)] + r_shape,
        scratch_shapes=[pltpu.VMEM((nh, 1, t), F32), pltpu.VMEM((nh, 1, t), F32), pltpu.VMEM((nh, V_HEAD, t), F32),
                        pltpu.VMEM((nh, t, t), F32), pltpu.VMEM((nh, t, t), F32)] + r_scratch,
        compiler_params=_cparams(("arbitrary", "arbitrary")),
    )(qc, kc, vt, *wsends)


def _attn_bwd(qc, kc, v, lse, delta, do, gsends):
    rows = qc.shape[0]
    t = _row_block(rows)
    nblk = rows // t
    ng = len(gsends)

    def body(q_ref, k_ref, v_ref, lse_ref, delta_ref, do_ref, *rest):
        dq_ref, dk_ref, dv_ref = rest[ng:ng + 3]
        dq_acc, dk_acc, dv_acc, st_a, dp_a, st_b, dp_b = rest[2 * ng + 3:2 * ng + 10]
        j = pl.program_id(1)
        start, wait = _rider(gsends, rest[:ng] + rest[ng + 3:2 * ng + 3] + rest[2 * ng + 10:],
                             jnp.logical_and(pl.program_id(0) == 0, j == 0),
                             jnp.logical_and(pl.program_id(0) == HEADS // ATTN_HEADS - 1, j == nblk - 1), True)
        start()

        @pl.when(j == 0)
        def _():
            dq_acc[...] = jnp.zeros_like(dq_acc)

        dk_acc[...] = jnp.zeros_like(dk_acc)
        dv_acc[...] = jnp.zeros_like(dv_acc)
        heads = range(ATTN_HEADS)
        qk = lambda hh: slice(hh * HEAD_PAD, (hh + 1) * HEAD_PAD)
        vo = lambda hh: slice(hh * V_HEAD, (hh + 1) * V_HEAD)
        ks = [k_ref[:, qk(hh)] for hh in heads]
        vs = [v_ref[:, vo(hh)] for hh in heads]

        def products(i, hh, st_ref, dp_ref):
            rs = pl.ds(pl.multiple_of(i * t, t), t)
            st_ref[hh] = lax.dot_general(ks[hh], q_ref[rs, qk(hh)], NT, preferred_element_type=F32)
            dp_ref[hh] = lax.dot_general(vs[hh], do_ref[rs, vo(hh)], NT, preferred_element_type=F32)

        def consume(i, hh, st_ref, dp_ref):
            rs = pl.ds(pl.multiple_of(i * t, t), t)
            q = q_ref[rs, qk(hh)]
            dob = do_ref[rs, vo(hh)]
            st = jnp.where(jnp.logical_or(_causal_mask_t(t), i != j), st_ref[hh], MASK_VALUE)
            pt = jnp.exp(st - lse_ref[hh, i])
            dv_acc[hh] += jnp.dot(pt.astype(BF16), dob, preferred_element_type=F32)
            dst = (pt * (dp_ref[hh] - delta_ref[hh, i])).astype(BF16)
            dk_acc[hh] += jnp.dot(dst, q, preferred_element_type=F32)
            dq_acc[hh, rs, :] += lax.dot_general(dst, ks[hh], TN, preferred_element_type=F32)

        bufs = ((st_a, dp_a), (st_b, dp_b))

        def step(i, parity, issue_next):
            if issue_next:
                for hh in heads:
                    products(i + 1, hh, *bufs[1 - parity])
            for hh in heads:
                consume(i, hh, *bufs[parity])

        for hh in heads:
            products(j, hh, st_a, dp_a)

        def trip(it, carry):
            for u in range(ATTN_UNROLL_BWD):
                step(j + it * ATTN_UNROLL_BWD + u, u % 2, True)
            return carry

        trips = (nblk - 1 - j) // ATTN_UNROLL_BWD
        lax.fori_loop(0, trips, trip, 0)
        i0 = j + trips * ATTN_UNROLL_BWD
        for left in range(1, ATTN_UNROLL_BWD + 1):
            @pl.when(nblk - i0 == left)
            def _(left=left):
                for u in range(left):
                    step(i0 + u, u % 2, u < left - 1)

        for hh in heads:
            dk_ref[:, qk(hh)] = dk_acc[hh].astype(BF16)
            dv_ref[:, vo(hh)] = dv_acc[hh].astype(BF16)

        @pl.when(j == nblk - 1)
        def _():
            for hh in heads:
                dq_ref[:, qk(hh)] = dq_acc[hh].astype(BF16)

        wait()

    nh = ATTN_HEADS
    stat = pl.BlockSpec((nh, nblk, 1, t), lambda g, j: (g, 0, 0, 0))
    r_in, r_out, r_shape, r_scratch = _rider_specs(gsends, True)
    return pl.pallas_call(
        body, name="attn_bwd", grid=(HEADS // nh, nblk),
        in_specs=[pl.BlockSpec((rows, nh * HEAD_PAD), lambda g, j: (0, g)),
                  pl.BlockSpec((t, nh * HEAD_PAD), lambda g, j: (j, g)),
                  pl.BlockSpec((t, nh * V_HEAD), lambda g, j: (j, g)),
                  stat, stat,
                  pl.BlockSpec((rows, nh * V_HEAD), lambda g, j: (0, g))] + r_in,
        out_specs=[pl.BlockSpec((rows, nh * HEAD_PAD), lambda g, j: (0, g)),
                   pl.BlockSpec((t, nh * HEAD_PAD), lambda g, j: (j, g)),
                   pl.BlockSpec((t, nh * V_HEAD), lambda g, j: (j, g))] + r_out,
        out_shape=[jax.ShapeDtypeStruct((rows, HEADS * HEAD_PAD), BF16),
                   jax.ShapeDtypeStruct((rows, HEADS * HEAD_PAD), BF16),
                   jax.ShapeDtypeStruct((rows, HEADS * V_HEAD), BF16)] + r_shape,
        scratch_shapes=[pltpu.VMEM((nh, rows, HEAD_PAD), F32), pltpu.VMEM((nh, t, HEAD_PAD), F32),
                        pltpu.VMEM((nh, t, V_HEAD), F32)] + [pltpu.VMEM((nh, t, t), F32)] * 4 + r_scratch,
        compiler_params=_cparams(("arbitrary", "arbitrary")),
    )(qc, kc, v, lse, delta, do, *gsends)


def _shift_down(prev_tile, x, k):
    xx = jnp.concatenate([prev_tile, x], axis=0)
    return pltpu.roll(xx, k, 0)[SUBLANES:]


def _shift_up(x, next_tile, k):
    n = x.shape[0]
    xx = jnp.concatenate([x, next_tile], axis=0)
    return pltpu.roll(xx, n + SUBLANES - k, 0)[:n]


def _lru_gates(u, u_prev, cw_ref, cb_ref, wrg_ref, brg_ref, wig_ref, big_ref, lam_ref, first_block):
    taps = [_shift_down(u_prev, u, CONV_WIDTH - 1 - j) if j < CONV_WIDTH - 1 else u for j in range(CONV_WIDTH)]
    uc = cb_ref[...] + taps[0] * cw_ref[0:1, :]
    for j in range(1, CONV_WIDTH):
        uc = uc + taps[j] * cw_ref[j:j + 1, :]
    ub = uc.astype(BF16)
    zr = jnp.concatenate([jnp.dot(ub[:, g * LRU_BLOCK:(g + 1) * LRU_BLOCK], wrg_ref[g], preferred_element_type=F32)
                          for g in range(LRU_BLOCKS)], axis=1) + brg_ref[...]
    zi = jnp.concatenate([jnp.dot(ub[:, g * LRU_BLOCK:(g + 1) * LRU_BLOCK], wig_ref[g], preferred_element_type=F32)
                          for g in range(LRU_BLOCKS)], axis=1) + big_ref[...]
    r = jax.nn.sigmoid(zr)
    ig = jax.nn.sigmoid(zi)
    sp = _softplus_neg(lam_ref[...])
    log_a = (-LRU_C) * r * sp
    a = jnp.exp(log_a)
    m2 = -_expm1_neg(2.0 * log_a)
    mult_raw = m2 * lax.rsqrt(jnp.maximum(m2, 1e-30))
    row = lax.broadcasted_iota(jnp.int32, u.shape, 0)
    is_start = jnp.logical_and(first_block, row == 0)
    mult = jnp.where(is_start, 1.0, mult_raw)
    return dict(taps=taps, uc=uc, ub=ub, r=r, ig=ig, sp=sp, a=a, mult=mult, mult_raw=mult_raw, is_start=is_start)


def _rglru_fwd(u, cw, cb, wrg, brg, wig, big, lam):
    rows = u.shape[0]
    tb = _row_block(rows)

    def body(u_ref, cw_ref, cb_ref, wrg_ref, brg_ref, wig_ref, big_ref, lam_ref, hs_ref, utail, hcar, a_s, b_s):
        i = pl.program_id(0)

        @pl.when(i == 0)
        def _():
            utail[...] = jnp.zeros_like(utail)
            hcar[...] = jnp.zeros_like(hcar)

        u = u_ref[...]
        gt = _lru_gates(u, utail[...], cw_ref, cb_ref, wrg_ref, brg_ref, wig_ref, big_ref, lam_ref, i == 0)
        a_s[...] = gt["a"]
        b_s[...] = gt["mult"] * (gt["ig"] * gt["uc"])
        row8 = lax.broadcasted_iota(jnp.int32, (SUBLANES, LRU_WIDTH), 0)

        def tile(tix, carry):
            rs = pl.ds(pl.multiple_of(tix * SUBLANES, SUBLANES), SUBLANES)
            av, bv = a_s[rs, :], b_s[rs, :]
            for k in (1, 2, 4):
                keep = row8 >= k
                bv = jnp.where(keep, av * pltpu.roll(bv, k, 0) + bv, bv)
                av = jnp.where(keep, av * pltpu.roll(av, k, 0), av)
            h8 = av * carry + bv
            hs_ref[rs, :] = h8
            return jnp.broadcast_to(h8[SUBLANES - 1:SUBLANES, :], (SUBLANES, LRU_WIDTH))

        hcar[...] = lax.fori_loop(0, tb // SUBLANES, tile, hcar[...])
        utail[...] = u[tb - SUBLANES:, :]

    full2 = lambda shape: pl.BlockSpec(shape, lambda i: (0, 0))
    full3 = lambda shape: pl.BlockSpec(shape, lambda i: (0, 0, 0))
    blk = pl.BlockSpec((tb, LRU_WIDTH), lambda i: (i, 0))
    return pl.pallas_call(
        body, name="rglru_fwd", grid=(rows // tb,),
        in_specs=[blk, full2((CONV_WIDTH, LRU_WIDTH)), full2((1, LRU_WIDTH)),
                  full3((LRU_BLOCKS, LRU_BLOCK, LRU_BLOCK)), full2((1, LRU_WIDTH)),
                  full3((LRU_BLOCKS, LRU_BLOCK, LRU_BLOCK)), full2((1, LRU_WIDTH)), full2((1, LRU_WIDTH))],
        out_specs=blk,
        out_shape=jax.ShapeDtypeStruct((rows, LRU_WIDTH), F32),
        scratch_shapes=[pltpu.VMEM((SUBLANES, LRU_WIDTH), F32), pltpu.VMEM((SUBLANES, LRU_WIDTH), F32),
                        pltpu.VMEM((tb, LRU_WIDTH), F32), pltpu.VMEM((tb, LRU_WIDTH), F32)],
        compiler_params=_cparams(("arbitrary",)),
    )(u, cw, cb, wrg, brg, wig, big, lam)


def _rglru_bwd(u, hs, dhs, cw, cb, wrg, brg, wig, big, lam):
    rows = u.shape[0]
    tb = _row_block(rows)
    nblk = rows // tb
    tiles = tb // SUBLANES

    def body(u_ref, up_ref, hs_ref, hp_ref, dhs_ref, cw_ref, cb_ref, wrg_ref, brg_ref, wig_ref, big_ref, lam_ref,
             du_ref, dcw_ref, dcb_ref, dwrg_ref, dbrg_ref, dwig_ref, dbig_ref, dlam_ref,
             gcar, duc_head, a_s, b_s, g_s, dsp_acc):
        step = pl.program_id(0)
        blk_ix = nblk - 1 - step

        @pl.when(step == 0)
        def _():
            for ref in (dcw_ref, dcb_ref, dwrg_ref, dbrg_ref, dwig_ref, dbig_ref, gcar, duc_head, dsp_acc):
                ref[...] = jnp.zeros_like(ref)

        first = blk_ix == 0
        u = u_ref[...]
        u_prev = jnp.where(first, 0.0, up_ref[...])
        h_prev_tile = jnp.where(first, 0.0, hp_ref[...])
        gt = _lru_gates(u, u_prev, cw_ref, cb_ref, wrg_ref, brg_ref, wig_ref, big_ref, lam_ref, first)
        a, r, ig, uc, mult = gt["a"], gt["r"], gt["ig"], gt["uc"], gt["mult"]
        dhs_v = dhs_ref[...]

        a_s[...] = a
        b_s[...] = a * dhs_v
        row8 = lax.broadcasted_iota(jnp.int32, (SUBLANES, LRU_WIDTH), 0)

        def tile(tix, carry):
            rs = pl.ds(pl.multiple_of((tiles - 1 - tix) * SUBLANES, SUBLANES), SUBLANES)
            av, bv = a_s[rs, :], b_s[rs, :]
            for k in (1, 2, 4):
                keep = row8 < SUBLANES - k
                bv = jnp.where(keep, av * pltpu.roll(bv, SUBLANES - k, 0) + bv, bv)
                av = jnp.where(keep, av * pltpu.roll(av, SUBLANES - k, 0), av)
            g8 = av * carry + bv
            g_s[rs, :] = g8
            return jnp.broadcast_to(g8[0:1, :], (SUBLANES, LRU_WIDTH))

        g_next = gcar[...]
        gcar[...] = lax.fori_loop(0, tiles, tile, g_next)
        g = dhs_v + _shift_up(g_s[...], g_next, 1)

        h_prev = _shift_down(h_prev_tile, hs_ref[...], 1)
        da = g * h_prev
        iu = ig * uc
        dmult = jnp.where(gt["is_start"], 0.0, g * iu)
        d_ig = g * mult * uc
        duc = g * mult * ig
        dlog_a = da * a - dmult * (a * a) / gt["mult_raw"]
        dzr = (dlog_a * ((-LRU_C) * gt["sp"])) * r * (1.0 - r)
        dsp_acc[...] += jnp.sum(dlog_a * ((-LRU_C) * r), axis=0, keepdims=True)
        dzi = d_ig * ig * (1.0 - ig)
        dbrg_ref[...] += jnp.sum(dzr, axis=0, keepdims=True)
        dbig_ref[...] += jnp.sum(dzi, axis=0, keepdims=True)
        dzr_b, dzi_b = dzr.astype(BF16), dzi.astype(BF16)
        ub = gt["ub"]
        duc_parts = []
        for gi in range(LRU_BLOCKS):
            cs = slice(gi * LRU_BLOCK, (gi + 1) * LRU_BLOCK)
            dwrg_ref[gi] += lax.dot_general(ub[:, cs], dzr_b[:, cs], TN, preferred_element_type=F32)
            dwig_ref[gi] += lax.dot_general(ub[:, cs], dzi_b[:, cs], TN, preferred_element_type=F32)
            duc_parts.append(lax.dot_general(dzr_b[:, cs], wrg_ref[gi], NT, preferred_element_type=F32)
                             + lax.dot_general(dzi_b[:, cs], wig_ref[gi], NT, preferred_element_type=F32))
        duc = duc + jnp.concatenate(duc_parts, axis=1)

        dcb_ref[...] += jnp.sum(duc, axis=0, keepdims=True)
        taps = gt["taps"]
        for jt in range(CONV_WIDTH):
            dcw_ref[jt:jt + 1, :] += jnp.sum(duc * taps[jt], axis=0, keepdims=True)
        head = duc_head[...]
        du = duc * cw_ref[CONV_WIDTH - 1:CONV_WIDTH, :]
        for jt in range(CONV_WIDTH - 1):
            du = du + _shift_up(duc, head, CONV_WIDTH - 1 - jt) * cw_ref[jt:jt + 1, :]
        du_ref[...] = du.astype(BF16)
        duc_head[...] = duc[:SUBLANES, :]

        @pl.when(step == nblk - 1)
        def _():
            dlam_ref[...] = -dsp_acc[...] * jax.nn.sigmoid(-lam_ref[...])

    full2 = lambda shape: pl.BlockSpec(shape, lambda s: (0, 0))
    full3 = lambda shape: pl.BlockSpec(shape, lambda s: (0, 0, 0))
    blk = pl.BlockSpec((tb, LRU_WIDTH), lambda s: (nblk - 1 - s, 0))
    prev_tile = pl.BlockSpec((SUBLANES, LRU_WIDTH), lambda s: (jnp.maximum((nblk - 1 - s) * tiles - 1, 0), 0))
    wshape = (LRU_BLOCKS, LRU_BLOCK, LRU_BLOCK)
    return pl.pallas_call(
        body, name="rglru_bwd", grid=(nblk,),
        in_specs=[blk, prev_tile, blk, prev_tile, blk, full2((CONV_WIDTH, LRU_WIDTH)), full2((1, LRU_WIDTH)),
                  full3(wshape), full2((1, LRU_WIDTH)), full3(wshape), full2((1, LRU_WIDTH)), full2((1, LRU_WIDTH))],
        out_specs=[blk, full2((CONV_WIDTH, LRU_WIDTH)), full2((1, LRU_WIDTH)), full3(wshape), full2((1, LRU_WIDTH)),
                   full3(wshape), full2((1, LRU_WIDTH)), full2((1, LRU_WIDTH))],
        out_shape=[jax.ShapeDtypeStruct((rows, LRU_WIDTH), BF16),
                   jax.ShapeDtypeStruct((CONV_WIDTH, LRU_WIDTH), F32), jax.ShapeDtypeStruct((1, LRU_WIDTH), F32),
                   jax.ShapeDtypeStruct(wshape, F32), jax.ShapeDtypeStruct((1, LRU_WIDTH), F32),
                   jax.ShapeDtypeStruct(wshape, F32), jax.ShapeDtypeStruct((1, LRU_WIDTH), F32),
                   jax.ShapeDtypeStruct((1, LRU_WIDTH), F32)],
        scratch_shapes=[pltpu.VMEM((SUBLANES, LRU_WIDTH), F32), pltpu.VMEM((SUBLANES, LRU_WIDTH), F32),
                        pltpu.VMEM((tb, LRU_WIDTH), F32), pltpu.VMEM((tb, LRU_WIDTH), F32),
                        pltpu.VMEM((tb, LRU_WIDTH), F32), pltpu.VMEM((1, LRU_WIDTH), F32)],
        compiler_params=_cparams(("arbitrary",)),
    )(u, u, hs, hs, dhs, cw, cb, wrg, brg, wig, big, lam)


def _out_proj_loss(a, gate, h, w, gf, target, n_real):
    rows = h.shape[0]
    tr = _row_block(rows)

    def body(a_ref, gate_ref, h_ref, w_ref, g_ref, t_ref, dh_ref, loss_ref, dg_ref, da_ref, dgate_ref, dw_ref,
             dw_acc):
        i = pl.program_id(0)

        @pl.when(i == 0)
        def _():
            loss_ref[...] = jnp.zeros_like(loss_ref)
            dg_ref[...] = jnp.zeros_like(dg_ref)
            dw_acc[...] = jnp.zeros_like(dw_acc)

        gv = g_ref[...]
        av, gatev = a_ref[...], gate_ref[...]
        sg = _silu(gatev)
        y = (av * sg).astype(BF16)
        xn, r = _rms_fwd(h_ref[...] + jnp.dot(y, w_ref[...], preferred_element_type=F32))
        row = i * tr + lax.broadcasted_iota(jnp.int32, (tr, 1), 0)
        live = jnp.logical_and(row >= N_META, row < n_real)
        tgt = t_ref[...]
        tgt = jnp.where(i == 0, pltpu.roll(tgt, N_META, 0), tgt)
        err = jnp.where(live, xn * gv - tgt, 0.0)
        loss_ref[...] += (0.5 / D_MODEL) * jnp.sum(jnp.sum(err * err, axis=1, keepdims=True), axis=0, keepdims=True)
        dx, dg = _rms_bwd(err * (1.0 / D_MODEL), xn, r, gv)
        dg_ref[...] += dg
        dh_ref[...] = dx
        dhb = dx.astype(BF16)
        dw_acc[...] += lax.dot_general(y, dhb, TN, preferred_element_type=F32)
        dy = lax.dot_general(dhb, w_ref[...], NT, preferred_element_type=F32)
        da_ref[...] = dy * sg
        dgate_ref[...] = (dy * av * _dsilu(gatev)).astype(BF16)

        @pl.when(i == rows // tr - 1)
        def _():
            dw_ref[...] = dw_acc[...].astype(BF16)

    blk = pl.BlockSpec((tr, D_MODEL), lambda i: (i, 0))
    wblk = pl.BlockSpec((D_MODEL, D_MODEL), lambda i: (0, 0))
    window = pl.BlockSpec((pl.Element(tr, (0, rows - n_real)), pl.Element(D_MODEL)),
                          lambda i: (pl.multiple_of(jnp.maximum(i * tr - N_META, 0), SUBLANES), 0))
    return pl.pallas_call(
        body, name="b_out_loss", grid=(rows // tr,),
        in_specs=[blk, blk, blk, wblk, pl.BlockSpec((1, D_MODEL), lambda i: (0, 0)), window],
        out_specs=[blk, pl.BlockSpec((1, 1), lambda i: (0, 0)), pl.BlockSpec((1, D_MODEL), lambda i: (0, 0)),
                   blk, blk, wblk],
        out_shape=[jax.ShapeDtypeStruct((rows, D_MODEL), F32), jax.ShapeDtypeStruct((1, 1), F32),
                   jax.ShapeDtypeStruct((1, D_MODEL), F32), jax.ShapeDtypeStruct((rows, D_MODEL), F32),
                   jax.ShapeDtypeStruct((rows, D_MODEL), BF16), jax.ShapeDtypeStruct((D_MODEL, D_MODEL), BF16)],
        scratch_shapes=[pltpu.VMEM((D_MODEL, D_MODEL), F32)],
        compiler_params=_cparams(("arbitrary",)),
    )(a, gate, h, w, gf, target)


def _my_place():
    x, y, c = lax.axis_index("x"), lax.axis_index("y"), lax.axis_index("c")
    return x, y, c, 4 * x + 2 * y + c


def _peer(x, y, c, k):
    px, py, pc = x ^ (k >> 2), y ^ ((k >> 1) & 1), c ^ (k & 1)
    return (px, py, pc), 4 * px + 2 * py + pc


def _exchange_copies(src_of, dst_ref, send_sems, recv_sems, local_sem):
    x, y, c, me = _my_place()
    copies = [pltpu.make_async_copy(src_of(me), dst_ref.at[me], local_sem)]
    for k in range(1, N_DEV):
        peer, pid = _peer(x, y, c, k)
        copies.append(pltpu.make_async_remote_copy(
            src_ref=src_of(pid), dst_ref=dst_ref.at[me], send_sem=send_sems.at[k], recv_sem=recv_sems.at[k],
            device_id=peer, device_id_type=MESH))
    return copies


def _exchange_sems(nb):
    return [pltpu.SemaphoreType.DMA((nb, N_DEV)), pltpu.SemaphoreType.DMA((nb, N_DEV)), pltpu.SemaphoreType.DMA((nb,))]


def _sum_blocks(lands, name):
    n = len(lands)

    def body(*refs):
        for land_ref, out_ref in zip(refs[:n], refs[n:]):
            acc = land_ref[0].astype(F32)
            for d in range(1, N_DEV):
                acc = acc + land_ref[d].astype(F32)
            out_ref[...] = acc

    return pl.pallas_call(
        body, name=name, out_shape=[jax.ShapeDtypeStruct(l.shape[1:], F32) for l in lands],
        compiler_params=pltpu.CompilerParams(vmem_limit_bytes=VMEM_LIMIT),
    )(*lands)


def _two_level_gather(parts, send_sems, recv_sems, local_sems):
    x, y, c, _ = _my_place()
    me, sibling = (x, y, c), (x, y, 1 - c)
    chips = [(1 - x, y), (x, 1 - y), (1 - x, 1 - y)]
    nparts = range(len(parts))

    def slot(dst, place):
        return dst.at[4 * place[0] + 2 * place[1] + place[2]]

    def copy(part, k, block, to, first_hand=False):
        src, dst = parts[part]
        return pltpu.make_async_remote_copy(
            src_ref=src if first_hand else slot(dst, block), dst_ref=slot(dst, block),
            send_sem=send_sems.at[part, k], recv_sem=recv_sems.at[part, k], device_id=to, device_id_type=MESH)

    own = lambda: [pltpu.make_async_copy(src, slot(dst, me), local_sems.at[part])
                   for part, (src, dst) in enumerate(parts)]
    first = lambda: [cp for part in nparts for cp in
                     [copy(part, 0, me, sibling, True)] + [copy(part, 1 + j, me, (*chip, c), True)
                                                           for j, chip in enumerate(chips)]]
    passed = lambda: [copy(part, 4 + j, (*chip, c), sibling) for j, chip in enumerate(chips) for part in nparts]

    def issue():
        for cp in own() + first():
            cp.start()

    def forward():
        for j, chip in enumerate(chips):
            for part in nparts:
                copy(part, 1 + j, (*chip, c), me).wait_recv()
                copy(part, 4 + j, (*chip, c), sibling).start()

    def finish():
        for part in nparts:
            copy(part, 0, sibling, me).wait_recv()
            for j, chip in enumerate(chips):
                copy(part, 4 + j, (*chip, 1 - c), me).wait_recv()
        for cp in first() + passed():
            cp.wait_send()
        for cp in own():
            cp.wait()

    return issue, forward, finish


def _all_gather(big, small):
    def body(big_ref, small_ref, obig_ref, osmall_ref, send_sems, recv_sems, local_sems):
        issue, forward, finish = _two_level_gather(((big_ref, obig_ref), (small_ref, osmall_ref)), send_sems,
                                                   recv_sems, local_sems)
        issue()
        forward()
        finish()

    hbm = pl.BlockSpec(memory_space=pl.ANY)
    return pl.pallas_call(
        body, name="weight_all_gather",
        in_specs=[hbm, hbm], out_specs=[hbm, hbm],
        out_shape=[jax.ShapeDtypeStruct((N_DEV,) + big.shape, BF16), jax.ShapeDtypeStruct((N_DEV,) + small.shape, F32)],
        scratch_shapes=[pltpu.SemaphoreType.DMA((2, N_DEV)), pltpu.SemaphoreType.DMA((2, N_DEV)),
                        pltpu.SemaphoreType.DMA((2,))],
        compiler_params=pltpu.CompilerParams(has_side_effects=True),
    )(big, small)


GRAD_CHUNK = 32


def _grad_exchange(gbig, rep):
    n, width = gbig.shape[1:]
    nrep = rep.shape[0]
    n_chips = N_DEV // 2

    def body(gbig_ref, rep_ref, out_ref, orep_ref, pre, stage, got, own_sum, land_rep, send_sems, recv_sems,
             local_sem):
        x, y, c, me = _my_place()
        my_chip = 2 * x + y
        sibling = (x, y, 1 - c)

        local = pltpu.make_async_copy(rep_ref, land_rep.at[me], local_sem.at[0])
        local.start()
        rep_copies = []
        for k in range(1, N_DEV):
            peer, _ = _peer(x, y, c, k)
            rep_copies.append(pltpu.make_async_remote_copy(
                src_ref=rep_ref, dst_ref=land_rep.at[me], send_sem=send_sems.at[6 + k], recv_sem=recv_sems.at[6 + k],
                device_id=peer, device_id_type=MESH))
        swaps = [pltpu.make_async_remote_copy(
            src_ref=gbig_ref.at[2 * q + (1 - c)], dst_ref=pre.at[q], send_sem=send_sems.at[q], recv_sem=recv_sems.at[q],
            device_id=sibling, device_id_type=MESH) for q in range(n_chips)]
        for cp in rep_copies + swaps:
            cp.start()
        for cp in swaps:
            cp.wait_recv()

        def pair_sums(ci, carry):
            rs = pl.ds(pl.multiple_of(ci * GRAD_CHUNK, GRAD_CHUNK), GRAD_CHUNK)
            for q in range(n_chips):
                stage[q, rs, :] = (gbig_ref[2 * q + c, rs, :].astype(F32) + pre[q, rs, :].astype(F32)).astype(BF16)
            own_sum[rs, :] = gbig_ref[me, rs, :].astype(F32) + pre[my_chip, rs, :].astype(F32)
            return carry

        lax.fori_loop(0, n // GRAD_CHUNK, pair_sums, 0)

        hops = []
        for rel in range(1, n_chips):
            qx, qy = x ^ (rel >> 1), y ^ (rel & 1)
            hops.append(pltpu.make_async_remote_copy(
                src_ref=stage.at[2 * qx + qy], dst_ref=got.at[my_chip], send_sem=send_sems.at[3 + rel],
                recv_sem=recv_sems.at[3 + rel], device_id=(qx, qy, c), device_id_type=MESH))
        for cp in hops:
            cp.start()
        for cp in hops:
            cp.wait_recv()

        def chip_sums(ci, carry):
            rs = pl.ds(pl.multiple_of(ci * GRAD_CHUNK, GRAD_CHUNK), GRAD_CHUNK)
            mine = own_sum[rs, :]
            acc = jnp.where(my_chip == 0, mine, got[0, rs, :].astype(F32))
            for q in range(1, n_chips):
                acc = acc + jnp.where(my_chip == q, mine, got[q, rs, :].astype(F32))
            out_ref[rs, :] = acc
            return carry

        lax.fori_loop(0, n // GRAD_CHUNK, chip_sums, 0)

        for cp in rep_copies:
            cp.wait_recv()
        local.wait()
        acc = land_rep[0]
        for d in range(1, N_DEV):
            acc = acc + land_rep[d]
        orep_ref[...] = acc
        for cp in swaps + hops + rep_copies:
            cp.wait_send()

    return pl.pallas_call(
        body, name="grad_exchange",
        in_specs=[pl.BlockSpec(memory_space=pltpu.VMEM), pl.BlockSpec(memory_space=pltpu.VMEM)],
        out_specs=[pl.BlockSpec(memory_space=pltpu.VMEM), pl.BlockSpec(memory_space=pltpu.VMEM)],
        out_shape=[jax.ShapeDtypeStruct((n, width), F32), jax.ShapeDtypeStruct((nrep, LANES), F32)],
        scratch_shapes=[pltpu.VMEM((n_chips, n, width), BF16), pltpu.VMEM((n_chips, n, width), BF16),
                        pltpu.VMEM((n_chips, n, width), BF16), pltpu.VMEM((n, width), F32),
                        pltpu.VMEM((N_DEV, nrep, LANES), F32),
                        pltpu.SemaphoreType.DMA((2 * N_DEV - 2,)), pltpu.SemaphoreType.DMA((2 * N_DEV - 2,)),
                        pltpu.SemaphoreType.DMA((1,))],
        compiler_params=pltpu.CompilerParams(vmem_limit_bytes=VMEM_LIMIT, has_side_effects=True),
    )(gbig, rep)


def _adamw_all(ws, gs, ms, vs):
    n = len(ws)

    def body(*refs):
        w_refs, g_refs, m_refs, v_refs = refs[0:n], refs[n:2 * n], refs[2 * n:3 * n], refs[3 * n:4 * n]
        d_refs, nm_refs, nv_refs = refs[4 * n:5 * n], refs[5 * n:6 * n], refs[6 * n:7 * n]
        for w_ref, g_ref, m_ref, v_ref, d_ref, nm_ref, nv_ref in zip(w_refs, g_refs, m_refs, v_refs, d_refs, nm_refs, nv_refs):
            g = g_ref[...]
            m = ADAM_B1 * m_ref[...] + (1.0 - ADAM_B1) * g
            v = ADAM_B2 * v_ref[...] + (1.0 - ADAM_B2) * jnp.square(g)
            m_hat = m / (1.0 - ADAM_B1 ** ADAM_STEP)
            v_hat = v / (1.0 - ADAM_B2 ** ADAM_STEP)
            d_ref[...] = -ADAM_LR * (m_hat / (jnp.sqrt(v_hat) + ADAM_EPS) + ADAM_WD * w_ref[...])
            nm_ref[...] = m
            nv_ref[...] = v

    shapes = [jax.ShapeDtypeStruct(w.shape, F32) for w in ws]
    outs = pl.pallas_call(
        body, name="adamw", out_shape=shapes * 3,
        compiler_params=pltpu.CompilerParams(vmem_limit_bytes=VMEM_LIMIT),
    )(*ws, *gs, *ms, *vs)
    return outs[0:n], outs[n:2 * n], outs[2 * n:3 * n]


SMALL_A = (("meta_tokens", 16),)
SMALL_B = (("b_norm_g", 1), ("b_conv_w", 4), ("b_conv_b", 1), ("b_b_rg", 1), ("b_b_ig", 1), ("b_lam", 1))
REP = (("a_norm_g", 8), ("a_q_norm_g", 3), ("a_kv_norm_g", 2), ("final_norm_g", 8), ("loss", 1))
SLOT = 16


def _offsets(table, slot=1, start=0):
    out, o = {}, start
    for name, n in table:
        out[name] = (o, n)
        o += -(-n // slot) * slot
    return out, o


def _slotted(a, axis):
    pad = -a.shape[axis] % SLOT
    if not pad:
        return a
    widths = [(0, 0)] * a.ndim
    widths[axis] = (0, pad)
    return jnp.pad(a, widths)


def _rope_tables(rows):
    pos = np.arange(rows, dtype=np.float32)
    inv_freq = (np.float32(ROPE_BASE) ** (-np.arange(0, QK_ROPE, 2, dtype=np.float32) / np.float32(QK_ROPE))).astype(
        np.float32)
    ang = pos[:, None] * inv_freq[None, :]
    cos, sin = np.cos(ang).astype(np.float32), np.sin(ang).astype(np.float32)
    zeros = np.zeros((rows, LANES - QK_ROPE), np.float32)
    return jnp.asarray(np.concatenate([cos, cos, zeros], axis=1)), jnp.asarray(np.concatenate([-sin, sin, zeros], axis=1))


def kernel(x, meta_tokens, a_norm_g, a_w_in, a_q_norm_g, a_kv_norm_g, a_w_uq, a_w_ukv, a_w_out, b_norm_g, b_w_in, b_conv_w, b_conv_b, b_w_rg, b_b_rg, b_w_ig, b_b_ig, b_lam, b_w_out, final_norm_g, loss_target, m_meta_tokens, m_a_norm_g, m_a_w_in, m_a_q_norm_g, m_a_kv_norm_g, m_a_w_uq, m_a_w_ukv, m_a_w_out, m_b_norm_g, m_b_w_in, m_b_conv_w, m_b_conv_b, m_b_w_rg, m_b_b_rg, m_b_w_ig, m_b_b_ig, m_b_lam, m_b_w_out, m_final_norm_g, v_meta_tokens, v_a_norm_g, v_a_w_in, v_a_q_norm_g, v_a_kv_norm_g, v_a_w_uq, v_a_w_ukv, v_a_w_out, v_b_norm_g, v_b_w_in, v_b_conv_w, v_b_conv_b, v_b_w_rg, v_b_b_rg, v_b_w_ig, v_b_b_ig, v_b_lam, v_b_w_out, v_final_norm_g):
    names = ("meta_tokens", "a_norm_g", "a_w_in", "a_q_norm_g", "a_kv_norm_g", "a_w_uq", "a_w_ukv", "a_w_out",
             "b_norm_g", "b_w_in", "b_conv_w", "b_conv_b", "b_w_rg", "b_b_rg", "b_w_ig", "b_b_ig", "b_lam", "b_w_out",
             "final_norm_g")
    w = dict(zip(names, (meta_tokens, a_norm_g, a_w_in, a_q_norm_g, a_kv_norm_g, a_w_uq, a_w_ukv, a_w_out, b_norm_g,
                         b_w_in, b_conv_w, b_conv_b, b_w_rg, b_b_rg, b_w_ig, b_b_ig, b_lam, b_w_out, final_norm_g)))
    mom_m = dict(zip(names, (m_meta_tokens, m_a_norm_g, m_a_w_in, m_a_q_norm_g, m_a_kv_norm_g, m_a_w_uq, m_a_w_ukv,
                             m_a_w_out, m_b_norm_g, m_b_w_in, m_b_conv_w, m_b_conv_b, m_b_w_rg, m_b_b_rg, m_b_w_ig,
                             m_b_b_ig, m_b_lam, m_b_w_out, m_final_norm_g)))
    mom_v = dict(zip(names, (v_meta_tokens, v_a_norm_g, v_a_w_in, v_a_q_norm_g, v_a_kv_norm_g, v_a_w_uq, v_a_w_ukv,
                             v_a_w_out, v_b_norm_g, v_b_w_in, v_b_conv_w, v_b_conv_b, v_b_w_rg, v_b_b_rg, v_b_w_ig,
                             v_b_b_ig, v_b_lam, v_b_w_out, v_final_norm_g)))

    seq = x.shape[1]
    n_real = N_META + seq
    rows = -(-n_real // LANES) * LANES
    scale = (QK_NOPE + QK_ROPE) ** -0.5
    small_off, _ = _offsets(SMALL_A + SMALL_B, SLOT)
    gsmallb_off, _ = _offsets(SMALL_B, SLOT)
    rep_off, _ = _offsets(REP, SLOT)
    cdev_a = a_w_in.shape[-1]
    wide = 2 * LANES

    send_a0 = jnp.pad(a_w_in[0], ((0, 0), (0, wide - cdev_a))).astype(BF16)
    send_small = jnp.concatenate([_slotted(w[nm].reshape(-1, LANES), 0) for nm, _ in SMALL_A + SMALL_B], axis=0)
    sends_a1 = [jnp.pad(a_w_uq[0], ((0, 0), (0, HEAD_PAD - QK_NOPE - QK_ROPE))).astype(BF16), a_w_ukv[0].astype(BF16)]
    lru_rows = LRU_BLOCKS * LRU_BLOCK // N_DEV
    sends_b = [a_w_out[0].astype(BF16), b_w_in[0].astype(BF16), b_w_rg.reshape(lru_rows, LRU_BLOCK).astype(BF16),
               b_w_ig.reshape(lru_rows, LRU_BLOCK).astype(BF16), b_w_out[0].astype(BF16)]
    all_a0, all_small = _all_gather(send_a0, send_small)

    def small_seg(nm):
        o, n = small_off[nm]
        return all_small[:, o:o + n, :]

    w_in_a = all_a0[:, :, :cdev_a].transpose(1, 0, 2).reshape(D_MODEL, N_DEV * cdev_a)
    w_in_a = jnp.concatenate([w_in_a[:, :LAT + QK_ROPE], jnp.zeros((D_MODEL, LAT_PAD - LAT - QK_ROPE), BF16),
                              w_in_a[:, LAT + QK_ROPE:]], axis=1)[None]
    meta_full = small_seg("meta_tokens").transpose(1, 0, 2).reshape(N_META, D_MODEL)
    vec = lambda nm: small_seg(nm).reshape(1, D_MODEL)
    g_b, conv_b, b_rg, b_ig, lam = vec("b_norm_g"), vec("b_conv_b"), vec("b_b_rg"), vec("b_b_ig"), vec("b_lam")
    conv_w = small_seg("b_conv_w").transpose(1, 0, 2).reshape(CONV_WIDTH, LRU_WIDTH)
    g_a, g_q, g_kv = a_norm_g, a_q_norm_g, a_kv_norm_g
    g_f = final_norm_g.reshape(1, D_MODEL)

    cos, sin = _rope_tables(rows)

    h0, lat, gate_a, w_uq, w_ukv = _embed_norm_proj_fwd(x[0], meta_full, rows, g_a, w_in_a, LAT_PAD, "a_in_fwd",
                                                        sends_a1)
    qc, kc, v, vt = _mla_qkv_fwd(lat, g_q, g_kv, w_uq, w_ukv, cos, sin, scale)
    o, lse, w_out_a, w_in_b, w_rg, w_ig, w_out_b = _attn_fwd(qc, kc, vt, sends_b)

    lru_w = lambda g: g.reshape(N_DEV, LRU_BLOCKS, LRU_BLOCK // N_DEV, LRU_BLOCK).transpose(1, 0, 2, 3).reshape(
        LRU_BLOCKS, LRU_BLOCK, LRU_BLOCK)
    w_out_a, w_out_b = w_out_a.reshape(D_MODEL, D_MODEL), w_out_b.reshape(D_MODEL, D_MODEL)
    w_rg, w_ig = lru_w(w_rg), lru_w(w_ig)

    h1, u, gate_b = _out_proj_in_proj(o, gate_a, h0, w_out_a, g_b, w_in_b, LRU_WIDTH, "a_out_b_in_fwd")
    hs = _rglru_fwd(u, conv_w, conv_b, w_rg, b_rg, w_ig, b_ig, lam)
    dh2, loss_part, dg_f, dhs, dgate_b, dw_out_b = _out_proj_loss(hs, gate_b, h1, w_out_b, g_f, loss_target[0],
                                                                   n_real)

    du, dconv_w, dconv_b, dw_rg, db_rg, dw_ig, db_ig, dlam = _rglru_bwd(u, hs, dhs, conv_w, conv_b, w_rg, b_rg, w_ig,
                                                                       b_ig, lam)
    dh1, dw_in_b, dg_b = _norm_proj_bwd(h1, g_b, w_in_b, du, dgate_b, dh2, "b_in_bwd")
    do, dgate_a, dw_out_a, delta = _attn_out_bwd(o, gate_a, dh1, w_out_a)

    def to_cols(g, cdev):
        r = g.shape[0]
        return g.reshape(r, N_DEV, cdev).transpose(1, 0, 2).reshape(N_DEV, -1, LANES)

    lru_g = lambda g: g.reshape(LRU_BLOCKS, N_DEV, LRU_BLOCK // N_DEV, LRU_BLOCK).transpose(1, 0, 2, 3).reshape(
        N_DEV, lru_rows, LRU_BLOCK)
    small_b = {"b_norm_g": dg_b, "b_conv_w": dconv_w, "b_conv_b": dconv_b, "b_b_rg": db_rg, "b_b_ig": db_ig,
               "b_lam": dlam}
    gsends_b = [dw_out_a.reshape(N_DEV, -1, D_MODEL), dw_in_b, lru_g(dw_rg).astype(BF16), lru_g(dw_ig).astype(BF16),
                dw_out_b.reshape(N_DEV, -1, D_MODEL),
                jnp.concatenate([_slotted(to_cols(small_b[nm], LANES).astype(BF16), 1) for nm, _ in SMALL_B], axis=1)]

    dqc, dkc, dv, *lands_b = _attn_bwd(qc, kc, v, lse, delta, do, gsends_b)
    g_out_a, g_in_b, g_rg, g_ig, g_out_b, gsum_small_b = _sum_blocks(lands_b, "sum_blocks_b")
    dlat, dw_uq, dw_ukv, dg_q, dg_kv = _mla_qkv_bwd(lat, g_q, g_kv, w_uq, w_ukv, cos, sin, dqc, dkc, dv, scale)
    dh0, dw_in_a, dg_a, *lands_a1 = _norm_proj_bwd(h0, g_a, w_in_a, dlat, dgate_a, dh1, "a_in_bwd", [dw_uq, dw_ukv])
    g_uq, g_ukv = _sum_blocks(lands_a1, "sum_blocks_a1")

    grad_x = dh0[N_META:n_real][None]

    dw_in_a_nat = jnp.concatenate([dw_in_a[0, :, :LAT + QK_ROPE], dw_in_a[0, :, LAT_PAD:]], axis=1)
    in_lanes = lambda g, cdev: jnp.pad(g.reshape(g.shape[0], N_DEV, cdev).transpose(1, 0, 2),
                                       ((0, 0), (0, 0), (0, wide - cdev)))
    pieces = [in_lanes(dw_in_a_nat, cdev_a), in_lanes(dh0[:N_META].astype(BF16), LANES)]
    used = sum(p.shape[1] for p in pieces)
    pieces.append(jnp.zeros((N_DEV, -used % GRAD_CHUNK, wide), BF16))
    gsend_a0 = jnp.concatenate(pieces, axis=1)
    rep_parts = {"a_norm_g": dg_a, "a_q_norm_g": dg_q, "a_kv_norm_g": dg_kv, "final_norm_g": dg_f,
                 "loss": jnp.broadcast_to(loss_part, (1, LANES))}
    rep = jnp.concatenate([_slotted(rep_parts[nm].reshape(-1, LANES), 0) for nm, _ in REP], axis=0)
    gsum_a0, rep_sum = _grad_exchange(gsend_a0, rep)

    grads = {"a_w_out": g_out_a, "b_w_in": g_in_b, "b_w_rg": g_rg, "b_w_ig": g_ig, "b_w_out": g_out_b,
             "a_w_uq": g_uq[:, :QK_NOPE + QK_ROPE], "a_w_ukv": g_ukv}
    grads = {nm: g.reshape(w[nm].shape) for nm, g in grads.items()}
    grads["a_w_in"] = gsum_a0[:D_MODEL, :cdev_a].reshape(w["a_w_in"].shape)
    grads["meta_tokens"] = gsum_a0[D_MODEL:D_MODEL + N_META, :LANES]
    for off, src in ((gsmallb_off, gsum_small_b), (rep_off, rep_sum)):
        for nm, (o_r, n) in off.items():
            if nm in w:
                grads[nm] = src[o_r:o_r + n].reshape(w[nm].shape)
    loss = rep_sum[rep_off["loss"][0], 0]

    as2d = lambda a: a.reshape(-1, a.shape[-1])
    deltas, new_ms, new_vs = _adamw_all([as2d(w[nm]) for nm in names], [as2d(grads[nm]) for nm in names],
                                        [as2d(mom_m[nm]) for nm in names], [as2d(mom_v[nm]) for nm in names])
    shaped = lambda arrs: [a.reshape(w[nm].shape) for a, nm in zip(arrs, names)]
    return (loss, grad_x, *[grads[nm] for nm in names], *shaped(deltas), *shaped(new_ms), *shaped(new_vs))
```

```python
import functools

import numpy as np
import jax
import jax.numpy as jnp
from jax import lax
from jax.experimental import pallas as pl
from jax.experimental.pallas import tpu as pltpu

F32 = jnp.float32
BF16 = jnp.bfloat16

D_MODEL = 1024
N_META = 16
RMS_EPS = 1e-6
HEADS = 8
QK_NOPE = 128
QK_ROPE = 64
V_HEAD = 128
Q_LORA = 384
KV_LORA = 256
HEAD_PAD = 256
LAT = Q_LORA + KV_LORA
LAT_PAD = LAT + 128
ROPE_BASE = 10000.0
MASK_VALUE = -1e30
LRU_WIDTH = 1024
LRU_BLOCKS = 4
LRU_BLOCK = 256
CONV_WIDTH = 4
LRU_C = 8.0
N_DEV = 8
ADAM_LR, ADAM_B1, ADAM_B2, ADAM_EPS, ADAM_WD, ADAM_STEP = 0.001, 0.9, 0.999, 1e-08, 0.01, 10

LANES = 128
SUBLANES = 8
VMEM_LIMIT = 56 * 1024 * 1024
MESH = pl.DeviceIdType.MESH

NT = (((1,), (1,)), ((), ()))
TN = (((0,), (0,)), ((), ()))


def _row_block(rows):
    return 384 if rows % 384 == 0 else 128


def _cparams(sem):
    return pltpu.CompilerParams(dimension_semantics=sem, vmem_limit_bytes=VMEM_LIMIT)


def _silu(x):
    return x * jax.nn.sigmoid(x)


def _dsilu(x):
    s = jax.nn.sigmoid(x)
    return s * (1.0 + x * (1.0 - s))


def _rms_fwd(x):
    r = lax.rsqrt(jnp.mean(x * x, axis=-1, keepdims=True) + RMS_EPS)
    return x * r, r


def _rms_bwd(dy, xn, r, g):
    t = dy * g
    dx = r * (t - xn * jnp.mean(t * xn, axis=-1, keepdims=True))
    return dx, jnp.sum(dy * xn, axis=0, keepdims=True)


def _expm1_neg(x):
    small = x * (1.0 + x * (1 / 2 + x * (1 / 6 + x * (1 / 24))))
    return jnp.where(x > -0.05, small, jnp.exp(x) - 1.0)


def _softplus_neg(lam):
    z = jnp.exp(-jnp.abs(lam))
    w = z / (2.0 + z)
    w2 = w * w
    series = 2.0 * w * (1.0 + w2 * (1 / 3) + w2 * w2 * (1 / 5))
    return jnp.maximum(-lam, 0.0) + jnp.where(z < 0.1, series, jnp.log(1.0 + z))


def _rider(sends, refs, first, last, all_to_all):
    nb = len(sends)
    if not nb:
        return (lambda: None), (lambda: None)
    send_refs, result_refs = refs[:nb], refs[nb:2 * nb]
    send_sems, recv_sems, local_sems = refs[2 * nb:]
    pick = (lambda ref: (lambda d: ref.at[d])) if all_to_all else (lambda ref: (lambda d: ref))

    def copies():
        out = []
        for b in range(nb):
            out += _exchange_copies(pick(send_refs[b]), result_refs[b], send_sems.at[b], recv_sems.at[b],
                                    local_sems.at[b])
        return out

    def start():
        @pl.when(first)
        def _():
            for cp in copies():
                cp.start()

    def wait():
        @pl.when(last)
        def _():
            for cp in copies():
                cp.wait()

    return start, wait


def _rider_specs(sends, all_to_all):
    nb = len(sends)
    if not nb:
        return [], [], [], []
    hbm = pl.BlockSpec(memory_space=pl.ANY)
    shapes = [jax.ShapeDtypeStruct(s.shape if all_to_all else (N_DEV,) + s.shape, s.dtype) for s in sends]
    return [hbm] * nb, [hbm] * nb, shapes, _exchange_sems(nb)


def _proj_blocks(x, w_ref):
    return jnp.concatenate([jnp.dot(x, w_ref[d], preferred_element_type=F32) for d in range(w_ref.shape[0])], axis=1)


def _embed_norm_proj_fwd(x, meta, rows, g, w, n1, name, wsends=()):
    n_real = N_META + x.shape[0]
    nb, _, cb = w.shape
    n = nb * cb
    tr = _row_block(rows)
    nsteps = rows // tr
    extra = len(wsends)

    def body(x_ref, meta_ref, g_ref, w_ref, *rest):
        h_ref, p1_ref, p2_ref = rest[extra:extra + 3]
        i = pl.program_id(0)
        issue, forward, finish = _two_level_gather(
            tuple(zip(rest[:extra], rest[extra + 3:2 * extra + 3])), *rest[2 * extra + 3:])
        pl.when(i == 0)(issue)
        pl.when(i == nsteps // 2)(forward)
        xw = x_ref[...]
        xw = jnp.where(i == 0, pltpu.roll(xw, N_META, 0), xw)
        row = i * tr + lax.broadcasted_iota(jnp.int32, (tr, 1), 0)
        meta_rows = jnp.concatenate([meta_ref[...], jnp.zeros((tr - N_META, D_MODEL), F32)], axis=0)
        h = jnp.where(row < N_META, meta_rows, jnp.where(row < n_real, xw, 0.0))
        h_ref[...] = h
        xn, _ = _rms_fwd(h)
        p = _proj_blocks((xn * g_ref[...]).astype(BF16), w_ref)
        p1_ref[...] = p[:, :n1]
        p2_ref[...] = p[:, n1:].astype(BF16)
        pl.when(i == nsteps - 1)(finish)

    assert nsteps >= 3, "the three phases of the riding gather need three grid steps"
    r_in, r_out, r_shape, r_scratch = _rider_specs(wsends, False)
    window = pl.BlockSpec((pl.Element(tr, (0, rows - n_real)), pl.Element(D_MODEL)),
                          lambda i: (pl.multiple_of(jnp.maximum(i * tr - N_META, 0), SUBLANES), 0))
    return pl.pallas_call(
        body, name=name, grid=(nsteps,),
        in_specs=[window,
                  pl.BlockSpec((N_META, D_MODEL), lambda i: (0, 0)),
                  pl.BlockSpec((1, D_MODEL), lambda i: (0, 0)),
                  pl.BlockSpec((nb, D_MODEL, cb), lambda i: (0, 0, 0))] + r_in,
        out_specs=[pl.BlockSpec((tr, D_MODEL), lambda i: (i, 0)),
                   pl.BlockSpec((tr, n1), lambda i: (i, 0)),
                   pl.BlockSpec((tr, n - n1), lambda i: (i, 0))] + r_out,
        out_shape=[jax.ShapeDtypeStruct((rows, D_MODEL), F32), jax.ShapeDtypeStruct((rows, n1), F32),
                   jax.ShapeDtypeStruct((rows, n - n1), BF16)] + r_shape,
        scratch_shapes=r_scratch,
        compiler_params=_cparams(("arbitrary",)),
    )(x, meta, g, w, *wsends)


def _out_proj_in_proj(a, gate, h, w_out, g, w_in, n1, name):
    rows = h.shape[0]
    nb, _, cb = w_in.shape
    n = nb * cb
    tr = _row_block(rows)

    def body(a_ref, gate_ref, h_ref, wo_ref, g_ref, wi_ref, hn_ref, p1_ref, p2_ref):
        y = (a_ref[...] * _silu(gate_ref[...].astype(F32))).astype(BF16)
        h_new = h_ref[...] + jnp.dot(y, wo_ref[...], preferred_element_type=F32)
        hn_ref[...] = h_new
        xn, _ = _rms_fwd(h_new)
        p = _proj_blocks((xn * g_ref[...]).astype(BF16), wi_ref)
        p1_ref[...] = p[:, :n1]
        p2_ref[...] = p[:, n1:].astype(BF16)

    blk = pl.BlockSpec((tr, D_MODEL), lambda i: (i, 0))
    return pl.pallas_call(
        body, name=name, grid=(rows // tr,),
        in_specs=[blk, blk, blk, pl.BlockSpec((D_MODEL, D_MODEL), lambda i: (0, 0)),
                  pl.BlockSpec((1, D_MODEL), lambda i: (0, 0)), pl.BlockSpec((nb, D_MODEL, cb), lambda i: (0, 0, 0))],
        out_specs=[blk, pl.BlockSpec((tr, n1), lambda i: (i, 0)), pl.BlockSpec((tr, n - n1), lambda i: (i, 0))],
        out_shape=[jax.ShapeDtypeStruct((rows, D_MODEL), F32), jax.ShapeDtypeStruct((rows, n1), F32),
                   jax.ShapeDtypeStruct((rows, n - n1), BF16)],
        compiler_params=_cparams(("parallel",)),
    )(a, gate, h, w_out, g, w_in)


def _norm_proj_bwd(h, g, w, dp1, dp2, dh_in, name, gsends=()):
    rows = h.shape[0]
    nb, _, cb = w.shape
    n1 = dp1.shape[1]
    n2 = nb * cb - n1
    tr = _row_block(rows)
    nsteps = rows // tr
    extra = len(gsends)

    def body(h_ref, g_ref, w_ref, dp1_ref, dp2_ref, dhin_ref, *rest):
        dh_ref, dw_ref, dg_ref = rest[extra:extra + 3]
        dw_acc = rest[2 * extra + 3]
        i = pl.program_id(0)
        start, wait = _rider(gsends, rest[:extra] + rest[extra + 3:2 * extra + 3] + rest[2 * extra + 4:],
                             i == 0, i == nsteps - 1, True)
        start()

        @pl.when(i == 0)
        def _():
            dw_acc[...] = jnp.zeros_like(dw_acc)
            dg_ref[...] = jnp.zeros_like(dg_ref)

        gv = g_ref[...]
        xn, r = _rms_fwd(h_ref[...])
        hn = (xn * gv).astype(BF16)
        dp = jnp.concatenate([dp1_ref[...].astype(BF16), dp2_ref[...].astype(BF16)], axis=1)
        dhn = jnp.zeros((tr, D_MODEL), F32)
        for d in range(nb):
            dpd = dp[:, d * cb:(d + 1) * cb]
            dw_acc[d] += lax.dot_general(hn, dpd, TN, preferred_element_type=F32)
            dhn = dhn + lax.dot_general(dpd, w_ref[d], NT, preferred_element_type=F32)
        dx, dg = _rms_bwd(dhn, xn, r, gv)
        dg_ref[...] += dg
        dh_ref[...] = dhin_ref[...] + dx

        @pl.when(i == nsteps - 1)
        def _():
            dw_ref[...] = dw_acc[...].astype(BF16)

        wait()

    r_in, r_out, r_shape, r_scratch = _rider_specs(gsends, True)
    wblk = pl.BlockSpec((nb, D_MODEL, cb), lambda i: (0, 0, 0))
    return pl.pallas_call(
        body, name=name, grid=(nsteps,),
        in_specs=[pl.BlockSpec((tr, D_MODEL), lambda i: (i, 0)),
                  pl.BlockSpec((1, D_MODEL), lambda i: (0, 0)),
                  wblk,
                  pl.BlockSpec((tr, n1), lambda i: (i, 0)),
                  pl.BlockSpec((tr, n2), lambda i: (i, 0)),
                  pl.BlockSpec((tr, D_MODEL), lambda i: (i, 0))] + r_in,
        out_specs=[pl.BlockSpec((tr, D_MODEL), lambda i: (i, 0)), wblk,
                   pl.BlockSpec((1, D_MODEL), lambda i: (0, 0))] + r_out,
        out_shape=[jax.ShapeDtypeStruct((rows, D_MODEL), F32),
                   jax.ShapeDtypeStruct((nb, D_MODEL, cb), BF16),
                   jax.ShapeDtypeStruct((1, D_MODEL), F32)] + r_shape,
        scratch_shapes=[pltpu.VMEM((nb, D_MODEL, cb), F32)] + r_scratch,
        compiler_params=_cparams(("arbitrary",)),
    )(h, g, w, dp1, dp2, dh_in, *gsends)


def _attn_out_bwd(o, gate, dh, w):
    rows = o.shape[0]
    tr = _row_block(rows)
    nsteps = rows // tr

    def body(o_ref, gate_ref, dh_ref, w_ref, do_ref, dgate_ref, dw_ref, delta_ref, dw_acc):
        i = pl.program_id(0)

        @pl.when(i == 0)
        def _():
            dw_acc[...] = jnp.zeros_like(dw_acc)

        ov, gv = o_ref[...], gate_ref[...].astype(F32)
        sg = _silu(gv)
        dhb = dh_ref[...].astype(BF16)
        dw_acc[...] += lax.dot_general((ov * sg).astype(BF16), dhb, TN, preferred_element_type=F32)
        dy = lax.dot_general(dhb, w_ref[...], NT, preferred_element_type=F32)
        do = (dy * sg).astype(BF16)
        do_ref[...] = do
        dgate_ref[...] = (dy * ov * _dsilu(gv)).astype(BF16)
        prod = do.astype(F32) * ov
        lane = lax.broadcasted_iota(jnp.int32, (tr, LANES), 1)
        per_head = jnp.zeros((tr, LANES), F32)
        for hd in range(HEADS):
            dsum = jnp.sum(prod[:, hd * V_HEAD:(hd + 1) * V_HEAD], axis=1, keepdims=True)
            per_head = jnp.where(lane == hd, dsum, per_head)
        delta_t = per_head.T
        for hd in range(HEADS):
            delta_ref[hd, 0] = delta_t[hd:hd + 1, :]

        @pl.when(i == nsteps - 1)
        def _():
            dw_ref[...] = dw_acc[...].astype(BF16)

    blk = pl.BlockSpec((tr, D_MODEL), lambda i: (i, 0))
    wblk = pl.BlockSpec((D_MODEL, D_MODEL), lambda i: (0, 0))
    return pl.pallas_call(
        body, name="a_out_bwd", grid=(nsteps,),
        in_specs=[blk, blk, blk, wblk],
        out_specs=[blk, blk, wblk, pl.BlockSpec((HEADS, 1, 1, tr), lambda i: (0, i, 0, 0))],
        out_shape=[jax.ShapeDtypeStruct((rows, D_MODEL), BF16), jax.ShapeDtypeStruct((rows, D_MODEL), BF16),
                   jax.ShapeDtypeStruct((D_MODEL, D_MODEL), BF16),
                   jax.ShapeDtypeStruct((HEADS, nsteps, 1, tr), F32)],
        scratch_shapes=[pltpu.VMEM((D_MODEL, D_MODEL), F32)],
        compiler_params=_cparams(("arbitrary",)),
    )(o, gate, dh, w)


def _rope(v, cos, sin, lane):
    swapped = jnp.where(lane < QK_ROPE // 2, pltpu.roll(v, LANES - QK_ROPE // 2, 1), pltpu.roll(v, QK_ROPE // 2, 1))
    return v * cos + swapped * sin


def _unrope(dv, cos, sin, lane):
    t = dv * sin
    swapped = jnp.where(lane < QK_ROPE // 2, pltpu.roll(t, LANES - QK_ROPE // 2, 1), pltpu.roll(t, QK_ROPE // 2, 1))
    return dv * cos + swapped


def _mla_qkv_fwd(lat, gq, gkv, wuq, wukv, cos, sin, scale):
    rows = lat.shape[0]
    tr = _row_block(rows)

    def body(lat_ref, gq_ref, gkv_ref, wuq_ref, wukv_ref, cos_ref, sin_ref, qc_ref, kc_ref, v_ref, vt_ref):
        qn, _ = _rms_fwd(lat_ref[:, :Q_LORA])
        kvn, _ = _rms_fwd(lat_ref[:, Q_LORA:LAT])
        qnb = (qn * gq_ref[...]).astype(BF16)
        kvnb = (kvn * gkv_ref[...]).astype(BF16)
        c, s = cos_ref[...], sin_ref[...]
        lane = lax.broadcasted_iota(jnp.int32, (tr, LANES), 1)
        kr = _rope(lat_ref[:, LAT:LAT_PAD], c, s, lane).astype(BF16)
        for hd in range(HEADS):
            o = hd * HEAD_PAD
            q = jnp.dot(qnb, wuq_ref[hd], preferred_element_type=F32)
            kv = jnp.dot(kvnb, wukv_ref[hd], preferred_element_type=F32)
            qc_ref[:, o:o + QK_NOPE] = (q[:, :QK_NOPE] * scale).astype(BF16)
            qc_ref[:, o + QK_NOPE:o + HEAD_PAD] = (_rope(q[:, QK_NOPE:], c, s, lane) * scale).astype(BF16)
            kc_ref[:, o:o + QK_NOPE] = kv[:, :QK_NOPE].astype(BF16)
            kc_ref[:, o + QK_NOPE:o + HEAD_PAD] = kr
            vh = kv[:, QK_NOPE:]
            v_ref[:, hd * V_HEAD:(hd + 1) * V_HEAD] = vh.astype(BF16)
            vt_ref[hd, 0] = vh.T.astype(BF16)

    full = lambda shape: pl.BlockSpec(shape, lambda i: (0,) * len(shape))
    rowb = lambda n: pl.BlockSpec((tr, n), lambda i: (i, 0))
    return pl.pallas_call(
        body, name="mla_qkv_fwd", grid=(rows // tr,),
        in_specs=[rowb(LAT_PAD), full((1, Q_LORA)), full((1, KV_LORA)), full((HEADS, Q_LORA, HEAD_PAD)),
                  full((HEADS, KV_LORA, HEAD_PAD)), rowb(LANES), rowb(LANES)],
        out_specs=[rowb(HEADS * HEAD_PAD), rowb(HEADS * HEAD_PAD), rowb(HEADS * V_HEAD),
                   pl.BlockSpec((HEADS, 1, V_HEAD, tr), lambda i: (0, i, 0, 0))],
        out_shape=[jax.ShapeDtypeStruct((rows, HEADS * HEAD_PAD), BF16),
                   jax.ShapeDtypeStruct((rows, HEADS * HEAD_PAD), BF16),
                   jax.ShapeDtypeStruct((rows, HEADS * V_HEAD), BF16),
                   jax.ShapeDtypeStruct((HEADS, rows // tr, V_HEAD, tr), BF16)],
        compiler_params=_cparams(("parallel",)),
    )(lat, gq, gkv, wuq, wukv, cos, sin)


def _mla_qkv_bwd(lat, gq, gkv, wuq, wukv, cos, sin, dqc, dkc, dv, scale):
    rows = lat.shape[0]
    tr = _row_block(rows)
    nsteps = rows // tr

    def body(lat_ref, gq_ref, gkv_ref, wuq_ref, wukv_ref, cos_ref, sin_ref, dqc_ref, dkc_ref, dv_ref,
             dlat_ref, dwuq_out, dwukv_out, dgq_ref, dgkv_ref, dwuq_ref, dwukv_ref):
        @pl.when(pl.program_id(0) == 0)
        def _():
            dwuq_ref[...] = jnp.zeros_like(dwuq_ref)
            dwukv_ref[...] = jnp.zeros_like(dwukv_ref)
            dgq_ref[...] = jnp.zeros_like(dgq_ref)
            dgkv_ref[...] = jnp.zeros_like(dgkv_ref)

        c, s = cos_ref[...], sin_ref[...]
        lane = lax.broadcasted_iota(jnp.int32, (tr, LANES), 1)
        gqv, gkvv = gq_ref[...], gkv_ref[...]
        qn, rq = _rms_fwd(lat_ref[:, :Q_LORA])
        kvn, rkv = _rms_fwd(lat_ref[:, Q_LORA:LAT])
        qnb = (qn * gqv).astype(BF16)
        kvnb = (kvn * gkvv).astype(BF16)
        dkr = jnp.zeros((tr, LANES), F32)
        dqn = jnp.zeros((tr, Q_LORA), F32)
        dkvn = jnp.zeros((tr, KV_LORA), F32)
        for hd in range(HEADS):
            o = hd * HEAD_PAD
            dq = jnp.concatenate(
                [dqc_ref[:, o:o + QK_NOPE],
                 _unrope(dqc_ref[:, o + QK_NOPE:o + HEAD_PAD].astype(F32), c, s, lane).astype(BF16)], axis=1)
            dkv = jnp.concatenate([dkc_ref[:, o:o + QK_NOPE], dv_ref[:, hd * V_HEAD:(hd + 1) * V_HEAD]], axis=1)
            dkr = dkr + dkc_ref[:, o + QK_NOPE:o + HEAD_PAD].astype(F32)
            dwuq_ref[hd] += scale * lax.dot_general(qnb, dq, TN, preferred_element_type=F32)
            dwukv_ref[hd] += lax.dot_general(kvnb, dkv, TN, preferred_element_type=F32)
            dqn = dqn + lax.dot_general(dq, wuq_ref[hd], NT, preferred_element_type=F32)
            dkvn = dkvn + lax.dot_general(dkv, wukv_ref[hd], NT, preferred_element_type=F32)
        dqn = scale * dqn
        dqlat, dgq = _rms_bwd(dqn, qn, rq, gqv)
        dkvlat, dgkv = _rms_bwd(dkvn, kvn, rkv, gkvv)
        dgq_ref[...] += dgq
        dgkv_ref[...] += dgkv
        dlat_ref[:, :Q_LORA] = dqlat.astype(BF16)
        dlat_ref[:, Q_LORA:LAT] = dkvlat.astype(BF16)
        dlat_ref[:, LAT:LAT_PAD] = _unrope(dkr, c, s, lane).astype(BF16)

        @pl.when(pl.program_id(0) == nsteps - 1)
        def _():
            dwuq_out[...] = dwuq_ref[...].astype(BF16)
            dwukv_out[...] = dwukv_ref[...].astype(BF16)

    full = lambda shape: pl.BlockSpec(shape, lambda i: (0,) * len(shape))
    rowb = lambda n: pl.BlockSpec((tr, n), lambda i: (i, 0))
    return pl.pallas_call(
        body, name="mla_qkv_bwd", grid=(nsteps,),
        in_specs=[rowb(LAT_PAD), full((1, Q_LORA)), full((1, KV_LORA)), full((HEADS, Q_LORA, HEAD_PAD)),
                  full((HEADS, KV_LORA, HEAD_PAD)), rowb(LANES), rowb(LANES),
                  rowb(HEADS * HEAD_PAD), rowb(HEADS * HEAD_PAD), rowb(HEADS * V_HEAD)],
        out_specs=[rowb(LAT_PAD), full((HEADS, Q_LORA, HEAD_PAD)), full((HEADS, KV_LORA, HEAD_PAD)),
                   full((1, Q_LORA)), full((1, KV_LORA))],
        out_shape=[jax.ShapeDtypeStruct((rows, LAT_PAD), BF16),
                   jax.ShapeDtypeStruct((HEADS, Q_LORA, HEAD_PAD), BF16),
                   jax.ShapeDtypeStruct((HEADS, KV_LORA, HEAD_PAD), BF16),
                   jax.ShapeDtypeStruct((1, Q_LORA), F32),
                   jax.ShapeDtypeStruct((1, KV_LORA), F32)],
        scratch_shapes=[pltpu.VMEM((HEADS, Q_LORA, HEAD_PAD), F32), pltpu.VMEM((HEADS, KV_LORA, HEAD_PAD), F32)],
        compiler_params=_cparams(("arbitrary",)),
    )(lat, gq, gkv, wuq, wukv, cos, sin, dqc, dkc, dv)


ATTN_UNROLL = 4
ATTN_UNROLL_BWD = 4
ATTN_HEADS = 2
ATTN_HEADS_FWD = 4


def _causal_mask_t(t):
    key = lax.broadcasted_iota(jnp.int32, (t, t), 0)
    query = lax.broadcasted_iota(jnp.int32, (t, t), 1)
    return key <= query


def _attn_fwd(qc, kc, vt, wsends):
    rows = qc.shape[0]
    t = _row_block(rows)
    nblk = rows // t
    nw = len(wsends)

    def body(q_ref, k_ref, vt_ref, *rest):
        o_ref, lse_ref = rest[nw:nw + 2]
        m_ref, l_ref, acc_ref, st_a, st_b = rest[2 * nw + 2:2 * nw + 7]
        i = pl.program_id(1)
        start, wait = _rider(wsends, rest[:nw] + rest[nw + 2:2 * nw + 2] + rest[2 * nw + 7:],
                             jnp.logical_and(pl.program_id(0) == 0, i == 0),
                             jnp.logical_and(pl.program_id(0) == HEADS // ATTN_HEADS_FWD - 1, i == nblk - 1), False)
        start()

        m_ref[...] = jnp.full_like(m_ref, MASK_VALUE)
        l_ref[...] = jnp.zeros_like(l_ref)
        acc_ref[...] = jnp.zeros_like(acc_ref)
        heads = range(ATTN_HEADS_FWD)
        qs = [q_ref[:, hh * HEAD_PAD:(hh + 1) * HEAD_PAD] for hh in heads]

        def scores(j, hh, st_ref):
            rs = pl.ds(pl.multiple_of(j * t, t), t)
            st_ref[hh] = lax.dot_general(k_ref[rs, hh * HEAD_PAD:(hh + 1) * HEAD_PAD], qs[hh], NT,
                                         preferred_element_type=F32)

        def consume(j, hh, st_ref, masked):
            st = st_ref[hh]
            if masked:
                st = jnp.where(_causal_mask_t(t), st, MASK_VALUE)
            m_prev = m_ref[hh]
            m_new = jnp.maximum(m_prev, jnp.max(st, axis=0, keepdims=True))
            alpha = jnp.exp(m_prev - m_new)
            pt = jnp.exp(st - m_new)
            l_ref[hh] = alpha * l_ref[hh] + jnp.sum(pt, axis=0, keepdims=True)
            acc_ref[hh] = alpha * acc_ref[hh] + jnp.dot(vt_ref[hh, j], pt.astype(BF16), preferred_element_type=F32)
            m_ref[hh] = m_new

        bufs = (st_a, st_b)

        def step(j, parity, issue_next, masked):
            if issue_next:
                for hh in heads:
                    scores(j + 1, hh, bufs[1 - parity])
            for hh in heads:
                consume(j, hh, bufs[parity], masked)

        for hh in heads:
            scores(0, hh, st_a)

        def trip(it, carry):
            for u in range(ATTN_UNROLL):
                step(it * ATTN_UNROLL + u, u % 2, True, False)
            return carry

        trips = i // ATTN_UNROLL
        lax.fori_loop(0, trips, trip, 0)
        j0 = trips * ATTN_UNROLL
        for left in range(1, ATTN_UNROLL + 1):
            @pl.when(i + 1 - j0 == left)
            def _(left=left):
                for u in range(left):
                    step(j0 + u, u % 2, u < left - 1, u == left - 1)

        for hh in heads:
            o_ref[:, hh * V_HEAD:(hh + 1) * V_HEAD] = (acc_ref[hh] / l_ref[hh]).T
            lse_ref[hh, 0] = m_ref[hh] + jnp.log(l_ref[hh])
        wait()

    r_in, r_out, r_shape, r_scratch = _rider_specs(wsends, False)
    nh = ATTN_HEADS_FWD
    return pl.pallas_call(
        body, name="attn_fwd", grid=(HEADS // nh, nblk),
        in_specs=[pl.BlockSpec((t, nh * HEAD_PAD), lambda g, i: (i, g)),
                  pl.BlockSpec((rows, nh * HEAD_PAD), lambda g, i: (0, g)),
                  pl.BlockSpec((nh, nblk, V_HEAD, t), lambda g, i: (g, 0, 0, 0))] + r_in,
        out_specs=[pl.BlockSpec((t, nh * V_HEAD), lambda g, i: (i, g)),
                   pl.BlockSpec((nh, 1, 1, t), lambda g, i: (g, i, 0, 0))] + r_out,
        out_shape=[jax.ShapeDtypeStruct((rows, HEADS * V_HEAD), F32),
                   jax.ShapeDtypeStruct((HEADS, nblk, 1, t), F32)] + r_shape,
        scratch_shapes=[pltpu.VMEM((nh, 1, t), F32), pltpu.VMEM((nh, 1, t), F32), pltpu.VMEM((nh, V_HEAD, t), F32),
                        pltpu.VMEM((nh, t, t), F32), pltpu.VMEM((nh, t, t), F32)] + r_scratch,
        compiler_params=_cparams(("arbitrary", "arbitrary")),
    )(qc, kc, vt, *wsends)


def _attn_bwd(qc, kc, v, lse, delta, do, gsends):
    rows = qc.shape[0]
    t = _row_block(rows)
    nblk = rows // t
    ng = len(gsends)

    def body(q_ref, k_ref, v_ref, lse_ref, delta_ref, do_ref, *rest):
        dq_ref, dk_ref, dv_ref = rest[ng:ng + 3]
        dq_acc, dk_acc, dv_acc, st_a, dp_a, st_b, dp_b = rest[2 * ng + 3:2 * ng + 10]
        j = pl.program_id(1)
        start, wait = _rider(gsends, rest[:ng] + rest[ng + 3:2 * ng + 3] + rest[2 * ng + 10:],
                             jnp.logical_and(pl.program_id(0) == 0, j == 0),
                             jnp.logical_and(pl.program_id(0) == HEADS // ATTN_HEADS - 1, j == nblk - 1), True)
        start()

        @pl.when(j == 0)
        def _():
            dq_acc[...] = jnp.zeros_like(dq_acc)

        dk_acc[...] = jnp.zeros_like(dk_acc)
        dv_acc[...] = jnp.zeros_like(dv_acc)
        heads = range(ATTN_HEADS)
        qk = lambda hh: slice(hh * HEAD_PAD, (hh + 1) * HEAD_PAD)
        vo = lambda hh: slice(hh * V_HEAD, (hh + 1) * V_HEAD)
        ks = [k_ref[:, qk(hh)] for hh in heads]
        vs = [v_ref[:, vo(hh)] for hh in heads]

        def products(i, hh, st_ref, dp_ref):
            rs = pl.ds(pl.multiple_of(i * t, t), t)
            st_ref[hh] = lax.dot_general(ks[hh], q_ref[rs, qk(hh)], NT, preferred_element_type=F32)
            dp_ref[hh] = lax.dot_general(vs[hh], do_ref[rs, vo(hh)], NT, preferred_element_type=F32)

        def consume(i, hh, st_ref, dp_ref):
            rs = pl.ds(pl.multiple_of(i * t, t), t)
            q = q_ref[rs, qk(hh)]
            dob = do_ref[rs, vo(hh)]
            st = jnp.where(jnp.logical_or(_causal_mask_t(t), i != j), st_ref[hh], MASK_VALUE)
            pt = jnp.exp(st - lse_ref[hh, i])
            dv_acc[hh] += jnp.dot(pt.astype(BF16), dob, preferred_element_type=F32)
            dst = (pt * (dp_ref[hh] - delta_ref[hh, i])).astype(BF16)
            dk_acc[hh] += jnp.dot(dst, q, preferred_element_type=F32)
            dq_acc[hh, rs, :] += lax.dot_general(dst, ks[hh], TN, preferred_element_type=F32)

        bufs = ((st_a, dp_a), (st_b, dp_b))

        def step(i, parity, issue_next):
            if issue_next:
                for hh in heads:
                    products(i + 1, hh, *bufs[1 - parity])
            for hh in heads:
                consume(i, hh, *bufs[parity])

        for hh in heads:
            products(j, hh, st_a, dp_a)

        def trip(it, carry):
            for u in range(ATTN_UNROLL_BWD):
                step(j + it * ATTN_UNROLL_BWD + u, u % 2, True)
            return carry

        trips = (nblk - 1 - j) // ATTN_UNROLL_BWD
        lax.fori_loop(0, trips, trip, 0)
        i0 = j + trips * ATTN_UNROLL_BWD
        for left in range(1, ATTN_UNROLL_BWD + 1):
            @pl.when(nblk - i0 == left)
            def _(left=left):
                for u in range(left):
                    step(i0 + u, u % 2, u < left - 1)

        for hh in heads:
            dk_ref[:, qk(hh)] = dk_acc[hh].astype(BF16)
            dv_ref[:, vo(hh)] = dv_acc[hh].astype(BF16)

        @pl.when(j == nblk - 1)
        def _():
            for hh in heads:
                dq_ref[:, qk(hh)] = dq_acc[hh].astype(BF16)

        wait()

    nh = ATTN_HEADS
    stat = pl.BlockSpec((nh, nblk, 1, t), lambda g, j: (g, 0, 0, 0))
    r_in, r_out, r_shape, r_scratch = _rider_specs(gsends, True)
    return pl.pallas_call(
        body, name="attn_bwd", grid=(HEADS // nh, nblk),
        in_specs=[pl.BlockSpec((rows, nh * HEAD_PAD), lambda g, j: (0, g)),
                  pl.BlockSpec((t, nh * HEAD_PAD), lambda g, j: (j, g)),
                  pl.BlockSpec((t, nh * V_HEAD), lambda g, j: (j, g)),
                  stat, stat,
                  pl.BlockSpec((rows, nh * V_HEAD), lambda g, j: (0, g))] + r_in,
        out_specs=[pl.BlockSpec((rows, nh * HEAD_PAD), lambda g, j: (0, g)),
                   pl.BlockSpec((t, nh * HEAD_PAD), lambda g, j: (j, g)),
                   pl.BlockSpec((t, nh * V_HEAD), lambda g, j: (j, g))] + r_out,
        out_shape=[jax.ShapeDtypeStruct((rows, HEADS * HEAD_PAD), BF16),
                   jax.ShapeDtypeStruct((rows, HEADS * HEAD_PAD), BF16),
                   jax.ShapeDtypeStruct((rows, HEADS * V_HEAD), BF16)] + r_shape,
        scratch_shapes=[pltpu.VMEM((nh, rows, HEAD_PAD), F32), pltpu.VMEM((nh, t, HEAD_PAD), F32),
                        pltpu.VMEM((nh, t, V_HEAD), F32)] + [pltpu.VMEM((nh, t, t), F32)] * 4 + r_scratch,
        compiler_params=_cparams(("arbitrary", "arbitrary")),
    )(qc, kc, v, lse, delta, do, *gsends)


def _shift_down(prev_tile, x, k):
    xx = jnp.concatenate([prev_tile, x], axis=0)
    return pltpu.roll(xx, k, 0)[SUBLANES:]


def _shift_up(x, next_tile, k):
    n = x.shape[0]
    xx = jnp.concatenate([x, next_tile], axis=0)
    return pltpu.roll(xx, n + SUBLANES - k, 0)[:n]


def _lru_gates(u, u_prev, cw_ref, cb_ref, wrg_ref, brg_ref, wig_ref, big_ref, lam_ref, first_block):
    taps = [_shift_down(u_prev, u, CONV_WIDTH - 1 - j) if j < CONV_WIDTH - 1 else u for j in range(CONV_WIDTH)]
    uc = cb_ref[...] + taps[0] * cw_ref[0:1, :]
    for j in range(1, CONV_WIDTH):
        uc = uc + taps[j] * cw_ref[j:j + 1, :]
    ub = uc.astype(BF16)
    zr = jnp.concatenate([jnp.dot(ub[:, g * LRU_BLOCK:(g + 1) * LRU_BLOCK], wrg_ref[g], preferred_element_type=F32)
                          for g in range(LRU_BLOCKS)], axis=1) + brg_ref[...]
    zi = jnp.concatenate([jnp.dot(ub[:, g * LRU_BLOCK:(g + 1) * LRU_BLOCK], wig_ref[g], preferred_element_type=F32)
                          for g in range(LRU_BLOCKS)], axis=1) + big_ref[...]
    r = jax.nn.sigmoid(zr)
    ig = jax.nn.sigmoid(zi)
    sp = _softplus_neg(lam_ref[...])
    log_a = (-LRU_C) * r * sp
    a = jnp.exp(log_a)
    m2 = -_expm1_neg(2.0 * log_a)
    mult_raw = m2 * lax.rsqrt(jnp.maximum(m2, 1e-30))
    row = lax.broadcasted_iota(jnp.int32, u.shape, 0)
    is_start = jnp.logical_and(first_block, row == 0)
    mult = jnp.where(is_start, 1.0, mult_raw)
    return dict(taps=taps, uc=uc, ub=ub, r=r, ig=ig, sp=sp, a=a, mult=mult, mult_raw=mult_raw, is_start=is_start)


def _rglru_fwd(u, cw, cb, wrg, brg, wig, big, lam):
    rows = u.shape[0]
    tb = _row_block(rows)

    def body(u_ref, cw_ref, cb_ref, wrg_ref, brg_ref, wig_ref, big_ref, lam_ref, hs_ref, utail, hcar, a_s, b_s):
        i = pl.program_id(0)

        @pl.when(i == 0)
        def _():
            utail[...] = jnp.zeros_like(utail)
            hcar[...] = jnp.zeros_like(hcar)

        u = u_ref[...]
        gt = _lru_gates(u, utail[...], cw_ref, cb_ref, wrg_ref, brg_ref, wig_ref, big_ref, lam_ref, i == 0)
        a_s[...] = gt["a"]
        b_s[...] = gt["mult"] * (gt["ig"] * gt["uc"])
        row8 = lax.broadcasted_iota(jnp.int32, (SUBLANES, LRU_WIDTH), 0)

        def tile(tix, carry):
            rs = pl.ds(pl.multiple_of(tix * SUBLANES, SUBLANES), SUBLANES)
            av, bv = a_s[rs, :], b_s[rs, :]
            for k in (1, 2, 4):
                keep = row8 >= k
                bv = jnp.where(keep, av * pltpu.roll(bv, k, 0) + bv, bv)
                av = jnp.where(keep, av * pltpu.roll(av, k, 0), av)
            h8 = av * carry + bv
            hs_ref[rs, :] = h8
            return jnp.broadcast_to(h8[SUBLANES - 1:SUBLANES, :], (SUBLANES, LRU_WIDTH))

        hcar[...] = lax.fori_loop(0, tb // SUBLANES, tile, hcar[...])
        utail[...] = u[tb - SUBLANES:, :]

    full2 = lambda shape: pl.BlockSpec(shape, lambda i: (0, 0))
    full3 = lambda shape: pl.BlockSpec(shape, lambda i: (0, 0, 0))
    blk = pl.BlockSpec((tb, LRU_WIDTH), lambda i: (i, 0))
    return pl.pallas_call(
        body, name="rglru_fwd", grid=(rows // tb,),
        in_specs=[blk, full2((CONV_WIDTH, LRU_WIDTH)), full2((1, LRU_WIDTH)),
                  full3((LRU_BLOCKS, LRU_BLOCK, LRU_BLOCK)), full2((1, LRU_WIDTH)),
                  full3((LRU_BLOCKS, LRU_BLOCK, LRU_BLOCK)), full2((1, LRU_WIDTH)), full2((1, LRU_WIDTH))],
        out_specs=blk,
        out_shape=jax.ShapeDtypeStruct((rows, LRU_WIDTH), F32),
        scratch_shapes=[pltpu.VMEM((SUBLANES, LRU_WIDTH), F32), pltpu.VMEM((SUBLANES, LRU_WIDTH), F32),
                        pltpu.VMEM((tb, LRU_WIDTH), F32), pltpu.VMEM((tb, LRU_WIDTH), F32)],
        compiler_params=_cparams(("arbitrary",)),
    )(u, cw, cb, wrg, brg, wig, big, lam)


def _rglru_bwd(u, hs, dhs, cw, cb, wrg, brg, wig, big, lam):
    rows = u.shape[0]
    tb = _row_block(rows)
    nblk = rows // tb
    tiles = tb // SUBLANES

    def body(u_ref, up_ref, hs_ref, hp_ref, dhs_ref, cw_ref, cb_ref, wrg_ref, brg_ref, wig_ref, big_ref, lam_ref,
             du_ref, dcw_ref, dcb_ref, dwrg_ref, dbrg_ref, dwig_ref, dbig_ref, dlam_ref,
             gcar, duc_head, a_s, b_s, g_s, dsp_acc):
        step = pl.program_id(0)
        blk_ix = nblk - 1 - step

        @pl.when(step == 0)
        def _():
            for ref in (dcw_ref, dcb_ref, dwrg_ref, dbrg_ref, dwig_ref, dbig_ref, gcar, duc_head, dsp_acc):
                ref[...] = jnp.zeros_like(ref)

        first = blk_ix == 0
        u = u_ref[...]
        u_prev = jnp.where(first, 0.0, up_ref[...])
        h_prev_tile = jnp.where(first, 0.0, hp_ref[...])
        gt = _lru_gates(u, u_prev, cw_ref, cb_ref, wrg_ref, brg_ref, wig_ref, big_ref, lam_ref, first)
        a, r, ig, uc, mult = gt["a"], gt["r"], gt["ig"], gt["uc"], gt["mult"]
        dhs_v = dhs_ref[...]

        a_s[...] = a
        b_s[...] = a * dhs_v
        row8 = lax.broadcasted_iota(jnp.int32, (SUBLANES, LRU_WIDTH), 0)

        def tile(tix, carry):
            rs = pl.ds(pl.multiple_of((tiles - 1 - tix) * SUBLANES, SUBLANES), SUBLANES)
            av, bv = a_s[rs, :], b_s[rs, :]
            for k in (1, 2, 4):
                keep = row8 < SUBLANES - k
                bv = jnp.where(keep, av * pltpu.roll(bv, SUBLANES - k, 0) + bv, bv)
                av = jnp.where(keep, av * pltpu.roll(av, SUBLANES - k, 0), av)
            g8 = av * carry + bv
            g_s[rs, :] = g8
            return jnp.broadcast_to(g8[0:1, :], (SUBLANES, LRU_WIDTH))

        g_next = gcar[...]
        gcar[...] = lax.fori_loop(0, tiles, tile, g_next)
        g = dhs_v + _shift_up(g_s[...], g_next, 1)

        h_prev = _shift_down(h_prev_tile, hs_ref[...], 1)
        da = g * h_prev
        iu = ig * uc
        dmult = jnp.where(gt["is_start"], 0.0, g * iu)
        d_ig = g * mult * uc
        duc = g * mult * ig
        dlog_a = da * a - dmult * (a * a) / gt["mult_raw"]
        dzr = (dlog_a * ((-LRU_C) * gt["sp"])) * r * (1.0 - r)
        dsp_acc[...] += jnp.sum(dlog_a * ((-LRU_C) * r), axis=0, keepdims=True)
        dzi = d_ig * ig * (1.0 - ig)
        dbrg_ref[...] += jnp.sum(dzr, axis=0, keepdims=True)
        dbig_ref[...] += jnp.sum(dzi, axis=0, keepdims=True)
        dzr_b, dzi_b = dzr.astype(BF16), dzi.astype(BF16)
        ub = gt["ub"]
        duc_parts = []
        for gi in range(LRU_BLOCKS):
            cs = slice(gi * LRU_BLOCK, (gi + 1) * LRU_BLOCK)
            dwrg_ref[gi] += lax.dot_general(ub[:, cs], dzr_b[:, cs], TN, preferred_element_type=F32)
            dwig_ref[gi] += lax.dot_general(ub[:, cs], dzi_b[:, cs], TN, preferred_element_type=F32)
            duc_parts.append(lax.dot_general(dzr_b[:, cs], wrg_ref[gi], NT, preferred_element_type=F32)
                             + lax.dot_general(dzi_b[:, cs], wig_ref[gi], NT, preferred_element_type=F32))
        duc = duc + jnp.concatenate(duc_parts, axis=1)

        dcb_ref[...] += jnp.sum(duc, axis=0, keepdims=True)
        taps = gt["taps"]
        for jt in range(CONV_WIDTH):
            dcw_ref[jt:jt + 1, :] += jnp.sum(duc * taps[jt], axis=0, keepdims=True)
        head = duc_head[...]
        du = duc * cw_ref[CONV_WIDTH - 1:CONV_WIDTH, :]
        for jt in range(CONV_WIDTH - 1):
            du = du + _shift_up(duc, head, CONV_WIDTH - 1 - jt) * cw_ref[jt:jt + 1, :]
        du_ref[...] = du.astype(BF16)
        duc_head[...] = duc[:SUBLANES, :]

        @pl.when(step == nblk - 1)
        def _():
            dlam_ref[...] = -dsp_acc[...] * jax.nn.sigmoid(-lam_ref[...])

    full2 = lambda shape: pl.BlockSpec(shape, lambda s: (0, 0))
    full3 = lambda shape: pl.BlockSpec(shape, lambda s: (0, 0, 0))
    blk = pl.BlockSpec((tb, LRU_WIDTH), lambda s: (nblk - 1 - s, 0))
    prev_tile = pl.BlockSpec((SUBLANES, LRU_WIDTH), lambda s: (jnp.maximum((nblk - 1 - s) * tiles - 1, 0), 0))
    wshape = (LRU_BLOCKS, LRU_BLOCK, LRU_BLOCK)
    return pl.pallas_call(
        body, name="rglru_bwd", grid=(nblk,),
        in_specs=[blk, prev_tile, blk, prev_tile, blk, full2((CONV_WIDTH, LRU_WIDTH)), full2((1, LRU_WIDTH)),
                  full3(wshape), full2((1, LRU_WIDTH)), full3(wshape), full2((1, LRU_WIDTH)), full2((1, LRU_WIDTH))],
        out_specs=[blk, full2((CONV_WIDTH, LRU_WIDTH)), full2((1, LRU_WIDTH)), full3(wshape), full2((1, LRU_WIDTH)),
                   full3(wshape), full2((1, LRU_WIDTH)), full2((1, LRU_WIDTH))],
        out_shape=[jax.ShapeDtypeStruct((rows, LRU_WIDTH), BF16),
                   jax.ShapeDtypeStruct((CONV_WIDTH, LRU_WIDTH), F32), jax.ShapeDtypeStruct((1, LRU_WIDTH), F32),
                   jax.ShapeDtypeStruct(wshape, F32), jax.ShapeDtypeStruct((1, LRU_WIDTH), F32),
                   jax.ShapeDtypeStruct(wshape, F32), jax.ShapeDtypeStruct((1, LRU_WIDTH), F32),
                   jax.ShapeDtypeStruct((1, LRU_WIDTH), F32)],
        scratch_shapes=[pltpu.VMEM((SUBLANES, LRU_WIDTH), F32), pltpu.VMEM((SUBLANES, LRU_WIDTH), F32),
                        pltpu.VMEM((tb, LRU_WIDTH), F32), pltpu.VMEM((tb, LRU_WIDTH), F32),
                        pltpu.VMEM((tb, LRU_WIDTH), F32), pltpu.VMEM((1, LRU_WIDTH), F32)],
        compiler_params=_cparams(("arbitrary",)),
    )(u, u, hs, hs, dhs, cw, cb, wrg, brg, wig, big, lam)


def _out_proj_loss(a, gate, h, w, gf, target, n_real):
    rows = h.shape[0]
    tr = _row_block(rows)

    def body(a_ref, gate_ref, h_ref, w_ref, g_ref, t_ref, dh_ref, loss_ref, dg_ref, da_ref, dgate_ref, dw_ref,
             dw_acc):
        i = pl.program_id(0)

        @pl.when(i == 0)
        def _():
            loss_ref[...] = jnp.zeros_like(loss_ref)
            dg_ref[...] = jnp.zeros_like(dg_ref)
            dw_acc[...] = jnp.zeros_like(dw_acc)

        gv = g_ref[...]
        av, gatev = a_ref[...], gate_ref[...].astype(F32)
        sg = _silu(gatev)
        y = (av * sg).astype(BF16)
        xn, r = _rms_fwd(h_ref[...] + jnp.dot(y, w_ref[...], preferred_element_type=F32))
        row = i * tr + lax.broadcasted_iota(jnp.int32, (tr, 1), 0)
        live = jnp.logical_and(row >= N_META, row < n_real)
        tgt = t_ref[...]
        tgt = jnp.where(i == 0, pltpu.roll(tgt, N_META, 0), tgt)
        err = jnp.where(live, xn * gv - tgt, 0.0)
        loss_ref[...] += (0.5 / D_MODEL) * jnp.sum(jnp.sum(err * err, axis=1, keepdims=True), axis=0, keepdims=True)
        dx, dg = _rms_bwd(err * (1.0 / D_MODEL), xn, r, gv)
        dg_ref[...] += dg
        dh_ref[...] = dx
        dhb = dx.astype(BF16)
        dw_acc[...] += lax.dot_general(y, dhb, TN, preferred_element_type=F32)
        dy = lax.dot_general(dhb, w_ref[...], NT, preferred_element_type=F32)
        da_ref[...] = dy * sg
        dgate_ref[...] = (dy * av * _dsilu(gatev)).astype(BF16)

        @pl.when(i == rows // tr - 1)
        def _():
            dw_ref[...] = dw_acc[...].astype(BF16)

    blk = pl.BlockSpec((tr, D_MODEL), lambda i: (i, 0))
    wblk = pl.BlockSpec((D_MODEL, D_MODEL), lambda i: (0, 0))
    window = pl.BlockSpec((pl.Element(tr, (0, rows - n_real)), pl.Element(D_MODEL)),
                          lambda i: (pl.multiple_of(jnp.maximum(i * tr - N_META, 0), SUBLANES), 0))
    return pl.pallas_call(
        body, name="b_out_loss", grid=(rows // tr,),
        in_specs=[blk, blk, blk, wblk, pl.BlockSpec((1, D_MODEL), lambda i: (0, 0)), window],
        out_specs=[blk, pl.BlockSpec((1, 1), lambda i: (0, 0)), pl.BlockSpec((1, D_MODEL), lambda i: (0, 0)),
                   blk, blk, wblk],
        out_shape=[jax.ShapeDtypeStruct((rows, D_MODEL), F32), jax.ShapeDtypeStruct((1, 1), F32),
                   jax.ShapeDtypeStruct((1, D_MODEL), F32), jax.ShapeDtypeStruct((rows, D_MODEL), F32),
                   jax.ShapeDtypeStruct((rows, D_MODEL), BF16), jax.ShapeDtypeStruct((D_MODEL, D_MODEL), BF16)],
        scratch_shapes=[pltpu.VMEM((D_MODEL, D_MODEL), F32)],
        compiler_params=_cparams(("arbitrary",)),
    )(a, gate, h, w, gf, target)


def _my_place():
    x, y, c = lax.axis_index("x"), lax.axis_index("y"), lax.axis_index("c")
    return x, y, c, 4 * x + 2 * y + c


def _peer(x, y, c, k):
    px, py, pc = x ^ (k >> 2), y ^ ((k >> 1) & 1), c ^ (k & 1)
    return (px, py, pc), 4 * px + 2 * py + pc


def _exchange_copies(src_of, dst_ref, send_sems, recv_sems, local_sem):
    x, y, c, me = _my_place()
    copies = [pltpu.make_async_copy(src_of(me), dst_ref.at[me], local_sem)]
    for k in range(1, N_DEV):
        peer, pid = _peer(x, y, c, k)
        copies.append(pltpu.make_async_remote_copy(
            src_ref=src_of(pid), dst_ref=dst_ref.at[me], send_sem=send_sems.at[k], recv_sem=recv_sems.at[k],
            device_id=peer, device_id_type=MESH))
    return copies


def _exchange_sems(nb):
    return [pltpu.SemaphoreType.DMA((nb, N_DEV)), pltpu.SemaphoreType.DMA((nb, N_DEV)), pltpu.SemaphoreType.DMA((nb,))]


def _sum_blocks(lands, name):
    n = len(lands)

    def body(*refs):
        for land_ref, out_ref in zip(refs[:n], refs[n:]):
            acc = land_ref[0].astype(F32)
            for d in range(1, N_DEV):
                acc = acc + land_ref[d].astype(F32)
            out_ref[...] = acc

    return pl.pallas_call(
        body, name=name, out_shape=[jax.ShapeDtypeStruct(l.shape[1:], F32) for l in lands],
        compiler_params=pltpu.CompilerParams(vmem_limit_bytes=VMEM_LIMIT),
    )(*lands)


def _two_level_gather(parts, send_sems, recv_sems, local_sems):
    x, y, c, _ = _my_place()
    me, sibling = (x, y, c), (x, y, 1 - c)
    chips = [(1 - x, y), (x, 1 - y), (1 - x, 1 - y)]
    nparts = range(len(parts))

    def slot(dst, place):
        return dst.at[4 * place[0] + 2 * place[1] + place[2]]

    def copy(part, k, block, to, first_hand=False):
        src, dst = parts[part]
        return pltpu.make_async_remote_copy(
            src_ref=src if first_hand else slot(dst, block), dst_ref=slot(dst, block),
            send_sem=send_sems.at[part, k], recv_sem=recv_sems.at[part, k], device_id=to, device_id_type=MESH)

    own = lambda: [pltpu.make_async_copy(src, slot(dst, me), local_sems.at[part])
                   for part, (src, dst) in enumerate(parts)]
    first = lambda: [cp for part in nparts for cp in
                     [copy(part, 0, me, sibling, True)] + [copy(part, 1 + j, me, (*chip, c), True)
                                                           for j, chip in enumerate(chips)]]
    passed = lambda: [copy(part, 4 + j, (*chip, c), sibling) for j, chip in enumerate(chips) for part in nparts]

    def issue():
        for cp in own() + first():
            cp.start()

    def forward():
        for j, chip in enumerate(chips):
            for part in nparts:
                copy(part, 1 + j, (*chip, c), me).wait_recv()
                copy(part, 4 + j, (*chip, c), sibling).start()

    def finish():
        for part in nparts:
            copy(part, 0, sibling, me).wait_recv()
            for j, chip in enumerate(chips):
                copy(part, 4 + j, (*chip, 1 - c), me).wait_recv()
        for cp in first() + passed():
            cp.wait_send()
        for cp in own():
            cp.wait()

    return issue, forward, finish


def _all_gather(big, small):
    def body(big_ref, small_ref, obig_ref, osmall_ref, send_sems, recv_sems, local_sems):
        issue, forward, finish = _two_level_gather(((big_ref, obig_ref), (small_ref, osmall_ref)), send_sems,
                                                   recv_sems, local_sems)
        issue()
        forward()
        finish()

    hbm = pl.BlockSpec(memory_space=pl.ANY)
    return pl.pallas_call(
        body, name="weight_all_gather",
        in_specs=[hbm, hbm], out_specs=[hbm, hbm],
        out_shape=[jax.ShapeDtypeStruct((N_DEV,) + big.shape, BF16), jax.ShapeDtypeStruct((N_DEV,) + small.shape, F32)],
        scratch_shapes=[pltpu.SemaphoreType.DMA((2, N_DEV)), pltpu.SemaphoreType.DMA((2, N_DEV)),
                        pltpu.SemaphoreType.DMA((2,))],
        compiler_params=pltpu.CompilerParams(has_side_effects=True),
    )(big, small)


GRAD_CHUNK = 32


def _grad_exchange(gbig, rep):
    n, width = gbig.shape[1:]
    nrep = rep.shape[0]
    n_chips = N_DEV // 2

    def body(gbig_ref, rep_ref, out_ref, orep_ref, pre, stage, got, own_sum, land_rep, send_sems, recv_sems,
             local_sem):
        x, y, c, me = _my_place()
        my_chip = 2 * x + y
        sibling = (x, y, 1 - c)

        local = pltpu.make_async_copy(rep_ref, land_rep.at[me], local_sem.at[0])
        local.start()
        rep_copies = []
        for k in range(1, N_DEV):
            peer, _ = _peer(x, y, c, k)
            rep_copies.append(pltpu.make_async_remote_copy(
                src_ref=rep_ref, dst_ref=land_rep.at[me], send_sem=send_sems.at[6 + k], recv_sem=recv_sems.at[6 + k],
                device_id=peer, device_id_type=MESH))
        swaps = [pltpu.make_async_remote_copy(
            src_ref=gbig_ref.at[2 * q + (1 - c)], dst_ref=pre.at[q], send_sem=send_sems.at[q], recv_sem=recv_sems.at[q],
            device_id=sibling, device_id_type=MESH) for q in range(n_chips)]
        for cp in rep_copies + swaps:
            cp.start()
        for cp in swaps:
            cp.wait_recv()

        def pair_sums(ci, carry):
            rs = pl.ds(pl.multiple_of(ci * GRAD_CHUNK, GRAD_CHUNK), GRAD_CHUNK)
            for q in range(n_chips):
                stage[q, rs, :] = (gbig_ref[2 * q + c, rs, :].astype(F32) + pre[q, rs, :].astype(F32)).astype(BF16)
            own_sum[rs, :] = gbig_ref[me, rs, :].astype(F32) + pre[my_chip, rs, :].astype(F32)
            return carry

        lax.fori_loop(0, n // GRAD_CHUNK, pair_sums, 0)

        hops = []
        for rel in range(1, n_chips):
            qx, qy = x ^ (rel >> 1), y ^ (rel & 1)
            hops.append(pltpu.make_async_remote_copy(
                src_ref=stage.at[2 * qx + qy], dst_ref=got.at[my_chip], send_sem=send_sems.at[3 + rel],
                recv_sem=recv_sems.at[3 + rel], device_id=(qx, qy, c), device_id_type=MESH))
        for cp in hops:
            cp.start()
        for cp in hops:
            cp.wait_recv()

        def chip_sums(ci, carry):
            rs = pl.ds(pl.multiple_of(ci * GRAD_CHUNK, GRAD_CHUNK), GRAD_CHUNK)
            mine = own_sum[rs, :]
            acc = jnp.where(my_chip == 0, mine, got[0, rs, :].astype(F32))
            for q in range(1, n_chips):
                acc = acc + jnp.where(my_chip == q, mine, got[q, rs, :].astype(F32))
            out_ref[rs, :] = acc
            return carry

        lax.fori_loop(0, n // GRAD_CHUNK, chip_sums, 0)

        for cp in rep_copies:
            cp.wait_recv()
        local.wait()
        acc = land_rep[0]
        for d in range(1, N_DEV):
            acc = acc + land_rep[d]
        orep_ref[...] = acc
        for cp in swaps + hops + rep_copies:
            cp.wait_send()

    return pl.pallas_call(
        body, name="grad_exchange",
        in_specs=[pl.BlockSpec(memory_space=pltpu.VMEM), pl.BlockSpec(memory_space=pltpu.VMEM)],
        out_specs=[pl.BlockSpec(memory_space=pltpu.VMEM), pl.BlockSpec(memory_space=pltpu.VMEM)],
        out_shape=[jax.ShapeDtypeStruct((n, width), F32), jax.ShapeDtypeStruct((nrep, LANES), F32)],
        scratch_shapes=[pltpu.VMEM((n_chips, n, width), BF16), pltpu.VMEM((n_chips, n, width), BF16),
                        pltpu.VMEM((n_chips, n, width), BF16), pltpu.VMEM((n, width), F32),
                        pltpu.VMEM((N_DEV, nrep, LANES), F32),
                        pltpu.SemaphoreType.DMA((2 * N_DEV - 2,)), pltpu.SemaphoreType.DMA((2 * N_DEV - 2,)),
                        pltpu.SemaphoreType.DMA((1,))],
        compiler_params=pltpu.CompilerParams(vmem_limit_bytes=VMEM_LIMIT, has_side_effects=True),
    )(gbig, rep)


def _adamw_all(ws, gs, ms, vs):
    n = len(ws)

    def body(*refs):
        w_refs, g_refs, m_refs, v_refs = refs[0:n], refs[n:2 * n], refs[2 * n:3 * n], refs[3 * n:4 * n]
        d_refs, nm_refs, nv_refs = refs[4 * n:5 * n], refs[5 * n:6 * n], refs[6 * n:7 * n]
        for w_ref, g_ref, m_ref, v_ref, d_ref, nm_ref, nv_ref in zip(w_refs, g_refs, m_refs, v_refs, d_refs, nm_refs, nv_refs):
            g = g_ref[...]
            m = ADAM_B1 * m_ref[...] + (1.0 - ADAM_B1) * g
            v = ADAM_B2 * v_ref[...] + (1.0 - ADAM_B2) * jnp.square(g)
            m_hat = m / (1.0 - ADAM_B1 ** ADAM_STEP)
            v_hat = v / (1.0 - ADAM_B2 ** ADAM_STEP)
            d_ref[...] = -ADAM_LR * (m_hat / (jnp.sqrt(v_hat) + ADAM_EPS) + ADAM_WD * w_ref[...])
            nm_ref[...] = m
            nv_ref[...] = v

    shapes = [jax.ShapeDtypeStruct(w.shape, F32) for w in ws]
    outs = pl.pallas_call(
        body, name="adamw", out_shape=shapes * 3,
        compiler_params=pltpu.CompilerParams(vmem_limit_bytes=VMEM_LIMIT),
    )(*ws, *gs, *ms, *vs)
    return outs[0:n], outs[n:2 * n], outs[2 * n:3 * n]


SMALL_A = (("meta_tokens", 16),)
SMALL_B = (("b_norm_g", 1), ("b_conv_w", 4), ("b_conv_b", 1), ("b_b_rg", 1), ("b_b_ig", 1), ("b_lam", 1))
REP = (("a_norm_g", 8), ("a_q_norm_g", 3), ("a_kv_norm_g", 2), ("final_norm_g", 8), ("loss", 1))
SLOT = 16


def _offsets(table, slot=1, start=0):
    out, o = {}, start
    for name, n in table:
        out[name] = (o, n)
        o += -(-n // slot) * slot
    return out, o


def _slotted(a, axis):
    pad = -a.shape[axis] % SLOT
    if not pad:
        return a
    widths = [(0, 0)] * a.ndim
    widths[axis] = (0, pad)
    return jnp.pad(a, widths)


def _rope_tables(rows):
    pos = np.arange(rows, dtype=np.float32)
    inv_freq = (np.float32(ROPE_BASE) ** (-np.arange(0, QK_ROPE, 2, dtype=np.float32) / np.float32(QK_ROPE))).astype(
        np.float32)
    ang = pos[:, None] * inv_freq[None, :]
    cos, sin = np.cos(ang).astype(np.float32), np.sin(ang).astype(np.float32)
    zeros = np.zeros((rows, LANES - QK_ROPE), np.float32)
    return jnp.asarray(np.concatenate([cos, cos, zeros], axis=1)), jnp.asarray(np.concatenate([-sin, sin, zeros], axis=1))


def kernel(x, meta_tokens, a_norm_g, a_w_in, a_q_norm_g, a_kv_norm_g, a_w_uq, a_w_ukv, a_w_out, b_norm_g, b_w_in, b_conv_w, b_conv_b, b_w_rg, b_b_rg, b_w_ig, b_b_ig, b_lam, b_w_out, final_norm_g, loss_target, m_meta_tokens, m_a_norm_g, m_a_w_in, m_a_q_norm_g, m_a_kv_norm_g, m_a_w_uq, m_a_w_ukv, m_a_w_out, m_b_norm_g, m_b_w_in, m_b_conv_w, m_b_conv_b, m_b_w_rg, m_b_b_rg, m_b_w_ig, m_b_b_ig, m_b_lam, m_b_w_out, m_final_norm_g, v_meta_tokens, v_a_norm_g, v_a_w_in, v_a_q_norm_g, v_a_kv_norm_g, v_a_w_uq, v_a_w_ukv, v_a_w_out, v_b_norm_g, v_b_w_in, v_b_conv_w, v_b_conv_b, v_b_w_rg, v_b_b_rg, v_b_w_ig, v_b_b_ig, v_b_lam, v_b_w_out, v_final_norm_g):
    names = ("meta_tokens", "a_norm_g", "a_w_in", "a_q_norm_g", "a_kv_norm_g", "a_w_uq", "a_w_ukv", "a_w_out",
             "b_norm_g", "b_w_in", "b_conv_w", "b_conv_b", "b_w_rg", "b_b_rg", "b_w_ig", "b_b_ig", "b_lam", "b_w_out",
             "final_norm_g")
    w = dict(zip(names, (meta_tokens, a_norm_g, a_w_in, a_q_norm_g, a_kv_norm_g, a_w_uq, a_w_ukv, a_w_out, b_norm_g,
                         b_w_in, b_conv_w, b_conv_b, b_w_rg, b_b_rg, b_w_ig, b_b_ig, b_lam, b_w_out, final_norm_g)))
    mom_m = dict(zip(names, (m_meta_tokens, m_a_norm_g, m_a_w_in, m_a_q_norm_g, m_a_kv_norm_g, m_a_w_uq, m_a_w_ukv,
                             m_a_w_out, m_b_norm_g, m_b_w_in, m_b_conv_w, m_b_conv_b, m_b_w_rg, m_b_b_rg, m_b_w_ig,
                             m_b_b_ig, m_b_lam, m_b_w_out, m_final_norm_g)))
    mom_v = dict(zip(names, (v_meta_tokens, v_a_norm_g, v_a_w_in, v_a_q_norm_g, v_a_kv_norm_g, v_a_w_uq, v_a_w_ukv,
                             v_a_w_out, v_b_norm_g, v_b_w_in, v_b_conv_w, v_b_conv_b, v_b_w_rg, v_b_b_rg, v_b_w_ig,
                             v_b_b_ig, v_b_lam, v_b_w_out, v_final_norm_g)))

    seq = x.shape[1]
    n_real = N_META + seq
    rows = -(-n_real // LANES) * LANES
    scale = (QK_NOPE + QK_ROPE) ** -0.5
    small_off, _ = _offsets(SMALL_A + SMALL_B, SLOT)
    gsmallb_off, _ = _offsets(SMALL_B, SLOT)
    rep_off, _ = _offsets(REP, SLOT)
    cdev_a = a_w_in.shape[-1]
    wide = 2 * LANES

    send_a0 = jnp.pad(a_w_in[0], ((0, 0), (0, wide - cdev_a))).astype(BF16)
    send_small = jnp.concatenate([_slotted(w[nm].reshape(-1, LANES), 0) for nm, _ in SMALL_A + SMALL_B], axis=0)
    sends_a1 = [jnp.pad(a_w_uq[0], ((0, 0), (0, HEAD_PAD - QK_NOPE - QK_ROPE))).astype(BF16), a_w_ukv[0].astype(BF16)]
    lru_rows = LRU_BLOCKS * LRU_BLOCK // N_DEV
    sends_b = [a_w_out[0].astype(BF16), b_w_in[0].astype(BF16), b_w_rg.reshape(lru_rows, LRU_BLOCK).astype(BF16),
               b_w_ig.reshape(lru_rows, LRU_BLOCK).astype(BF16), b_w_out[0].astype(BF16)]
    all_a0, all_small = _all_gather(send_a0, send_small)

    def small_seg(nm):
        o, n = small_off[nm]
        return all_small[:, o:o + n, :]

    w_in_a = all_a0[:, :, :cdev_a].transpose(1, 0, 2).reshape(D_MODEL, N_DEV * cdev_a)
    w_in_a = jnp.concatenate([w_in_a[:, :LAT + QK_ROPE], jnp.zeros((D_MODEL, LAT_PAD - LAT - QK_ROPE), BF16),
                              w_in_a[:, LAT + QK_ROPE:]], axis=1)[None]
    meta_full = small_seg("meta_tokens").transpose(1, 0, 2).reshape(N_META, D_MODEL)
    vec = lambda nm: small_seg(nm).reshape(1, D_MODEL)
    g_b, conv_b, b_rg, b_ig, lam = vec("b_norm_g"), vec("b_conv_b"), vec("b_b_rg"), vec("b_b_ig"), vec("b_lam")
    conv_w = small_seg("b_conv_w").transpose(1, 0, 2).reshape(CONV_WIDTH, LRU_WIDTH)
    g_a, g_q, g_kv = a_norm_g, a_q_norm_g, a_kv_norm_g
    g_f = final_norm_g.reshape(1, D_MODEL)

    cos, sin = _rope_tables(rows)

    h0, lat, gate_a, w_uq, w_ukv = _embed_norm_proj_fwd(x[0], meta_full, rows, g_a, w_in_a, LAT_PAD, "a_in_fwd",
                                                        sends_a1)
    qc, kc, v, vt = _mla_qkv_fwd(lat, g_q, g_kv, w_uq, w_ukv, cos, sin, scale)
    o, lse, w_out_a, w_in_b, w_rg, w_ig, w_out_b = _attn_fwd(qc, kc, vt, sends_b)

    lru_w = lambda g: g.reshape(N_DEV, LRU_BLOCKS, LRU_BLOCK // N_DEV, LRU_BLOCK).transpose(1, 0, 2, 3).reshape(
        LRU_BLOCKS, LRU_BLOCK, LRU_BLOCK)
    w_out_a, w_out_b = w_out_a.reshape(D_MODEL, D_MODEL), w_out_b.reshape(D_MODEL, D_MODEL)
    w_rg, w_ig = lru_w(w_rg), lru_w(w_ig)

    h1, u, gate_b = _out_proj_in_proj(o, gate_a, h0, w_out_a, g_b, w_in_b, LRU_WIDTH, "a_out_b_in_fwd")
    hs = _rglru_fwd(u, conv_w, conv_b, w_rg, b_rg, w_ig, b_ig, lam)
    dh2, loss_part, dg_f, dhs, dgate_b, dw_out_b = _out_proj_loss(hs, gate_b, h1, w_out_b, g_f, loss_target[0],
                                                                   n_real)

    du, dconv_w, dconv_b, dw_rg, db_rg, dw_ig, db_ig, dlam = _rglru_bwd(u, hs, dhs, conv_w, conv_b, w_rg, b_rg, w_ig,
                                                                       b_ig, lam)
    dh1, dw_in_b, dg_b = _norm_proj_bwd(h1, g_b, w_in_b, du, dgate_b, dh2, "b_in_bwd")
    do, dgate_a, dw_out_a, delta = _attn_out_bwd(o, gate_a, dh1, w_out_a)

    def to_cols(g, cdev):
        r = g.shape[0]
        return g.reshape(r, N_DEV, cdev).transpose(1, 0, 2).reshape(N_DEV, -1, LANES)

    lru_g = lambda g: g.reshape(LRU_BLOCKS, N_DEV, LRU_BLOCK // N_DEV, LRU_BLOCK).transpose(1, 0, 2, 3).reshape(
        N_DEV, lru_rows, LRU_BLOCK)
    small_b = {"b_norm_g": dg_b, "b_conv_w": dconv_w, "b_conv_b": dconv_b, "b_b_rg": db_rg, "b_b_ig": db_ig,
               "b_lam": dlam}
    gsends_b = [dw_out_a.reshape(N_DEV, -1, D_MODEL), dw_in_b, lru_g(dw_rg).astype(BF16), lru_g(dw_ig).astype(BF16),
                dw_out_b.reshape(N_DEV, -1, D_MODEL),
                jnp.concatenate([_slotted(to_cols(small_b[nm], LANES).astype(BF16), 1) for nm, _ in SMALL_B], axis=1)]

    dqc, dkc, dv, *lands_b = _attn_bwd(qc, kc, v, lse, delta, do, gsends_b)
    g_out_a, g_in_b, g_rg, g_ig, g_out_b, gsum_small_b = _sum_blocks(lands_b, "sum_blocks_b")
    dlat, dw_uq, dw_ukv, dg_q, dg_kv = _mla_qkv_bwd(lat, g_q, g_kv, w_uq, w_ukv, cos, sin, dqc, dkc, dv, scale)
    dh0, dw_in_a, dg_a, *lands_a1 = _norm_proj_bwd(h0, g_a, w_in_a, dlat, dgate_a, dh1, "a_in_bwd", [dw_uq, dw_ukv])
    g_uq, g_ukv = _sum_blocks(lands_a1, "sum_blocks_a1")

    grad_x = dh0[N_META:n_real][None]

    dw_in_a_nat = jnp.concatenate([dw_in_a[0, :, :LAT + QK_ROPE], dw_in_a[0, :, LAT_PAD:]], axis=1)
    in_lanes = lambda g, cdev: jnp.pad(g.reshape(g.shape[0], N_DEV, cdev).transpose(1, 0, 2),
                                       ((0, 0), (0, 0), (0, wide - cdev)))
    pieces = [in_lanes(dw_in_a_nat, cdev_a), in_lanes(dh0[:N_META].astype(BF16), LANES)]
    used = sum(p.shape[1] for p in pieces)
    pieces.append(jnp.zeros((N_DEV, -used % GRAD_CHUNK, wide), BF16))
    gsend_a0 = jnp.concatenate(pieces, axis=1)
    rep_parts = {"a_norm_g": dg_a, "a_q_norm_g": dg_q, "a_kv_norm_g": dg_kv, "final_norm_g": dg_f,
                 "loss": jnp.broadcast_to(loss_part, (1, LANES))}
    rep = jnp.concatenate([_slotted(rep_parts[nm].reshape(-1, LANES), 0) for nm, _ in REP], axis=0)
    gsum_a0, rep_sum = _grad_exchange(gsend_a0, rep)

    grads = {"a_w_out": g_out_a, "b_w_in": g_in_b, "b_w_rg": g_rg, "b_w_ig": g_ig, "b_w_out": g_out_b,
             "a_w_uq": g_uq[:, :QK_NOPE + QK_ROPE], "a_w_ukv": g_ukv}
    grads = {nm: g.reshape(w[nm].shape) for nm, g in grads.items()}
    grads["a_w_in"] = gsum_a0[:D_MODEL, :cdev_a].reshape(w["a_w_in"].shape)
    grads["meta_tokens"] = gsum_a0[D_MODEL:D_MODEL + N_META, :LANES]
    for off, src in ((gsmallb_off, gsum_small_b), (rep_off, rep_sum)):
        for nm, (o_r, n) in off.items():
            if nm in w:
                grads[nm] = src[o_r:o_r + n].reshape(w[nm].shape)
    loss = rep_sum[rep_off["loss"][0], 0]

    as2d = lambda a: a.reshape(-1, a.shape[-1])
    deltas, new_ms, new_vs = _adamw_all([as2d(w[nm]) for nm in names], [as2d(grads[nm]) for nm in names],
                                        [as2d(mom_m[nm]) for nm in names], [as2d(mom_v[nm]) for nm in names])
    shaped = lambda arrs: [a.reshape(w[nm].shape) for a, nm in zip(arrs, names)]
    return (loss, grad_x, *[grads[nm] for nm in names], *shaped(deltas), *shaped(new_ms), *shaped(new_vs))
```
